```python
import math
import jax, jax.numpy as jnp
from jax import lax
import numpy as np

D_MODEL = 2048
BATCH = 8
SEQ = 2048
DEPTH = 1

HEAD_DIM = 64
LRU_WIDTH = D_MODEL // 2
LRU_HEADS = LRU_WIDTH // HEAD_DIM
LRU_HEAD_DIM = HEAD_DIM
ATTN_WIDTH = D_MODEL - LRU_WIDTH
ATTN_HEADS = ATTN_WIDTH // HEAD_DIM
ATTN_HEAD_DIM = HEAD_DIM
MIX_WIDTH = LRU_WIDTH + ATTN_WIDTH
IN_PROJ_WIDTH = 2 * LRU_WIDTH + 3 * ATTN_WIDTH
CONV_WIDTH = 4
RG_C = 8.0
GRID_W = 64
WIN_ROWS = 8
WIN_COLS = 16
D_FF = 256 * math.ceil(8 * D_MODEL / 3 / 256)
NORM_EPS = 1e-6

kernel_name = "hybrid_rglru_natten_macaron"


def rms_norm(x, g):
    xf = x.astype(jnp.float32)
    y = xf * lax.rsqrt(jnp.mean(xf * xf, axis=-1, keepdims=True) + NORM_EPS) * g.astype(jnp.float32)
    return y.astype(x.dtype)


def swiglu(u, w_in, w_out):
    gate, up = jnp.split(u @ w_in, 2, axis=-1)
    return (jax.nn.silu(gate) * up) @ w_out


def _lin_rec_combine(left, right):
    a_l, b_l = left
    a_r, b_r = right
    return a_l * a_r, a_r * b_l + b_r


def rglru_mixer(x_in, gate_in, conv_w, conv_b, gate_w, gate_b, lam):
    B, S, C = x_in.shape
    pad_l = CONV_WIDTH // 2
    xc = lax.conv_general_dilated(
        x_in, conv_w[:, None, :], window_strides=(1,),
        padding=[(pad_l, CONV_WIDTH - 1 - pad_l)],
        dimension_numbers=("NWC", "WIO", "NWC"), feature_group_count=C) + conv_b
    xf = xc.astype(jnp.float32)
    xh = xf.reshape(B, S, LRU_HEADS, LRU_HEAD_DIM)
    gates = jax.nn.sigmoid(
        jnp.einsum("bshi,zghij->zgbshj", xh, gate_w.astype(jnp.float32))
        + gate_b.astype(jnp.float32)[:, :, None, None])
    gates = gates.reshape(2, 2, B, S, C)
    r, i = gates[:, 0], gates[:, 1]
    log_a = -RG_C * r * jax.nn.softplus(-lam.astype(jnp.float32))[:, None, None, :]
    a = jnp.exp(log_a)
    b = jnp.sqrt(-jnp.expm1(2.0 * log_a)) * (i * xf)
    h_fwd = lax.associative_scan(_lin_rec_combine, (a[0], b[0]), axis=1)[1]
    h_bwd = lax.associative_scan(_lin_rec_combine, (a[1], b[1]), axis=1, reverse=True)[1]
    y = jax.nn.gelu(gate_in.astype(jnp.float32)) * (h_fwd + h_bwd)
    return y.astype(x_in.dtype)


def neighbourhood_attention(q, k, v, rpb):
    B, S, _ = q.shape
    rows = S // GRID_W
    kr = min(WIN_ROWS, rows)
    grid = (B, rows, GRID_W, ATTN_HEADS, ATTN_HEAD_DIM)
    scale = ATTN_HEAD_DIM ** -0.5
    qg = q.reshape(grid).astype(jnp.float32) * scale
    kg = k.reshape(grid).astype(jnp.float32)
    vg = v.reshape(grid).astype(jnp.float32)
    cols = jnp.arange(GRID_W)
    col_start = jnp.clip(cols - WIN_COLS // 2, 0, GRID_W - WIN_COLS)
    col_idx = col_start[:, None] + jnp.arange(WIN_COLS)
    sel = jax.nn.one_hot(col_idx, GRID_W, dtype=jnp.float32)
    col_bias = rpb.astype(jnp.float32)[:, :, col_idx - cols[:, None] + WIN_COLS - 1]

    def row_block(r):
        rs = jnp.clip(r - kr // 2, 0, rows - kr)
        q_row = lax.dynamic_index_in_dim(qg, r, axis=1, keepdims=False)
        k_band = lax.dynamic_slice_in_dim(kg, rs, kr, axis=1)
        v_band = lax.dynamic_slice_in_dim(vg, rs, kr, axis=1)
        s_band = jnp.einsum("bqhd,brkhd->bhqrk", q_row, k_band)
        s = jnp.einsum("bhqrk,qwk->bhqrw", s_band, sel)
        row_off = rs + jnp.arange(kr) - r + WIN_ROWS - 1
        bias = jnp.take(col_bias, row_off, axis=1).transpose(0, 2, 1, 3)
        s = s + bias[None]
        p = jax.nn.softmax(s.reshape(B, ATTN_HEADS, GRID_W, kr * WIN_COLS), axis=-1)
        p = p.reshape(B, ATTN_HEADS, GRID_W, kr, WIN_COLS)
        p_band = jnp.einsum("bhqrw,qwk->bhqrk", p, sel)
        return jnp.einsum("bhqrk,brkhd->bqhd", p_band, v_band)

    out = lax.map(row_block, jnp.arange(rows))
    return out.transpose(1, 0, 2, 3, 4).reshape(B, S, ATTN_WIDTH).astype(q.dtype)


def _fwd_setup_inputs(seed: int = 0) -> dict:
    key = jax.random.key(seed)
    ks = jax.random.split(key, 24)
    f32 = jnp.float32
    L = DEPTH

    def nrm(k, shape, scale):
        return jax.random.normal(k, shape, f32) * scale

    def gain(k, shape):
        return 1.0 + 0.05 * jax.random.normal(k, shape, f32)

    a_c = jax.random.uniform(ks[10], (L, 2, LRU_WIDTH), f32, 0.9, 0.999)
    a_base = a_c ** (1.0 / RG_C)
    lru_lambda = jnp.log(a_base) - jnp.log1p(-a_base)
    return {
        "x": nrm(ks[0], (BATCH, SEQ, D_MODEL), 1.0),
        "norm_ffn1": gain(ks[1], (L, D_MODEL)),
        "ffn1_w_in": nrm(ks[2], (L, D_MODEL, 2 * D_FF), D_MODEL ** -0.5),
        "ffn1_w_out": nrm(ks[3], (L, D_FF, D_MODEL), D_FF ** -0.5),
        "norm_mix": gain(ks[4], (L, D_MODEL)),
        "w_in_mix": nrm(ks[5], (L, D_MODEL, IN_PROJ_WIDTH), D_MODEL ** -0.5),
        "lru_conv_w": nrm(ks[6], (L, CONV_WIDTH, LRU_WIDTH), CONV_WIDTH ** -0.5),
        "lru_conv_b": nrm(ks[7], (L, LRU_WIDTH), 0.01),
        "lru_gate_w": nrm(ks[8], (L, 2, 2, LRU_HEADS, LRU_HEAD_DIM, LRU_HEAD_DIM), LRU_HEAD_DIM ** -0.5),
        "lru_gate_b": nrm(ks[9], (L, 2, 2, LRU_HEADS, LRU_HEAD_DIM), 0.01),
        "lru_lambda": lru_lambda,
        "attn_rpb": nrm(ks[11], (L, ATTN_HEADS, 2 * WIN_ROWS - 1, 2 * WIN_COLS - 1), 0.02),
        "lru_out_norm": gain(ks[12], (L, LRU_WIDTH)),
        "attn_out_norm": gain(ks[13], (L, ATTN_WIDTH)),
        "w_out_mix": nrm(ks[14], (L, MIX_WIDTH, D_MODEL), MIX_WIDTH ** -0.5),
        "norm_ffn2": gain(ks[15], (L, D_MODEL)),
        "ffn2_w_in": nrm(ks[16], (L, D_MODEL, 2 * D_FF), D_MODEL ** -0.5),
        "ffn2_w_out": nrm(ks[17], (L, D_FF, D_MODEL), D_FF ** -0.5),
        "norm_final": gain(ks[18], (D_MODEL,)),
    }


def _fwd_reference(x, norm_ffn1, ffn1_w_in, ffn1_w_out, norm_mix, w_in_mix, lru_conv_w, lru_conv_b,
              lru_gate_w, lru_gate_b, lru_lambda, attn_rpb, lru_out_norm, attn_out_norm,
              w_out_mix, norm_ffn2, ffn2_w_in, ffn2_w_out, norm_final):
    h = x
    for l in range(DEPTH):
        h = h + 0.5 * swiglu(rms_norm(h, norm_ffn1[l]), ffn1_w_in[l], ffn1_w_out[l])
        u = rms_norm(h, norm_mix[l])
        proj = u @ w_in_mix[l]
        x_lru, g_lru, q, k, v = jnp.split(
            proj, [LRU_WIDTH, 2 * LRU_WIDTH, 2 * LRU_WIDTH + ATTN_WIDTH,
                   2 * LRU_WIDTH + 2 * ATTN_WIDTH], axis=-1)
        y_a = rglru_mixer(x_lru, g_lru, lru_conv_w[l], lru_conv_b[l], lru_gate_w[l],
                          lru_gate_b[l], lru_lambda[l])
        y_b = neighbourhood_attention(q, k, v, attn_rpb[l])
        y = jnp.concatenate([rms_norm(y_a, lru_out_norm[l]), rms_norm(y_b, attn_out_norm[l])], axis=-1)
        h = h + y @ w_out_mix[l]
        h = h + 0.5 * swiglu(rms_norm(h, norm_ffn2[l]), ffn2_w_in[l], ffn2_w_out[l])
    return rms_norm(h, norm_final)


import jax as _jax
import jax.numpy as _jnp

TWIN_FORMAT = 'train_step'
FWD_PARAMS = ['x', 'norm_ffn1', 'ffn1_w_in', 'ffn1_w_out', 'norm_mix', 'w_in_mix', 'lru_conv_w', 'lru_conv_b', 'lru_gate_w', 'lru_gate_b', 'lru_lambda', 'attn_rpb', 'lru_out_norm', 'attn_out_norm', 'w_out_mix', 'norm_ffn2', 'ffn2_w_in', 'ffn2_w_out', 'norm_final']
TWIN_WEIGHTS = ['norm_ffn1', 'ffn1_w_in', 'ffn1_w_out', 'norm_mix', 'w_in_mix', 'lru_conv_w', 'lru_conv_b', 'lru_gate_w', 'lru_gate_b', 'lru_lambda', 'attn_rpb', 'lru_out_norm', 'attn_out_norm', 'w_out_mix', 'norm_ffn2', 'ffn2_w_in', 'ffn2_w_out', 'norm_final']
TWIN_DIFF_INPUT = 'x'
TWIN_INPUTS = ['x', 'norm_ffn1', 'ffn1_w_in', 'ffn1_w_out', 'norm_mix', 'w_in_mix', 'lru_conv_w', 'lru_conv_b', 'lru_gate_w', 'lru_gate_b', 'lru_lambda', 'attn_rpb', 'lru_out_norm', 'attn_out_norm', 'w_out_mix', 'norm_ffn2', 'ffn2_w_in', 'ffn2_w_out', 'norm_final', 'loss_target', 'm_norm_ffn1', 'm_ffn1_w_in', 'm_ffn1_w_out', 'm_norm_mix', 'm_w_in_mix', 'm_lru_conv_w', 'm_lru_conv_b', 'm_lru_gate_w', 'm_lru_gate_b', 'm_lru_lambda', 'm_attn_rpb', 'm_lru_out_norm', 'm_attn_out_norm', 'm_w_out_mix', 'm_norm_ffn2', 'm_ffn2_w_in', 'm_ffn2_w_out', 'm_norm_final', 'v_norm_ffn1', 'v_ffn1_w_in', 'v_ffn1_w_out', 'v_norm_mix', 'v_w_in_mix', 'v_lru_conv_w', 'v_lru_conv_b', 'v_lru_gate_w', 'v_lru_gate_b', 'v_lru_lambda', 'v_attn_rpb', 'v_lru_out_norm', 'v_attn_out_norm', 'v_w_out_mix', 'v_norm_ffn2', 'v_ffn2_w_in', 'v_ffn2_w_out', 'v_norm_final']
TWIN_OUTPUTS = ['loss', 'grad_x', 'grad_norm_ffn1', 'grad_ffn1_w_in', 'grad_ffn1_w_out', 'grad_norm_mix', 'grad_w_in_mix', 'grad_lru_conv_w', 'grad_lru_conv_b', 'grad_lru_gate_w', 'grad_lru_gate_b', 'grad_lru_lambda', 'grad_attn_rpb', 'grad_lru_out_norm', 'grad_attn_out_norm', 'grad_w_out_mix', 'grad_norm_ffn2', 'grad_ffn2_w_in', 'grad_ffn2_w_out', 'grad_norm_final', 'delta_norm_ffn1', 'delta_ffn1_w_in', 'delta_ffn1_w_out', 'delta_norm_mix', 'delta_w_in_mix', 'delta_lru_conv_w', 'delta_lru_conv_b', 'delta_lru_gate_w', 'delta_lru_gate_b', 'delta_lru_lambda', 'delta_attn_rpb', 'delta_lru_out_norm', 'delta_attn_out_norm', 'delta_w_out_mix', 'delta_norm_ffn2', 'delta_ffn2_w_in', 'delta_ffn2_w_out', 'delta_norm_final', 'new_m_norm_ffn1', 'new_m_ffn1_w_in', 'new_m_ffn1_w_out', 'new_m_norm_mix', 'new_m_w_in_mix', 'new_m_lru_conv_w', 'new_m_lru_conv_b', 'new_m_lru_gate_w', 'new_m_lru_gate_b', 'new_m_lru_lambda', 'new_m_attn_rpb', 'new_m_lru_out_norm', 'new_m_attn_out_norm', 'new_m_w_out_mix', 'new_m_norm_ffn2', 'new_m_ffn2_w_in', 'new_m_ffn2_w_out', 'new_m_norm_final', 'new_v_norm_ffn1', 'new_v_ffn1_w_in', 'new_v_ffn1_w_out', 'new_v_norm_mix', 'new_v_w_in_mix', 'new_v_lru_conv_w', 'new_v_lru_conv_b', 'new_v_lru_gate_w', 'new_v_lru_gate_b', 'new_v_lru_lambda', 'new_v_attn_rpb', 'new_v_lru_out_norm', 'new_v_attn_out_norm', 'new_v_w_out_mix', 'new_v_norm_ffn2', 'new_v_ffn2_w_in', 'new_v_ffn2_w_out', 'new_v_norm_final']
TWIN_LEAF_KINDS = {'loss': 'loss', 'grad_x': 'grad_x', 'grad_norm_ffn1': 'grad_w', 'grad_ffn1_w_in': 'grad_w', 'grad_ffn1_w_out': 'grad_w', 'grad_norm_mix': 'grad_w', 'grad_w_in_mix': 'grad_w', 'grad_lru_conv_w': 'grad_w', 'grad_lru_conv_b': 'grad_w', 'grad_lru_gate_w': 'grad_w', 'grad_lru_gate_b': 'grad_w', 'grad_lru_lambda': 'grad_w', 'grad_attn_rpb': 'grad_w', 'grad_lru_out_norm': 'grad_w', 'grad_attn_out_norm': 'grad_w', 'grad_w_out_mix': 'grad_w', 'grad_norm_ffn2': 'grad_w', 'grad_ffn2_w_in': 'grad_w', 'grad_ffn2_w_out': 'grad_w', 'grad_norm_final': 'grad_w', 'delta_norm_ffn1': 'delta_w', 'delta_ffn1_w_in': 'delta_w', 'delta_ffn1_w_out': 'delta_w', 'delta_norm_mix': 'delta_w', 'delta_w_in_mix': 'delta_w', 'delta_lru_conv_w': 'delta_w', 'delta_lru_conv_b': 'delta_w', 'delta_lru_gate_w': 'delta_w', 'delta_lru_gate_b': 'delta_w', 'delta_lru_lambda': 'delta_w', 'delta_attn_rpb': 'delta_w', 'delta_lru_out_norm': 'delta_w', 'delta_attn_out_norm': 'delta_w', 'delta_w_out_mix': 'delta_w', 'delta_norm_ffn2': 'delta_w', 'delta_ffn2_w_in': 'delta_w', 'delta_ffn2_w_out': 'delta_w', 'delta_norm_final': 'delta_w', 'new_m_norm_ffn1': 'new_m', 'new_m_ffn1_w_in': 'new_m', 'new_m_ffn1_w_out': 'new_m', 'new_m_norm_mix': 'new_m', 'new_m_w_in_mix': 'new_m', 'new_m_lru_conv_w': 'new_m', 'new_m_lru_conv_b': 'new_m', 'new_m_lru_gate_w': 'new_m', 'new_m_lru_gate_b': 'new_m', 'new_m_lru_lambda': 'new_m', 'new_m_attn_rpb': 'new_m', 'new_m_lru_out_norm': 'new_m', 'new_m_attn_out_norm': 'new_m', 'new_m_w_out_mix': 'new_m', 'new_m_norm_ffn2': 'new_m', 'new_m_ffn2_w_in': 'new_m', 'new_m_ffn2_w_out': 'new_m', 'new_m_norm_final': 'new_m', 'new_v_norm_ffn1': 'new_v', 'new_v_ffn1_w_in': 'new_v', 'new_v_ffn1_w_out': 'new_v', 'new_v_norm_mix': 'new_v', 'new_v_w_in_mix': 'new_v', 'new_v_lru_conv_w': 'new_v', 'new_v_lru_conv_b': 'new_v', 'new_v_lru_gate_w': 'new_v', 'new_v_lru_gate_b': 'new_v', 'new_v_lru_lambda': 'new_v', 'new_v_attn_rpb': 'new_v', 'new_v_lru_out_norm': 'new_v', 'new_v_attn_out_norm': 'new_v', 'new_v_w_out_mix': 'new_v', 'new_v_norm_ffn2': 'new_v', 'new_v_ffn2_w_in': 'new_v', 'new_v_ffn2_w_out': 'new_v', 'new_v_norm_final': 'new_v'}


def _forward(args):
    return _fwd_reference(*[args[k] for k in FWD_PARAMS])


def _output_shape():
    out = _jax.eval_shape(lambda: _forward(_fwd_setup_inputs(0)))
    return out.shape, out.dtype

N_MICROBATCH = 1
ADAM_LR = 0.001
ADAM_B1 = 0.9
ADAM_B2 = 0.999
ADAM_EPS = 1e-08
ADAM_WD = 0.01
ADAM_STEP = 10
PER_EXAMPLE_BATCH_AXIS = {'x': 0, 'loss_target': 0}
SHARED_INPUTS = []
_WEIGHT_DTYPES = {'norm_ffn1': _jnp.float32, 'ffn1_w_in': _jnp.float32, 'ffn1_w_out': _jnp.float32, 'norm_mix': _jnp.float32, 'w_in_mix': _jnp.float32, 'lru_conv_w': _jnp.float32, 'lru_conv_b': _jnp.float32, 'lru_gate_w': _jnp.float32, 'lru_gate_b': _jnp.float32, 'lru_lambda': _jnp.float32, 'attn_rpb': _jnp.float32, 'lru_out_norm': _jnp.float32, 'attn_out_norm': _jnp.float32, 'w_out_mix': _jnp.float32, 'norm_ffn2': _jnp.float32, 'ffn2_w_in': _jnp.float32, 'ffn2_w_out': _jnp.float32, 'norm_final': _jnp.float32}
MOMENT_SCALE = {'norm_ffn1': 3.535111e-02, 'ffn1_w_in': 1.470808e-02, 'ffn1_w_out': 2.405264e-02, 'norm_mix': 7.005150e-02, 'w_in_mix': 4.390657e-02, 'lru_conv_w': 4.752534e-02, 'lru_conv_b': 6.286311e-01, 'lru_gate_w': 1.831697e-02, 'lru_gate_b': 1.008141e-02, 'lru_lambda': 1.608654e-02, 'attn_rpb': 1.563542e-02, 'lru_out_norm': 4.356122e-02, 'attn_out_norm': 4.590477e-02, 'w_out_mix': 4.660773e-02, 'norm_ffn2': 1.928238e-02, 'ffn2_w_in': 7.819024e-03, 'ffn2_w_out': 1.278455e-02, 'norm_final': 8.027217e+00}


def _to_microbatches(a, axis):
    t = _jnp.moveaxis(a, axis, 0)
    t = t.reshape((N_MICROBATCH, t.shape[0] // N_MICROBATCH) + t.shape[1:])
    return _jnp.moveaxis(t, 1, axis + 1)


def setup_inputs(seed: int = 0) -> dict:
    inp = _fwd_setup_inputs(seed)
    key = _jax.random.fold_in(_jax.random.key(seed), 7919)
    shape, _ = _output_shape()
    out = dict(inp)
    out["loss_target"] = _jax.random.normal(_jax.random.fold_in(key, 0), shape, _jnp.float32)
    for i, name in enumerate(TWIN_WEIGHTS):
        w = inp[name].astype(_jnp.float32)
        if MOMENT_SCALE is None:
            s = _jnp.sqrt(_jnp.mean(_jnp.square(w)) + 1e-30)
        else:
            s = MOMENT_SCALE[name]
        km, kv = _jax.random.split(_jax.random.fold_in(key, i + 1))
        out[name] = w
        out["m_" + name] = s * _jax.random.normal(km, w.shape, _jnp.float32)
        out["v_" + name] = (s * s) * _jax.random.uniform(kv, w.shape, _jnp.float32, 0.5, 1.5)
    if N_MICROBATCH > 1:
        for name, axis in PER_EXAMPLE_BATCH_AXIS.items():
            out[name] = _to_microbatches(out[name], axis)
    return {'x': out['x'], 'norm_ffn1': out['norm_ffn1'], 'ffn1_w_in': out['ffn1_w_in'], 'ffn1_w_out': out['ffn1_w_out'], 'norm_mix': out['norm_mix'], 'w_in_mix': out['w_in_mix'], 'lru_conv_w': out['lru_conv_w'], 'lru_conv_b': out['lru_conv_b'], 'lru_gate_w': out['lru_gate_w'], 'lru_gate_b': out['lru_gate_b'], 'lru_lambda': out['lru_lambda'], 'attn_rpb': out['attn_rpb'], 'lru_out_norm': out['lru_out_norm'], 'attn_out_norm': out['attn_out_norm'], 'w_out_mix': out['w_out_mix'], 'norm_ffn2': out['norm_ffn2'], 'ffn2_w_in': out['ffn2_w_in'], 'ffn2_w_out': out['ffn2_w_out'], 'norm_final': out['norm_final'], 'loss_target': out['loss_target'], 'm_norm_ffn1': out['m_norm_ffn1'], 'm_ffn1_w_in': out['m_ffn1_w_in'], 'm_ffn1_w_out': out['m_ffn1_w_out'], 'm_norm_mix': out['m_norm_mix'], 'm_w_in_mix': out['m_w_in_mix'], 'm_lru_conv_w': out['m_lru_conv_w'], 'm_lru_conv_b': out['m_lru_conv_b'], 'm_lru_gate_w': out['m_lru_gate_w'], 'm_lru_gate_b': out['m_lru_gate_b'], 'm_lru_lambda': out['m_lru_lambda'], 'm_attn_rpb': out['m_attn_rpb'], 'm_lru_out_norm': out['m_lru_out_norm'], 'm_attn_out_norm': out['m_attn_out_norm'], 'm_w_out_mix': out['m_w_out_mix'], 'm_norm_ffn2': out['m_norm_ffn2'], 'm_ffn2_w_in': out['m_ffn2_w_in'], 'm_ffn2_w_out': out['m_ffn2_w_out'], 'm_norm_final': out['m_norm_final'], 'v_norm_ffn1': out['v_norm_ffn1'], 'v_ffn1_w_in': out['v_ffn1_w_in'], 'v_ffn1_w_out': out['v_ffn1_w_out'], 'v_norm_mix': out['v_norm_mix'], 'v_w_in_mix': out['v_w_in_mix'], 'v_lru_conv_w': out['v_lru_conv_w'], 'v_lru_conv_b': out['v_lru_conv_b'], 'v_lru_gate_w': out['v_lru_gate_w'], 'v_lru_gate_b': out['v_lru_gate_b'], 'v_lru_lambda': out['v_lru_lambda'], 'v_attn_rpb': out['v_attn_rpb'], 'v_lru_out_norm': out['v_lru_out_norm'], 'v_attn_out_norm': out['v_attn_out_norm'], 'v_w_out_mix': out['v_w_out_mix'], 'v_norm_ffn2': out['v_norm_ffn2'], 'v_ffn2_w_in': out['v_ffn2_w_in'], 'v_ffn2_w_out': out['v_ffn2_w_out'], 'v_norm_final': out['v_norm_final']}


def _loss(weights, diff, rest, loss_target):
    with _jax.named_scope("forward"):
        args = {**rest, TWIN_DIFF_INPUT: diff, **{k: w.astype(_WEIGHT_DTYPES[k]) for k, w in weights.items()}}
        y = _forward(args)
    with _jax.named_scope("loss_head"):
        err = _jnp.square(y.astype(_jnp.float32) - loss_target)
        return 0.5 * _jnp.sum(_jnp.mean(err, axis=-1)) if err.ndim else 0.5 * err


def _adamw(w, g, m, v):
    m = ADAM_B1 * m + (1.0 - ADAM_B1) * g
    v = ADAM_B2 * v + (1.0 - ADAM_B2) * _jnp.square(g)
    m_hat = m / (1.0 - ADAM_B1 ** ADAM_STEP)
    v_hat = v / (1.0 - ADAM_B2 ** ADAM_STEP)
    delta = -ADAM_LR * (m_hat / (_jnp.sqrt(v_hat) + ADAM_EPS) + ADAM_WD * w)
    return delta, m, v


def reference(x, norm_ffn1, ffn1_w_in, ffn1_w_out, norm_mix, w_in_mix, lru_conv_w, lru_conv_b, lru_gate_w, lru_gate_b, lru_lambda, attn_rpb, lru_out_norm, attn_out_norm, w_out_mix, norm_ffn2, ffn2_w_in, ffn2_w_out, norm_final, loss_target, m_norm_ffn1, m_ffn1_w_in, m_ffn1_w_out, m_norm_mix, m_w_in_mix, m_lru_conv_w, m_lru_conv_b, m_lru_gate_w, m_lru_gate_b, m_lru_lambda, m_attn_rpb, m_lru_out_norm, m_attn_out_norm, m_w_out_mix, m_norm_ffn2, m_ffn2_w_in, m_ffn2_w_out, m_norm_final, v_norm_ffn1, v_ffn1_w_in, v_ffn1_w_out, v_norm_mix, v_w_in_mix, v_lru_conv_w, v_lru_conv_b, v_lru_gate_w, v_lru_gate_b, v_lru_lambda, v_attn_rpb, v_lru_out_norm, v_attn_out_norm, v_w_out_mix, v_norm_ffn2, v_ffn2_w_in, v_ffn2_w_out, v_norm_final):
    given = dict(x=x, norm_ffn1=norm_ffn1, ffn1_w_in=ffn1_w_in, ffn1_w_out=ffn1_w_out, norm_mix=norm_mix, w_in_mix=w_in_mix, lru_conv_w=lru_conv_w, lru_conv_b=lru_conv_b, lru_gate_w=lru_gate_w, lru_gate_b=lru_gate_b, lru_lambda=lru_lambda, attn_rpb=attn_rpb, lru_out_norm=lru_out_norm, attn_out_norm=attn_out_norm, w_out_mix=w_out_mix, norm_ffn2=norm_ffn2, ffn2_w_in=ffn2_w_in, ffn2_w_out=ffn2_w_out, norm_final=norm_final, loss_target=loss_target, m_norm_ffn1=m_norm_ffn1, m_ffn1_w_in=m_ffn1_w_in, m_ffn1_w_out=m_ffn1_w_out, m_norm_mix=m_norm_mix, m_w_in_mix=m_w_in_mix, m_lru_conv_w=m_lru_conv_w, m_lru_conv_b=m_lru_conv_b, m_lru_gate_w=m_lru_gate_w, m_lru_gate_b=m_lru_gate_b, m_lru_lambda=m_lru_lambda, m_attn_rpb=m_attn_rpb, m_lru_out_norm=m_lru_out_norm, m_attn_out_norm=m_attn_out_norm, m_w_out_mix=m_w_out_mix, m_norm_ffn2=m_norm_ffn2, m_ffn2_w_in=m_ffn2_w_in, m_ffn2_w_out=m_ffn2_w_out, m_norm_final=m_norm_final, v_norm_ffn1=v_norm_ffn1, v_ffn1_w_in=v_ffn1_w_in, v_ffn1_w_out=v_ffn1_w_out, v_norm_mix=v_norm_mix, v_w_in_mix=v_w_in_mix, v_lru_conv_w=v_lru_conv_w, v_lru_conv_b=v_lru_conv_b, v_lru_gate_w=v_lru_gate_w, v_lru_gate_b=v_lru_gate_b, v_lru_lambda=v_lru_lambda, v_attn_rpb=v_attn_rpb, v_lru_out_norm=v_lru_out_norm, v_attn_out_norm=v_attn_out_norm, v_w_out_mix=v_w_out_mix, v_norm_ffn2=v_norm_ffn2, v_ffn2_w_in=v_ffn2_w_in, v_ffn2_w_out=v_ffn2_w_out, v_norm_final=v_norm_final)
    weights = {n: given[n] for n in TWIN_WEIGHTS}
    shared = {n: given[n] for n in SHARED_INPUTS}
    per_example = {n: given[n] for n in ['x']}
    grad_fn = _jax.value_and_grad(_loss, argnums=(0, 1))

    def one_microbatch(ex, loss_target):
        ex = dict(ex)
        diff = ex.pop(TWIN_DIFF_INPUT)
        return grad_fn(weights, diff, {**shared, **ex}, loss_target)

    if N_MICROBATCH == 1:
        loss, (grad_w, grad_x) = one_microbatch(per_example, given["loss_target"])
    else:
        def body(carry, xs):
            loss_sum, grad_sum = carry
            l_k, (gw_k, gx_k) = one_microbatch(xs[0], xs[1])
            with _jax.named_scope("update"):
                return (loss_sum + l_k, _jax.tree.map(_jnp.add, grad_sum, gw_k)), gx_k

        init = (_jnp.zeros((), _jnp.float32), _jax.tree.map(_jnp.zeros_like, weights))
        (loss, grad_w), grad_x = _jax.lax.scan(body, init, (per_example, given["loss_target"]))
    with _jax.named_scope("update"):
        delta_w, new_m, new_v = {}, {}, {}
        for n in TWIN_WEIGHTS:
            delta_w[n], new_m[n], new_v[n] = _adamw(weights[n], grad_w[n], given["m_" + n], given["v_" + n])
    return (loss, grad_x, *[grad_w[n] for n in TWIN_WEIGHTS], *[delta_w[n] for n in TWIN_WEIGHTS],
            *[new_m[n] for n in TWIN_WEIGHTS], *[new_v[n] for n in TWIN_WEIGHTS])
```

```python
import functools
import math

import numpy as np
import jax
import jax.numpy as jnp
from jax import lax
from jax.experimental import pallas as pl
from jax.experimental.pallas import tpu as pltpu

F32 = jnp.float32
BF16 = jnp.bfloat16
SDS = jax.ShapeDtypeStruct

N_DEV = 8
NORM_EPS = 1e-6
RG_C = 8.0
CONV_WIDTH = 4
HEAD_DIM = 64
GRID_W = 64
WIN_ROWS = 8
WIN_COLS = 16
BAND = WIN_ROWS * GRID_W
NEG = -1e30

ADAM_LR = 0.001
ADAM_B1 = 0.9
ADAM_B2 = 0.999
ADAM_EPS = 1e-08
ADAM_WD = 0.01
ADAM_STEP = 10

LANES = 128
SUBLANES = 8
VMEM_LIMIT = 56 * 1024 * 1024

NT = (((1,), (1,)), ((), ()))
TN = (((0,), (0,)), ((), ()))
ANY = pl.BlockSpec(memory_space=pl.ANY)


def _params(n_axes):
    return pltpu.CompilerParams(dimension_semantics=("arbitrary",) * n_axes, vmem_limit_bytes=VMEM_LIMIT)


def _sigmoid(x):
    return 1.0 / (1.0 + jnp.exp(-x))


def _gelu_parts(x):
    c = math.sqrt(2.0 / math.pi)
    t = jnp.tanh(c * (x + 0.044715 * (x * x * x)))
    gelu = 0.5 * x * (1.0 + t)
    dgelu = 0.5 * (1.0 + t) + 0.5 * x * (1.0 - t * t) * (c * (1.0 + 3.0 * 0.044715 * (x * x)))
    return gelu, dgelu


def _expm1(x):
    poly = x * (1.0 + x * (1.0 / 2) * (1.0 + x * (1.0 / 3) * (1.0 + x * (1.0 / 4) * (1.0 + x * (1.0 / 5) * (1.0 + x * (1.0 / 6))))))
    return jnp.where(jnp.abs(x) < 0.25, poly, jnp.exp(x) - 1.0)


def _softplus(x):
    return jnp.maximum(x, 0.0) + jnp.log1p(jnp.exp(-jnp.abs(x)))


def _cast_rows(w, name):
    def body(w_ref, o_ref):
        o_ref[...] = w_ref[...].astype(BF16)

    return pl.pallas_call(body, out_shape=SDS(w.shape, BF16), name=name,
                          compiler_params=pltpu.CompilerParams(vmem_limit_bytes=VMEM_LIMIT))(w)


def _cast_transposed(w, name):
    d, n = w.shape
    td = 512

    def body(w_ref, o_ref):
        o_ref[...] = w_ref[...].T.astype(BF16)

    return pl.pallas_call(
        body, out_shape=SDS((n, d), BF16), grid=(d // td,),
        in_specs=[pl.BlockSpec((td, n), lambda i: (i, 0))],
        out_specs=pl.BlockSpec((n, td), lambda i: (0, i)),
        name=name, compiler_params=_params(1))(w)


def _position():
    return lax.axis_index("x"), lax.axis_index("y"), lax.axis_index("c")


def _flat(px, py, pc):
    return 4 * px + 2 * py + pc


def _all_gather(shards, name):
    n_arr = len(shards)

    def body(*refs):
        ins, outs = refs[:n_arr], refs[n_arr:2 * n_arr]
        send_sems, recv_sems, local_sems = refs[2 * n_arr:]
        x, y, c = _position()
        me, sibling = (x, y, c), (x, y, 1 - c)
        chips = [(1 - x, y), (x, 1 - y), (1 - x, 1 - y)]

        def rows(a, block):
            rb = shards[a].shape[0]
            return outs[a].at[pl.ds(_flat(*block) * rb, rb), :]

        def copy(a, k, block, to, src=None):
            return pltpu.make_async_remote_copy(
                src_ref=rows(a, block) if src is None else src, dst_ref=rows(a, block),
                send_sem=send_sems.at[a * 7 + k], recv_sem=recv_sems.at[a * 7 + k],
                device_id=to, device_id_type=pl.DeviceIdType.MESH)

        mine, first, passed = [], [], []
        for a in range(n_arr):
            cp = pltpu.make_async_copy(ins[a], rows(a, me), local_sems.at[a])
            cp.start()
            mine.append(cp)
            first.append(copy(a, 0, me, sibling, src=ins[a]))
            first += [copy(a, 1 + j, me, (*chip, c), src=ins[a]) for j, chip in enumerate(chips)]
        for cp in first:
            cp.start()
        for a in range(n_arr):
            for j, chip in enumerate(chips):
                copy(a, 1 + j, (*chip, c), me).wait_recv()
                fwd = copy(a, 4 + j, (*chip, c), sibling)
                fwd.start()
                passed.append(fwd)
        for a in range(n_arr):
            copy(a, 0, sibling, me).wait_recv()
            for j, chip in enumerate(chips):
                copy(a, 4 + j, (*chip, 1 - c), me).wait_recv()
        for cp in first + passed:
            cp.wait_send()
        for cp in mine:
            cp.wait()

    return pl.pallas_call(
        body, out_shape=tuple(SDS((N_DEV * s.shape[0], s.shape[1]), s.dtype) for s in shards),
        in_specs=[ANY] * n_arr, out_specs=tuple([ANY] * n_arr),
        scratch_shapes=[pltpu.SemaphoreType.DMA((7 * n_arr,)), pltpu.SemaphoreType.DMA((7 * n_arr,)),
                        pltpu.SemaphoreType.DMA((n_arr,))],
        name=name)(*shards)


PEER_FLIPS = [(0, 0, 1), (1, 0, 0), (0, 1, 0), (1, 1, 0), (1, 0, 1), (0, 1, 1), (1, 1, 1)]


def _flip(pos, flips):
    return tuple((1 - p) if f else p for p, f in zip(pos, flips))


def _scatter_blocks(grads, name):
    n_arr = len(grads)

    def body(*refs):
        ins, outs = refs[:n_arr], refs[n_arr:2 * n_arr]
        send_sems, recv_sems, local_sems = refs[2 * n_arr:]
        me = _position()
        my_flat = _flat(*me)

        def copy(a, k):
            rb = grads[a].shape[0] // N_DEV
            peer = _flip(me, PEER_FLIPS[k])
            return pltpu.make_async_remote_copy(
                src_ref=ins[a].at[pl.ds(_flat(*peer) * rb, rb), :],
                dst_ref=outs[a].at[pl.ds(my_flat * rb, rb), :],
                send_sem=send_sems.at[a * 7 + k], recv_sem=recv_sems.at[a * 7 + k],
                device_id=peer, device_id_type=pl.DeviceIdType.MESH)

        def landing(a, k):
            rb = grads[a].shape[0] // N_DEV
            peer = _flip(me, PEER_FLIPS[k])
            slot = outs[a].at[pl.ds(_flat(*peer) * rb, rb), :]
            return pltpu.make_async_remote_copy(
                src_ref=slot, dst_ref=slot, send_sem=send_sems.at[a * 7 + k], recv_sem=recv_sems.at[a * 7 + k],
                device_id=peer, device_id_type=pl.DeviceIdType.MESH)

        mine = []
        for a in range(n_arr):
            rb = grads[a].shape[0] // N_DEV
            own = pl.ds(my_flat * rb, rb)
            cp = pltpu.make_async_copy(ins[a].at[own, :], outs[a].at[own, :], local_sems.at[a])
            cp.start()
            mine.append(cp)
        sent = [copy(a, k) for k in range(7) for a in range(n_arr)]
        for cp in sent:
            cp.start()
        for k in range(7):
            for a in range(n_arr):
                landing(a, k).wait_recv()
        for cp in sent:
            cp.wait_send()
        for cp in mine:
            cp.wait()

    return pl.pallas_call(
        body, out_shape=tuple(SDS(g.shape, g.dtype) for g in grads),
        in_specs=[ANY] * n_arr, out_specs=tuple([ANY] * n_arr),
        scratch_shapes=[pltpu.SemaphoreType.DMA((7 * n_arr,)), pltpu.SemaphoreType.DMA((7 * n_arr,)),
                        pltpu.SemaphoreType.DMA((n_arr,))],
        name=name)(*grads)


def _all_reduce_small(pack, name):
    r = pack.shape[0]

    def body(p_ref, o_ref, buf, send_sems, recv_sems):
        me = _position()
        my_flat = _flat(*me)
        buf[my_flat] = p_ref[...]
        sent = []
        for k in range(7):
            peer = _flip(me, PEER_FLIPS[k])
            cp = pltpu.make_async_remote_copy(
                src_ref=p_ref, dst_ref=buf.at[my_flat], send_sem=send_sems.at[k], recv_sem=recv_sems.at[k],
                device_id=peer, device_id_type=pl.DeviceIdType.MESH)
            cp.start()
            sent.append(cp)
        for k in range(7):
            peer = _flip(me, PEER_FLIPS[k])
            slot = buf.at[_flat(*peer)]
            pltpu.make_async_remote_copy(
                src_ref=slot, dst_ref=slot, send_sem=send_sems.at[k], recv_sem=recv_sems.at[k],
                device_id=peer, device_id_type=pl.DeviceIdType.MESH).wait_recv()
        for cp in sent:
            cp.wait_send()
        acc = buf[0]
        for s in range(1, N_DEV):
            acc = acc + buf[s]
        o_ref[...] = acc

    vmem = pl.BlockSpec(memory_space=pltpu.VMEM)
    return pl.pallas_call(
        body, out_shape=SDS(pack.shape, F32), in_specs=[vmem], out_specs=vmem,
        scratch_shapes=[pltpu.VMEM((N_DEV, r, LANES), F32), pltpu.SemaphoreType.DMA((7,)), pltpu.SemaphoreType.DMA((7,))],
        compiler_params=pltpu.CompilerParams(vmem_limit_bytes=VMEM_LIMIT), name=name)(pack)


ROW_TILE = 256


def _rmsnorm_fwd(h, gain, name):
    t, d = h.shape

    def body(h_ref, g_ref, u_ref):
        x = h_ref[...]
        u_ref[...] = (x * lax.rsqrt(jnp.mean(x * x, axis=-1, keepdims=True) + NORM_EPS) * g_ref[...]).astype(BF16)

    row = pl.BlockSpec((ROW_TILE, d), lambda i: (i, 0))
    return pl.pallas_call(
        body, out_shape=SDS((t, d), BF16), grid=(t // ROW_TILE,),
        in_specs=[row, pl.BlockSpec((1, d), lambda i: (0, 0))], out_specs=row,
        name=name, compiler_params=_params(1))(h, gain)


def _rms_bwd_math(x, gain, dy):
    rstd = lax.rsqrt(jnp.mean(x * x, axis=-1, keepdims=True) + NORM_EPS)
    xhat = x * rstd
    dxh = dy * gain
    dx = rstd * (dxh - xhat * jnp.mean(dxh * xhat, axis=-1, keepdims=True))
    return dx, jnp.sum(dy * xhat, axis=0, keepdims=True)


def _rmsnorm_bwd(du, h, gain, resid, bf_scale, name):
    t, d = h.shape

    def body(du_ref, h_ref, g_ref, r_ref, dh_ref, dhb_ref, dg_ref):
        @pl.when(pl.program_id(0) == 0)
        def _():
            dg_ref[...] = jnp.zeros_like(dg_ref)

        dx, dg = _rms_bwd_math(h_ref[...], g_ref[...], du_ref[...])
        dh = r_ref[...] + dx
        dh_ref[...] = dh
        dhb_ref[...] = (bf_scale * dh).astype(BF16)
        dg_ref[...] += dg

    row = pl.BlockSpec((ROW_TILE, d), lambda i: (i, 0))
    vec = pl.BlockSpec((1, d), lambda i: (0, 0))
    return pl.pallas_call(
        body, out_shape=(SDS((t, d), F32), SDS((t, d), BF16), SDS((1, d), F32)), grid=(t // ROW_TILE,),
        in_specs=[row, row, vec, row], out_specs=(row, row, vec),
        name=name, compiler_params=_params(1))(du, h, gain, resid)


def _final_loss(h, gain, target, name):
    t, d = h.shape

    def body(h_ref, g_ref, t_ref, dh_ref, dhb_ref, loss_ref, dg_ref):
        @pl.when(pl.program_id(0) == 0)
        def _():
            dg_ref[...] = jnp.zeros_like(dg_ref)
            loss_ref[...] = jnp.zeros_like(loss_ref)

        x = h_ref[...]
        gain = g_ref[...]
        out = x * lax.rsqrt(jnp.mean(x * x, axis=-1, keepdims=True) + NORM_EPS) * gain
        err = out - t_ref[...]
        loss_ref[...] += 0.5 * jnp.sum(jnp.mean(err * err, axis=-1, keepdims=True), axis=0, keepdims=True)
        dx, dg = _rms_bwd_math(x, gain, err * (1.0 / d))
        dh_ref[...] = dx
        dhb_ref[...] = (0.5 * dx).astype(BF16)
        dg_ref[...] += dg

    row = pl.BlockSpec((ROW_TILE, d), lambda i: (i, 0))
    vec = pl.BlockSpec((1, d), lambda i: (0, 0))
    one = pl.BlockSpec((SUBLANES, LANES), lambda i: (0, 0))
    return pl.pallas_call(
        body, out_shape=(SDS((t, d), F32), SDS((t, d), BF16), SDS((SUBLANES, LANES), F32), SDS((1, d), F32)),
        grid=(t // ROW_TILE,), in_specs=[row, vec, row], out_specs=(row, row, one, vec),
        name=name, compiler_params=_params(1))(h, gain, target)


def _mixnorm_fwd(ya, yb, ga, gb, name):
    t, c = ya.shape

    def body(ya_ref, yb_ref, ga_ref, gb_ref, y_ref, yt_ref):
        for k, (src, g_ref) in enumerate(((ya_ref, ga_ref), (yb_ref, gb_ref))):
            x = src[...]
            u = x * lax.rsqrt(jnp.mean(x * x, axis=-1, keepdims=True) + NORM_EPS) * g_ref[...]
            y_ref[:, k * c:(k + 1) * c] = u.astype(BF16)
            yt_ref[k * c:(k + 1) * c, :] = u.T.astype(BF16)

    row = pl.BlockSpec((ROW_TILE, c), lambda i: (i, 0))
    vec = pl.BlockSpec((1, c), lambda i: (0, 0))
    return pl.pallas_call(
        body, out_shape=(SDS((t, 2 * c), BF16), SDS((2 * c, t), BF16)), grid=(t // ROW_TILE,),
        in_specs=[row, row, vec, vec],
        out_specs=(pl.BlockSpec((ROW_TILE, 2 * c), lambda i: (i, 0)), pl.BlockSpec((2 * c, ROW_TILE), lambda i: (0, i))),
        name=name, compiler_params=_params(1))(ya, yb, ga, gb)


def _mixnorm_bwd(dy, ya, yb, ga, gb, name):
    t, c = ya.shape

    def body(dy_ref, ya_ref, yb_ref, ga_ref, gb_ref, dya_ref, dyb_ref, dga_ref, dgb_ref):
        @pl.when(pl.program_id(0) == 0)
        def _():
            dga_ref[...] = jnp.zeros_like(dga_ref)
            dgb_ref[...] = jnp.zeros_like(dgb_ref)

        dxa, dga = _rms_bwd_math(ya_ref[...], ga_ref[...], dy_ref[:, :c])
        dxb, dgb = _rms_bwd_math(yb_ref[...], gb_ref[...], dy_ref[:, c:])
        dya_ref[...] = dxa
        dyb_ref[...] = dxb
        dga_ref[...] += dga
        dgb_ref[...] += dgb

    row = pl.BlockSpec((ROW_TILE, c), lambda i: (i, 0))
    vec = pl.BlockSpec((1, c), lambda i: (0, 0))
    return pl.pallas_call(
        body, out_shape=(SDS((t, c), F32), SDS((t, c), F32), SDS((1, c), F32), SDS((1, c), F32)),
        grid=(t // ROW_TILE,),
        in_specs=[pl.BlockSpec((ROW_TILE, 2 * c), lambda i: (i, 0)), row, row, vec, vec],
        out_specs=(row, row, vec, vec), name=name, compiler_params=_params(1))(dy, ya, yb, ga, gb)


def _tile(n, want):
    return max(t for t in range(LANES, min(n, want) + 1, LANES) if n % t == 0)


def _mm(a, b, *, nt, out_dtype, tm, tn, name, residual=None):
    m, k = a.shape
    n = b.shape[0] if nt else b.shape[1]
    tm, tn = _tile(m, tm), _tile(n, tn)

    def body(a_ref, b_ref, *rest):
        o_ref = rest[-1]
        av, bv = a_ref[...].astype(BF16), b_ref[...].astype(BF16)
        if nt:
            out = lax.dot_general(av, bv, NT, preferred_element_type=F32)
        else:
            out = jnp.dot(av, bv, preferred_element_type=F32)
        if residual is not None:
            out = rest[0][...] + out
        o_ref[...] = out.astype(out_dtype)

    in_specs = [pl.BlockSpec((tm, k), lambda i, j: (i, 0)),
                pl.BlockSpec((tn, k), lambda i, j: (j, 0)) if nt else pl.BlockSpec((k, tn), lambda i, j: (0, j))]
    args = [a, b]
    if residual is not None:
        in_specs.append(pl.BlockSpec((tm, tn), lambda i, j: (i, j)))
        args.append(residual)
    return pl.pallas_call(
        body, out_shape=SDS((m, n), out_dtype), grid=(m // tm, n // tn), in_specs=in_specs,
        out_specs=pl.BlockSpec((tm, tn), lambda i, j: (i, j)), name=name, compiler_params=_params(2))(*args)


FFN_TM = 512
FFN_HB = 512


def _ffn_fwd(h, u, w_in_t, w_out, name):
    t, d = h.shape
    f = w_out.shape[0]
    nk = f // FFN_HB

    def body(u_ref, w_ref, wo_ref, h_ref, hn_ref, g_ref, up_ref, acc):
        k = pl.program_id(1)

        @pl.when(k == 0)
        def _():
            acc[...] = jnp.zeros_like(acc)

        uu = u_ref[...]
        g = lax.dot_general(uu, w_ref[0], NT, preferred_element_type=F32)
        up = lax.dot_general(uu, w_ref[1], NT, preferred_element_type=F32)
        g_ref[...] = g
        up_ref[...] = up
        hid = (g * _sigmoid(g)) * up
        acc[...] += jnp.dot(hid.astype(BF16), wo_ref[...], preferred_element_type=F32)

        @pl.when(k == nk - 1)
        def _():
            hn_ref[...] = h_ref[...] + 0.5 * acc[...]

    tok = pl.BlockSpec((FFN_TM, d), lambda i, k: (i, 0))
    pre = pl.BlockSpec((FFN_TM, FFN_HB), lambda i, k: (i, k))
    return pl.pallas_call(
        body, out_shape=(SDS((t, d), F32), SDS((t, f), F32), SDS((t, f), F32)), grid=(t // FFN_TM, nk),
        in_specs=[tok, pl.BlockSpec((2, FFN_HB, d), lambda i, k: (0, k, 0)),
                  pl.BlockSpec((FFN_HB, d), lambda i, k: (k, 0)), tok],
        out_specs=(tok, pre, pre), scratch_shapes=[pltpu.VMEM((FFN_TM, d), F32)],
        name=name, compiler_params=_params(2))(u, w_in_t.reshape(2, f, d), w_out, h)


def _ffn_bwd(dfb, gpre, upre, w_in_t, w_out, name):
    t, d = dfb.shape
    f = w_out.shape[0]
    nk = f // FFN_HB

    def body(df_ref, g_ref, up_ref, w_ref, wo_ref, du_ref, hid_t_ref, da_t_ref, acc):
        k = pl.program_id(1)

        @pl.when(k == 0)
        def _():
            acc[...] = jnp.zeros_like(acc)

        dhid = lax.dot_general(df_ref[...], wo_ref[...], NT, preferred_element_type=F32)
        g, up = g_ref[...], up_ref[...]
        sig = _sigmoid(g)
        silu = g * sig
        dup = dhid * silu
        dg = dhid * up * (sig * (1.0 + g * (1.0 - sig)))
        hid_t_ref[...] = (silu * up).T.astype(BF16)
        da_t_ref[0] = dg.T.astype(BF16)
        da_t_ref[1] = dup.T.astype(BF16)
        acc[...] += (jnp.dot(dg.astype(BF16), w_ref[0], preferred_element_type=F32)
                     + jnp.dot(dup.astype(BF16), w_ref[1], preferred_element_type=F32))

        @pl.when(k == nk - 1)
        def _():
            du_ref[...] = acc[...]

    tok = pl.BlockSpec((FFN_TM, d), lambda i, k: (i, 0))
    pre = pl.BlockSpec((FFN_TM, FFN_HB), lambda i, k: (i, k))
    return pl.pallas_call(
        body, out_shape=(SDS((t, d), F32), SDS((f, t), BF16), SDS((2, f, t), BF16)), grid=(t // FFN_TM, nk),
        in_specs=[tok, pre, pre, pl.BlockSpec((2, FFN_HB, d), lambda i, k: (0, k, 0)),
                  pl.BlockSpec((FFN_HB, d), lambda i, k: (k, 0))],
        out_specs=(tok, pl.BlockSpec((FFN_HB, FFN_TM), lambda i, k: (k, i)),
                   pl.BlockSpec((2, FFN_HB, FFN_TM), lambda i, k: (0, k, i))),
        scratch_shapes=[pltpu.VMEM((FFN_TM, d), F32)],
        name=name, compiler_params=_params(2))(dfb, gpre, upre, w_in_t.reshape(2, f, d), w_out)


CH = LANES
PAD = SUBLANES


def _lru_gates(xc, gw_ref, gb_ref, lam_ref, z):
    xcb = xc.astype(BF16)
    r = _sigmoid(jnp.dot(xcb, gw_ref[2 * z], preferred_element_type=F32) + gb_ref[pl.ds(2 * z, 1), :])
    i = _sigmoid(jnp.dot(xcb, gw_ref[2 * z + 1], preferred_element_type=F32) + gb_ref[pl.ds(2 * z + 1, 1), :])
    sp = _softplus(-lam_ref[pl.ds(z, 1), :])
    log_a = (-RG_C * r) * sp
    a = jnp.exp(log_a)
    mult = jnp.sqrt(-_expm1(2.0 * log_a))
    return r, i, sp, a, mult


def _conv(xpad, cw_ref, cb_ref, t):
    xc = cb_ref[...] + cw_ref[pl.ds(0, 1), :] * xpad[pl.ds(PAD - 2, t), :]
    for j in range(1, CONV_WIDTH):
        xc = xc + cw_ref[pl.ds(j, 1), :] * xpad[pl.ds(PAD - 2 + j, t), :]
    return xc


def _fill_padded(pad_ref, value, t):
    pad_ref[pl.ds(0, PAD), :] = jnp.zeros((PAD, CH), F32)
    pad_ref[pl.ds(PAD + t, PAD), :] = jnp.zeros((PAD, CH), F32)
    pad_ref[pl.ds(PAD, t), :] = value


def _scan_pair(t, a_up, b_up, out_up, a_down, b_down, out_down):
    def step(tt, carry):
        hu, hd = carry
        lo = pl.multiple_of(tt * SUBLANES, SUBLANES)
        hi = pl.multiple_of(t - SUBLANES - tt * SUBLANES, SUBLANES)
        for j in range(SUBLANES):
            su, sd = pl.ds(lo + j, 1), pl.ds(hi + SUBLANES - 1 - j, 1)
            hu = a_up(su) * hu + b_up(su)
            out_up[su, :] = hu
            hd = a_down(sd) * hd + b_down(sd)
            out_down[sd, :] = hd
        return hu, hd

    zero = jnp.zeros((1, CH), F32)
    lax.fori_loop(0, t // SUBLANES, step, (zero, zero))


def _lru_fwd(proj, cw, cb, gw, gb, lam, name):
    t = proj.shape[0]
    c = cw.shape[1]
    ncb = c // CH

    def body(x_ref, g_ref, cw_ref, cb_ref, gw_ref, gb_ref, lam_ref, ya_ref, hf_ref, hb_ref, xpad, a0, b0, a1, b1):
        _fill_padded(xpad, x_ref[...], t)
        xc = _conv(xpad, cw_ref, cb_ref, t)
        for z, (a_s, b_s) in enumerate(((a0, b0), (a1, b1))):
            _, i, _, a, mult = _lru_gates(xc, gw_ref, gb_ref, lam_ref, z)
            a_s[...] = a
            b_s[...] = mult * (i * xc)
        _scan_pair(t, lambda s: a0[s, :], lambda s: b0[s, :], hf_ref, lambda s: a1[s, :], lambda s: b1[s, :], hb_ref)
        gelu, _ = _gelu_parts(g_ref[...])
        ya_ref[...] = gelu * (hf_ref[...] + hb_ref[...])

    col = lambda off: pl.BlockSpec((t, CH), lambda i: (0, off + i))
    small = lambda rows: pl.BlockSpec((rows, CH), lambda i: (0, i))
    return pl.pallas_call(
        body, out_shape=(SDS((t, c), F32),) * 3, grid=(ncb,),
        in_specs=[col(0), col(ncb), small(CONV_WIDTH), small(1),
                  pl.BlockSpec((4, None, CH, CH), lambda i: (0, i, 0, 0)), small(4), small(2)],
        out_specs=(col(0),) * 3,
        scratch_shapes=[pltpu.VMEM((t + 2 * PAD, CH), F32)] + [pltpu.VMEM((t, CH), F32)] * 4,
        name=name, compiler_params=_params(1))(proj, proj, cw, cb, gw, gb, lam)


def _lru_bwd(proj, cw, cb, gw, gb, lam, hf, hb, dya, name):
    t = proj.shape[0]
    c = cw.shape[1]
    ncb = c // CH

    def body(x_ref, g_ref, cw_ref, cb_ref, gw_ref, gb_ref, lam_ref, hf_ref, hb_ref, dya_ref,
             dx_ref, dg_ref, dt_ref, dcw_ref, dcb_ref, dgw_ref, dgb_ref, dlam_ref,
             xpad, hpad, dxc, a0, a1, dhs, dh0, dh1):
        _fill_padded(xpad, x_ref[...], t)
        xc = _conv(xpad, cw_ref, cb_ref, t)
        xcb = xc.astype(BF16)
        gates = [_lru_gates(xc, gw_ref, gb_ref, lam_ref, z) for z in range(2)]
        a0[...] = gates[0][3]
        a1[...] = gates[1][3]

        gelu, dgelu = _gelu_parts(g_ref[...])
        dya = dya_ref[...]
        dgate = dya * (hf_ref[...] + hb_ref[...]) * dgelu
        dg_ref[...] = dgate.astype(BF16)
        dt_ref[1] = dgate.T.astype(BF16)
        dhs[...] = dya * gelu

        def step(tt, carry):
            c0, p0, c1, p1 = carry
            lo = pl.multiple_of(tt * SUBLANES, SUBLANES)
            hi = pl.multiple_of(t - SUBLANES - tt * SUBLANES, SUBLANES)
            for j in range(SUBLANES):
                su, sd = pl.ds(lo + j, 1), pl.ds(hi + SUBLANES - 1 - j, 1)
                c0 = dhs[sd, :] + p0 * c0
                dh0[sd, :] = c0
                p0 = a0[sd, :]
                c1 = dhs[su, :] + p1 * c1
                dh1[su, :] = c1
                p1 = a1[su, :]
            return c0, p0, c1, p1

        zero = jnp.zeros((1, CH), F32)
        lax.fori_loop(0, t // SUBLANES, step, (zero, zero, zero, zero))

        acc_dxc = jnp.zeros((t, CH), F32)
        for z, (h_ref, dh_ref, shift) in enumerate(((hf_ref, dh0, -1), (hb_ref, dh1, 1))):
            r, i, sp, a, mult = gates[z]
            _fill_padded(hpad, h_ref[...], t)
            h_nb = hpad[pl.ds(PAD + shift, t), :]
            db = dh_ref[...]
            da = db * h_nb
            d_i = db * mult * xc
            acc_dxc = acc_dxc + db * mult * i
            d_mult = db * i * xc
            d_la = da * a - d_mult * (a * a) / mult
            d_r = d_la * (-RG_C * sp)
            dlam_ref[pl.ds(z, 1), :] = (jnp.sum(d_la * (-RG_C * r), axis=0, keepdims=True)
                                        * (-_sigmoid(-lam_ref[pl.ds(z, 1), :])))
            for gate, d_pre in ((0, d_r * r * (1.0 - r)), (1, d_i * i * (1.0 - i))):
                zg = 2 * z + gate
                dgb_ref[pl.ds(zg, 1), :] = jnp.sum(d_pre, axis=0, keepdims=True)
                d_pre_b = d_pre.astype(BF16)
                dgw_ref[zg] = lax.dot_general(xcb, d_pre_b, TN, preferred_element_type=F32)
                acc_dxc = acc_dxc + lax.dot_general(d_pre_b, gw_ref[zg], NT, preferred_element_type=F32)

        dcb_ref[...] = jnp.sum(acc_dxc, axis=0, keepdims=True)
        for j in range(CONV_WIDTH):
            dcw_ref[pl.ds(j, 1), :] = jnp.sum(acc_dxc * xpad[pl.ds(PAD - 2 + j, t), :], axis=0, keepdims=True)
        _fill_padded(dxc, acc_dxc, t)
        dx = cw_ref[pl.ds(0, 1), :] * dxc[pl.ds(PAD + 2, t), :]
        for j in range(1, CONV_WIDTH):
            dx = dx + cw_ref[pl.ds(j, 1), :] * dxc[pl.ds(PAD + 2 - j, t), :]
        dx_ref[...] = dx.astype(BF16)
        dt_ref[0] = dx.T.astype(BF16)

    col = lambda off: pl.BlockSpec((t, CH), lambda i: (0, off + i))
    small = lambda rows: pl.BlockSpec((rows, CH), lambda i: (0, i))
    dense = pl.BlockSpec((4, None, CH, CH), lambda i: (0, i, 0, 0))
    padded = pltpu.VMEM((t + 2 * PAD, CH), F32)
    return pl.pallas_call(
        body,
        out_shape=(SDS((t, c), BF16), SDS((t, c), BF16), SDS((2, c, t), BF16), SDS((CONV_WIDTH, c), F32),
                   SDS((1, c), F32), SDS((4, ncb, CH, CH), F32), SDS((4, c), F32), SDS((2, c), F32)),
        grid=(ncb,),
        in_specs=[col(0), col(ncb), small(CONV_WIDTH), small(1), dense, small(4), small(2), col(0), col(0), col(0)],
        out_specs=(col(0), col(0), pl.BlockSpec((2, CH, t), lambda i: (0, i, 0)), small(CONV_WIDTH), small(1),
                   dense, small(4), small(2)),
        scratch_shapes=[padded, padded, padded] + [pltpu.VMEM((t, CH), F32)] * 5,
        name=name, compiler_params=_params(1))(proj, proj, cw, cb, gw, gb, lam, hf, hb, dya)


def _band_start(r, rows):
    return jnp.clip(r - WIN_ROWS // 2, 0, rows - WIN_ROWS)


def _bias_tables(rpb):
    cols = np.arange(GRID_W)
    start = np.clip(cols - WIN_COLS // 2, 0, GRID_W - WIN_COLS)
    valid = (cols[None, :] >= start[:, None]) & (cols[None, :] < start[:, None] + WIN_COLS)
    col_off = np.clip(cols[None, :] - cols[:, None] + WIN_COLS - 1, 0, 2 * WIN_COLS - 2)
    row_off = np.arange(WIN_ROWS)[None, :] - np.arange(WIN_ROWS)[:, None] + WIN_ROWS - 1
    pick_row = jnp.asarray(np.eye(2 * WIN_ROWS - 1, dtype=np.float32)[row_off])
    pick_col = jnp.asarray(np.eye(2 * WIN_COLS - 1, dtype=np.float32)[col_off] * valid[..., None])
    hi = lax.Precision.HIGHEST
    by_row = jnp.einsum("hrc,ajr->hajc", rpb, pick_row, precision=hi)
    table = jnp.einsum("hajc,qkc->haqjk", by_row, pick_col, precision=hi)
    table = jnp.where(jnp.asarray(valid)[None, None, :, None, :], table, NEG)
    return table.reshape(rpb.shape[0], WIN_ROWS, GRID_W, BAND)


def _attn_scores(q_ref, k_ref, bm_ref, hh, r, rows):
    rs = _band_start(r, rows)
    lanes = pl.ds(hh * HEAD_DIM, HEAD_DIM)
    qrows = pl.ds(pl.multiple_of(r * GRID_W, GRID_W), GRID_W)
    band = pl.ds(pl.multiple_of(rs * GRID_W, GRID_W), BAND)
    q = q_ref[qrows, lanes].astype(BF16)
    kb = k_ref[band, lanes].astype(BF16)
    s = lax.dot_general(q, kb, NT, preferred_element_type=F32) * (HEAD_DIM ** -0.5) + bm_ref[hh, r - rs]
    p = jnp.exp(s - jnp.max(s, axis=-1, keepdims=True))
    p = p / jnp.sum(p, axis=-1, keepdims=True)
    return q, kb, p, qrows, band, lanes, r - rs


def _attn_fwd(proj, tables, width, name):
    t = proj.shape[0]
    rows = t // GRID_W
    npair = width // LANES
    first = (proj.shape[1] - 3 * width) // LANES

    def body(q_ref, k_ref, v_ref, bm_ref, o_ref):
        for hh in range(2):
            def row(r, carry):
                _, _, p, qrows, band, lanes, _ = _attn_scores(q_ref, k_ref, bm_ref, hh, r, rows)
                vb = v_ref[band, lanes].astype(BF16)
                o_ref[qrows, lanes] = jnp.dot(p.astype(BF16), vb, preferred_element_type=F32)
                return carry

            lax.fori_loop(0, rows, row, 0)

    col = lambda off: pl.BlockSpec((t, LANES), lambda i: (0, off + i))
    return pl.pallas_call(
        body, out_shape=SDS((t, width), F32), grid=(npair,),
        in_specs=[col(first), col(first + npair), col(first + 2 * npair),
                  pl.BlockSpec((2, WIN_ROWS, GRID_W, BAND), lambda i: (i, 0, 0, 0))],
        out_specs=col(0), name=name, compiler_params=_params(1))(proj, proj, proj, tables)


def _attn_bwd(proj, tables, dyb, name):
    t, width = dyb.shape
    rows = t // GRID_W
    npair = width // LANES
    first = (proj.shape[1] - 3 * width) // LANES

    def body(q_ref, k_ref, v_ref, bm_ref, do_ref, dq_ref, dk_ref, dv_ref, dt_ref, dbm_ref, dq_s, dk_s, dv_s):
        dk_s[...] = jnp.zeros_like(dk_s)
        dv_s[...] = jnp.zeros_like(dv_s)
        dbm_ref[...] = jnp.zeros_like(dbm_ref)
        for hh in range(2):
            def row(r, carry):
                q, kb, p, qrows, band, lanes, case = _attn_scores(q_ref, k_ref, bm_ref, hh, r, rows)
                vb = v_ref[band, lanes].astype(BF16)
                do = do_ref[qrows, lanes].astype(BF16)
                dp = lax.dot_general(do, vb, NT, preferred_element_type=F32)
                ds = p * (dp - jnp.sum(dp * p, axis=-1, keepdims=True))
                dbm_ref[hh, case] += ds
                dsb = (ds * (HEAD_DIM ** -0.5)).astype(BF16)
                dq_s[qrows, lanes] = jnp.dot(dsb, kb, preferred_element_type=F32)
                dk_s[band, lanes] += lax.dot_general(dsb, q, TN, preferred_element_type=F32)
                dv_s[band, lanes] += lax.dot_general(p.astype(BF16), do, TN, preferred_element_type=F32)
                return carry

            lax.fori_loop(0, rows, row, 0)
        for n, (src, dst) in enumerate(((dq_s, dq_ref), (dk_s, dk_ref), (dv_s, dv_ref))):
            val = src[...]
            dst[...] = val.astype(BF16)
            dt_ref[n] = val.T.astype(BF16)

    col = lambda off: pl.BlockSpec((t, LANES), lambda i: (0, off + i))
    table = pl.BlockSpec((2, WIN_ROWS, GRID_W, BAND), lambda i: (i, 0, 0, 0))
    return pl.pallas_call(
        body, out_shape=(SDS((t, width), BF16),) * 3 + (SDS((3, width, t), BF16), SDS(tables.shape, F32)),
        grid=(npair,),
        in_specs=[col(first), col(first + npair), col(first + 2 * npair), table, col(0)],
        out_specs=(col(0), col(0), col(0), pl.BlockSpec((3, LANES, t), lambda i: (0, i, 0)), table),
        scratch_shapes=[pltpu.VMEM((t, LANES), F32)] * 3,
        name=name, compiler_params=_params(1))(proj, proj, proj, tables, dyb)


def _adamw_math(w, g, m, v):
    m = ADAM_B1 * m + (1.0 - ADAM_B1) * g
    v = ADAM_B2 * v + (1.0 - ADAM_B2) * (g * g)
    m_hat = m / (1.0 - ADAM_B1 ** ADAM_STEP)
    v_hat = v / (1.0 - ADAM_B2 ** ADAM_STEP)
    delta = -ADAM_LR * (m_hat / (jnp.sqrt(v_hat) + ADAM_EPS) + ADAM_WD * w)
    return delta, m, v


def _sum_partials(p_ref):
    g = p_ref[0].astype(F32)
    for s in range(1, N_DEV):
        g = g + p_ref[s].astype(F32)
    return g


def _adamw_rows(w, partials, m, v, name):
    rb, n = w.shape
    tr = 64

    def body(w_ref, p_ref, m_ref, v_ref, g_ref, d_ref, nm_ref, nv_ref):
        g = _sum_partials(p_ref)
        g_ref[...] = g
        d_ref[...], nm_ref[...], nv_ref[...] = _adamw_math(w_ref[...], g, m_ref[...], v_ref[...])

    blk = pl.BlockSpec((tr, n), lambda i: (i, 0))
    return pl.pallas_call(
        body, out_shape=(SDS((rb, n), F32),) * 4, grid=(rb // tr,),
        in_specs=[blk, pl.BlockSpec((N_DEV, tr, n), lambda i: (0, i, 0)), blk, blk], out_specs=(blk,) * 4,
        name=name, compiler_params=_params(1))(w, partials.reshape(N_DEV, rb, n), m, v)


def _adamw_cols(w, partials, m, v, name):
    d, nb = w.shape
    td = 256

    def body(w_ref, p_ref, m_ref, v_ref, g_ref, d_ref, nm_ref, nv_ref):
        g = _sum_partials(p_ref).T
        g_ref[...] = g
        d_ref[...], nm_ref[...], nv_ref[...] = _adamw_math(w_ref[...], g, m_ref[...], v_ref[...])

    blk = pl.BlockSpec((td, nb), lambda i: (i, 0))
    return pl.pallas_call(
        body, out_shape=(SDS((d, nb), F32),) * 4, grid=(d // td,),
        in_specs=[blk, pl.BlockSpec((N_DEV, nb, td), lambda i: (0, 0, i)), blk, blk], out_specs=(blk,) * 4,
        name=name, compiler_params=_params(1))(w, partials.reshape(N_DEV, nb, d), m, v)


def _adamw_small(w, g, m, v, name):
    def body(w_ref, g_ref, m_ref, v_ref, d_ref, nm_ref, nv_ref):
        d_ref[...], nm_ref[...], nv_ref[...] = _adamw_math(w_ref[...], g_ref[...], m_ref[...], v_ref[...])

    return pl.pallas_call(body, out_shape=(SDS(w.shape, F32),) * 3, name=name,
                          compiler_params=pltpu.CompilerParams(vmem_limit_bytes=VMEM_LIMIT))(w, g, m, v)


TILE = SUBLANES * LANES


def _pack(arrays):
    parts = []
    for a in arrays:
        flat = a.reshape(-1).astype(F32)
        flat = jnp.pad(flat, (0, -flat.size % TILE))
        parts.append(flat.reshape(-1, LANES))
    return jnp.concatenate(parts, axis=0)


def _unpack(pack, like):
    out, row = [], 0
    for a in like:
        n = int(np.prod(a.shape))
        nrows = -(-n // TILE) * SUBLANES
        out.append(pack[row:row + nrows].reshape(-1)[:n].reshape(a.shape))
        row += nrows
    return out


def _dense_gate_blocks(gate_w):
    w = gate_w.reshape(4, -1, 2, HEAD_DIM, HEAD_DIM)
    zero = jnp.zeros_like(w[:, :, 0])
    top = jnp.concatenate([w[:, :, 0], zero], axis=-1)
    bottom = jnp.concatenate([zero, w[:, :, 1]], axis=-1)
    return jnp.concatenate([top, bottom], axis=-2)


def _diag_gate_blocks(dense, shape):
    even = dense[:, :, :HEAD_DIM, :HEAD_DIM]
    odd = dense[:, :, HEAD_DIM:, HEAD_DIM:]
    return jnp.stack([even, odd], axis=2).reshape(shape)


def _forward_backward(x, target, w, s):
    c = s["lru_conv_b"].shape[1]
    width = s["attn_out_norm"].shape[1]

    u1 = _rmsnorm_fwd(x, s["norm_ffn1"], "norm_ffn1")
    h1, g1, up1 = _ffn_fwd(x, u1, w["ffn1_in"], w["ffn1_out"], "ffn1_fwd")
    u2 = _rmsnorm_fwd(h1, s["norm_mix"], "norm_mix")
    proj = _mm(u2, w["mix_in"], nt=True, out_dtype=F32, tm=512, tn=512, name="mix_in_proj")
    gw = _dense_gate_blocks(s["lru_gate_w"]).astype(BF16)
    gb = s["lru_gate_b"].reshape(4, c)
    ya, hf, hb = _lru_fwd(proj, s["lru_conv_w"], s["lru_conv_b"], gw, gb, s["lru_lambda"], "lru_fwd")
    tables, tables_vjp = jax.vjp(_bias_tables, s["attn_rpb"])
    yb = _attn_fwd(proj, tables, width, "attn_fwd")
    y, yt = _mixnorm_fwd(ya, yb, s["lru_out_norm"], s["attn_out_norm"], "mix_norm")
    h2 = _mm(y, w["mix_out"], nt=False, out_dtype=F32, tm=512, tn=512, name="mix_out_proj", residual=h1)
    u3 = _rmsnorm_fwd(h2, s["norm_ffn2"], "norm_ffn2")
    h3, g2, up2 = _ffn_fwd(h2, u3, w["ffn2_in"], w["ffn2_out"], "ffn2_fwd")
    dh3, df2, loss_part, d_norm_final = _final_loss(h3, s["norm_final"], target, "final_loss")

    grads = {}
    du3, hid2_t, da2_t = _ffn_bwd(df2, g2, up2, w["ffn2_in"], w["ffn2_out"], "ffn2_bwd")
    f = hid2_t.shape[0]
    t = x.shape[0]
    grads["ffn2_out"] = _mm(hid2_t, df2, nt=False, out_dtype=BF16, tm=512, tn=1024, name="ffn2_out_grad")
    grads["ffn2_in"] = _mm(da2_t.reshape(2 * f, t), u3, nt=False, out_dtype=BF16, tm=512, tn=1024, name="ffn2_in_grad")
    dh2, dh2b, d_norm_ffn2 = _rmsnorm_bwd(du3, h2, s["norm_ffn2"], dh3, 1.0, "norm_ffn2_bwd")

    grads["mix_out"] = _mm(yt, dh2b, nt=False, out_dtype=BF16, tm=512, tn=1024, name="mix_out_grad")
    dy = _mm(dh2b, w["mix_out"], nt=True, out_dtype=F32, tm=512, tn=512, name="mix_out_bwd")
    dya, dyb, d_lru_out_norm, d_attn_out_norm = _mixnorm_bwd(dy, ya, yb, s["lru_out_norm"], s["attn_out_norm"],
                                                             "mix_norm_bwd")
    dq, dk, dv, dqkv_t, d_tables = _attn_bwd(proj, tables, dyb, "attn_bwd")
    (dx_lru, dg_lru, dxg_t, d_conv_w, d_conv_b, d_gw, d_gb, d_lam) = _lru_bwd(
        proj, s["lru_conv_w"], s["lru_conv_b"], gw, gb, s["lru_lambda"], hf, hb, dya, "lru_bwd")
    dproj = jnp.concatenate([dx_lru, dg_lru, dq, dk, dv], axis=1)
    dproj_t = jnp.concatenate([dxg_t.reshape(2 * c, t), dqkv_t.reshape(3 * width, t)], axis=0)
    grads["mix_in"] = _mm(dproj_t, u2, nt=False, out_dtype=BF16, tm=512, tn=1024, name="mix_in_grad")
    du2 = _mm(dproj, w["mix_in"], nt=False, out_dtype=F32, tm=512, tn=512, name="mix_in_bwd")
    dh1, df1, d_norm_mix = _rmsnorm_bwd(du2, h1, s["norm_mix"], dh2, 0.5, "norm_mix_bwd")

    du1, hid1_t, da1_t = _ffn_bwd(df1, g1, up1, w["ffn1_in"], w["ffn1_out"], "ffn1_bwd")
    grads["ffn1_out"] = _mm(hid1_t, df1, nt=False, out_dtype=BF16, tm=512, tn=1024, name="ffn1_out_grad")
    grads["ffn1_in"] = _mm(da1_t.reshape(2 * f, t), u1, nt=False, out_dtype=BF16, tm=512, tn=1024, name="ffn1_in_grad")
    grad_x, _, d_norm_ffn1 = _rmsnorm_bwd(du1, x, s["norm_ffn1"], dh1, 1.0, "norm_ffn1_bwd")

    small = {
        "norm_ffn1": d_norm_ffn1, "norm_mix": d_norm_mix, "lru_conv_w": d_conv_w, "lru_conv_b": d_conv_b,
        "lru_gate_w": _diag_gate_blocks(d_gw, s["lru_gate_w"].shape), "lru_gate_b": d_gb.reshape(s["lru_gate_b"].shape),
        "lru_lambda": d_lam, "attn_rpb": tables_vjp(d_tables)[0], "lru_out_norm": d_lru_out_norm,
        "attn_out_norm": d_attn_out_norm, "norm_ffn2": d_norm_ffn2, "norm_final": d_norm_final,
    }
    return loss_part[0, 0], grad_x, grads, small


LARGE = ("ffn1_w_in", "ffn1_w_out", "w_in_mix", "w_out_mix", "ffn2_w_in", "ffn2_w_out")
LARGE_KEY = {"ffn1_w_in": "ffn1_in", "ffn1_w_out": "ffn1_out", "w_in_mix": "mix_in", "w_out_mix": "mix_out",
             "ffn2_w_in": "ffn2_in", "ffn2_w_out": "ffn2_out"}
COLUMN_SHARDED = ("ffn1_w_in", "w_in_mix", "ffn2_w_in")
SHARDED_SMALL = ("lru_conv_w", "lru_lambda")
REPLICATED = ("norm_ffn1", "norm_mix", "lru_conv_b", "lru_gate_w", "lru_gate_b", "attn_rpb", "lru_out_norm",
              "attn_out_norm", "norm_ffn2", "norm_final")
WEIGHTS = ("norm_ffn1", "ffn1_w_in", "ffn1_w_out", "norm_mix", "w_in_mix", "lru_conv_w", "lru_conv_b", "lru_gate_w",
           "lru_gate_b", "lru_lambda", "attn_rpb", "lru_out_norm", "attn_out_norm", "w_out_mix", "norm_ffn2",
           "ffn2_w_in", "ffn2_w_out", "norm_final")


def _step(x, loss_target, p, m, v):
    me = 4 * lax.axis_index("x") + 2 * lax.axis_index("y") + lax.axis_index("c")

    shards = [(_cast_transposed if n in COLUMN_SHARDED else _cast_rows)(p[n], "cast_" + n) for n in LARGE]
    sharded_small = (jnp.pad(p["lru_conv_w"], ((0, SUBLANES - CONV_WIDTH), (0, 0)))
                     + jnp.pad(p["lru_lambda"], ((CONV_WIDTH, SUBLANES - CONV_WIDTH - 2), (0, 0))))
    gathered = _all_gather(shards + [sharded_small], "all_gather_weights")
    w = {LARGE_KEY[n]: g for n, g in zip(LARGE, gathered)}
    full_small = gathered[-1].reshape(N_DEV, SUBLANES, LANES)
    s = {n: p[n] if n in ("lru_gate_w", "lru_gate_b", "attn_rpb") else p[n].reshape(1, -1) for n in REPLICATED}
    s["lru_conv_w"] = full_small[:, :CONV_WIDTH].transpose(1, 0, 2).reshape(CONV_WIDTH, -1)
    s["lru_lambda"] = full_small[:, CONV_WIDTH:CONV_WIDTH + 2].transpose(1, 0, 2).reshape(2, -1)

    loss_part, grad_x, grads, small = _forward_backward(x, loss_target, w, s)
    loss = lax.psum(loss_part, ("x", "y", "c"))

    partials = _scatter_blocks([grads[LARGE_KEY[n]] for n in LARGE], "scatter_grads")
    out = {}
    for n, part in zip(LARGE, partials):
        update = _adamw_cols if n in COLUMN_SHARDED else _adamw_rows
        out[n] = update(p[n], part, m[n], v[n], "adamw_" + n)

    by_device = lambda a: a.reshape(a.shape[0], N_DEV, LANES).transpose(1, 0, 2)
    small_list = [small[n] for n in REPLICATED] + [by_device(small[n]) for n in SHARDED_SMALL]
    reduced = _unpack(_all_reduce_small(_pack(small_list), "all_reduce_small"), small_list)
    g_rep = dict(zip(REPLICATED, reduced[:len(REPLICATED)]))
    g_sh = {n: lax.dynamic_index_in_dim(r, me, axis=0, keepdims=False)
            for n, r in zip(SHARDED_SMALL, reduced[len(REPLICATED):])}
    g_small = {**g_rep, **g_sh}
    names = REPLICATED + SHARDED_SMALL
    like = [p[n] for n in names]
    pack_of = lambda d: _pack([d[n].reshape(p[n].shape) for n in names])
    upd = _adamw_small(pack_of(p), pack_of(g_small), pack_of(m), pack_of(v), "adamw_small")
    for n, d_, m_, v_ in zip(names, *[_unpack(u, like) for u in upd]):
        out[n] = (g_small[n].reshape(p[n].shape), d_, m_, v_)
    return loss, grad_x, out


def kernel(x, norm_ffn1, ffn1_w_in, ffn1_w_out, norm_mix, w_in_mix, lru_conv_w, lru_conv_b, lru_gate_w, lru_gate_b, lru_lambda, attn_rpb, lru_out_norm, attn_out_norm, w_out_mix, norm_ffn2, ffn2_w_in, ffn2_w_out, norm_final, loss_target, m_norm_ffn1, m_ffn1_w_in, m_ffn1_w_out, m_norm_mix, m_w_in_mix, m_lru_conv_w, m_lru_conv_b, m_lru_gate_w, m_lru_gate_b, m_lru_lambda, m_attn_rpb, m_lru_out_norm, m_attn_out_norm, m_w_out_mix, m_norm_ffn2, m_ffn2_w_in, m_ffn2_w_out, m_norm_final, v_norm_ffn1, v_ffn1_w_in, v_ffn1_w_out, v_norm_mix, v_w_in_mix, v_lru_conv_w, v_lru_conv_b, v_lru_gate_w, v_lru_gate_b, v_lru_lambda, v_attn_rpb, v_lru_out_norm, v_attn_out_norm, v_w_out_mix, v_norm_ffn2, v_ffn2_w_in, v_ffn2_w_out, v_norm_final):
    given = dict(locals())
    drop_layer = lambda n, a: a if n == "norm_final" else a[0]
    p = {n: drop_layer(n, given[n]) for n in WEIGHTS}
    m = {n: drop_layer(n, given["m_" + n]) for n in WEIGHTS}
    v = {n: drop_layer(n, given["v_" + n]) for n in WEIGHTS}
    loss, grad_x, out = _step(x[0], loss_target[0], p, m, v)
    shaped = lambda n, a: a.reshape(given[n].shape)
    return (loss, grad_x[None], *[shaped(n, out[n][k]) for k in range(4) for n in WEIGHTS])
```

```python
import math

import numpy as np
import jax
import jax.numpy as jnp
from jax import lax
from jax.experimental import pallas as pl
from jax.experimental.pallas import tpu as pltpu

F32 = jnp.float32
BF16 = jnp.bfloat16
SDS = jax.ShapeDtypeStruct

N_DEV = 8
N_CHIP = 4
NORM_EPS = 1e-6
RG_C = 8.0
CONV_WIDTH = 4
HEAD_DIM = 64
GRID_W = 64
WIN_ROWS = 8
WIN_COLS = 16
BAND = WIN_ROWS * GRID_W
NEG = -1e30

ADAM_LR = 0.001
ADAM_B1 = 0.9
ADAM_B2 = 0.999
ADAM_EPS = 1e-08
ADAM_WD = 0.01
ADAM_STEP = 10

LANES = 128
SUBLANES = 8
VMEM_LIMIT = 56 * 1024 * 1024

NT = (((1,), (1,)), ((), ()))
TN = (((0,), (0,)), ((), ()))
ANY = pl.BlockSpec(memory_space=pl.ANY)
WHOLE = pl.BlockSpec(memory_space=pltpu.VMEM)
MESH = pl.DeviceIdType.MESH


def _sigmoid(x):
    return 1.0 / (1.0 + jnp.exp(-x))


def _gelu_parts(x):
    c = math.sqrt(2.0 / math.pi)
    t = jnp.tanh(c * (x + 0.044715 * (x * x * x)))
    gelu = 0.5 * x * (1.0 + t)
    dgelu = 0.5 * (1.0 + t) + 0.5 * x * (1.0 - t * t) * (c * (1.0 + 3.0 * 0.044715 * (x * x)))
    return gelu, dgelu


def _expm1(x):
    poly = x * (1.0 + x * (1.0 / 2) * (1.0 + x * (1.0 / 3) * (1.0 + x * (1.0 / 4) * (1.0 + x * (1.0 / 5) * (1.0 + x * (1.0 / 6))))))
    return jnp.where(jnp.abs(x) < 0.25, poly, jnp.exp(x) - 1.0)


def _softplus(x):
    return jnp.maximum(x, 0.0) + jnp.log1p(jnp.exp(-jnp.abs(x)))


class _Job:
    def __init__(self, ins, out_shapes, n_remote, n_local, start, finish):
        self.ins, self.out_shapes = list(ins), list(out_shapes)
        self.n_remote, self.n_local = n_remote, max(n_local, 1)
        self.start, self.finish = start, finish


def _call(body, *, name, args, out_shape, in_specs, out_specs, grid=(), scratch_shapes=(), job=None):
    single = not isinstance(out_shape, (tuple, list))
    out_shape = (out_shape,) if single else tuple(out_shape)
    out_specs = (out_specs,) if single else tuple(out_specs)
    params = pltpu.CompilerParams(dimension_semantics=("arbitrary",) * len(grid) if grid else None,
                                  vmem_limit_bytes=VMEM_LIMIT)
    if job is None:
        res = pl.pallas_call(body, out_shape=out_shape, grid=grid, in_specs=list(in_specs), out_specs=out_specs,
                             scratch_shapes=list(scratch_shapes), name=name, compiler_params=params)(*args)
        return res[0] if single else res

    n_in, n_out, n_scr = len(args), len(out_shape), len(scratch_shapes)
    j_in, j_out = len(job.ins), len(job.out_shapes)

    def hosted(*refs):
        ins, refs = refs[:n_in], refs[n_in:]
        j_ins, refs = refs[:j_in], refs[j_in:]
        outs, refs = refs[:n_out], refs[n_out:]
        j_outs, refs = refs[:j_out], refs[j_out:]
        scr, sems = refs[:n_scr], refs[n_scr:]
        if grid:
            first = last = None
            for axis, size in enumerate(grid):
                at_first, at_last = pl.program_id(axis) == 0, pl.program_id(axis) == size - 1
                first = at_first if first is None else first & at_first
                last = at_last if last is None else last & at_last
            pl.when(first)(lambda: job.start(j_ins, j_outs, *sems))
            body(*ins, *outs, *scr)
            pl.when(last)(lambda: job.finish(j_ins, j_outs, *sems))
        else:
            job.start(j_ins, j_outs, *sems)
            body(*ins, *outs, *scr)
            job.finish(j_ins, j_outs, *sems)

    res = pl.pallas_call(
        hosted, out_shape=out_shape + tuple(job.out_shapes), grid=grid,
        in_specs=list(in_specs) + [ANY] * j_in, out_specs=out_specs + (ANY,) * j_out,
        scratch_shapes=list(scratch_shapes) + [pltpu.SemaphoreType.DMA((job.n_remote,)),
                                               pltpu.SemaphoreType.DMA((job.n_remote,)),
                                               pltpu.SemaphoreType.DMA((job.n_local,))],
        name=name, compiler_params=params)(*args, *job.ins)
    own, carried = res[:n_out], res[n_out:]
    return (own[0] if single else own), carried


def _run_job(job, name):
    def body(*refs):
        j_ins, j_outs, sems = refs[:len(job.ins)], refs[len(job.ins):len(job.ins) + len(job.out_shapes)], refs[-3:]
        job.start(j_ins, j_outs, *sems)
        job.finish(j_ins, j_outs, *sems)

    return pl.pallas_call(
        body, out_shape=tuple(job.out_shapes), in_specs=[ANY] * len(job.ins), out_specs=(ANY,) * len(job.out_shapes),
        scratch_shapes=[pltpu.SemaphoreType.DMA((job.n_remote,)), pltpu.SemaphoreType.DMA((job.n_remote,)),
                        pltpu.SemaphoreType.DMA((job.n_local,))],
        name=name)(*job.ins)


def _position():
    return lax.axis_index("x"), lax.axis_index("y"), lax.axis_index("c")


def _flat(px, py, pc):
    return 4 * px + 2 * py + pc


def _gather_job(shards):
    n_arr = len(shards)

    def parts(ins, outs, send_sems, recv_sems):
        x, y, c = _position()
        me, sibling = (x, y, c), (x, y, 1 - c)
        chips = [(1 - x, y), (x, 1 - y), (1 - x, 1 - y)]

        def rows(a, block):
            rb = shards[a].shape[0]
            return outs[a].at[pl.ds(_flat(*block) * rb, rb), :]

        def copy(a, k, block, to, src=None):
            return pltpu.make_async_remote_copy(
                src_ref=rows(a, block) if src is None else src, dst_ref=rows(a, block),
                send_sem=send_sems.at[a * 7 + k], recv_sem=recv_sems.at[a * 7 + k], device_id=to, device_id_type=MESH)

        return me, sibling, chips, c, rows, copy

    def start(ins, outs, send_sems, recv_sems, local_sems):
        me, sibling, chips, c, rows, copy = parts(ins, outs, send_sems, recv_sems)
        for a in range(n_arr):
            pltpu.make_async_copy(ins[a], rows(a, me), local_sems.at[a]).start()
        for a in range(n_arr):
            copy(a, 0, me, sibling, src=ins[a]).start()
            for j, chip in enumerate(chips):
                copy(a, 1 + j, me, (*chip, c), src=ins[a]).start()

    def finish(ins, outs, send_sems, recv_sems, local_sems):
        me, sibling, chips, c, rows, copy = parts(ins, outs, send_sems, recv_sems)
        for a in range(n_arr):
            for j, chip in enumerate(chips):
                copy(a, 1 + j, (*chip, c), me).wait_recv()
                copy(a, 4 + j, (*chip, c), sibling).start()
        for a in range(n_arr):
            copy(a, 0, sibling, me).wait_recv()
            for j, chip in enumerate(chips):
                copy(a, 4 + j, (*chip, 1 - c), me).wait_recv()
        for a in range(n_arr):
            copy(a, 0, me, sibling, src=ins[a]).wait_send()
            for j, chip in enumerate(chips):
                copy(a, 1 + j, me, (*chip, c), src=ins[a]).wait_send()
                copy(a, 4 + j, (*chip, c), sibling).wait_send()
            pltpu.make_async_copy(ins[a], rows(a, me), local_sems.at[a]).wait()

    return _Job(shards, [SDS((N_DEV * s.shape[0], s.shape[1]), s.dtype) for s in shards], 7 * n_arr, n_arr, start, finish)


def _sibling_job(grads):
    n_arr = len(grads)

    def copy(ins, outs, send_sems, recv_sems, a, q):
        x, y, c = _position()
        rb = grads[a].shape[0] // N_DEV
        return pltpu.make_async_remote_copy(
            src_ref=ins[a].at[pl.ds((2 * q + 1 - c) * rb, rb), :], dst_ref=outs[a].at[pl.ds(q * rb, rb), :],
            send_sem=send_sems.at[a * N_CHIP + q], recv_sem=recv_sems.at[a * N_CHIP + q],
            device_id=(x, y, 1 - c), device_id_type=MESH)

    def start(ins, outs, send_sems, recv_sems, local_sems):
        for a in range(n_arr):
            for q in range(N_CHIP):
                copy(ins, outs, send_sems, recv_sems, a, q).start()

    def finish(ins, outs, send_sems, recv_sems, local_sems):
        for a in range(n_arr):
            for q in range(N_CHIP):
                copy(ins, outs, send_sems, recv_sems, a, q).wait()

    return _Job(grads, [SDS((g.shape[0] // 2, g.shape[1]), g.dtype) for g in grads], N_CHIP * n_arr, 0, start, finish)


CHIP_FLIPS = [(1, 0), (0, 1), (1, 1)]


def _chips_job(partials):
    n_arr = len(partials)

    def parts(ins, outs, send_sems, recv_sems):
        x, y, c = _position()

        def slot(ref, a, px, py):
            rb = partials[a].shape[0] // N_CHIP
            return ref.at[pl.ds((2 * px + py) * rb, rb), :]

        def copy(a, k, landing=False):
            px = 1 - x if CHIP_FLIPS[k][0] else x
            py = 1 - y if CHIP_FLIPS[k][1] else y
            return pltpu.make_async_remote_copy(
                src_ref=slot(outs[a], a, px, py) if landing else slot(ins[a], a, px, py),
                dst_ref=slot(outs[a], a, px, py) if landing else slot(outs[a], a, x, y),
                send_sem=send_sems.at[a * 3 + k], recv_sem=recv_sems.at[a * 3 + k],
                device_id=(px, py, c), device_id_type=MESH)

        return x, y, slot, copy

    def start(ins, outs, send_sems, recv_sems, local_sems):
        x, y, slot, copy = parts(ins, outs, send_sems, recv_sems)
        for a in range(n_arr):
            pltpu.make_async_copy(slot(ins[a], a, x, y), slot(outs[a], a, x, y), local_sems.at[a]).start()
        for k in range(3):
            for a in range(n_arr):
                copy(a, k).start()

    def finish(ins, outs, send_sems, recv_sems, local_sems):
        x, y, slot, copy = parts(ins, outs, send_sems, recv_sems)
        for k in range(3):
            for a in range(n_arr):
                copy(a, k, landing=True).wait_recv()
        for k in range(3):
            for a in range(n_arr):
                copy(a, k).wait_send()
        for a in range(n_arr):
            pltpu.make_async_copy(slot(ins[a], a, x, y), slot(outs[a], a, x, y), local_sems.at[a]).wait()

    return _Job(partials, [SDS(p.shape, p.dtype) for p in partials], 3 * n_arr, n_arr, start, finish)


def _pair_sum(g, from_sibling, name):
    rb, n = g.shape[0] // N_DEV, g.shape[1]
    tr = rb if rb * n * 2 <= 3 * 1024 * 1024 else rb // 2
    core = lax.axis_index("c").astype(jnp.int32).reshape(1)

    def body(c_ref, g_ref, r_ref, o_ref):
        o_ref[...] = (g_ref[...].astype(F32) + r_ref[...].astype(F32)).astype(BF16)

    grid_spec = pltpu.PrefetchScalarGridSpec(
        num_scalar_prefetch=1, grid=(N_CHIP, rb // tr),
        in_specs=[pl.BlockSpec((None, None, tr, n), lambda q, i, c_ref: (q, c_ref[0], i, 0)),
                  pl.BlockSpec((None, tr, n), lambda q, i, c_ref: (q, i, 0))],
        out_specs=pl.BlockSpec((None, tr, n), lambda q, i, c_ref: (q, i, 0)))
    out = pl.pallas_call(
        body, grid_spec=grid_spec, out_shape=SDS((N_CHIP, rb, n), BF16), name=name,
        compiler_params=pltpu.CompilerParams(dimension_semantics=("arbitrary",) * 2, vmem_limit_bytes=VMEM_LIMIT))(
            core, g.reshape(N_CHIP, 2, rb, n), from_sibling.reshape(N_CHIP, rb, n))
    return out.reshape(N_CHIP * rb, n)


PEER_FLIPS = [(0, 0, 1), (1, 0, 0), (0, 1, 0), (1, 1, 0), (1, 0, 1), (0, 1, 1), (1, 1, 1)]


def _flip(pos, flips):
    return tuple((1 - p) if f else p for p, f in zip(pos, flips))


def _all_reduce_small(pack, name):
    r = pack.shape[0]

    def body(p_ref, o_ref, buf, send_sems, recv_sems):
        me = _position()
        my_flat = _flat(*me)
        buf[my_flat] = p_ref[...]
        sent = []
        for k in range(7):
            peer = _flip(me, PEER_FLIPS[k])
            cp = pltpu.make_async_remote_copy(
                src_ref=p_ref, dst_ref=buf.at[my_flat], send_sem=send_sems.at[k], recv_sem=recv_sems.at[k],
                device_id=peer, device_id_type=MESH)
            cp.start()
            sent.append(cp)
        for k in range(7):
            peer = _flip(me, PEER_FLIPS[k])
            slot = buf.at[_flat(*peer)]
            pltpu.make_async_remote_copy(
                src_ref=slot, dst_ref=slot, send_sem=send_sems.at[k], recv_sem=recv_sems.at[k],
                device_id=peer, device_id_type=MESH).wait_recv()
        for cp in sent:
            cp.wait_send()
        acc = buf[0]
        for s in range(1, N_DEV):
            acc = acc + buf[s]
        o_ref[...] = acc

    return pl.pallas_call(
        body, out_shape=SDS(pack.shape, F32), in_specs=[WHOLE], out_specs=WHOLE,
        scratch_shapes=[pltpu.VMEM((N_DEV, r, LANES), F32), pltpu.SemaphoreType.DMA((7,)), pltpu.SemaphoreType.DMA((7,))],
        compiler_params=pltpu.CompilerParams(vmem_limit_bytes=VMEM_LIMIT), name=name)(pack)


def _cast_rows(w, name):
    def body(w_ref, o_ref):
        o_ref[...] = w_ref[...].astype(BF16)

    return _call(body, name=name, args=[w], out_shape=SDS(w.shape, BF16), in_specs=[WHOLE], out_specs=WHOLE)


def _cast_transposed(w, name):
    d, n = w.shape
    td = 512

    def body(w_ref, o_ref):
        o_ref[...] = w_ref[...].T.astype(BF16)

    return _call(body, name=name, args=[w], out_shape=SDS((n, d), BF16), grid=(d // td,),
                 in_specs=[pl.BlockSpec((td, n), lambda i: (i, 0))], out_specs=pl.BlockSpec((n, td), lambda i: (0, i)))


ROW_TILE = 256


def _rmsnorm_fwd(h, gain, name):
    t, d = h.shape

    def body(h_ref, g_ref, u_ref):
        x = h_ref[...]
        u_ref[...] = (x * lax.rsqrt(jnp.mean(x * x, axis=-1, keepdims=True) + NORM_EPS) * g_ref[...]).astype(BF16)

    row = pl.BlockSpec((ROW_TILE, d), lambda i: (i, 0))
    return _call(body, name=name, args=[h, gain], out_shape=SDS((t, d), BF16), grid=(t // ROW_TILE,),
                 in_specs=[row, pl.BlockSpec((1, d), lambda i: (0, 0))], out_specs=row)


def _rms_bwd_math(x, gain, dy):
    rstd = lax.rsqrt(jnp.mean(x * x, axis=-1, keepdims=True) + NORM_EPS)
    xhat = x * rstd
    dxh = dy * gain
    dx = rstd * (dxh - xhat * jnp.mean(dxh * xhat, axis=-1, keepdims=True))
    return dx, jnp.sum(dy * xhat, axis=0, keepdims=True)


def _rmsnorm_bwd(du, h, gain, resid, bf_scale, name):
    t, d = h.shape

    def body(du_ref, h_ref, g_ref, r_ref, dh_ref, dhb_ref, dg_ref):
        @pl.when(pl.program_id(0) == 0)
        def _():
            dg_ref[...] = jnp.zeros_like(dg_ref)

        dx, dg = _rms_bwd_math(h_ref[...], g_ref[...], du_ref[...])
        dh = r_ref[...] + dx
        dh_ref[...] = dh
        dhb_ref[...] = (bf_scale * dh).astype(BF16)
        dg_ref[...] += dg

    row = pl.BlockSpec((ROW_TILE, d), lambda i: (i, 0))
    vec = pl.BlockSpec((1, d), lambda i: (0, 0))
    return _call(body, name=name, args=[du, h, gain, resid],
                 out_shape=(SDS((t, d), F32), SDS((t, d), BF16), SDS((1, d), F32)), grid=(t // ROW_TILE,),
                 in_specs=[row, row, vec, row], out_specs=(row, row, vec))


def _final_loss(h, gain, target, name):
    t, d = h.shape

    def body(h_ref, g_ref, t_ref, dh_ref, dhb_ref, loss_ref, dg_ref):
        @pl.when(pl.program_id(0) == 0)
        def _():
            dg_ref[...] = jnp.zeros_like(dg_ref)
            loss_ref[...] = jnp.zeros_like(loss_ref)

        x = h_ref[...]
        gain = g_ref[...]
        out = x * lax.rsqrt(jnp.mean(x * x, axis=-1, keepdims=True) + NORM_EPS) * gain
        err = out - t_ref[...]
        loss_ref[...] += 0.5 * jnp.sum(jnp.mean(err * err, axis=-1, keepdims=True), axis=0, keepdims=True)
        dx, dg = _rms_bwd_math(x, gain, err * (1.0 / d))
        dh_ref[...] = dx
        dhb_ref[...] = (0.5 * dx).astype(BF16)
        dg_ref[...] += dg

    row = pl.BlockSpec((ROW_TILE, d), lambda i: (i, 0))
    vec = pl.BlockSpec((1, d), lambda i: (0, 0))
    one = pl.BlockSpec((SUBLANES, LANES), lambda i: (0, 0))
    return _call(body, name=name, args=[h, gain, target],
                 out_shape=(SDS((t, d), F32), SDS((t, d), BF16), SDS((SUBLANES, LANES), F32), SDS((1, d), F32)),
                 grid=(t // ROW_TILE,), in_specs=[row, vec, row], out_specs=(row, row, one, vec))


def _mixnorm_fwd(ya, yb, ga, gb, name):
    t, c = ya.shape

    def body(ya_ref, yb_ref, ga_ref, gb_ref, y_ref, yt_ref):
        for k, (src, g_ref) in enumerate(((ya_ref, ga_ref), (yb_ref, gb_ref))):
            x = src[...]
            u = x * lax.rsqrt(jnp.mean(x * x, axis=-1, keepdims=True) + NORM_EPS) * g_ref[...]
            y_ref[:, k * c:(k + 1) * c] = u.astype(BF16)
            yt_ref[k * c:(k + 1) * c, :] = u.T.astype(BF16)

    row = pl.BlockSpec((ROW_TILE, c), lambda i: (i, 0))
    vec = pl.BlockSpec((1, c), lambda i: (0, 0))
    return _call(body, name=name, args=[ya, yb, ga, gb],
                 out_shape=(SDS((t, 2 * c), BF16), SDS((2 * c, t), BF16)), grid=(t // ROW_TILE,),
                 in_specs=[row, row, vec, vec],
                 out_specs=(pl.BlockSpec((ROW_TILE, 2 * c), lambda i: (i, 0)),
                            pl.BlockSpec((2 * c, ROW_TILE), lambda i: (0, i))))


def _mixnorm_bwd(dy, ya, yb, ga, gb, name):
    t, c = ya.shape

    def body(dy_ref, ya_ref, yb_ref, ga_ref, gb_ref, dya_ref, dyb_ref, dga_ref, dgb_ref):
        @pl.when(pl.program_id(0) == 0)
        def _():
            dga_ref[...] = jnp.zeros_like(dga_ref)
            dgb_ref[...] = jnp.zeros_like(dgb_ref)

        dxa, dga = _rms_bwd_math(ya_ref[...], ga_ref[...], dy_ref[:, :c])
        dxb, dgb = _rms_bwd_math(yb_ref[...], gb_ref[...], dy_ref[:, c:])
        dya_ref[...] = dxa
        dyb_ref[...] = dxb
        dga_ref[...] += dga
        dgb_ref[...] += dgb

    row = pl.BlockSpec((ROW_TILE, c), lambda i: (i, 0))
    vec = pl.BlockSpec((1, c), lambda i: (0, 0))
    return _call(body, name=name, args=[dy, ya, yb, ga, gb],
                 out_shape=(SDS((t, c), F32), SDS((t, c), F32), SDS((1, c), F32), SDS((1, c), F32)),
                 grid=(t // ROW_TILE,),
                 in_specs=[pl.BlockSpec((ROW_TILE, 2 * c), lambda i: (i, 0)), row, row, vec, vec],
                 out_specs=(row, row, vec, vec))


def _tile(n, want):
    return max(t for t in range(LANES, min(n, want) + 1, LANES) if n % t == 0)


def _mm(a, b, *, nt, out_dtype, tm, tn, name, residual=None, job=None):
    m, k = a.shape
    n = b.shape[0] if nt else b.shape[1]
    tm, tn = _tile(m, tm), _tile(n, tn)

    def body(a_ref, b_ref, *rest):
        o_ref = rest[-1]
        av, bv = a_ref[...].astype(BF16), b_ref[...].astype(BF16)
        if nt:
            out = lax.dot_general(av, bv, NT, preferred_element_type=F32)
        else:
            out = jnp.dot(av, bv, preferred_element_type=F32)
        if residual is not None:
            out = rest[0][...] + out
        o_ref[...] = out.astype(out_dtype)

    in_specs = [pl.BlockSpec((tm, k), lambda i, j: (i, 0)),
                pl.BlockSpec((tn, k), lambda i, j: (j, 0)) if nt else pl.BlockSpec((k, tn), lambda i, j: (0, j))]
    args = [a, b]
    if residual is not None:
        in_specs.append(pl.BlockSpec((tm, tn), lambda i, j: (i, j)))
        args.append(residual)
    return _call(body, name=name, args=args, out_shape=SDS((m, n), out_dtype), grid=(m // tm, n // tn),
                 in_specs=in_specs, out_specs=pl.BlockSpec((tm, tn), lambda i, j: (i, j)), job=job)


FFN_TM = 512
FFN_HB = 512


def _ffn_fwd(h, u, w_in_t, w_out, name, job=None):
    t, d = h.shape
    f = w_out.shape[0]
    nk = f // FFN_HB

    def body(u_ref, w_ref, wo_ref, h_ref, hn_ref, g_ref, up_ref, acc):
        k = pl.program_id(1)

        @pl.when(k == 0)
        def _():
            acc[...] = jnp.zeros_like(acc)

        uu = u_ref[...]
        g = lax.dot_general(uu, w_ref[0], NT, preferred_element_type=F32)
        up = lax.dot_general(uu, w_ref[1], NT, preferred_element_type=F32)
        g_ref[...] = g
        up_ref[...] = up
        hid = (g * _sigmoid(g)) * up
        acc[...] += jnp.dot(hid.astype(BF16), wo_ref[...], preferred_element_type=F32)

        @pl.when(k == nk - 1)
        def _():
            hn_ref[...] = h_ref[...] + 0.5 * acc[...]

    tok = pl.BlockSpec((FFN_TM, d), lambda i, k: (i, 0))
    pre = pl.BlockSpec((FFN_TM, FFN_HB), lambda i, k: (i, k))
    return _call(body, name=name, args=[u, w_in_t.reshape(2, f, d), w_out, h],
                 out_shape=(SDS((t, d), F32), SDS((t, f), F32), SDS((t, f), F32)), grid=(t // FFN_TM, nk),
                 in_specs=[tok, pl.BlockSpec((2, FFN_HB, d), lambda i, k: (0, k, 0)),
                           pl.BlockSpec((FFN_HB, d), lambda i, k: (k, 0)), tok],
                 out_specs=(tok, pre, pre), scratch_shapes=[pltpu.VMEM((FFN_TM, d), F32)], job=job)


def _ffn_bwd(dfb, gpre, upre, w_in_t, w_out, name, job=None):
    t, d = dfb.shape
    f = w_out.shape[0]
    nk = f // FFN_HB

    def body(df_ref, g_ref, up_ref, w_ref, wo_ref, du_ref, hid_t_ref, da_t_ref, acc):
        k = pl.program_id(1)

        @pl.when(k == 0)
        def _():
            acc[...] = jnp.zeros_like(acc)

        dhid = lax.dot_general(df_ref[...], wo_ref[...], NT, preferred_element_type=F32)
        g, up = g_ref[...], up_ref[...]
        sig = _sigmoid(g)
        silu = g * sig
        dup = dhid * silu
        dg = dhid * up * (sig * (1.0 + g * (1.0 - sig)))
        hid_t_ref[...] = (silu * up).T.astype(BF16)
        da_t_ref[0] = dg.T.astype(BF16)
        da_t_ref[1] = dup.T.astype(BF16)
        acc[...] += (jnp.dot(dg.astype(BF16), w_ref[0], preferred_element_type=F32)
                     + jnp.dot(dup.astype(BF16), w_ref[1], preferred_element_type=F32))

        @pl.when(k == nk - 1)
        def _():
            du_ref[...] = acc[...]

    tok = pl.BlockSpec((FFN_TM, d), lambda i, k: (i, 0))
    pre = pl.BlockSpec((FFN_TM, FFN_HB), lambda i, k: (i, k))
    return _call(body, name=name, args=[dfb, gpre, upre, w_in_t.reshape(2, f, d), w_out],
                 out_shape=(SDS((t, d), F32), SDS((f, t), BF16), SDS((2, f, t), BF16)), grid=(t // FFN_TM, nk),
                 in_specs=[tok, pre, pre, pl.BlockSpec((2, FFN_HB, d), lambda i, k: (0, k, 0)),
                           pl.BlockSpec((FFN_HB, d), lambda i, k: (k, 0))],
                 out_specs=(tok, pl.BlockSpec((FFN_HB, FFN_TM), lambda i, k: (k, i)),
                            pl.BlockSpec((2, FFN_HB, FFN_TM), lambda i, k: (0, k, i))),
                 scratch_shapes=[pltpu.VMEM((FFN_TM, d), F32)], job=job)


CH = LANES
PAD = SUBLANES


def _lru_gates(xc, gw_ref, gb_ref, lam_ref, z):
    xcb = xc.astype(BF16)
    r = _sigmoid(jnp.dot(xcb, gw_ref[2 * z], preferred_element_type=F32) + gb_ref[pl.ds(2 * z, 1), :])
    i = _sigmoid(jnp.dot(xcb, gw_ref[2 * z + 1], preferred_element_type=F32) + gb_ref[pl.ds(2 * z + 1, 1), :])
    sp = _softplus(-lam_ref[pl.ds(z, 1), :])
    log_a = (-RG_C * r) * sp
    a = jnp.exp(log_a)
    mult = jnp.sqrt(-_expm1(2.0 * log_a))
    return r, i, sp, a, mult


def _conv(xpad, cw_ref, cb_ref, t):
    xc = cb_ref[...] + cw_ref[pl.ds(0, 1), :] * xpad[pl.ds(PAD - 2, t), :]
    for j in range(1, CONV_WIDTH):
        xc = xc + cw_ref[pl.ds(j, 1), :] * xpad[pl.ds(PAD - 2 + j, t), :]
    return xc


def _fill_padded(pad_ref, value, t):
    pad_ref[pl.ds(0, PAD), :] = jnp.zeros((PAD, CH), F32)
    pad_ref[pl.ds(PAD + t, PAD), :] = jnp.zeros((PAD, CH), F32)
    pad_ref[pl.ds(PAD, t), :] = value


def _scan_pair(t, a_up, b_up, out_up, a_down, b_down, out_down):
    def step(tt, carry):
        hu, hd = carry
        lo = pl.multiple_of(tt * SUBLANES, SUBLANES)
        hi = pl.multiple_of(t - SUBLANES - tt * SUBLANES, SUBLANES)
        for j in range(SUBLANES):
            su, sd = pl.ds(lo + j, 1), pl.ds(hi + SUBLANES - 1 - j, 1)
            hu = a_up(su) * hu + b_up(su)
            out_up[su, :] = hu
            hd = a_down(sd) * hd + b_down(sd)
            out_down[sd, :] = hd
        return hu, hd

    zero = jnp.zeros((1, CH), F32)
    lax.fori_loop(0, t // SUBLANES, step, (zero, zero))


def _lru_fwd(proj, cw, cb, gw, gb, lam, name, job=None):
    t = proj.shape[0]
    c = cw.shape[1]
    ncb = c // CH

    def body(x_ref, g_ref, cw_ref, cb_ref, gw_ref, gb_ref, lam_ref, ya_ref, hf_ref, hb_ref, xpad, a0, b0, a1, b1):
        _fill_padded(xpad, x_ref[...], t)
        xc = _conv(xpad, cw_ref, cb_ref, t)
        for z, (a_s, b_s) in enumerate(((a0, b0), (a1, b1))):
            _, i, _, a, mult = _lru_gates(xc, gw_ref, gb_ref, lam_ref, z)
            a_s[...] = a
            b_s[...] = mult * (i * xc)
        _scan_pair(t, lambda s: a0[s, :], lambda s: b0[s, :], hf_ref, lambda s: a1[s, :], lambda s: b1[s, :], hb_ref)
        gelu, _ = _gelu_parts(g_ref[...])
        ya_ref[...] = gelu * (hf_ref[...] + hb_ref[...])

    col = lambda off: pl.BlockSpec((t, CH), lambda i: (0, off + i))
    small = lambda rows: pl.BlockSpec((rows, CH), lambda i: (0, i))
    return _call(body, name=name, args=[proj, proj, cw, cb, gw, gb, lam], out_shape=(SDS((t, c), F32),) * 3,
                 grid=(ncb,),
                 in_specs=[col(0), col(ncb), small(CONV_WIDTH), small(1),
                           pl.BlockSpec((4, None, CH, CH), lambda i: (0, i, 0, 0)), small(4), small(2)],
                 out_specs=(col(0),) * 3,
                 scratch_shapes=[pltpu.VMEM((t + 2 * PAD, CH), F32)] + [pltpu.VMEM((t, CH), F32)] * 4, job=job)


def _lru_bwd(proj, cw, cb, gw, gb, lam, hf, hb, dya, name, job=None):
    t = proj.shape[0]
    c = cw.shape[1]
    ncb = c // CH

    def body(x_ref, g_ref, cw_ref, cb_ref, gw_ref, gb_ref, lam_ref, hf_ref, hb_ref, dya_ref,
             dx_ref, dg_ref, dt_ref, dcw_ref, dcb_ref, dgw_ref, dgb_ref, dlam_ref,
             xpad, hpad, dxc, a0, a1, dhs, dh0, dh1):
        _fill_padded(xpad, x_ref[...], t)
        xc = _conv(xpad, cw_ref, cb_ref, t)
        xcb = xc.astype(BF16)
        gates = [_lru_gates(xc, gw_ref, gb_ref, lam_ref, z) for z in range(2)]
        a0[...] = gates[0][3]
        a1[...] = gates[1][3]

        gelu, dgelu = _gelu_parts(g_ref[...])
        dya = dya_ref[...]
        dgate = dya * (hf_ref[...] + hb_ref[...]) * dgelu
        dg_ref[...] = dgate.astype(BF16)
        dt_ref[1] = dgate.T.astype(BF16)
        dhs[...] = dya * gelu

        def step(tt, carry):
            c0, p0, c1, p1 = carry
            lo = pl.multiple_of(tt * SUBLANES, SUBLANES)
            hi = pl.multiple_of(t - SUBLANES - tt * SUBLANES, SUBLANES)
            for j in range(SUBLANES):
                su, sd = pl.ds(lo + j, 1), pl.ds(hi + SUBLANES - 1 - j, 1)
                c0 = dhs[sd, :] + p0 * c0
                dh0[sd, :] = c0
                p0 = a0[sd, :]
                c1 = dhs[su, :] + p1 * c1
                dh1[su, :] = c1
                p1 = a1[su, :]
            return c0, p0, c1, p1

        zero = jnp.zeros((1, CH), F32)
        lax.fori_loop(0, t // SUBLANES, step, (zero, zero, zero, zero))

        acc_dxc = jnp.zeros((t, CH), F32)
        for z, (h_ref, dh_ref, shift) in enumerate(((hf_ref, dh0, -1), (hb_ref, dh1, 1))):
            r, i, sp, a, mult = gates[z]
            _fill_padded(hpad, h_ref[...], t)
            h_nb = hpad[pl.ds(PAD + shift, t), :]
            db = dh_ref[...]
            da = db * h_nb
            d_i = db * mult * xc
            acc_dxc = acc_dxc + db * mult * i
            d_mult = db * i * xc
            d_la = da * a - d_mult * (a * a) / mult
            d_r = d_la * (-RG_C * sp)
            dlam_ref[pl.ds(z, 1), :] = (jnp.sum(d_la * (-RG_C * r), axis=0, keepdims=True)
                                        * (-_sigmoid(-lam_ref[pl.ds(z, 1), :])))
            for gate, d_pre in ((0, d_r * r * (1.0 - r)), (1, d_i * i * (1.0 - i))):
                zg = 2 * z + gate
                dgb_ref[pl.ds(zg, 1), :] = jnp.sum(d_pre, axis=0, keepdims=True)
                d_pre_b = d_pre.astype(BF16)
                dgw_ref[zg] = lax.dot_general(xcb, d_pre_b, TN, preferred_element_type=F32)
                acc_dxc = acc_dxc + lax.dot_general(d_pre_b, gw_ref[zg], NT, preferred_element_type=F32)

        dcb_ref[...] = jnp.sum(acc_dxc, axis=0, keepdims=True)
        for j in range(CONV_WIDTH):
            dcw_ref[pl.ds(j, 1), :] = jnp.sum(acc_dxc * xpad[pl.ds(PAD - 2 + j, t), :], axis=0, keepdims=True)
        _fill_padded(dxc, acc_dxc, t)
        dx = cw_ref[pl.ds(0, 1), :] * dxc[pl.ds(PAD + 2, t), :]
        for j in range(1, CONV_WIDTH):
            dx = dx + cw_ref[pl.ds(j, 1), :] * dxc[pl.ds(PAD + 2 - j, t), :]
        dx_ref[...] = dx.astype(BF16)
        dt_ref[0] = dx.T.astype(BF16)

    col = lambda off: pl.BlockSpec((t, CH), lambda i: (0, off + i))
    small = lambda rows: pl.BlockSpec((rows, CH), lambda i: (0, i))
    dense = pl.BlockSpec((4, None, CH, CH), lambda i: (0, i, 0, 0))
    padded = pltpu.VMEM((t + 2 * PAD, CH), F32)
    return _call(
        body, name=name, args=[proj, proj, cw, cb, gw, gb, lam, hf, hb, dya],
        out_shape=(SDS((t, c), BF16), SDS((t, c), BF16), SDS((2, c, t), BF16), SDS((CONV_WIDTH, c), F32),
                   SDS((1, c), F32), SDS((4, ncb, CH, CH), F32), SDS((4, c), F32), SDS((2, c), F32)),
        grid=(ncb,),
        in_specs=[col(0), col(ncb), small(CONV_WIDTH), small(1), dense, small(4), small(2), col(0), col(0), col(0)],
        out_specs=(col(0), col(0), pl.BlockSpec((2, CH, t), lambda i: (0, i, 0)), small(CONV_WIDTH), small(1),
                   dense, small(4), small(2)),
        scratch_shapes=[padded, padded, padded] + [pltpu.VMEM((t, CH), F32)] * 5, job=job)


def _band_start(r, rows):
    return jnp.clip(r - WIN_ROWS // 2, 0, rows - WIN_ROWS)


def _bias_tables(rpb):
    cols = np.arange(GRID_W)
    start = np.clip(cols - WIN_COLS // 2, 0, GRID_W - WIN_COLS)
    valid = (cols[None, :] >= start[:, None]) & (cols[None, :] < start[:, None] + WIN_COLS)
    col_off = np.clip(cols[None, :] - cols[:, None] + WIN_COLS - 1, 0, 2 * WIN_COLS - 2)
    row_off = np.arange(WIN_ROWS)[None, :] - np.arange(WIN_ROWS)[:, None] + WIN_ROWS - 1
    pick_row = jnp.asarray(np.eye(2 * WIN_ROWS - 1, dtype=np.float32)[row_off])
    pick_col = jnp.asarray(np.eye(2 * WIN_COLS - 1, dtype=np.float32)[col_off] * valid[..., None])
    hi = lax.Precision.HIGHEST
    by_row = jnp.einsum("hrc,ajr->hajc", rpb, pick_row, precision=hi)
    table = jnp.einsum("hajc,qkc->haqjk", by_row, pick_col, precision=hi)
    table = jnp.where(jnp.asarray(valid)[None, None, :, None, :], table, NEG)
    return table.reshape(rpb.shape[0], WIN_ROWS, GRID_W, BAND)


def _attn_scores(q_ref, k_ref, bm_ref, hh, r, rows):
    rs = _band_start(r, rows)
    lanes = pl.ds(hh * HEAD_DIM, HEAD_DIM)
    qrows = pl.ds(pl.multiple_of(r * GRID_W, GRID_W), GRID_W)
    band = pl.ds(pl.multiple_of(rs * GRID_W, GRID_W), BAND)
    q = q_ref[qrows, lanes].astype(BF16)
    kb = k_ref[band, lanes].astype(BF16)
    s = lax.dot_general(q, kb, NT, preferred_element_type=F32) * (HEAD_DIM ** -0.5) + bm_ref[hh, r - rs]
    p = jnp.exp(s - jnp.max(s, axis=-1, keepdims=True))
    p = p / jnp.sum(p, axis=-1, keepdims=True)
    return q, kb, p, qrows, band, lanes, r - rs


def _attn_fwd(proj, tables, width, name, job=None):
    t = proj.shape[0]
    rows = t // GRID_W
    npair = width // LANES
    first = (proj.shape[1] - 3 * width) // LANES

    def body(q_ref, k_ref, v_ref, bm_ref, o_ref):
        for hh in range(2):
            def row(r, carry):
                _, _, p, qrows, band, lanes, _ = _attn_scores(q_ref, k_ref, bm_ref, hh, r, rows)
                vb = v_ref[band, lanes].astype(BF16)
                o_ref[qrows, lanes] = jnp.dot(p.astype(BF16), vb, preferred_element_type=F32)
                return carry

            lax.fori_loop(0, rows, row, 0)

    col = lambda off: pl.BlockSpec((t, LANES), lambda i: (0, off + i))
    return _call(body, name=name, args=[proj, proj, proj, tables], out_shape=SDS((t, width), F32), grid=(npair,),
                 in_specs=[col(first), col(first + npair), col(first + 2 * npair),
                           pl.BlockSpec((2, WIN_ROWS, GRID_W, BAND), lambda i: (i, 0, 0, 0))],
                 out_specs=col(0), job=job)


def _attn_bwd(proj, tables, dyb, name, job=None):
    t, width = dyb.shape
    rows = t // GRID_W
    npair = width // LANES
    first = (proj.shape[1] - 3 * width) // LANES

    def body(q_ref, k_ref, v_ref, bm_ref, do_ref, dq_ref, dk_ref, dv_ref, dt_ref, dbm_ref, dq_s, dk_s, dv_s):
        dk_s[...] = jnp.zeros_like(dk_s)
        dv_s[...] = jnp.zeros_like(dv_s)
        dbm_ref[...] = jnp.zeros_like(dbm_ref)
        for hh in range(2):
            def row(r, carry):
                q, kb, p, qrows, band, lanes, case = _attn_scores(q_ref, k_ref, bm_ref, hh, r, rows)
                vb = v_ref[band, lanes].astype(BF16)
                do = do_ref[qrows, lanes].astype(BF16)
                dp = lax.dot_general(do, vb, NT, preferred_element_type=F32)
                ds = p * (dp - jnp.sum(dp * p, axis=-1, keepdims=True))
                dbm_ref[hh, case] += ds
                dsb = (ds * (HEAD_DIM ** -0.5)).astype(BF16)
                dq_s[qrows, lanes] = jnp.dot(dsb, kb, preferred_element_type=F32)
                dk_s[band, lanes] += lax.dot_general(dsb, q, TN, preferred_element_type=F32)
                dv_s[band, lanes] += lax.dot_general(p.astype(BF16), do, TN, preferred_element_type=F32)
                return carry

            lax.fori_loop(0, rows, row, 0)
        for n, (src, dst) in enumerate(((dq_s, dq_ref), (dk_s, dk_ref), (dv_s, dv_ref))):
            val = src[...]
            dst[...] = val.astype(BF16)
            dt_ref[n] = val.T.astype(BF16)

    col = lambda off: pl.BlockSpec((t, LANES), lambda i: (0, off + i))
    table = pl.BlockSpec((2, WIN_ROWS, GRID_W, BAND), lambda i: (i, 0, 0, 0))
    return _call(body, name=name, args=[proj, proj, proj, tables, dyb],
                 out_shape=(SDS((t, width), BF16),) * 3 + (SDS((3, width, t), BF16), SDS(tables.shape, F32)),
                 grid=(npair,),
                 in_specs=[col(first), col(first + npair), col(first + 2 * npair), table, col(0)],
                 out_specs=(col(0), col(0), col(0), pl.BlockSpec((3, LANES, t), lambda i: (0, i, 0)), table),
                 scratch_shapes=[pltpu.VMEM((t, LANES), F32)] * 3, job=job)


def _adamw_math(w, g, m, v):
    m = ADAM_B1 * m + (1.0 - ADAM_B1) * g
    v = ADAM_B2 * v + (1.0 - ADAM_B2) * (g * g)
    m_hat = m / (1.0 - ADAM_B1 ** ADAM_STEP)
    v_hat = v / (1.0 - ADAM_B2 ** ADAM_STEP)
    delta = -ADAM_LR * (m_hat / (jnp.sqrt(v_hat) + ADAM_EPS) + ADAM_WD * w)
    return delta, m, v


def _sum_partials(p_ref):
    g = p_ref[0].astype(F32)
    for s in range(1, N_CHIP):
        g = g + p_ref[s].astype(F32)
    return g


def _adamw_rows(w, partials, m, v, name):
    rb, n = w.shape
    tr = 64

    def body(w_ref, p_ref, m_ref, v_ref, g_ref, d_ref, nm_ref, nv_ref):
        g = _sum_partials(p_ref)
        g_ref[...] = g
        d_ref[...], nm_ref[...], nv_ref[...] = _adamw_math(w_ref[...], g, m_ref[...], v_ref[...])

    blk = pl.BlockSpec((tr, n), lambda i: (i, 0))
    return _call(body, name=name, args=[w, partials.reshape(N_CHIP, rb, n), m, v], out_shape=(SDS((rb, n), F32),) * 4,
                 grid=(rb // tr,), in_specs=[blk, pl.BlockSpec((N_CHIP, tr, n), lambda i: (0, i, 0)), blk, blk],
                 out_specs=(blk,) * 4)


def _adamw_cols(w, partials, m, v, name):
    d, nb = w.shape
    td = 256

    def body(w_ref, p_ref, m_ref, v_ref, g_ref, d_ref, nm_ref, nv_ref):
        g = _sum_partials(p_ref).T
        g_ref[...] = g
        d_ref[...], nm_ref[...], nv_ref[...] = _adamw_math(w_ref[...], g, m_ref[...], v_ref[...])

    blk = pl.BlockSpec((td, nb), lambda i: (i, 0))
    return _call(body, name=name, args=[w, partials.reshape(N_CHIP, nb, d), m, v], out_shape=(SDS((d, nb), F32),) * 4,
                 grid=(d // td,), in_specs=[blk, pl.BlockSpec((N_CHIP, nb, td), lambda i: (0, 0, i)), blk, blk],
                 out_specs=(blk,) * 4)


def _adamw_small(w, g, m, v, name):
    def body(w_ref, g_ref, m_ref, v_ref, d_ref, nm_ref, nv_ref):
        d_ref[...], nm_ref[...], nv_ref[...] = _adamw_math(w_ref[...], g_ref[...], m_ref[...], v_ref[...])

    return _call(body, name=name, args=[w, g, m, v], out_shape=(SDS(w.shape, F32),) * 3, in_specs=[WHOLE] * 4,
                 out_specs=(WHOLE,) * 3)


TILE = SUBLANES * LANES


def _pack(arrays):
    parts = []
    for a in arrays:
        flat = a.reshape(-1).astype(F32)
        flat = jnp.pad(flat, (0, -flat.size % TILE))
        parts.append(flat.reshape(-1, LANES))
    return jnp.concatenate(parts, axis=0)


def _unpack(pack, like):
    out, row = [], 0
    for a in like:
        n = int(np.prod(a.shape))
        nrows = -(-n // TILE) * SUBLANES
        out.append(pack[row:row + nrows].reshape(-1)[:n].reshape(a.shape))
        row += nrows
    return out


def _dense_gate_blocks(gate_w):
    w = gate_w.reshape(4, -1, 2, HEAD_DIM, HEAD_DIM)
    zero = jnp.zeros_like(w[:, :, 0])
    top = jnp.concatenate([w[:, :, 0], zero], axis=-1)
    bottom = jnp.concatenate([zero, w[:, :, 1]], axis=-1)
    return jnp.concatenate([top, bottom], axis=-2)


def _diag_gate_blocks(dense, shape):
    even = dense[:, :, :HEAD_DIM, :HEAD_DIM]
    odd = dense[:, :, HEAD_DIM:, HEAD_DIM:]
    return jnp.stack([even, odd], axis=2).reshape(shape)


LARGE = ("ffn1_w_in", "ffn1_w_out", "w_in_mix", "w_out_mix", "ffn2_w_in", "ffn2_w_out")
COLUMN_SHARDED = ("ffn1_w_in", "w_in_mix", "ffn2_w_in")
SHARDED_SMALL = ("lru_conv_w", "lru_lambda")
REPLICATED = ("norm_ffn1", "norm_mix", "lru_conv_b", "lru_gate_w", "lru_gate_b", "attn_rpb", "lru_out_norm",
              "attn_out_norm", "norm_ffn2", "norm_final")
WEIGHTS = ("norm_ffn1", "ffn1_w_in", "ffn1_w_out", "norm_mix", "w_in_mix", "lru_conv_w", "lru_conv_b", "lru_gate_w",
           "lru_gate_b", "lru_lambda", "attn_rpb", "lru_out_norm", "attn_out_norm", "w_out_mix", "norm_ffn2",
           "ffn2_w_in", "ffn2_w_out", "norm_final")


def _forward_backward(x, target, shards, sharded_small, s):
    c = s["lru_conv_b"].shape[1]
    width = s["attn_out_norm"].shape[1]
    t = x.shape[0]
    w = {}

    def gather(names, extra=()):
        return _gather_job([shards[n] for n in names] + list(extra))

    w["ffn1_w_in"], w["ffn1_w_out"], full_small = _run_job(gather(("ffn1_w_in", "ffn1_w_out"), [sharded_small]),
                                                           "gather_ffn1")
    full_small = full_small.reshape(N_DEV, SUBLANES, c // N_DEV)
    conv_w = full_small[:, :CONV_WIDTH].transpose(1, 0, 2).reshape(CONV_WIDTH, c)
    lam = full_small[:, CONV_WIDTH:CONV_WIDTH + 2].transpose(1, 0, 2).reshape(2, c)
    u1 = _rmsnorm_fwd(x, s["norm_ffn1"], "norm_ffn1")
    (h1, g1, up1), (w["w_in_mix"], w["w_out_mix"]) = _ffn_fwd(
        x, u1, w["ffn1_w_in"], w["ffn1_w_out"], "ffn1_fwd", job=gather(("w_in_mix", "w_out_mix")))
    u2 = _rmsnorm_fwd(h1, s["norm_mix"], "norm_mix")
    proj = _mm(u2, w["w_in_mix"], nt=True, out_dtype=F32, tm=512, tn=512, name="mix_in_proj")
    gw = _dense_gate_blocks(s["lru_gate_w"]).astype(BF16)
    gb = s["lru_gate_b"].reshape(4, c)
    tables, tables_vjp = jax.vjp(_bias_tables, s["attn_rpb"])
    (ya, hf, hb), (w["ffn2_w_out"],) = _lru_fwd(proj, conv_w, s["lru_conv_b"], gw, gb, lam, "lru_fwd",
                                                job=gather(("ffn2_w_out",)))
    yb, (w["ffn2_w_in"],) = _attn_fwd(proj, tables, width, "attn_fwd", job=gather(("ffn2_w_in",)))
    y, yt = _mixnorm_fwd(ya, yb, s["lru_out_norm"], s["attn_out_norm"], "mix_norm")
    h2 = _mm(y, w["w_out_mix"], nt=False, out_dtype=F32, tm=512, tn=512, name="mix_out_proj", residual=h1)
    u3 = _rmsnorm_fwd(h2, s["norm_ffn2"], "norm_ffn2")
    h3, g2, up2 = _ffn_fwd(h2, u3, w["ffn2_w_in"], w["ffn2_w_out"], "ffn2_fwd")
    dh3, df2, loss_part, d_norm_final = _final_loss(h3, s["norm_final"], target, "final_loss")

    grads, from_sibling, partials = {}, {}, {}

    def to_sibling(n):
        return _sibling_job([grads[n]])

    def chip_sum(n):
        return _pair_sum(grads[n], from_sibling[n], "pair_sum_" + n)

    du3, hid2_t, da2_t = _ffn_bwd(df2, g2, up2, w["ffn2_w_in"], w["ffn2_w_out"], "ffn2_bwd")
    f = hid2_t.shape[0]
    grads["ffn2_w_out"] = _mm(hid2_t, df2, nt=False, out_dtype=BF16, tm=512, tn=1024, name="ffn2_out_grad")
    grads["ffn2_w_in"], (from_sibling["ffn2_w_out"],) = _mm(
        da2_t.reshape(2 * f, t), u3, nt=False, out_dtype=BF16, tm=512, tn=1024, name="ffn2_in_grad",
        job=to_sibling("ffn2_w_out"))
    dh2, dh2b, d_norm_ffn2 = _rmsnorm_bwd(du3, h2, s["norm_ffn2"], dh3, 1.0, "norm_ffn2_bwd")
    summed_ffn2_out = chip_sum("ffn2_w_out")
    grads["w_out_mix"], (from_sibling["ffn2_w_in"],) = _mm(
        yt, dh2b, nt=False, out_dtype=BF16, tm=512, tn=1024, name="mix_out_grad", job=to_sibling("ffn2_w_in"))
    summed_ffn2_in = chip_sum("ffn2_w_in")
    dy = _mm(dh2b, w["w_out_mix"], nt=True, out_dtype=F32, tm=512, tn=512, name="mix_out_bwd")
    dya, dyb, d_lru_out_norm, d_attn_out_norm = _mixnorm_bwd(dy, ya, yb, s["lru_out_norm"], s["attn_out_norm"],
                                                             "mix_norm_bwd")
    (dq, dk, dv, dqkv_t, d_tables), (partials["ffn2_w_out"], partials["ffn2_w_in"]) = _attn_bwd(
        proj, tables, dyb, "attn_bwd", job=_chips_job([summed_ffn2_out, summed_ffn2_in]))
    (dx_lru, dg_lru, dxg_t, d_conv_w, d_conv_b, d_gw, d_gb, d_lam), (from_sibling["w_out_mix"],) = _lru_bwd(
        proj, conv_w, s["lru_conv_b"], gw, gb, lam, hf, hb, dya, "lru_bwd", job=to_sibling("w_out_mix"))
    dproj = jnp.concatenate([dx_lru, dg_lru, dq, dk, dv], axis=1)
    dproj_t = jnp.concatenate([dxg_t.reshape(2 * c, t), dqkv_t.reshape(3 * width, t)], axis=0)
    grads["w_in_mix"] = _mm(dproj_t, u2, nt=False, out_dtype=BF16, tm=512, tn=1024, name="mix_in_grad")
    du2, (from_sibling["w_in_mix"],) = _mm(dproj, w["w_in_mix"], nt=False, out_dtype=F32, tm=512, tn=512,
                                           name="mix_in_bwd", job=to_sibling("w_in_mix"))
    dh1, df1, d_norm_mix = _rmsnorm_bwd(du2, h1, s["norm_mix"], dh2, 0.5, "norm_mix_bwd")
    (du1, hid1_t, da1_t), (partials["w_out_mix"], partials["w_in_mix"]) = _ffn_bwd(
        df1, g1, up1, w["ffn1_w_in"], w["ffn1_w_out"], "ffn1_bwd",
        job=_chips_job([chip_sum("w_out_mix"), chip_sum("w_in_mix")]))
    grads["ffn1_w_out"] = _mm(hid1_t, df1, nt=False, out_dtype=BF16, tm=512, tn=1024, name="ffn1_out_grad")
    grads["ffn1_w_in"], (from_sibling["ffn1_w_out"],) = _mm(
        da1_t.reshape(2 * f, t), u1, nt=False, out_dtype=BF16, tm=512, tn=1024, name="ffn1_in_grad",
        job=to_sibling("ffn1_w_out"))
    grad_x, _, d_norm_ffn1 = _rmsnorm_bwd(du1, x, s["norm_ffn1"], dh1, 1.0, "norm_ffn1_bwd")
    (from_sibling["ffn1_w_in"],) = _run_job(to_sibling("ffn1_w_in"), "to_sibling_ffn1_in")
    partials["ffn1_w_out"], partials["ffn1_w_in"] = _run_job(
        _chips_job([chip_sum("ffn1_w_out"), chip_sum("ffn1_w_in")]), "to_chips_ffn1")

    small = {
        "norm_ffn1": d_norm_ffn1, "norm_mix": d_norm_mix, "lru_conv_w": d_conv_w, "lru_conv_b": d_conv_b,
        "lru_gate_w": _diag_gate_blocks(d_gw, s["lru_gate_w"].shape), "lru_gate_b": d_gb.reshape(s["lru_gate_b"].shape),
        "lru_lambda": d_lam, "attn_rpb": tables_vjp(d_tables)[0], "lru_out_norm": d_lru_out_norm,
        "attn_out_norm": d_attn_out_norm, "norm_ffn2": d_norm_ffn2, "norm_final": d_norm_final,
    }
    return loss_part[0, 0], grad_x, partials, small


def _step(x, loss_target, p, m, v):
    me = 4 * lax.axis_index("x") + 2 * lax.axis_index("y") + lax.axis_index("c")

    shards = {n: (_cast_transposed if n in COLUMN_SHARDED else _cast_rows)(p[n], "cast_" + n) for n in LARGE}
    sharded_small = (jnp.pad(p["lru_conv_w"], ((0, SUBLANES - CONV_WIDTH), (0, 0)))
                     + jnp.pad(p["lru_lambda"], ((CONV_WIDTH, SUBLANES - CONV_WIDTH - 2), (0, 0))))
    s = {n: p[n] if n in ("lru_gate_w", "lru_gate_b", "attn_rpb") else p[n].reshape(1, -1) for n in REPLICATED}

    loss_part, grad_x, partials, small = _forward_backward(x, loss_target, shards, sharded_small, s)
    loss = lax.psum(loss_part, ("x", "y", "c"))

    out = {}
    for n in LARGE:
        update = _adamw_cols if n in COLUMN_SHARDED else _adamw_rows
        out[n] = update(p[n], partials[n], m[n], v[n], "adamw_" + n)

    by_device = lambda a: a.reshape(a.shape[0], N_DEV, -1).transpose(1, 0, 2)
    small_list = [small[n] for n in REPLICATED] + [by_device(small[n]) for n in SHARDED_SMALL]
    reduced = _unpack(_all_reduce_small(_pack(small_list), "all_reduce_small"), small_list)
    g_rep = dict(zip(REPLICATED, reduced[:len(REPLICATED)]))
    g_sh = {n: lax.dynamic_index_in_dim(r, me, axis=0, keepdims=False)
            for n, r in zip(SHARDED_SMALL, reduced[len(REPLICATED):])}
    g_small = {**g_rep, **g_sh}
    names = REPLICATED + SHARDED_SMALL
    like = [p[n] for n in names]
    pack_of = lambda d: _pack([d[n].reshape(p[n].shape) for n in names])
    upd = _adamw_small(pack_of(p), pack_of(g_small), pack_of(m), pack_of(v), "adamw_small")
    for n, d_, m_, v_ in zip(names, *[_unpack(u, like) for u in upd]):
        out[n] = (g_small[n].reshape(p[n].shape), d_, m_, v_)
    return loss, grad_x, out


def kernel(x, norm_ffn1, ffn1_w_in, ffn1_w_out, norm_mix, w_in_mix, lru_conv_w, lru_conv_b, lru_gate_w, lru_gate_b, lru_lambda, attn_rpb, lru_out_norm, attn_out_norm, w_out_mix, norm_ffn2, ffn2_w_in, ffn2_w_out, norm_final, loss_target, m_norm_ffn1, m_ffn1_w_in, m_ffn1_w_out, m_norm_mix, m_w_in_mix, m_lru_conv_w, m_lru_conv_b, m_lru_gate_w, m_lru_gate_b, m_lru_lambda, m_attn_rpb, m_lru_out_norm, m_attn_out_norm, m_w_out_mix, m_norm_ffn2, m_ffn2_w_in, m_ffn2_w_out, m_norm_final, v_norm_ffn1, v_ffn1_w_in, v_ffn1_w_out, v_norm_mix, v_w_in_mix, v_lru_conv_w, v_lru_conv_b, v_lru_gate_w, v_lru_gate_b, v_lru_lambda, v_attn_rpb, v_lru_out_norm, v_attn_out_norm, v_w_out_mix, v_norm_ffn2, v_ffn2_w_in, v_ffn2_w_out, v_norm_final):
    given = dict(locals())
    drop_layer = lambda n, a: a if n == "norm_final" else a[0]
    p = {n: drop_layer(n, given[n]) for n in WEIGHTS}
    m = {n: drop_layer(n, given["m_" + n]) for n in WEIGHTS}
    v = {n: drop_layer(n, given["v_" + n]) for n in WEIGHTS}
    loss, grad_x, out = _step(x[0], loss_target[0], p, m, v)
    shaped = lambda n, a: a.reshape(given[n].shape)
    return (loss, grad_x[None], *[shaped(n, out[n][k]) for k in range(4) for n in WEIGHTS])
```

```python
import math

import numpy as np
import jax
import jax.numpy as jnp
from jax import lax
from jax.experimental import pallas as pl
from jax.experimental.pallas import tpu as pltpu

F32 = jnp.float32
BF16 = jnp.bfloat16
SDS = jax.ShapeDtypeStruct

N_DEV = 8
N_CHIP = 4
NORM_EPS = 1e-6
RG_C = 8.0
CONV_WIDTH = 4
HEAD_DIM = 64
GRID_W = 64
WIN_ROWS = 8
WIN_COLS = 16
BAND = WIN_ROWS * GRID_W
NEG = -1e30

ADAM_LR = 0.001
ADAM_B1 = 0.9
ADAM_B2 = 0.999
ADAM_EPS = 1e-08
ADAM_WD = 0.01
ADAM_STEP = 10

LANES = 128
SUBLANES = 8
VMEM_LIMIT = 56 * 1024 * 1024

NT = (((1,), (1,)), ((), ()))
TN = (((0,), (0,)), ((), ()))
ANY = pl.BlockSpec(memory_space=pl.ANY)
WHOLE = pl.BlockSpec(memory_space=pltpu.VMEM)
MESH = pl.DeviceIdType.MESH


def _sigmoid(x):
    return 1.0 / (1.0 + jnp.exp(-x))


def _gelu_parts(x):
    c = math.sqrt(2.0 / math.pi)
    t = jnp.tanh(c * (x + 0.044715 * (x * x * x)))
    gelu = 0.5 * x * (1.0 + t)
    dgelu = 0.5 * (1.0 + t) + 0.5 * x * (1.0 - t * t) * (c * (1.0 + 3.0 * 0.044715 * (x * x)))
    return gelu, dgelu


def _expm1(x):
    poly = x * (1.0 + x * (1.0 / 2) * (1.0 + x * (1.0 / 3) * (1.0 + x * (1.0 / 4) * (1.0 + x * (1.0 / 5) * (1.0 + x * (1.0 / 6))))))
    return jnp.where(jnp.abs(x) < 0.25, poly, jnp.exp(x) - 1.0)


def _softplus(x):
    return jnp.maximum(x, 0.0) + jnp.log1p(jnp.exp(-jnp.abs(x)))


class _Piece:
    N_REMOTE = {"gather": 7, "to_sibling": N_CHIP, "to_chips": 3}
    N_LOCAL = {"gather": 1, "to_sibling": 0, "to_chips": 1}

    def __init__(self, kind, src, dest, lo, hi):
        self.kind, self.src, self.dest, self.lo, self.hi = kind, src, dest, lo, hi


class _Job:
    def __init__(self, pieces):
        self.pieces = list(pieces)
        self.ins = [p.src for p in self.pieces]
        self.out_shapes = [SDS(p.dest.shape, p.dest.dtype) for p in self.pieces]
        self.aliased = [i for i, p in enumerate(self.pieces) if not isinstance(p.dest, SDS)]
        self.n_remote = sum(_Piece.N_REMOTE[p.kind] for p in self.pieces)
        self.n_local = max(sum(_Piece.N_LOCAL[p.kind] for p in self.pieces), 1)

    def _each(self, step, ins, outs, send_sems, recv_sems, local_sems):
        remote = local = 0
        for p, src, dst in zip(self.pieces, ins, outs):
            _EXCHANGES[p.kind](step, p, src, dst, send_sems, recv_sems, local_sems, remote, local)
            remote += _Piece.N_REMOTE[p.kind]
            local += _Piece.N_LOCAL[p.kind]

    def start(self, *refs):
        self._each("start", *refs)

    def finish(self, *refs):
        self._each("relay", *refs)
        self._each("finish", *refs)


def _call(body, *, name, args, out_shape, in_specs, out_specs, grid=(), scratch_shapes=(), aliases=None, job=None):
    single = not isinstance(out_shape, (tuple, list))
    out_shape = (out_shape,) if single else tuple(out_shape)
    out_specs = (out_specs,) if single else tuple(out_specs)
    aliases = dict(aliases or {})
    params = pltpu.CompilerParams(dimension_semantics=("arbitrary",) * len(grid) if grid else None,
                                  vmem_limit_bytes=VMEM_LIMIT)
    if job is None:
        res = pl.pallas_call(body, out_shape=out_shape, grid=grid, in_specs=list(in_specs), out_specs=out_specs,
                             scratch_shapes=list(scratch_shapes), input_output_aliases=aliases, name=name,
                             compiler_params=params)(*args)
        return res[0] if single else res

    n_in, n_out, n_scr = len(args), len(out_shape), len(scratch_shapes)
    j_in, j_out, j_alias = len(job.ins), len(job.out_shapes), len(job.aliased)

    def hosted(*refs):
        ins, refs = refs[:n_in], refs[n_in:]
        j_ins, refs = refs[:j_in], refs[j_in + j_alias:]
        outs, refs = refs[:n_out], refs[n_out:]
        j_outs, refs = refs[:j_out], refs[j_out:]
        scr, sems = refs[:n_scr], refs[n_scr:]
        if grid:
            first = last = None
            for axis, size in enumerate(grid):
                at_first, at_last = pl.program_id(axis) == 0, pl.program_id(axis) == size - 1
                first = at_first if first is None else first & at_first
                last = at_last if last is None else last & at_last
            pl.when(first)(lambda: job.start(j_ins, j_outs, *sems))
            body(*ins, *outs, *scr)
            pl.when(last)(lambda: job.finish(j_ins, j_outs, *sems))
        else:
            job.start(j_ins, j_outs, *sems)
            body(*ins, *outs, *scr)
            job.finish(j_ins, j_outs, *sems)

    res = pl.pallas_call(
        hosted, out_shape=out_shape + tuple(job.out_shapes), grid=grid,
        in_specs=list(in_specs) + [ANY] * (j_in + j_alias), out_specs=out_specs + (ANY,) * j_out,
        scratch_shapes=list(scratch_shapes) + [pltpu.SemaphoreType.DMA((job.n_remote,)),
                                               pltpu.SemaphoreType.DMA((job.n_remote,)),
                                               pltpu.SemaphoreType.DMA((job.n_local,))],
        input_output_aliases={**aliases, **{n_in + j_in + k: n_out + i for k, i in enumerate(job.aliased)}},
        name=name, compiler_params=params)(*args, *job.ins, *[job.pieces[i].dest for i in job.aliased])
    own, carried = res[:n_out], res[n_out:]
    return (own[0] if single else own), carried


def _run_job(job, name):
    return _call(lambda: None, name=name, args=[], out_shape=(), in_specs=[], out_specs=(), job=job)[1]


def _position():
    return lax.axis_index("x"), lax.axis_index("y"), lax.axis_index("c")


def _flat(px, py, pc):
    return 4 * px + 2 * py + pc


def _gather_exchange(step, p, src, dst, send_sems, recv_sems, local_sems, r0, l0):
    x, y, c = _position()
    me, sibling = (x, y, c), (x, y, 1 - c)
    chips = [(1 - x, y), (x, 1 - y), (1 - x, 1 - y)]
    rb, n_rows = p.src.shape[0], p.hi - p.lo
    mine = src.at[pl.ds(p.lo, n_rows), :]

    def rows(block):
        return dst.at[pl.ds(_flat(*block) * rb + p.lo, n_rows), :]

    def copy(k, block, to, own=False):
        return pltpu.make_async_remote_copy(
            src_ref=mine if own else rows(block), dst_ref=rows(block),
            send_sem=send_sems.at[r0 + k], recv_sem=recv_sems.at[r0 + k], device_id=to, device_id_type=MESH)

    local = pltpu.make_async_copy(mine, rows(me), local_sems.at[l0])
    if step == "start":
        local.start()
        copy(0, me, sibling, own=True).start()
        for j, chip in enumerate(chips):
            copy(1 + j, me, (*chip, c), own=True).start()
    elif step == "relay":
        for j, chip in enumerate(chips):
            copy(1 + j, (*chip, c), me).wait_recv()
            copy(4 + j, (*chip, c), sibling).start()
    else:
        copy(0, sibling, me).wait_recv()
        for j, chip in enumerate(chips):
            copy(4 + j, (*chip, 1 - c), me).wait_recv()
        copy(0, me, sibling, own=True).wait_send()
        for j, chip in enumerate(chips):
            copy(1 + j, me, (*chip, c), own=True).wait_send()
            copy(4 + j, (*chip, c), sibling).wait_send()
        local.wait()


def _sibling_exchange(step, p, src, dst, send_sems, recv_sems, local_sems, r0, l0):
    x, y, c = _position()
    rb, n_rows = p.src.shape[0] // N_DEV, p.hi - p.lo
    for q in range(N_CHIP):
        copy = pltpu.make_async_remote_copy(
            src_ref=src.at[pl.ds((2 * q + 1 - c) * rb + p.lo, n_rows), :],
            dst_ref=dst.at[pl.ds(q * rb + p.lo, n_rows), :],
            send_sem=send_sems.at[r0 + q], recv_sem=recv_sems.at[r0 + q], device_id=(x, y, 1 - c), device_id_type=MESH)
        if step == "start":
            copy.start()
        elif step == "finish":
            copy.wait()


CHIP_FLIPS = [(1, 0), (0, 1), (1, 1)]


def _chips_exchange(step, p, src, dst, send_sems, recv_sems, local_sems, r0, l0):
    x, y, c = _position()
    rb, n_rows = p.src.shape[0] // N_CHIP, p.hi - p.lo

    def slot(ref, px, py):
        return ref.at[pl.ds((2 * px + py) * rb + p.lo, n_rows), :]

    def copy(k, landing=False):
        px = 1 - x if CHIP_FLIPS[k][0] else x
        py = 1 - y if CHIP_FLIPS[k][1] else y
        return pltpu.make_async_remote_copy(
            src_ref=slot(dst, px, py) if landing else slot(src, px, py),
            dst_ref=slot(dst, px, py) if landing else slot(dst, x, y),
            send_sem=send_sems.at[r0 + k], recv_sem=recv_sems.at[r0 + k], device_id=(px, py, c), device_id_type=MESH)

    local = pltpu.make_async_copy(slot(src, x, y), slot(dst, x, y), local_sems.at[l0])
    if step == "start":
        local.start()
        for k in range(3):
            copy(k).start()
    elif step == "finish":
        for k in range(3):
            copy(k, landing=True).wait_recv()
        for k in range(3):
            copy(k).wait_send()
        local.wait()


_EXCHANGES = {"gather": _gather_exchange, "to_sibling": _sibling_exchange, "to_chips": _chips_exchange}


def _gathered(shard):
    return SDS((N_DEV * shard.shape[0], shard.shape[1]), shard.dtype)


def _split(rows, parts):
    cuts = [rows * k // parts // 16 * 16 for k in range(parts)] + [rows]
    return list(zip(cuts[:-1], cuts[1:]))


def _pair_sum(g, from_sibling, name):
    rb, n = g.shape[0] // N_DEV, g.shape[1]
    tr = rb if rb * n * 2 <= 3 * 1024 * 1024 else rb // 2
    core = lax.axis_index("c").astype(jnp.int32).reshape(1)

    def body(c_ref, g_ref, r_ref, o_ref):
        o_ref[...] = (g_ref[...].astype(F32) + r_ref[...].astype(F32)).astype(BF16)

    grid_spec = pltpu.PrefetchScalarGridSpec(
        num_scalar_prefetch=1, grid=(N_CHIP, rb // tr),
        in_specs=[pl.BlockSpec((None, None, tr, n), lambda q, i, c_ref: (q, c_ref[0], i, 0)),
                  pl.BlockSpec((None, tr, n), lambda q, i, c_ref: (q, i, 0))],
        out_specs=pl.BlockSpec((None, tr, n), lambda q, i, c_ref: (q, i, 0)))
    out = pl.pallas_call(
        body, grid_spec=grid_spec, out_shape=SDS((N_CHIP, rb, n), BF16), name=name,
        compiler_params=pltpu.CompilerParams(dimension_semantics=("arbitrary",) * 2, vmem_limit_bytes=VMEM_LIMIT))(
            core, g.reshape(N_CHIP, 2, rb, n), from_sibling.reshape(N_CHIP, rb, n))
    return out.reshape(N_CHIP * rb, n)


PEER_FLIPS = [(0, 0, 1), (1, 0, 0), (0, 1, 0), (1, 1, 0), (1, 0, 1), (0, 1, 1), (1, 1, 1)]


def _flip(pos, flips):
    return tuple((1 - p) if f else p for p, f in zip(pos, flips))


def _all_reduce_small(pack, name):
    r = pack.shape[0]

    def body(p_ref, o_ref, buf, send_sems, recv_sems):
        me = _position()
        my_flat = _flat(*me)
        buf[my_flat] = p_ref[...]
        sent = []
        for k in range(7):
            peer = _flip(me, PEER_FLIPS[k])
            cp = pltpu.make_async_remote_copy(
                src_ref=p_ref, dst_ref=buf.at[my_flat], send_sem=send_sems.at[k], recv_sem=recv_sems.at[k],
                device_id=peer, device_id_type=MESH)
            cp.start()
            sent.append(cp)
        for k in range(7):
            peer = _flip(me, PEER_FLIPS[k])
            slot = buf.at[_flat(*peer)]
            pltpu.make_async_remote_copy(
                src_ref=slot, dst_ref=slot, send_sem=send_sems.at[k], recv_sem=recv_sems.at[k],
                device_id=peer, device_id_type=MESH).wait_recv()
        for cp in sent:
            cp.wait_send()
        acc = buf[0]
        for s in range(1, N_DEV):
            acc = acc + buf[s]
        o_ref[...] = acc

    return pl.pallas_call(
        body, out_shape=SDS(pack.shape, F32), in_specs=[WHOLE], out_specs=WHOLE,
        scratch_shapes=[pltpu.VMEM((N_DEV, r, LANES), F32), pltpu.SemaphoreType.DMA((7,)), pltpu.SemaphoreType.DMA((7,))],
        compiler_params=pltpu.CompilerParams(vmem_limit_bytes=VMEM_LIMIT), name=name)(pack)


def _cast_rows(w, name):
    def body(w_ref, o_ref):
        o_ref[...] = w_ref[...].astype(BF16)

    return _call(body, name=name, args=[w], out_shape=SDS(w.shape, BF16), in_specs=[WHOLE], out_specs=WHOLE)


def _cast_transposed(w, name):
    d, n = w.shape
    td = 512

    def body(w_ref, o_ref):
        o_ref[...] = w_ref[...].T.astype(BF16)

    return _call(body, name=name, args=[w], out_shape=SDS((n, d), BF16), grid=(d // td,),
                 in_specs=[pl.BlockSpec((td, n), lambda i: (i, 0))], out_specs=pl.BlockSpec((n, td), lambda i: (0, i)))


ROW_TILE = 256


def _rmsnorm_fwd(h, gain, name):
    t, d = h.shape

    def body(h_ref, g_ref, u_ref):
        x = h_ref[...]
        u_ref[...] = (x * lax.rsqrt(jnp.mean(x * x, axis=-1, keepdims=True) + NORM_EPS) * g_ref[...]).astype(BF16)

    row = pl.BlockSpec((ROW_TILE, d), lambda i: (i, 0))
    return _call(body, name=name, args=[h, gain], out_shape=SDS((t, d), BF16), grid=(t // ROW_TILE,),
                 in_specs=[row, pl.BlockSpec((1, d), lambda i: (0, 0))], out_specs=row)


def _rms_bwd_math(x, gain, dy):
    rstd = lax.rsqrt(jnp.mean(x * x, axis=-1, keepdims=True) + NORM_EPS)
    xhat = x * rstd
    dxh = dy * gain
    dx = rstd * (dxh - xhat * jnp.mean(dxh * xhat, axis=-1, keepdims=True))
    return dx, jnp.sum(dy * xhat, axis=0, keepdims=True)


def _rmsnorm_bwd(du, h, gain, resid, bf_scale, name, job=None):
    t, d = h.shape

    def body(du_ref, h_ref, g_ref, r_ref, dh_ref, dhb_ref, dg_ref):
        @pl.when(pl.program_id(0) == 0)
        def _():
            dg_ref[...] = jnp.zeros_like(dg_ref)

        dx, dg = _rms_bwd_math(h_ref[...], g_ref[...], du_ref[...])
        dh = r_ref[...] + dx
        dh_ref[...] = dh
        dhb_ref[...] = (bf_scale * dh).astype(BF16)
        dg_ref[...] += dg

    row = pl.BlockSpec((ROW_TILE, d), lambda i: (i, 0))
    vec = pl.BlockSpec((1, d), lambda i: (0, 0))
    return _call(body, name=name, args=[du, h, gain, resid],
                 out_shape=(SDS((t, d), F32), SDS((t, d), BF16), SDS((1, d), F32)), grid=(t // ROW_TILE,),
                 in_specs=[row, row, vec, row], out_specs=(row, row, vec), job=job)


def _final_loss(h, gain, target, name):
    t, d = h.shape

    def body(h_ref, g_ref, t_ref, dh_ref, dhb_ref, loss_ref, dg_ref):
        @pl.when(pl.program_id(0) == 0)
        def _():
            dg_ref[...] = jnp.zeros_like(dg_ref)
            loss_ref[...] = jnp.zeros_like(loss_ref)

        x = h_ref[...]
        gain = g_ref[...]
        out = x * lax.rsqrt(jnp.mean(x * x, axis=-1, keepdims=True) + NORM_EPS) * gain
        err = out - t_ref[...]
        loss_ref[...] += 0.5 * jnp.sum(jnp.mean(err * err, axis=-1, keepdims=True), axis=0, keepdims=True)
        dx, dg = _rms_bwd_math(x, gain, err * (1.0 / d))
        dh_ref[...] = dx
        dhb_ref[...] = (0.5 * dx).astype(BF16)
        dg_ref[...] += dg

    row = pl.BlockSpec((ROW_TILE, d), lambda i: (i, 0))
    vec = pl.BlockSpec((1, d), lambda i: (0, 0))
    one = pl.BlockSpec((SUBLANES, LANES), lambda i: (0, 0))
    return _call(body, name=name, args=[h, gain, target],
                 out_shape=(SDS((t, d), F32), SDS((t, d), BF16), SDS((SUBLANES, LANES), F32), SDS((1, d), F32)),
                 grid=(t // ROW_TILE,), in_specs=[row, vec, row], out_specs=(row, row, one, vec))


def _mixnorm_fwd(ya, yb, ga, gb, name):
    t, c = ya.shape

    def body(ya_ref, yb_ref, ga_ref, gb_ref, y_ref, yt_ref):
        for k, (src, g_ref) in enumerate(((ya_ref, ga_ref), (yb_ref, gb_ref))):
            x = src[...]
            u = x * lax.rsqrt(jnp.mean(x * x, axis=-1, keepdims=True) + NORM_EPS) * g_ref[...]
            y_ref[:, k * c:(k + 1) * c] = u.astype(BF16)
            yt_ref[k * c:(k + 1) * c, :] = u.T.astype(BF16)

    row = pl.BlockSpec((ROW_TILE, c), lambda i: (i, 0))
    vec = pl.BlockSpec((1, c), lambda i: (0, 0))
    return _call(body, name=name, args=[ya, yb, ga, gb],
                 out_shape=(SDS((t, 2 * c), BF16), SDS((2 * c, t), BF16)), grid=(t // ROW_TILE,),
                 in_specs=[row, row, vec, vec],
                 out_specs=(pl.BlockSpec((ROW_TILE, 2 * c), lambda i: (i, 0)),
                            pl.BlockSpec((2 * c, ROW_TILE), lambda i: (0, i))))


def _mixnorm_bwd(dy, ya, yb, ga, gb, name):
    t, c = ya.shape

    def body(dy_ref, ya_ref, yb_ref, ga_ref, gb_ref, dya_ref, dyb_ref, dga_ref, dgb_ref):
        @pl.when(pl.program_id(0) == 0)
        def _():
            dga_ref[...] = jnp.zeros_like(dga_ref)
            dgb_ref[...] = jnp.zeros_like(dgb_ref)

        dxa, dga = _rms_bwd_math(ya_ref[...], ga_ref[...], dy_ref[:, :c])
        dxb, dgb = _rms_bwd_math(yb_ref[...], gb_ref[...], dy_ref[:, c:])
        dya_ref[...] = dxa
        dyb_ref[...] = dxb
        dga_ref[...] += dga
        dgb_ref[...] += dgb

    row = pl.BlockSpec((ROW_TILE, c), lambda i: (i, 0))
    vec = pl.BlockSpec((1, c), lambda i: (0, 0))
    return _call(body, name=name, args=[dy, ya, yb, ga, gb],
                 out_shape=(SDS((t, c), F32), SDS((t, c), F32), SDS((1, c), F32), SDS((1, c), F32)),
                 grid=(t // ROW_TILE,),
                 in_specs=[pl.BlockSpec((ROW_TILE, 2 * c), lambda i: (i, 0)), row, row, vec, vec],
                 out_specs=(row, row, vec, vec))


def _tile(n, want):
    return max(t for t in range(LANES, min(n, want) + 1, LANES) if n % t == 0)


def _mm(a, b, *, nt, out_dtype, tm, tn, name, residual=None, lead=None, out_rows=None, row_offset=0, into=None,
        job=None):
    m, k = a.shape[-2:]
    n = b.shape[0] if nt else b.shape[1]
    tm, tn = _tile(m, tm), _tile(n, tn)
    out_rows = m if out_rows is None else out_rows

    def body(a_ref, b_ref, *rest):
        o_ref = rest[-1]
        av, bv = a_ref[...].astype(BF16), b_ref[...].astype(BF16)
        if nt:
            out = lax.dot_general(av, bv, NT, preferred_element_type=F32)
        else:
            out = jnp.dot(av, bv, preferred_element_type=F32)
        if residual is not None:
            out = rest[0][...] + out
        o_ref[...] = out.astype(out_dtype)

    a_spec = (pl.BlockSpec((tm, k), lambda i, j: (i, 0)) if lead is None
              else pl.BlockSpec((None, tm, k), lambda i, j: (lead, i, 0)))
    in_specs = [a_spec, pl.BlockSpec((tn, k), lambda i, j: (j, 0)) if nt else pl.BlockSpec((k, tn), lambda i, j: (0, j))]
    args, aliases = [a, b], {}
    if residual is not None:
        in_specs.append(pl.BlockSpec((tm, tn), lambda i, j: (i, j)))
        args.append(residual)
    if into is not None:
        in_specs.append(ANY)
        aliases[len(args)] = 0
        args.append(into)
    return _call(body, name=name, args=args, out_shape=SDS((out_rows, n), out_dtype), grid=(m // tm, n // tn),
                 in_specs=in_specs, out_specs=pl.BlockSpec((tm, tn), lambda i, j: (row_offset // tm + i, j)),
                 aliases=aliases, job=job)


FFN_TM = 512
FFN_HB = 512


def _ffn_fwd(h, u, w_in_t, w_out, name, job=None):
    t, d = h.shape
    f = w_out.shape[0]
    nk = f // FFN_HB

    def body(u_ref, w_ref, wo_ref, h_ref, hn_ref, g_ref, up_ref, acc):
        k = pl.program_id(1)

        @pl.when(k == 0)
        def _():
            acc[...] = jnp.zeros_like(acc)

        uu = u_ref[...]
        g = lax.dot_general(uu, w_ref[0], NT, preferred_element_type=F32)
        up = lax.dot_general(uu, w_ref[1], NT, preferred_element_type=F32)
        g_ref[...] = g
        up_ref[...] = up
        hid = (g * _sigmoid(g)) * up
        acc[...] += jnp.dot(hid.astype(BF16), wo_ref[...], preferred_element_type=F32)

        @pl.when(k == nk - 1)
        def _():
            hn_ref[...] = h_ref[...] + 0.5 * acc[...]

    tok = pl.BlockSpec((FFN_TM, d), lambda i, k: (i, 0))
    pre = pl.BlockSpec((FFN_TM, FFN_HB), lambda i, k: (i, k))
    return _call(body, name=name, args=[u, w_in_t.reshape(2, f, d), w_out, h],
                 out_shape=(SDS((t, d), F32), SDS((t, f), F32), SDS((t, f), F32)), grid=(t // FFN_TM, nk),
                 in_specs=[tok, pl.BlockSpec((2, FFN_HB, d), lambda i, k: (0, k, 0)),
                           pl.BlockSpec((FFN_HB, d), lambda i, k: (k, 0)), tok],
                 out_specs=(tok, pre, pre), scratch_shapes=[pltpu.VMEM((FFN_TM, d), F32)], job=job)


def _ffn_bwd(dfb, gpre, upre, w_in_t, w_out, name, job=None):
    t, d = dfb.shape
    f = w_out.shape[0]
    nk = f // FFN_HB

    def body(df_ref, g_ref, up_ref, w_ref, wo_ref, du_ref, hid_t_ref, da_t_ref, acc):
        k = pl.program_id(1)

        @pl.when(k == 0)
        def _():
            acc[...] = jnp.zeros_like(acc)

        dhid = lax.dot_general(df_ref[...], wo_ref[...], NT, preferred_element_type=F32)
        g, up = g_ref[...], up_ref[...]
        sig = _sigmoid(g)
        silu = g * sig
        dup = dhid * silu
        dg = dhid * up * (sig * (1.0 + g * (1.0 - sig)))
        hid_t_ref[...] = (silu * up).T.astype(BF16)
        da_t_ref[0] = dg.T.astype(BF16)
        da_t_ref[1] = dup.T.astype(BF16)
        acc[...] += (jnp.dot(dg.astype(BF16), w_ref[0], preferred_element_type=F32)
                     + jnp.dot(dup.astype(BF16), w_ref[1], preferred_element_type=F32))

        @pl.when(k == nk - 1)
        def _():
            du_ref[...] = acc[...]

    tok = pl.BlockSpec((FFN_TM, d), lambda i, k: (i, 0))
    pre = pl.BlockSpec((FFN_TM, FFN_HB), lambda i, k: (i, k))
    return _call(body, name=name, args=[dfb, gpre, upre, w_in_t.reshape(2, f, d), w_out],
                 out_shape=(SDS((t, d), F32), SDS((f, t), BF16), SDS((2, f, t), BF16)), grid=(t // FFN_TM, nk),
                 in_specs=[tok, pre, pre, pl.BlockSpec((2, FFN_HB, d), lambda i, k: (0, k, 0)),
                           pl.BlockSpec((FFN_HB, d), lambda i, k: (k, 0))],
                 out_specs=(tok, pl.BlockSpec((FFN_HB, FFN_TM), lambda i, k: (k, i)),
                            pl.BlockSpec((2, FFN_HB, FFN_TM), lambda i, k: (0, k, i))),
                 scratch_shapes=[pltpu.VMEM((FFN_TM, d), F32)], job=job)


CH = LANES
PAD = SUBLANES


def _lru_gates(xc, gw_ref, gb_ref, lam_ref, z):
    xcb = xc.astype(BF16)
    r = _sigmoid(jnp.dot(xcb, gw_ref[2 * z], preferred_element_type=F32) + gb_ref[pl.ds(2 * z, 1), :])
    i = _sigmoid(jnp.dot(xcb, gw_ref[2 * z + 1], preferred_element_type=F32) + gb_ref[pl.ds(2 * z + 1, 1), :])
    sp = _softplus(-lam_ref[pl.ds(z, 1), :])
    log_a = (-RG_C * r) * sp
    a = jnp.exp(log_a)
    mult = jnp.sqrt(-_expm1(2.0 * log_a))
    return r, i, sp, a, mult


def _conv(xpad, cw_ref, cb_ref, t):
    xc = cb_ref[...] + cw_ref[pl.ds(0, 1), :] * xpad[pl.ds(PAD - 2, t), :]
    for j in range(1, CONV_WIDTH):
        xc = xc + cw_ref[pl.ds(j, 1), :] * xpad[pl.ds(PAD - 2 + j, t), :]
    return xc


def _fill_padded(pad_ref, value, t):
    pad_ref[pl.ds(0, PAD), :] = jnp.zeros((PAD, CH), F32)
    pad_ref[pl.ds(PAD + t, PAD), :] = jnp.zeros((PAD, CH), F32)
    pad_ref[pl.ds(PAD, t), :] = value


def _scan_pair(t, a_up, b_up, out_up, a_down, b_down, out_down):
    def step(tt, carry):
        hu, hd = carry
        lo = pl.multiple_of(tt * SUBLANES, SUBLANES)
        hi = pl.multiple_of(t - SUBLANES - tt * SUBLANES, SUBLANES)
        for j in range(SUBLANES):
            su, sd = pl.ds(lo + j, 1), pl.ds(hi + SUBLANES - 1 - j, 1)
            hu = a_up(su) * hu + b_up(su)
            out_up[su, :] = hu
            hd = a_down(sd) * hd + b_down(sd)
            out_down[sd, :] = hd
        return hu, hd

    zero = jnp.zeros((1, CH), F32)
    lax.fori_loop(0, t // SUBLANES, step, (zero, zero))


def _lru_fwd(proj, cw, cb, gw, gb, lam, name, job=None):
    t = proj.shape[0]
    c = cw.shape[1]
    ncb = c // CH

    def body(x_ref, g_ref, cw_ref, cb_ref, gw_ref, gb_ref, lam_ref, ya_ref, hf_ref, hb_ref, xpad, a0, b0, a1, b1):
        _fill_padded(xpad, x_ref[...], t)
        xc = _conv(xpad, cw_ref, cb_ref, t)
        for z, (a_s, b_s) in enumerate(((a0, b0), (a1, b1))):
            _, i, _, a, mult = _lru_gates(xc, gw_ref, gb_ref, lam_ref, z)
            a_s[...] = a
            b_s[...] = mult * (i * xc)
        _scan_pair(t, lambda s: a0[s, :], lambda s: b0[s, :], hf_ref, lambda s: a1[s, :], lambda s: b1[s, :], hb_ref)
        gelu, _ = _gelu_parts(g_ref[...])
        ya_ref[...] = gelu * (hf_ref[...] + hb_ref[...])

    col = lambda off: pl.BlockSpec((t, CH), lambda i: (0, off + i))
    small = lambda rows: pl.BlockSpec((rows, CH), lambda i: (0, i))
    return _call(body, name=name, args=[proj, proj, cw, cb, gw, gb, lam], out_shape=(SDS((t, c), F32),) * 3,
                 grid=(ncb,),
                 in_specs=[col(0), col(ncb), small(CONV_WIDTH), small(1),
                           pl.BlockSpec((4, None, CH, CH), lambda i: (0, i, 0, 0)), small(4), small(2)],
                 out_specs=(col(0),) * 3,
                 scratch_shapes=[pltpu.VMEM((t + 2 * PAD, CH), F32)] + [pltpu.VMEM((t, CH), F32)] * 4, job=job)


def _lru_bwd(proj, cw, cb, gw, gb, lam, hf, hb, dya, name, job=None):
    t = proj.shape[0]
    c = cw.shape[1]
    ncb = c // CH

    def body(x_ref, g_ref, cw_ref, cb_ref, gw_ref, gb_ref, lam_ref, hf_ref, hb_ref, dya_ref,
             dx_ref, dg_ref, dt_ref, dcw_ref, dcb_ref, dgw_ref, dgb_ref, dlam_ref,
             xpad, hpad, dxc, a0, a1, dhs, dh0, dh1):
        _fill_padded(xpad, x_ref[...], t)
        xc = _conv(xpad, cw_ref, cb_ref, t)
        xcb = xc.astype(BF16)
        gates = [_lru_gates(xc, gw_ref, gb_ref, lam_ref, z) for z in range(2)]
        a0[...] = gates[0][3]
        a1[...] = gates[1][3]

        gelu, dgelu = _gelu_parts(g_ref[...])
        dya = dya_ref[...]
        dgate = dya * (hf_ref[...] + hb_ref[...]) * dgelu
        dg_ref[...] = dgate.astype(BF16)
        dt_ref[1] = dgate.T.astype(BF16)
        dhs[...] = dya * gelu

        def step(tt, carry):
            c0, p0, c1, p1 = carry
            lo = pl.multiple_of(tt * SUBLANES, SUBLANES)
            hi = pl.multiple_of(t - SUBLANES - tt * SUBLANES, SUBLANES)
            for j in range(SUBLANES):
                su, sd = pl.ds(lo + j, 1), pl.ds(hi + SUBLANES - 1 - j, 1)
                c0 = dhs[sd, :] + p0 * c0
                dh0[sd, :] = c0
                p0 = a0[sd, :]
                c1 = dhs[su, :] + p1 * c1
                dh1[su, :] = c1
                p1 = a1[su, :]
            return c0, p0, c1, p1

        zero = jnp.zeros((1, CH), F32)
        lax.fori_loop(0, t // SUBLANES, step, (zero, zero, zero, zero))

        acc_dxc = jnp.zeros((t, CH), F32)
        for z, (h_ref, dh_ref, shift) in enumerate(((hf_ref, dh0, -1), (hb_ref, dh1, 1))):
            r, i, sp, a, mult = gates[z]
            _fill_padded(hpad, h_ref[...], t)
            h_nb = hpad[pl.ds(PAD + shift, t), :]
            db = dh_ref[...]
            da = db * h_nb
            d_i = db * mult * xc
            acc_dxc = acc_dxc + db * mult * i
            d_mult = db * i * xc
            d_la = da * a - d_mult * (a * a) / mult
            d_r = d_la * (-RG_C * sp)
            dlam_ref[pl.ds(z, 1), :] = (jnp.sum(d_la * (-RG_C * r), axis=0, keepdims=True)
                                        * (-_sigmoid(-lam_ref[pl.ds(z, 1), :])))
            for gate, d_pre in ((0, d_r * r * (1.0 - r)), (1, d_i * i * (1.0 - i))):
                zg = 2 * z + gate
                dgb_ref[pl.ds(zg, 1), :] = jnp.sum(d_pre, axis=0, keepdims=True)
                d_pre_b = d_pre.astype(BF16)
                dgw_ref[zg] = lax.dot_general(xcb, d_pre_b, TN, preferred_element_type=F32)
                acc_dxc = acc_dxc + lax.dot_general(d_pre_b, gw_ref[zg], NT, preferred_element_type=F32)

        dcb_ref[...] = jnp.sum(acc_dxc, axis=0, keepdims=True)
        for j in range(CONV_WIDTH):
            dcw_ref[pl.ds(j, 1), :] = jnp.sum(acc_dxc * xpad[pl.ds(PAD - 2 + j, t), :], axis=0, keepdims=True)
        _fill_padded(dxc, acc_dxc, t)
        dx = cw_ref[pl.ds(0, 1), :] * dxc[pl.ds(PAD + 2, t), :]
        for j in range(1, CONV_WIDTH):
            dx = dx + cw_ref[pl.ds(j, 1), :] * dxc[pl.ds(PAD + 2 - j, t), :]
        dx_ref[...] = dx.astype(BF16)
        dt_ref[0] = dx.T.astype(BF16)

    col = lambda off: pl.BlockSpec((t, CH), lambda i: (0, off + i))
    small = lambda rows: pl.BlockSpec((rows, CH), lambda i: (0, i))
    dense = pl.BlockSpec((4, None, CH, CH), lambda i: (0, i, 0, 0))
    padded = pltpu.VMEM((t + 2 * PAD, CH), F32)
    return _call(
        body, name=name, args=[proj, proj, cw, cb, gw, gb, lam, hf, hb, dya],
        out_shape=(SDS((t, c), BF16), SDS((t, c), BF16), SDS((2, c, t), BF16), SDS((CONV_WIDTH, c), F32),
                   SDS((1, c), F32), SDS((4, ncb, CH, CH), F32), SDS((4, c), F32), SDS((2, c), F32)),
        grid=(ncb,),
        in_specs=[col(0), col(ncb), small(CONV_WIDTH), small(1), dense, small(4), small(2), col(0), col(0), col(0)],
        out_specs=(col(0), col(0), pl.BlockSpec((2, CH, t), lambda i: (0, i, 0)), small(CONV_WIDTH), small(1),
                   dense, small(4), small(2)),
        scratch_shapes=[padded, padded, padded] + [pltpu.VMEM((t, CH), F32)] * 5, job=job)


def _band_start(r, rows):
    return jnp.clip(r - WIN_ROWS // 2, 0, rows - WIN_ROWS)


def _bias_tables(rpb):
    cols = np.arange(GRID_W)
    start = np.clip(cols - WIN_COLS // 2, 0, GRID_W - WIN_COLS)
    valid = (cols[None, :] >= start[:, None]) & (cols[None, :] < start[:, None] + WIN_COLS)
    col_off = np.clip(cols[None, :] - cols[:, None] + WIN_COLS - 1, 0, 2 * WIN_COLS - 2)
    row_off = np.arange(WIN_ROWS)[None, :] - np.arange(WIN_ROWS)[:, None] + WIN_ROWS - 1
    pick_row = jnp.asarray(np.eye(2 * WIN_ROWS - 1, dtype=np.float32)[row_off])
    pick_col = jnp.asarray(np.eye(2 * WIN_COLS - 1, dtype=np.float32)[col_off] * valid[..., None])
    hi = lax.Precision.HIGHEST
    by_row = jnp.einsum("hrc,ajr->hajc", rpb, pick_row, precision=hi)
    table = jnp.einsum("hajc,qkc->haqjk", by_row, pick_col, precision=hi)
    table = jnp.where(jnp.asarray(valid)[None, None, :, None, :], table, NEG)
    return table.reshape(rpb.shape[0], WIN_ROWS, GRID_W, BAND)


def _attn_scores(q_ref, k_ref, bm_ref, hh, r, rows):
    rs = _band_start(r, rows)
    lanes = pl.ds(hh * HEAD_DIM, HEAD_DIM)
    qrows = pl.ds(pl.multiple_of(r * GRID_W, GRID_W), GRID_W)
    band = pl.ds(pl.multiple_of(rs * GRID_W, GRID_W), BAND)
    q = q_ref[qrows, lanes].astype(BF16)
    kb = k_ref[band, lanes].astype(BF16)
    s = lax.dot_general(q, kb, NT, preferred_element_type=F32) * (HEAD_DIM ** -0.5) + bm_ref[hh, r - rs]
    p = jnp.exp(s - jnp.max(s, axis=-1, keepdims=True))
    p = p / jnp.sum(p, axis=-1, keepdims=True)
    return q, kb, p, qrows, band, lanes, r - rs


def _attn_fwd(proj, tables, width, name, job=None):
    t = proj.shape[0]
    rows = t // GRID_W
    npair = width // LANES
    first = (proj.shape[1] - 3 * width) // LANES

    def body(q_ref, k_ref, v_ref, bm_ref, o_ref, qs, ks, vs):
        qs[...] = q_ref[...].astype(BF16)
        ks[...] = k_ref[...].astype(BF16)
        vs[...] = v_ref[...].astype(BF16)

        def row(r, carry):
            for hh in range(2):
                _, _, p, qrows, band, lanes, _ = _attn_scores(qs, ks, bm_ref, hh, r, rows)
                o_ref[qrows, lanes] = jnp.dot(p.astype(BF16), vs[band, lanes], preferred_element_type=F32)
            return carry

        lax.fori_loop(0, rows, row, 0, unroll=2)

    col = lambda off: pl.BlockSpec((t, LANES), lambda i: (0, off + i))
    return _call(body, name=name, args=[proj, proj, proj, tables], out_shape=SDS((t, width), F32), grid=(npair,),
                 in_specs=[col(first), col(first + npair), col(first + 2 * npair),
                           pl.BlockSpec((2, WIN_ROWS, GRID_W, BAND), lambda i: (i, 0, 0, 0))],
                 out_specs=col(0), scratch_shapes=[pltpu.VMEM((t, LANES), BF16)] * 3, job=job)


def _attn_bwd(proj, tables, dyb, name, job=None):
    t, width = dyb.shape
    rows = t // GRID_W
    npair = width // LANES
    first = (proj.shape[1] - 3 * width) // LANES

    def body(q_ref, k_ref, v_ref, bm_ref, do_ref, dq_ref, dk_ref, dv_ref, dt_ref, dbm_ref, dq_s, dk_s, dv_s,
             qs, ks, vs, dos):
        qs[...] = q_ref[...].astype(BF16)
        ks[...] = k_ref[...].astype(BF16)
        vs[...] = v_ref[...].astype(BF16)
        dos[...] = do_ref[...].astype(BF16)
        dk_s[...] = jnp.zeros_like(dk_s)
        dv_s[...] = jnp.zeros_like(dv_s)
        dbm_ref[...] = jnp.zeros_like(dbm_ref)

        def row(r, carry):
            for hh in range(2):
                q, kb, p, qrows, band, lanes, case = _attn_scores(qs, ks, bm_ref, hh, r, rows)
                do = dos[qrows, lanes]
                dp = lax.dot_general(do, vs[band, lanes], NT, preferred_element_type=F32)
                ds = p * (dp - jnp.sum(dp * p, axis=-1, keepdims=True))
                dbm_ref[hh, case] += ds
                dsb = (ds * (HEAD_DIM ** -0.5)).astype(BF16)
                dq_s[qrows, lanes] = jnp.dot(dsb, kb, preferred_element_type=F32)
                dk_s[band, lanes] += lax.dot_general(dsb, q, TN, preferred_element_type=F32)
                dv_s[band, lanes] += lax.dot_general(p.astype(BF16), do, TN, preferred_element_type=F32)
            return carry

        lax.fori_loop(0, rows, row, 0, unroll=2)
        for n, (src, dst) in enumerate(((dq_s, dq_ref), (dk_s, dk_ref), (dv_s, dv_ref))):
            val = src[...]
            dst[...] = val.astype(BF16)
            dt_ref[n] = val.T.astype(BF16)

    col = lambda off: pl.BlockSpec((t, LANES), lambda i: (0, off + i))
    table = pl.BlockSpec((2, WIN_ROWS, GRID_W, BAND), lambda i: (i, 0, 0, 0))
    return _call(body, name=name, args=[proj, proj, proj, tables, dyb],
                 out_shape=(SDS((t, width), BF16),) * 3 + (SDS((3, width, t), BF16), SDS(tables.shape, F32)),
                 grid=(npair,),
                 in_specs=[col(first), col(first + npair), col(first + 2 * npair), table, col(0)],
                 out_specs=(col(0), col(0), col(0), pl.BlockSpec((3, LANES, t), lambda i: (0, i, 0)), table),
                 scratch_shapes=[pltpu.VMEM((t, LANES), F32)] * 3 + [pltpu.VMEM((t, LANES), BF16)] * 4, job=job)


def _adamw_math(w, g, m, v):
    m = ADAM_B1 * m + (1.0 - ADAM_B1) * g
    v = ADAM_B2 * v + (1.0 - ADAM_B2) * (g * g)
    m_hat = m / (1.0 - ADAM_B1 ** ADAM_STEP)
    v_hat = v / (1.0 - ADAM_B2 ** ADAM_STEP)
    delta = -ADAM_LR * (m_hat / (jnp.sqrt(v_hat) + ADAM_EPS) + ADAM_WD * w)
    return delta, m, v


def _sum_partials(p_ref):
    g = p_ref[0].astype(F32)
    for s in range(1, N_CHIP):
        g = g + p_ref[s].astype(F32)
    return g


def _adamw_rows(w, partials, m, v, name):
    rb, n = w.shape
    tr = 64

    def body(w_ref, p_ref, m_ref, v_ref, g_ref, d_ref, nm_ref, nv_ref):
        g = _sum_partials(p_ref)
        g_ref[...] = g
        d_ref[...], nm_ref[...], nv_ref[...] = _adamw_math(w_ref[...], g, m_ref[...], v_ref[...])

    blk = pl.BlockSpec((tr, n), lambda i: (i, 0))
    return _call(body, name=name, args=[w, partials.reshape(N_CHIP, rb, n), m, v], out_shape=(SDS((rb, n), F32),) * 4,
                 grid=(rb // tr,), in_specs=[blk, pl.BlockSpec((N_CHIP, tr, n), lambda i: (0, i, 0)), blk, blk],
                 out_specs=(blk,) * 4)


def _adamw_cols(w, partials, m, v, name):
    d, nb = w.shape
    td = 256

    def body(w_ref, p_ref, m_ref, v_ref, g_ref, d_ref, nm_ref, nv_ref):
        g = _sum_partials(p_ref).T
        g_ref[...] = g
        d_ref[...], nm_ref[...], nv_ref[...] = _adamw_math(w_ref[...], g, m_ref[...], v_ref[...])

    blk = pl.BlockSpec((td, nb), lambda i: (i, 0))
    return _call(body, name=name, args=[w, partials.reshape(N_CHIP, nb, d), m, v], out_shape=(SDS((d, nb), F32),) * 4,
                 grid=(d // td,), in_specs=[blk, pl.BlockSpec((N_CHIP, nb, td), lambda i: (0, 0, i)), blk, blk],
                 out_specs=(blk,) * 4)


def _adamw_small(w, g, m, v, name):
    def body(w_ref, g_ref, m_ref, v_ref, d_ref, nm_ref, nv_ref):
        d_ref[...], nm_ref[...], nv_ref[...] = _adamw_math(w_ref[...], g_ref[...], m_ref[...], v_ref[...])

    return _call(body, name=name, args=[w, g, m, v], out_shape=(SDS(w.shape, F32),) * 3, in_specs=[WHOLE] * 4,
                 out_specs=(WHOLE,) * 3)


TILE = SUBLANES * LANES


def _pack(arrays):
    parts = []
    for a in arrays:
        flat = a.reshape(-1).astype(F32)
        flat = jnp.pad(flat, (0, -flat.size % TILE))
        parts.append(flat.reshape(-1, LANES))
    return jnp.concatenate(parts, axis=0)


def _unpack(pack, like):
    out, row = [], 0
    for a in like:
        n = int(np.prod(a.shape))
        nrows = -(-n // TILE) * SUBLANES
        out.append(pack[row:row + nrows].reshape(-1)[:n].reshape(a.shape))
        row += nrows
    return out


def _dense_gate_blocks(gate_w):
    w = gate_w.reshape(4, -1, 2, HEAD_DIM, HEAD_DIM)
    zero = jnp.zeros_like(w[:, :, 0])
    top = jnp.concatenate([w[:, :, 0], zero], axis=-1)
    bottom = jnp.concatenate([zero, w[:, :, 1]], axis=-1)
    return jnp.concatenate([top, bottom], axis=-2)


def _diag_gate_blocks(dense, shape):
    even = dense[:, :, :HEAD_DIM, :HEAD_DIM]
    odd = dense[:, :, HEAD_DIM:, HEAD_DIM:]
    return jnp.stack([even, odd], axis=2).reshape(shape)


LARGE = ("ffn1_w_in", "ffn1_w_out", "w_in_mix", "w_out_mix", "ffn2_w_in", "ffn2_w_out")
COLUMN_SHARDED = ("ffn1_w_in", "w_in_mix", "ffn2_w_in")
SHARDED_SMALL = ("lru_conv_w", "lru_lambda")
REPLICATED = ("norm_ffn1", "norm_mix", "lru_conv_b", "lru_gate_w", "lru_gate_b", "attn_rpb", "lru_out_norm",
              "attn_out_norm", "norm_ffn2", "norm_final")
WEIGHTS = ("norm_ffn1", "ffn1_w_in", "ffn1_w_out", "norm_mix", "w_in_mix", "lru_conv_w", "lru_conv_b", "lru_gate_w",
           "lru_gate_b", "lru_lambda", "attn_rpb", "lru_out_norm", "attn_out_norm", "w_out_mix", "norm_ffn2",
           "ffn2_w_in", "ffn2_w_out", "norm_final")


PARTS = {("gather", "ffn2_w_out"): 2, ("to_chips", "ffn1_w_out"): 2}
CARRIES = {
    "gather_ffn1": [(("gather", "ffn1_w_in"), 1), (("gather", "ffn1_w_out"), 1), (("gather", "small"), 1)],
    "ffn1_fwd": [(("gather", "w_in_mix"), 1), (("gather", "w_out_mix"), 1)],
    "mix_in_proj": [(("gather", "ffn2_w_out"), 1)],
    "lru_fwd": [(("gather", "ffn2_w_out"), 1)],
    "attn_fwd": [(("gather", "ffn2_w_in"), 1)],
    "ffn2_in_grad": [(("to_sibling", "ffn2_w_out"), 1)],
    "norm_ffn2_bwd": [(("to_sibling", "ffn2_w_in"), 1)],
    "attn_bwd": [(("to_chips", "ffn2_w_out"), 1), (("to_chips", "ffn2_w_in"), 1)],
    "lru_bwd": [(("to_sibling", "w_out_mix"), 1)],
    "mix_in_grad": [(("to_chips", "w_out_mix"), 1)],
    "mix_in_bwd": [(("to_sibling", "w_in_mix"), 1)],
    "ffn1_bwd": [(("to_chips", "w_in_mix"), 1)],
    "ffn1_in_grad_gate": [(("to_sibling", "ffn1_w_out"), 1)],
    "ffn1_in_grad_up": [(("to_chips", "ffn1_w_out"), 1)],
    "to_sibling_ffn1_in": [(("to_sibling", "ffn1_w_in"), 1)],
    "to_chips_ffn1": [(("to_chips", "ffn1_w_out"), 1), (("to_chips", "ffn1_w_in"), 1)],
}


class _Transfer:
    def __init__(self, kind, src, dest, block_rows, parts):
        self.kind, self.src, self.dest = kind, src, dest
        self.ranges, self.taken = _split(block_rows, parts), 0

    def take(self, count):
        lo, hi = self.ranges[self.taken][0], self.ranges[self.taken + count - 1][1]
        self.taken += count
        return _Piece(self.kind, self.src, self.dest, lo, hi)


class _Traffic:
    def __init__(self):
        self.transfers = {}

    def open(self, kind, name, src):
        if kind == "gather":
            dest, rows = _gathered(src), src.shape[0]
        elif kind == "to_sibling":
            dest, rows = SDS((src.shape[0] // 2, src.shape[1]), src.dtype), src.shape[0] // N_DEV
        else:
            dest, rows = SDS(src.shape, src.dtype), src.shape[0] // N_CHIP
        self.transfers[kind, name] = _Transfer(kind, src, dest, rows, PARTS.get((kind, name), 1))

    def _job(self, host):
        moved = [self.transfers[key] for key, _ in CARRIES[host]]
        return moved, _Job([tr.take(count) for tr, (_, count) in zip(moved, CARRIES[host])])

    def carry(self, host, fn, *args, **kw):
        if host not in CARRIES:
            return fn(*args, name=host, **kw)
        moved, job = self._job(host)
        res, landed = fn(*args, name=host, job=job, **kw)
        for tr, arr in zip(moved, landed):
            tr.dest = arr
        return res

    def alone(self, host):
        moved, job = self._job(host)
        for tr, arr in zip(moved, _run_job(job, host)):
            tr.dest = arr

    def result(self, kind, name):
        tr = self.transfers.pop((kind, name))
        assert tr.taken == len(tr.ranges), (kind, name)
        return tr.dest


def _forward_backward(x, target, shards, sharded_small, s):
    c = s["lru_conv_b"].shape[1]
    width = s["attn_out_norm"].shape[1]
    t = x.shape[0]
    traffic = _Traffic()
    carry = traffic.carry
    weight = lambda n: traffic.result("gather", n)

    for n in LARGE:
        traffic.open("gather", n, shards[n])
    traffic.open("gather", "small", sharded_small)
    traffic.alone("gather_ffn1")
    full_small = weight("small").reshape(N_DEV, SUBLANES, c // N_DEV)
    conv_w = full_small[:, :CONV_WIDTH].transpose(1, 0, 2).reshape(CONV_WIDTH, c)
    lam = full_small[:, CONV_WIDTH:CONV_WIDTH + 2].transpose(1, 0, 2).reshape(2, c)
    w = {n: weight(n) for n in ("ffn1_w_in", "ffn1_w_out")}
    u1 = _rmsnorm_fwd(x, s["norm_ffn1"], "norm_ffn1")
    h1, g1, up1 = carry("ffn1_fwd", _ffn_fwd, x, u1, w["ffn1_w_in"], w["ffn1_w_out"])
    w["w_in_mix"], w["w_out_mix"] = weight("w_in_mix"), weight("w_out_mix")
    u2 = _rmsnorm_fwd(h1, s["norm_mix"], "norm_mix")
    proj = carry("mix_in_proj", _mm, u2, w["w_in_mix"], nt=True, out_dtype=F32, tm=512, tn=512)
    gw = _dense_gate_blocks(s["lru_gate_w"]).astype(BF16)
    gb = s["lru_gate_b"].reshape(4, c)
    tables, tables_vjp = jax.vjp(_bias_tables, s["attn_rpb"])
    ya, hf, hb = carry("lru_fwd", _lru_fwd, proj, conv_w, s["lru_conv_b"], gw, gb, lam)
    yb = carry("attn_fwd", _attn_fwd, proj, tables, width)
    y, yt = _mixnorm_fwd(ya, yb, s["lru_out_norm"], s["attn_out_norm"], "mix_norm")
    h2 = carry("mix_out_proj", _mm, y, w["w_out_mix"], nt=False, out_dtype=F32, tm=512, tn=512, residual=h1)
    u3 = _rmsnorm_fwd(h2, s["norm_ffn2"], "norm_ffn2")
    w["ffn2_w_in"], w["ffn2_w_out"] = weight("ffn2_w_in"), weight("ffn2_w_out")
    h3, g2, up2 = carry("ffn2_fwd", _ffn_fwd, h2, u3, w["ffn2_w_in"], w["ffn2_w_out"])
    dh3, df2, loss_part, d_norm_final = _final_loss(h3, s["norm_final"], target, "final_loss")

    grads = {}
    grad_of = dict(nt=False, out_dtype=BF16, tm=512, tn=1024)

    def reduce_in_chip(n):
        traffic.open("to_sibling", n, grads[n])

    def reduce_over_chips(n):
        traffic.open("to_chips", n, _pair_sum(grads[n], traffic.result("to_sibling", n), "pair_sum_" + n))

    du3, hid2_t, da2_t = carry("ffn2_bwd", _ffn_bwd, df2, g2, up2, w["ffn2_w_in"], w["ffn2_w_out"])
    f = hid2_t.shape[0]
    grads["ffn2_w_out"] = carry("ffn2_out_grad", _mm, hid2_t, df2, **grad_of)
    reduce_in_chip("ffn2_w_out")
    grads["ffn2_w_in"] = carry("ffn2_in_grad", _mm, da2_t.reshape(2 * f, t), u3, **grad_of)
    reduce_in_chip("ffn2_w_in")
    reduce_over_chips("ffn2_w_out")
    dh2, dh2b, d_norm_ffn2 = carry("norm_ffn2_bwd", _rmsnorm_bwd, du3, h2, s["norm_ffn2"], dh3, 1.0)
    reduce_over_chips("ffn2_w_in")
    grads["w_out_mix"] = carry("mix_out_grad", _mm, yt, dh2b, **grad_of)
    reduce_in_chip("w_out_mix")
    dy = carry("mix_out_bwd", _mm, dh2b, w["w_out_mix"], nt=True, out_dtype=F32, tm=512, tn=512)
    dya, dyb, d_lru_out_norm, d_attn_out_norm = _mixnorm_bwd(dy, ya, yb, s["lru_out_norm"], s["attn_out_norm"],
                                                             "mix_norm_bwd")
    dq, dk, dv, dqkv_t, d_tables = carry("attn_bwd", _attn_bwd, proj, tables, dyb)
    dx_lru, dg_lru, dxg_t, d_conv_w, d_conv_b, d_gw, d_gb, d_lam = carry(
        "lru_bwd", _lru_bwd, proj, conv_w, s["lru_conv_b"], gw, gb, lam, hf, hb, dya)
    reduce_over_chips("w_out_mix")
    dproj = jnp.concatenate([dx_lru, dg_lru, dq, dk, dv], axis=1)
    dproj_t = jnp.concatenate([dxg_t.reshape(2 * c, t), dqkv_t.reshape(3 * width, t)], axis=0)
    grads["w_in_mix"] = carry("mix_in_grad", _mm, dproj_t, u2, **grad_of)
    reduce_in_chip("w_in_mix")
    du2 = carry("mix_in_bwd", _mm, dproj, w["w_in_mix"], nt=False, out_dtype=F32, tm=512, tn=512)
    reduce_over_chips("w_in_mix")
    dh1, df1, d_norm_mix = carry("norm_mix_bwd", _rmsnorm_bwd, du2, h1, s["norm_mix"], dh2, 0.5)
    du1, hid1_t, da1_t = carry("ffn1_bwd", _ffn_bwd, df1, g1, up1, w["ffn1_w_in"], w["ffn1_w_out"])
    grads["ffn1_w_out"] = carry("ffn1_out_grad", _mm, hid1_t, df1, **grad_of)
    reduce_in_chip("ffn1_w_out")
    gate_rows = carry("ffn1_in_grad_gate", _mm, da1_t, u1, lead=0, out_rows=2 * f, **grad_of)
    reduce_over_chips("ffn1_w_out")
    grads["ffn1_w_in"] = carry("ffn1_in_grad_up", _mm, da1_t, u1, lead=1, out_rows=2 * f, row_offset=f,
                               into=gate_rows, **grad_of)
    reduce_in_chip("ffn1_w_in")
    grad_x, _, d_norm_ffn1 = _rmsnorm_bwd(du1, x, s["norm_ffn1"], dh1, 1.0, "norm_ffn1_bwd")
    traffic.alone("to_sibling_ffn1_in")
    reduce_over_chips("ffn1_w_in")
    traffic.alone("to_chips_ffn1")
    partials = {n: traffic.result("to_chips", n) for n in LARGE}
    assert not traffic.transfers, list(traffic.transfers)

    small = {
        "norm_ffn1": d_norm_ffn1, "norm_mix": d_norm_mix, "lru_conv_w": d_conv_w, "lru_conv_b": d_conv_b,
        "lru_gate_w": _diag_gate_blocks(d_gw, s["lru_gate_w"].shape), "lru_gate_b": d_gb.reshape(s["lru_gate_b"].shape),
        "lru_lambda": d_lam, "attn_rpb": tables_vjp(d_tables)[0], "lru_out_norm": d_lru_out_norm,
        "attn_out_norm": d_attn_out_norm, "norm_ffn2": d_norm_ffn2, "norm_final": d_norm_final,
    }
    return loss_part[0, 0], grad_x, partials, small


def _step(x, loss_target, p, m, v):
    me = 4 * lax.axis_index("x") + 2 * lax.axis_index("y") + lax.axis_index("c")

    shards = {n: (_cast_transposed if n in COLUMN_SHARDED else _cast_rows)(p[n], "cast_" + n) for n in LARGE}
    sharded_small = (jnp.pad(p["lru_conv_w"], ((0, SUBLANES - CONV_WIDTH), (0, 0)))
                     + jnp.pad(p["lru_lambda"], ((CONV_WIDTH, SUBLANES - CONV_WIDTH - 2), (0, 0))))
    s = {n: p[n] if n in ("lru_gate_w", "lru_gate_b", "attn_rpb") else p[n].reshape(1, -1) for n in REPLICATED}

    loss_part, grad_x, partials, small = _forward_backward(x, loss_target, shards, sharded_small, s)
    loss = lax.psum(loss_part, ("x", "y", "c"))

    out = {}
    for n in LARGE:
        update = _adamw_cols if n in COLUMN_SHARDED else _adamw_rows
        out[n] = update(p[n], partials[n], m[n], v[n], "adamw_" + n)

    by_device = lambda a: a.reshape(a.shape[0], N_DEV, -1).transpose(1, 0, 2)
    small_list = [small[n] for n in REPLICATED] + [by_device(small[n]) for n in SHARDED_SMALL]
    reduced = _unpack(_all_reduce_small(_pack(small_list), "all_reduce_small"), small_list)
    g_rep = dict(zip(REPLICATED, reduced[:len(REPLICATED)]))
    g_sh = {n: lax.dynamic_index_in_dim(r, me, axis=0, keepdims=False)
            for n, r in zip(SHARDED_SMALL, reduced[len(REPLICATED):])}
    g_small = {**g_rep, **g_sh}
    names = REPLICATED + SHARDED_SMALL
    like = [p[n] for n in names]
    pack_of = lambda d: _pack([d[n].reshape(p[n].shape) for n in names])
    upd = _adamw_small(pack_of(p), pack_of(g_small), pack_of(m), pack_of(v), "adamw_small")
    for n, d_, m_, v_ in zip(names, *[_unpack(u, like) for u in upd]):
        out[n] = (g_small[n].reshape(p[n].shape), d_, m_, v_)
    return loss, grad_x, out


def kernel(x, norm_ffn1, ffn1_w_in, ffn1_w_out, norm_mix, w_in_mix, lru_conv_w, lru_conv_b, lru_gate_w, lru_gate_b, lru_lambda, attn_rpb, lru_out_norm, attn_out_norm, w_out_mix, norm_ffn2, ffn2_w_in, ffn2_w_out, norm_final, loss_target, m_norm_ffn1, m_ffn1_w_in, m_ffn1_w_out, m_norm_mix, m_w_in_mix, m_lru_conv_w, m_lru_conv_b, m_lru_gate_w, m_lru_gate_b, m_lru_lambda, m_attn_rpb, m_lru_out_norm, m_attn_out_norm, m_w_out_mix, m_norm_ffn2, m_ffn2_w_in, m_ffn2_w_out, m_norm_final, v_norm_ffn1, v_ffn1_w_in, v_ffn1_w_out, v_norm_mix, v_w_in_mix, v_lru_conv_w, v_lru_conv_b, v_lru_gate_w, v_lru_gate_b, v_lru_lambda, v_attn_rpb, v_lru_out_norm, v_attn_out_norm, v_w_out_mix, v_norm_ffn2, v_ffn2_w_in, v_ffn2_w_out, v_norm_final):
    given = dict(locals())
    drop_layer = lambda n, a: a if n == "norm_final" else a[0]
    p = {n: drop_layer(n, given[n]) for n in WEIGHTS}
    m = {n: drop_layer(n, given["m_" + n]) for n in WEIGHTS}
    v = {n: drop_layer(n, given["v_" + n]) for n in WEIGHTS}
    loss, grad_x, out = _step(x[0], loss_target[0], p, m, v)
    shaped = lambda n, a: a.reshape(given[n].shape)
    return (loss, grad_x[None], *[shaped(n, out[n][k]) for k in range(4) for n in WEIGHTS])
```

```python
import math

import numpy as np
import jax
import jax.numpy as jnp
from jax import lax
from jax.experimental import pallas as pl
from jax.experimental.pallas import tpu as pltpu

F32 = jnp.float32
BF16 = jnp.bfloat16
SDS = jax.ShapeDtypeStruct

N_DEV = 8
N_CHIP = 4
NORM_EPS = 1e-6
RG_C = 8.0
CONV_WIDTH = 4
HEAD_DIM = 64
GRID_W = 64
WIN_ROWS = 8
WIN_COLS = 16
BAND = WIN_ROWS * GRID_W
NEG = -1e30

ADAM_LR = 0.001
ADAM_B1 = 0.9
ADAM_B2 = 0.999
ADAM_EPS = 1e-08
ADAM_WD = 0.01
ADAM_STEP = 10

LANES = 128
SUBLANES = 8
VMEM_LIMIT = 56 * 1024 * 1024

NT = (((1,), (1,)), ((), ()))
TN = (((0,), (0,)), ((), ()))
ANY = pl.BlockSpec(memory_space=pl.ANY)
WHOLE = pl.BlockSpec(memory_space=pltpu.VMEM)
MESH = pl.DeviceIdType.MESH


def _sigmoid(x):
    return 1.0 / (1.0 + jnp.exp(-x))


def _gelu_parts(x):
    c = math.sqrt(2.0 / math.pi)
    t = jnp.tanh(c * (x + 0.044715 * (x * x * x)))
    gelu = 0.5 * x * (1.0 + t)
    dgelu = 0.5 * (1.0 + t) + 0.5 * x * (1.0 - t * t) * (c * (1.0 + 3.0 * 0.044715 * (x * x)))
    return gelu, dgelu


def _expm1(x):
    poly = x * (1.0 + x * (1.0 / 2) * (1.0 + x * (1.0 / 3) * (1.0 + x * (1.0 / 4) * (1.0 + x * (1.0 / 5) * (1.0 + x * (1.0 / 6))))))
    return jnp.where(jnp.abs(x) < 0.25, poly, jnp.exp(x) - 1.0)


def _softplus(x):
    return jnp.maximum(x, 0.0) + jnp.log1p(jnp.exp(-jnp.abs(x)))


class _Piece:
    N_REMOTE = {"gather": 7, "to_sibling": N_CHIP, "to_chips": 3}
    N_LOCAL = {"gather": 1, "to_sibling": 0, "to_chips": 1}

    def __init__(self, kind, src, dest, lo, hi):
        self.kind, self.src, self.dest, self.lo, self.hi = kind, src, dest, lo, hi


class _Job:
    def __init__(self, pieces):
        self.pieces = list(pieces)
        self.ins = [p.src for p in self.pieces]
        self.out_shapes = [SDS(p.dest.shape, p.dest.dtype) for p in self.pieces]
        self.aliased = [i for i, p in enumerate(self.pieces) if not isinstance(p.dest, SDS)]
        self.n_remote = sum(_Piece.N_REMOTE[p.kind] for p in self.pieces)
        self.n_local = max(sum(_Piece.N_LOCAL[p.kind] for p in self.pieces), 1)

    def _each(self, step, ins, outs, send_sems, recv_sems, local_sems):
        remote = local = 0
        for p, src, dst in zip(self.pieces, ins, outs):
            _EXCHANGES[p.kind](step, p, src, dst, send_sems, recv_sems, local_sems, remote, local)
            remote += _Piece.N_REMOTE[p.kind]
            local += _Piece.N_LOCAL[p.kind]

    def start(self, *refs):
        self._each("start", *refs)

    def finish(self, *refs):
        self._each("relay", *refs)
        self._each("finish", *refs)


def _call(body, *, name, args, out_shape, in_specs, out_specs, grid=(), scratch_shapes=(), aliases=None, job=None):
    single = not isinstance(out_shape, (tuple, list))
    out_shape = (out_shape,) if single else tuple(out_shape)
    out_specs = (out_specs,) if single else tuple(out_specs)
    aliases = dict(aliases or {})
    params = pltpu.CompilerParams(dimension_semantics=("arbitrary",) * len(grid) if grid else None,
                                  vmem_limit_bytes=VMEM_LIMIT)
    if job is None:
        res = pl.pallas_call(body, out_shape=out_shape, grid=grid, in_specs=list(in_specs), out_specs=out_specs,
                             scratch_shapes=list(scratch_shapes), input_output_aliases=aliases, name=name,
                             compiler_params=params)(*args)
        return res[0] if single else res

    n_in, n_out, n_scr = len(args), len(out_shape), len(scratch_shapes)
    j_in, j_out, j_alias = len(job.ins), len(job.out_shapes), len(job.aliased)

    def hosted(*refs):
        ins, refs = refs[:n_in], refs[n_in:]
        j_ins, refs = refs[:j_in], refs[j_in + j_alias:]
        outs, refs = refs[:n_out], refs[n_out:]
        j_outs, refs = refs[:j_out], refs[j_out:]
        scr, sems = refs[:n_scr], refs[n_scr:]
        if grid:
            first = last = None
            for axis, size in enumerate(grid):
                at_first, at_last = pl.program_id(axis) == 0, pl.program_id(axis) == size - 1
                first = at_first if first is None else first & at_first
                last = at_last if last is None else last & at_last
            pl.when(first)(lambda: job.start(j_ins, j_outs, *sems))
            body(*ins, *outs, *scr)
            pl.when(last)(lambda: job.finish(j_ins, j_outs, *sems))
        else:
            job.start(j_ins, j_outs, *sems)
            body(*ins, *outs, *scr)
            job.finish(j_ins, j_outs, *sems)

    res = pl.pallas_call(
        hosted, out_shape=out_shape + tuple(job.out_shapes), grid=grid,
        in_specs=list(in_specs) + [ANY] * (j_in + j_alias), out_specs=out_specs + (ANY,) * j_out,
        scratch_shapes=list(scratch_shapes) + [pltpu.SemaphoreType.DMA((job.n_remote,)),
                                               pltpu.SemaphoreType.DMA((job.n_remote,)),
                                               pltpu.SemaphoreType.DMA((job.n_local,))],
        input_output_aliases={**aliases, **{n_in + j_in + k: n_out + i for k, i in enumerate(job.aliased)}},
        name=name, compiler_params=params)(*args, *job.ins, *[job.pieces[i].dest for i in job.aliased])
    own, carried = res[:n_out], res[n_out:]
    return (own[0] if single else own), carried


def _run_job(job, name):
    return _call(lambda: None, name=name, args=[], out_shape=(), in_specs=[], out_specs=(), job=job)[1]


def _position():
    return lax.axis_index("x"), lax.axis_index("y"), lax.axis_index("c")


def _flat(px, py, pc):
    return 4 * px + 2 * py + pc


def _gather_exchange(step, p, src, dst, send_sems, recv_sems, local_sems, r0, l0):
    x, y, c = _position()
    me, sibling = (x, y, c), (x, y, 1 - c)
    chips = [(1 - x, y), (x, 1 - y), (1 - x, 1 - y)]
    rb, n_rows = p.src.shape[0], p.hi - p.lo
    mine = src.at[pl.ds(p.lo, n_rows), :]

    def rows(block):
        return dst.at[pl.ds(_flat(*block) * rb + p.lo, n_rows), :]

    def copy(k, block, to, own=False):
        return pltpu.make_async_remote_copy(
            src_ref=mine if own else rows(block), dst_ref=rows(block),
            send_sem=send_sems.at[r0 + k], recv_sem=recv_sems.at[r0 + k], device_id=to, device_id_type=MESH)

    local = pltpu.make_async_copy(mine, rows(me), local_sems.at[l0])
    if step == "start":
        local.start()
        copy(0, me, sibling, own=True).start()
        for j, chip in enumerate(chips):
            copy(1 + j, me, (*chip, c), own=True).start()
    elif step == "relay":
        for j, chip in enumerate(chips):
            copy(1 + j, (*chip, c), me).wait_recv()
            copy(4 + j, (*chip, c), sibling).start()
    else:
        copy(0, sibling, me).wait_recv()
        for j, chip in enumerate(chips):
            copy(4 + j, (*chip, 1 - c), me).wait_recv()
        copy(0, me, sibling, own=True).wait_send()
        for j, chip in enumerate(chips):
            copy(1 + j, me, (*chip, c), own=True).wait_send()
            copy(4 + j, (*chip, c), sibling).wait_send()
        local.wait()


def _sibling_exchange(step, p, src, dst, send_sems, recv_sems, local_sems, r0, l0):
    x, y, c = _position()
    rb, n_rows = p.src.shape[0] // N_DEV, p.hi - p.lo
    for q in range(N_CHIP):
        copy = pltpu.make_async_remote_copy(
            src_ref=src.at[pl.ds((2 * q + 1 - c) * rb + p.lo, n_rows), :],
            dst_ref=dst.at[pl.ds(q * rb + p.lo, n_rows), :],
            send_sem=send_sems.at[r0 + q], recv_sem=recv_sems.at[r0 + q], device_id=(x, y, 1 - c), device_id_type=MESH)
        if step == "start":
            copy.start()
        elif step == "finish":
            copy.wait()


CHIP_FLIPS = [(1, 0), (0, 1), (1, 1)]


def _chips_exchange(step, p, src, dst, send_sems, recv_sems, local_sems, r0, l0):
    x, y, c = _position()
    rb, n_rows = p.src.shape[0] // N_CHIP, p.hi - p.lo

    def slot(ref, px, py):
        return ref.at[pl.ds((2 * px + py) * rb + p.lo, n_rows), :]

    def copy(k, landing=False):
        px = 1 - x if CHIP_FLIPS[k][0] else x
        py = 1 - y if CHIP_FLIPS[k][1] else y
        return pltpu.make_async_remote_copy(
            src_ref=slot(dst, px, py) if landing else slot(src, px, py),
            dst_ref=slot(dst, px, py) if landing else slot(dst, x, y),
            send_sem=send_sems.at[r0 + k], recv_sem=recv_sems.at[r0 + k], device_id=(px, py, c), device_id_type=MESH)

    local = pltpu.make_async_copy(slot(src, x, y), slot(dst, x, y), local_sems.at[l0])
    if step == "start":
        local.start()
        for k in range(3):
            copy(k).start()
    elif step == "finish":
        for k in range(3):
            copy(k, landing=True).wait_recv()
        for k in range(3):
            copy(k).wait_send()
        local.wait()


_EXCHANGES = {"gather": _gather_exchange, "to_sibling": _sibling_exchange, "to_chips": _chips_exchange}


def _gathered(shard):
    return SDS((N_DEV * shard.shape[0], shard.shape[1]), shard.dtype)


def _split(rows, parts):
    cuts = [rows * k // parts // 16 * 16 for k in range(parts)] + [rows]
    return list(zip(cuts[:-1], cuts[1:]))


def _pair_sum(g, from_sibling, name):
    rb, n = g.shape[0] // N_DEV, g.shape[1]
    tr = rb if rb * n * 2 <= 3 * 1024 * 1024 else rb // 2
    core = lax.axis_index("c").astype(jnp.int32).reshape(1)

    def body(c_ref, g_ref, r_ref, o_ref):
        o_ref[...] = (g_ref[...].astype(F32) + r_ref[...].astype(F32)).astype(BF16)

    grid_spec = pltpu.PrefetchScalarGridSpec(
        num_scalar_prefetch=1, grid=(N_CHIP, rb // tr),
        in_specs=[pl.BlockSpec((None, None, tr, n), lambda q, i, c_ref: (q, c_ref[0], i, 0)),
                  pl.BlockSpec((None, tr, n), lambda q, i, c_ref: (q, i, 0))],
        out_specs=pl.BlockSpec((None, tr, n), lambda q, i, c_ref: (q, i, 0)))
    out = pl.pallas_call(
        body, grid_spec=grid_spec, out_shape=SDS((N_CHIP, rb, n), BF16), name=name,
        compiler_params=pltpu.CompilerParams(dimension_semantics=("arbitrary",) * 2, vmem_limit_bytes=VMEM_LIMIT))(
            core, g.reshape(N_CHIP, 2, rb, n), from_sibling.reshape(N_CHIP, rb, n))
    return out.reshape(N_CHIP * rb, n)


def _sum_devices(gathered, name):
    r = gathered.shape[0] // N_DEV

    def body(g_ref, o_ref):
        acc = g_ref[0]
        for s in range(1, N_DEV):
            acc = acc + g_ref[s]
        o_ref[...] = acc

    return _call(body, name=name, args=[gathered.reshape(N_DEV, r, LANES)], out_shape=SDS((r, LANES), F32),
                 in_specs=[WHOLE], out_specs=WHOLE)


def _cast_rows(w, name):
    def body(w_ref, o_ref):
        o_ref[...] = w_ref[...].astype(BF16)

    return _call(body, name=name, args=[w], out_shape=SDS(w.shape, BF16), in_specs=[WHOLE], out_specs=WHOLE)


def _cast_transposed(w, name):
    d, n = w.shape
    td = 512

    def body(w_ref, o_ref):
        o_ref[...] = w_ref[...].T.astype(BF16)

    return _call(body, name=name, args=[w], out_shape=SDS((n, d), BF16), grid=(d // td,),
                 in_specs=[pl.BlockSpec((td, n), lambda i: (i, 0))], out_specs=pl.BlockSpec((n, td), lambda i: (0, i)))


ROW_TILE = 256


def _rmsnorm_fwd(h, gain, name):
    t, d = h.shape

    def body(h_ref, g_ref, u_ref):
        x = h_ref[...]
        u_ref[...] = (x * lax.rsqrt(jnp.mean(x * x, axis=-1, keepdims=True) + NORM_EPS) * g_ref[...]).astype(BF16)

    row = pl.BlockSpec((ROW_TILE, d), lambda i: (i, 0))
    return _call(body, name=name, args=[h, gain], out_shape=SDS((t, d), BF16), grid=(t // ROW_TILE,),
                 in_specs=[row, pl.BlockSpec((1, d), lambda i: (0, 0))], out_specs=row)


def _rms_bwd_math(x, gain, dy):
    rstd = lax.rsqrt(jnp.mean(x * x, axis=-1, keepdims=True) + NORM_EPS)
    xhat = x * rstd
    dxh = dy * gain
    dx = rstd * (dxh - xhat * jnp.mean(dxh * xhat, axis=-1, keepdims=True))
    return dx, jnp.sum(dy * xhat, axis=0, keepdims=True)


def _rmsnorm_bwd(du, h, gain, resid, bf_scale, name, job=None):
    t, d = h.shape

    def body(du_ref, h_ref, g_ref, r_ref, dh_ref, dhb_ref, dg_ref):
        @pl.when(pl.program_id(0) == 0)
        def _():
            dg_ref[...] = jnp.zeros_like(dg_ref)

        dx, dg = _rms_bwd_math(h_ref[...], g_ref[...], du_ref[...])
        dh = r_ref[...] + dx
        dh_ref[...] = dh
        dhb_ref[...] = (bf_scale * dh).astype(BF16)
        dg_ref[...] += dg

    row = pl.BlockSpec((ROW_TILE, d), lambda i: (i, 0))
    vec = pl.BlockSpec((1, d), lambda i: (0, 0))
    return _call(body, name=name, args=[du, h, gain, resid],
                 out_shape=(SDS((t, d), F32), SDS((t, d), BF16), SDS((1, d), F32)), grid=(t // ROW_TILE,),
                 in_specs=[row, row, vec, row], out_specs=(row, row, vec), job=job)


def _final_loss(h, gain, target, name):
    t, d = h.shape

    def body(h_ref, g_ref, t_ref, dh_ref, dhb_ref, loss_ref, dg_ref):
        @pl.when(pl.program_id(0) == 0)
        def _():
            dg_ref[...] = jnp.zeros_like(dg_ref)
            loss_ref[...] = jnp.zeros_like(loss_ref)

        x = h_ref[...]
        gain = g_ref[...]
        out = x * lax.rsqrt(jnp.mean(x * x, axis=-1, keepdims=True) + NORM_EPS) * gain
        err = out - t_ref[...]
        loss_ref[...] += 0.5 * jnp.sum(jnp.mean(err * err, axis=-1, keepdims=True), axis=0, keepdims=True)
        dx, dg = _rms_bwd_math(x, gain, err * (1.0 / d))
        dh_ref[...] = dx
        dhb_ref[...] = (0.5 * dx).astype(BF16)
        dg_ref[...] += dg

    row = pl.BlockSpec((ROW_TILE, d), lambda i: (i, 0))
    vec = pl.BlockSpec((1, d), lambda i: (0, 0))
    one = pl.BlockSpec((SUBLANES, LANES), lambda i: (0, 0))
    return _call(body, name=name, args=[h, gain, target],
                 out_shape=(SDS((t, d), F32), SDS((t, d), BF16), SDS((SUBLANES, LANES), F32), SDS((1, d), F32)),
                 grid=(t // ROW_TILE,), in_specs=[row, vec, row], out_specs=(row, row, one, vec))


def _mixnorm_fwd(ya, yb, ga, gb, name):
    t, c = ya.shape

    def body(ya_ref, yb_ref, ga_ref, gb_ref, y_ref, yt_ref):
        for k, (src, g_ref) in enumerate(((ya_ref, ga_ref), (yb_ref, gb_ref))):
            x = src[...]
            u = x * lax.rsqrt(jnp.mean(x * x, axis=-1, keepdims=True) + NORM_EPS) * g_ref[...]
            y_ref[:, k * c:(k + 1) * c] = u.astype(BF16)
            yt_ref[k * c:(k + 1) * c, :] = u.T.astype(BF16)

    row = pl.BlockSpec((ROW_TILE, c), lambda i: (i, 0))
    vec = pl.BlockSpec((1, c), lambda i: (0, 0))
    return _call(body, name=name, args=[ya, yb, ga, gb],
                 out_shape=(SDS((t, 2 * c), BF16), SDS((2 * c, t), BF16)), grid=(t // ROW_TILE,),
                 in_specs=[row, row, vec, vec],
                 out_specs=(pl.BlockSpec((ROW_TILE, 2 * c), lambda i: (i, 0)),
                            pl.BlockSpec((2 * c, ROW_TILE), lambda i: (0, i))))


def _mixnorm_bwd(dy, ya, yb, ga, gb, name):
    t, c = ya.shape

    def body(dy_ref, ya_ref, yb_ref, ga_ref, gb_ref, dya_ref, dyb_ref, dga_ref, dgb_ref):
        @pl.when(pl.program_id(0) == 0)
        def _():
            dga_ref[...] = jnp.zeros_like(dga_ref)
            dgb_ref[...] = jnp.zeros_like(dgb_ref)

        dxa, dga = _rms_bwd_math(ya_ref[...], ga_ref[...], dy_ref[:, :c])
        dxb, dgb = _rms_bwd_math(yb_ref[...], gb_ref[...], dy_ref[:, c:])
        dya_ref[...] = dxa
        dyb_ref[...] = dxb
        dga_ref[...] += dga
        dgb_ref[...] += dgb

    row = pl.BlockSpec((ROW_TILE, c), lambda i: (i, 0))
    vec = pl.BlockSpec((1, c), lambda i: (0, 0))
    return _call(body, name=name, args=[dy, ya, yb, ga, gb],
                 out_shape=(SDS((t, c), F32), SDS((t, c), F32), SDS((1, c), F32), SDS((1, c), F32)),
                 grid=(t // ROW_TILE,),
                 in_specs=[pl.BlockSpec((ROW_TILE, 2 * c), lambda i: (i, 0)), row, row, vec, vec],
                 out_specs=(row, row, vec, vec))


def _tile(n, want):
    return max(t for t in range(LANES, min(n, want) + 1, LANES) if n % t == 0)


def _mm(a, b, *, nt, out_dtype, tm, tn, name, residual=None, lead=None, out_rows=None, row_offset=0, into=None,
        job=None):
    m, k = a.shape[-2:]
    n = b.shape[0] if nt else b.shape[1]
    tm, tn = _tile(m, tm), _tile(n, tn)
    out_rows = m if out_rows is None else out_rows

    def body(a_ref, b_ref, *rest):
        o_ref = rest[-1]
        av, bv = a_ref[...].astype(BF16), b_ref[...].astype(BF16)
        if nt:
            out = lax.dot_general(av, bv, NT, preferred_element_type=F32)
        else:
            out = jnp.dot(av, bv, preferred_element_type=F32)
        if residual is not None:
            out = rest[0][...] + out
        o_ref[...] = out.astype(out_dtype)

    a_spec = (pl.BlockSpec((tm, k), lambda i, j: (i, 0)) if lead is None
              else pl.BlockSpec((None, tm, k), lambda i, j: (lead, i, 0)))
    in_specs = [a_spec, pl.BlockSpec((tn, k), lambda i, j: (j, 0)) if nt else pl.BlockSpec((k, tn), lambda i, j: (0, j))]
    args, aliases = [a, b], {}
    if residual is not None:
        in_specs.append(pl.BlockSpec((tm, tn), lambda i, j: (i, j)))
        args.append(residual)
    if into is not None:
        in_specs.append(ANY)
        aliases[len(args)] = 0
        args.append(into)
    return _call(body, name=name, args=args, out_shape=SDS((out_rows, n), out_dtype), grid=(m // tm, n // tn),
                 in_specs=in_specs, out_specs=pl.BlockSpec((tm, tn), lambda i, j: (row_offset // tm + i, j)),
                 aliases=aliases, job=job)


FFN_TM = 512
FFN_HB = 512


def _ffn_fwd(h, u, w_in_t, w_out, name, job=None):
    t, d = h.shape
    f = w_out.shape[0]
    nk = f // FFN_HB

    def body(u_ref, w_ref, wo_ref, h_ref, hn_ref, g_ref, up_ref, acc):
        k = pl.program_id(1)

        @pl.when(k == 0)
        def _():
            acc[...] = jnp.zeros_like(acc)

        uu = u_ref[...]
        g = lax.dot_general(uu, w_ref[0], NT, preferred_element_type=F32)
        up = lax.dot_general(uu, w_ref[1], NT, preferred_element_type=F32)
        g_ref[...] = g
        up_ref[...] = up
        hid = (g * _sigmoid(g)) * up
        acc[...] += jnp.dot(hid.astype(BF16), wo_ref[...], preferred_element_type=F32)

        @pl.when(k == nk - 1)
        def _():
            hn_ref[...] = h_ref[...] + 0.5 * acc[...]

    tok = pl.BlockSpec((FFN_TM, d), lambda i, k: (i, 0))
    pre = pl.BlockSpec((FFN_TM, FFN_HB), lambda i, k: (i, k))
    return _call(body, name=name, args=[u, w_in_t.reshape(2, f, d), w_out, h],
                 out_shape=(SDS((t, d), F32), SDS((t, f), F32), SDS((t, f), F32)), grid=(t // FFN_TM, nk),
                 in_specs=[tok, pl.BlockSpec((2, FFN_HB, d), lambda i, k: (0, k, 0)),
                           pl.BlockSpec((FFN_HB, d), lambda i, k: (k, 0)), tok],
                 out_specs=(tok, pre, pre), scratch_shapes=[pltpu.VMEM((FFN_TM, d), F32)], job=job)


def _ffn_bwd(dfb, gpre, upre, w_in_t, w_out, name, job=None):
    t, d = dfb.shape
    f = w_out.shape[0]
    nk = f // FFN_HB

    def body(df_ref, g_ref, up_ref, w_ref, wo_ref, du_ref, hid_t_ref, da_t_ref, acc):
        k = pl.program_id(1)

        @pl.when(k == 0)
        def _():
            acc[...] = jnp.zeros_like(acc)

        dhid = lax.dot_general(df_ref[...], wo_ref[...], NT, preferred_element_type=F32)
        g, up = g_ref[...], up_ref[...]
        sig = _sigmoid(g)
        silu = g * sig
        dup = dhid * silu
        dg = dhid * up * (sig * (1.0 + g * (1.0 - sig)))
        hid_t_ref[...] = (silu * up).T.astype(BF16)
        da_t_ref[0] = dg.T.astype(BF16)
        da_t_ref[1] = dup.T.astype(BF16)
        acc[...] += (jnp.dot(dg.astype(BF16), w_ref[0], preferred_element_type=F32)
                     + jnp.dot(dup.astype(BF16), w_ref[1], preferred_element_type=F32))

        @pl.when(k == nk - 1)
        def _():
            du_ref[...] = acc[...]

    tok = pl.BlockSpec((FFN_TM, d), lambda i, k: (i, 0))
    pre = pl.BlockSpec((FFN_TM, FFN_HB), lambda i, k: (i, k))
    return _call(body, name=name, args=[dfb, gpre, upre, w_in_t.reshape(2, f, d), w_out],
                 out_shape=(SDS((t, d), F32), SDS((f, t), BF16), SDS((2, f, t), BF16)), grid=(t // FFN_TM, nk),
                 in_specs=[tok, pre, pre, pl.BlockSpec((2, FFN_HB, d), lambda i, k: (0, k, 0)),
                           pl.BlockSpec((FFN_HB, d), lambda i, k: (k, 0))],
                 out_specs=(tok, pl.BlockSpec((FFN_HB, FFN_TM), lambda i, k: (k, i)),
                            pl.BlockSpec((2, FFN_HB, FFN_TM), lambda i, k: (0, k, i))),
                 scratch_shapes=[pltpu.VMEM((FFN_TM, d), F32)], job=job)


CH = LANES
PAD = SUBLANES


def _lru_gates(xc, gw_ref, gb_ref, lam_ref, z):
    xcb = xc.astype(BF16)
    r = _sigmoid(jnp.dot(xcb, gw_ref[2 * z], preferred_element_type=F32) + gb_ref[pl.ds(2 * z, 1), :])
    i = _sigmoid(jnp.dot(xcb, gw_ref[2 * z + 1], preferred_element_type=F32) + gb_ref[pl.ds(2 * z + 1, 1), :])
    sp = _softplus(-lam_ref[pl.ds(z, 1), :])
    log_a = (-RG_C * r) * sp
    a = jnp.exp(log_a)
    mult = jnp.sqrt(-_expm1(2.0 * log_a))
    return r, i, sp, a, mult


def _conv(xpad, cw_ref, cb_ref, t):
    xc = cb_ref[...] + cw_ref[pl.ds(0, 1), :] * xpad[pl.ds(PAD - 2, t), :]
    for j in range(1, CONV_WIDTH):
        xc = xc + cw_ref[pl.ds(j, 1), :] * xpad[pl.ds(PAD - 2 + j, t), :]
    return xc


def _fill_padded(pad_ref, value, t):
    pad_ref[pl.ds(0, PAD), :] = jnp.zeros((PAD, CH), F32)
    pad_ref[pl.ds(PAD + t, PAD), :] = jnp.zeros((PAD, CH), F32)
    pad_ref[pl.ds(PAD, t), :] = value


def _scan_pair(t, a_up, b_up, out_up, a_down, b_down, out_down):
    def step(tt, carry):
        hu, hd = carry
        lo = pl.multiple_of(tt * SUBLANES, SUBLANES)
        hi = pl.multiple_of(t - SUBLANES - tt * SUBLANES, SUBLANES)
        for j in range(SUBLANES):
            su, sd = pl.ds(lo + j, 1), pl.ds(hi + SUBLANES - 1 - j, 1)
            hu = a_up(su) * hu + b_up(su)
            out_up[su, :] = hu
            hd = a_down(sd) * hd + b_down(sd)
            out_down[sd, :] = hd
        return hu, hd

    zero = jnp.zeros((1, CH), F32)
    lax.fori_loop(0, t // SUBLANES, step, (zero, zero))


def _lru_fwd(proj, cw, cb, gw, gb, lam, name, job=None):
    t = proj.shape[0]
    c = cw.shape[1]
    ncb = c // CH

    def body(x_ref, g_ref, cw_ref, cb_ref, gw_ref, gb_ref, lam_ref, ya_ref, hf_ref, hb_ref, xpad, a0, b0, a1, b1):
        _fill_padded(xpad, x_ref[...], t)
        xc = _conv(xpad, cw_ref, cb_ref, t)
        for z, (a_s, b_s) in enumerate(((a0, b0), (a1, b1))):
            _, i, _, a, mult = _lru_gates(xc, gw_ref, gb_ref, lam_ref, z)
            a_s[...] = a
            b_s[...] = mult * (i * xc)
        _scan_pair(t, lambda s: a0[s, :], lambda s: b0[s, :], hf_ref, lambda s: a1[s, :], lambda s: b1[s, :], hb_ref)
        gelu, _ = _gelu_parts(g_ref[...])
        ya_ref[...] = gelu * (hf_ref[...] + hb_ref[...])

    col = lambda off: pl.BlockSpec((t, CH), lambda i: (0, off + i))
    small = lambda rows: pl.BlockSpec((rows, CH), lambda i: (0, i))
    return _call(body, name=name, args=[proj, proj, cw, cb, gw, gb, lam], out_shape=(SDS((t, c), F32),) * 3,
                 grid=(ncb,),
                 in_specs=[col(0), col(ncb), small(CONV_WIDTH), small(1),
                           pl.BlockSpec((4, None, CH, CH), lambda i: (0, i, 0, 0)), small(4), small(2)],
                 out_specs=(col(0),) * 3,
                 scratch_shapes=[pltpu.VMEM((t + 2 * PAD, CH), F32)] + [pltpu.VMEM((t, CH), F32)] * 4, job=job)


def _lru_bwd(proj, cw, cb, gw, gb, lam, hf, hb, dya, name, job=None):
    t = proj.shape[0]
    c = cw.shape[1]
    ncb = c // CH

    def body(x_ref, g_ref, cw_ref, cb_ref, gw_ref, gb_ref, lam_ref, hf_ref, hb_ref, dya_ref,
             dx_ref, dg_ref, dt_ref, dcw_ref, dcb_ref, dgw_ref, dgb_ref, dlam_ref,
             xpad, hpad, dxc, a0, a1, dhs, dh0, dh1):
        _fill_padded(xpad, x_ref[...], t)
        xc = _conv(xpad, cw_ref, cb_ref, t)
        xcb = xc.astype(BF16)
        gates = [_lru_gates(xc, gw_ref, gb_ref, lam_ref, z) for z in range(2)]
        a0[...] = gates[0][3]
        a1[...] = gates[1][3]

        gelu, dgelu = _gelu_parts(g_ref[...])
        dya = dya_ref[...]
        dgate = dya * (hf_ref[...] + hb_ref[...]) * dgelu
        dg_ref[...] = dgate.astype(BF16)
        dt_ref[1] = dgate.T.astype(BF16)
        dhs[...] = dya * gelu

        def step(tt, carry):
            c0, p0, c1, p1 = carry
            lo = pl.multiple_of(tt * SUBLANES, SUBLANES)
            hi = pl.multiple_of(t - SUBLANES - tt * SUBLANES, SUBLANES)
            for j in range(SUBLANES):
                su, sd = pl.ds(lo + j, 1), pl.ds(hi + SUBLANES - 1 - j, 1)
                c0 = dhs[sd, :] + p0 * c0
                dh0[sd, :] = c0
                p0 = a0[sd, :]
                c1 = dhs[su, :] + p1 * c1
                dh1[su, :] = c1
                p1 = a1[su, :]
            return c0, p0, c1, p1

        zero = jnp.zeros((1, CH), F32)
        lax.fori_loop(0, t // SUBLANES, step, (zero, zero, zero, zero))

        acc_dxc = jnp.zeros((t, CH), F32)
        for z, (h_ref, dh_ref, shift) in enumerate(((hf_ref, dh0, -1), (hb_ref, dh1, 1))):
            r, i, sp, a, mult = gates[z]
            _fill_padded(hpad, h_ref[...], t)
            h_nb = hpad[pl.ds(PAD + shift, t), :]
            db = dh_ref[...]
            da = db * h_nb
            d_i = db * mult * xc
            acc_dxc = acc_dxc + db * mult * i
            d_mult = db * i * xc
            d_la = da * a - d_mult * (a * a) / mult
            d_r = d_la * (-RG_C * sp)
            dlam_ref[pl.ds(z, 1), :] = (jnp.sum(d_la * (-RG_C * r), axis=0, keepdims=True)
                                        * (-_sigmoid(-lam_ref[pl.ds(z, 1), :])))
            for gate, d_pre in ((0, d_r * r * (1.0 - r)), (1, d_i * i * (1.0 - i))):
                zg = 2 * z + gate
                dgb_ref[pl.ds(zg, 1), :] = jnp.sum(d_pre, axis=0, keepdims=True)
                d_pre_b = d_pre.astype(BF16)
                dgw_ref[zg] = lax.dot_general(xcb, d_pre_b, TN, preferred_element_type=F32)
                acc_dxc = acc_dxc + lax.dot_general(d_pre_b, gw_ref[zg], NT, preferred_element_type=F32)

        dcb_ref[...] = jnp.sum(acc_dxc, axis=0, keepdims=True)
        for j in range(CONV_WIDTH):
            dcw_ref[pl.ds(j, 1), :] = jnp.sum(acc_dxc * xpad[pl.ds(PAD - 2 + j, t), :], axis=0, keepdims=True)
        _fill_padded(dxc, acc_dxc, t)
        dx = cw_ref[pl.ds(0, 1), :] * dxc[pl.ds(PAD + 2, t), :]
        for j in range(1, CONV_WIDTH):
            dx = dx + cw_ref[pl.ds(j, 1), :] * dxc[pl.ds(PAD + 2 - j, t), :]
        dx_ref[...] = dx.astype(BF16)
        dt_ref[0] = dx.T.astype(BF16)

    col = lambda off: pl.BlockSpec((t, CH), lambda i: (0, off + i))
    small = lambda rows: pl.BlockSpec((rows, CH), lambda i: (0, i))
    dense = pl.BlockSpec((4, None, CH, CH), lambda i: (0, i, 0, 0))
    padded = pltpu.VMEM((t + 2 * PAD, CH), F32)
    return _call(
        body, name=name, args=[proj, proj, cw, cb, gw, gb, lam, hf, hb, dya],
        out_shape=(SDS((t, c), BF16), SDS((t, c), BF16), SDS((2, c, t), BF16), SDS((CONV_WIDTH, c), F32),
                   SDS((1, c), F32), SDS((4, ncb, CH, CH), F32), SDS((4, c), F32), SDS((2, c), F32)),
        grid=(ncb,),
        in_specs=[col(0), col(ncb), small(CONV_WIDTH), small(1), dense, small(4), small(2), col(0), col(0), col(0)],
        out_specs=(col(0), col(0), pl.BlockSpec((2, CH, t), lambda i: (0, i, 0)), small(CONV_WIDTH), small(1),
                   dense, small(4), small(2)),
        scratch_shapes=[padded, padded, padded] + [pltpu.VMEM((t, CH), F32)] * 5, job=job)


def _band_start(r, rows):
    return jnp.clip(r - WIN_ROWS // 2, 0, rows - WIN_ROWS)


def _bias_tables(rpb):
    cols = np.arange(GRID_W)
    start = np.clip(cols - WIN_COLS // 2, 0, GRID_W - WIN_COLS)
    valid = (cols[None, :] >= start[:, None]) & (cols[None, :] < start[:, None] + WIN_COLS)
    col_off = np.clip(cols[None, :] - cols[:, None] + WIN_COLS - 1, 0, 2 * WIN_COLS - 2)
    row_off = np.arange(WIN_ROWS)[None, :] - np.arange(WIN_ROWS)[:, None] + WIN_ROWS - 1
    pick_row = jnp.asarray(np.eye(2 * WIN_ROWS - 1, dtype=np.float32)[row_off])
    pick_col = jnp.asarray(np.eye(2 * WIN_COLS - 1, dtype=np.float32)[col_off] * valid[..., None])
    hi = lax.Precision.HIGHEST
    by_row = jnp.einsum("hrc,ajr->hajc", rpb, pick_row, precision=hi)
    table = jnp.einsum("hajc,qkc->haqjk", by_row, pick_col, precision=hi)
    table = jnp.where(jnp.asarray(valid)[None, None, :, None, :], table, NEG)
    return table.reshape(rpb.shape[0], WIN_ROWS, GRID_W, BAND)


def _attn_scores(q_ref, k_ref, bm_ref, hh, r, rows):
    rs = _band_start(r, rows)
    lanes = pl.ds(hh * HEAD_DIM, HEAD_DIM)
    qrows = pl.ds(pl.multiple_of(r * GRID_W, GRID_W), GRID_W)
    band = pl.ds(pl.multiple_of(rs * GRID_W, GRID_W), BAND)
    q = q_ref[qrows, lanes].astype(BF16)
    kb = k_ref[band, lanes].astype(BF16)
    s = lax.dot_general(q, kb, NT, preferred_element_type=F32) * (HEAD_DIM ** -0.5) + bm_ref[hh, r - rs]
    p = jnp.exp(s - jnp.max(s, axis=-1, keepdims=True))
    p = p / jnp.sum(p, axis=-1, keepdims=True)
    return q, kb, p, qrows, band, lanes, r - rs


def _attn_fwd(proj, tables, width, name, job=None):
    t = proj.shape[0]
    rows = t // GRID_W
    npair = width // LANES
    first = (proj.shape[1] - 3 * width) // LANES

    def body(q_ref, k_ref, v_ref, bm_ref, o_ref, qs, ks, vs):
        qs[...] = q_ref[...].astype(BF16)
        ks[...] = k_ref[...].astype(BF16)
        vs[...] = v_ref[...].astype(BF16)

        def row(r, carry):
            for hh in range(2):
                _, _, p, qrows, band, lanes, _ = _attn_scores(qs, ks, bm_ref, hh, r, rows)
                o_ref[qrows, lanes] = jnp.dot(p.astype(BF16), vs[band, lanes], preferred_element_type=F32)
            return carry

        lax.fori_loop(0, rows, row, 0, unroll=2)

    col = lambda off: pl.BlockSpec((t, LANES), lambda i: (0, off + i))
    return _call(body, name=name, args=[proj, proj, proj, tables], out_shape=SDS((t, width), F32), grid=(npair,),
                 in_specs=[col(first), col(first + npair), col(first + 2 * npair),
                           pl.BlockSpec((2, WIN_ROWS, GRID_W, BAND), lambda i: (i, 0, 0, 0))],
                 out_specs=col(0), scratch_shapes=[pltpu.VMEM((t, LANES), BF16)] * 3, job=job)


def _attn_bwd(proj, tables, dyb, name, job=None):
    t, width = dyb.shape
    rows = t // GRID_W
    npair = width // LANES
    first = (proj.shape[1] - 3 * width) // LANES

    def body(q_ref, k_ref, v_ref, bm_ref, do_ref, dq_ref, dk_ref, dv_ref, dt_ref, dbm_ref, dq_s, dk_s, dv_s,
             qs, ks, vs, dos):
        qs[...] = q_ref[...].astype(BF16)
        ks[...] = k_ref[...].astype(BF16)
        vs[...] = v_ref[...].astype(BF16)
        dos[...] = do_ref[...].astype(BF16)
        dk_s[...] = jnp.zeros_like(dk_s)
        dv_s[...] = jnp.zeros_like(dv_s)
        dbm_ref[...] = jnp.zeros_like(dbm_ref)

        def row(r, carry):
            for hh in range(2):
                q, kb, p, qrows, band, lanes, case = _attn_scores(qs, ks, bm_ref, hh, r, rows)
                do = dos[qrows, lanes]
                dp = lax.dot_general(do, vs[band, lanes], NT, preferred_element_type=F32)
                ds = p * (dp - jnp.sum(dp * p, axis=-1, keepdims=True))
                dbm_ref[hh, case] += ds
                dsb = (ds * (HEAD_DIM ** -0.5)).astype(BF16)
                dq_s[qrows, lanes] = jnp.dot(dsb, kb, preferred_element_type=F32)
                dk_s[band, lanes] += lax.dot_general(dsb, q, TN, preferred_element_type=F32)
                dv_s[band, lanes] += lax.dot_general(p.astype(BF16), do, TN, preferred_element_type=F32)
            return carry

        lax.fori_loop(0, rows, row, 0, unroll=2)
        for n, (src, dst) in enumerate(((dq_s, dq_ref), (dk_s, dk_ref), (dv_s, dv_ref))):
            val = src[...]
            dst[...] = val.astype(BF16)
            dt_ref[n] = val.T.astype(BF16)

    col = lambda off: pl.BlockSpec((t, LANES), lambda i: (0, off + i))
    table = pl.BlockSpec((2, WIN_ROWS, GRID_W, BAND), lambda i: (i, 0, 0, 0))
    return _call(body, name=name, args=[proj, proj, proj, tables, dyb],
                 out_shape=(SDS((t, width), BF16),) * 3 + (SDS((3, width, t), BF16), SDS(tables.shape, F32)),
                 grid=(npair,),
                 in_specs=[col(first), col(first + npair), col(first + 2 * npair), table, col(0)],
                 out_specs=(col(0), col(0), col(0), pl.BlockSpec((3, LANES, t), lambda i: (0, i, 0)), table),
                 scratch_shapes=[pltpu.VMEM((t, LANES), F32)] * 3 + [pltpu.VMEM((t, LANES), BF16)] * 4, job=job)


def _adamw_math(w, g, m, v):
    m = ADAM_B1 * m + (1.0 - ADAM_B1) * g
    v = ADAM_B2 * v + (1.0 - ADAM_B2) * (g * g)
    m_hat = m / (1.0 - ADAM_B1 ** ADAM_STEP)
    v_hat = v / (1.0 - ADAM_B2 ** ADAM_STEP)
    delta = -ADAM_LR * (m_hat / (jnp.sqrt(v_hat) + ADAM_EPS) + ADAM_WD * w)
    return delta, m, v


def _sum_partials(p_ref):
    g = p_ref[0].astype(F32)
    for s in range(1, N_CHIP):
        g = g + p_ref[s].astype(F32)
    return g


def _adamw_rows(w, partials, m, v, name):
    rb, n = w.shape
    tr = 64

    def body(w_ref, p_ref, m_ref, v_ref, g_ref, d_ref, nm_ref, nv_ref):
        g = _sum_partials(p_ref)
        g_ref[...] = g
        d_ref[...], nm_ref[...], nv_ref[...] = _adamw_math(w_ref[...], g, m_ref[...], v_ref[...])

    blk = pl.BlockSpec((tr, n), lambda i: (i, 0))
    return _call(body, name=name, args=[w, partials.reshape(N_CHIP, rb, n), m, v], out_shape=(SDS((rb, n), F32),) * 4,
                 grid=(rb // tr,), in_specs=[blk, pl.BlockSpec((N_CHIP, tr, n), lambda i: (0, i, 0)), blk, blk],
                 out_specs=(blk,) * 4)


def _adamw_cols(w, partials, m, v, name):
    d, nb = w.shape
    td = 256

    def body(w_ref, p_ref, m_ref, v_ref, g_ref, d_ref, nm_ref, nv_ref):
        g = _sum_partials(p_ref).T
        g_ref[...] = g
        d_ref[...], nm_ref[...], nv_ref[...] = _adamw_math(w_ref[...], g, m_ref[...], v_ref[...])

    blk = pl.BlockSpec((td, nb), lambda i: (i, 0))
    return _call(body, name=name, args=[w, partials.reshape(N_CHIP, nb, d), m, v], out_shape=(SDS((d, nb), F32),) * 4,
                 grid=(d // td,), in_specs=[blk, pl.BlockSpec((N_CHIP, nb, td), lambda i: (0, 0, i)), blk, blk],
                 out_specs=(blk,) * 4)


def _adamw_small(w, g, m, v, name):
    def body(w_ref, g_ref, m_ref, v_ref, d_ref, nm_ref, nv_ref):
        d_ref[...], nm_ref[...], nv_ref[...] = _adamw_math(w_ref[...], g_ref[...], m_ref[...], v_ref[...])

    return _call(body, name=name, args=[w, g, m, v], out_shape=(SDS(w.shape, F32),) * 3, in_specs=[WHOLE] * 4,
                 out_specs=(WHOLE,) * 3)


TILE = SUBLANES * LANES


def _pack(arrays):
    parts = []
    for a in arrays:
        flat = a.reshape(-1).astype(F32)
        flat = jnp.pad(flat, (0, -flat.size % TILE))
        parts.append(flat.reshape(-1, LANES))
    return jnp.concatenate(parts, axis=0)


def _unpack(pack, like):
    out, row = [], 0
    for a in like:
        n = int(np.prod(a.shape))
        nrows = -(-n // TILE) * SUBLANES
        out.append(pack[row:row + nrows].reshape(-1)[:n].reshape(a.shape))
        row += nrows
    return out


def _dense_gate_blocks(gate_w):
    w = gate_w.reshape(4, -1, 2, HEAD_DIM, HEAD_DIM)
    zero = jnp.zeros_like(w[:, :, 0])
    top = jnp.concatenate([w[:, :, 0], zero], axis=-1)
    bottom = jnp.concatenate([zero, w[:, :, 1]], axis=-1)
    return jnp.concatenate([top, bottom], axis=-2)


def _diag_gate_blocks(dense, shape):
    even = dense[:, :, :HEAD_DIM, :HEAD_DIM]
    odd = dense[:, :, HEAD_DIM:, HEAD_DIM:]
    return jnp.stack([even, odd], axis=2).reshape(shape)


LARGE = ("ffn1_w_in", "ffn1_w_out", "w_in_mix", "w_out_mix", "ffn2_w_in", "ffn2_w_out")
COLUMN_SHARDED = ("ffn1_w_in", "w_in_mix", "ffn2_w_in")
SHARDED_SMALL = ("lru_conv_w", "lru_lambda")
REPLICATED = ("norm_ffn1", "norm_mix", "lru_conv_b", "lru_gate_w", "lru_gate_b", "attn_rpb", "lru_out_norm",
              "attn_out_norm", "norm_ffn2", "norm_final")
SMALL_ORDER = REPLICATED + SHARDED_SMALL
WEIGHTS = ("norm_ffn1", "ffn1_w_in", "ffn1_w_out", "norm_mix", "w_in_mix", "lru_conv_w", "lru_conv_b", "lru_gate_w",
           "lru_gate_b", "lru_lambda", "attn_rpb", "lru_out_norm", "attn_out_norm", "w_out_mix", "norm_ffn2",
           "ffn2_w_in", "ffn2_w_out", "norm_final")


PARTS = {("gather", "ffn2_w_out"): 2, ("gather", "ffn2_w_in"): 4, ("to_chips", "ffn2_w_in"): 4,
         ("to_chips", "ffn1_w_out"): 2}
CARRIES = {
    "gather_ffn1": [(("gather", "ffn1_w_in"), 1), (("gather", "ffn1_w_out"), 1), (("gather", "small"), 1)],
    "ffn1_fwd": [(("gather", "w_in_mix"), 1), (("gather", "w_out_mix"), 1)],
    "mix_in_proj": [(("gather", "ffn2_w_out"), 1)],
    "lru_fwd": [(("gather", "ffn2_w_out"), 1)],
    "attn_fwd": [(("gather", "ffn2_w_in"), 3)],
    "mix_out_proj": [(("gather", "ffn2_w_in"), 1)],
    "ffn2_in_grad": [(("to_sibling", "ffn2_w_out"), 1)],
    "norm_ffn2_bwd": [(("to_sibling", "ffn2_w_in"), 1)],
    "attn_bwd": [(("to_chips", "ffn2_w_out"), 1), (("to_chips", "ffn2_w_in"), 3)],
    "lru_bwd": [(("to_chips", "ffn2_w_in"), 1), (("to_sibling", "w_out_mix"), 1)],
    "mix_in_grad": [(("to_chips", "w_out_mix"), 1)],
    "mix_in_bwd": [(("to_sibling", "w_in_mix"), 1)],
    "ffn1_bwd": [(("to_chips", "w_in_mix"), 1), (("gather", "small_grads"), 1)],
    "ffn1_in_grad_gate": [(("to_sibling", "ffn1_w_out"), 1)],
    "ffn1_in_grad_up": [(("to_chips", "ffn1_w_out"), 1)],
    "to_sibling_ffn1_in": [(("to_sibling", "ffn1_w_in"), 1), (("to_chips", "ffn1_w_out"), 1)],
    "to_chips_ffn1": [(("to_chips", "ffn1_w_in"), 1), (("gather", "late_grads"), 1)],
}


class _Transfer:
    def __init__(self, kind, src, dest, block_rows, parts):
        self.kind, self.src, self.dest = kind, src, dest
        self.ranges, self.taken = _split(block_rows, parts), 0

    def take(self, count):
        lo, hi = self.ranges[self.taken][0], self.ranges[self.taken + count - 1][1]
        self.taken += count
        return _Piece(self.kind, self.src, self.dest, lo, hi)


class _Traffic:
    def __init__(self):
        self.transfers = {}

    def open(self, kind, name, src):
        if kind == "gather":
            dest, rows = _gathered(src), src.shape[0]
        elif kind == "to_sibling":
            dest, rows = SDS((src.shape[0] // 2, src.shape[1]), src.dtype), src.shape[0] // N_DEV
        else:
            dest, rows = SDS(src.shape, src.dtype), src.shape[0] // N_CHIP
        self.transfers[kind, name] = _Transfer(kind, src, dest, rows, PARTS.get((kind, name), 1))

    def _job(self, host):
        moved = [self.transfers[key] for key, _ in CARRIES[host]]
        return moved, _Job([tr.take(count) for tr, (_, count) in zip(moved, CARRIES[host])])

    def carry(self, host, fn, *args, **kw):
        if host not in CARRIES:
            return fn(*args, name=host, **kw)
        moved, job = self._job(host)
        res, landed = fn(*args, name=host, job=job, **kw)
        for tr, arr in zip(moved, landed):
            tr.dest = arr
        return res

    def alone(self, host):
        moved, job = self._job(host)
        for tr, arr in zip(moved, _run_job(job, host)):
            tr.dest = arr

    def result(self, kind, name):
        tr = self.transfers.pop((kind, name))
        assert tr.taken == len(tr.ranges), (kind, name)
        return tr.dest


def _forward_backward(x, target, shards, sharded_small, s):
    c = s["lru_conv_b"].shape[1]
    width = s["attn_out_norm"].shape[1]
    t = x.shape[0]
    traffic = _Traffic()
    carry = traffic.carry
    weight = lambda n: traffic.result("gather", n)

    for n in LARGE:
        traffic.open("gather", n, shards[n])
    traffic.open("gather", "small", sharded_small)
    traffic.alone("gather_ffn1")
    full_small = weight("small").reshape(N_DEV, SUBLANES, c // N_DEV)
    conv_w = full_small[:, :CONV_WIDTH].transpose(1, 0, 2).reshape(CONV_WIDTH, c)
    lam = full_small[:, CONV_WIDTH:CONV_WIDTH + 2].transpose(1, 0, 2).reshape(2, c)
    w = {n: weight(n) for n in ("ffn1_w_in", "ffn1_w_out")}
    u1 = _rmsnorm_fwd(x, s["norm_ffn1"], "norm_ffn1")
    h1, g1, up1 = carry("ffn1_fwd", _ffn_fwd, x, u1, w["ffn1_w_in"], w["ffn1_w_out"])
    w["w_in_mix"], w["w_out_mix"] = weight("w_in_mix"), weight("w_out_mix")
    u2 = _rmsnorm_fwd(h1, s["norm_mix"], "norm_mix")
    proj = carry("mix_in_proj", _mm, u2, w["w_in_mix"], nt=True, out_dtype=F32, tm=512, tn=512)
    gw = _dense_gate_blocks(s["lru_gate_w"]).astype(BF16)
    gb = s["lru_gate_b"].reshape(4, c)
    tables, tables_vjp = jax.vjp(_bias_tables, s["attn_rpb"])
    ya, hf, hb = carry("lru_fwd", _lru_fwd, proj, conv_w, s["lru_conv_b"], gw, gb, lam)
    yb = carry("attn_fwd", _attn_fwd, proj, tables, width)
    y, yt = _mixnorm_fwd(ya, yb, s["lru_out_norm"], s["attn_out_norm"], "mix_norm")
    h2 = carry("mix_out_proj", _mm, y, w["w_out_mix"], nt=False, out_dtype=F32, tm=512, tn=512, residual=h1)
    u3 = _rmsnorm_fwd(h2, s["norm_ffn2"], "norm_ffn2")
    w["ffn2_w_in"], w["ffn2_w_out"] = weight("ffn2_w_in"), weight("ffn2_w_out")
    h3, g2, up2 = carry("ffn2_fwd", _ffn_fwd, h2, u3, w["ffn2_w_in"], w["ffn2_w_out"])
    dh3, df2, loss_part, d_norm_final = _final_loss(h3, s["norm_final"], target, "final_loss")

    grads = {}
    grad_of = dict(nt=False, out_dtype=BF16, tm=512, tn=1024)

    def reduce_in_chip(n):
        traffic.open("to_sibling", n, grads[n])

    def reduce_over_chips(n):
        traffic.open("to_chips", n, _pair_sum(grads[n], traffic.result("to_sibling", n), "pair_sum_" + n))

    du3, hid2_t, da2_t = carry("ffn2_bwd", _ffn_bwd, df2, g2, up2, w["ffn2_w_in"], w["ffn2_w_out"])
    f = hid2_t.shape[0]
    grads["ffn2_w_out"] = carry("ffn2_out_grad", _mm, hid2_t, df2, **grad_of)
    reduce_in_chip("ffn2_w_out")
    grads["ffn2_w_in"] = carry("ffn2_in_grad", _mm, da2_t.reshape(2 * f, t), u3, **grad_of)
    reduce_in_chip("ffn2_w_in")
    reduce_over_chips("ffn2_w_out")
    dh2, dh2b, d_norm_ffn2 = carry("norm_ffn2_bwd", _rmsnorm_bwd, du3, h2, s["norm_ffn2"], dh3, 1.0)
    reduce_over_chips("ffn2_w_in")
    grads["w_out_mix"] = carry("mix_out_grad", _mm, yt, dh2b, **grad_of)
    reduce_in_chip("w_out_mix")
    dy = carry("mix_out_bwd", _mm, dh2b, w["w_out_mix"], nt=True, out_dtype=F32, tm=512, tn=512)
    dya, dyb, d_lru_out_norm, d_attn_out_norm = _mixnorm_bwd(dy, ya, yb, s["lru_out_norm"], s["attn_out_norm"],
                                                             "mix_norm_bwd")
    dq, dk, dv, dqkv_t, d_tables = carry("attn_bwd", _attn_bwd, proj, tables, dyb)
    dx_lru, dg_lru, dxg_t, d_conv_w, d_conv_b, d_gw, d_gb, d_lam = carry(
        "lru_bwd", _lru_bwd, proj, conv_w, s["lru_conv_b"], gw, gb, lam, hf, hb, dya)
    reduce_over_chips("w_out_mix")
    dproj = jnp.concatenate([dx_lru, dg_lru, dq, dk, dv], axis=1)
    dproj_t = jnp.concatenate([dxg_t.reshape(2 * c, t), dqkv_t.reshape(3 * width, t)], axis=0)
    grads["w_in_mix"] = carry("mix_in_grad", _mm, dproj_t, u2, **grad_of)
    reduce_in_chip("w_in_mix")
    du2 = carry("mix_in_bwd", _mm, dproj, w["w_in_mix"], nt=False, out_dtype=F32, tm=512, tn=512)
    reduce_over_chips("w_in_mix")
    dh1, df1, d_norm_mix = carry("norm_mix_bwd", _rmsnorm_bwd, du2, h1, s["norm_mix"], dh2, 0.5)

    by_device = lambda a: a.reshape(a.shape[0], N_DEV, -1).transpose(1, 0, 2)
    small = {
        "norm_mix": d_norm_mix, "lru_conv_b": d_conv_b, "lru_gate_w": _diag_gate_blocks(d_gw, s["lru_gate_w"].shape),
        "lru_gate_b": d_gb.reshape(s["lru_gate_b"].shape), "attn_rpb": tables_vjp(d_tables)[0],
        "lru_out_norm": d_lru_out_norm, "attn_out_norm": d_attn_out_norm, "norm_ffn2": d_norm_ffn2,
        "norm_final": d_norm_final, "lru_conv_w": by_device(d_conv_w), "lru_lambda": by_device(d_lam),
    }
    early = [small[n] for n in SMALL_ORDER[1:]]
    traffic.open("gather", "small_grads", _pack(early))

    du1, hid1_t, da1_t = carry("ffn1_bwd", _ffn_bwd, df1, g1, up1, w["ffn1_w_in"], w["ffn1_w_out"])
    grads["ffn1_w_out"] = carry("ffn1_out_grad", _mm, hid1_t, df1, **grad_of)
    reduce_in_chip("ffn1_w_out")
    gate_rows = carry("ffn1_in_grad_gate", _mm, da1_t, u1, lead=0, out_rows=2 * f, **grad_of)
    reduce_over_chips("ffn1_w_out")
    grads["ffn1_w_in"] = carry("ffn1_in_grad_up", _mm, da1_t, u1, lead=1, out_rows=2 * f, row_offset=f,
                               into=gate_rows, **grad_of)
    reduce_in_chip("ffn1_w_in")
    grad_x, _, d_norm_ffn1 = _rmsnorm_bwd(du1, x, s["norm_ffn1"], dh1, 1.0, "norm_ffn1_bwd")
    traffic.open("gather", "late_grads", _pack([d_norm_ffn1]))
    traffic.alone("to_sibling_ffn1_in")
    reduce_over_chips("ffn1_w_in")
    traffic.alone("to_chips_ffn1")
    partials = {n: traffic.result("to_chips", n) for n in LARGE}
    reduced = (_unpack(_sum_devices(traffic.result("gather", "late_grads"), "sum_late_grads"), [d_norm_ffn1])
               + _unpack(_sum_devices(traffic.result("gather", "small_grads"), "sum_small_grads"), early))
    assert not traffic.transfers, list(traffic.transfers)
    return loss_part[0, 0], grad_x, partials, dict(zip(SMALL_ORDER, reduced))


def _step(x, loss_target, p, m, v):
    me = 4 * lax.axis_index("x") + 2 * lax.axis_index("y") + lax.axis_index("c")

    shards = {n: (_cast_transposed if n in COLUMN_SHARDED else _cast_rows)(p[n], "cast_" + n) for n in LARGE}
    sharded_small = (jnp.pad(p["lru_conv_w"], ((0, SUBLANES - CONV_WIDTH), (0, 0)))
                     + jnp.pad(p["lru_lambda"], ((CONV_WIDTH, SUBLANES - CONV_WIDTH - 2), (0, 0))))
    s = {n: p[n] if n in ("lru_gate_w", "lru_gate_b", "attn_rpb") else p[n].reshape(1, -1) for n in REPLICATED}

    loss_part, grad_x, partials, small = _forward_backward(x, loss_target, shards, sharded_small, s)
    loss = lax.psum(loss_part, ("x", "y", "c"))

    out = {}
    for n in LARGE:
        update = _adamw_cols if n in COLUMN_SHARDED else _adamw_rows
        out[n] = update(p[n], partials[n], m[n], v[n], "adamw_" + n)

    g_small = {n: lax.dynamic_index_in_dim(g, me, axis=0, keepdims=False) if n in SHARDED_SMALL else g
               for n, g in small.items()}
    names = SMALL_ORDER
    like = [p[n] for n in names]
    pack_of = lambda d: _pack([d[n].reshape(p[n].shape) for n in names])
    upd = _adamw_small(pack_of(p), pack_of(g_small), pack_of(m), pack_of(v), "adamw_small")
    for n, d_, m_, v_ in zip(names, *[_unpack(u, like) for u in upd]):
        out[n] = (g_small[n].reshape(p[n].shape), d_, m_, v_)
    return loss, grad_x, out


def kernel(x, norm_ffn1, ffn1_w_in, ffn1_w_out, norm_mix, w_in_mix, lru_conv_w, lru_conv_b, lru_gate_w, lru_gate_b, lru_lambda, attn_rpb, lru_out_norm, attn_out_norm, w_out_mix, norm_ffn2, ffn2_w_in, ffn2_w_out, norm_final, loss_target, m_norm_ffn1, m_ffn1_w_in, m_ffn1_w_out, m_norm_mix, m_w_in_mix, m_lru_conv_w, m_lru_conv_b, m_lru_gate_w, m_lru_gate_b, m_lru_lambda, m_attn_rpb, m_lru_out_norm, m_attn_out_norm, m_w_out_mix, m_norm_ffn2, m_ffn2_w_in, m_ffn2_w_out, m_norm_final, v_norm_ffn1, v_ffn1_w_in, v_ffn1_w_out, v_norm_mix, v_w_in_mix, v_lru_conv_w, v_lru_conv_b, v_lru_gate_w, v_lru_gate_b, v_lru_lambda, v_attn_rpb, v_lru_out_norm, v_attn_out_norm, v_w_out_mix, v_norm_ffn2, v_ffn2_w_in, v_ffn2_w_out, v_norm_final):
    given = dict(locals())
    drop_layer = lambda n, a: a if n == "norm_final" else a[0]
    p = {n: drop_layer(n, given[n]) for n in WEIGHTS}
    m = {n: drop_layer(n, given["m_" + n]) for n in WEIGHTS}
    v = {n: drop_layer(n, given["v_" + n]) for n in WEIGHTS}
    loss, grad_x, out = _step(x[0], loss_target[0], p, m, v)
    shaped = lambda n, a: a.reshape(given[n].shape)
    return (loss, grad_x[None], *[shaped(n, out[n][k]) for k in range(4) for n in WEIGHTS])
```

```python
import math

import numpy as np
import jax
import jax.numpy as jnp
from jax import lax
from jax.experimental import pallas as pl
from jax.experimental.pallas import tpu as pltpu

F32 = jnp.float32
BF16 = jnp.bfloat16
SDS = jax.ShapeDtypeStruct

N_DEV = 8
N_CHIP = 4
NORM_EPS = 1e-6
RG_C = 8.0
CONV_WIDTH = 4
HEAD_DIM = 64
GRID_W = 64
WIN_ROWS = 8
WIN_COLS = 16
BAND = WIN_ROWS * GRID_W
NEG = -1e30

ADAM_LR = 0.001
ADAM_B1 = 0.9
ADAM_B2 = 0.999
ADAM_EPS = 1e-08
ADAM_WD = 0.01
ADAM_STEP = 10

LANES = 128
SUBLANES = 8
VMEM_LIMIT = 56 * 1024 * 1024

NT = (((1,), (1,)), ((), ()))
TN = (((0,), (0,)), ((), ()))
ANY = pl.BlockSpec(memory_space=pl.ANY)
WHOLE = pl.BlockSpec(memory_space=pltpu.VMEM)
MESH = pl.DeviceIdType.MESH


def _sigmoid(x):
    return 1.0 / (1.0 + jnp.exp(-x))


def _gelu_parts(x):
    c = math.sqrt(2.0 / math.pi)
    t = jnp.tanh(c * (x + 0.044715 * (x * x * x)))
    gelu = 0.5 * x * (1.0 + t)
    dgelu = 0.5 * (1.0 + t) + 0.5 * x * (1.0 - t * t) * (c * (1.0 + 3.0 * 0.044715 * (x * x)))
    return gelu, dgelu


def _expm1(x):
    poly = x * (1.0 + x * (1.0 / 2) * (1.0 + x * (1.0 / 3) * (1.0 + x * (1.0 / 4) * (1.0 + x * (1.0 / 5) * (1.0 + x * (1.0 / 6))))))
    return jnp.where(jnp.abs(x) < 0.25, poly, jnp.exp(x) - 1.0)


def _softplus(x):
    return jnp.maximum(x, 0.0) + jnp.log1p(jnp.exp(-jnp.abs(x)))


class _Piece:
    N_REMOTE = {"gather": 7, "to_sibling": N_CHIP, "to_chips": 3}
    N_LOCAL = {"gather": 1, "to_sibling": 0, "to_chips": 1}

    def __init__(self, kind, src, dest, lo, hi):
        self.kind, self.src, self.dest, self.lo, self.hi = kind, src, dest, lo, hi


class _Job:
    def __init__(self, pieces):
        self.pieces = list(pieces)
        self.ins = [p.src for p in self.pieces]
        self.out_shapes = [SDS(p.dest.shape, p.dest.dtype) for p in self.pieces]
        self.aliased = [i for i, p in enumerate(self.pieces) if not isinstance(p.dest, SDS)]
        self.n_remote = sum(_Piece.N_REMOTE[p.kind] for p in self.pieces)
        self.n_local = max(sum(_Piece.N_LOCAL[p.kind] for p in self.pieces), 1)

    def _each(self, step, ins, outs, send_sems, recv_sems, local_sems):
        remote = local = 0
        for p, src, dst in zip(self.pieces, ins, outs):
            _EXCHANGES[p.kind](step, p, src, dst, send_sems, recv_sems, local_sems, remote, local)
            remote += _Piece.N_REMOTE[p.kind]
            local += _Piece.N_LOCAL[p.kind]

    def start(self, *refs):
        self._each("start", *refs)

    def finish(self, *refs):
        self._each("relay", *refs)
        self._each("finish", *refs)


def _call(body, *, name, args, out_shape, in_specs, out_specs, grid=(), scratch_shapes=(), aliases=None, job=None):
    single = not isinstance(out_shape, (tuple, list))
    out_shape = (out_shape,) if single else tuple(out_shape)
    out_specs = (out_specs,) if single else tuple(out_specs)
    aliases = dict(aliases or {})
    params = pltpu.CompilerParams(dimension_semantics=("arbitrary",) * len(grid) if grid else None,
                                  vmem_limit_bytes=VMEM_LIMIT)
    if job is None:
        res = pl.pallas_call(body, out_shape=out_shape, grid=grid, in_specs=list(in_specs), out_specs=out_specs,
                             scratch_shapes=list(scratch_shapes), input_output_aliases=aliases, name=name,
                             compiler_params=params)(*args)
        return res[0] if single else res

    n_in, n_out, n_scr = len(args), len(out_shape), len(scratch_shapes)
    j_in, j_out, j_alias = len(job.ins), len(job.out_shapes), len(job.aliased)

    def hosted(*refs):
        ins, refs = refs[:n_in], refs[n_in:]
        j_ins, refs = refs[:j_in], refs[j_in + j_alias:]
        outs, refs = refs[:n_out], refs[n_out:]
        j_outs, refs = refs[:j_out], refs[j_out:]
        scr, sems = refs[:n_scr], refs[n_scr:]
        if grid:
            first = last = None
            for axis, size in enumerate(grid):
                at_first, at_last = pl.program_id(axis) == 0, pl.program_id(axis) == size - 1
                first = at_first if first is None else first & at_first
                last = at_last if last is None else last & at_last
            pl.when(first)(lambda: job.start(j_ins, j_outs, *sems))
            body(*ins, *outs, *scr)
            pl.when(last)(lambda: job.finish(j_ins, j_outs, *sems))
        else:
            job.start(j_ins, j_outs, *sems)
            body(*ins, *outs, *scr)
            job.finish(j_ins, j_outs, *sems)

    res = pl.pallas_call(
        hosted, out_shape=out_shape + tuple(job.out_shapes), grid=grid,
        in_specs=list(in_specs) + [ANY] * (j_in + j_alias), out_specs=out_specs + (ANY,) * j_out,
        scratch_shapes=list(scratch_shapes) + [pltpu.SemaphoreType.DMA((job.n_remote,)),
                                               pltpu.SemaphoreType.DMA((job.n_remote,)),
                                               pltpu.SemaphoreType.DMA((job.n_local,))],
        input_output_aliases={**aliases, **{n_in + j_in + k: n_out + i for k, i in enumerate(job.aliased)}},
        name=name, compiler_params=params)(*args, *job.ins, *[job.pieces[i].dest for i in job.aliased])
    own, carried = res[:n_out], res[n_out:]
    return (own[0] if single else own), carried


def _run_job(job, name):
    return _call(lambda: None, name=name, args=[], out_shape=(), in_specs=[], out_specs=(), job=job)[1]


def _position():
    return lax.axis_index("x"), lax.axis_index("y"), lax.axis_index("c")


def _flat(px, py, pc):
    return 4 * px + 2 * py + pc


def _gather_exchange(step, p, src, dst, send_sems, recv_sems, local_sems, r0, l0):
    x, y, c = _position()
    me, sibling = (x, y, c), (x, y, 1 - c)
    chips = [(1 - x, y), (x, 1 - y), (1 - x, 1 - y)]
    rb, n_rows = p.src.shape[0], p.hi - p.lo
    mine = src.at[pl.ds(p.lo, n_rows), :]

    def rows(block):
        return dst.at[pl.ds(_flat(*block) * rb + p.lo, n_rows), :]

    def copy(k, block, to, own=False):
        return pltpu.make_async_remote_copy(
            src_ref=mine if own else rows(block), dst_ref=rows(block),
            send_sem=send_sems.at[r0 + k], recv_sem=recv_sems.at[r0 + k], device_id=to, device_id_type=MESH)

    local = pltpu.make_async_copy(mine, rows(me), local_sems.at[l0])
    if step == "start":
        local.start()
        copy(0, me, sibling, own=True).start()
        for j, chip in enumerate(chips):
            copy(1 + j, me, (*chip, c), own=True).start()
    elif step == "relay":
        for j, chip in enumerate(chips):
            copy(1 + j, (*chip, c), me).wait_recv()
            copy(4 + j, (*chip, c), sibling).start()
    else:
        copy(0, sibling, me).wait_recv()
        for j, chip in enumerate(chips):
            copy(4 + j, (*chip, 1 - c), me).wait_recv()
        copy(0, me, sibling, own=True).wait_send()
        for j, chip in enumerate(chips):
            copy(1 + j, me, (*chip, c), own=True).wait_send()
            copy(4 + j, (*chip, c), sibling).wait_send()
        local.wait()


def _sibling_exchange(step, p, src, dst, send_sems, recv_sems, local_sems, r0, l0):
    x, y, c = _position()
    rb, n_rows = p.src.shape[0] // N_DEV, p.hi - p.lo
    for q in range(N_CHIP):
        copy = pltpu.make_async_remote_copy(
            src_ref=src.at[pl.ds((2 * q + 1 - c) * rb + p.lo, n_rows), :],
            dst_ref=dst.at[pl.ds(q * rb + p.lo, n_rows), :],
            send_sem=send_sems.at[r0 + q], recv_sem=recv_sems.at[r0 + q], device_id=(x, y, 1 - c), device_id_type=MESH)
        if step == "start":
            copy.start()
        elif step == "finish":
            copy.wait()


CHIP_FLIPS = [(1, 0), (0, 1), (1, 1)]


def _chips_exchange(step, p, src, dst, send_sems, recv_sems, local_sems, r0, l0):
    x, y, c = _position()
    rb, n_rows = p.src.shape[0] // N_CHIP, p.hi - p.lo

    def slot(ref, px, py):
        return ref.at[pl.ds((2 * px + py) * rb + p.lo, n_rows), :]

    def copy(k, landing=False):
        px = 1 - x if CHIP_FLIPS[k][0] else x
        py = 1 - y if CHIP_FLIPS[k][1] else y
        return pltpu.make_async_remote_copy(
            src_ref=slot(dst, px, py) if landing else slot(src, px, py),
            dst_ref=slot(dst, px, py) if landing else slot(dst, x, y),
            send_sem=send_sems.at[r0 + k], recv_sem=recv_sems.at[r0 + k], device_id=(px, py, c), device_id_type=MESH)

    local = pltpu.make_async_copy(slot(src, x, y), slot(dst, x, y), local_sems.at[l0])
    if step == "start":
        local.start()
        for k in range(3):
            copy(k).start()
    elif step == "finish":
        for k in range(3):
            copy(k, landing=True).wait_recv()
        for k in range(3):
            copy(k).wait_send()
        local.wait()


_EXCHANGES = {"gather": _gather_exchange, "to_sibling": _sibling_exchange, "to_chips": _chips_exchange}


def _gathered(shard):
    return SDS((N_DEV * shard.shape[0], shard.shape[1]), shard.dtype)


def _split(rows, parts):
    cuts = [rows * k // parts // 16 * 16 for k in range(parts)] + [rows]
    return list(zip(cuts[:-1], cuts[1:]))


def _pair_sum(g, from_sibling, name):
    rb, n = g.shape[0] // N_DEV, g.shape[1]
    tr = rb if rb * n * 2 <= 3 * 1024 * 1024 else rb // 2
    core = lax.axis_index("c").astype(jnp.int32).reshape(1)

    def body(c_ref, g_ref, r_ref, o_ref):
        o_ref[...] = (g_ref[...].astype(F32) + r_ref[...].astype(F32)).astype(BF16)

    grid_spec = pltpu.PrefetchScalarGridSpec(
        num_scalar_prefetch=1, grid=(N_CHIP, rb // tr),
        in_specs=[pl.BlockSpec((None, None, tr, n), lambda q, i, c_ref: (q, c_ref[0], i, 0)),
                  pl.BlockSpec((None, tr, n), lambda q, i, c_ref: (q, i, 0))],
        out_specs=pl.BlockSpec((None, tr, n), lambda q, i, c_ref: (q, i, 0)))
    out = pl.pallas_call(
        body, grid_spec=grid_spec, out_shape=SDS((N_CHIP, rb, n), BF16), name=name,
        compiler_params=pltpu.CompilerParams(dimension_semantics=("arbitrary",) * 2, vmem_limit_bytes=VMEM_LIMIT))(
            core, g.reshape(N_CHIP, 2, rb, n), from_sibling.reshape(N_CHIP, rb, n))
    return out.reshape(N_CHIP * rb, n)


def _sum_devices(gathered, name):
    r = gathered.shape[0] // N_DEV

    def body(g_ref, o_ref):
        acc = g_ref[0]
        for s in range(1, N_DEV):
            acc = acc + g_ref[s]
        o_ref[...] = acc

    return _call(body, name=name, args=[gathered.reshape(N_DEV, r, LANES)], out_shape=SDS((r, LANES), F32),
                 in_specs=[WHOLE], out_specs=WHOLE)


def _cast_rows(w, name):
    def body(w_ref, o_ref):
        o_ref[...] = w_ref[...].astype(BF16)

    return _call(body, name=name, args=[w], out_shape=SDS(w.shape, BF16), in_specs=[WHOLE], out_specs=WHOLE)


def _cast_transposed(w, name):
    d, n = w.shape
    td = 512

    def body(w_ref, o_ref):
        o_ref[...] = w_ref[...].T.astype(BF16)

    return _call(body, name=name, args=[w], out_shape=SDS((n, d), BF16), grid=(d // td,),
                 in_specs=[pl.BlockSpec((td, n), lambda i: (i, 0))], out_specs=pl.BlockSpec((n, td), lambda i: (0, i)))


ROW_TILE = 256


def _rmsnorm_fwd(h, gain, name):
    t, d = h.shape

    def body(h_ref, g_ref, u_ref):
        x = h_ref[...]
        u_ref[...] = (x * lax.rsqrt(jnp.mean(x * x, axis=-1, keepdims=True) + NORM_EPS) * g_ref[...]).astype(BF16)

    row = pl.BlockSpec((ROW_TILE, d), lambda i: (i, 0))
    return _call(body, name=name, args=[h, gain], out_shape=SDS((t, d), BF16), grid=(t // ROW_TILE,),
                 in_specs=[row, pl.BlockSpec((1, d), lambda i: (0, 0))], out_specs=row)


def _rms_bwd_math(x, gain, dy):
    rstd = lax.rsqrt(jnp.mean(x * x, axis=-1, keepdims=True) + NORM_EPS)
    xhat = x * rstd
    dxh = dy * gain
    dx = rstd * (dxh - xhat * jnp.mean(dxh * xhat, axis=-1, keepdims=True))
    return dx, jnp.sum(dy * xhat, axis=0, keepdims=True)


def _rmsnorm_bwd(du, h, gain, resid, bf_scale, name, job=None):
    t, d = h.shape

    def body(du_ref, h_ref, g_ref, r_ref, dh_ref, dhb_ref, dg_ref):
        @pl.when(pl.program_id(0) == 0)
        def _():
            dg_ref[...] = jnp.zeros_like(dg_ref)

        dx, dg = _rms_bwd_math(h_ref[...], g_ref[...], du_ref[...])
        dh = r_ref[...] + dx
        dh_ref[...] = dh
        dhb_ref[...] = (bf_scale * dh).astype(BF16)
        dg_ref[...] += dg

    row = pl.BlockSpec((ROW_TILE, d), lambda i: (i, 0))
    vec = pl.BlockSpec((1, d), lambda i: (0, 0))
    return _call(body, name=name, args=[du, h, gain, resid],
                 out_shape=(SDS((t, d), F32), SDS((t, d), BF16), SDS((1, d), F32)), grid=(t // ROW_TILE,),
                 in_specs=[row, row, vec, row], out_specs=(row, row, vec), job=job)


def _final_loss(h, gain, target, name):
    t, d = h.shape

    def body(h_ref, g_ref, t_ref, dh_ref, dhb_ref, loss_ref, dg_ref):
        @pl.when(pl.program_id(0) == 0)
        def _():
            dg_ref[...] = jnp.zeros_like(dg_ref)
            loss_ref[...] = jnp.zeros_like(loss_ref)

        x = h_ref[...]
        gain = g_ref[...]
        out = x * lax.rsqrt(jnp.mean(x * x, axis=-1, keepdims=True) + NORM_EPS) * gain
        err = out - t_ref[...]
        loss_ref[...] += 0.5 * jnp.sum(jnp.mean(err * err, axis=-1, keepdims=True), axis=0, keepdims=True)
        dx, dg = _rms_bwd_math(x, gain, err * (1.0 / d))
        dh_ref[...] = dx
        dhb_ref[...] = (0.5 * dx).astype(BF16)
        dg_ref[...] += dg

    row = pl.BlockSpec((ROW_TILE, d), lambda i: (i, 0))
    vec = pl.BlockSpec((1, d), lambda i: (0, 0))
    one = pl.BlockSpec((SUBLANES, LANES), lambda i: (0, 0))
    return _call(body, name=name, args=[h, gain, target],
                 out_shape=(SDS((t, d), F32), SDS((t, d), BF16), SDS((SUBLANES, LANES), F32), SDS((1, d), F32)),
                 grid=(t // ROW_TILE,), in_specs=[row, vec, row], out_specs=(row, row, one, vec))


def _mixnorm_fwd(ya, yb, ga, gb, name):
    t, c = ya.shape

    def body(ya_ref, yb_ref, ga_ref, gb_ref, y_ref, yt_ref):
        for k, (src, g_ref) in enumerate(((ya_ref, ga_ref), (yb_ref, gb_ref))):
            x = src[...]
            u = x * lax.rsqrt(jnp.mean(x * x, axis=-1, keepdims=True) + NORM_EPS) * g_ref[...]
            y_ref[:, k * c:(k + 1) * c] = u.astype(BF16)
            yt_ref[k * c:(k + 1) * c, :] = u.T.astype(BF16)

    row = pl.BlockSpec((ROW_TILE, c), lambda i: (i, 0))
    vec = pl.BlockSpec((1, c), lambda i: (0, 0))
    return _call(body, name=name, args=[ya, yb, ga, gb],
                 out_shape=(SDS((t, 2 * c), BF16), SDS((2 * c, t), BF16)), grid=(t // ROW_TILE,),
                 in_specs=[row, row, vec, vec],
                 out_specs=(pl.BlockSpec((ROW_TILE, 2 * c), lambda i: (i, 0)),
                            pl.BlockSpec((2 * c, ROW_TILE), lambda i: (0, i))))


def _mixnorm_bwd(dy, ya, yb, ga, gb, name):
    t, c = ya.shape

    def body(dy_ref, ya_ref, yb_ref, ga_ref, gb_ref, dya_ref, dyb_ref, dga_ref, dgb_ref):
        @pl.when(pl.program_id(0) == 0)
        def _():
            dga_ref[...] = jnp.zeros_like(dga_ref)
            dgb_ref[...] = jnp.zeros_like(dgb_ref)

        dxa, dga = _rms_bwd_math(ya_ref[...], ga_ref[...], dy_ref[:, :c])
        dxb, dgb = _rms_bwd_math(yb_ref[...], gb_ref[...], dy_ref[:, c:])
        dya_ref[...] = dxa
        dyb_ref[...] = dxb
        dga_ref[...] += dga
        dgb_ref[...] += dgb

    row = pl.BlockSpec((ROW_TILE, c), lambda i: (i, 0))
    vec = pl.BlockSpec((1, c), lambda i: (0, 0))
    return _call(body, name=name, args=[dy, ya, yb, ga, gb],
                 out_shape=(SDS((t, c), F32), SDS((t, c), F32), SDS((1, c), F32), SDS((1, c), F32)),
                 grid=(t // ROW_TILE,),
                 in_specs=[pl.BlockSpec((ROW_TILE, 2 * c), lambda i: (i, 0)), row, row, vec, vec],
                 out_specs=(row, row, vec, vec))


def _tile(n, want):
    return max(t for t in range(LANES, min(n, want) + 1, LANES) if n % t == 0)


def _mm(a, b, *, nt, out_dtype, tm, tn, name, residual=None, lead=None, out_rows=None, row_offset=0, into=None,
        job=None):
    m, k = a.shape[-2:]
    n = b.shape[0] if nt else b.shape[1]
    tm, tn = _tile(m, tm), _tile(n, tn)
    out_rows = m if out_rows is None else out_rows

    def body(a_ref, b_ref, *rest):
        o_ref = rest[-1]
        av, bv = a_ref[...].astype(BF16), b_ref[...].astype(BF16)
        if nt:
            out = lax.dot_general(av, bv, NT, preferred_element_type=F32)
        else:
            out = jnp.dot(av, bv, preferred_element_type=F32)
        if residual is not None:
            out = rest[0][...] + out
        o_ref[...] = out.astype(out_dtype)

    a_spec = (pl.BlockSpec((tm, k), lambda i, j: (i, 0)) if lead is None
              else pl.BlockSpec((None, tm, k), lambda i, j: (lead, i, 0)))
    in_specs = [a_spec, pl.BlockSpec((tn, k), lambda i, j: (j, 0)) if nt else pl.BlockSpec((k, tn), lambda i, j: (0, j))]
    args, aliases = [a, b], {}
    if residual is not None:
        in_specs.append(pl.BlockSpec((tm, tn), lambda i, j: (i, j)))
        args.append(residual)
    if into is not None:
        in_specs.append(ANY)
        aliases[len(args)] = 0
        args.append(into)
    return _call(body, name=name, args=args, out_shape=SDS((out_rows, n), out_dtype), grid=(m // tm, n // tn),
                 in_specs=in_specs, out_specs=pl.BlockSpec((tm, tn), lambda i, j: (row_offset // tm + i, j)),
                 aliases=aliases, job=job)


FFN_TM = 512
FFN_HB = 512


def _ffn_fwd(h, u, w_in_t, w_out, name, job=None):
    t, d = h.shape
    f = w_out.shape[0]
    nk = f // FFN_HB

    def body(u_ref, w_ref, wo_ref, h_ref, hn_ref, g_ref, up_ref, acc):
        k = pl.program_id(1)

        @pl.when(k == 0)
        def _():
            acc[...] = jnp.zeros_like(acc)

        uu = u_ref[...]
        g = lax.dot_general(uu, w_ref[0], NT, preferred_element_type=F32)
        up = lax.dot_general(uu, w_ref[1], NT, preferred_element_type=F32)
        g_ref[...] = g
        up_ref[...] = up
        hid = (g * _sigmoid(g)) * up
        acc[...] += jnp.dot(hid.astype(BF16), wo_ref[...], preferred_element_type=F32)

        @pl.when(k == nk - 1)
        def _():
            hn_ref[...] = h_ref[...] + 0.5 * acc[...]

    tok = pl.BlockSpec((FFN_TM, d), lambda i, k: (i, 0))
    pre = pl.BlockSpec((FFN_TM, FFN_HB), lambda i, k: (i, k))
    return _call(body, name=name, args=[u, w_in_t.reshape(2, f, d), w_out, h],
                 out_shape=(SDS((t, d), F32), SDS((t, f), F32), SDS((t, f), F32)), grid=(t // FFN_TM, nk),
                 in_specs=[tok, pl.BlockSpec((2, FFN_HB, d), lambda i, k: (0, k, 0)),
                           pl.BlockSpec((FFN_HB, d), lambda i, k: (k, 0)), tok],
                 out_specs=(tok, pre, pre), scratch_shapes=[pltpu.VMEM((FFN_TM, d), F32)], job=job)


def _ffn_bwd(dfb, gpre, upre, w_in_t, w_out, name, job=None):
    t, d = dfb.shape
    f = w_out.shape[0]
    nk = f // FFN_HB

    def body(df_ref, g_ref, up_ref, w_ref, wo_ref, du_ref, hid_t_ref, da_t_ref, acc):
        k = pl.program_id(1)

        @pl.when(k == 0)
        def _():
            acc[...] = jnp.zeros_like(acc)

        dhid = lax.dot_general(df_ref[...], wo_ref[...], NT, preferred_element_type=F32)
        g, up = g_ref[...], up_ref[...]
        sig = _sigmoid(g)
        silu = g * sig
        dup = dhid * silu
        dg = dhid * up * (sig * (1.0 + g * (1.0 - sig)))
        hid_t_ref[...] = (silu * up).T.astype(BF16)
        da_t_ref[0] = dg.T.astype(BF16)
        da_t_ref[1] = dup.T.astype(BF16)
        acc[...] += (jnp.dot(dg.astype(BF16), w_ref[0], preferred_element_type=F32)
                     + jnp.dot(dup.astype(BF16), w_ref[1], preferred_element_type=F32))

        @pl.when(k == nk - 1)
        def _():
            du_ref[...] = acc[...]

    tok = pl.BlockSpec((FFN_TM, d), lambda i, k: (i, 0))
    pre = pl.BlockSpec((FFN_TM, FFN_HB), lambda i, k: (i, k))
    return _call(body, name=name, args=[dfb, gpre, upre, w_in_t.reshape(2, f, d), w_out],
                 out_shape=(SDS((t, d), F32), SDS((f, t), BF16), SDS((2, f, t), BF16)), grid=(t // FFN_TM, nk),
                 in_specs=[tok, pre, pre, pl.BlockSpec((2, FFN_HB, d), lambda i, k: (0, k, 0)),
                           pl.BlockSpec((FFN_HB, d), lambda i, k: (k, 0))],
                 out_specs=(tok, pl.BlockSpec((FFN_HB, FFN_TM), lambda i, k: (k, i)),
                            pl.BlockSpec((2, FFN_HB, FFN_TM), lambda i, k: (0, k, i))),
                 scratch_shapes=[pltpu.VMEM((FFN_TM, d), F32)], job=job)


CH = LANES
PAD = SUBLANES


def _lru_gates(xc, gw_ref, gb_ref, lam_ref, z):
    xcb = xc.astype(BF16)
    r = _sigmoid(jnp.dot(xcb, gw_ref[2 * z], preferred_element_type=F32) + gb_ref[pl.ds(2 * z, 1), :])
    i = _sigmoid(jnp.dot(xcb, gw_ref[2 * z + 1], preferred_element_type=F32) + gb_ref[pl.ds(2 * z + 1, 1), :])
    sp = _softplus(-lam_ref[pl.ds(z, 1), :])
    log_a = (-RG_C * r) * sp
    a = jnp.exp(log_a)
    mult = jnp.sqrt(-_expm1(2.0 * log_a))
    return r, i, sp, a, mult


def _conv(xpad, cw_ref, cb_ref, t):
    xc = cb_ref[...] + cw_ref[pl.ds(0, 1), :] * xpad[pl.ds(PAD - 2, t), :]
    for j in range(1, CONV_WIDTH):
        xc = xc + cw_ref[pl.ds(j, 1), :] * xpad[pl.ds(PAD - 2 + j, t), :]
    return xc


def _fill_padded(pad_ref, value, t):
    pad_ref[pl.ds(0, PAD), :] = jnp.zeros((PAD, CH), F32)
    pad_ref[pl.ds(PAD + t, PAD), :] = jnp.zeros((PAD, CH), F32)
    pad_ref[pl.ds(PAD, t), :] = value


def _scan_pair(t, a_up, b_up, out_up, a_down, b_down, out_down):
    def step(tt, carry):
        hu, hd = carry
        lo = pl.multiple_of(tt * SUBLANES, SUBLANES)
        hi = pl.multiple_of(t - SUBLANES - tt * SUBLANES, SUBLANES)
        for j in range(SUBLANES):
            su, sd = pl.ds(lo + j, 1), pl.ds(hi + SUBLANES - 1 - j, 1)
            hu = a_up(su) * hu + b_up(su)
            out_up[su, :] = hu
            hd = a_down(sd) * hd + b_down(sd)
            out_down[sd, :] = hd
        return hu, hd

    zero = jnp.zeros((1, CH), F32)
    lax.fori_loop(0, t // SUBLANES, step, (zero, zero))


def _lru_fwd(proj, cw, cb, gw, gb, lam, name, job=None):
    t = proj.shape[0]
    c = cw.shape[1]
    ncb = c // CH

    def body(x_ref, g_ref, cw_ref, cb_ref, gw_ref, gb_ref, lam_ref, ya_ref, hf_ref, hb_ref, xpad, a0, b0, a1, b1):
        _fill_padded(xpad, x_ref[...], t)
        xc = _conv(xpad, cw_ref, cb_ref, t)
        for z, (a_s, b_s) in enumerate(((a0, b0), (a1, b1))):
            _, i, _, a, mult = _lru_gates(xc, gw_ref, gb_ref, lam_ref, z)
            a_s[...] = a
            b_s[...] = mult * (i * xc)
        _scan_pair(t, lambda s: a0[s, :], lambda s: b0[s, :], hf_ref, lambda s: a1[s, :], lambda s: b1[s, :], hb_ref)
        gelu, _ = _gelu_parts(g_ref[...])
        ya_ref[...] = gelu * (hf_ref[...] + hb_ref[...])

    col = lambda off: pl.BlockSpec((t, CH), lambda i: (0, off + i))
    small = lambda rows: pl.BlockSpec((rows, CH), lambda i: (0, i))
    return _call(body, name=name, args=[proj, proj, cw, cb, gw, gb, lam], out_shape=(SDS((t, c), F32),) * 3,
                 grid=(ncb,),
                 in_specs=[col(0), col(ncb), small(CONV_WIDTH), small(1),
                           pl.BlockSpec((4, None, CH, CH), lambda i: (0, i, 0, 0)), small(4), small(2)],
                 out_specs=(col(0),) * 3,
                 scratch_shapes=[pltpu.VMEM((t + 2 * PAD, CH), F32)] + [pltpu.VMEM((t, CH), F32)] * 4, job=job)


def _lru_bwd(proj, cw, cb, gw, gb, lam, hf, hb, dya, name, job=None):
    t = proj.shape[0]
    c = cw.shape[1]
    ncb = c // CH

    def body(x_ref, g_ref, cw_ref, cb_ref, gw_ref, gb_ref, lam_ref, hf_ref, hb_ref, dya_ref,
             dx_ref, dg_ref, dt_ref, dcw_ref, dcb_ref, dgw_ref, dgb_ref, dlam_ref,
             xpad, hpad, dxc, a0, a1, dhs, dh0, dh1):
        _fill_padded(xpad, x_ref[...], t)
        xc = _conv(xpad, cw_ref, cb_ref, t)
        xcb = xc.astype(BF16)
        gates = [_lru_gates(xc, gw_ref, gb_ref, lam_ref, z) for z in range(2)]
        a0[...] = gates[0][3]
        a1[...] = gates[1][3]

        gelu, dgelu = _gelu_parts(g_ref[...])
        dya = dya_ref[...]
        dgate = dya * (hf_ref[...] + hb_ref[...]) * dgelu
        dg_ref[...] = dgate.astype(BF16)
        dt_ref[1] = dgate.T.astype(BF16)
        dhs[...] = dya * gelu

        def step(tt, carry):
            c0, p0, c1, p1 = carry
            lo = pl.multiple_of(tt * SUBLANES, SUBLANES)
            hi = pl.multiple_of(t - SUBLANES - tt * SUBLANES, SUBLANES)
            for j in range(SUBLANES):
                su, sd = pl.ds(lo + j, 1), pl.ds(hi + SUBLANES - 1 - j, 1)
                c0 = dhs[sd, :] + p0 * c0
                dh0[sd, :] = c0
                p0 = a0[sd, :]
                c1 = dhs[su, :] + p1 * c1
                dh1[su, :] = c1
                p1 = a1[su, :]
            return c0, p0, c1, p1

        zero = jnp.zeros((1, CH), F32)
        lax.fori_loop(0, t // SUBLANES, step, (zero, zero, zero, zero))

        acc_dxc = jnp.zeros((t, CH), F32)
        for z, (h_ref, dh_ref, shift) in enumerate(((hf_ref, dh0, -1), (hb_ref, dh1, 1))):
            r, i, sp, a, mult = gates[z]
            _fill_padded(hpad, h_ref[...], t)
            h_nb = hpad[pl.ds(PAD + shift, t), :]
            db = dh_ref[...]
            da = db * h_nb
            d_i = db * mult * xc
            acc_dxc = acc_dxc + db * mult * i
            d_mult = db * i * xc
            d_la = da * a - d_mult * (a * a) / mult
            d_r = d_la * (-RG_C * sp)
            dlam_ref[pl.ds(z, 1), :] = (jnp.sum(d_la * (-RG_C * r), axis=0, keepdims=True)
                                        * (-_sigmoid(-lam_ref[pl.ds(z, 1), :])))
            for gate, d_pre in ((0, d_r * r * (1.0 - r)), (1, d_i * i * (1.0 - i))):
                zg = 2 * z + gate
                dgb_ref[pl.ds(zg, 1), :] = jnp.sum(d_pre, axis=0, keepdims=True)
                d_pre_b = d_pre.astype(BF16)
                dgw_ref[zg] = lax.dot_general(xcb, d_pre_b, TN, preferred_element_type=F32)
                acc_dxc = acc_dxc + lax.dot_general(d_pre_b, gw_ref[zg], NT, preferred_element_type=F32)

        dcb_ref[...] = jnp.sum(acc_dxc, axis=0, keepdims=True)
        for j in range(CONV_WIDTH):
            dcw_ref[pl.ds(j, 1), :] = jnp.sum(acc_dxc * xpad[pl.ds(PAD - 2 + j, t), :], axis=0, keepdims=True)
        _fill_padded(dxc, acc_dxc, t)
        dx = cw_ref[pl.ds(0, 1), :] * dxc[pl.ds(PAD + 2, t), :]
        for j in range(1, CONV_WIDTH):
            dx = dx + cw_ref[pl.ds(j, 1), :] * dxc[pl.ds(PAD + 2 - j, t), :]
        dx_ref[...] = dx.astype(BF16)
        dt_ref[0] = dx.T.astype(BF16)

    col = lambda off: pl.BlockSpec((t, CH), lambda i: (0, off + i))
    small = lambda rows: pl.BlockSpec((rows, CH), lambda i: (0, i))
    dense = pl.BlockSpec((4, None, CH, CH), lambda i: (0, i, 0, 0))
    padded = pltpu.VMEM((t + 2 * PAD, CH), F32)
    return _call(
        body, name=name, args=[proj, proj, cw, cb, gw, gb, lam, hf, hb, dya],
        out_shape=(SDS((t, c), BF16), SDS((t, c), BF16), SDS((2, c, t), BF16), SDS((CONV_WIDTH, c), F32),
                   SDS((1, c), F32), SDS((4, ncb, CH, CH), F32), SDS((4, c), F32), SDS((2, c), F32)),
        grid=(ncb,),
        in_specs=[col(0), col(ncb), small(CONV_WIDTH), small(1), dense, small(4), small(2), col(0), col(0), col(0)],
        out_specs=(col(0), col(0), pl.BlockSpec((2, CH, t), lambda i: (0, i, 0)), small(CONV_WIDTH), small(1),
                   dense, small(4), small(2)),
        scratch_shapes=[padded, padded, padded] + [pltpu.VMEM((t, CH), F32)] * 5, job=job)


def _band_start(r, rows):
    return jnp.clip(r - WIN_ROWS // 2, 0, rows - WIN_ROWS)


N_PAIRS = 2 * WIN_ROWS - 2
PAIR_W = 2 * GRID_W


def _bias_tables(rpb):
    cols = np.arange(GRID_W)
    start = np.clip(cols - WIN_COLS // 2, 0, GRID_W - WIN_COLS)
    valid = (cols[None, :] >= start[:, None]) & (cols[None, :] < start[:, None] + WIN_COLS)
    col_off = np.clip(cols[None, :] - cols[:, None] + WIN_COLS - 1, 0, 2 * WIN_COLS - 2)
    pick_col = jnp.asarray(np.eye(2 * WIN_COLS - 1, dtype=np.float32)[col_off] * valid[..., None])
    by_row = jnp.einsum("hrc,qkc->hrqk", rpb, pick_col, precision=lax.Precision.HIGHEST)
    by_row = jnp.where(jnp.asarray(valid)[None, None], by_row, NEG)
    return jnp.concatenate([by_row[:, :-1], by_row[:, 1:]], axis=-1)


def _pair_index(case, pair):
    return 2 * pair + (WIN_ROWS - 1) - case


def _attn_scores(q_ref, k_ref, bm_ref, hh, r, rows):
    rs = _band_start(r, rows)
    lanes = pl.ds(hh * HEAD_DIM, HEAD_DIM)
    qrows = pl.ds(pl.multiple_of(r * GRID_W, GRID_W), GRID_W)
    band = pl.ds(pl.multiple_of(rs * GRID_W, GRID_W), BAND)
    bias = jnp.concatenate([bm_ref[hh, _pair_index(r - rs, pair)] for pair in range(WIN_ROWS // 2)], axis=1)
    q = q_ref[qrows, lanes].astype(BF16)
    kb = k_ref[band, lanes].astype(BF16)
    s = lax.dot_general(q, kb, NT, preferred_element_type=F32) * (HEAD_DIM ** -0.5) + bias
    p = jnp.exp(s - jnp.max(s, axis=-1, keepdims=True))
    p = p / jnp.sum(p, axis=-1, keepdims=True)
    return q, kb, p, qrows, band, lanes, r - rs


def _attn_fwd(proj, tables, width, name, job=None):
    t = proj.shape[0]
    rows = t // GRID_W
    npair = width // LANES
    first = (proj.shape[1] - 3 * width) // LANES

    def body(q_ref, k_ref, v_ref, bm_ref, o_ref, qs, ks, vs):
        qs[...] = q_ref[...].astype(BF16)
        ks[...] = k_ref[...].astype(BF16)
        vs[...] = v_ref[...].astype(BF16)

        def row(r, carry):
            for hh in range(2):
                _, _, p, qrows, band, lanes, _ = _attn_scores(qs, ks, bm_ref, hh, r, rows)
                o_ref[qrows, lanes] = jnp.dot(p.astype(BF16), vs[band, lanes], preferred_element_type=F32)
            return carry

        lax.fori_loop(0, rows, row, 0, unroll=2)

    col = lambda off: pl.BlockSpec((t, LANES), lambda i: (0, off + i))
    return _call(body, name=name, args=[proj, proj, proj, tables], out_shape=SDS((t, width), F32), grid=(npair,),
                 in_specs=[col(first), col(first + npair), col(first + 2 * npair),
                           pl.BlockSpec((2, N_PAIRS, GRID_W, PAIR_W), lambda i: (i, 0, 0, 0))],
                 out_specs=col(0), scratch_shapes=[pltpu.VMEM((t, LANES), BF16)] * 3, job=job)


def _attn_bwd(proj, tables, dyb, name, job=None):
    t, width = dyb.shape
    rows = t // GRID_W
    npair = width // LANES
    first = (proj.shape[1] - 3 * width) // LANES

    def body(q_ref, k_ref, v_ref, bm_ref, do_ref, dq_ref, dk_ref, dv_ref, dt_ref, dbm_ref, dq_s, dk_s, dv_s,
             qs, ks, vs, dos):
        qs[...] = q_ref[...].astype(BF16)
        ks[...] = k_ref[...].astype(BF16)
        vs[...] = v_ref[...].astype(BF16)
        dos[...] = do_ref[...].astype(BF16)
        dk_s[...] = jnp.zeros_like(dk_s)
        dv_s[...] = jnp.zeros_like(dv_s)
        dbm_ref[...] = jnp.zeros_like(dbm_ref)

        def row(r, carry):
            for hh in range(2):
                q, kb, p, qrows, band, lanes, case = _attn_scores(qs, ks, bm_ref, hh, r, rows)
                do = dos[qrows, lanes]
                dp = lax.dot_general(do, vs[band, lanes], NT, preferred_element_type=F32)
                ds = p * (dp - jnp.sum(dp * p, axis=-1, keepdims=True))
                for pair in range(WIN_ROWS // 2):
                    dbm_ref[hh, _pair_index(case, pair)] += ds[:, pair * PAIR_W:(pair + 1) * PAIR_W]
                dsb = (ds * (HEAD_DIM ** -0.5)).astype(BF16)
                dq_s[qrows, lanes] = jnp.dot(dsb, kb, preferred_element_type=F32)
                dk_s[band, lanes] += lax.dot_general(dsb, q, TN, preferred_element_type=F32)
                dv_s[band, lanes] += lax.dot_general(p.astype(BF16), do, TN, preferred_element_type=F32)
            return carry

        lax.fori_loop(0, rows, row, 0, unroll=2)
        for n, (src, dst) in enumerate(((dq_s, dq_ref), (dk_s, dk_ref), (dv_s, dv_ref))):
            val = src[...]
            dst[...] = val.astype(BF16)
            dt_ref[n] = val.T.astype(BF16)

    col = lambda off: pl.BlockSpec((t, LANES), lambda i: (0, off + i))
    table = pl.BlockSpec((2, N_PAIRS, GRID_W, PAIR_W), lambda i: (i, 0, 0, 0))
    return _call(body, name=name, args=[proj, proj, proj, tables, dyb],
                 out_shape=(SDS((t, width), BF16),) * 3 + (SDS((3, width, t), BF16), SDS(tables.shape, F32)),
                 grid=(npair,),
                 in_specs=[col(first), col(first + npair), col(first + 2 * npair), table, col(0)],
                 out_specs=(col(0), col(0), col(0), pl.BlockSpec((3, LANES, t), lambda i: (0, i, 0)), table),
                 scratch_shapes=[pltpu.VMEM((t, LANES), F32)] * 3 + [pltpu.VMEM((t, LANES), BF16)] * 4, job=job)


def _adamw_math(w, g, m, v):
    m = ADAM_B1 * m + (1.0 - ADAM_B1) * g
    v = ADAM_B2 * v + (1.0 - ADAM_B2) * (g * g)
    m_hat = m / (1.0 - ADAM_B1 ** ADAM_STEP)
    v_hat = v / (1.0 - ADAM_B2 ** ADAM_STEP)
    delta = -ADAM_LR * (m_hat / (jnp.sqrt(v_hat) + ADAM_EPS) + ADAM_WD * w)
    return delta, m, v


def _sum_partials(p_ref):
    g = p_ref[0].astype(F32)
    for s in range(1, N_CHIP):
        g = g + p_ref[s].astype(F32)
    return g


def _adamw_rows(w, partials, m, v, name):
    rb, n = w.shape
    tr = 64

    def body(w_ref, p_ref, m_ref, v_ref, g_ref, d_ref, nm_ref, nv_ref):
        g = _sum_partials(p_ref)
        g_ref[...] = g
        d_ref[...], nm_ref[...], nv_ref[...] = _adamw_math(w_ref[...], g, m_ref[...], v_ref[...])

    blk = pl.BlockSpec((tr, n), lambda i: (i, 0))
    return _call(body, name=name, args=[w, partials.reshape(N_CHIP, rb, n), m, v], out_shape=(SDS((rb, n), F32),) * 4,
                 grid=(rb // tr,), in_specs=[blk, pl.BlockSpec((N_CHIP, tr, n), lambda i: (0, i, 0)), blk, blk],
                 out_specs=(blk,) * 4)


def _adamw_cols(w, partials, m, v, name):
    d, nb = w.shape
    td = 256

    def body(w_ref, p_ref, m_ref, v_ref, g_ref, d_ref, nm_ref, nv_ref):
        g = _sum_partials(p_ref).T
        g_ref[...] = g
        d_ref[...], nm_ref[...], nv_ref[...] = _adamw_math(w_ref[...], g, m_ref[...], v_ref[...])

    blk = pl.BlockSpec((td, nb), lambda i: (i, 0))
    return _call(body, name=name, args=[w, partials.reshape(N_CHIP, nb, d), m, v], out_shape=(SDS((d, nb), F32),) * 4,
                 grid=(d // td,), in_specs=[blk, pl.BlockSpec((N_CHIP, nb, td), lambda i: (0, 0, i)), blk, blk],
                 out_specs=(blk,) * 4)


def _adamw_small(w, g, m, v, name):
    def body(w_ref, g_ref, m_ref, v_ref, d_ref, nm_ref, nv_ref):
        d_ref[...], nm_ref[...], nv_ref[...] = _adamw_math(w_ref[...], g_ref[...], m_ref[...], v_ref[...])

    return _call(body, name=name, args=[w, g, m, v], out_shape=(SDS(w.shape, F32),) * 3, in_specs=[WHOLE] * 4,
                 out_specs=(WHOLE,) * 3)


TILE = SUBLANES * LANES


def _pack(arrays):
    parts = []
    for a in arrays:
        flat = a.reshape(-1).astype(F32)
        flat = jnp.pad(flat, (0, -flat.size % TILE))
        parts.append(flat.reshape(-1, LANES))
    return jnp.concatenate(parts, axis=0)


def _unpack(pack, like):
    out, row = [], 0
    for a in like:
        n = int(np.prod(a.shape))
        nrows = -(-n // TILE) * SUBLANES
        out.append(pack[row:row + nrows].reshape(-1)[:n].reshape(a.shape))
        row += nrows
    return out


def _dense_gate_blocks(gate_w):
    w = gate_w.reshape(4, -1, 2, HEAD_DIM, HEAD_DIM)
    zero = jnp.zeros_like(w[:, :, 0])
    top = jnp.concatenate([w[:, :, 0], zero], axis=-1)
    bottom = jnp.concatenate([zero, w[:, :, 1]], axis=-1)
    return jnp.concatenate([top, bottom], axis=-2)


def _diag_gate_blocks(dense, shape):
    even = dense[:, :, :HEAD_DIM, :HEAD_DIM]
    odd = dense[:, :, HEAD_DIM:, HEAD_DIM:]
    return jnp.stack([even, odd], axis=2).reshape(shape)


LARGE = ("ffn1_w_in", "ffn1_w_out", "w_in_mix", "w_out_mix", "ffn2_w_in", "ffn2_w_out")
COLUMN_SHARDED = ("ffn1_w_in", "w_in_mix", "ffn2_w_in")
SHARDED_SMALL = ("lru_conv_w", "lru_lambda")
REPLICATED = ("norm_ffn1", "norm_mix", "lru_conv_b", "lru_gate_w", "lru_gate_b", "attn_rpb", "lru_out_norm",
              "attn_out_norm", "norm_ffn2", "norm_final")
SMALL_ORDER = REPLICATED + SHARDED_SMALL
WEIGHTS = ("norm_ffn1", "ffn1_w_in", "ffn1_w_out", "norm_mix", "w_in_mix", "lru_conv_w", "lru_conv_b", "lru_gate_w",
           "lru_gate_b", "lru_lambda", "attn_rpb", "lru_out_norm", "attn_out_norm", "w_out_mix", "norm_ffn2",
           "ffn2_w_in", "ffn2_w_out", "norm_final")


PARTS = {("gather", "ffn2_w_out"): 2, ("gather", "ffn2_w_in"): 4, ("to_chips", "ffn2_w_in"): 4,
         ("to_chips", "ffn1_w_out"): 2}
CARRIES = {
    "gather_ffn1": [(("gather", "ffn1_w_in"), 1), (("gather", "ffn1_w_out"), 1), (("gather", "small"), 1)],
    "ffn1_fwd": [(("gather", "w_in_mix"), 1), (("gather", "w_out_mix"), 1)],
    "mix_in_proj": [(("gather", "ffn2_w_out"), 1)],
    "lru_fwd": [(("gather", "ffn2_w_out"), 1)],
    "attn_fwd": [(("gather", "ffn2_w_in"), 3)],
    "mix_out_proj": [(("gather", "ffn2_w_in"), 1)],
    "ffn2_in_grad": [(("to_sibling", "ffn2_w_out"), 1)],
    "norm_ffn2_bwd": [(("to_sibling", "ffn2_w_in"), 1)],
    "attn_bwd": [(("to_chips", "ffn2_w_out"), 1), (("to_chips", "ffn2_w_in"), 3)],
    "lru_bwd": [(("to_chips", "ffn2_w_in"), 1), (("to_sibling", "w_out_mix"), 1)],
    "mix_in_grad": [(("to_chips", "w_out_mix"), 1)],
    "mix_in_bwd": [(("to_sibling", "w_in_mix"), 1)],
    "ffn1_bwd": [(("to_chips", "w_in_mix"), 1), (("gather", "small_grads"), 1)],
    "ffn1_in_grad_gate": [(("to_sibling", "ffn1_w_out"), 1)],
    "ffn1_in_grad_up": [(("to_chips", "ffn1_w_out"), 1)],
    "to_sibling_ffn1_in": [(("to_sibling", "ffn1_w_in"), 1), (("to_chips", "ffn1_w_out"), 1)],
    "to_chips_ffn1": [(("to_chips", "ffn1_w_in"), 1), (("gather", "late_grads"), 1)],
}


class _Transfer:
    def __init__(self, kind, src, dest, block_rows, parts):
        self.kind, self.src, self.dest = kind, src, dest
        self.ranges, self.taken = _split(block_rows, parts), 0

    def take(self, count):
        lo, hi = self.ranges[self.taken][0], self.ranges[self.taken + count - 1][1]
        self.taken += count
        return _Piece(self.kind, self.src, self.dest, lo, hi)


class _Traffic:
    def __init__(self):
        self.transfers = {}

    def open(self, kind, name, src):
        if kind == "gather":
            dest, rows = _gathered(src), src.shape[0]
        elif kind == "to_sibling":
            dest, rows = SDS((src.shape[0] // 2, src.shape[1]), src.dtype), src.shape[0] // N_DEV
        else:
            dest, rows = SDS(src.shape, src.dtype), src.shape[0] // N_CHIP
        self.transfers[kind, name] = _Transfer(kind, src, dest, rows, PARTS.get((kind, name), 1))

    def _job(self, host):
        moved = [self.transfers[key] for key, _ in CARRIES[host]]
        return moved, _Job([tr.take(count) for tr, (_, count) in zip(moved, CARRIES[host])])

    def carry(self, host, fn, *args, **kw):
        if host not in CARRIES:
            return fn(*args, name=host, **kw)
        moved, job = self._job(host)
        res, landed = fn(*args, name=host, job=job, **kw)
        for tr, arr in zip(moved, landed):
            tr.dest = arr
        return res

    def alone(self, host):
        moved, job = self._job(host)
        for tr, arr in zip(moved, _run_job(job, host)):
            tr.dest = arr

    def result(self, kind, name):
        tr = self.transfers.pop((kind, name))
        assert tr.taken == len(tr.ranges), (kind, name)
        return tr.dest


def _forward_backward(x, target, shards, sharded_small, s):
    c = s["lru_conv_b"].shape[1]
    width = s["attn_out_norm"].shape[1]
    t = x.shape[0]
    traffic = _Traffic()
    carry = traffic.carry
    weight = lambda n: traffic.result("gather", n)

    for n in LARGE:
        traffic.open("gather", n, shards[n])
    traffic.open("gather", "small", sharded_small)
    traffic.alone("gather_ffn1")
    full_small = weight("small").reshape(N_DEV, SUBLANES, c // N_DEV)
    conv_w = full_small[:, :CONV_WIDTH].transpose(1, 0, 2).reshape(CONV_WIDTH, c)
    lam = full_small[:, CONV_WIDTH:CONV_WIDTH + 2].transpose(1, 0, 2).reshape(2, c)
    w = {n: weight(n) for n in ("ffn1_w_in", "ffn1_w_out")}
    u1 = _rmsnorm_fwd(x, s["norm_ffn1"], "norm_ffn1")
    h1, g1, up1 = carry("ffn1_fwd", _ffn_fwd, x, u1, w["ffn1_w_in"], w["ffn1_w_out"])
    w["w_in_mix"], w["w_out_mix"] = weight("w_in_mix"), weight("w_out_mix")
    u2 = _rmsnorm_fwd(h1, s["norm_mix"], "norm_mix")
    proj = carry("mix_in_proj", _mm, u2, w["w_in_mix"], nt=True, out_dtype=F32, tm=512, tn=512)
    gw = _dense_gate_blocks(s["lru_gate_w"]).astype(BF16)
    gb = s["lru_gate_b"].reshape(4, c)
    tables, tables_vjp = jax.vjp(_bias_tables, s["attn_rpb"])
    ya, hf, hb = carry("lru_fwd", _lru_fwd, proj, conv_w, s["lru_conv_b"], gw, gb, lam)
    yb = carry("attn_fwd", _attn_fwd, proj, tables, width)
    y, yt = _mixnorm_fwd(ya, yb, s["lru_out_norm"], s["attn_out_norm"], "mix_norm")
    h2 = carry("mix_out_proj", _mm, y, w["w_out_mix"], nt=False, out_dtype=F32, tm=512, tn=512, residual=h1)
    u3 = _rmsnorm_fwd(h2, s["norm_ffn2"], "norm_ffn2")
    w["ffn2_w_in"], w["ffn2_w_out"] = weight("ffn2_w_in"), weight("ffn2_w_out")
    h3, g2, up2 = carry("ffn2_fwd", _ffn_fwd, h2, u3, w["ffn2_w_in"], w["ffn2_w_out"])
    dh3, df2, loss_part, d_norm_final = _final_loss(h3, s["norm_final"], target, "final_loss")

    grads = {}
    grad_of = dict(nt=False, out_dtype=BF16, tm=512, tn=1024)

    def reduce_in_chip(n):
        traffic.open("to_sibling", n, grads[n])

    def reduce_over_chips(n):
        traffic.open("to_chips", n, _pair_sum(grads[n], traffic.result("to_sibling", n), "pair_sum_" + n))

    du3, hid2_t, da2_t = carry("ffn2_bwd", _ffn_bwd, df2, g2, up2, w["ffn2_w_in"], w["ffn2_w_out"])
    f = hid2_t.shape[0]
    grads["ffn2_w_out"] = carry("ffn2_out_grad", _mm, hid2_t, df2, **grad_of)
    reduce_in_chip("ffn2_w_out")
    grads["ffn2_w_in"] = carry("ffn2_in_grad", _mm, da2_t.reshape(2 * f, t), u3, **grad_of)
    reduce_in_chip("ffn2_w_in")
    reduce_over_chips("ffn2_w_out")
    dh2, dh2b, d_norm_ffn2 = carry("norm_ffn2_bwd", _rmsnorm_bwd, du3, h2, s["norm_ffn2"], dh3, 1.0)
    reduce_over_chips("ffn2_w_in")
    grads["w_out_mix"] = carry("mix_out_grad", _mm, yt, dh2b, **grad_of)
    reduce_in_chip("w_out_mix")
    dy = carry("mix_out_bwd", _mm, dh2b, w["w_out_mix"], nt=True, out_dtype=F32, tm=512, tn=512)
    dya, dyb, d_lru_out_norm, d_attn_out_norm = _mixnorm_bwd(dy, ya, yb, s["lru_out_norm"], s["attn_out_norm"],
                                                             "mix_norm_bwd")
    dq, dk, dv, dqkv_t, d_tables = carry("attn_bwd", _attn_bwd, proj, tables, dyb)
    dx_lru, dg_lru, dxg_t, d_conv_w, d_conv_b, d_gw, d_gb, d_lam = carry(
        "lru_bwd", _lru_bwd, proj, conv_w, s["lru_conv_b"], gw, gb, lam, hf, hb, dya)
    reduce_over_chips("w_out_mix")
    dproj = jnp.concatenate([dx_lru, dg_lru, dq, dk, dv], axis=1)
    dproj_t = jnp.concatenate([dxg_t.reshape(2 * c, t), dqkv_t.reshape(3 * width, t)], axis=0)
    grads["w_in_mix"] = carry("mix_in_grad", _mm, dproj_t, u2, **grad_of)
    reduce_in_chip("w_in_mix")
    du2 = carry("mix_in_bwd", _mm, dproj, w["w_in_mix"], nt=False, out_dtype=F32, tm=512, tn=512)
    reduce_over_chips("w_in_mix")
    dh1, df1, d_norm_mix = carry("norm_mix_bwd", _rmsnorm_bwd, du2, h1, s["norm_mix"], dh2, 0.5)

    by_device = lambda a: a.reshape(a.shape[0], N_DEV, -1).transpose(1, 0, 2)
    small = {
        "norm_mix": d_norm_mix, "lru_conv_b": d_conv_b, "lru_gate_w": _diag_gate_blocks(d_gw, s["lru_gate_w"].shape),
        "lru_gate_b": d_gb.reshape(s["lru_gate_b"].shape), "attn_rpb": tables_vjp(d_tables)[0],
        "lru_out_norm": d_lru_out_norm, "attn_out_norm": d_attn_out_norm, "norm_ffn2": d_norm_ffn2,
        "norm_final": d_norm_final, "lru_conv_w": by_device(d_conv_w), "lru_lambda": by_device(d_lam),
    }
    early = [small[n] for n in SMALL_ORDER[1:]]
    traffic.open("gather", "small_grads", _pack(early))

    du1, hid1_t, da1_t = carry("ffn1_bwd", _ffn_bwd, df1, g1, up1, w["ffn1_w_in"], w["ffn1_w_out"])
    grads["ffn1_w_out"] = carry("ffn1_out_grad", _mm, hid1_t, df1, **grad_of)
    reduce_in_chip("ffn1_w_out")
    gate_rows = carry("ffn1_in_grad_gate", _mm, da1_t, u1, lead=0, out_rows=2 * f, **grad_of)
    reduce_over_chips("ffn1_w_out")
    grads["ffn1_w_in"] = carry("ffn1_in_grad_up", _mm, da1_t, u1, lead=1, out_rows=2 * f, row_offset=f,
                               into=gate_rows, **grad_of)
    reduce_in_chip("ffn1_w_in")
    grad_x, _, d_norm_ffn1 = _rmsnorm_bwd(du1, x, s["norm_ffn1"], dh1, 1.0, "norm_ffn1_bwd")
    traffic.open("gather", "late_grads", _pack([d_norm_ffn1]))
    traffic.alone("to_sibling_ffn1_in")
    reduce_over_chips("ffn1_w_in")
    traffic.alone("to_chips_ffn1")
    partials = {n: traffic.result("to_chips", n) for n in LARGE}
    reduced = (_unpack(_sum_devices(traffic.result("gather", "late_grads"), "sum_late_grads"), [d_norm_ffn1])
               + _unpack(_sum_devices(traffic.result("gather", "small_grads"), "sum_small_grads"), early))
    assert not traffic.transfers, list(traffic.transfers)
    return loss_part[0, 0], grad_x, partials, dict(zip(SMALL_ORDER, reduced))


def _step(x, loss_target, p, m, v):
    me = 4 * lax.axis_index("x") + 2 * lax.axis_index("y") + lax.axis_index("c")

    shards = {n: (_cast_transposed if n in COLUMN_SHARDED else _cast_rows)(p[n], "cast_" + n) for n in LARGE}
    sharded_small = (jnp.pad(p["lru_conv_w"], ((0, SUBLANES - CONV_WIDTH), (0, 0)))
                     + jnp.pad(p["lru_lambda"], ((CONV_WIDTH, SUBLANES - CONV_WIDTH - 2), (0, 0))))
    s = {n: p[n] if n in ("lru_gate_w", "lru_gate_b", "attn_rpb") else p[n].reshape(1, -1) for n in REPLICATED}

    loss_part, grad_x, partials, small = _forward_backward(x, loss_target, shards, sharded_small, s)
    loss = lax.psum(loss_part, ("x", "y", "c"))

    out = {}
    for n in LARGE:
        update = _adamw_cols if n in COLUMN_SHARDED else _adamw_rows
        out[n] = update(p[n], partials[n], m[n], v[n], "adamw_" + n)

    g_small = {n: lax.dynamic_index_in_dim(g, me, axis=0, keepdims=False) if n in SHARDED_SMALL else g
               for n, g in small.items()}
    names = SMALL_ORDER
    like = [p[n] for n in names]
    pack_of = lambda d: _pack([d[n].reshape(p[n].shape) for n in names])
    upd = _adamw_small(pack_of(p), pack_of(g_small), pack_of(m), pack_of(v), "adamw_small")
    for n, d_, m_, v_ in zip(names, *[_unpack(u, like) for u in upd]):
        out[n] = (g_small[n].reshape(p[n].shape), d_, m_, v_)
    return loss, grad_x, out


def kernel(x, norm_ffn1, ffn1_w_in, ffn1_w_out, norm_mix, w_in_mix, lru_conv_w, lru_conv_b, lru_gate_w, lru_gate_b, lru_lambda, attn_rpb, lru_out_norm, attn_out_norm, w_out_mix, norm_ffn2, ffn2_w_in, ffn2_w_out, norm_final, loss_target, m_norm_ffn1, m_ffn1_w_in, m_ffn1_w_out, m_norm_mix, m_w_in_mix, m_lru_conv_w, m_lru_conv_b, m_lru_gate_w, m_lru_gate_b, m_lru_lambda, m_attn_rpb, m_lru_out_norm, m_attn_out_norm, m_w_out_mix, m_norm_ffn2, m_ffn2_w_in, m_ffn2_w_out, m_norm_final, v_norm_ffn1, v_ffn1_w_in, v_ffn1_w_out, v_norm_mix, v_w_in_mix, v_lru_conv_w, v_lru_conv_b, v_lru_gate_w, v_lru_gate_b, v_lru_lambda, v_attn_rpb, v_lru_out_norm, v_attn_out_norm, v_w_out_mix, v_norm_ffn2, v_ffn2_w_in, v_ffn2_w_out, v_norm_final):
    given = dict(locals())
    drop_layer = lambda n, a: a if n == "norm_final" else a[0]
    p = {n: drop_layer(n, given[n]) for n in WEIGHTS}
    m = {n: drop_layer(n, given["m_" + n]) for n in WEIGHTS}
    v = {n: drop_layer(n, given["v_" + n]) for n in WEIGHTS}
    loss, grad_x, out = _step(x[0], loss_target[0], p, m, v)
    shaped = lambda n, a: a.reshape(given[n].shape)
    return (loss, grad_x[None], *[shaped(n, out[n][k]) for k in range(4) for n in WEIGHTS])
```

```python
import math

import numpy as np
import jax
import jax.numpy as jnp
from jax import lax
from jax.experimental import pallas as pl
from jax.experimental.pallas import tpu as pltpu

F32 = jnp.float32
BF16 = jnp.bfloat16
SDS = jax.ShapeDtypeStruct

N_DEV = 8
N_CHIP = 4
NORM_EPS = 1e-6
RG_C = 8.0
CONV_WIDTH = 4
HEAD_DIM = 64
GRID_W = 64
WIN_ROWS = 8
WIN_COLS = 16
NEG = -1e30

ADAM_LR = 0.001
ADAM_B1 = 0.9
ADAM_B2 = 0.999
ADAM_EPS = 1e-08
ADAM_WD = 0.01
ADAM_STEP = 10

LANES = 128
SUBLANES = 8
VMEM_LIMIT = 56 * 1024 * 1024

NT = (((1,), (1,)), ((), ()))
TN = (((0,), (0,)), ((), ()))
ANY = pl.BlockSpec(memory_space=pl.ANY)
WHOLE = pl.BlockSpec(memory_space=pltpu.VMEM)
MESH = pl.DeviceIdType.MESH


def _sigmoid(x):
    return 1.0 / (1.0 + jnp.exp(-x))


def _gelu_parts(x):
    c = math.sqrt(2.0 / math.pi)
    t = jnp.tanh(c * (x + 0.044715 * (x * x * x)))
    gelu = 0.5 * x * (1.0 + t)
    dgelu = 0.5 * (1.0 + t) + 0.5 * x * (1.0 - t * t) * (c * (1.0 + 3.0 * 0.044715 * (x * x)))
    return gelu, dgelu


def _expm1(x):
    poly = x * (1.0 + x * (1.0 / 2) * (1.0 + x * (1.0 / 3) * (1.0 + x * (1.0 / 4) * (1.0 + x * (1.0 / 5) * (1.0 + x * (1.0 / 6))))))
    return jnp.where(jnp.abs(x) < 0.25, poly, jnp.exp(x) - 1.0)


def _softplus(x):
    return jnp.maximum(x, 0.0) + jnp.log1p(jnp.exp(-jnp.abs(x)))


class _Piece:
    N_REMOTE = {"gather": 7, "to_sibling": N_CHIP, "to_chips": 3}
    N_LOCAL = {"gather": 1, "to_sibling": 0, "to_chips": 1}

    def __init__(self, kind, src, dest, lo, hi):
        self.kind, self.src, self.dest, self.lo, self.hi = kind, src, dest, lo, hi


class _Job:
    def __init__(self, pieces):
        self.pieces = list(pieces)
        self.ins = [p.src for p in self.pieces]
        self.out_shapes = [SDS(p.dest.shape, p.dest.dtype) for p in self.pieces]
        self.aliased = [i for i, p in enumerate(self.pieces) if not isinstance(p.dest, SDS)]
        self.n_remote = sum(_Piece.N_REMOTE[p.kind] for p in self.pieces)
        self.n_local = max(sum(_Piece.N_LOCAL[p.kind] for p in self.pieces), 1)

    def _each(self, step, ins, outs, send_sems, recv_sems, local_sems):
        remote = local = 0
        for p, src, dst in zip(self.pieces, ins, outs):
            _EXCHANGES[p.kind](step, p, src, dst, send_sems, recv_sems, local_sems, remote, local)
            remote += _Piece.N_REMOTE[p.kind]
            local += _Piece.N_LOCAL[p.kind]

    def start(self, *refs):
        self._each("start", *refs)

    def finish(self, *refs):
        self._each("relay", *refs)
        self._each("finish", *refs)


def _call(body, *, name, args, out_shape, in_specs, out_specs, grid=(), scratch_shapes=(), aliases=None, job=None):
    single = not isinstance(out_shape, (tuple, list))
    out_shape = (out_shape,) if single else tuple(out_shape)
    out_specs = (out_specs,) if single else tuple(out_specs)
    aliases = dict(aliases or {})
    params = pltpu.CompilerParams(dimension_semantics=("arbitrary",) * len(grid) if grid else None,
                                  vmem_limit_bytes=VMEM_LIMIT)
    if job is None:
        res = pl.pallas_call(body, out_shape=out_shape, grid=grid, in_specs=list(in_specs), out_specs=out_specs,
                             scratch_shapes=list(scratch_shapes), input_output_aliases=aliases, name=name,
                             compiler_params=params)(*args)
        return res[0] if single else res

    n_in, n_out, n_scr = len(args), len(out_shape), len(scratch_shapes)
    j_in, j_out, j_alias = len(job.ins), len(job.out_shapes), len(job.aliased)

    def hosted(*refs):
        ins, refs = refs[:n_in], refs[n_in:]
        j_ins, refs = refs[:j_in], refs[j_in + j_alias:]
        outs, refs = refs[:n_out], refs[n_out:]
        j_outs, refs = refs[:j_out], refs[j_out:]
        scr, sems = refs[:n_scr], refs[n_scr:]
        if grid:
            first = last = None
            for axis, size in enumerate(grid):
                at_first, at_last = pl.program_id(axis) == 0, pl.program_id(axis) == size - 1
                first = at_first if first is None else first & at_first
                last = at_last if last is None else last & at_last
            pl.when(first)(lambda: job.start(j_ins, j_outs, *sems))
            body(*ins, *outs, *scr)
            pl.when(last)(lambda: job.finish(j_ins, j_outs, *sems))
        else:
            job.start(j_ins, j_outs, *sems)
            body(*ins, *outs, *scr)
            job.finish(j_ins, j_outs, *sems)

    res = pl.pallas_call(
        hosted, out_shape=out_shape + tuple(job.out_shapes), grid=grid,
        in_specs=list(in_specs) + [ANY] * (j_in + j_alias), out_specs=out_specs + (ANY,) * j_out,
        scratch_shapes=list(scratch_shapes) + [pltpu.SemaphoreType.DMA((job.n_remote,)),
                                               pltpu.SemaphoreType.DMA((job.n_remote,)),
                                               pltpu.SemaphoreType.DMA((job.n_local,))],
        input_output_aliases={**aliases, **{n_in + j_in + k: n_out + i for k, i in enumerate(job.aliased)}},
        name=name, compiler_params=params)(*args, *job.ins, *[job.pieces[i].dest for i in job.aliased])
    own, carried = res[:n_out], res[n_out:]
    return (own[0] if single else own), carried


def _run_job(job, name):
    return _call(lambda: None, name=name, args=[], out_shape=(), in_specs=[], out_specs=(), job=job)[1]


def _position():
    return lax.axis_index("x"), lax.axis_index("y"), lax.axis_index("c")


def _flat(px, py, pc):
    return 4 * px + 2 * py + pc


def _gather_exchange(step, p, src, dst, send_sems, recv_sems, local_sems, r0, l0):
    x, y, c = _position()
    me, sibling = (x, y, c), (x, y, 1 - c)
    chips = [(1 - x, y), (x, 1 - y), (1 - x, 1 - y)]
    rb, n_rows = p.src.shape[0], p.hi - p.lo
    mine = src.at[pl.ds(p.lo, n_rows), :]

    def rows(block):
        return dst.at[pl.ds(_flat(*block) * rb + p.lo, n_rows), :]

    def copy(k, block, to, own=False):
        return pltpu.make_async_remote_copy(
            src_ref=mine if own else rows(block), dst_ref=rows(block),
            send_sem=send_sems.at[r0 + k], recv_sem=recv_sems.at[r0 + k], device_id=to, device_id_type=MESH)

    local = pltpu.make_async_copy(mine, rows(me), local_sems.at[l0])
    if step == "start":
        local.start()
        copy(0, me, sibling, own=True).start()
        for j, chip in enumerate(chips):
            copy(1 + j, me, (*chip, c), own=True).start()
    elif step == "relay":
        for j, chip in enumerate(chips):
            copy(1 + j, (*chip, c), me).wait_recv()
            copy(4 + j, (*chip, c), sibling).start()
    else:
        copy(0, sibling, me).wait_recv()
        for j, chip in enumerate(chips):
            copy(4 + j, (*chip, 1 - c), me).wait_recv()
        copy(0, me, sibling, own=True).wait_send()
        for j, chip in enumerate(chips):
            copy(1 + j, me, (*chip, c), own=True).wait_send()
            copy(4 + j, (*chip, c), sibling).wait_send()
        local.wait()


def _sibling_exchange(step, p, src, dst, send_sems, recv_sems, local_sems, r0, l0):
    x, y, c = _position()
    rb, n_rows = p.src.shape[0] // N_DEV, p.hi - p.lo
    for q in range(N_CHIP):
        copy = pltpu.make_async_remote_copy(
            src_ref=src.at[pl.ds((2 * q + 1 - c) * rb + p.lo, n_rows), :],
            dst_ref=dst.at[pl.ds(q * rb + p.lo, n_rows), :],
            send_sem=send_sems.at[r0 + q], recv_sem=recv_sems.at[r0 + q], device_id=(x, y, 1 - c), device_id_type=MESH)
        if step == "start":
            copy.start()
        elif step == "finish":
            copy.wait()


CHIP_FLIPS = [(1, 0), (0, 1), (1, 1)]


def _chips_exchange(step, p, src, dst, send_sems, recv_sems, local_sems, r0, l0):
    x, y, c = _position()
    rb, n_rows = p.src.shape[0] // N_CHIP, p.hi - p.lo

    def slot(ref, px, py):
        return ref.at[pl.ds((2 * px + py) * rb + p.lo, n_rows), :]

    def copy(k, landing=False):
        px = 1 - x if CHIP_FLIPS[k][0] else x
        py = 1 - y if CHIP_FLIPS[k][1] else y
        return pltpu.make_async_remote_copy(
            src_ref=slot(dst, px, py) if landing else slot(src, px, py),
            dst_ref=slot(dst, px, py) if landing else slot(dst, x, y),
            send_sem=send_sems.at[r0 + k], recv_sem=recv_sems.at[r0 + k], device_id=(px, py, c), device_id_type=MESH)

    local = pltpu.make_async_copy(slot(src, x, y), slot(dst, x, y), local_sems.at[l0])
    if step == "start":
        local.start()
        for k in range(3):
            copy(k).start()
    elif step == "finish":
        for k in range(3):
            copy(k, landing=True).wait_recv()
        for k in range(3):
            copy(k).wait_send()
        local.wait()


_EXCHANGES = {"gather": _gather_exchange, "to_sibling": _sibling_exchange, "to_chips": _chips_exchange}


def _gathered(shard):
    return SDS((N_DEV * shard.shape[0], shard.shape[1]), shard.dtype)


def _split(rows, parts):
    cuts = [rows * k // parts // 16 * 16 for k in range(parts)] + [rows]
    return list(zip(cuts[:-1], cuts[1:]))


def _pair_sum(g, from_sibling, name):
    rb, n = g.shape[0] // N_DEV, g.shape[1]
    tr = rb if rb * n * 2 <= 3 * 1024 * 1024 else rb // 2
    core = lax.axis_index("c").astype(jnp.int32).reshape(1)

    def body(c_ref, g_ref, r_ref, o_ref):
        o_ref[...] = (g_ref[...].astype(F32) + r_ref[...].astype(F32)).astype(BF16)

    grid_spec = pltpu.PrefetchScalarGridSpec(
        num_scalar_prefetch=1, grid=(N_CHIP, rb // tr),
        in_specs=[pl.BlockSpec((None, None, tr, n), lambda q, i, c_ref: (q, c_ref[0], i, 0)),
                  pl.BlockSpec((None, tr, n), lambda q, i, c_ref: (q, i, 0))],
        out_specs=pl.BlockSpec((None, tr, n), lambda q, i, c_ref: (q, i, 0)))
    out = pl.pallas_call(
        body, grid_spec=grid_spec, out_shape=SDS((N_CHIP, rb, n), BF16), name=name,
        compiler_params=pltpu.CompilerParams(dimension_semantics=("arbitrary",) * 2, vmem_limit_bytes=VMEM_LIMIT))(
            core, g.reshape(N_CHIP, 2, rb, n), from_sibling.reshape(N_CHIP, rb, n))
    return out.reshape(N_CHIP * rb, n)


def _sum_devices(gathered, name):
    r = gathered.shape[0] // N_DEV

    def body(g_ref, o_ref):
        acc = g_ref[0]
        for s in range(1, N_DEV):
            acc = acc + g_ref[s]
        o_ref[...] = acc

    return _call(body, name=name, args=[gathered.reshape(N_DEV, r, LANES)], out_shape=SDS((r, LANES), F32),
                 in_specs=[WHOLE], out_specs=WHOLE)


def _cast_rows(w, name):
    def body(w_ref, o_ref):
        o_ref[...] = w_ref[...].astype(BF16)

    return _call(body, name=name, args=[w], out_shape=SDS(w.shape, BF16), in_specs=[WHOLE], out_specs=WHOLE)


def _cast_transposed(w, name):
    d, n = w.shape
    td = 512

    def body(w_ref, o_ref):
        o_ref[...] = w_ref[...].T.astype(BF16)

    return _call(body, name=name, args=[w], out_shape=SDS((n, d), BF16), grid=(d // td,),
                 in_specs=[pl.BlockSpec((td, n), lambda i: (i, 0))], out_specs=pl.BlockSpec((n, td), lambda i: (0, i)))


ROW_TILE = 256


def _rmsnorm_fwd(h, gain, name):
    t, d = h.shape

    def body(h_ref, g_ref, u_ref):
        x = h_ref[...]
        u_ref[...] = (x * lax.rsqrt(jnp.mean(x * x, axis=-1, keepdims=True) + NORM_EPS) * g_ref[...]).astype(BF16)

    row = pl.BlockSpec((ROW_TILE, d), lambda i: (i, 0))
    return _call(body, name=name, args=[h, gain], out_shape=SDS((t, d), BF16), grid=(t // ROW_TILE,),
                 in_specs=[row, pl.BlockSpec((1, d), lambda i: (0, 0))], out_specs=row)


def _rms_bwd_math(x, gain, dy):
    rstd = lax.rsqrt(jnp.mean(x * x, axis=-1, keepdims=True) + NORM_EPS)
    xhat = x * rstd
    dxh = dy * gain
    dx = rstd * (dxh - xhat * jnp.mean(dxh * xhat, axis=-1, keepdims=True))
    return dx, jnp.sum(dy * xhat, axis=0, keepdims=True)


def _rmsnorm_bwd(du, h, gain, resid, bf_scale, name, job=None):
    t, d = h.shape

    def body(du_ref, h_ref, g_ref, r_ref, dh_ref, dhb_ref, dg_ref):
        @pl.when(pl.program_id(0) == 0)
        def _():
            dg_ref[...] = jnp.zeros_like(dg_ref)

        dx, dg = _rms_bwd_math(h_ref[...], g_ref[...], du_ref[...])
        dh = r_ref[...] + dx
        dh_ref[...] = dh
        dhb_ref[...] = (bf_scale * dh).astype(BF16)
        dg_ref[...] += dg

    row = pl.BlockSpec((ROW_TILE, d), lambda i: (i, 0))
    vec = pl.BlockSpec((1, d), lambda i: (0, 0))
    return _call(body, name=name, args=[du, h, gain, resid],
                 out_shape=(SDS((t, d), F32), SDS((t, d), BF16), SDS((1, d), F32)), grid=(t // ROW_TILE,),
                 in_specs=[row, row, vec, row], out_specs=(row, row, vec), job=job)


def _final_loss(h, gain, target, name):
    t, d = h.shape

    def body(h_ref, g_ref, t_ref, dh_ref, dhb_ref, loss_ref, dg_ref):
        @pl.when(pl.program_id(0) == 0)
        def _():
            dg_ref[...] = jnp.zeros_like(dg_ref)
            loss_ref[...] = jnp.zeros_like(loss_ref)

        x = h_ref[...]
        gain = g_ref[...]
        out = x * lax.rsqrt(jnp.mean(x * x, axis=-1, keepdims=True) + NORM_EPS) * gain
        err = out - t_ref[...]
        loss_ref[...] += 0.5 * jnp.sum(jnp.mean(err * err, axis=-1, keepdims=True), axis=0, keepdims=True)
        dx, dg = _rms_bwd_math(x, gain, err * (1.0 / d))
        dh_ref[...] = dx
        dhb_ref[...] = (0.5 * dx).astype(BF16)
        dg_ref[...] += dg

    row = pl.BlockSpec((ROW_TILE, d), lambda i: (i, 0))
    vec = pl.BlockSpec((1, d), lambda i: (0, 0))
    one = pl.BlockSpec((SUBLANES, LANES), lambda i: (0, 0))
    return _call(body, name=name, args=[h, gain, target],
                 out_shape=(SDS((t, d), F32), SDS((t, d), BF16), SDS((SUBLANES, LANES), F32), SDS((1, d), F32)),
                 grid=(t // ROW_TILE,), in_specs=[row, vec, row], out_specs=(row, row, one, vec))


def _mixnorm_fwd(ya, yb, ga, gb, name):
    t, c = ya.shape

    def body(ya_ref, yb_ref, ga_ref, gb_ref, y_ref, yt_ref):
        for k, (src, g_ref) in enumerate(((ya_ref, ga_ref), (yb_ref, gb_ref))):
            x = src[...]
            u = x * lax.rsqrt(jnp.mean(x * x, axis=-1, keepdims=True) + NORM_EPS) * g_ref[...]
            y_ref[:, k * c:(k + 1) * c] = u.astype(BF16)
            yt_ref[k * c:(k + 1) * c, :] = u.T.astype(BF16)

    row = pl.BlockSpec((ROW_TILE, c), lambda i: (i, 0))
    vec = pl.BlockSpec((1, c), lambda i: (0, 0))
    return _call(body, name=name, args=[ya, yb, ga, gb],
                 out_shape=(SDS((t, 2 * c), BF16), SDS((2 * c, t), BF16)), grid=(t // ROW_TILE,),
                 in_specs=[row, row, vec, vec],
                 out_specs=(pl.BlockSpec((ROW_TILE, 2 * c), lambda i: (i, 0)),
                            pl.BlockSpec((2 * c, ROW_TILE), lambda i: (0, i))))


def _mixnorm_bwd(dy, ya, yb, ga, gb, name):
    t, c = ya.shape

    def body(dy_ref, ya_ref, yb_ref, ga_ref, gb_ref, dya_ref, dyb_ref, dga_ref, dgb_ref):
        @pl.when(pl.program_id(0) == 0)
        def _():
            dga_ref[...] = jnp.zeros_like(dga_ref)
            dgb_ref[...] = jnp.zeros_like(dgb_ref)

        dxa, dga = _rms_bwd_math(ya_ref[...], ga_ref[...], dy_ref[:, :c])
        dxb, dgb = _rms_bwd_math(yb_ref[...], gb_ref[...], dy_ref[:, c:])
        dya_ref[...] = dxa
        dyb_ref[...] = dxb
        dga_ref[...] += dga
        dgb_ref[...] += dgb

    row = pl.BlockSpec((ROW_TILE, c), lambda i: (i, 0))
    vec = pl.BlockSpec((1, c), lambda i: (0, 0))
    return _call(body, name=name, args=[dy, ya, yb, ga, gb],
                 out_shape=(SDS((t, c), F32), SDS((t, c), F32), SDS((1, c), F32), SDS((1, c), F32)),
                 grid=(t // ROW_TILE,),
                 in_specs=[pl.BlockSpec((ROW_TILE, 2 * c), lambda i: (i, 0)), row, row, vec, vec],
                 out_specs=(row, row, vec, vec))


def _tile(n, want):
    return max(t for t in range(LANES, min(n, want) + 1, LANES) if n % t == 0)


def _mm(a, b, *, nt, out_dtype, tm, tn, name, residual=None, lead=None, out_rows=None, row_offset=0, into=None,
        job=None):
    m, k = a.shape[-2:]
    n = b.shape[0] if nt else b.shape[1]
    tm, tn = _tile(m, tm), _tile(n, tn)
    out_rows = m if out_rows is None else out_rows

    def body(a_ref, b_ref, *rest):
        o_ref = rest[-1]
        av, bv = a_ref[...].astype(BF16), b_ref[...].astype(BF16)
        if nt:
            out = lax.dot_general(av, bv, NT, preferred_element_type=F32)
        else:
            out = jnp.dot(av, bv, preferred_element_type=F32)
        if residual is not None:
            out = rest[0][...] + out
        o_ref[...] = out.astype(out_dtype)

    a_spec = (pl.BlockSpec((tm, k), lambda i, j: (i, 0)) if lead is None
              else pl.BlockSpec((None, tm, k), lambda i, j: (lead, i, 0)))
    in_specs = [a_spec, pl.BlockSpec((tn, k), lambda i, j: (j, 0)) if nt else pl.BlockSpec((k, tn), lambda i, j: (0, j))]
    args, aliases = [a, b], {}
    if residual is not None:
        in_specs.append(pl.BlockSpec((tm, tn), lambda i, j: (i, j)))
        args.append(residual)
    if into is not None:
        in_specs.append(ANY)
        aliases[len(args)] = 0
        args.append(into)
    return _call(body, name=name, args=args, out_shape=SDS((out_rows, n), out_dtype), grid=(m // tm, n // tn),
                 in_specs=in_specs, out_specs=pl.BlockSpec((tm, tn), lambda i, j: (row_offset // tm + i, j)),
                 aliases=aliases, job=job)


FFN_TM = 512
FFN_HB = 512


def _ffn_fwd(h, u, w_in_t, w_out, name, job=None):
    t, d = h.shape
    f = w_out.shape[0]
    nk = f // FFN_HB

    def body(u_ref, w_ref, wo_ref, h_ref, hn_ref, g_ref, up_ref, acc):
        k = pl.program_id(1)

        @pl.when(k == 0)
        def _():
            acc[...] = jnp.zeros_like(acc)

        uu = u_ref[...]
        g = lax.dot_general(uu, w_ref[0], NT, preferred_element_type=F32)
        up = lax.dot_general(uu, w_ref[1], NT, preferred_element_type=F32)
        g_ref[...] = g
        up_ref[...] = up
        hid = (g * _sigmoid(g)) * up
        acc[...] += jnp.dot(hid.astype(BF16), wo_ref[...], preferred_element_type=F32)

        @pl.when(k == nk - 1)
        def _():
            hn_ref[...] = h_ref[...] + 0.5 * acc[...]

    tok = pl.BlockSpec((FFN_TM, d), lambda i, k: (i, 0))
    pre = pl.BlockSpec((FFN_TM, FFN_HB), lambda i, k: (i, k))
    return _call(body, name=name, args=[u, w_in_t.reshape(2, f, d), w_out, h],
                 out_shape=(SDS((t, d), F32), SDS((t, f), F32), SDS((t, f), F32)), grid=(t // FFN_TM, nk),
                 in_specs=[tok, pl.BlockSpec((2, FFN_HB, d), lambda i, k: (0, k, 0)),
                           pl.BlockSpec((FFN_HB, d), lambda i, k: (k, 0)), tok],
                 out_specs=(tok, pre, pre), scratch_shapes=[pltpu.VMEM((FFN_TM, d), F32)], job=job)


def _ffn_bwd(dfb, gpre, upre, w_in_t, w_out, name, job=None):
    t, d = dfb.shape
    f = w_out.shape[0]
    nk = f // FFN_HB

    def body(df_ref, g_ref, up_ref, w_ref, wo_ref, du_ref, hid_t_ref, da_t_ref, acc):
        k = pl.program_id(1)

        @pl.when(k == 0)
        def _():
            acc[...] = jnp.zeros_like(acc)

        dhid = lax.dot_general(df_ref[...], wo_ref[...], NT, preferred_element_type=F32)
        g, up = g_ref[...], up_ref[...]
        sig = _sigmoid(g)
        silu = g * sig
        dup = dhid * silu
        dg = dhid * up * (sig * (1.0 + g * (1.0 - sig)))
        hid_t_ref[...] = (silu * up).T.astype(BF16)
        da_t_ref[0] = dg.T.astype(BF16)
        da_t_ref[1] = dup.T.astype(BF16)
        acc[...] += (jnp.dot(dg.astype(BF16), w_ref[0], preferred_element_type=F32)
                     + jnp.dot(dup.astype(BF16), w_ref[1], preferred_element_type=F32))

        @pl.when(k == nk - 1)
        def _():
            du_ref[...] = acc[...]

    tok = pl.BlockSpec((FFN_TM, d), lambda i, k: (i, 0))
    pre = pl.BlockSpec((FFN_TM, FFN_HB), lambda i, k: (i, k))
    return _call(body, name=name, args=[dfb, gpre, upre, w_in_t.reshape(2, f, d), w_out],
                 out_shape=(SDS((t, d), F32), SDS((f, t), BF16), SDS((2, f, t), BF16)), grid=(t // FFN_TM, nk),
                 in_specs=[tok, pre, pre, pl.BlockSpec((2, FFN_HB, d), lambda i, k: (0, k, 0)),
                           pl.BlockSpec((FFN_HB, d), lambda i, k: (k, 0))],
                 out_specs=(tok, pl.BlockSpec((FFN_HB, FFN_TM), lambda i, k: (k, i)),
                            pl.BlockSpec((2, FFN_HB, FFN_TM), lambda i, k: (0, k, i))),
                 scratch_shapes=[pltpu.VMEM((FFN_TM, d), F32)], job=job)


CH = LANES
PAD = SUBLANES


def _lru_gates(xc, gw_ref, gb_ref, lam_ref, z):
    xcb = xc.astype(BF16)
    r = _sigmoid(jnp.dot(xcb, gw_ref[2 * z], preferred_element_type=F32) + gb_ref[pl.ds(2 * z, 1), :])
    i = _sigmoid(jnp.dot(xcb, gw_ref[2 * z + 1], preferred_element_type=F32) + gb_ref[pl.ds(2 * z + 1, 1), :])
    sp = _softplus(-lam_ref[pl.ds(z, 1), :])
    log_a = (-RG_C * r) * sp
    a = jnp.exp(log_a)
    mult = jnp.sqrt(-_expm1(2.0 * log_a))
    return r, i, sp, a, mult


def _conv(xpad, cw_ref, cb_ref, t):
    xc = cb_ref[...] + cw_ref[pl.ds(0, 1), :] * xpad[pl.ds(PAD - 2, t), :]
    for j in range(1, CONV_WIDTH):
        xc = xc + cw_ref[pl.ds(j, 1), :] * xpad[pl.ds(PAD - 2 + j, t), :]
    return xc


def _fill_padded(pad_ref, value, t):
    pad_ref[pl.ds(0, PAD), :] = jnp.zeros((PAD, CH), F32)
    pad_ref[pl.ds(PAD + t, PAD), :] = jnp.zeros((PAD, CH), F32)
    pad_ref[pl.ds(PAD, t), :] = value


def _scan_pair(t, a_up, b_up, out_up, a_down, b_down, out_down):
    def step(tt, carry):
        hu, hd = carry
        lo = pl.multiple_of(tt * SUBLANES, SUBLANES)
        hi = pl.multiple_of(t - SUBLANES - tt * SUBLANES, SUBLANES)
        for j in range(SUBLANES):
            su, sd = pl.ds(lo + j, 1), pl.ds(hi + SUBLANES - 1 - j, 1)
            hu = a_up(su) * hu + b_up(su)
            out_up[su, :] = hu
            hd = a_down(sd) * hd + b_down(sd)
            out_down[sd, :] = hd
        return hu, hd

    zero = jnp.zeros((1, CH), F32)
    lax.fori_loop(0, t // SUBLANES, step, (zero, zero))


def _lru_fwd(proj, cw, cb, gw, gb, lam, name, job=None):
    t = proj.shape[0]
    c = cw.shape[1]
    ncb = c // CH

    def body(x_ref, g_ref, cw_ref, cb_ref, gw_ref, gb_ref, lam_ref, ya_ref, hf_ref, hb_ref, xpad, a0, b0, a1, b1):
        _fill_padded(xpad, x_ref[...], t)
        xc = _conv(xpad, cw_ref, cb_ref, t)
        for z, (a_s, b_s) in enumerate(((a0, b0), (a1, b1))):
            _, i, _, a, mult = _lru_gates(xc, gw_ref, gb_ref, lam_ref, z)
            a_s[...] = a
            b_s[...] = mult * (i * xc)
        _scan_pair(t, lambda s: a0[s, :], lambda s: b0[s, :], hf_ref, lambda s: a1[s, :], lambda s: b1[s, :], hb_ref)
        gelu, _ = _gelu_parts(g_ref[...])
        ya_ref[...] = gelu * (hf_ref[...] + hb_ref[...])

    col = lambda off: pl.BlockSpec((t, CH), lambda i: (0, off + i))
    small = lambda rows: pl.BlockSpec((rows, CH), lambda i: (0, i))
    return _call(body, name=name, args=[proj, proj, cw, cb, gw, gb, lam], out_shape=(SDS((t, c), F32),) * 3,
                 grid=(ncb,),
                 in_specs=[col(0), col(ncb), small(CONV_WIDTH), small(1),
                           pl.BlockSpec((4, None, CH, CH), lambda i: (0, i, 0, 0)), small(4), small(2)],
                 out_specs=(col(0),) * 3,
                 scratch_shapes=[pltpu.VMEM((t + 2 * PAD, CH), F32)] + [pltpu.VMEM((t, CH), F32)] * 4, job=job)


def _lru_bwd(proj, cw, cb, gw, gb, lam, hf, hb, dya, name, job=None):
    t = proj.shape[0]
    c = cw.shape[1]
    ncb = c // CH

    def body(x_ref, g_ref, cw_ref, cb_ref, gw_ref, gb_ref, lam_ref, hf_ref, hb_ref, dya_ref,
             dx_ref, dg_ref, dt_ref, dcw_ref, dcb_ref, dgw_ref, dgb_ref, dlam_ref,
             xpad, hpad, dxc, a0, a1, dhs, dh0, dh1):
        _fill_padded(xpad, x_ref[...], t)
        xc = _conv(xpad, cw_ref, cb_ref, t)
        xcb = xc.astype(BF16)
        gates = [_lru_gates(xc, gw_ref, gb_ref, lam_ref, z) for z in range(2)]
        a0[...] = gates[0][3]
        a1[...] = gates[1][3]

        gelu, dgelu = _gelu_parts(g_ref[...])
        dya = dya_ref[...]
        dgate = dya * (hf_ref[...] + hb_ref[...]) * dgelu
        dg_ref[...] = dgate.astype(BF16)
        dt_ref[1] = dgate.T.astype(BF16)
        dhs[...] = dya * gelu

        def step(tt, carry):
            c0, p0, c1, p1 = carry
            lo = pl.multiple_of(tt * SUBLANES, SUBLANES)
            hi = pl.multiple_of(t - SUBLANES - tt * SUBLANES, SUBLANES)
            for j in range(SUBLANES):
                su, sd = pl.ds(lo + j, 1), pl.ds(hi + SUBLANES - 1 - j, 1)
                c0 = dhs[sd, :] + p0 * c0
                dh0[sd, :] = c0
                p0 = a0[sd, :]
                c1 = dhs[su, :] + p1 * c1
                dh1[su, :] = c1
                p1 = a1[su, :]
            return c0, p0, c1, p1

        zero = jnp.zeros((1, CH), F32)
        lax.fori_loop(0, t // SUBLANES, step, (zero, zero, zero, zero))

        acc_dxc = jnp.zeros((t, CH), F32)
        for z, (h_ref, dh_ref, shift) in enumerate(((hf_ref, dh0, -1), (hb_ref, dh1, 1))):
            r, i, sp, a, mult = gates[z]
            _fill_padded(hpad, h_ref[...], t)
            h_nb = hpad[pl.ds(PAD + shift, t), :]
            db = dh_ref[...]
            da = db * h_nb
            d_i = db * mult * xc
            acc_dxc = acc_dxc + db * mult * i
            d_mult = db * i * xc
            d_la = da * a - d_mult * (a * a) / mult
            d_r = d_la * (-RG_C * sp)
            dlam_ref[pl.ds(z, 1), :] = (jnp.sum(d_la * (-RG_C * r), axis=0, keepdims=True)
                                        * (-_sigmoid(-lam_ref[pl.ds(z, 1), :])))
            for gate, d_pre in ((0, d_r * r * (1.0 - r)), (1, d_i * i * (1.0 - i))):
                zg = 2 * z + gate
                dgb_ref[pl.ds(zg, 1), :] = jnp.sum(d_pre, axis=0, keepdims=True)
                d_pre_b = d_pre.astype(BF16)
                dgw_ref[zg] = lax.dot_general(xcb, d_pre_b, TN, preferred_element_type=F32)
                acc_dxc = acc_dxc + lax.dot_general(d_pre_b, gw_ref[zg], NT, preferred_element_type=F32)

        dcb_ref[...] = jnp.sum(acc_dxc, axis=0, keepdims=True)
        for j in range(CONV_WIDTH):
            dcw_ref[pl.ds(j, 1), :] = jnp.sum(acc_dxc * xpad[pl.ds(PAD - 2 + j, t), :], axis=0, keepdims=True)
        _fill_padded(dxc, acc_dxc, t)
        dx = cw_ref[pl.ds(0, 1), :] * dxc[pl.ds(PAD + 2, t), :]
        for j in range(1, CONV_WIDTH):
            dx = dx + cw_ref[pl.ds(j, 1), :] * dxc[pl.ds(PAD + 2 - j, t), :]
        dx_ref[...] = dx.astype(BF16)
        dt_ref[0] = dx.T.astype(BF16)

    col = lambda off: pl.BlockSpec((t, CH), lambda i: (0, off + i))
    small = lambda rows: pl.BlockSpec((rows, CH), lambda i: (0, i))
    dense = pl.BlockSpec((4, None, CH, CH), lambda i: (0, i, 0, 0))
    padded = pltpu.VMEM((t + 2 * PAD, CH), F32)
    return _call(
        body, name=name, args=[proj, proj, cw, cb, gw, gb, lam, hf, hb, dya],
        out_shape=(SDS((t, c), BF16), SDS((t, c), BF16), SDS((2, c, t), BF16), SDS((CONV_WIDTH, c), F32),
                   SDS((1, c), F32), SDS((4, ncb, CH, CH), F32), SDS((4, c), F32), SDS((2, c), F32)),
        grid=(ncb,),
        in_specs=[col(0), col(ncb), small(CONV_WIDTH), small(1), dense, small(4), small(2), col(0), col(0), col(0)],
        out_specs=(col(0), col(0), pl.BlockSpec((2, CH, t), lambda i: (0, i, 0)), small(CONV_WIDTH), small(1),
                   dense, small(4), small(2)),
        scratch_shapes=[padded, padded, padded] + [pltpu.VMEM((t, CH), F32)] * 5, job=job)


Q_ROWS = 4
BAND_ROWS = WIN_ROWS + Q_ROWS
BAND_PAIRS = BAND_ROWS // 2
Q_BLOCK = Q_ROWS * GRID_W
BAND = BAND_ROWS * GRID_W
PAIR_W = 2 * GRID_W
N_BOTH = 2 * WIN_ROWS - 2
ENTRY_LEFT_OUT, ENTRY_RIGHT_OUT, ENTRY_OUT = N_BOTH, N_BOTH + 1, N_BOTH + 2
N_ENTRIES = N_BOTH + 3


def _bias_tables(rpb):
    cols = np.arange(GRID_W)
    start = np.clip(cols - WIN_COLS // 2, 0, GRID_W - WIN_COLS)
    valid = (cols[None, :] >= start[:, None]) & (cols[None, :] < start[:, None] + WIN_COLS)
    col_off = np.clip(cols[None, :] - cols[:, None] + WIN_COLS - 1, 0, 2 * WIN_COLS - 2)
    pick_col = jnp.asarray(np.eye(2 * WIN_COLS - 1, dtype=np.float32)[col_off] * valid[..., None])
    by_row = jnp.einsum("hrc,qkc->hrqk", rpb, pick_col, precision=lax.Precision.HIGHEST)
    by_row = jnp.where(jnp.asarray(valid)[None, None], by_row, NEG)
    out = jnp.full_like(by_row[:, :1], NEG)
    first_in, last_in = WIN_ROWS - 1 - WIN_ROWS // 2, 2 * (WIN_ROWS - 1) - WIN_ROWS // 2
    both = jnp.concatenate([by_row[:, :-1], by_row[:, 1:]], axis=-1)
    left_out = jnp.concatenate([out, by_row[:, first_in:first_in + 1]], axis=-1)
    right_out = jnp.concatenate([by_row[:, last_in:last_in + 1], out], axis=-1)
    return jnp.concatenate([both, left_out, right_out, jnp.concatenate([out, out], axis=-1)], axis=1)


def _band_start(m, rows):
    return jnp.clip(Q_ROWS * m - WIN_ROWS // 2, 0, rows - BAND_ROWS)


def _entry(r, key_row, rows):
    w0 = jnp.clip(r - WIN_ROWS // 2, 0, rows - WIN_ROWS)
    left = (key_row >= w0) & (key_row < w0 + WIN_ROWS)
    right = (key_row + 1 >= w0) & (key_row + 1 < w0 + WIN_ROWS)
    return jnp.where(left & right, key_row - r + WIN_ROWS - 1,
                     jnp.where(right, ENTRY_LEFT_OUT, jnp.where(left, ENTRY_RIGHT_OUT, ENTRY_OUT)))


def _transposed_pairs(dst, src_ref):
    for g in range(dst.shape[0]):
        dst[g] = src_ref[pl.ds(g * PAIR_W, PAIR_W), :].T.astype(BF16)


def _band_of(pairs_ref, first_pair, hh):
    heads = pl.ds(hh * HEAD_DIM, HEAD_DIM)
    return jnp.concatenate([pairs_ref[first_pair + g, heads, :] for g in range(BAND_PAIRS)], axis=1)


def _attn_block(qs, kt, tz_ref, hh, m, rows):
    rs = _band_start(m, rows)
    lanes = pl.ds(hh * HEAD_DIM, HEAD_DIM)
    qrows = pl.ds(pl.multiple_of(m * Q_BLOCK, Q_BLOCK), Q_BLOCK)
    band = pl.ds(pl.multiple_of(rs * GRID_W, PAIR_W), BAND)
    entries = [[_entry(Q_ROWS * m + i, rs + 2 * g, rows) for g in range(BAND_PAIRS)] for i in range(Q_ROWS)]
    bias = jnp.concatenate([jnp.concatenate([tz_ref[hh, e] for e in row], axis=1) for row in entries], axis=0)
    q = qs[qrows, lanes]
    s = jnp.dot(q, _band_of(kt, rs // 2, hh), preferred_element_type=F32) * (HEAD_DIM ** -0.5) + bias
    p = jnp.exp(s - jnp.max(s, axis=-1, keepdims=True))
    p = p / jnp.sum(p, axis=-1, keepdims=True)
    return q, p, qrows, band, lanes, entries, rs // 2


def _attn_fwd(proj, tables, width, name, job=None):
    t = proj.shape[0]
    rows = t // GRID_W
    npair = width // LANES
    first = (proj.shape[1] - 3 * width) // LANES

    def body(q_ref, k_ref, v_ref, tz_ref, o_ref, qs, vs, kt):
        qs[...] = q_ref[...].astype(BF16)
        vs[...] = v_ref[...].astype(BF16)
        _transposed_pairs(kt, k_ref)

        def block(m, carry):
            for hh in range(2):
                _, p, qrows, band, lanes, _, _ = _attn_block(qs, kt, tz_ref, hh, m, rows)
                o_ref[qrows, lanes] = jnp.dot(p.astype(BF16), vs[band, lanes], preferred_element_type=F32)
            return carry

        lax.fori_loop(0, rows // Q_ROWS, block, 0)

    col = lambda off: pl.BlockSpec((t, LANES), lambda i: (0, off + i))
    return _call(body, name=name, args=[proj, proj, proj, tables], out_shape=SDS((t, width), F32), grid=(npair,),
                 in_specs=[col(first), col(first + npair), col(first + 2 * npair),
                           pl.BlockSpec((2, N_ENTRIES, GRID_W, PAIR_W), lambda i: (i, 0, 0, 0))],
                 out_specs=col(0),
                 scratch_shapes=[pltpu.VMEM((t, LANES), BF16)] * 2 + [pltpu.VMEM((t // PAIR_W, LANES, PAIR_W), BF16)],
                 job=job)


def _attn_bwd(proj, tables, dyb, name, job=None):
    t, width = dyb.shape
    rows = t // GRID_W
    npair = width // LANES
    first = (proj.shape[1] - 3 * width) // LANES

    def body(q_ref, k_ref, v_ref, tz_ref, do_ref, dq_ref, dk_ref, dv_ref, dt_ref, dtz_ref, dq_s, dk_s, dv_s,
             qs, ks, vs, dos, kt, vt):
        qs[...] = q_ref[...].astype(BF16)
        ks[...] = k_ref[...].astype(BF16)
        vs[...] = v_ref[...].astype(BF16)
        dos[...] = do_ref[...].astype(BF16)
        _transposed_pairs(kt, k_ref)
        _transposed_pairs(vt, v_ref)
        dk_s[...] = jnp.zeros_like(dk_s)
        dv_s[...] = jnp.zeros_like(dv_s)
        dtz_ref[...] = jnp.zeros_like(dtz_ref)

        def block(m, carry):
            for hh in range(2):
                q, p, qrows, band, lanes, entries, first_pair = _attn_block(qs, kt, tz_ref, hh, m, rows)
                do = dos[qrows, lanes]
                dp = jnp.dot(do, _band_of(vt, first_pair, hh), preferred_element_type=F32)
                ds = p * (dp - jnp.sum(dp * p, axis=-1, keepdims=True))
                for i, row in enumerate(entries):
                    for g, e in enumerate(row):
                        dtz_ref[hh, e] += ds[i * GRID_W:(i + 1) * GRID_W, g * PAIR_W:(g + 1) * PAIR_W]
                dsb = (ds * (HEAD_DIM ** -0.5)).astype(BF16)
                dq_s[qrows, lanes] = jnp.dot(dsb, ks[band, lanes], preferred_element_type=F32)
                dk_s[band, lanes] += lax.dot_general(dsb, q, TN, preferred_element_type=F32)
                dv_s[band, lanes] += lax.dot_general(p.astype(BF16), do, TN, preferred_element_type=F32)
            return carry

        lax.fori_loop(0, rows // Q_ROWS, block, 0)
        for n, (src, dst) in enumerate(((dq_s, dq_ref), (dk_s, dk_ref), (dv_s, dv_ref))):
            val = src[...]
            dst[...] = val.astype(BF16)
            dt_ref[n] = val.T.astype(BF16)

    col = lambda off: pl.BlockSpec((t, LANES), lambda i: (0, off + i))
    table = pl.BlockSpec((2, N_ENTRIES, GRID_W, PAIR_W), lambda i: (i, 0, 0, 0))
    pairs = pltpu.VMEM((t // PAIR_W, LANES, PAIR_W), BF16)
    return _call(body, name=name, args=[proj, proj, proj, tables, dyb],
                 out_shape=(SDS((t, width), BF16),) * 3 + (SDS((3, width, t), BF16), SDS(tables.shape, F32)),
                 grid=(npair,),
                 in_specs=[col(first), col(first + npair), col(first + 2 * npair), table, col(0)],
                 out_specs=(col(0), col(0), col(0), pl.BlockSpec((3, LANES, t), lambda i: (0, i, 0)), table),
                 scratch_shapes=[pltpu.VMEM((t, LANES), F32)] * 3 + [pltpu.VMEM((t, LANES), BF16)] * 4 + [pairs, pairs],
                 job=job)


def _adamw_math(w, g, m, v):
    m = ADAM_B1 * m + (1.0 - ADAM_B1) * g
    v = ADAM_B2 * v + (1.0 - ADAM_B2) * (g * g)
    m_hat = m / (1.0 - ADAM_B1 ** ADAM_STEP)
    v_hat = v / (1.0 - ADAM_B2 ** ADAM_STEP)
    delta = -ADAM_LR * (m_hat / (jnp.sqrt(v_hat) + ADAM_EPS) + ADAM_WD * w)
    return delta, m, v


def _sum_partials(p_ref):
    g = p_ref[0].astype(F32)
    for s in range(1, N_CHIP):
        g = g + p_ref[s].astype(F32)
    return g


def _adamw_rows(w, partials, m, v, name):
    rb, n = w.shape
    tr = 64

    def body(w_ref, p_ref, m_ref, v_ref, g_ref, d_ref, nm_ref, nv_ref):
        g = _sum_partials(p_ref)
        g_ref[...] = g
        d_ref[...], nm_ref[...], nv_ref[...] = _adamw_math(w_ref[...], g, m_ref[...], v_ref[...])

    blk = pl.BlockSpec((tr, n), lambda i: (i, 0))
    return _call(body, name=name, args=[w, partials.reshape(N_CHIP, rb, n), m, v], out_shape=(SDS((rb, n), F32),) * 4,
                 grid=(rb // tr,), in_specs=[blk, pl.BlockSpec((N_CHIP, tr, n), lambda i: (0, i, 0)), blk, blk],
                 out_specs=(blk,) * 4)


def _adamw_cols(w, partials, m, v, name):
    d, nb = w.shape
    td = 256

    def body(w_ref, p_ref, m_ref, v_ref, g_ref, d_ref, nm_ref, nv_ref):
        g = _sum_partials(p_ref).T
        g_ref[...] = g
        d_ref[...], nm_ref[...], nv_ref[...] = _adamw_math(w_ref[...], g, m_ref[...], v_ref[...])

    blk = pl.BlockSpec((td, nb), lambda i: (i, 0))
    return _call(body, name=name, args=[w, partials.reshape(N_CHIP, nb, d), m, v], out_shape=(SDS((d, nb), F32),) * 4,
                 grid=(d // td,), in_specs=[blk, pl.BlockSpec((N_CHIP, nb, td), lambda i: (0, 0, i)), blk, blk],
                 out_specs=(blk,) * 4)


def _adamw_small(w, g, m, v, name):
    def body(w_ref, g_ref, m_ref, v_ref, d_ref, nm_ref, nv_ref):
        d_ref[...], nm_ref[...], nv_ref[...] = _adamw_math(w_ref[...], g_ref[...], m_ref[...], v_ref[...])

    return _call(body, name=name, args=[w, g, m, v], out_shape=(SDS(w.shape, F32),) * 3, in_specs=[WHOLE] * 4,
                 out_specs=(WHOLE,) * 3)


TILE = SUBLANES * LANES


def _pack(arrays):
    parts = []
    for a in arrays:
        flat = a.reshape(-1).astype(F32)
        flat = jnp.pad(flat, (0, -flat.size % TILE))
        parts.append(flat.reshape(-1, LANES))
    return jnp.concatenate(parts, axis=0)


def _unpack(pack, like):
    out, row = [], 0
    for a in like:
        n = int(np.prod(a.shape))
        nrows = -(-n // TILE) * SUBLANES
        out.append(pack[row:row + nrows].reshape(-1)[:n].reshape(a.shape))
        row += nrows
    return out


def _dense_gate_blocks(gate_w):
    w = gate_w.reshape(4, -1, 2, HEAD_DIM, HEAD_DIM)
    zero = jnp.zeros_like(w[:, :, 0])
    top = jnp.concatenate([w[:, :, 0], zero], axis=-1)
    bottom = jnp.concatenate([zero, w[:, :, 1]], axis=-1)
    return jnp.concatenate([top, bottom], axis=-2)


def _diag_gate_blocks(dense, shape):
    even = dense[:, :, :HEAD_DIM, :HEAD_DIM]
    odd = dense[:, :, HEAD_DIM:, HEAD_DIM:]
    return jnp.stack([even, odd], axis=2).reshape(shape)


LARGE = ("ffn1_w_in", "ffn1_w_out", "w_in_mix", "w_out_mix", "ffn2_w_in", "ffn2_w_out")
COLUMN_SHARDED = ("ffn1_w_in", "w_in_mix", "ffn2_w_in")
SHARDED_SMALL = ("lru_conv_w", "lru_lambda")
REPLICATED = ("norm_ffn1", "norm_mix", "lru_conv_b", "lru_gate_w", "lru_gate_b", "attn_rpb", "lru_out_norm",
              "attn_out_norm", "norm_ffn2", "norm_final")
SMALL_ORDER = REPLICATED + SHARDED_SMALL
WEIGHTS = ("norm_ffn1", "ffn1_w_in", "ffn1_w_out", "norm_mix", "w_in_mix", "lru_conv_w", "lru_conv_b", "lru_gate_w",
           "lru_gate_b", "lru_lambda", "attn_rpb", "lru_out_norm", "attn_out_norm", "w_out_mix", "norm_ffn2",
           "ffn2_w_in", "ffn2_w_out", "norm_final")


PARTS = {("gather", "ffn2_w_out"): 4, ("gather", "ffn2_w_in"): 8, ("to_chips", "ffn2_w_out"): 4,
         ("to_chips", "ffn2_w_in"): 8, ("to_chips", "ffn1_w_out"): 2}
CARRIES = {
    "gather_ffn1": [(("gather", "ffn1_w_in"), 1), (("gather", "ffn1_w_out"), 1), (("gather", "small"), 1)],
    "ffn1_fwd": [(("gather", "w_in_mix"), 1), (("gather", "w_out_mix"), 1)],
    "mix_in_proj": [(("gather", "ffn2_w_out"), 3)],
    "lru_fwd": [(("gather", "ffn2_w_out"), 1), (("gather", "ffn2_w_in"), 2)],
    "attn_fwd": [(("gather", "ffn2_w_in"), 3)],
    "mix_out_proj": [(("gather", "ffn2_w_in"), 3)],
    "ffn2_in_grad": [(("to_sibling", "ffn2_w_out"), 1)],
    "norm_ffn2_bwd": [(("to_sibling", "ffn2_w_in"), 1)],
    "mix_out_grad": [(("to_chips", "ffn2_w_out"), 1)],
    "mix_out_bwd": [(("to_chips", "ffn2_w_out"), 1)],
    "attn_bwd": [(("to_chips", "ffn2_w_out"), 2), (("to_chips", "ffn2_w_in"), 2)],
    "lru_bwd": [(("to_chips", "ffn2_w_in"), 3), (("to_sibling", "w_out_mix"), 1)],
    "mix_in_grad": [(("to_chips", "ffn2_w_in"), 2)],
    "mix_in_bwd": [(("to_chips", "ffn2_w_in"), 1), (("to_sibling", "w_in_mix"), 1)],
    "norm_mix_bwd": [(("to_chips", "w_out_mix"), 1)],
    "ffn1_bwd": [(("to_chips", "w_in_mix"), 1), (("gather", "small_grads"), 1)],
    "ffn1_in_grad_gate": [(("to_sibling", "ffn1_w_out"), 1)],
    "ffn1_in_grad_up": [(("to_chips", "ffn1_w_out"), 1)],
    "to_sibling_ffn1_in": [(("to_sibling", "ffn1_w_in"), 1), (("to_chips", "ffn1_w_out"), 1)],
    "to_chips_ffn1": [(("to_chips", "ffn1_w_in"), 1), (("gather", "late_grads"), 1)],
}


class _Transfer:
    def __init__(self, kind, src, dest, block_rows, parts):
        self.kind, self.src, self.dest = kind, src, dest
        self.ranges, self.taken = _split(block_rows, parts), 0

    def take(self, count):
        lo, hi = self.ranges[self.taken][0], self.ranges[self.taken + count - 1][1]
        self.taken += count
        return _Piece(self.kind, self.src, self.dest, lo, hi)


class _Traffic:
    def __init__(self):
        self.transfers = {}

    def open(self, kind, name, src):
        if kind == "gather":
            dest, rows = _gathered(src), src.shape[0]
        elif kind == "to_sibling":
            dest, rows = SDS((src.shape[0] // 2, src.shape[1]), src.dtype), src.shape[0] // N_DEV
        else:
            dest, rows = SDS(src.shape, src.dtype), src.shape[0] // N_CHIP
        self.transfers[kind, name] = _Transfer(kind, src, dest, rows, PARTS.get((kind, name), 1))

    def _job(self, host):
        moved = [self.transfers[key] for key, _ in CARRIES[host]]
        return moved, _Job([tr.take(count) for tr, (_, count) in zip(moved, CARRIES[host])])

    def carry(self, host, fn, *args, **kw):
        if host not in CARRIES:
            return fn(*args, name=host, **kw)
        moved, job = self._job(host)
        res, landed = fn(*args, name=host, job=job, **kw)
        for tr, arr in zip(moved, landed):
            tr.dest = arr
        return res

    def alone(self, host):
        moved, job = self._job(host)
        for tr, arr in zip(moved, _run_job(job, host)):
            tr.dest = arr

    def result(self, kind, name):
        tr = self.transfers.pop((kind, name))
        assert tr.taken == len(tr.ranges), (kind, name)
        return tr.dest


def _forward_backward(x, target, shards, sharded_small, s):
    c = s["lru_conv_b"].shape[1]
    width = s["attn_out_norm"].shape[1]
    t = x.shape[0]
    traffic = _Traffic()
    carry = traffic.carry
    weight = lambda n: traffic.result("gather", n)

    for n in LARGE:
        traffic.open("gather", n, shards[n])
    traffic.open("gather", "small", sharded_small)
    traffic.alone("gather_ffn1")
    full_small = weight("small").reshape(N_DEV, SUBLANES, c // N_DEV)
    conv_w = full_small[:, :CONV_WIDTH].transpose(1, 0, 2).reshape(CONV_WIDTH, c)
    lam = full_small[:, CONV_WIDTH:CONV_WIDTH + 2].transpose(1, 0, 2).reshape(2, c)
    w = {n: weight(n) for n in ("ffn1_w_in", "ffn1_w_out")}
    u1 = _rmsnorm_fwd(x, s["norm_ffn1"], "norm_ffn1")
    h1, g1, up1 = carry("ffn1_fwd", _ffn_fwd, x, u1, w["ffn1_w_in"], w["ffn1_w_out"])
    w["w_in_mix"], w["w_out_mix"] = weight("w_in_mix"), weight("w_out_mix")
    u2 = _rmsnorm_fwd(h1, s["norm_mix"], "norm_mix")
    proj = carry("mix_in_proj", _mm, u2, w["w_in_mix"], nt=True, out_dtype=F32, tm=512, tn=512)
    gw = _dense_gate_blocks(s["lru_gate_w"]).astype(BF16)
    gb = s["lru_gate_b"].reshape(4, c)
    tables, tables_vjp = jax.vjp(_bias_tables, s["attn_rpb"])
    ya, hf, hb = carry("lru_fwd", _lru_fwd, proj, conv_w, s["lru_conv_b"], gw, gb, lam)
    yb = carry("attn_fwd", _attn_fwd, proj, tables, width)
    y, yt = _mixnorm_fwd(ya, yb, s["lru_out_norm"], s["attn_out_norm"], "mix_norm")
    h2 = carry("mix_out_proj", _mm, y, w["w_out_mix"], nt=False, out_dtype=F32, tm=512, tn=512, residual=h1)
    u3 = _rmsnorm_fwd(h2, s["norm_ffn2"], "norm_ffn2")
    w["ffn2_w_in"], w["ffn2_w_out"] = weight("ffn2_w_in"), weight("ffn2_w_out")
    h3, g2, up2 = carry("ffn2_fwd", _ffn_fwd, h2, u3, w["ffn2_w_in"], w["ffn2_w_out"])
    dh3, df2, loss_part, d_norm_final = _final_loss(h3, s["norm_final"], target, "final_loss")

    grads = {}
    grad_of = dict(nt=False, out_dtype=BF16, tm=512, tn=1024)

    def reduce_in_chip(n):
        traffic.open("to_sibling", n, grads[n])

    def reduce_over_chips(n):
        traffic.open("to_chips", n, _pair_sum(grads[n], traffic.result("to_sibling", n), "pair_sum_" + n))

    du3, hid2_t, da2_t = carry("ffn2_bwd", _ffn_bwd, df2, g2, up2, w["ffn2_w_in"], w["ffn2_w_out"])
    f = hid2_t.shape[0]
    grads["ffn2_w_out"] = carry("ffn2_out_grad", _mm, hid2_t, df2, **grad_of)
    reduce_in_chip("ffn2_w_out")
    grads["ffn2_w_in"] = carry("ffn2_in_grad", _mm, da2_t.reshape(2 * f, t), u3, **grad_of)
    reduce_in_chip("ffn2_w_in")
    reduce_over_chips("ffn2_w_out")
    dh2, dh2b, d_norm_ffn2 = carry("norm_ffn2_bwd", _rmsnorm_bwd, du3, h2, s["norm_ffn2"], dh3, 1.0)
    reduce_over_chips("ffn2_w_in")
    grads["w_out_mix"] = carry("mix_out_grad", _mm, yt, dh2b, **grad_of)
    reduce_in_chip("w_out_mix")
    dy = carry("mix_out_bwd", _mm, dh2b, w["w_out_mix"], nt=True, out_dtype=F32, tm=512, tn=512)
    dya, dyb, d_lru_out_norm, d_attn_out_norm = _mixnorm_bwd(dy, ya, yb, s["lru_out_norm"], s["attn_out_norm"],
                                                             "mix_norm_bwd")
    dq, dk, dv, dqkv_t, d_tables = carry("attn_bwd", _attn_bwd, proj, tables, dyb)
    dx_lru, dg_lru, dxg_t, d_conv_w, d_conv_b, d_gw, d_gb, d_lam = carry(
        "lru_bwd", _lru_bwd, proj, conv_w, s["lru_conv_b"], gw, gb, lam, hf, hb, dya)
    reduce_over_chips("w_out_mix")
    dproj = jnp.concatenate([dx_lru, dg_lru, dq, dk, dv], axis=1)
    dproj_t = jnp.concatenate([dxg_t.reshape(2 * c, t), dqkv_t.reshape(3 * width, t)], axis=0)
    grads["w_in_mix"] = carry("mix_in_grad", _mm, dproj_t, u2, **grad_of)
    reduce_in_chip("w_in_mix")
    du2 = carry("mix_in_bwd", _mm, dproj, w["w_in_mix"], nt=False, out_dtype=F32, tm=512, tn=512)
    reduce_over_chips("w_in_mix")
    dh1, df1, d_norm_mix = carry("norm_mix_bwd", _rmsnorm_bwd, du2, h1, s["norm_mix"], dh2, 0.5)

    by_device = lambda a: a.reshape(a.shape[0], N_DEV, -1).transpose(1, 0, 2)
    small = {
        "norm_mix": d_norm_mix, "lru_conv_b": d_conv_b, "lru_gate_w": _diag_gate_blocks(d_gw, s["lru_gate_w"].shape),
        "lru_gate_b": d_gb.reshape(s["lru_gate_b"].shape), "attn_rpb": tables_vjp(d_tables)[0],
        "lru_out_norm": d_lru_out_norm, "attn_out_norm": d_attn_out_norm, "norm_ffn2": d_norm_ffn2,
        "norm_final": d_norm_final, "lru_conv_w": by_device(d_conv_w), "lru_lambda": by_device(d_lam),
    }
    early = [small[n] for n in SMALL_ORDER[1:]]
    traffic.open("gather", "small_grads", _pack(early))

    du1, hid1_t, da1_t = carry("ffn1_bwd", _ffn_bwd, df1, g1, up1, w["ffn1_w_in"], w["ffn1_w_out"])
    grads["ffn1_w_out"] = carry("ffn1_out_grad", _mm, hid1_t, df1, **grad_of)
    reduce_in_chip("ffn1_w_out")
    gate_rows = carry("ffn1_in_grad_gate", _mm, da1_t, u1, lead=0, out_rows=2 * f, **grad_of)
    reduce_over_chips("ffn1_w_out")
    grads["ffn1_w_in"] = carry("ffn1_in_grad_up", _mm, da1_t, u1, lead=1, out_rows=2 * f, row_offset=f,
                               into=gate_rows, **grad_of)
    reduce_in_chip("ffn1_w_in")
    grad_x, _, d_norm_ffn1 = _rmsnorm_bwd(du1, x, s["norm_ffn1"], dh1, 1.0, "norm_ffn1_bwd")
    traffic.open("gather", "late_grads", _pack([d_norm_ffn1]))
    traffic.alone("to_sibling_ffn1_in")
    reduce_over_chips("ffn1_w_in")
    traffic.alone("to_chips_ffn1")
    partials = {n: traffic.result("to_chips", n) for n in LARGE}
    reduced = (_unpack(_sum_devices(traffic.result("gather", "late_grads"), "sum_late_grads"), [d_norm_ffn1])
               + _unpack(_sum_devices(traffic.result("gather", "small_grads"), "sum_small_grads"), early))
    assert not traffic.transfers, list(traffic.transfers)
    return loss_part[0, 0], grad_x, partials, dict(zip(SMALL_ORDER, reduced))


def _step(x, loss_target, p, m, v):
    me = 4 * lax.axis_index("x") + 2 * lax.axis_index("y") + lax.axis_index("c")

    shards = {n: (_cast_transposed if n in COLUMN_SHARDED else _cast_rows)(p[n], "cast_" + n) for n in LARGE}
    sharded_small = (jnp.pad(p["lru_conv_w"], ((0, SUBLANES - CONV_WIDTH), (0, 0)))
                     + jnp.pad(p["lru_lambda"], ((CONV_WIDTH, SUBLANES - CONV_WIDTH - 2), (0, 0))))
    s = {n: p[n] if n in ("lru_gate_w", "lru_gate_b", "attn_rpb") else p[n].reshape(1, -1) for n in REPLICATED}

    loss_part, grad_x, partials, small = _forward_backward(x, loss_target, shards, sharded_small, s)
    loss = lax.psum(loss_part, ("x", "y", "c"))

    out = {}
    for n in LARGE:
        update = _adamw_cols if n in COLUMN_SHARDED else _adamw_rows
        out[n] = update(p[n], partials[n], m[n], v[n], "adamw_" + n)

    g_small = {n: lax.dynamic_index_in_dim(g, me, axis=0, keepdims=False) if n in SHARDED_SMALL else g
               for n, g in small.items()}
    names = SMALL_ORDER
    like = [p[n] for n in names]
    pack_of = lambda d: _pack([d[n].reshape(p[n].shape) for n in names])
    upd = _adamw_small(pack_of(p), pack_of(g_small), pack_of(m), pack_of(v), "adamw_small")
    for n, d_, m_, v_ in zip(names, *[_unpack(u, like) for u in upd]):
        out[n] = (g_small[n].reshape(p[n].shape), d_, m_, v_)
    return loss, grad_x, out


def kernel(x, norm_ffn1, ffn1_w_in, ffn1_w_out, norm_mix, w_in_mix, lru_conv_w, lru_conv_b, lru_gate_w, lru_gate_b, lru_lambda, attn_rpb, lru_out_norm, attn_out_norm, w_out_mix, norm_ffn2, ffn2_w_in, ffn2_w_out, norm_final, loss_target, m_norm_ffn1, m_ffn1_w_in, m_ffn1_w_out, m_norm_mix, m_w_in_mix, m_lru_conv_w, m_lru_conv_b, m_lru_gate_w, m_lru_gate_b, m_lru_lambda, m_attn_rpb, m_lru_out_norm, m_attn_out_norm, m_w_out_mix, m_norm_ffn2, m_ffn2_w_in, m_ffn2_w_out, m_norm_final, v_norm_ffn1, v_ffn1_w_in, v_ffn1_w_out, v_norm_mix, v_w_in_mix, v_lru_conv_w, v_lru_conv_b, v_lru_gate_w, v_lru_gate_b, v_lru_lambda, v_attn_rpb, v_lru_out_norm, v_attn_out_norm, v_w_out_mix, v_norm_ffn2, v_ffn2_w_in, v_ffn2_w_out, v_norm_final):
    given = dict(locals())
    drop_layer = lambda n, a: a if n == "norm_final" else a[0]
    p = {n: drop_layer(n, given[n]) for n in WEIGHTS}
    m = {n: drop_layer(n, given["m_" + n]) for n in WEIGHTS}
    v = {n: drop_layer(n, given["v_" + n]) for n in WEIGHTS}
    loss, grad_x, out = _step(x[0], loss_target[0], p, m, v)
    shaped = lambda n, a: a.reshape(given[n].shape)
    return (loss, grad_x[None], *[shaped(n, out[n][k]) for k in range(4) for n in WEIGHTS])
```

```python
import math

import numpy as np
import jax
import jax.numpy as jnp
from jax import lax
from jax.experimental import pallas as pl
from jax.experimental.pallas import tpu as pltpu

F32 = jnp.float32
BF16 = jnp.bfloat16
SDS = jax.ShapeDtypeStruct

N_DEV = 8
N_CHIP = 4
NORM_EPS = 1e-6
RG_C = 8.0
CONV_WIDTH = 4
HEAD_DIM = 64
GRID_W = 64
WIN_ROWS = 8
WIN_COLS = 16
NEG = -1e30

ADAM_LR = 0.001
ADAM_B1 = 0.9
ADAM_B2 = 0.999
ADAM_EPS = 1e-08
ADAM_WD = 0.01
ADAM_STEP = 10

LANES = 128
SUBLANES = 8
VMEM_LIMIT = 56 * 1024 * 1024

NT = (((1,), (1,)), ((), ()))
TN = (((0,), (0,)), ((), ()))
ANY = pl.BlockSpec(memory_space=pl.ANY)
WHOLE = pl.BlockSpec(memory_space=pltpu.VMEM)
MESH = pl.DeviceIdType.MESH


def _sigmoid(x):
    return 1.0 / (1.0 + jnp.exp(-x))


def _gelu_parts(x):
    c = math.sqrt(2.0 / math.pi)
    t = jnp.tanh(c * (x + 0.044715 * (x * x * x)))
    gelu = 0.5 * x * (1.0 + t)
    dgelu = 0.5 * (1.0 + t) + 0.5 * x * (1.0 - t * t) * (c * (1.0 + 3.0 * 0.044715 * (x * x)))
    return gelu, dgelu


def _expm1(x):
    poly = x * (1.0 + x * (1.0 / 2) * (1.0 + x * (1.0 / 3) * (1.0 + x * (1.0 / 4) * (1.0 + x * (1.0 / 5) * (1.0 + x * (1.0 / 6))))))
    return jnp.where(jnp.abs(x) < 0.25, poly, jnp.exp(x) - 1.0)


def _softplus(x):
    return jnp.maximum(x, 0.0) + jnp.log1p(jnp.exp(-jnp.abs(x)))


class _Piece:
    N_REMOTE = {"gather": 7, "to_sibling": N_CHIP, "to_chips": 3}
    N_LOCAL = {"gather": 1, "to_sibling": 0, "to_chips": 1}

    def __init__(self, kind, src, dest, lo, hi):
        self.kind, self.src, self.dest, self.lo, self.hi = kind, src, dest, lo, hi


RELAY_AT = 60


class _Job:
    def __init__(self, pieces):
        self.pieces = list(pieces)
        self.ins = [p.src for p in self.pieces]
        self.out_shapes = [SDS(p.dest.shape, p.dest.dtype) for p in self.pieces]
        self.aliased = [i for i, p in enumerate(self.pieces) if not isinstance(p.dest, SDS)]
        self.n_remote = sum(_Piece.N_REMOTE[p.kind] for p in self.pieces)
        self.n_local = max(sum(_Piece.N_LOCAL[p.kind] for p in self.pieces), 1)

    def _each(self, step, ins, outs, send_sems, recv_sems, local_sems):
        remote = local = 0
        for p, src, dst in zip(self.pieces, ins, outs):
            _EXCHANGES[p.kind](step, p, src, dst, send_sems, recv_sems, local_sems, remote, local)
            remote += _Piece.N_REMOTE[p.kind]
            local += _Piece.N_LOCAL[p.kind]

    def start(self, *refs):
        self._each("start", *refs)

    def relay(self, *refs):
        self._each("relay", *refs)

    def finish(self, *refs):
        self._each("finish", *refs)


def _call(body, *, name, args, out_shape, in_specs, out_specs, grid=(), scratch_shapes=(), aliases=None, job=None):
    single = not isinstance(out_shape, (tuple, list))
    out_shape = (out_shape,) if single else tuple(out_shape)
    out_specs = (out_specs,) if single else tuple(out_specs)
    aliases = dict(aliases or {})
    params = pltpu.CompilerParams(dimension_semantics=("arbitrary",) * len(grid) if grid else None,
                                  vmem_limit_bytes=VMEM_LIMIT)
    if job is None:
        res = pl.pallas_call(body, out_shape=out_shape, grid=grid, in_specs=list(in_specs), out_specs=out_specs,
                             scratch_shapes=list(scratch_shapes), input_output_aliases=aliases, name=name,
                             compiler_params=params)(*args)
        return res[0] if single else res

    n_in, n_out, n_scr = len(args), len(out_shape), len(scratch_shapes)
    j_in, j_out, j_alias = len(job.ins), len(job.out_shapes), len(job.aliased)

    def hosted(*refs):
        ins, refs = refs[:n_in], refs[n_in:]
        j_ins, refs = refs[:j_in], refs[j_in + j_alias:]
        outs, refs = refs[:n_out], refs[n_out:]
        j_outs, refs = refs[:j_out], refs[j_out:]
        scr, sems = refs[:n_scr], refs[n_scr:]
        if grid:
            step = 0
            for axis, size in enumerate(grid):
                step = step * size + pl.program_id(axis)
            steps = math.prod(grid)
            pl.when(step == 0)(lambda: job.start(j_ins, j_outs, *sems))
            body(*ins, *outs, *scr)
            pl.when(step == min(RELAY_AT * steps // 100, steps - 1))(lambda: job.relay(j_ins, j_outs, *sems))
            pl.when(step == steps - 1)(lambda: job.finish(j_ins, j_outs, *sems))
        else:
            job.start(j_ins, j_outs, *sems)
            body(*ins, *outs, *scr)
            job.relay(j_ins, j_outs, *sems)
            job.finish(j_ins, j_outs, *sems)

    res = pl.pallas_call(
        hosted, out_shape=out_shape + tuple(job.out_shapes), grid=grid,
        in_specs=list(in_specs) + [ANY] * (j_in + j_alias), out_specs=out_specs + (ANY,) * j_out,
        scratch_shapes=list(scratch_shapes) + [pltpu.SemaphoreType.DMA((job.n_remote,)),
                                               pltpu.SemaphoreType.DMA((job.n_remote,)),
                                               pltpu.SemaphoreType.DMA((job.n_local,))],
        input_output_aliases={**aliases, **{n_in + j_in + k: n_out + i for k, i in enumerate(job.aliased)}},
        name=name, compiler_params=params)(*args, *job.ins, *[job.pieces[i].dest for i in job.aliased])
    own, carried = res[:n_out], res[n_out:]
    return (own[0] if single else own), carried


def _run_job(job, name):
    return _call(lambda: None, name=name, args=[], out_shape=(), in_specs=[], out_specs=(), job=job)[1]


def _position():
    return lax.axis_index("x"), lax.axis_index("y"), lax.axis_index("c")


def _flat(px, py, pc):
    return 4 * px + 2 * py + pc


def _gather_exchange(step, p, src, dst, send_sems, recv_sems, local_sems, r0, l0):
    x, y, c = _position()
    me, sibling = (x, y, c), (x, y, 1 - c)
    along_x, along_y, diagonal = (1 - x, y), (x, 1 - y), (1 - x, 1 - y)
    south = c == 0
    passed_on = (jnp.where(south, 1 - x, x), jnp.where(south, y, 1 - y))
    passed_to = (jnp.where(south, x, 1 - x), jnp.where(south, 1 - y, y))
    rb, n_rows = p.src.shape[0], p.hi - p.lo
    mine = src.at[pl.ds(p.lo, n_rows), :]

    def rows(block):
        return dst.at[pl.ds(_flat(*block) * rb + p.lo, n_rows), :]

    def copy(k, block, to, own=False):
        return pltpu.make_async_remote_copy(
            src_ref=mine if own else rows(block), dst_ref=rows(block),
            send_sem=send_sems.at[r0 + k], recv_sem=recv_sems.at[r0 + k], device_id=to, device_id_type=MESH)

    local = pltpu.make_async_copy(mine, rows(me), local_sems.at[l0])
    if step == "start":
        local.start()
        copy(0, me, sibling, own=True).start()
        copy(1, me, (*along_x, c), own=True).start()
        copy(2, me, (*along_y, c), own=True).start()
    elif step == "relay":
        copy(1, (*along_x, c), me).wait_recv()
        copy(2, (*along_y, c), me).wait_recv()
        copy(3, (*passed_on, c), (*passed_to, c)).start()
        copy(4, (*along_x, c), sibling).start()
        copy(5, (*along_y, c), sibling).start()
    else:
        copy(3, (*diagonal, c), me).wait_recv()
        copy(6, (*diagonal, c), sibling).start()
        copy(0, sibling, me).wait_recv()
        copy(4, (*along_x, 1 - c), me).wait_recv()
        copy(5, (*along_y, 1 - c), me).wait_recv()
        copy(6, (*diagonal, 1 - c), me).wait_recv()
        copy(0, me, sibling, own=True).wait_send()
        copy(1, me, (*along_x, c), own=True).wait_send()
        copy(2, me, (*along_y, c), own=True).wait_send()
        copy(3, (*passed_on, c), (*passed_to, c)).wait_send()
        copy(4, (*along_x, c), sibling).wait_send()
        copy(5, (*along_y, c), sibling).wait_send()
        copy(6, (*diagonal, c), sibling).wait_send()
        local.wait()


def _sibling_exchange(step, p, src, dst, send_sems, recv_sems, local_sems, r0, l0):
    x, y, c = _position()
    rb, n_rows = p.src.shape[0] // N_DEV, p.hi - p.lo
    for q in range(N_CHIP):
        copy = pltpu.make_async_remote_copy(
            src_ref=src.at[pl.ds((2 * q + 1 - c) * rb + p.lo, n_rows), :],
            dst_ref=dst.at[pl.ds(q * rb + p.lo, n_rows), :],
            send_sem=send_sems.at[r0 + q], recv_sem=recv_sems.at[r0 + q], device_id=(x, y, 1 - c), device_id_type=MESH)
        if step == "start":
            copy.start()
        elif step == "finish":
            copy.wait()


CHIP_FLIPS = [(1, 0), (0, 1), (1, 1)]


def _chips_exchange(step, p, src, dst, send_sems, recv_sems, local_sems, r0, l0):
    x, y, c = _position()
    rb, n_rows = p.src.shape[0] // N_CHIP, p.hi - p.lo

    def slot(ref, px, py):
        return ref.at[pl.ds((2 * px + py) * rb + p.lo, n_rows), :]

    def copy(k, landing=False):
        px = 1 - x if CHIP_FLIPS[k][0] else x
        py = 1 - y if CHIP_FLIPS[k][1] else y
        return pltpu.make_async_remote_copy(
            src_ref=slot(dst, px, py) if landing else slot(src, px, py),
            dst_ref=slot(dst, px, py) if landing else slot(dst, x, y),
            send_sem=send_sems.at[r0 + k], recv_sem=recv_sems.at[r0 + k], device_id=(px, py, c), device_id_type=MESH)

    local = pltpu.make_async_copy(slot(src, x, y), slot(dst, x, y), local_sems.at[l0])
    if step == "start":
        local.start()
        for k in range(3):
            copy(k).start()
    elif step == "finish":
        for k in range(3):
            copy(k, landing=True).wait_recv()
        for k in range(3):
            copy(k).wait_send()
        local.wait()


_EXCHANGES = {"gather": _gather_exchange, "to_sibling": _sibling_exchange, "to_chips": _chips_exchange}


def _gathered(shard):
    return SDS((N_DEV * shard.shape[0], shard.shape[1]), shard.dtype)


def _split(rows, parts):
    cuts = [rows * k // parts // 16 * 16 for k in range(parts)] + [rows]
    return list(zip(cuts[:-1], cuts[1:]))


def _pair_sum(g, from_sibling, name):
    rb, n = g.shape[0] // N_DEV, g.shape[1]
    tr = rb if rb * n * 2 <= 3 * 1024 * 1024 else rb // 2
    core = lax.axis_index("c").astype(jnp.int32).reshape(1)

    def body(c_ref, g_ref, r_ref, o_ref):
        o_ref[...] = (g_ref[...].astype(F32) + r_ref[...].astype(F32)).astype(BF16)

    grid_spec = pltpu.PrefetchScalarGridSpec(
        num_scalar_prefetch=1, grid=(N_CHIP, rb // tr),
        in_specs=[pl.BlockSpec((None, None, tr, n), lambda q, i, c_ref: (q, c_ref[0], i, 0)),
                  pl.BlockSpec((None, tr, n), lambda q, i, c_ref: (q, i, 0))],
        out_specs=pl.BlockSpec((None, tr, n), lambda q, i, c_ref: (q, i, 0)))
    out = pl.pallas_call(
        body, grid_spec=grid_spec, out_shape=SDS((N_CHIP, rb, n), BF16), name=name,
        compiler_params=pltpu.CompilerParams(dimension_semantics=("arbitrary",) * 2, vmem_limit_bytes=VMEM_LIMIT))(
            core, g.reshape(N_CHIP, 2, rb, n), from_sibling.reshape(N_CHIP, rb, n))
    return out.reshape(N_CHIP * rb, n)


def _sum_devices(gathered, name):
    r = gathered.shape[0] // N_DEV

    def body(g_ref, o_ref):
        acc = g_ref[0]
        for s in range(1, N_DEV):
            acc = acc + g_ref[s]
        o_ref[...] = acc

    return _call(body, name=name, args=[gathered.reshape(N_DEV, r, LANES)], out_shape=SDS((r, LANES), F32),
                 in_specs=[WHOLE], out_specs=WHOLE)


def _cast_rows(w, name):
    def body(w_ref, o_ref):
        o_ref[...] = w_ref[...].astype(BF16)

    return _call(body, name=name, args=[w], out_shape=SDS(w.shape, BF16), in_specs=[WHOLE], out_specs=WHOLE)


def _cast_transposed(w, name):
    d, n = w.shape
    td = 512

    def body(w_ref, o_ref):
        o_ref[...] = w_ref[...].T.astype(BF16)

    return _call(body, name=name, args=[w], out_shape=SDS((n, d), BF16), grid=(d // td,),
                 in_specs=[pl.BlockSpec((td, n), lambda i: (i, 0))], out_specs=pl.BlockSpec((n, td), lambda i: (0, i)))


ROW_TILE = 256


def _rmsnorm_fwd(h, gain, name):
    t, d = h.shape

    def body(h_ref, g_ref, u_ref):
        x = h_ref[...]
        u_ref[...] = (x * lax.rsqrt(jnp.mean(x * x, axis=-1, keepdims=True) + NORM_EPS) * g_ref[...]).astype(BF16)

    row = pl.BlockSpec((ROW_TILE, d), lambda i: (i, 0))
    return _call(body, name=name, args=[h, gain], out_shape=SDS((t, d), BF16), grid=(t // ROW_TILE,),
                 in_specs=[row, pl.BlockSpec((1, d), lambda i: (0, 0))], out_specs=row)


def _rms_bwd_math(x, gain, dy):
    rstd = lax.rsqrt(jnp.mean(x * x, axis=-1, keepdims=True) + NORM_EPS)
    xhat = x * rstd
    dxh = dy * gain
    dx = rstd * (dxh - xhat * jnp.mean(dxh * xhat, axis=-1, keepdims=True))
    return dx, jnp.sum(dy * xhat, axis=0, keepdims=True)


def _rmsnorm_bwd(du, h, gain, resid, bf_scale, name, job=None):
    t, d = h.shape

    def body(du_ref, h_ref, g_ref, r_ref, dh_ref, dhb_ref, dg_ref):
        @pl.when(pl.program_id(0) == 0)
        def _():
            dg_ref[...] = jnp.zeros_like(dg_ref)

        dx, dg = _rms_bwd_math(h_ref[...], g_ref[...], du_ref[...])
        dh = r_ref[...] + dx
        dh_ref[...] = dh
        dhb_ref[...] = (bf_scale * dh).astype(BF16)
        dg_ref[...] += dg

    row = pl.BlockSpec((ROW_TILE, d), lambda i: (i, 0))
    vec = pl.BlockSpec((1, d), lambda i: (0, 0))
    return _call(body, name=name, args=[du, h, gain, resid],
                 out_shape=(SDS((t, d), F32), SDS((t, d), BF16), SDS((1, d), F32)), grid=(t // ROW_TILE,),
                 in_specs=[row, row, vec, row], out_specs=(row, row, vec), job=job)


def _final_loss(h, gain, target, name):
    t, d = h.shape

    def body(h_ref, g_ref, t_ref, dh_ref, dhb_ref, loss_ref, dg_ref):
        @pl.when(pl.program_id(0) == 0)
        def _():
            dg_ref[...] = jnp.zeros_like(dg_ref)
            loss_ref[...] = jnp.zeros_like(loss_ref)

        x = h_ref[...]
        gain = g_ref[...]
        out = x * lax.rsqrt(jnp.mean(x * x, axis=-1, keepdims=True) + NORM_EPS) * gain
        err = out - t_ref[...]
        loss_ref[...] += 0.5 * jnp.sum(jnp.mean(err * err, axis=-1, keepdims=True), axis=0, keepdims=True)
        dx, dg = _rms_bwd_math(x, gain, err * (1.0 / d))
        dh_ref[...] = dx
        dhb_ref[...] = (0.5 * dx).astype(BF16)
        dg_ref[...] += dg

    row = pl.BlockSpec((ROW_TILE, d), lambda i: (i, 0))
    vec = pl.BlockSpec((1, d), lambda i: (0, 0))
    one = pl.BlockSpec((SUBLANES, LANES), lambda i: (0, 0))
    return _call(body, name=name, args=[h, gain, target],
                 out_shape=(SDS((t, d), F32), SDS((t, d), BF16), SDS((SUBLANES, LANES), F32), SDS((1, d), F32)),
                 grid=(t // ROW_TILE,), in_specs=[row, vec, row], out_specs=(row, row, one, vec))


def _mixnorm_fwd(ya, yb, ga, gb, name):
    t, c = ya.shape

    def body(ya_ref, yb_ref, ga_ref, gb_ref, y_ref, yt_ref):
        for k, (src, g_ref) in enumerate(((ya_ref, ga_ref), (yb_ref, gb_ref))):
            x = src[...]
            u = x * lax.rsqrt(jnp.mean(x * x, axis=-1, keepdims=True) + NORM_EPS) * g_ref[...]
            y_ref[:, k * c:(k + 1) * c] = u.astype(BF16)
            yt_ref[k * c:(k + 1) * c, :] = u.T.astype(BF16)

    row = pl.BlockSpec((ROW_TILE, c), lambda i: (i, 0))
    vec = pl.BlockSpec((1, c), lambda i: (0, 0))
    return _call(body, name=name, args=[ya, yb, ga, gb],
                 out_shape=(SDS((t, 2 * c), BF16), SDS((2 * c, t), BF16)), grid=(t // ROW_TILE,),
                 in_specs=[row, row, vec, vec],
                 out_specs=(pl.BlockSpec((ROW_TILE, 2 * c), lambda i: (i, 0)),
                            pl.BlockSpec((2 * c, ROW_TILE), lambda i: (0, i))))


def _mixnorm_bwd(dy, ya, yb, ga, gb, name):
    t, c = ya.shape

    def body(dy_ref, ya_ref, yb_ref, ga_ref, gb_ref, dya_ref, dyb_ref, dga_ref, dgb_ref):
        @pl.when(pl.program_id(0) == 0)
        def _():
            dga_ref[...] = jnp.zeros_like(dga_ref)
            dgb_ref[...] = jnp.zeros_like(dgb_ref)

        dxa, dga = _rms_bwd_math(ya_ref[...], ga_ref[...], dy_ref[:, :c])
        dxb, dgb = _rms_bwd_math(yb_ref[...], gb_ref[...], dy_ref[:, c:])
        dya_ref[...] = dxa
        dyb_ref[...] = dxb
        dga_ref[...] += dga
        dgb_ref[...] += dgb

    row = pl.BlockSpec((ROW_TILE, c), lambda i: (i, 0))
    vec = pl.BlockSpec((1, c), lambda i: (0, 0))
    return _call(body, name=name, args=[dy, ya, yb, ga, gb],
                 out_shape=(SDS((t, c), F32), SDS((t, c), F32), SDS((1, c), F32), SDS((1, c), F32)),
                 grid=(t // ROW_TILE,),
                 in_specs=[pl.BlockSpec((ROW_TILE, 2 * c), lambda i: (i, 0)), row, row, vec, vec],
                 out_specs=(row, row, vec, vec))


def _tile(n, want):
    return max(t for t in range(LANES, min(n, want) + 1, LANES) if n % t == 0)


def _mm(a, b, *, nt, out_dtype, tm, tn, name, residual=None, lead=None, out_rows=None, row_offset=0, into=None,
        job=None):
    m, k = a.shape[-2:]
    n = b.shape[0] if nt else b.shape[1]
    tm, tn = _tile(m, tm), _tile(n, tn)
    out_rows = m if out_rows is None else out_rows

    def body(a_ref, b_ref, *rest):
        o_ref = rest[-1]
        av, bv = a_ref[...].astype(BF16), b_ref[...].astype(BF16)
        if nt:
            out = lax.dot_general(av, bv, NT, preferred_element_type=F32)
        else:
            out = jnp.dot(av, bv, preferred_element_type=F32)
        if residual is not None:
            out = rest[0][...] + out
        o_ref[...] = out.astype(out_dtype)

    a_spec = (pl.BlockSpec((tm, k), lambda i, j: (i, 0)) if lead is None
              else pl.BlockSpec((None, tm, k), lambda i, j: (lead, i, 0)))
    in_specs = [a_spec, pl.BlockSpec((tn, k), lambda i, j: (j, 0)) if nt else pl.BlockSpec((k, tn), lambda i, j: (0, j))]
    args, aliases = [a, b], {}
    if residual is not None:
        in_specs.append(pl.BlockSpec((tm, tn), lambda i, j: (i, j)))
        args.append(residual)
    if into is not None:
        in_specs.append(ANY)
        aliases[len(args)] = 0
        args.append(into)
    return _call(body, name=name, args=args, out_shape=SDS((out_rows, n), out_dtype), grid=(m // tm, n // tn),
                 in_specs=in_specs, out_specs=pl.BlockSpec((tm, tn), lambda i, j: (row_offset // tm + i, j)),
                 aliases=aliases, job=job)


FFN_TM = 512
FFN_HB = 512


def _ffn_fwd(h, u, w_in_t, w_out, name, job=None):
    t, d = h.shape
    f = w_out.shape[0]
    nk = f // FFN_HB

    def body(u_ref, w_ref, wo_ref, h_ref, hn_ref, g_ref, up_ref, acc):
        k = pl.program_id(1)

        @pl.when(k == 0)
        def _():
            acc[...] = jnp.zeros_like(acc)

        uu = u_ref[...]
        g = lax.dot_general(uu, w_ref[0], NT, preferred_element_type=F32)
        up = lax.dot_general(uu, w_ref[1], NT, preferred_element_type=F32)
        g_ref[...] = g
        up_ref[...] = up
        hid = (g * _sigmoid(g)) * up
        acc[...] += jnp.dot(hid.astype(BF16), wo_ref[...], preferred_element_type=F32)

        @pl.when(k == nk - 1)
        def _():
            hn_ref[...] = h_ref[...] + 0.5 * acc[...]

    tok = pl.BlockSpec((FFN_TM, d), lambda i, k: (i, 0))
    pre = pl.BlockSpec((FFN_TM, FFN_HB), lambda i, k: (i, k))
    return _call(body, name=name, args=[u, w_in_t.reshape(2, f, d), w_out, h],
                 out_shape=(SDS((t, d), F32), SDS((t, f), F32), SDS((t, f), F32)), grid=(t // FFN_TM, nk),
                 in_specs=[tok, pl.BlockSpec((2, FFN_HB, d), lambda i, k: (0, k, 0)),
                           pl.BlockSpec((FFN_HB, d), lambda i, k: (k, 0)), tok],
                 out_specs=(tok, pre, pre), scratch_shapes=[pltpu.VMEM((FFN_TM, d), F32)], job=job)


def _ffn_bwd(dfb, gpre, upre, w_in_t, w_out, name, job=None):
    t, d = dfb.shape
    f = w_out.shape[0]
    nk = f // FFN_HB

    def body(df_ref, g_ref, up_ref, w_ref, wo_ref, du_ref, hid_t_ref, da_t_ref, acc):
        k = pl.program_id(1)

        @pl.when(k == 0)
        def _():
            acc[...] = jnp.zeros_like(acc)

        dhid = lax.dot_general(df_ref[...], wo_ref[...], NT, preferred_element_type=F32)
        g, up = g_ref[...], up_ref[...]
        sig = _sigmoid(g)
        silu = g * sig
        dup = dhid * silu
        dg = dhid * up * (sig * (1.0 + g * (1.0 - sig)))
        hid_t_ref[...] = (silu * up).T.astype(BF16)
        da_t_ref[0] = dg.T.astype(BF16)
        da_t_ref[1] = dup.T.astype(BF16)
        acc[...] += (jnp.dot(dg.astype(BF16), w_ref[0], preferred_element_type=F32)
                     + jnp.dot(dup.astype(BF16), w_ref[1], preferred_element_type=F32))

        @pl.when(k == nk - 1)
        def _():
            du_ref[...] = acc[...]

    tok = pl.BlockSpec((FFN_TM, d), lambda i, k: (i, 0))
    pre = pl.BlockSpec((FFN_TM, FFN_HB), lambda i, k: (i, k))
    return _call(body, name=name, args=[dfb, gpre, upre, w_in_t.reshape(2, f, d), w_out],
                 out_shape=(SDS((t, d), F32), SDS((f, t), BF16), SDS((2, f, t), BF16)), grid=(t // FFN_TM, nk),
                 in_specs=[tok, pre, pre, pl.BlockSpec((2, FFN_HB, d), lambda i, k: (0, k, 0)),
                           pl.BlockSpec((FFN_HB, d), lambda i, k: (k, 0))],
                 out_specs=(tok, pl.BlockSpec((FFN_HB, FFN_TM), lambda i, k: (k, i)),
                            pl.BlockSpec((2, FFN_HB, FFN_TM), lambda i, k: (0, k, i))),
                 scratch_shapes=[pltpu.VMEM((FFN_TM, d), F32)], job=job)


CH = LANES
PAD = SUBLANES


def _lru_gates(xc, gw_ref, gb_ref, lam_ref, z):
    xcb = xc.astype(BF16)
    r = _sigmoid(jnp.dot(xcb, gw_ref[2 * z], preferred_element_type=F32) + gb_ref[pl.ds(2 * z, 1), :])
    i = _sigmoid(jnp.dot(xcb, gw_ref[2 * z + 1], preferred_element_type=F32) + gb_ref[pl.ds(2 * z + 1, 1), :])
    sp = _softplus(-lam_ref[pl.ds(z, 1), :])
    log_a = (-RG_C * r) * sp
    a = jnp.exp(log_a)
    mult = jnp.sqrt(-_expm1(2.0 * log_a))
    return r, i, sp, a, mult


def _conv(xpad, cw_ref, cb_ref, t):
    xc = cb_ref[...] + cw_ref[pl.ds(0, 1), :] * xpad[pl.ds(PAD - 2, t), :]
    for j in range(1, CONV_WIDTH):
        xc = xc + cw_ref[pl.ds(j, 1), :] * xpad[pl.ds(PAD - 2 + j, t), :]
    return xc


def _fill_padded(pad_ref, value, t):
    pad_ref[pl.ds(0, PAD), :] = jnp.zeros((PAD, CH), F32)
    pad_ref[pl.ds(PAD + t, PAD), :] = jnp.zeros((PAD, CH), F32)
    pad_ref[pl.ds(PAD, t), :] = value


def _scan_pair(t, a_up, b_up, out_up, a_down, b_down, out_down):
    def step(tt, carry):
        hu, hd = carry
        lo = pl.multiple_of(tt * SUBLANES, SUBLANES)
        hi = pl.multiple_of(t - SUBLANES - tt * SUBLANES, SUBLANES)
        for j in range(SUBLANES):
            su, sd = pl.ds(lo + j, 1), pl.ds(hi + SUBLANES - 1 - j, 1)
            hu = a_up(su) * hu + b_up(su)
            out_up[su, :] = hu
            hd = a_down(sd) * hd + b_down(sd)
            out_down[sd, :] = hd
        return hu, hd

    zero = jnp.zeros((1, CH), F32)
    lax.fori_loop(0, t // SUBLANES, step, (zero, zero))


def _lru_fwd(proj, cw, cb, gw, gb, lam, name, job=None):
    t = proj.shape[0]
    c = cw.shape[1]
    ncb = c // CH

    def body(x_ref, g_ref, cw_ref, cb_ref, gw_ref, gb_ref, lam_ref, ya_ref, hf_ref, hb_ref, xpad, a0, b0, a1, b1):
        _fill_padded(xpad, x_ref[...], t)
        xc = _conv(xpad, cw_ref, cb_ref, t)
        for z, (a_s, b_s) in enumerate(((a0, b0), (a1, b1))):
            _, i, _, a, mult = _lru_gates(xc, gw_ref, gb_ref, lam_ref, z)
            a_s[...] = a
            b_s[...] = mult * (i * xc)
        _scan_pair(t, lambda s: a0[s, :], lambda s: b0[s, :], hf_ref, lambda s: a1[s, :], lambda s: b1[s, :], hb_ref)
        gelu, _ = _gelu_parts(g_ref[...])
        ya_ref[...] = gelu * (hf_ref[...] + hb_ref[...])

    col = lambda off: pl.BlockSpec((t, CH), lambda i: (0, off + i))
    small = lambda rows: pl.BlockSpec((rows, CH), lambda i: (0, i))
    return _call(body, name=name, args=[proj, proj, cw, cb, gw, gb, lam], out_shape=(SDS((t, c), F32),) * 3,
                 grid=(ncb,),
                 in_specs=[col(0), col(ncb), small(CONV_WIDTH), small(1),
                           pl.BlockSpec((4, None, CH, CH), lambda i: (0, i, 0, 0)), small(4), small(2)],
                 out_specs=(col(0),) * 3,
                 scratch_shapes=[pltpu.VMEM((t + 2 * PAD, CH), F32)] + [pltpu.VMEM((t, CH), F32)] * 4, job=job)


def _lru_bwd(proj, cw, cb, gw, gb, lam, hf, hb, dya, name, job=None):
    t = proj.shape[0]
    c = cw.shape[1]
    ncb = c // CH

    def body(x_ref, g_ref, cw_ref, cb_ref, gw_ref, gb_ref, lam_ref, hf_ref, hb_ref, dya_ref,
             dx_ref, dg_ref, dt_ref, dcw_ref, dcb_ref, dgw_ref, dgb_ref, dlam_ref,
             xpad, hpad, dxc, a0, a1, dhs, dh0, dh1):
        _fill_padded(xpad, x_ref[...], t)
        xc = _conv(xpad, cw_ref, cb_ref, t)
        xcb = xc.astype(BF16)
        gates = [_lru_gates(xc, gw_ref, gb_ref, lam_ref, z) for z in range(2)]
        a0[...] = gates[0][3]
        a1[...] = gates[1][3]

        gelu, dgelu = _gelu_parts(g_ref[...])
        dya = dya_ref[...]
        dgate = dya * (hf_ref[...] + hb_ref[...]) * dgelu
        dg_ref[...] = dgate.astype(BF16)
        dt_ref[1] = dgate.T.astype(BF16)
        dhs[...] = dya * gelu

        def step(tt, carry):
            c0, p0, c1, p1 = carry
            lo = pl.multiple_of(tt * SUBLANES, SUBLANES)
            hi = pl.multiple_of(t - SUBLANES - tt * SUBLANES, SUBLANES)
            for j in range(SUBLANES):
                su, sd = pl.ds(lo + j, 1), pl.ds(hi + SUBLANES - 1 - j, 1)
                c0 = dhs[sd, :] + p0 * c0
                dh0[sd, :] = c0
                p0 = a0[sd, :]
                c1 = dhs[su, :] + p1 * c1
                dh1[su, :] = c1
                p1 = a1[su, :]
            return c0, p0, c1, p1

        zero = jnp.zeros((1, CH), F32)
        lax.fori_loop(0, t // SUBLANES, step, (zero, zero, zero, zero))

        acc_dxc = jnp.zeros((t, CH), F32)
        for z, (h_ref, dh_ref, shift) in enumerate(((hf_ref, dh0, -1), (hb_ref, dh1, 1))):
            r, i, sp, a, mult = gates[z]
            _fill_padded(hpad, h_ref[...], t)
            h_nb = hpad[pl.ds(PAD + shift, t), :]
            db = dh_ref[...]
            da = db * h_nb
            d_i = db * mult * xc
            acc_dxc = acc_dxc + db * mult * i
            d_mult = db * i * xc
            d_la = da * a - d_mult * (a * a) / mult
            d_r = d_la * (-RG_C * sp)
            dlam_ref[pl.ds(z, 1), :] = (jnp.sum(d_la * (-RG_C * r), axis=0, keepdims=True)
                                        * (-_sigmoid(-lam_ref[pl.ds(z, 1), :])))
            for gate, d_pre in ((0, d_r * r * (1.0 - r)), (1, d_i * i * (1.0 - i))):
                zg = 2 * z + gate
                dgb_ref[pl.ds(zg, 1), :] = jnp.sum(d_pre, axis=0, keepdims=True)
                d_pre_b = d_pre.astype(BF16)
                dgw_ref[zg] = lax.dot_general(xcb, d_pre_b, TN, preferred_element_type=F32)
                acc_dxc = acc_dxc + lax.dot_general(d_pre_b, gw_ref[zg], NT, preferred_element_type=F32)

        dcb_ref[...] = jnp.sum(acc_dxc, axis=0, keepdims=True)
        for j in range(CONV_WIDTH):
            dcw_ref[pl.ds(j, 1), :] = jnp.sum(acc_dxc * xpad[pl.ds(PAD - 2 + j, t), :], axis=0, keepdims=True)
        _fill_padded(dxc, acc_dxc, t)
        dx = cw_ref[pl.ds(0, 1), :] * dxc[pl.ds(PAD + 2, t), :]
        for j in range(1, CONV_WIDTH):
            dx = dx + cw_ref[pl.ds(j, 1), :] * dxc[pl.ds(PAD + 2 - j, t), :]
        dx_ref[...] = dx.astype(BF16)
        dt_ref[0] = dx.T.astype(BF16)

    col = lambda off: pl.BlockSpec((t, CH), lambda i: (0, off + i))
    small = lambda rows: pl.BlockSpec((rows, CH), lambda i: (0, i))
    dense = pl.BlockSpec((4, None, CH, CH), lambda i: (0, i, 0, 0))
    padded = pltpu.VMEM((t + 2 * PAD, CH), F32)
    return _call(
        body, name=name, args=[proj, proj, cw, cb, gw, gb, lam, hf, hb, dya],
        out_shape=(SDS((t, c), BF16), SDS((t, c), BF16), SDS((2, c, t), BF16), SDS((CONV_WIDTH, c), F32),
                   SDS((1, c), F32), SDS((4, ncb, CH, CH), F32), SDS((4, c), F32), SDS((2, c), F32)),
        grid=(ncb,),
        in_specs=[col(0), col(ncb), small(CONV_WIDTH), small(1), dense, small(4), small(2), col(0), col(0), col(0)],
        out_specs=(col(0), col(0), pl.BlockSpec((2, CH, t), lambda i: (0, i, 0)), small(CONV_WIDTH), small(1),
                   dense, small(4), small(2)),
        scratch_shapes=[padded, padded, padded] + [pltpu.VMEM((t, CH), F32)] * 5, job=job)


Q_ROWS = 4
BAND_ROWS = WIN_ROWS + Q_ROWS
BAND_PAIRS = BAND_ROWS // 2
Q_BLOCK = Q_ROWS * GRID_W
BAND = BAND_ROWS * GRID_W
PAIR_W = 2 * GRID_W
N_BOTH = 2 * WIN_ROWS - 2
ENTRY_LEFT_OUT, ENTRY_RIGHT_OUT, ENTRY_OUT = N_BOTH, N_BOTH + 1, N_BOTH + 2
N_ENTRIES = N_BOTH + 3


def _bias_tables(rpb):
    cols = np.arange(GRID_W)
    start = np.clip(cols - WIN_COLS // 2, 0, GRID_W - WIN_COLS)
    valid = (cols[None, :] >= start[:, None]) & (cols[None, :] < start[:, None] + WIN_COLS)
    col_off = np.clip(cols[None, :] - cols[:, None] + WIN_COLS - 1, 0, 2 * WIN_COLS - 2)
    pick_col = jnp.asarray(np.eye(2 * WIN_COLS - 1, dtype=np.float32)[col_off] * valid[..., None])
    by_row = jnp.einsum("hrc,qkc->hrqk", rpb, pick_col, precision=lax.Precision.HIGHEST)
    by_row = jnp.where(jnp.asarray(valid)[None, None], by_row, NEG)
    out = jnp.full_like(by_row[:, :1], NEG)
    first_in, last_in = WIN_ROWS - 1 - WIN_ROWS // 2, 2 * (WIN_ROWS - 1) - WIN_ROWS // 2
    both = jnp.concatenate([by_row[:, :-1], by_row[:, 1:]], axis=-1)
    left_out = jnp.concatenate([out, by_row[:, first_in:first_in + 1]], axis=-1)
    right_out = jnp.concatenate([by_row[:, last_in:last_in + 1], out], axis=-1)
    return jnp.concatenate([both, left_out, right_out, jnp.concatenate([out, out], axis=-1)], axis=1)


def _band_start(m, rows):
    return jnp.clip(Q_ROWS * m - WIN_ROWS // 2, 0, rows - BAND_ROWS)


def _entry(r, key_row, rows):
    w0 = jnp.clip(r - WIN_ROWS // 2, 0, rows - WIN_ROWS)
    left = (key_row >= w0) & (key_row < w0 + WIN_ROWS)
    right = (key_row + 1 >= w0) & (key_row + 1 < w0 + WIN_ROWS)
    return jnp.where(left & right, key_row - r + WIN_ROWS - 1,
                     jnp.where(right, ENTRY_LEFT_OUT, jnp.where(left, ENTRY_RIGHT_OUT, ENTRY_OUT)))


def _transposed_pairs(dst, src_ref):
    for g in range(dst.shape[0]):
        dst[g] = src_ref[pl.ds(g * PAIR_W, PAIR_W), :].T.astype(BF16)


def _band_of(pairs_ref, first_pair, hh):
    heads = pl.ds(hh * HEAD_DIM, HEAD_DIM)
    return jnp.concatenate([pairs_ref[first_pair + g, heads, :] for g in range(BAND_PAIRS)], axis=1)


def _attn_block(qs, kt, tz_ref, hh, m, rows):
    rs = _band_start(m, rows)
    lanes = pl.ds(hh * HEAD_DIM, HEAD_DIM)
    qrows = pl.ds(pl.multiple_of(m * Q_BLOCK, Q_BLOCK), Q_BLOCK)
    band = pl.ds(pl.multiple_of(rs * GRID_W, PAIR_W), BAND)
    entries = [[_entry(Q_ROWS * m + i, rs + 2 * g, rows) for g in range(BAND_PAIRS)] for i in range(Q_ROWS)]
    bias = jnp.concatenate([jnp.concatenate([tz_ref[hh, e] for e in row], axis=1) for row in entries], axis=0)
    q = qs[qrows, lanes]
    s = jnp.dot(q, _band_of(kt, rs // 2, hh), preferred_element_type=F32) * (HEAD_DIM ** -0.5) + bias
    p = jnp.exp(s - jnp.max(s, axis=-1, keepdims=True))
    p = p / jnp.sum(p, axis=-1, keepdims=True)
    return q, p, qrows, band, lanes, entries, rs // 2


def _attn_fwd(proj, tables, width, name, job=None):
    t = proj.shape[0]
    rows = t // GRID_W
    npair = width // LANES
    first = (proj.shape[1] - 3 * width) // LANES

    def body(q_ref, k_ref, v_ref, tz_ref, o_ref, qs, vs, kt):
        qs[...] = q_ref[...].astype(BF16)
        vs[...] = v_ref[...].astype(BF16)
        _transposed_pairs(kt, k_ref)

        def block(m, carry):
            for hh in range(2):
                _, p, qrows, band, lanes, _, _ = _attn_block(qs, kt, tz_ref, hh, m, rows)
                o_ref[qrows, lanes] = jnp.dot(p.astype(BF16), vs[band, lanes], preferred_element_type=F32)
            return carry

        lax.fori_loop(0, rows // Q_ROWS, block, 0)

    col = lambda off: pl.BlockSpec((t, LANES), lambda i: (0, off + i))
    return _call(body, name=name, args=[proj, proj, proj, tables], out_shape=SDS((t, width), F32), grid=(npair,),
                 in_specs=[col(first), col(first + npair), col(first + 2 * npair),
                           pl.BlockSpec((2, N_ENTRIES, GRID_W, PAIR_W), lambda i: (i, 0, 0, 0))],
                 out_specs=col(0),
                 scratch_shapes=[pltpu.VMEM((t, LANES), BF16)] * 2 + [pltpu.VMEM((t // PAIR_W, LANES, PAIR_W), BF16)],
                 job=job)


def _attn_bwd(proj, tables, dyb, name, job=None):
    t, width = dyb.shape
    rows = t // GRID_W
    npair = width // LANES
    first = (proj.shape[1] - 3 * width) // LANES

    def body(q_ref, k_ref, v_ref, tz_ref, do_ref, dq_ref, dk_ref, dv_ref, dt_ref, dtz_ref, dq_s, dk_s, dv_s,
             qs, ks, vs, dos, kt, vt):
        qs[...] = q_ref[...].astype(BF16)
        ks[...] = k_ref[...].astype(BF16)
        vs[...] = v_ref[...].astype(BF16)
        dos[...] = do_ref[...].astype(BF16)
        _transposed_pairs(kt, k_ref)
        _transposed_pairs(vt, v_ref)
        dk_s[...] = jnp.zeros_like(dk_s)
        dv_s[...] = jnp.zeros_like(dv_s)
        dtz_ref[...] = jnp.zeros_like(dtz_ref)

        def block(m, carry):
            for hh in range(2):
                q, p, qrows, band, lanes, entries, first_pair = _attn_block(qs, kt, tz_ref, hh, m, rows)
                do = dos[qrows, lanes]
                dp = jnp.dot(do, _band_of(vt, first_pair, hh), preferred_element_type=F32)
                ds = p * (dp - jnp.sum(dp * p, axis=-1, keepdims=True))
                for i, row in enumerate(entries):
                    for g, e in enumerate(row):
                        dtz_ref[hh, e] += ds[i * GRID_W:(i + 1) * GRID_W, g * PAIR_W:(g + 1) * PAIR_W]
                dsb = (ds * (HEAD_DIM ** -0.5)).astype(BF16)
                dq_s[qrows, lanes] = jnp.dot(dsb, ks[band, lanes], preferred_element_type=F32)
                dk_s[band, lanes] += lax.dot_general(dsb, q, TN, preferred_element_type=F32)
                dv_s[band, lanes] += lax.dot_general(p.astype(BF16), do, TN, preferred_element_type=F32)
            return carry

        lax.fori_loop(0, rows // Q_ROWS, block, 0)
        for n, (src, dst) in enumerate(((dq_s, dq_ref), (dk_s, dk_ref), (dv_s, dv_ref))):
            val = src[...]
            dst[...] = val.astype(BF16)
            dt_ref[n] = val.T.astype(BF16)

    col = lambda off: pl.BlockSpec((t, LANES), lambda i: (0, off + i))
    table = pl.BlockSpec((2, N_ENTRIES, GRID_W, PAIR_W), lambda i: (i, 0, 0, 0))
    pairs = pltpu.VMEM((t // PAIR_W, LANES, PAIR_W), BF16)
    return _call(body, name=name, args=[proj, proj, proj, tables, dyb],
                 out_shape=(SDS((t, width), BF16),) * 3 + (SDS((3, width, t), BF16), SDS(tables.shape, F32)),
                 grid=(npair,),
                 in_specs=[col(first), col(first + npair), col(first + 2 * npair), table, col(0)],
                 out_specs=(col(0), col(0), col(0), pl.BlockSpec((3, LANES, t), lambda i: (0, i, 0)), table),
                 scratch_shapes=[pltpu.VMEM((t, LANES), F32)] * 3 + [pltpu.VMEM((t, LANES), BF16)] * 4 + [pairs, pairs],
                 job=job)


def _adamw_math(w, g, m, v):
    m = ADAM_B1 * m + (1.0 - ADAM_B1) * g
    v = ADAM_B2 * v + (1.0 - ADAM_B2) * (g * g)
    m_hat = m / (1.0 - ADAM_B1 ** ADAM_STEP)
    v_hat = v / (1.0 - ADAM_B2 ** ADAM_STEP)
    delta = -ADAM_LR * (m_hat / (jnp.sqrt(v_hat) + ADAM_EPS) + ADAM_WD * w)
    return delta, m, v


def _sum_partials(p_ref):
    g = p_ref[0].astype(F32)
    for s in range(1, N_CHIP):
        g = g + p_ref[s].astype(F32)
    return g


def _adamw_rows(w, partials, m, v, name):
    rb, n = w.shape
    tr = 64

    def body(w_ref, p_ref, m_ref, v_ref, g_ref, d_ref, nm_ref, nv_ref):
        g = _sum_partials(p_ref)
        g_ref[...] = g
        d_ref[...], nm_ref[...], nv_ref[...] = _adamw_math(w_ref[...], g, m_ref[...], v_ref[...])

    blk = pl.BlockSpec((tr, n), lambda i: (i, 0))
    return _call(body, name=name, args=[w, partials.reshape(N_CHIP, rb, n), m, v], out_shape=(SDS((rb, n), F32),) * 4,
                 grid=(rb // tr,), in_specs=[blk, pl.BlockSpec((N_CHIP, tr, n), lambda i: (0, i, 0)), blk, blk],
                 out_specs=(blk,) * 4)


def _adamw_cols(w, partials, m, v, name):
    d, nb = w.shape
    td = 256

    def body(w_ref, p_ref, m_ref, v_ref, g_ref, d_ref, nm_ref, nv_ref):
        g = _sum_partials(p_ref).T
        g_ref[...] = g
        d_ref[...], nm_ref[...], nv_ref[...] = _adamw_math(w_ref[...], g, m_ref[...], v_ref[...])

    blk = pl.BlockSpec((td, nb), lambda i: (i, 0))
    return _call(body, name=name, args=[w, partials.reshape(N_CHIP, nb, d), m, v], out_shape=(SDS((d, nb), F32),) * 4,
                 grid=(d // td,), in_specs=[blk, pl.BlockSpec((N_CHIP, nb, td), lambda i: (0, 0, i)), blk, blk],
                 out_specs=(blk,) * 4)


def _adamw_small(w, g, m, v, name):
    def body(w_ref, g_ref, m_ref, v_ref, d_ref, nm_ref, nv_ref):
        d_ref[...], nm_ref[...], nv_ref[...] = _adamw_math(w_ref[...], g_ref[...], m_ref[...], v_ref[...])

    return _call(body, name=name, args=[w, g, m, v], out_shape=(SDS(w.shape, F32),) * 3, in_specs=[WHOLE] * 4,
                 out_specs=(WHOLE,) * 3)


TILE = SUBLANES * LANES


def _pack(arrays):
    parts = []
    for a in arrays:
        flat = a.reshape(-1).astype(F32)
        flat = jnp.pad(flat, (0, -flat.size % TILE))
        parts.append(flat.reshape(-1, LANES))
    return jnp.concatenate(parts, axis=0)


def _unpack(pack, like):
    out, row = [], 0
    for a in like:
        n = int(np.prod(a.shape))
        nrows = -(-n // TILE) * SUBLANES
        out.append(pack[row:row + nrows].reshape(-1)[:n].reshape(a.shape))
        row += nrows
    return out


def _dense_gate_blocks(gate_w):
    w = gate_w.reshape(4, -1, 2, HEAD_DIM, HEAD_DIM)
    zero = jnp.zeros_like(w[:, :, 0])
    top = jnp.concatenate([w[:, :, 0], zero], axis=-1)
    bottom = jnp.concatenate([zero, w[:, :, 1]], axis=-1)
    return jnp.concatenate([top, bottom], axis=-2)


def _diag_gate_blocks(dense, shape):
    even = dense[:, :, :HEAD_DIM, :HEAD_DIM]
    odd = dense[:, :, HEAD_DIM:, HEAD_DIM:]
    return jnp.stack([even, odd], axis=2).reshape(shape)


LARGE = ("ffn1_w_in", "ffn1_w_out", "w_in_mix", "w_out_mix", "ffn2_w_in", "ffn2_w_out")
COLUMN_SHARDED = ("ffn1_w_in", "w_in_mix", "ffn2_w_in")
SHARDED_SMALL = ("lru_conv_w", "lru_lambda")
REPLICATED = ("norm_ffn1", "norm_mix", "lru_conv_b", "lru_gate_w", "lru_gate_b", "attn_rpb", "lru_out_norm",
              "attn_out_norm", "norm_ffn2", "norm_final")
SMALL_ORDER = REPLICATED + SHARDED_SMALL
WEIGHTS = ("norm_ffn1", "ffn1_w_in", "ffn1_w_out", "norm_mix", "w_in_mix", "lru_conv_w", "lru_conv_b", "lru_gate_w",
           "lru_gate_b", "lru_lambda", "attn_rpb", "lru_out_norm", "attn_out_norm", "w_out_mix", "norm_ffn2",
           "ffn2_w_in", "ffn2_w_out", "norm_final")


PARTS = {("gather", "ffn2_w_out"): 4, ("gather", "ffn2_w_in"): 8, ("to_chips", "ffn2_w_out"): 4,
         ("to_chips", "ffn2_w_in"): 8, ("to_chips", "ffn1_w_out"): 2}
CARRIES = {
    "gather_ffn1": [(("gather", "ffn1_w_in"), 1), (("gather", "ffn1_w_out"), 1), (("gather", "small"), 1)],
    "ffn1_fwd": [(("gather", "w_in_mix"), 1), (("gather", "w_out_mix"), 1)],
    "mix_in_proj": [(("gather", "ffn2_w_out"), 3)],
    "lru_fwd": [(("gather", "ffn2_w_out"), 1), (("gather", "ffn2_w_in"), 2)],
    "attn_fwd": [(("gather", "ffn2_w_in"), 3)],
    "mix_out_proj": [(("gather", "ffn2_w_in"), 3)],
    "ffn2_in_grad": [(("to_sibling", "ffn2_w_out"), 1)],
    "norm_ffn2_bwd": [(("to_sibling", "ffn2_w_in"), 1)],
    "mix_out_grad": [(("to_chips", "ffn2_w_out"), 1)],
    "mix_out_bwd": [(("to_chips", "ffn2_w_out"), 1)],
    "attn_bwd": [(("to_chips", "ffn2_w_out"), 2), (("to_chips", "ffn2_w_in"), 2)],
    "lru_bwd": [(("to_chips", "ffn2_w_in"), 3), (("to_sibling", "w_out_mix"), 1)],
    "mix_in_grad": [(("to_chips", "ffn2_w_in"), 2)],
    "mix_in_bwd": [(("to_chips", "ffn2_w_in"), 1), (("to_sibling", "w_in_mix"), 1)],
    "norm_mix_bwd": [(("to_chips", "w_out_mix"), 1)],
    "ffn1_bwd": [(("to_chips", "w_in_mix"), 1), (("gather", "small_grads"), 1)],
    "ffn1_in_grad_gate": [(("to_sibling", "ffn1_w_out"), 1)],
    "ffn1_in_grad_up": [(("to_chips", "ffn1_w_out"), 1)],
    "to_sibling_ffn1_in": [(("to_sibling", "ffn1_w_in"), 1), (("to_chips", "ffn1_w_out"), 1)],
    "to_chips_ffn1": [(("to_chips", "ffn1_w_in"), 1), (("gather", "late_grads"), 1)],
}


class _Transfer:
    def __init__(self, kind, src, dest, block_rows, parts):
        self.kind, self.src, self.dest = kind, src, dest
        self.ranges, self.taken = _split(block_rows, parts), 0

    def take(self, count):
        lo, hi = self.ranges[self.taken][0], self.ranges[self.taken + count - 1][1]
        self.taken += count
        return _Piece(self.kind, self.src, self.dest, lo, hi)


class _Traffic:
    def __init__(self):
        self.transfers = {}

    def open(self, kind, name, src):
        if kind == "gather":
            dest, rows = _gathered(src), src.shape[0]
        elif kind == "to_sibling":
            dest, rows = SDS((src.shape[0] // 2, src.shape[1]), src.dtype), src.shape[0] // N_DEV
        else:
            dest, rows = SDS(src.shape, src.dtype), src.shape[0] // N_CHIP
        self.transfers[kind, name] = _Transfer(kind, src, dest, rows, PARTS.get((kind, name), 1))

    def _job(self, host):
        moved = [self.transfers[key] for key, _ in CARRIES[host]]
        return moved, _Job([tr.take(count) for tr, (_, count) in zip(moved, CARRIES[host])])

    def carry(self, host, fn, *args, **kw):
        if host not in CARRIES:
            return fn(*args, name=host, **kw)
        moved, job = self._job(host)
        res, landed = fn(*args, name=host, job=job, **kw)
        for tr, arr in zip(moved, landed):
            tr.dest = arr
        return res

    def alone(self, host):
        moved, job = self._job(host)
        for tr, arr in zip(moved, _run_job(job, host)):
            tr.dest = arr

    def result(self, kind, name):
        tr = self.transfers.pop((kind, name))
        assert tr.taken == len(tr.ranges), (kind, name)
        return tr.dest


def _forward_backward(x, target, shards, sharded_small, s):
    c = s["lru_conv_b"].shape[1]
    width = s["attn_out_norm"].shape[1]
    t = x.shape[0]
    traffic = _Traffic()
    carry = traffic.carry
    weight = lambda n: traffic.result("gather", n)

    for n in LARGE:
        traffic.open("gather", n, shards[n])
    traffic.open("gather", "small", sharded_small)
    traffic.alone("gather_ffn1")
    full_small = weight("small").reshape(N_DEV, SUBLANES, c // N_DEV)
    conv_w = full_small[:, :CONV_WIDTH].transpose(1, 0, 2).reshape(CONV_WIDTH, c)
    lam = full_small[:, CONV_WIDTH:CONV_WIDTH + 2].transpose(1, 0, 2).reshape(2, c)
    w = {n: weight(n) for n in ("ffn1_w_in", "ffn1_w_out")}
    u1 = _rmsnorm_fwd(x, s["norm_ffn1"], "norm_ffn1")
    h1, g1, up1 = carry("ffn1_fwd", _ffn_fwd, x, u1, w["ffn1_w_in"], w["ffn1_w_out"])
    w["w_in_mix"], w["w_out_mix"] = weight("w_in_mix"), weight("w_out_mix")
    u2 = _rmsnorm_fwd(h1, s["norm_mix"], "norm_mix")
    proj = carry("mix_in_proj", _mm, u2, w["w_in_mix"], nt=True, out_dtype=F32, tm=512, tn=512)
    gw = _dense_gate_blocks(s["lru_gate_w"]).astype(BF16)
    gb = s["lru_gate_b"].reshape(4, c)
    tables, tables_vjp = jax.vjp(_bias_tables, s["attn_rpb"])
    ya, hf, hb = carry("lru_fwd", _lru_fwd, proj, conv_w, s["lru_conv_b"], gw, gb, lam)
    yb = carry("attn_fwd", _attn_fwd, proj, tables, width)
    y, yt = _mixnorm_fwd(ya, yb, s["lru_out_norm"], s["attn_out_norm"], "mix_norm")
    h2 = carry("mix_out_proj", _mm, y, w["w_out_mix"], nt=False, out_dtype=F32, tm=512, tn=512, residual=h1)
    u3 = _rmsnorm_fwd(h2, s["norm_ffn2"], "norm_ffn2")
    w["ffn2_w_in"], w["ffn2_w_out"] = weight("ffn2_w_in"), weight("ffn2_w_out")
    h3, g2, up2 = carry("ffn2_fwd", _ffn_fwd, h2, u3, w["ffn2_w_in"], w["ffn2_w_out"])
    dh3, df2, loss_part, d_norm_final = _final_loss(h3, s["norm_final"], target, "final_loss")

    grads = {}
    grad_of = dict(nt=False, out_dtype=BF16, tm=512, tn=1024)

    def reduce_in_chip(n):
        traffic.open("to_sibling", n, grads[n])

    def reduce_over_chips(n):
        traffic.open("to_chips", n, _pair_sum(grads[n], traffic.result("to_sibling", n), "pair_sum_" + n))

    du3, hid2_t, da2_t = carry("ffn2_bwd", _ffn_bwd, df2, g2, up2, w["ffn2_w_in"], w["ffn2_w_out"])
    f = hid2_t.shape[0]
    grads["ffn2_w_out"] = carry("ffn2_out_grad", _mm, hid2_t, df2, **grad_of)
    reduce_in_chip("ffn2_w_out")
    grads["ffn2_w_in"] = carry("ffn2_in_grad", _mm, da2_t.reshape(2 * f, t), u3, **grad_of)
    reduce_in_chip("ffn2_w_in")
    reduce_over_chips("ffn2_w_out")
    dh2, dh2b, d_norm_ffn2 = carry("norm_ffn2_bwd", _rmsnorm_bwd, du3, h2, s["norm_ffn2"], dh3, 1.0)
    reduce_over_chips("ffn2_w_in")
    grads["w_out_mix"] = carry("mix_out_grad", _mm, yt, dh2b, **grad_of)
    reduce_in_chip("w_out_mix")
    dy = carry("mix_out_bwd", _mm, dh2b, w["w_out_mix"], nt=True, out_dtype=F32, tm=512, tn=512)
    dya, dyb, d_lru_out_norm, d_attn_out_norm = _mixnorm_bwd(dy, ya, yb, s["lru_out_norm"], s["attn_out_norm"],
                                                             "mix_norm_bwd")
    dq, dk, dv, dqkv_t, d_tables = carry("attn_bwd", _attn_bwd, proj, tables, dyb)
    dx_lru, dg_lru, dxg_t, d_conv_w, d_conv_b, d_gw, d_gb, d_lam = carry(
        "lru_bwd", _lru_bwd, proj, conv_w, s["lru_conv_b"], gw, gb, lam, hf, hb, dya)
    reduce_over_chips("w_out_mix")
    dproj = jnp.concatenate([dx_lru, dg_lru, dq, dk, dv], axis=1)
    dproj_t = jnp.concatenate([dxg_t.reshape(2 * c, t), dqkv_t.reshape(3 * width, t)], axis=0)
    grads["w_in_mix"] = carry("mix_in_grad", _mm, dproj_t, u2, **grad_of)
    reduce_in_chip("w_in_mix")
    du2 = carry("mix_in_bwd", _mm, dproj, w["w_in_mix"], nt=False, out_dtype=F32, tm=512, tn=512)
    reduce_over_chips("w_in_mix")
    dh1, df1, d_norm_mix = carry("norm_mix_bwd", _rmsnorm_bwd, du2, h1, s["norm_mix"], dh2, 0.5)

    by_device = lambda a: a.reshape(a.shape[0], N_DEV, -1).transpose(1, 0, 2)
    small = {
        "norm_mix": d_norm_mix, "lru_conv_b": d_conv_b, "lru_gate_w": _diag_gate_blocks(d_gw, s["lru_gate_w"].shape),
        "lru_gate_b": d_gb.reshape(s["lru_gate_b"].shape), "attn_rpb": tables_vjp(d_tables)[0],
        "lru_out_norm": d_lru_out_norm, "attn_out_norm": d_attn_out_norm, "norm_ffn2": d_norm_ffn2,
        "norm_final": d_norm_final, "lru_conv_w": by_device(d_conv_w), "lru_lambda": by_device(d_lam),
    }
    early = [small[n] for n in SMALL_ORDER[1:]]
    traffic.open("gather", "small_grads", _pack(early))

    du1, hid1_t, da1_t = carry("ffn1_bwd", _ffn_bwd, df1, g1, up1, w["ffn1_w_in"], w["ffn1_w_out"])
    grads["ffn1_w_out"] = carry("ffn1_out_grad", _mm, hid1_t, df1, **grad_of)
    reduce_in_chip("ffn1_w_out")
    gate_rows = carry("ffn1_in_grad_gate", _mm, da1_t, u1, lead=0, out_rows=2 * f, **grad_of)
    reduce_over_chips("ffn1_w_out")
    grads["ffn1_w_in"] = carry("ffn1_in_grad_up", _mm, da1_t, u1, lead=1, out_rows=2 * f, row_offset=f,
                               into=gate_rows, **grad_of)
    reduce_in_chip("ffn1_w_in")
    grad_x, _, d_norm_ffn1 = _rmsnorm_bwd(du1, x, s["norm_ffn1"], dh1, 1.0, "norm_ffn1_bwd")
    traffic.open("gather", "late_grads", _pack([d_norm_ffn1]))
    traffic.alone("to_sibling_ffn1_in")
    reduce_over_chips("ffn1_w_in")
    traffic.alone("to_chips_ffn1")
    partials = {n: traffic.result("to_chips", n) for n in LARGE}
    reduced = (_unpack(_sum_devices(traffic.result("gather", "late_grads"), "sum_late_grads"), [d_norm_ffn1])
               + _unpack(_sum_devices(traffic.result("gather", "small_grads"), "sum_small_grads"), early))
    assert not traffic.transfers, list(traffic.transfers)
    return loss_part[0, 0], grad_x, partials, dict(zip(SMALL_ORDER, reduced))


def _step(x, loss_target, p, m, v):
    me = 4 * lax.axis_index("x") + 2 * lax.axis_index("y") + lax.axis_index("c")

    shards = {n: (_cast_transposed if n in COLUMN_SHARDED else _cast_rows)(p[n], "cast_" + n) for n in LARGE}
    sharded_small = (jnp.pad(p["lru_conv_w"], ((0, SUBLANES - CONV_WIDTH), (0, 0)))
                     + jnp.pad(p["lru_lambda"], ((CONV_WIDTH, SUBLANES - CONV_WIDTH - 2), (0, 0))))
    s = {n: p[n] if n in ("lru_gate_w", "lru_gate_b", "attn_rpb") else p[n].reshape(1, -1) for n in REPLICATED}

    loss_part, grad_x, partials, small = _forward_backward(x, loss_target, shards, sharded_small, s)
    loss = lax.psum(loss_part, ("x", "y", "c"))

    out = {}
    for n in LARGE:
        update = _adamw_cols if n in COLUMN_SHARDED else _adamw_rows
        out[n] = update(p[n], partials[n], m[n], v[n], "adamw_" + n)

    g_small = {n: lax.dynamic_index_in_dim(g, me, axis=0, keepdims=False) if n in SHARDED_SMALL else g
               for n, g in small.items()}
    names = SMALL_ORDER
    like = [p[n] for n in names]
    pack_of = lambda d: _pack([d[n].reshape(p[n].shape) for n in names])
    upd = _adamw_small(pack_of(p), pack_of(g_small), pack_of(m), pack_of(v), "adamw_small")
    for n, d_, m_, v_ in zip(names, *[_unpack(u, like) for u in upd]):
        out[n] = (g_small[n].reshape(p[n].shape), d_, m_, v_)
    return loss, grad_x, out


def kernel(x, norm_ffn1, ffn1_w_in, ffn1_w_out, norm_mix, w_in_mix, lru_conv_w, lru_conv_b, lru_gate_w, lru_gate_b, lru_lambda, attn_rpb, lru_out_norm, attn_out_norm, w_out_mix, norm_ffn2, ffn2_w_in, ffn2_w_out, norm_final, loss_target, m_norm_ffn1, m_ffn1_w_in, m_ffn1_w_out, m_norm_mix, m_w_in_mix, m_lru_conv_w, m_lru_conv_b, m_lru_gate_w, m_lru_gate_b, m_lru_lambda, m_attn_rpb, m_lru_out_norm, m_attn_out_norm, m_w_out_mix, m_norm_ffn2, m_ffn2_w_in, m_ffn2_w_out, m_norm_final, v_norm_ffn1, v_ffn1_w_in, v_ffn1_w_out, v_norm_mix, v_w_in_mix, v_lru_conv_w, v_lru_conv_b, v_lru_gate_w, v_lru_gate_b, v_lru_lambda, v_attn_rpb, v_lru_out_norm, v_attn_out_norm, v_w_out_mix, v_norm_ffn2, v_ffn2_w_in, v_ffn2_w_out, v_norm_final):
    given = dict(locals())
    drop_layer = lambda n, a: a if n == "norm_final" else a[0]
    p = {n: drop_layer(n, given[n]) for n in WEIGHTS}
    m = {n: drop_layer(n, given["m_" + n]) for n in WEIGHTS}
    v = {n: drop_layer(n, given["v_" + n]) for n in WEIGHTS}
    loss, grad_x, out = _step(x[0], loss_target[0], p, m, v)
    shaped = lambda n, a: a.reshape(given[n].shape)
    return (loss, grad_x[None], *[shaped(n, out[n][k]) for k in range(4) for n in WEIGHTS])
```

```python
import math

import numpy as np
import jax
import jax.numpy as jnp
from jax import lax
from jax.experimental import pallas as pl
from jax.experimental.pallas import tpu as pltpu

F32 = jnp.float32
BF16 = jnp.bfloat16
SDS = jax.ShapeDtypeStruct

N_DEV = 8
N_CHIP = 4
NORM_EPS = 1e-6
RG_C = 8.0
CONV_WIDTH = 4
HEAD_DIM = 64
GRID_W = 64
WIN_ROWS = 8
WIN_COLS = 16
NEG = -1e30

ADAM_LR = 0.001
ADAM_B1 = 0.9
ADAM_B2 = 0.999
ADAM_EPS = 1e-08
ADAM_WD = 0.01
ADAM_STEP = 10

LANES = 128
SUBLANES = 8
VMEM_LIMIT = 56 * 1024 * 1024

NT = (((1,), (1,)), ((), ()))
TN = (((0,), (0,)), ((), ()))
ANY = pl.BlockSpec(memory_space=pl.ANY)
WHOLE = pl.BlockSpec(memory_space=pltpu.VMEM)
MESH = pl.DeviceIdType.MESH


def _sigmoid(x):
    return 1.0 / (1.0 + jnp.exp(-x))


def _gelu_parts(x):
    c = math.sqrt(2.0 / math.pi)
    t = jnp.tanh(c * (x + 0.044715 * (x * x * x)))
    gelu = 0.5 * x * (1.0 + t)
    dgelu = 0.5 * (1.0 + t) + 0.5 * x * (1.0 - t * t) * (c * (1.0 + 3.0 * 0.044715 * (x * x)))
    return gelu, dgelu


def _expm1(x):
    poly = x * (1.0 + x * (1.0 / 2) * (1.0 + x * (1.0 / 3) * (1.0 + x * (1.0 / 4) * (1.0 + x * (1.0 / 5) * (1.0 + x * (1.0 / 6))))))
    return jnp.where(jnp.abs(x) < 0.25, poly, jnp.exp(x) - 1.0)


def _softplus(x):
    return jnp.maximum(x, 0.0) + jnp.log1p(jnp.exp(-jnp.abs(x)))


class _Piece:
    N_REMOTE = {"gather": 7, "to_sibling": N_CHIP, "to_chips": 3}
    N_LOCAL = {"gather": 1, "to_sibling": 0, "to_chips": 1}

    def __init__(self, kind, src, dest, lo, hi):
        self.kind, self.src, self.dest, self.lo, self.hi = kind, src, dest, lo, hi


RELAY_AT = 60


class _Job:
    def __init__(self, pieces):
        self.pieces = list(pieces)
        self.ins = [p.src for p in self.pieces]
        self.out_shapes = [SDS(p.dest.shape, p.dest.dtype) for p in self.pieces]
        self.aliased = [i for i, p in enumerate(self.pieces) if not isinstance(p.dest, SDS)]
        self.n_remote = sum(_Piece.N_REMOTE[p.kind] for p in self.pieces)
        self.n_local = max(sum(_Piece.N_LOCAL[p.kind] for p in self.pieces), 1)

    def _each(self, step, ins, outs, send_sems, recv_sems, local_sems):
        remote = local = 0
        for p, src, dst in zip(self.pieces, ins, outs):
            _EXCHANGES[p.kind](step, p, src, dst, send_sems, recv_sems, local_sems, remote, local)
            remote += _Piece.N_REMOTE[p.kind]
            local += _Piece.N_LOCAL[p.kind]

    def start(self, *refs):
        self._each("start", *refs)

    def relay(self, *refs):
        self._each("relay", *refs)

    def finish(self, *refs):
        self._each("finish", *refs)


def _call(body, *, name, args, out_shape, in_specs, out_specs, grid=(), scratch_shapes=(), aliases=None, job=None):
    single = not isinstance(out_shape, (tuple, list))
    out_shape = (out_shape,) if single else tuple(out_shape)
    out_specs = (out_specs,) if single else tuple(out_specs)
    aliases = dict(aliases or {})
    params = pltpu.CompilerParams(dimension_semantics=("arbitrary",) * len(grid) if grid else None,
                                  vmem_limit_bytes=VMEM_LIMIT)
    if job is None:
        res = pl.pallas_call(body, out_shape=out_shape, grid=grid, in_specs=list(in_specs), out_specs=out_specs,
                             scratch_shapes=list(scratch_shapes), input_output_aliases=aliases, name=name,
                             compiler_params=params)(*args)
        return res[0] if single else res

    n_in, n_out, n_scr = len(args), len(out_shape), len(scratch_shapes)
    j_in, j_out, j_alias = len(job.ins), len(job.out_shapes), len(job.aliased)

    def hosted(*refs):
        ins, refs = refs[:n_in], refs[n_in:]
        j_ins, refs = refs[:j_in], refs[j_in + j_alias:]
        outs, refs = refs[:n_out], refs[n_out:]
        j_outs, refs = refs[:j_out], refs[j_out:]
        scr, sems = refs[:n_scr], refs[n_scr:]
        if grid:
            step = 0
            for axis, size in enumerate(grid):
                step = step * size + pl.program_id(axis)
            steps = math.prod(grid)
            pl.when(step == 0)(lambda: job.start(j_ins, j_outs, *sems))
            body(*ins, *outs, *scr)
            pl.when(step == min(RELAY_AT * steps // 100, steps - 1))(lambda: job.relay(j_ins, j_outs, *sems))
            pl.when(step == steps - 1)(lambda: job.finish(j_ins, j_outs, *sems))
        else:
            job.start(j_ins, j_outs, *sems)
            body(*ins, *outs, *scr)
            job.relay(j_ins, j_outs, *sems)
            job.finish(j_ins, j_outs, *sems)

    res = pl.pallas_call(
        hosted, out_shape=out_shape + tuple(job.out_shapes), grid=grid,
        in_specs=list(in_specs) + [ANY] * (j_in + j_alias), out_specs=out_specs + (ANY,) * j_out,
        scratch_shapes=list(scratch_shapes) + [pltpu.SemaphoreType.DMA((job.n_remote,)),
                                               pltpu.SemaphoreType.DMA((job.n_remote,)),
                                               pltpu.SemaphoreType.DMA((job.n_local,))],
        input_output_aliases={**aliases, **{n_in + j_in + k: n_out + i for k, i in enumerate(job.aliased)}},
        name=name, compiler_params=params)(*args, *job.ins, *[job.pieces[i].dest for i in job.aliased])
    own, carried = res[:n_out], res[n_out:]
    return (own[0] if single else own), carried


def _run_job(job, name):
    return _call(lambda: None, name=name, args=[], out_shape=(), in_specs=[], out_specs=(), job=job)[1]


def _position():
    return lax.axis_index("x"), lax.axis_index("y"), lax.axis_index("c")


def _flat(px, py, pc):
    return 4 * px + 2 * py + pc


def _gather_exchange(step, p, src, dst, send_sems, recv_sems, local_sems, r0, l0):
    x, y, c = _position()
    me, sibling = (x, y, c), (x, y, 1 - c)
    along_x, along_y, diagonal = (1 - x, y), (x, 1 - y), (1 - x, 1 - y)
    south = c == 0
    passed_on = (jnp.where(south, 1 - x, x), jnp.where(south, y, 1 - y))
    passed_to = (jnp.where(south, x, 1 - x), jnp.where(south, 1 - y, y))
    rb, n_rows = p.src.shape[0], p.hi - p.lo
    mine = src.at[pl.ds(p.lo, n_rows), :]

    def rows(block):
        return dst.at[pl.ds(_flat(*block) * rb + p.lo, n_rows), :]

    def copy(k, block, to, own=False):
        return pltpu.make_async_remote_copy(
            src_ref=mine if own else rows(block), dst_ref=rows(block),
            send_sem=send_sems.at[r0 + k], recv_sem=recv_sems.at[r0 + k], device_id=to, device_id_type=MESH)

    local = pltpu.make_async_copy(mine, rows(me), local_sems.at[l0])
    if step == "start":
        local.start()
        copy(0, me, sibling, own=True).start()
        copy(1, me, (*along_x, c), own=True).start()
        copy(2, me, (*along_y, c), own=True).start()
    elif step == "relay":
        copy(1, (*along_x, c), me).wait_recv()
        copy(2, (*along_y, c), me).wait_recv()
        copy(3, (*passed_on, c), (*passed_to, c)).start()
        copy(4, (*along_x, c), sibling).start()
        copy(5, (*along_y, c), sibling).start()
    else:
        copy(3, (*diagonal, c), me).wait_recv()
        copy(6, (*diagonal, c), sibling).start()
        copy(0, sibling, me).wait_recv()
        copy(4, (*along_x, 1 - c), me).wait_recv()
        copy(5, (*along_y, 1 - c), me).wait_recv()
        copy(6, (*diagonal, 1 - c), me).wait_recv()
        copy(0, me, sibling, own=True).wait_send()
        copy(1, me, (*along_x, c), own=True).wait_send()
        copy(2, me, (*along_y, c), own=True).wait_send()
        copy(3, (*passed_on, c), (*passed_to, c)).wait_send()
        copy(4, (*along_x, c), sibling).wait_send()
        copy(5, (*along_y, c), sibling).wait_send()
        copy(6, (*diagonal, c), sibling).wait_send()
        local.wait()


def _sibling_exchange(step, p, src, dst, send_sems, recv_sems, local_sems, r0, l0):
    x, y, c = _position()
    rb, n_rows = p.src.shape[0] // N_DEV, p.hi - p.lo
    for q in range(N_CHIP):
        copy = pltpu.make_async_remote_copy(
            src_ref=src.at[pl.ds((2 * q + 1 - c) * rb + p.lo, n_rows), :],
            dst_ref=dst.at[pl.ds(q * rb + p.lo, n_rows), :],
            send_sem=send_sems.at[r0 + q], recv_sem=recv_sems.at[r0 + q], device_id=(x, y, 1 - c), device_id_type=MESH)
        if step == "start":
            copy.start()
        elif step == "finish":
            copy.wait()


CHIP_FLIPS = [(1, 0), (0, 1), (1, 1)]


def _chips_exchange(step, p, src, dst, send_sems, recv_sems, local_sems, r0, l0):
    x, y, c = _position()
    rb, n_rows = p.src.shape[0] // N_CHIP, p.hi - p.lo

    def slot(ref, px, py):
        return ref.at[pl.ds((2 * px + py) * rb + p.lo, n_rows), :]

    def copy(k, landing=False):
        px = 1 - x if CHIP_FLIPS[k][0] else x
        py = 1 - y if CHIP_FLIPS[k][1] else y
        return pltpu.make_async_remote_copy(
            src_ref=slot(dst, px, py) if landing else slot(src, px, py),
            dst_ref=slot(dst, px, py) if landing else slot(dst, x, y),
            send_sem=send_sems.at[r0 + k], recv_sem=recv_sems.at[r0 + k], device_id=(px, py, c), device_id_type=MESH)

    local = pltpu.make_async_copy(slot(src, x, y), slot(dst, x, y), local_sems.at[l0])
    if step == "start":
        local.start()
        for k in range(3):
            copy(k).start()
    elif step == "finish":
        for k in range(3):
            copy(k, landing=True).wait_recv()
        for k in range(3):
            copy(k).wait_send()
        local.wait()


_EXCHANGES = {"gather": _gather_exchange, "to_sibling": _sibling_exchange, "to_chips": _chips_exchange}


def _gathered(shard):
    return SDS((N_DEV * shard.shape[0], shard.shape[1]), shard.dtype)


def _split(rows, parts):
    cuts = [rows * k // parts // 16 * 16 for k in range(parts)] + [rows]
    return list(zip(cuts[:-1], cuts[1:]))


def _pair_sum(g, from_sibling, name):
    rb, n = g.shape[0] // N_DEV, g.shape[1]
    tr = rb if rb * n * 2 <= 3 * 1024 * 1024 else rb // 2
    core = lax.axis_index("c").astype(jnp.int32).reshape(1)

    def body(c_ref, g_ref, r_ref, o_ref):
        o_ref[...] = (g_ref[...].astype(F32) + r_ref[...].astype(F32)).astype(BF16)

    grid_spec = pltpu.PrefetchScalarGridSpec(
        num_scalar_prefetch=1, grid=(N_CHIP, rb // tr),
        in_specs=[pl.BlockSpec((None, None, tr, n), lambda q, i, c_ref: (q, c_ref[0], i, 0)),
                  pl.BlockSpec((None, tr, n), lambda q, i, c_ref: (q, i, 0))],
        out_specs=pl.BlockSpec((None, tr, n), lambda q, i, c_ref: (q, i, 0)))
    out = pl.pallas_call(
        body, grid_spec=grid_spec, out_shape=SDS((N_CHIP, rb, n), BF16), name=name,
        compiler_params=pltpu.CompilerParams(dimension_semantics=("arbitrary",) * 2, vmem_limit_bytes=VMEM_LIMIT))(
            core, g.reshape(N_CHIP, 2, rb, n), from_sibling.reshape(N_CHIP, rb, n))
    return out.reshape(N_CHIP * rb, n)


def _sum_devices(gathered, name):
    r = gathered.shape[0] // N_DEV

    def body(g_ref, o_ref):
        acc = g_ref[0]
        for s in range(1, N_DEV):
            acc = acc + g_ref[s]
        o_ref[...] = acc

    return _call(body, name=name, args=[gathered.reshape(N_DEV, r, LANES)], out_shape=SDS((r, LANES), F32),
                 in_specs=[WHOLE], out_specs=WHOLE)


def _cast_rows(w, name):
    def body(w_ref, o_ref):
        o_ref[...] = w_ref[...].astype(BF16)

    return _call(body, name=name, args=[w], out_shape=SDS(w.shape, BF16), in_specs=[WHOLE], out_specs=WHOLE)


def _cast_transposed(w, name):
    d, n = w.shape
    td = 512

    def body(w_ref, o_ref):
        o_ref[...] = w_ref[...].T.astype(BF16)

    return _call(body, name=name, args=[w], out_shape=SDS((n, d), BF16), grid=(d // td,),
                 in_specs=[pl.BlockSpec((td, n), lambda i: (i, 0))], out_specs=pl.BlockSpec((n, td), lambda i: (0, i)))


ROW_TILE = 256


def _rmsnorm_fwd(h, gain, name):
    t, d = h.shape

    def body(h_ref, g_ref, u_ref):
        x = h_ref[...]
        u_ref[...] = (x * lax.rsqrt(jnp.mean(x * x, axis=-1, keepdims=True) + NORM_EPS) * g_ref[...]).astype(BF16)

    row = pl.BlockSpec((ROW_TILE, d), lambda i: (i, 0))
    return _call(body, name=name, args=[h, gain], out_shape=SDS((t, d), BF16), grid=(t // ROW_TILE,),
                 in_specs=[row, pl.BlockSpec((1, d), lambda i: (0, 0))], out_specs=row)


def _rms_bwd_math(x, gain, dy):
    rstd = lax.rsqrt(jnp.mean(x * x, axis=-1, keepdims=True) + NORM_EPS)
    xhat = x * rstd
    dxh = dy * gain
    dx = rstd * (dxh - xhat * jnp.mean(dxh * xhat, axis=-1, keepdims=True))
    return dx, jnp.sum(dy * xhat, axis=0, keepdims=True)


def _rmsnorm_bwd(du, h, gain, resid, bf_scale, name, job=None):
    t, d = h.shape

    def body(du_ref, h_ref, g_ref, r_ref, dh_ref, dhb_ref, dg_ref):
        @pl.when(pl.program_id(0) == 0)
        def _():
            dg_ref[...] = jnp.zeros_like(dg_ref)

        dx, dg = _rms_bwd_math(h_ref[...], g_ref[...], du_ref[...])
        dh = r_ref[...] + dx
        dh_ref[...] = dh
        dhb_ref[...] = (bf_scale * dh).astype(BF16)
        dg_ref[...] += dg

    row = pl.BlockSpec((ROW_TILE, d), lambda i: (i, 0))
    vec = pl.BlockSpec((1, d), lambda i: (0, 0))
    return _call(body, name=name, args=[du, h, gain, resid],
                 out_shape=(SDS((t, d), F32), SDS((t, d), BF16), SDS((1, d), F32)), grid=(t // ROW_TILE,),
                 in_specs=[row, row, vec, row], out_specs=(row, row, vec), job=job)


def _final_loss(h, gain, target, name):
    t, d = h.shape

    def body(h_ref, g_ref, t_ref, dh_ref, dhb_ref, loss_ref, dg_ref):
        @pl.when(pl.program_id(0) == 0)
        def _():
            dg_ref[...] = jnp.zeros_like(dg_ref)
            loss_ref[...] = jnp.zeros_like(loss_ref)

        x = h_ref[...]
        gain = g_ref[...]
        out = x * lax.rsqrt(jnp.mean(x * x, axis=-1, keepdims=True) + NORM_EPS) * gain
        err = out - t_ref[...]
        loss_ref[...] += 0.5 * jnp.sum(jnp.mean(err * err, axis=-1, keepdims=True), axis=0, keepdims=True)
        dx, dg = _rms_bwd_math(x, gain, err * (1.0 / d))
        dh_ref[...] = dx
        dhb_ref[...] = (0.5 * dx).astype(BF16)
        dg_ref[...] += dg

    row = pl.BlockSpec((ROW_TILE, d), lambda i: (i, 0))
    vec = pl.BlockSpec((1, d), lambda i: (0, 0))
    one = pl.BlockSpec((SUBLANES, LANES), lambda i: (0, 0))
    return _call(body, name=name, args=[h, gain, target],
                 out_shape=(SDS((t, d), F32), SDS((t, d), BF16), SDS((SUBLANES, LANES), F32), SDS((1, d), F32)),
                 grid=(t // ROW_TILE,), in_specs=[row, vec, row], out_specs=(row, row, one, vec))


def _mixnorm_fwd(ya, yb, ga, gb, name):
    t, c = ya.shape

    def body(ya_ref, yb_ref, ga_ref, gb_ref, y_ref, yt_ref):
        for k, (src, g_ref) in enumerate(((ya_ref, ga_ref), (yb_ref, gb_ref))):
            x = src[...]
            u = x * lax.rsqrt(jnp.mean(x * x, axis=-1, keepdims=True) + NORM_EPS) * g_ref[...]
            y_ref[:, k * c:(k + 1) * c] = u.astype(BF16)
            yt_ref[k * c:(k + 1) * c, :] = u.T.astype(BF16)

    row = pl.BlockSpec((ROW_TILE, c), lambda i: (i, 0))
    vec = pl.BlockSpec((1, c), lambda i: (0, 0))
    return _call(body, name=name, args=[ya, yb, ga, gb],
                 out_shape=(SDS((t, 2 * c), BF16), SDS((2 * c, t), BF16)), grid=(t // ROW_TILE,),
                 in_specs=[row, row, vec, vec],
                 out_specs=(pl.BlockSpec((ROW_TILE, 2 * c), lambda i: (i, 0)),
                            pl.BlockSpec((2 * c, ROW_TILE), lambda i: (0, i))))


def _mixnorm_bwd(dy, ya, yb, ga, gb, name):
    t, c = ya.shape

    def body(dy_ref, ya_ref, yb_ref, ga_ref, gb_ref, dya_ref, dyb_ref, dga_ref, dgb_ref):
        @pl.when(pl.program_id(0) == 0)
        def _():
            dga_ref[...] = jnp.zeros_like(dga_ref)
            dgb_ref[...] = jnp.zeros_like(dgb_ref)

        dxa, dga = _rms_bwd_math(ya_ref[...], ga_ref[...], dy_ref[:, :c])
        dxb, dgb = _rms_bwd_math(yb_ref[...], gb_ref[...], dy_ref[:, c:])
        dya_ref[...] = dxa
        dyb_ref[...] = dxb
        dga_ref[...] += dga
        dgb_ref[...] += dgb

    row = pl.BlockSpec((ROW_TILE, c), lambda i: (i, 0))
    vec = pl.BlockSpec((1, c), lambda i: (0, 0))
    return _call(body, name=name, args=[dy, ya, yb, ga, gb],
                 out_shape=(SDS((t, c), F32), SDS((t, c), F32), SDS((1, c), F32), SDS((1, c), F32)),
                 grid=(t // ROW_TILE,),
                 in_specs=[pl.BlockSpec((ROW_TILE, 2 * c), lambda i: (i, 0)), row, row, vec, vec],
                 out_specs=(row, row, vec, vec))


def _tile(n, want):
    return max(t for t in range(LANES, min(n, want) + 1, LANES) if n % t == 0)


def _mm(a, b, *, nt, out_dtype, tm, tn, name, residual=None, lead=None, out_rows=None, row_offset=0, into=None,
        job=None):
    parts = list(a) if isinstance(a, (list, tuple)) else [a]
    m = parts[0].shape[-2]
    widths = [p.shape[-1] for p in parts]
    k = sum(widths)
    n = b.shape[0] if nt else b.shape[1]
    tm, tn = _tile(math.gcd(m, row_offset), tm), _tile(n, tn)
    out_rows = m if out_rows is None else out_rows

    def body(*refs):
        a_refs, b_ref, rest = refs[:len(parts)], refs[len(parts)], refs[len(parts) + 1:]
        o_ref = rest[-1]
        out, at = None, 0
        for a_ref, width in zip(a_refs, widths):
            av = a_ref[...].astype(BF16)
            if nt:
                term = lax.dot_general(av, b_ref[:, at:at + width].astype(BF16), NT, preferred_element_type=F32)
            else:
                term = jnp.dot(av, b_ref[at:at + width, :].astype(BF16), preferred_element_type=F32)
            out = term if out is None else out + term
            at += width
        if residual is not None:
            out = rest[0][...] + out
        o_ref[...] = out.astype(out_dtype)

    a_specs = ([pl.BlockSpec((tm, width), lambda i, j: (i, 0)) for width in widths] if lead is None
               else [pl.BlockSpec((None, tm, k), lambda i, j: (lead, i, 0))])
    in_specs = a_specs + [pl.BlockSpec((tn, k), lambda i, j: (j, 0)) if nt else pl.BlockSpec((k, tn), lambda i, j: (0, j))]
    args, aliases = parts + [b], {}
    if residual is not None:
        in_specs.append(pl.BlockSpec((tm, tn), lambda i, j: (i, j)))
        args.append(residual)
    if into is not None:
        in_specs.append(ANY)
        aliases[len(args)] = 0
        args.append(into)
    return _call(body, name=name, args=args, out_shape=SDS((out_rows, n), out_dtype), grid=(m // tm, n // tn),
                 in_specs=in_specs, out_specs=pl.BlockSpec((tm, tn), lambda i, j: (row_offset // tm + i, j)),
                 aliases=aliases, job=job)


FFN_TM = 512
FFN_HB = 512


def _ffn_fwd(h, u, w_in_t, w_out, name, job=None):
    t, d = h.shape
    f = w_out.shape[0]
    nk = f // FFN_HB

    def body(u_ref, w_ref, wo_ref, h_ref, hn_ref, g_ref, up_ref, acc):
        k = pl.program_id(1)

        @pl.when(k == 0)
        def _():
            acc[...] = jnp.zeros_like(acc)

        uu = u_ref[...]
        g = lax.dot_general(uu, w_ref[0], NT, preferred_element_type=F32)
        up = lax.dot_general(uu, w_ref[1], NT, preferred_element_type=F32)
        g_ref[...] = g
        up_ref[...] = up
        hid = (g * _sigmoid(g)) * up
        acc[...] += jnp.dot(hid.astype(BF16), wo_ref[...], preferred_element_type=F32)

        @pl.when(k == nk - 1)
        def _():
            hn_ref[...] = h_ref[...] + 0.5 * acc[...]

    tok = pl.BlockSpec((FFN_TM, d), lambda i, k: (i, 0))
    pre = pl.BlockSpec((FFN_TM, FFN_HB), lambda i, k: (i, k))
    return _call(body, name=name, args=[u, w_in_t.reshape(2, f, d), w_out, h],
                 out_shape=(SDS((t, d), F32), SDS((t, f), F32), SDS((t, f), F32)), grid=(t // FFN_TM, nk),
                 in_specs=[tok, pl.BlockSpec((2, FFN_HB, d), lambda i, k: (0, k, 0)),
                           pl.BlockSpec((FFN_HB, d), lambda i, k: (k, 0)), tok],
                 out_specs=(tok, pre, pre), scratch_shapes=[pltpu.VMEM((FFN_TM, d), F32)], job=job)


def _ffn_bwd(dfb, gpre, upre, w_in_t, w_out, name, job=None):
    t, d = dfb.shape
    f = w_out.shape[0]
    nk = f // FFN_HB

    def body(df_ref, g_ref, up_ref, w_ref, wo_ref, du_ref, hid_t_ref, da_t_ref, acc):
        k = pl.program_id(1)

        @pl.when(k == 0)
        def _():
            acc[...] = jnp.zeros_like(acc)

        dhid = lax.dot_general(df_ref[...], wo_ref[...], NT, preferred_element_type=F32)
        g, up = g_ref[...], up_ref[...]
        sig = _sigmoid(g)
        silu = g * sig
        dup = dhid * silu
        dg = dhid * up * (sig * (1.0 + g * (1.0 - sig)))
        hid_t_ref[...] = (silu * up).T.astype(BF16)
        da_t_ref[0] = dg.T.astype(BF16)
        da_t_ref[1] = dup.T.astype(BF16)
        acc[...] += (jnp.dot(dg.astype(BF16), w_ref[0], preferred_element_type=F32)
                     + jnp.dot(dup.astype(BF16), w_ref[1], preferred_element_type=F32))

        @pl.when(k == nk - 1)
        def _():
            du_ref[...] = acc[...]

    tok = pl.BlockSpec((FFN_TM, d), lambda i, k: (i, 0))
    pre = pl.BlockSpec((FFN_TM, FFN_HB), lambda i, k: (i, k))
    return _call(body, name=name, args=[dfb, gpre, upre, w_in_t.reshape(2, f, d), w_out],
                 out_shape=(SDS((t, d), F32), SDS((f, t), BF16), SDS((2, f, t), BF16)), grid=(t // FFN_TM, nk),
                 in_specs=[tok, pre, pre, pl.BlockSpec((2, FFN_HB, d), lambda i, k: (0, k, 0)),
                           pl.BlockSpec((FFN_HB, d), lambda i, k: (k, 0))],
                 out_specs=(tok, pl.BlockSpec((FFN_HB, FFN_TM), lambda i, k: (k, i)),
                            pl.BlockSpec((2, FFN_HB, FFN_TM), lambda i, k: (0, k, i))),
                 scratch_shapes=[pltpu.VMEM((FFN_TM, d), F32)], job=job)


CH = LANES
PAD = SUBLANES


def _lru_gates(xc, gw_ref, gb_ref, lam_ref, z):
    xcb = xc.astype(BF16)
    r = _sigmoid(jnp.dot(xcb, gw_ref[2 * z], preferred_element_type=F32) + gb_ref[pl.ds(2 * z, 1), :])
    i = _sigmoid(jnp.dot(xcb, gw_ref[2 * z + 1], preferred_element_type=F32) + gb_ref[pl.ds(2 * z + 1, 1), :])
    sp = _softplus(-lam_ref[pl.ds(z, 1), :])
    log_a = (-RG_C * r) * sp
    a = jnp.exp(log_a)
    mult = jnp.sqrt(-_expm1(2.0 * log_a))
    return r, i, sp, a, mult


def _conv(xpad, cw_ref, cb_ref, t):
    xc = cb_ref[...] + cw_ref[pl.ds(0, 1), :] * xpad[pl.ds(PAD - 2, t), :]
    for j in range(1, CONV_WIDTH):
        xc = xc + cw_ref[pl.ds(j, 1), :] * xpad[pl.ds(PAD - 2 + j, t), :]
    return xc


def _fill_padded(pad_ref, value, t):
    pad_ref[pl.ds(0, PAD), :] = jnp.zeros((PAD, CH), F32)
    pad_ref[pl.ds(PAD + t, PAD), :] = jnp.zeros((PAD, CH), F32)
    pad_ref[pl.ds(PAD, t), :] = value


def _scan_pair(t, a_up, b_up, out_up, a_down, b_down, out_down):
    row = lax.broadcasted_iota(jnp.int32, (SUBLANES, CH), 0)

    def compose(a, b, rising):
        for dist in (1, 2, 4):
            shift = dist if rising else SUBLANES - dist
            keep = (row >= dist) if rising else (row < SUBLANES - dist)
            b = jnp.where(keep, b + a * pltpu.roll(b, shift, axis=0), b)
            a = jnp.where(keep, a * pltpu.roll(a, shift, axis=0), a)
        return a, b

    def step(tt, carry):
        hu, hd = carry
        lo = pl.ds(pl.multiple_of(tt * SUBLANES, SUBLANES), SUBLANES)
        hi = pl.ds(pl.multiple_of(t - SUBLANES - tt * SUBLANES, SUBLANES), SUBLANES)
        a, b = compose(a_up[lo, :], b_up[lo, :], True)
        up = b + a * hu
        out_up[lo, :] = up
        a, b = compose(a_down[hi, :], b_down[hi, :], False)
        down = b + a * hd
        out_down[hi, :] = down
        return up[SUBLANES - 1:, :], down[:1, :]

    zero = jnp.zeros((1, CH), F32)
    lax.fori_loop(0, t // SUBLANES, step, (zero, zero), unroll=2)


def _lru_fwd(proj, cw, cb, gw, gb, lam, name, job=None):
    t = proj.shape[0]
    c = cw.shape[1]
    ncb = c // CH

    def body(x_ref, g_ref, cw_ref, cb_ref, gw_ref, gb_ref, lam_ref, ya_ref, hf_ref, hb_ref, xpad, a0, b0, a1, b1):
        _fill_padded(xpad, x_ref[...], t)
        xc = _conv(xpad, cw_ref, cb_ref, t)
        for z, (a_s, b_s) in enumerate(((a0, b0), (a1, b1))):
            _, i, _, a, mult = _lru_gates(xc, gw_ref, gb_ref, lam_ref, z)
            a_s[...] = a
            b_s[...] = mult * (i * xc)
        _scan_pair(t, a0, b0, hf_ref, a1, b1, hb_ref)
        gelu, _ = _gelu_parts(g_ref[...])
        ya_ref[...] = gelu * (hf_ref[...] + hb_ref[...])

    col = lambda off: pl.BlockSpec((t, CH), lambda i: (0, off + i))
    small = lambda rows: pl.BlockSpec((rows, CH), lambda i: (0, i))
    return _call(body, name=name, args=[proj, proj, cw, cb, gw, gb, lam], out_shape=(SDS((t, c), F32),) * 3,
                 grid=(ncb,),
                 in_specs=[col(0), col(ncb), small(CONV_WIDTH), small(1),
                           pl.BlockSpec((4, None, CH, CH), lambda i: (0, i, 0, 0)), small(4), small(2)],
                 out_specs=(col(0),) * 3,
                 scratch_shapes=[pltpu.VMEM((t + 2 * PAD, CH), F32)] + [pltpu.VMEM((t, CH), F32)] * 4, job=job)


def _lru_bwd(proj, cw, cb, gw, gb, lam, hf, hb, dya, name, job=None):
    t = proj.shape[0]
    c = cw.shape[1]
    ncb = c // CH

    def body(x_ref, g_ref, cw_ref, cb_ref, gw_ref, gb_ref, lam_ref, hf_ref, hb_ref, dya_ref,
             dx_ref, dg_ref, dt_ref, dcw_ref, dcb_ref, dgw_ref, dgb_ref, dlam_ref,
             xpad, hpad, dxc, a0, a1, dhs, dh0, dh1):
        _fill_padded(xpad, x_ref[...], t)
        xc = _conv(xpad, cw_ref, cb_ref, t)
        xcb = xc.astype(BF16)
        gates = [_lru_gates(xc, gw_ref, gb_ref, lam_ref, z) for z in range(2)]

        gelu, dgelu = _gelu_parts(g_ref[...])
        dya = dya_ref[...]
        dgate = dya * (hf_ref[...] + hb_ref[...]) * dgelu
        dg_ref[...] = dgate.astype(BF16)
        dt_ref[1] = dgate.T.astype(BF16)
        dhs[...] = dya * gelu

        _fill_padded(hpad, gates[0][3], t)
        a0[...] = hpad[pl.ds(PAD + 1, t), :]
        _fill_padded(hpad, gates[1][3], t)
        a1[...] = hpad[pl.ds(PAD - 1, t), :]
        _scan_pair(t, a1, dhs, dh1, a0, dhs, dh0)

        acc_dxc = jnp.zeros((t, CH), F32)
        for z, (h_ref, dh_ref, shift) in enumerate(((hf_ref, dh0, -1), (hb_ref, dh1, 1))):
            r, i, sp, a, mult = gates[z]
            _fill_padded(hpad, h_ref[...], t)
            h_nb = hpad[pl.ds(PAD + shift, t), :]
            db = dh_ref[...]
            da = db * h_nb
            d_i = db * mult * xc
            acc_dxc = acc_dxc + db * mult * i
            d_mult = db * i * xc
            d_la = da * a - d_mult * (a * a) / mult
            d_r = d_la * (-RG_C * sp)
            dlam_ref[pl.ds(z, 1), :] = (jnp.sum(d_la * (-RG_C * r), axis=0, keepdims=True)
                                        * (-_sigmoid(-lam_ref[pl.ds(z, 1), :])))
            for gate, d_pre in ((0, d_r * r * (1.0 - r)), (1, d_i * i * (1.0 - i))):
                zg = 2 * z + gate
                dgb_ref[pl.ds(zg, 1), :] = jnp.sum(d_pre, axis=0, keepdims=True)
                d_pre_b = d_pre.astype(BF16)
                dgw_ref[zg] = lax.dot_general(xcb, d_pre_b, TN, preferred_element_type=F32)
                acc_dxc = acc_dxc + lax.dot_general(d_pre_b, gw_ref[zg], NT, preferred_element_type=F32)

        dcb_ref[...] = jnp.sum(acc_dxc, axis=0, keepdims=True)
        for j in range(CONV_WIDTH):
            dcw_ref[pl.ds(j, 1), :] = jnp.sum(acc_dxc * xpad[pl.ds(PAD - 2 + j, t), :], axis=0, keepdims=True)
        _fill_padded(dxc, acc_dxc, t)
        dx = cw_ref[pl.ds(0, 1), :] * dxc[pl.ds(PAD + 2, t), :]
        for j in range(1, CONV_WIDTH):
            dx = dx + cw_ref[pl.ds(j, 1), :] * dxc[pl.ds(PAD + 2 - j, t), :]
        dx_ref[...] = dx.astype(BF16)
        dt_ref[0] = dx.T.astype(BF16)

    col = lambda off: pl.BlockSpec((t, CH), lambda i: (0, off + i))
    small = lambda rows: pl.BlockSpec((rows, CH), lambda i: (0, i))
    dense = pl.BlockSpec((4, None, CH, CH), lambda i: (0, i, 0, 0))
    padded = pltpu.VMEM((t + 2 * PAD, CH), F32)
    return _call(
        body, name=name, args=[proj, proj, cw, cb, gw, gb, lam, hf, hb, dya],
        out_shape=(SDS((t, c), BF16), SDS((t, c), BF16), SDS((2, c, t), BF16), SDS((CONV_WIDTH, c), F32),
                   SDS((1, c), F32), SDS((4, ncb, CH, CH), F32), SDS((4, c), F32), SDS((2, c), F32)),
        grid=(ncb,),
        in_specs=[col(0), col(ncb), small(CONV_WIDTH), small(1), dense, small(4), small(2), col(0), col(0), col(0)],
        out_specs=(col(0), col(0), pl.BlockSpec((2, CH, t), lambda i: (0, i, 0)), small(CONV_WIDTH), small(1),
                   dense, small(4), small(2)),
        scratch_shapes=[padded, padded, padded] + [pltpu.VMEM((t, CH), F32)] * 5, job=job)


Q_ROWS = 4
BAND_ROWS = WIN_ROWS + Q_ROWS
BAND_PAIRS = BAND_ROWS // 2
Q_BLOCK = Q_ROWS * GRID_W
BAND = BAND_ROWS * GRID_W
PAIR_W = 2 * GRID_W
N_BOTH = 2 * WIN_ROWS - 2
ENTRY_LEFT_OUT, ENTRY_RIGHT_OUT, ENTRY_OUT = N_BOTH, N_BOTH + 1, N_BOTH + 2
N_ENTRIES = N_BOTH + 3


def _bias_tables(rpb):
    cols = np.arange(GRID_W)
    start = np.clip(cols - WIN_COLS // 2, 0, GRID_W - WIN_COLS)
    valid = (cols[None, :] >= start[:, None]) & (cols[None, :] < start[:, None] + WIN_COLS)
    col_off = np.clip(cols[None, :] - cols[:, None] + WIN_COLS - 1, 0, 2 * WIN_COLS - 2)
    pick_col = jnp.asarray(np.eye(2 * WIN_COLS - 1, dtype=np.float32)[col_off] * valid[..., None])
    by_row = jnp.einsum("hrc,qkc->hrqk", rpb, pick_col, precision=lax.Precision.HIGHEST)
    by_row = jnp.where(jnp.asarray(valid)[None, None], by_row, NEG)
    out = jnp.full_like(by_row[:, :1], NEG)
    first_in, last_in = WIN_ROWS - 1 - WIN_ROWS // 2, 2 * (WIN_ROWS - 1) - WIN_ROWS // 2
    both = jnp.concatenate([by_row[:, :-1], by_row[:, 1:]], axis=-1)
    left_out = jnp.concatenate([out, by_row[:, first_in:first_in + 1]], axis=-1)
    right_out = jnp.concatenate([by_row[:, last_in:last_in + 1], out], axis=-1)
    return jnp.concatenate([both, left_out, right_out, jnp.concatenate([out, out], axis=-1)], axis=1)


def _band_start(m, rows):
    return jnp.clip(Q_ROWS * m - WIN_ROWS // 2, 0, rows - BAND_ROWS)


def _entry(r, key_row, rows):
    w0 = jnp.clip(r - WIN_ROWS // 2, 0, rows - WIN_ROWS)
    left = (key_row >= w0) & (key_row < w0 + WIN_ROWS)
    right = (key_row + 1 >= w0) & (key_row + 1 < w0 + WIN_ROWS)
    return jnp.where(left & right, key_row - r + WIN_ROWS - 1,
                     jnp.where(right, ENTRY_LEFT_OUT, jnp.where(left, ENTRY_RIGHT_OUT, ENTRY_OUT)))


def _transposed_pairs(dst, src_ref):
    for g in range(dst.shape[0]):
        dst[g] = src_ref[pl.ds(g * PAIR_W, PAIR_W), :].T.astype(BF16)


def _band_of(pairs_ref, first_pair, hh):
    heads = pl.ds(hh * HEAD_DIM, HEAD_DIM)
    return jnp.concatenate([pairs_ref[first_pair + g, heads, :] for g in range(BAND_PAIRS)], axis=1)


def _attn_block(qs, kt, tz_ref, hh, m, rows):
    rs = _band_start(m, rows)
    lanes = pl.ds(hh * HEAD_DIM, HEAD_DIM)
    qrows = pl.ds(pl.multiple_of(m * Q_BLOCK, Q_BLOCK), Q_BLOCK)
    band = pl.ds(pl.multiple_of(rs * GRID_W, PAIR_W), BAND)
    entries = [[_entry(Q_ROWS * m + i, rs + 2 * g, rows) for g in range(BAND_PAIRS)] for i in range(Q_ROWS)]
    bias = jnp.concatenate([jnp.concatenate([tz_ref[hh, e] for e in row], axis=1) for row in entries], axis=0)
    q = qs[qrows, lanes]
    s = jnp.dot(q, _band_of(kt, rs // 2, hh), preferred_element_type=F32) * (HEAD_DIM ** -0.5) + bias
    p = jnp.exp(s - jnp.max(s, axis=-1, keepdims=True))
    p = p / jnp.sum(p, axis=-1, keepdims=True)
    return q, p, qrows, band, lanes, entries, rs // 2


def _attn_fwd(proj, tables, width, name, job=None):
    t = proj.shape[0]
    rows = t // GRID_W
    npair = width // LANES
    first = (proj.shape[1] - 3 * width) // LANES

    def body(q_ref, k_ref, v_ref, tz_ref, o_ref, qs, vs, kt):
        qs[...] = q_ref[...].astype(BF16)
        vs[...] = v_ref[...].astype(BF16)
        _transposed_pairs(kt, k_ref)

        def block(m, carry):
            for hh in range(2):
                _, p, qrows, band, lanes, _, _ = _attn_block(qs, kt, tz_ref, hh, m, rows)
                o_ref[qrows, lanes] = jnp.dot(p.astype(BF16), vs[band, lanes], preferred_element_type=F32)
            return carry

        lax.fori_loop(0, rows // Q_ROWS, block, 0)

    col = lambda off: pl.BlockSpec((t, LANES), lambda i: (0, off + i))
    return _call(body, name=name, args=[proj, proj, proj, tables], out_shape=SDS((t, width), F32), grid=(npair,),
                 in_specs=[col(first), col(first + npair), col(first + 2 * npair),
                           pl.BlockSpec((2, N_ENTRIES, GRID_W, PAIR_W), lambda i: (i, 0, 0, 0))],
                 out_specs=col(0),
                 scratch_shapes=[pltpu.VMEM((t, LANES), BF16)] * 2 + [pltpu.VMEM((t // PAIR_W, LANES, PAIR_W), BF16)],
                 job=job)


def _attn_bwd(proj, tables, dyb, name, job=None):
    t, width = dyb.shape
    rows = t // GRID_W
    npair = width // LANES
    first = (proj.shape[1] - 3 * width) // LANES

    def body(q_ref, k_ref, v_ref, tz_ref, do_ref, dq_ref, dk_ref, dv_ref, dt_ref, dtz_ref, dq_s, dk_s, dv_s,
             qs, ks, vs, dos, kt, vt):
        qs[...] = q_ref[...].astype(BF16)
        ks[...] = k_ref[...].astype(BF16)
        vs[...] = v_ref[...].astype(BF16)
        dos[...] = do_ref[...].astype(BF16)
        _transposed_pairs(kt, k_ref)
        _transposed_pairs(vt, v_ref)
        dk_s[...] = jnp.zeros_like(dk_s)
        dv_s[...] = jnp.zeros_like(dv_s)
        dtz_ref[...] = jnp.zeros_like(dtz_ref)

        def block(m, carry):
            for hh in range(2):
                q, p, qrows, band, lanes, entries, first_pair = _attn_block(qs, kt, tz_ref, hh, m, rows)
                do = dos[qrows, lanes]
                dp = jnp.dot(do, _band_of(vt, first_pair, hh), preferred_element_type=F32)
                ds = p * (dp - jnp.sum(dp * p, axis=-1, keepdims=True))
                for i, row in enumerate(entries):
                    for g, e in enumerate(row):
                        dtz_ref[hh, e] += ds[i * GRID_W:(i + 1) * GRID_W, g * PAIR_W:(g + 1) * PAIR_W]
                dsb = (ds * (HEAD_DIM ** -0.5)).astype(BF16)
                dq_s[qrows, lanes] = jnp.dot(dsb, ks[band, lanes], preferred_element_type=F32)
                dk_s[band, lanes] += lax.dot_general(dsb, q, TN, preferred_element_type=F32)
                dv_s[band, lanes] += lax.dot_general(p.astype(BF16), do, TN, preferred_element_type=F32)
            return carry

        lax.fori_loop(0, rows // Q_ROWS, block, 0)
        for n, (src, dst) in enumerate(((dq_s, dq_ref), (dk_s, dk_ref), (dv_s, dv_ref))):
            val = src[...]
            dst[...] = val.astype(BF16)
            dt_ref[n] = val.T.astype(BF16)

    col = lambda off: pl.BlockSpec((t, LANES), lambda i: (0, off + i))
    table = pl.BlockSpec((2, N_ENTRIES, GRID_W, PAIR_W), lambda i: (i, 0, 0, 0))
    pairs = pltpu.VMEM((t // PAIR_W, LANES, PAIR_W), BF16)
    return _call(body, name=name, args=[proj, proj, proj, tables, dyb],
                 out_shape=(SDS((t, width), BF16),) * 3 + (SDS((3, width, t), BF16), SDS(tables.shape, F32)),
                 grid=(npair,),
                 in_specs=[col(first), col(first + npair), col(first + 2 * npair), table, col(0)],
                 out_specs=(col(0), col(0), col(0), pl.BlockSpec((3, LANES, t), lambda i: (0, i, 0)), table),
                 scratch_shapes=[pltpu.VMEM((t, LANES), F32)] * 3 + [pltpu.VMEM((t, LANES), BF16)] * 4 + [pairs, pairs],
                 job=job)


def _adamw_math(w, g, m, v):
    m = ADAM_B1 * m + (1.0 - ADAM_B1) * g
    v = ADAM_B2 * v + (1.0 - ADAM_B2) * (g * g)
    m_hat = m / (1.0 - ADAM_B1 ** ADAM_STEP)
    v_hat = v / (1.0 - ADAM_B2 ** ADAM_STEP)
    delta = -ADAM_LR * (m_hat / (jnp.sqrt(v_hat) + ADAM_EPS) + ADAM_WD * w)
    return delta, m, v


def _sum_partials(p_ref):
    g = p_ref[0].astype(F32)
    for s in range(1, N_CHIP):
        g = g + p_ref[s].astype(F32)
    return g


def _adamw_rows(w, partials, m, v, name):
    rb, n = w.shape
    tr = 64

    def body(w_ref, p_ref, m_ref, v_ref, g_ref, d_ref, nm_ref, nv_ref):
        g = _sum_partials(p_ref)
        g_ref[...] = g
        d_ref[...], nm_ref[...], nv_ref[...] = _adamw_math(w_ref[...], g, m_ref[...], v_ref[...])

    blk = pl.BlockSpec((tr, n), lambda i: (i, 0))
    return _call(body, name=name, args=[w, partials.reshape(N_CHIP, rb, n), m, v], out_shape=(SDS((rb, n), F32),) * 4,
                 grid=(rb // tr,), in_specs=[blk, pl.BlockSpec((N_CHIP, tr, n), lambda i: (0, i, 0)), blk, blk],
                 out_specs=(blk,) * 4)


def _adamw_cols(w, partials, m, v, name):
    d, nb = w.shape
    td = 256

    def body(w_ref, p_ref, m_ref, v_ref, g_ref, d_ref, nm_ref, nv_ref):
        g = _sum_partials(p_ref).T
        g_ref[...] = g
        d_ref[...], nm_ref[...], nv_ref[...] = _adamw_math(w_ref[...], g, m_ref[...], v_ref[...])

    blk = pl.BlockSpec((td, nb), lambda i: (i, 0))
    return _call(body, name=name, args=[w, partials.reshape(N_CHIP, nb, d), m, v], out_shape=(SDS((d, nb), F32),) * 4,
                 grid=(d // td,), in_specs=[blk, pl.BlockSpec((N_CHIP, nb, td), lambda i: (0, 0, i)), blk, blk],
                 out_specs=(blk,) * 4)


def _adamw_small(w, g, m, v, name):
    def body(w_ref, g_ref, m_ref, v_ref, d_ref, nm_ref, nv_ref):
        d_ref[...], nm_ref[...], nv_ref[...] = _adamw_math(w_ref[...], g_ref[...], m_ref[...], v_ref[...])

    return _call(body, name=name, args=[w, g, m, v], out_shape=(SDS(w.shape, F32),) * 3, in_specs=[WHOLE] * 4,
                 out_specs=(WHOLE,) * 3)


TILE = SUBLANES * LANES


def _pack(arrays):
    parts = []
    for a in arrays:
        flat = a.reshape(-1).astype(F32)
        flat = jnp.pad(flat, (0, -flat.size % TILE))
        parts.append(flat.reshape(-1, LANES))
    return jnp.concatenate(parts, axis=0)


def _unpack(pack, like):
    out, row = [], 0
    for a in like:
        n = int(np.prod(a.shape))
        nrows = -(-n // TILE) * SUBLANES
        out.append(pack[row:row + nrows].reshape(-1)[:n].reshape(a.shape))
        row += nrows
    return out


def _dense_gate_blocks(gate_w):
    w = gate_w.reshape(4, -1, 2, HEAD_DIM, HEAD_DIM)
    zero = jnp.zeros_like(w[:, :, 0])
    top = jnp.concatenate([w[:, :, 0], zero], axis=-1)
    bottom = jnp.concatenate([zero, w[:, :, 1]], axis=-1)
    return jnp.concatenate([top, bottom], axis=-2)


def _diag_gate_blocks(dense, shape):
    even = dense[:, :, :HEAD_DIM, :HEAD_DIM]
    odd = dense[:, :, HEAD_DIM:, HEAD_DIM:]
    return jnp.stack([even, odd], axis=2).reshape(shape)


LARGE = ("ffn1_w_in", "ffn1_w_out", "w_in_mix", "w_out_mix", "ffn2_w_in", "ffn2_w_out")
COLUMN_SHARDED = ("ffn1_w_in", "w_in_mix", "ffn2_w_in")
SHARDED_SMALL = ("lru_conv_w", "lru_lambda")
REPLICATED = ("norm_ffn1", "norm_mix", "lru_conv_b", "lru_gate_w", "lru_gate_b", "attn_rpb", "lru_out_norm",
              "attn_out_norm", "norm_ffn2", "norm_final")
SMALL_ORDER = REPLICATED + SHARDED_SMALL
WEIGHTS = ("norm_ffn1", "ffn1_w_in", "ffn1_w_out", "norm_mix", "w_in_mix", "lru_conv_w", "lru_conv_b", "lru_gate_w",
           "lru_gate_b", "lru_lambda", "attn_rpb", "lru_out_norm", "attn_out_norm", "w_out_mix", "norm_ffn2",
           "ffn2_w_in", "ffn2_w_out", "norm_final")


PARTS = {("gather", "ffn2_w_out"): 4, ("gather", "ffn2_w_in"): 8, ("to_chips", "ffn2_w_out"): 4,
         ("to_chips", "ffn2_w_in"): 8, ("to_chips", "ffn1_w_out"): 2}
CARRIES = {
    "gather_ffn1": [(("gather", "ffn1_w_in"), 1), (("gather", "ffn1_w_out"), 1), (("gather", "small"), 1)],
    "ffn1_fwd": [(("gather", "w_in_mix"), 1), (("gather", "w_out_mix"), 1)],
    "mix_in_proj": [(("gather", "ffn2_w_out"), 3)],
    "lru_fwd": [(("gather", "ffn2_w_out"), 1), (("gather", "ffn2_w_in"), 2)],
    "attn_fwd": [(("gather", "ffn2_w_in"), 3)],
    "mix_out_proj": [(("gather", "ffn2_w_in"), 3)],
    "ffn2_in_grad": [(("to_sibling", "ffn2_w_out"), 1)],
    "norm_ffn2_bwd": [(("to_sibling", "ffn2_w_in"), 1)],
    "mix_out_grad": [(("to_chips", "ffn2_w_out"), 1)],
    "mix_out_bwd": [(("to_chips", "ffn2_w_out"), 1)],
    "attn_bwd": [(("to_chips", "ffn2_w_out"), 2), (("to_chips", "ffn2_w_in"), 2)],
    "lru_bwd": [(("to_chips", "ffn2_w_in"), 3), (("to_sibling", "w_out_mix"), 1)],
    "mix_in_grad_lru": [(("to_chips", "ffn2_w_in"), 1)],
    "mix_in_grad_attn": [(("to_chips", "ffn2_w_in"), 1)],
    "mix_in_bwd": [(("to_chips", "ffn2_w_in"), 1), (("to_sibling", "w_in_mix"), 1)],
    "norm_mix_bwd": [(("to_chips", "w_out_mix"), 1)],
    "ffn1_bwd": [(("to_chips", "w_in_mix"), 1), (("gather", "small_grads"), 1)],
    "ffn1_in_grad_gate": [(("to_sibling", "ffn1_w_out"), 1)],
    "ffn1_in_grad_up": [(("to_chips", "ffn1_w_out"), 1)],
    "to_sibling_ffn1_in": [(("to_sibling", "ffn1_w_in"), 1), (("to_chips", "ffn1_w_out"), 1)],
    "to_chips_ffn1": [(("to_chips", "ffn1_w_in"), 1), (("gather", "late_grads"), 1)],
}


class _Transfer:
    def __init__(self, kind, src, dest, block_rows, parts):
        self.kind, self.src, self.dest = kind, src, dest
        self.ranges, self.taken = _split(block_rows, parts), 0

    def take(self, count):
        lo, hi = self.ranges[self.taken][0], self.ranges[self.taken + count - 1][1]
        self.taken += count
        return _Piece(self.kind, self.src, self.dest, lo, hi)


class _Traffic:
    def __init__(self):
        self.transfers = {}

    def open(self, kind, name, src):
        if kind == "gather":
            dest, rows = _gathered(src), src.shape[0]
        elif kind == "to_sibling":
            dest, rows = SDS((src.shape[0] // 2, src.shape[1]), src.dtype), src.shape[0] // N_DEV
        else:
            dest, rows = SDS(src.shape, src.dtype), src.shape[0] // N_CHIP
        self.transfers[kind, name] = _Transfer(kind, src, dest, rows, PARTS.get((kind, name), 1))

    def _job(self, host):
        moved = [self.transfers[key] for key, _ in CARRIES[host]]
        return moved, _Job([tr.take(count) for tr, (_, count) in zip(moved, CARRIES[host])])

    def carry(self, host, fn, *args, **kw):
        if host not in CARRIES:
            return fn(*args, name=host, **kw)
        moved, job = self._job(host)
        res, landed = fn(*args, name=host, job=job, **kw)
        for tr, arr in zip(moved, landed):
            tr.dest = arr
        return res

    def alone(self, host):
        moved, job = self._job(host)
        for tr, arr in zip(moved, _run_job(job, host)):
            tr.dest = arr

    def result(self, kind, name):
        tr = self.transfers.pop((kind, name))
        assert tr.taken == len(tr.ranges), (kind, name)
        return tr.dest


def _forward_backward(x, target, shards, sharded_small, s):
    c = s["lru_conv_b"].shape[1]
    width = s["attn_out_norm"].shape[1]
    t = x.shape[0]
    traffic = _Traffic()
    carry = traffic.carry
    weight = lambda n: traffic.result("gather", n)

    for n in LARGE:
        traffic.open("gather", n, shards[n])
    traffic.open("gather", "small", sharded_small)
    traffic.alone("gather_ffn1")
    full_small = weight("small").reshape(N_DEV, SUBLANES, c // N_DEV)
    conv_w = full_small[:, :CONV_WIDTH].transpose(1, 0, 2).reshape(CONV_WIDTH, c)
    lam = full_small[:, CONV_WIDTH:CONV_WIDTH + 2].transpose(1, 0, 2).reshape(2, c)
    w = {n: weight(n) for n in ("ffn1_w_in", "ffn1_w_out")}
    u1 = _rmsnorm_fwd(x, s["norm_ffn1"], "norm_ffn1")
    h1, g1, up1 = carry("ffn1_fwd", _ffn_fwd, x, u1, w["ffn1_w_in"], w["ffn1_w_out"])
    w["w_in_mix"], w["w_out_mix"] = weight("w_in_mix"), weight("w_out_mix")
    u2 = _rmsnorm_fwd(h1, s["norm_mix"], "norm_mix")
    proj = carry("mix_in_proj", _mm, u2, w["w_in_mix"], nt=True, out_dtype=F32, tm=512, tn=512)
    gw = _dense_gate_blocks(s["lru_gate_w"]).astype(BF16)
    gb = s["lru_gate_b"].reshape(4, c)
    tables, tables_vjp = jax.vjp(_bias_tables, s["attn_rpb"])
    ya, hf, hb = carry("lru_fwd", _lru_fwd, proj, conv_w, s["lru_conv_b"], gw, gb, lam)
    yb = carry("attn_fwd", _attn_fwd, proj, tables, width)
    y, yt = _mixnorm_fwd(ya, yb, s["lru_out_norm"], s["attn_out_norm"], "mix_norm")
    h2 = carry("mix_out_proj", _mm, y, w["w_out_mix"], nt=False, out_dtype=F32, tm=512, tn=512, residual=h1)
    u3 = _rmsnorm_fwd(h2, s["norm_ffn2"], "norm_ffn2")
    w["ffn2_w_in"], w["ffn2_w_out"] = weight("ffn2_w_in"), weight("ffn2_w_out")
    h3, g2, up2 = carry("ffn2_fwd", _ffn_fwd, h2, u3, w["ffn2_w_in"], w["ffn2_w_out"])
    dh3, df2, loss_part, d_norm_final = _final_loss(h3, s["norm_final"], target, "final_loss")

    grads = {}
    grad_of = dict(nt=False, out_dtype=BF16, tm=512, tn=1024)

    def reduce_in_chip(n):
        traffic.open("to_sibling", n, grads[n])

    def reduce_over_chips(n):
        traffic.open("to_chips", n, _pair_sum(grads[n], traffic.result("to_sibling", n), "pair_sum_" + n))

    du3, hid2_t, da2_t = carry("ffn2_bwd", _ffn_bwd, df2, g2, up2, w["ffn2_w_in"], w["ffn2_w_out"])
    f = hid2_t.shape[0]
    grads["ffn2_w_out"] = carry("ffn2_out_grad", _mm, hid2_t, df2, **grad_of)
    reduce_in_chip("ffn2_w_out")
    grads["ffn2_w_in"] = carry("ffn2_in_grad", _mm, da2_t.reshape(2 * f, t), u3, **grad_of)
    reduce_in_chip("ffn2_w_in")
    reduce_over_chips("ffn2_w_out")
    dh2, dh2b, d_norm_ffn2 = carry("norm_ffn2_bwd", _rmsnorm_bwd, du3, h2, s["norm_ffn2"], dh3, 1.0)
    reduce_over_chips("ffn2_w_in")
    grads["w_out_mix"] = carry("mix_out_grad", _mm, yt, dh2b, **grad_of)
    reduce_in_chip("w_out_mix")
    dy = carry("mix_out_bwd", _mm, dh2b, w["w_out_mix"], nt=True, out_dtype=F32, tm=512, tn=512)
    dya, dyb, d_lru_out_norm, d_attn_out_norm = _mixnorm_bwd(dy, ya, yb, s["lru_out_norm"], s["attn_out_norm"],
                                                             "mix_norm_bwd")
    dq, dk, dv, dqkv_t, d_tables = carry("attn_bwd", _attn_bwd, proj, tables, dyb)
    dx_lru, dg_lru, dxg_t, d_conv_w, d_conv_b, d_gw, d_gb, d_lam = carry(
        "lru_bwd", _lru_bwd, proj, conv_w, s["lru_conv_b"], gw, gb, lam, hf, hb, dya)
    reduce_over_chips("w_out_mix")
    rows_of = 2 * c + 3 * width
    lru_rows = carry("mix_in_grad_lru", _mm, dxg_t.reshape(2 * c, t), u2, out_rows=rows_of, **grad_of)
    grads["w_in_mix"] = carry("mix_in_grad_attn", _mm, dqkv_t.reshape(3 * width, t), u2, out_rows=rows_of,
                              row_offset=2 * c, into=lru_rows, **grad_of)
    reduce_in_chip("w_in_mix")
    du2 = carry("mix_in_bwd", _mm, [dx_lru, dg_lru, dq, dk, dv], w["w_in_mix"], nt=False, out_dtype=F32, tm=512,
                tn=512)
    reduce_over_chips("w_in_mix")
    dh1, df1, d_norm_mix = carry("norm_mix_bwd", _rmsnorm_bwd, du2, h1, s["norm_mix"], dh2, 0.5)

    by_device = lambda a: a.reshape(a.shape[0], N_DEV, -1).transpose(1, 0, 2)
    small = {
        "norm_mix": d_norm_mix, "lru_conv_b": d_conv_b, "lru_gate_w": _diag_gate_blocks(d_gw, s["lru_gate_w"].shape),
        "lru_gate_b": d_gb.reshape(s["lru_gate_b"].shape), "attn_rpb": tables_vjp(d_tables)[0],
        "lru_out_norm": d_lru_out_norm, "attn_out_norm": d_attn_out_norm, "norm_ffn2": d_norm_ffn2,
        "norm_final": d_norm_final, "lru_conv_w": by_device(d_conv_w), "lru_lambda": by_device(d_lam),
    }
    early = [small[n] for n in SMALL_ORDER[1:]]
    traffic.open("gather", "small_grads", _pack(early))

    du1, hid1_t, da1_t = carry("ffn1_bwd", _ffn_bwd, df1, g1, up1, w["ffn1_w_in"], w["ffn1_w_out"])
    grads["ffn1_w_out"] = carry("ffn1_out_grad", _mm, hid1_t, df1, **grad_of)
    reduce_in_chip("ffn1_w_out")
    gate_rows = carry("ffn1_in_grad_gate", _mm, da1_t, u1, lead=0, out_rows=2 * f, **grad_of)
    reduce_over_chips("ffn1_w_out")
    grads["ffn1_w_in"] = carry("ffn1_in_grad_up", _mm, da1_t, u1, lead=1, out_rows=2 * f, row_offset=f,
                               into=gate_rows, **grad_of)
    reduce_in_chip("ffn1_w_in")
    grad_x, _, d_norm_ffn1 = _rmsnorm_bwd(du1, x, s["norm_ffn1"], dh1, 1.0, "norm_ffn1_bwd")
    traffic.open("gather", "late_grads", _pack([d_norm_ffn1]))
    traffic.alone("to_sibling_ffn1_in")
    reduce_over_chips("ffn1_w_in")
    traffic.alone("to_chips_ffn1")
    partials = {n: traffic.result("to_chips", n) for n in LARGE}
    reduced = (_unpack(_sum_devices(traffic.result("gather", "late_grads"), "sum_late_grads"), [d_norm_ffn1])
               + _unpack(_sum_devices(traffic.result("gather", "small_grads"), "sum_small_grads"), early))
    assert not traffic.transfers, list(traffic.transfers)
    return loss_part[0, 0], grad_x, partials, dict(zip(SMALL_ORDER, reduced))


def _step(x, loss_target, p, m, v):
    me = 4 * lax.axis_index("x") + 2 * lax.axis_index("y") + lax.axis_index("c")

    shards = {n: (_cast_transposed if n in COLUMN_SHARDED else _cast_rows)(p[n], "cast_" + n) for n in LARGE}
    sharded_small = (jnp.pad(p["lru_conv_w"], ((0, SUBLANES - CONV_WIDTH), (0, 0)))
                     + jnp.pad(p["lru_lambda"], ((CONV_WIDTH, SUBLANES - CONV_WIDTH - 2), (0, 0))))
    s = {n: p[n] if n in ("lru_gate_w", "lru_gate_b", "attn_rpb") else p[n].reshape(1, -1) for n in REPLICATED}

    loss_part, grad_x, partials, small = _forward_backward(x, loss_target, shards, sharded_small, s)
    loss = lax.psum(loss_part, ("x", "y", "c"))

    out = {}
    for n in LARGE:
        update = _adamw_cols if n in COLUMN_SHARDED else _adamw_rows
        out[n] = update(p[n], partials[n], m[n], v[n], "adamw_" + n)

    g_small = {n: lax.dynamic_index_in_dim(g, me, axis=0, keepdims=False) if n in SHARDED_SMALL else g
               for n, g in small.items()}
    names = SMALL_ORDER
    like = [p[n] for n in names]
    pack_of = lambda d: _pack([d[n].reshape(p[n].shape) for n in names])
    upd = _adamw_small(pack_of(p), pack_of(g_small), pack_of(m), pack_of(v), "adamw_small")
    for n, d_, m_, v_ in zip(names, *[_unpack(u, like) for u in upd]):
        out[n] = (g_small[n].reshape(p[n].shape), d_, m_, v_)
    return loss, grad_x, out


def kernel(x, norm_ffn1, ffn1_w_in, ffn1_w_out, norm_mix, w_in_mix, lru_conv_w, lru_conv_b, lru_gate_w, lru_gate_b, lru_lambda, attn_rpb, lru_out_norm, attn_out_norm, w_out_mix, norm_ffn2, ffn2_w_in, ffn2_w_out, norm_final, loss_target, m_norm_ffn1, m_ffn1_w_in, m_ffn1_w_out, m_norm_mix, m_w_in_mix, m_lru_conv_w, m_lru_conv_b, m_lru_gate_w, m_lru_gate_b, m_lru_lambda, m_attn_rpb, m_lru_out_norm, m_attn_out_norm, m_w_out_mix, m_norm_ffn2, m_ffn2_w_in, m_ffn2_w_out, m_norm_final, v_norm_ffn1, v_ffn1_w_in, v_ffn1_w_out, v_norm_mix, v_w_in_mix, v_lru_conv_w, v_lru_conv_b, v_lru_gate_w, v_lru_gate_b, v_lru_lambda, v_attn_rpb, v_lru_out_norm, v_attn_out_norm, v_w_out_mix, v_norm_ffn2, v_ffn2_w_in, v_ffn2_w_out, v_norm_final):
    given = dict(locals())
    drop_layer = lambda n, a: a if n == "norm_final" else a[0]
    p = {n: drop_layer(n, given[n]) for n in WEIGHTS}
    m = {n: drop_layer(n, given["m_" + n]) for n in WEIGHTS}
    v = {n: drop_layer(n, given["v_" + n]) for n in WEIGHTS}
    loss, grad_x, out = _step(x[0], loss_target[0], p, m, v)
    shaped = lambda n, a: a.reshape(given[n].shape)
    return (loss, grad_x[None], *[shaped(n, out[n][k]) for k in range(4) for n in WEIGHTS])
```

```python
import math

import numpy as np
import jax
import jax.numpy as jnp
from jax import lax
from jax.experimental import pallas as pl
from jax.experimental.pallas import tpu as pltpu

F32 = jnp.float32
BF16 = jnp.bfloat16
SDS = jax.ShapeDtypeStruct

N_DEV = 8
N_CHIP = 4
NORM_EPS = 1e-6
RG_C = 8.0
CONV_WIDTH = 4
HEAD_DIM = 64
GRID_W = 64
WIN_ROWS = 8
WIN_COLS = 16
NEG = -1e30

ADAM_LR = 0.001
ADAM_B1 = 0.9
ADAM_B2 = 0.999
ADAM_EPS = 1e-08
ADAM_WD = 0.01
ADAM_STEP = 10

LANES = 128
SUBLANES = 8
VMEM_LIMIT = 56 * 1024 * 1024

NT = (((1,), (1,)), ((), ()))
TN = (((0,), (0,)), ((), ()))
ANY = pl.BlockSpec(memory_space=pl.ANY)
WHOLE = pl.BlockSpec(memory_space=pltpu.VMEM)
MESH = pl.DeviceIdType.MESH


def _sigmoid(x):
    return 1.0 / (1.0 + jnp.exp(-x))


def _gelu_parts(x):
    c = math.sqrt(2.0 / math.pi)
    t = jnp.tanh(c * (x + 0.044715 * (x * x * x)))
    gelu = 0.5 * x * (1.0 + t)
    dgelu = 0.5 * (1.0 + t) + 0.5 * x * (1.0 - t * t) * (c * (1.0 + 3.0 * 0.044715 * (x * x)))
    return gelu, dgelu


def _expm1(x):
    poly = x * (1.0 + x * (1.0 / 2) * (1.0 + x * (1.0 / 3) * (1.0 + x * (1.0 / 4) * (1.0 + x * (1.0 / 5) * (1.0 + x * (1.0 / 6))))))
    return jnp.where(jnp.abs(x) < 0.25, poly, jnp.exp(x) - 1.0)


def _softplus(x):
    return jnp.maximum(x, 0.0) + jnp.log1p(jnp.exp(-jnp.abs(x)))


class _Piece:
    N_REMOTE = {"gather": 7, "to_sibling": N_CHIP, "to_chips": 3}
    N_LOCAL = {"gather": 1, "to_sibling": 0, "to_chips": 1}

    def __init__(self, kind, src, dest, lo, hi):
        self.kind, self.src, self.dest, self.lo, self.hi = kind, src, dest, lo, hi


RELAY_AT = 60


class _Job:
    def __init__(self, pieces):
        self.pieces = list(pieces)
        self.ins = [p.src for p in self.pieces]
        self.out_shapes = [SDS(p.dest.shape, p.dest.dtype) for p in self.pieces]
        self.aliased = [i for i, p in enumerate(self.pieces) if not isinstance(p.dest, SDS)]
        self.n_remote = sum(_Piece.N_REMOTE[p.kind] for p in self.pieces)
        self.n_local = max(sum(_Piece.N_LOCAL[p.kind] for p in self.pieces), 1)

    def _each(self, step, ins, outs, send_sems, recv_sems, local_sems):
        remote = local = 0
        for p, src, dst in zip(self.pieces, ins, outs):
            _EXCHANGES[p.kind](step, p, src, dst, send_sems, recv_sems, local_sems, remote, local)
            remote += _Piece.N_REMOTE[p.kind]
            local += _Piece.N_LOCAL[p.kind]

    def start(self, *refs):
        self._each("start", *refs)

    def relay(self, *refs):
        self._each("relay", *refs)

    def finish(self, *refs):
        self._each("finish", *refs)


def _call(body, *, name, args, out_shape, in_specs, out_specs, grid=(), scratch_shapes=(), aliases=None, job=None):
    single = not isinstance(out_shape, (tuple, list))
    out_shape = (out_shape,) if single else tuple(out_shape)
    out_specs = (out_specs,) if single else tuple(out_specs)
    aliases = dict(aliases or {})
    params = pltpu.CompilerParams(dimension_semantics=("arbitrary",) * len(grid) if grid else None,
                                  vmem_limit_bytes=VMEM_LIMIT)
    if job is None:
        res = pl.pallas_call(body, out_shape=out_shape, grid=grid, in_specs=list(in_specs), out_specs=out_specs,
                             scratch_shapes=list(scratch_shapes), input_output_aliases=aliases, name=name,
                             compiler_params=params)(*args)
        return res[0] if single else res

    n_in, n_out, n_scr = len(args), len(out_shape), len(scratch_shapes)
    j_in, j_out, j_alias = len(job.ins), len(job.out_shapes), len(job.aliased)

    def hosted(*refs):
        ins, refs = refs[:n_in], refs[n_in:]
        j_ins, refs = refs[:j_in], refs[j_in + j_alias:]
        outs, refs = refs[:n_out], refs[n_out:]
        j_outs, refs = refs[:j_out], refs[j_out:]
        scr, sems = refs[:n_scr], refs[n_scr:]
        if grid:
            step = 0
            for axis, size in enumerate(grid):
                step = step * size + pl.program_id(axis)
            steps = math.prod(grid)
            pl.when(step == 0)(lambda: job.start(j_ins, j_outs, *sems))
            body(*ins, *outs, *scr)
            pl.when(step == min(RELAY_AT * steps // 100, steps - 1))(lambda: job.relay(j_ins, j_outs, *sems))
            pl.when(step == steps - 1)(lambda: job.finish(j_ins, j_outs, *sems))
        else:
            job.start(j_ins, j_outs, *sems)
            body(*ins, *outs, *scr)
            job.relay(j_ins, j_outs, *sems)
            job.finish(j_ins, j_outs, *sems)

    res = pl.pallas_call(
        hosted, out_shape=out_shape + tuple(job.out_shapes), grid=grid,
        in_specs=list(in_specs) + [ANY] * (j_in + j_alias), out_specs=out_specs + (ANY,) * j_out,
        scratch_shapes=list(scratch_shapes) + [pltpu.SemaphoreType.DMA((job.n_remote,)),
                                               pltpu.SemaphoreType.DMA((job.n_remote,)),
                                               pltpu.SemaphoreType.DMA((job.n_local,))],
        input_output_aliases={**aliases, **{n_in + j_in + k: n_out + i for k, i in enumerate(job.aliased)}},
        name=name, compiler_params=params)(*args, *job.ins, *[job.pieces[i].dest for i in job.aliased])
    own, carried = res[:n_out], res[n_out:]
    return (own[0] if single else own), carried


def _run_job(job, name):
    return _call(lambda: None, name=name, args=[], out_shape=(), in_specs=[], out_specs=(), job=job)[1]


def _position():
    return lax.axis_index("x"), lax.axis_index("y"), lax.axis_index("c")


def _flat(px, py, pc):
    return 4 * px + 2 * py + pc


def _gather_exchange(step, p, src, dst, send_sems, recv_sems, local_sems, r0, l0):
    x, y, c = _position()
    me, sibling = (x, y, c), (x, y, 1 - c)
    along_x, along_y, diagonal = (1 - x, y), (x, 1 - y), (1 - x, 1 - y)
    south = c == 0
    passed_on = (jnp.where(south, 1 - x, x), jnp.where(south, y, 1 - y))
    passed_to = (jnp.where(south, x, 1 - x), jnp.where(south, 1 - y, y))
    rb, n_rows = p.src.shape[0], p.hi - p.lo
    mine = src.at[pl.ds(p.lo, n_rows), :]

    def rows(block):
        return dst.at[pl.ds(_flat(*block) * rb + p.lo, n_rows), :]

    def copy(k, block, to, own=False):
        return pltpu.make_async_remote_copy(
            src_ref=mine if own else rows(block), dst_ref=rows(block),
            send_sem=send_sems.at[r0 + k], recv_sem=recv_sems.at[r0 + k], device_id=to, device_id_type=MESH)

    local = pltpu.make_async_copy(mine, rows(me), local_sems.at[l0])
    if step == "start":
        local.start()
        copy(0, me, sibling, own=True).start()
        copy(1, me, (*along_x, c), own=True).start()
        copy(2, me, (*along_y, c), own=True).start()
    elif step == "relay":
        copy(1, (*along_x, c), me).wait_recv()
        copy(2, (*along_y, c), me).wait_recv()
        copy(3, (*passed_on, c), (*passed_to, c)).start()
        copy(4, (*along_x, c), sibling).start()
        copy(5, (*along_y, c), sibling).start()
    else:
        copy(3, (*diagonal, c), me).wait_recv()
        copy(6, (*diagonal, c), sibling).start()
        copy(0, sibling, me).wait_recv()
        copy(4, (*along_x, 1 - c), me).wait_recv()
        copy(5, (*along_y, 1 - c), me).wait_recv()
        copy(6, (*diagonal, 1 - c), me).wait_recv()
        copy(0, me, sibling, own=True).wait_send()
        copy(1, me, (*along_x, c), own=True).wait_send()
        copy(2, me, (*along_y, c), own=True).wait_send()
        copy(3, (*passed_on, c), (*passed_to, c)).wait_send()
        copy(4, (*along_x, c), sibling).wait_send()
        copy(5, (*along_y, c), sibling).wait_send()
        copy(6, (*diagonal, c), sibling).wait_send()
        local.wait()


def _sibling_exchange(step, p, src, dst, send_sems, recv_sems, local_sems, r0, l0):
    x, y, c = _position()
    rb, n_rows = p.src.shape[0] // N_DEV, p.hi - p.lo
    for q in range(N_CHIP):
        copy = pltpu.make_async_remote_copy(
            src_ref=src.at[pl.ds((2 * q + 1 - c) * rb + p.lo, n_rows), :],
            dst_ref=dst.at[pl.ds(q * rb + p.lo, n_rows), :],
            send_sem=send_sems.at[r0 + q], recv_sem=recv_sems.at[r0 + q], device_id=(x, y, 1 - c), device_id_type=MESH)
        if step == "start":
            copy.start()
        elif step == "finish":
            copy.wait()


CHIP_FLIPS = [(1, 0), (0, 1), (1, 1)]


def _chips_exchange(step, p, src, dst, send_sems, recv_sems, local_sems, r0, l0):
    x, y, c = _position()
    rb, n_rows = p.src.shape[0] // N_CHIP, p.hi - p.lo

    def slot(ref, px, py):
        return ref.at[pl.ds((2 * px + py) * rb + p.lo, n_rows), :]

    def copy(k, landing=False):
        px = 1 - x if CHIP_FLIPS[k][0] else x
        py = 1 - y if CHIP_FLIPS[k][1] else y
        return pltpu.make_async_remote_copy(
            src_ref=slot(dst, px, py) if landing else slot(src, px, py),
            dst_ref=slot(dst, px, py) if landing else slot(dst, x, y),
            send_sem=send_sems.at[r0 + k], recv_sem=recv_sems.at[r0 + k], device_id=(px, py, c), device_id_type=MESH)

    local = pltpu.make_async_copy(slot(src, x, y), slot(dst, x, y), local_sems.at[l0])
    if step == "start":
        local.start()
        for k in range(3):
            copy(k).start()
    elif step == "finish":
        for k in range(3):
            copy(k, landing=True).wait_recv()
        for k in range(3):
            copy(k).wait_send()
        local.wait()


_EXCHANGES = {"gather": _gather_exchange, "to_sibling": _sibling_exchange, "to_chips": _chips_exchange}


def _gathered(shard):
    return SDS((N_DEV * shard.shape[0], shard.shape[1]), shard.dtype)


def _split(rows, parts):
    cuts = [rows * k // parts // 16 * 16 for k in range(parts)] + [rows]
    return list(zip(cuts[:-1], cuts[1:]))


def _pair_sum(g, from_sibling, name):
    rb, n = g.shape[0] // N_DEV, g.shape[1]
    tr = rb if rb * n * 2 <= 3 * 1024 * 1024 else rb // 2
    core = lax.axis_index("c").astype(jnp.int32).reshape(1)

    def body(c_ref, g_ref, r_ref, o_ref):
        o_ref[...] = (g_ref[...].astype(F32) + r_ref[...].astype(F32)).astype(BF16)

    grid_spec = pltpu.PrefetchScalarGridSpec(
        num_scalar_prefetch=1, grid=(N_CHIP, rb // tr),
        in_specs=[pl.BlockSpec((None, None, tr, n), lambda q, i, c_ref: (q, c_ref[0], i, 0)),
                  pl.BlockSpec((None, tr, n), lambda q, i, c_ref: (q, i, 0))],
        out_specs=pl.BlockSpec((None, tr, n), lambda q, i, c_ref: (q, i, 0)))
    out = pl.pallas_call(
        body, grid_spec=grid_spec, out_shape=SDS((N_CHIP, rb, n), BF16), name=name,
        compiler_params=pltpu.CompilerParams(dimension_semantics=("arbitrary",) * 2, vmem_limit_bytes=VMEM_LIMIT))(
            core, g.reshape(N_CHIP, 2, rb, n), from_sibling.reshape(N_CHIP, rb, n))
    return out.reshape(N_CHIP * rb, n)


def _sum_devices(gathered, name):
    r = gathered.shape[0] // N_DEV

    def body(g_ref, o_ref):
        acc = g_ref[0]
        for s in range(1, N_DEV):
            acc = acc + g_ref[s]
        o_ref[...] = acc

    return _call(body, name=name, args=[gathered.reshape(N_DEV, r, LANES)], out_shape=SDS((r, LANES), F32),
                 in_specs=[WHOLE], out_specs=WHOLE)


def _cast_rows(w, name):
    def body(w_ref, o_ref):
        o_ref[...] = w_ref[...].astype(BF16)

    return _call(body, name=name, args=[w], out_shape=SDS(w.shape, BF16), in_specs=[WHOLE], out_specs=WHOLE)


def _cast_transposed(w, name):
    d, n = w.shape
    td = 512

    def body(w_ref, o_ref):
        o_ref[...] = w_ref[...].T.astype(BF16)

    return _call(body, name=name, args=[w], out_shape=SDS((n, d), BF16), grid=(d // td,),
                 in_specs=[pl.BlockSpec((td, n), lambda i: (i, 0))], out_specs=pl.BlockSpec((n, td), lambda i: (0, i)))


ROW_TILE = 256


def _rmsnorm_fwd(h, gain, name):
    t, d = h.shape

    def body(h_ref, g_ref, u_ref):
        x = h_ref[...]
        u_ref[...] = (x * lax.rsqrt(jnp.mean(x * x, axis=-1, keepdims=True) + NORM_EPS) * g_ref[...]).astype(BF16)

    row = pl.BlockSpec((ROW_TILE, d), lambda i: (i, 0))
    return _call(body, name=name, args=[h, gain], out_shape=SDS((t, d), BF16), grid=(t // ROW_TILE,),
                 in_specs=[row, pl.BlockSpec((1, d), lambda i: (0, 0))], out_specs=row)


def _rms_bwd_math(x, gain, dy):
    rstd = lax.rsqrt(jnp.mean(x * x, axis=-1, keepdims=True) + NORM_EPS)
    xhat = x * rstd
    dxh = dy * gain
    dx = rstd * (dxh - xhat * jnp.mean(dxh * xhat, axis=-1, keepdims=True))
    return dx, jnp.sum(dy * xhat, axis=0, keepdims=True)


def _rmsnorm_bwd(du, h, gain, resid, bf_scale, name, job=None):
    t, d = h.shape

    def body(du_ref, h_ref, g_ref, r_ref, dh_ref, dhb_ref, dg_ref):
        @pl.when(pl.program_id(0) == 0)
        def _():
            dg_ref[...] = jnp.zeros_like(dg_ref)

        dx, dg = _rms_bwd_math(h_ref[...], g_ref[...], du_ref[...])
        dh = r_ref[...] + dx
        dh_ref[...] = dh
        dhb_ref[...] = (bf_scale * dh).astype(BF16)
        dg_ref[...] += dg

    row = pl.BlockSpec((ROW_TILE, d), lambda i: (i, 0))
    vec = pl.BlockSpec((1, d), lambda i: (0, 0))
    return _call(body, name=name, args=[du, h, gain, resid],
                 out_shape=(SDS((t, d), F32), SDS((t, d), BF16), SDS((1, d), F32)), grid=(t // ROW_TILE,),
                 in_specs=[row, row, vec, row], out_specs=(row, row, vec), job=job)


def _final_loss(h, gain, target, name):
    t, d = h.shape

    def body(h_ref, g_ref, t_ref, dh_ref, dhb_ref, loss_ref, dg_ref):
        @pl.when(pl.program_id(0) == 0)
        def _():
            dg_ref[...] = jnp.zeros_like(dg_ref)
            loss_ref[...] = jnp.zeros_like(loss_ref)

        x = h_ref[...]
        gain = g_ref[...]
        out = x * lax.rsqrt(jnp.mean(x * x, axis=-1, keepdims=True) + NORM_EPS) * gain
        err = out - t_ref[...]
        loss_ref[...] += 0.5 * jnp.sum(jnp.mean(err * err, axis=-1, keepdims=True), axis=0, keepdims=True)
        dx, dg = _rms_bwd_math(x, gain, err * (1.0 / d))
        dh_ref[...] = dx
        dhb_ref[...] = (0.5 * dx).astype(BF16)
        dg_ref[...] += dg

    row = pl.BlockSpec((ROW_TILE, d), lambda i: (i, 0))
    vec = pl.BlockSpec((1, d), lambda i: (0, 0))
    one = pl.BlockSpec((SUBLANES, LANES), lambda i: (0, 0))
    return _call(body, name=name, args=[h, gain, target],
                 out_shape=(SDS((t, d), F32), SDS((t, d), BF16), SDS((SUBLANES, LANES), F32), SDS((1, d), F32)),
                 grid=(t // ROW_TILE,), in_specs=[row, vec, row], out_specs=(row, row, one, vec))


def _mixnorm_fwd(ya, yb, ga, gb, name):
    t, c = ya.shape

    def body(ya_ref, yb_ref, ga_ref, gb_ref, y_ref, yt_ref):
        for k, (src, g_ref) in enumerate(((ya_ref, ga_ref), (yb_ref, gb_ref))):
            x = src[...]
            u = x * lax.rsqrt(jnp.mean(x * x, axis=-1, keepdims=True) + NORM_EPS) * g_ref[...]
            y_ref[:, k * c:(k + 1) * c] = u.astype(BF16)
            yt_ref[k * c:(k + 1) * c, :] = u.T.astype(BF16)

    row = pl.BlockSpec((ROW_TILE, c), lambda i: (i, 0))
    vec = pl.BlockSpec((1, c), lambda i: (0, 0))
    return _call(body, name=name, args=[ya, yb, ga, gb],
                 out_shape=(SDS((t, 2 * c), BF16), SDS((2 * c, t), BF16)), grid=(t // ROW_TILE,),
                 in_specs=[row, row, vec, vec],
                 out_specs=(pl.BlockSpec((ROW_TILE, 2 * c), lambda i: (i, 0)),
                            pl.BlockSpec((2 * c, ROW_TILE), lambda i: (0, i))))


def _mixnorm_bwd(dy, ya, yb, ga, gb, name):
    t, c = ya.shape

    def body(dy_ref, ya_ref, yb_ref, ga_ref, gb_ref, dya_ref, dyb_ref, dga_ref, dgb_ref):
        @pl.when(pl.program_id(0) == 0)
        def _():
            dga_ref[...] = jnp.zeros_like(dga_ref)
            dgb_ref[...] = jnp.zeros_like(dgb_ref)

        dxa, dga = _rms_bwd_math(ya_ref[...], ga_ref[...], dy_ref[:, :c])
        dxb, dgb = _rms_bwd_math(yb_ref[...], gb_ref[...], dy_ref[:, c:])
        dya_ref[...] = dxa
        dyb_ref[...] = dxb
        dga_ref[...] += dga
        dgb_ref[...] += dgb

    row = pl.BlockSpec((ROW_TILE, c), lambda i: (i, 0))
    vec = pl.BlockSpec((1, c), lambda i: (0, 0))
    return _call(body, name=name, args=[dy, ya, yb, ga, gb],
                 out_shape=(SDS((t, c), F32), SDS((t, c), F32), SDS((1, c), F32), SDS((1, c), F32)),
                 grid=(t // ROW_TILE,),
                 in_specs=[pl.BlockSpec((ROW_TILE, 2 * c), lambda i: (i, 0)), row, row, vec, vec],
                 out_specs=(row, row, vec, vec))


def _tile(n, want):
    return max(t for t in range(LANES, min(n, want) + 1, LANES) if n % t == 0)


def _mm(a, b, *, nt, out_dtype, tm, tn, name, residual=None, scale=None, lead=None, out_rows=None, row_offset=0,
        into=None, job=None):
    parts = list(a) if isinstance(a, (list, tuple)) else [a]
    m = parts[0].shape[-2]
    widths = [p.shape[-1] for p in parts]
    k = sum(widths)
    n = b.shape[0] if nt else b.shape[1]
    tm, tn = _tile(math.gcd(m, row_offset), tm), _tile(n, tn)
    out_rows = m if out_rows is None else out_rows

    def body(*refs):
        a_refs, b_ref, rest = refs[:len(parts)], refs[len(parts)], refs[len(parts) + 1:]
        o_ref = rest[-1]
        out, at = None, 0
        for a_ref, width in zip(a_refs, widths):
            av = a_ref[...].astype(BF16)
            if nt:
                term = lax.dot_general(av, b_ref[:, at:at + width].astype(BF16), NT, preferred_element_type=F32)
            else:
                term = jnp.dot(av, b_ref[at:at + width, :].astype(BF16), preferred_element_type=F32)
            out = term if out is None else out + term
            at += width
        if residual is not None:
            out = rest[0][...] + (out if scale is None else scale * out)
        o_ref[...] = out.astype(out_dtype)

    a_specs =([pl.BlockSpec((tm, width), lambda i, j: (i, 0)) for width in widths] if lead is None
               else [pl.BlockSpec((None, tm, k), lambda i, j: (lead, i, 0))])
    in_specs = a_specs + [pl.BlockSpec((tn, k), lambda i, j: (j, 0)) if nt else pl.BlockSpec((k, tn), lambda i, j: (0, j))]
    args, aliases = parts + [b], {}
    if residual is not None:
        in_specs.append(pl.BlockSpec((tm, tn), lambda i, j: (i, j)))
        args.append(residual)
    if into is not None:
        in_specs.append(ANY)
        aliases[len(args)] = 0
        args.append(into)
    return _call(body, name=name, args=args, out_shape=SDS((out_rows, n), out_dtype), grid=(m // tm, n // tn),
                 in_specs=in_specs, out_specs=pl.BlockSpec((tm, tn), lambda i, j: (row_offset // tm + i, j)),
                 aliases=aliases, job=job)


FFN_TM = 512
FFN_HB = 512


def _ffn_hidden(u, w_in_t, name, job=None):
    t, d = u.shape
    f = w_in_t.shape[0] // 2

    def body(u_ref, w_ref, g_ref, up_ref, hid_ref, hid_t_ref):
        uu = u_ref[...]
        g = lax.dot_general(uu, w_ref[0], NT, preferred_element_type=F32)
        up = lax.dot_general(uu, w_ref[1], NT, preferred_element_type=F32)
        g_ref[...] = g
        up_ref[...] = up
        hid = (g * _sigmoid(g)) * up
        hid_ref[...] = hid.astype(BF16)
        hid_t_ref[...] = hid.T.astype(BF16)

    pre = pl.BlockSpec((FFN_TM, FFN_HB), lambda i, k: (i, k))
    return _call(body, name=name, args=[u, w_in_t.reshape(2, f, d)],
                 out_shape=(SDS((t, f), F32), SDS((t, f), F32), SDS((t, f), BF16), SDS((f, t), BF16)),
                 grid=(t // FFN_TM, f // FFN_HB),
                 in_specs=[pl.BlockSpec((FFN_TM, d), lambda i, k: (i, 0)),
                           pl.BlockSpec((2, FFN_HB, d), lambda i, k: (0, k, 0))],
                 out_specs=(pre, pre, pre, pl.BlockSpec((FFN_HB, FFN_TM), lambda i, k: (k, i))), job=job)


def _ffn_bwd(dfb, gpre, upre, w_in_t, w_out, name, job=None):
    t, d = dfb.shape
    f = w_out.shape[0]
    nk = f // FFN_HB

    def body(df_ref, g_ref, up_ref, w_ref, wo_ref, du_ref, da_t_ref, acc):
        k = pl.program_id(1)

        @pl.when(k == 0)
        def _():
            acc[...] = jnp.zeros_like(acc)

        dhid = lax.dot_general(df_ref[...], wo_ref[...], NT, preferred_element_type=F32)
        g, up = g_ref[...], up_ref[...]
        sig = _sigmoid(g)
        silu = g * sig
        dup = dhid * silu
        dg = dhid * up * (sig * (1.0 + g * (1.0 - sig)))
        da_t_ref[0] = dg.T.astype(BF16)
        da_t_ref[1] = dup.T.astype(BF16)
        acc[...] += (jnp.dot(dg.astype(BF16), w_ref[0], preferred_element_type=F32)
                     + jnp.dot(dup.astype(BF16), w_ref[1], preferred_element_type=F32))

        @pl.when(k == nk - 1)
        def _():
            du_ref[...] = acc[...]

    tok = pl.BlockSpec((FFN_TM, d), lambda i, k: (i, 0))
    pre = pl.BlockSpec((FFN_TM, FFN_HB), lambda i, k: (i, k))
    return _call(body, name=name, args=[dfb, gpre, upre, w_in_t.reshape(2, f, d), w_out],
                 out_shape=(SDS((t, d), F32), SDS((2, f, t), BF16)), grid=(t // FFN_TM, nk),
                 in_specs=[tok, pre, pre, pl.BlockSpec((2, FFN_HB, d), lambda i, k: (0, k, 0)),
                           pl.BlockSpec((FFN_HB, d), lambda i, k: (k, 0))],
                 out_specs=(tok, pl.BlockSpec((2, FFN_HB, FFN_TM), lambda i, k: (0, k, i))),
                 scratch_shapes=[pltpu.VMEM((FFN_TM, d), F32)], job=job)


CH = LANES
PAD = SUBLANES


def _lru_gates(xc, gw_ref, gb_ref, lam_ref, z):
    xcb = xc.astype(BF16)
    r = _sigmoid(jnp.dot(xcb, gw_ref[2 * z], preferred_element_type=F32) + gb_ref[pl.ds(2 * z, 1), :])
    i = _sigmoid(jnp.dot(xcb, gw_ref[2 * z + 1], preferred_element_type=F32) + gb_ref[pl.ds(2 * z + 1, 1), :])
    sp = _softplus(-lam_ref[pl.ds(z, 1), :])
    log_a = (-RG_C * r) * sp
    a = jnp.exp(log_a)
    mult = jnp.sqrt(-_expm1(2.0 * log_a))
    return r, i, sp, a, mult


def _conv(xpad, cw_ref, cb_ref, t):
    xc = cb_ref[...] + cw_ref[pl.ds(0, 1), :] * xpad[pl.ds(PAD - 2, t), :]
    for j in range(1, CONV_WIDTH):
        xc = xc + cw_ref[pl.ds(j, 1), :] * xpad[pl.ds(PAD - 2 + j, t), :]
    return xc


def _fill_padded(pad_ref, value, t):
    pad_ref[pl.ds(0, PAD), :] = jnp.zeros((PAD, CH), F32)
    pad_ref[pl.ds(PAD + t, PAD), :] = jnp.zeros((PAD, CH), F32)
    pad_ref[pl.ds(PAD, t), :] = value


def _scan_pair(t, a_up, b_up, out_up, a_down, b_down, out_down):
    row = lax.broadcasted_iota(jnp.int32, (SUBLANES, CH), 0)

    def compose(a, b, rising):
        for dist in (1, 2, 4):
            shift = dist if rising else SUBLANES - dist
            keep = (row >= dist) if rising else (row < SUBLANES - dist)
            b = jnp.where(keep, b + a * pltpu.roll(b, shift, axis=0), b)
            a = jnp.where(keep, a * pltpu.roll(a, shift, axis=0), a)
        return a, b

    def step(tt, carry):
        hu, hd = carry
        lo = pl.ds(pl.multiple_of(tt * SUBLANES, SUBLANES), SUBLANES)
        hi = pl.ds(pl.multiple_of(t - SUBLANES - tt * SUBLANES, SUBLANES), SUBLANES)
        a, b = compose(a_up[lo, :], b_up[lo, :], True)
        up = b + a * hu
        out_up[lo, :] = up
        a, b = compose(a_down[hi, :], b_down[hi, :], False)
        down = b + a * hd
        out_down[hi, :] = down
        return up[SUBLANES - 1:, :], down[:1, :]

    zero = jnp.zeros((1, CH), F32)
    lax.fori_loop(0, t // SUBLANES, step, (zero, zero), unroll=2)


def _lru_fwd(proj, cw, cb, gw, gb, lam, name, job=None):
    t = proj.shape[0]
    c = cw.shape[1]
    ncb = c // CH

    def body(x_ref, g_ref, cw_ref, cb_ref, gw_ref, gb_ref, lam_ref, ya_ref, hf_ref, hb_ref, xpad, a0, b0, a1, b1):
        _fill_padded(xpad, x_ref[...], t)
        xc = _conv(xpad, cw_ref, cb_ref, t)
        for z, (a_s, b_s) in enumerate(((a0, b0), (a1, b1))):
            _, i, _, a, mult = _lru_gates(xc, gw_ref, gb_ref, lam_ref, z)
            a_s[...] = a
            b_s[...] = mult * (i * xc)
        _scan_pair(t, a0, b0, hf_ref, a1, b1, hb_ref)
        gelu, _ = _gelu_parts(g_ref[...])
        ya_ref[...] = gelu * (hf_ref[...] + hb_ref[...])

    col = lambda off: pl.BlockSpec((t, CH), lambda i: (0, off + i))
    small = lambda rows: pl.BlockSpec((rows, CH), lambda i: (0, i))
    return _call(body, name=name, args=[proj, proj, cw, cb, gw, gb, lam], out_shape=(SDS((t, c), F32),) * 3,
                 grid=(ncb,),
                 in_specs=[col(0), col(ncb), small(CONV_WIDTH), small(1),
                           pl.BlockSpec((4, None, CH, CH), lambda i: (0, i, 0, 0)), small(4), small(2)],
                 out_specs=(col(0),) * 3,
                 scratch_shapes=[pltpu.VMEM((t + 2 * PAD, CH), F32)] + [pltpu.VMEM((t, CH), F32)] * 4, job=job)


def _lru_bwd(proj, cw, cb, gw, gb, lam, hf, hb, dya, name, job=None):
    t = proj.shape[0]
    c = cw.shape[1]
    ncb = c // CH

    def body(x_ref, g_ref, cw_ref, cb_ref, gw_ref, gb_ref, lam_ref, hf_ref, hb_ref, dya_ref,
             dx_ref, dg_ref, dt_ref, dcw_ref, dcb_ref, dgw_ref, dgb_ref, dlam_ref,
             xpad, hpad, dxc, a0, a1, dhs, dh0, dh1):
        _fill_padded(xpad, x_ref[...], t)
        xc = _conv(xpad, cw_ref, cb_ref, t)
        xcb = xc.astype(BF16)
        gates = [_lru_gates(xc, gw_ref, gb_ref, lam_ref, z) for z in range(2)]

        gelu, dgelu = _gelu_parts(g_ref[...])
        dya = dya_ref[...]
        dgate = dya * (hf_ref[...] + hb_ref[...]) * dgelu
        dg_ref[...] = dgate.astype(BF16)
        dt_ref[1] = dgate.T.astype(BF16)
        dhs[...] = dya * gelu

        _fill_padded(hpad, gates[0][3], t)
        a0[...] = hpad[pl.ds(PAD + 1, t), :]
        _fill_padded(hpad, gates[1][3], t)
        a1[...] = hpad[pl.ds(PAD - 1, t), :]
        _scan_pair(t, a1, dhs, dh1, a0, dhs, dh0)

        acc_dxc = jnp.zeros((t, CH), F32)
        for z, (h_ref, dh_ref, shift) in enumerate(((hf_ref, dh0, -1), (hb_ref, dh1, 1))):
            r, i, sp, a, mult = gates[z]
            _fill_padded(hpad, h_ref[...], t)
            h_nb = hpad[pl.ds(PAD + shift, t), :]
            db = dh_ref[...]
            da = db * h_nb
            d_i = db * mult * xc
            acc_dxc = acc_dxc + db * mult * i
            d_mult = db * i * xc
            d_la = da * a - d_mult * (a * a) / mult
            d_r = d_la * (-RG_C * sp)
            dlam_ref[pl.ds(z, 1), :] = (jnp.sum(d_la * (-RG_C * r), axis=0, keepdims=True)
                                        * (-_sigmoid(-lam_ref[pl.ds(z, 1), :])))
            for gate, d_pre in ((0, d_r * r * (1.0 - r)), (1, d_i * i * (1.0 - i))):
                zg = 2 * z + gate
                dgb_ref[pl.ds(zg, 1), :] = jnp.sum(d_pre, axis=0, keepdims=True)
                d_pre_b = d_pre.astype(BF16)
                dgw_ref[zg] = lax.dot_general(xcb, d_pre_b, TN, preferred_element_type=F32)
                acc_dxc = acc_dxc + lax.dot_general(d_pre_b, gw_ref[zg], NT, preferred_element_type=F32)

        dcb_ref[...] = jnp.sum(acc_dxc, axis=0, keepdims=True)
        for j in range(CONV_WIDTH):
            dcw_ref[pl.ds(j, 1), :] = jnp.sum(acc_dxc * xpad[pl.ds(PAD - 2 + j, t), :], axis=0, keepdims=True)
        _fill_padded(dxc, acc_dxc, t)
        dx = cw_ref[pl.ds(0, 1), :] * dxc[pl.ds(PAD + 2, t), :]
        for j in range(1, CONV_WIDTH):
            dx = dx + cw_ref[pl.ds(j, 1), :] * dxc[pl.ds(PAD + 2 - j, t), :]
        dx_ref[...] = dx.astype(BF16)
        dt_ref[0] = dx.T.astype(BF16)

    col = lambda off: pl.BlockSpec((t, CH), lambda i: (0, off + i))
    small = lambda rows: pl.BlockSpec((rows, CH), lambda i: (0, i))
    dense = pl.BlockSpec((4, None, CH, CH), lambda i: (0, i, 0, 0))
    padded = pltpu.VMEM((t + 2 * PAD, CH), F32)
    return _call(
        body, name=name, args=[proj, proj, cw, cb, gw, gb, lam, hf, hb, dya],
        out_shape=(SDS((t, c), BF16), SDS((t, c), BF16), SDS((2, c, t), BF16), SDS((CONV_WIDTH, c), F32),
                   SDS((1, c), F32), SDS((4, ncb, CH, CH), F32), SDS((4, c), F32), SDS((2, c), F32)),
        grid=(ncb,),
        in_specs=[col(0), col(ncb), small(CONV_WIDTH), small(1), dense, small(4), small(2), col(0), col(0), col(0)],
        out_specs=(col(0), col(0), pl.BlockSpec((2, CH, t), lambda i: (0, i, 0)), small(CONV_WIDTH), small(1),
                   dense, small(4), small(2)),
        scratch_shapes=[padded, padded, padded] + [pltpu.VMEM((t, CH), F32)] * 5, job=job)


Q_ROWS = 4
BAND_ROWS = WIN_ROWS + Q_ROWS
BAND_PAIRS = BAND_ROWS // 2
Q_BLOCK = Q_ROWS * GRID_W
BAND = BAND_ROWS * GRID_W
PAIR_W = 2 * GRID_W
N_BOTH = 2 * WIN_ROWS - 2
ENTRY_LEFT_OUT, ENTRY_RIGHT_OUT, ENTRY_OUT = N_BOTH, N_BOTH + 1, N_BOTH + 2
N_ENTRIES = N_BOTH + 3


def _bias_tables(rpb):
    cols = np.arange(GRID_W)
    start = np.clip(cols - WIN_COLS // 2, 0, GRID_W - WIN_COLS)
    valid = (cols[None, :] >= start[:, None]) & (cols[None, :] < start[:, None] + WIN_COLS)
    col_off = np.clip(cols[None, :] - cols[:, None] + WIN_COLS - 1, 0, 2 * WIN_COLS - 2)
    pick_col = jnp.asarray(np.eye(2 * WIN_COLS - 1, dtype=np.float32)[col_off] * valid[..., None])
    by_row = jnp.einsum("hrc,qkc->hrqk", rpb, pick_col, precision=lax.Precision.HIGHEST)
    by_row = jnp.where(jnp.asarray(valid)[None, None], by_row, NEG)
    out = jnp.full_like(by_row[:, :1], NEG)
    first_in, last_in = WIN_ROWS - 1 - WIN_ROWS // 2, 2 * (WIN_ROWS - 1) - WIN_ROWS // 2
    both = jnp.concatenate([by_row[:, :-1], by_row[:, 1:]], axis=-1)
    left_out = jnp.concatenate([out, by_row[:, first_in:first_in + 1]], axis=-1)
    right_out = jnp.concatenate([by_row[:, last_in:last_in + 1], out], axis=-1)
    return jnp.concatenate([both, left_out, right_out, jnp.concatenate([out, out], axis=-1)], axis=1)


def _band_start(m, rows):
    return jnp.clip(Q_ROWS * m - WIN_ROWS // 2, 0, rows - BAND_ROWS)


def _entry(r, key_row, rows):
    w0 = jnp.clip(r - WIN_ROWS // 2, 0, rows - WIN_ROWS)
    left = (key_row >= w0) & (key_row < w0 + WIN_ROWS)
    right = (key_row + 1 >= w0) & (key_row + 1 < w0 + WIN_ROWS)
    return jnp.where(left & right, key_row - r + WIN_ROWS - 1,
                     jnp.where(right, ENTRY_LEFT_OUT, jnp.where(left, ENTRY_RIGHT_OUT, ENTRY_OUT)))


def _transposed_pairs(dst, src_ref):
    for g in range(dst.shape[0]):
        dst[g] = src_ref[pl.ds(g * PAIR_W, PAIR_W), :].T.astype(BF16)


def _band_of(pairs_ref, first_pair, hh):
    heads = pl.ds(hh * HEAD_DIM, HEAD_DIM)
    return jnp.concatenate([pairs_ref[first_pair + g, heads, :] for g in range(BAND_PAIRS)], axis=1)


def _attn_block(qs, kt, tz_ref, hh, m, rows):
    rs = _band_start(m, rows)
    lanes = pl.ds(hh * HEAD_DIM, HEAD_DIM)
    qrows = pl.ds(pl.multiple_of(m * Q_BLOCK, Q_BLOCK), Q_BLOCK)
    band = pl.ds(pl.multiple_of(rs * GRID_W, PAIR_W), BAND)
    entries = [[_entry(Q_ROWS * m + i, rs + 2 * g, rows) for g in range(BAND_PAIRS)] for i in range(Q_ROWS)]
    bias = jnp.concatenate([jnp.concatenate([tz_ref[hh, e] for e in row], axis=1) for row in entries], axis=0)
    q = qs[qrows, lanes]
    s = jnp.dot(q, _band_of(kt, rs // 2, hh), preferred_element_type=F32) * (HEAD_DIM ** -0.5) + bias
    p = jnp.exp(s - jnp.max(s, axis=-1, keepdims=True))
    p = p / jnp.sum(p, axis=-1, keepdims=True)
    return q, p, qrows, band, lanes, entries, rs // 2


def _attn_fwd(proj, tables, width, name, job=None):
    t = proj.shape[0]
    rows = t // GRID_W
    npair = width // LANES
    first = (proj.shape[1] - 3 * width) // LANES

    def body(q_ref, k_ref, v_ref, tz_ref, o_ref, qs, vs, kt):
        qs[...] = q_ref[...].astype(BF16)
        vs[...] = v_ref[...].astype(BF16)
        _transposed_pairs(kt, k_ref)

        def block(m, carry):
            for hh in range(2):
                _, p, qrows, band, lanes, _, _ = _attn_block(qs, kt, tz_ref, hh, m, rows)
                o_ref[qrows, lanes] = jnp.dot(p.astype(BF16), vs[band, lanes], preferred_element_type=F32)
            return carry

        lax.fori_loop(0, rows // Q_ROWS, block, 0)

    col = lambda off: pl.BlockSpec((t, LANES), lambda i: (0, off + i))
    return _call(body, name=name, args=[proj, proj, proj, tables], out_shape=SDS((t, width), F32), grid=(npair,),
                 in_specs=[col(first), col(first + npair), col(first + 2 * npair),
                           pl.BlockSpec((2, N_ENTRIES, GRID_W, PAIR_W), lambda i: (i, 0, 0, 0))],
                 out_specs=col(0),
                 scratch_shapes=[pltpu.VMEM((t, LANES), BF16)] * 2 + [pltpu.VMEM((t // PAIR_W, LANES, PAIR_W), BF16)],
                 job=job)


def _attn_bwd(proj, tables, dyb, name, job=None):
    t, width = dyb.shape
    rows = t // GRID_W
    npair = width // LANES
    first = (proj.shape[1] - 3 * width) // LANES

    def body(q_ref, k_ref, v_ref, tz_ref, do_ref, dq_ref, dk_ref, dv_ref, dt_ref, dtz_ref, dq_s, dk_s, dv_s,
             qs, ks, vs, dos, kt, vt):
        qs[...] = q_ref[...].astype(BF16)
        ks[...] = k_ref[...].astype(BF16)
        vs[...] = v_ref[...].astype(BF16)
        dos[...] = do_ref[...].astype(BF16)
        _transposed_pairs(kt, k_ref)
        _transposed_pairs(vt, v_ref)
        dk_s[...] = jnp.zeros_like(dk_s)
        dv_s[...] = jnp.zeros_like(dv_s)
        dtz_ref[...] = jnp.zeros_like(dtz_ref)

        def block(m, carry):
            for hh in range(2):
                q, p, qrows, band, lanes, entries, first_pair = _attn_block(qs, kt, tz_ref, hh, m, rows)
                do = dos[qrows, lanes]
                dp = jnp.dot(do, _band_of(vt, first_pair, hh), preferred_element_type=F32)
                ds = p * (dp - jnp.sum(dp * p, axis=-1, keepdims=True))
                for i, row in enumerate(entries):
                    for g, e in enumerate(row):
                        dtz_ref[hh, e] += ds[i * GRID_W:(i + 1) * GRID_W, g * PAIR_W:(g + 1) * PAIR_W]
                dsb = (ds * (HEAD_DIM ** -0.5)).astype(BF16)
                dq_s[qrows, lanes] = jnp.dot(dsb, ks[band, lanes], preferred_element_type=F32)
                dk_s[band, lanes] += lax.dot_general(dsb, q, TN, preferred_element_type=F32)
                dv_s[band, lanes] += lax.dot_general(p.astype(BF16), do, TN, preferred_element_type=F32)
            return carry

        lax.fori_loop(0, rows // Q_ROWS, block, 0)
        for n, (src, dst) in enumerate(((dq_s, dq_ref), (dk_s, dk_ref), (dv_s, dv_ref))):
            val = src[...]
            dst[...] = val.astype(BF16)
            dt_ref[n] = val.T.astype(BF16)

    col = lambda off: pl.BlockSpec((t, LANES), lambda i: (0, off + i))
    table = pl.BlockSpec((2, N_ENTRIES, GRID_W, PAIR_W), lambda i: (i, 0, 0, 0))
    pairs = pltpu.VMEM((t // PAIR_W, LANES, PAIR_W), BF16)
    return _call(body, name=name, args=[proj, proj, proj, tables, dyb],
                 out_shape=(SDS((t, width), BF16),) * 3 + (SDS((3, width, t), BF16), SDS(tables.shape, F32)),
                 grid=(npair,),
                 in_specs=[col(first), col(first + npair), col(first + 2 * npair), table, col(0)],
                 out_specs=(col(0), col(0), col(0), pl.BlockSpec((3, LANES, t), lambda i: (0, i, 0)), table),
                 scratch_shapes=[pltpu.VMEM((t, LANES), F32)] * 3 + [pltpu.VMEM((t, LANES), BF16)] * 4 + [pairs, pairs],
                 job=job)


def _adamw_math(w, g, m, v):
    m = ADAM_B1 * m + (1.0 - ADAM_B1) * g
    v = ADAM_B2 * v + (1.0 - ADAM_B2) * (g * g)
    m_hat = m / (1.0 - ADAM_B1 ** ADAM_STEP)
    v_hat = v / (1.0 - ADAM_B2 ** ADAM_STEP)
    delta = -ADAM_LR * (m_hat / (jnp.sqrt(v_hat) + ADAM_EPS) + ADAM_WD * w)
    return delta, m, v


def _sum_partials(p_ref):
    g = p_ref[0].astype(F32)
    for s in range(1, N_CHIP):
        g = g + p_ref[s].astype(F32)
    return g


def _adamw_rows(w, partials, m, v, name):
    rb, n = w.shape
    tr = 64

    def body(w_ref, p_ref, m_ref, v_ref, g_ref, d_ref, nm_ref, nv_ref):
        g = _sum_partials(p_ref)
        g_ref[...] = g
        d_ref[...], nm_ref[...], nv_ref[...] = _adamw_math(w_ref[...], g, m_ref[...], v_ref[...])

    blk = pl.BlockSpec((tr, n), lambda i: (i, 0))
    return _call(body, name=name, args=[w, partials.reshape(N_CHIP, rb, n), m, v], out_shape=(SDS((rb, n), F32),) * 4,
                 grid=(rb // tr,), in_specs=[blk, pl.BlockSpec((N_CHIP, tr, n), lambda i: (0, i, 0)), blk, blk],
                 out_specs=(blk,) * 4)


def _adamw_cols(w, partials, m, v, name):
    d, nb = w.shape
    td = 256

    def body(w_ref, p_ref, m_ref, v_ref, g_ref, d_ref, nm_ref, nv_ref):
        g = _sum_partials(p_ref).T
        g_ref[...] = g
        d_ref[...], nm_ref[...], nv_ref[...] = _adamw_math(w_ref[...], g, m_ref[...], v_ref[...])

    blk = pl.BlockSpec((td, nb), lambda i: (i, 0))
    return _call(body, name=name, args=[w, partials.reshape(N_CHIP, nb, d), m, v], out_shape=(SDS((d, nb), F32),) * 4,
                 grid=(d // td,), in_specs=[blk, pl.BlockSpec((N_CHIP, nb, td), lambda i: (0, 0, i)), blk, blk],
                 out_specs=(blk,) * 4)


def _adamw_small(w, g, m, v, name):
    def body(w_ref, g_ref, m_ref, v_ref, d_ref, nm_ref, nv_ref):
        d_ref[...], nm_ref[...], nv_ref[...] = _adamw_math(w_ref[...], g_ref[...], m_ref[...], v_ref[...])

    return _call(body, name=name, args=[w, g, m, v], out_shape=(SDS(w.shape, F32),) * 3, in_specs=[WHOLE] * 4,
                 out_specs=(WHOLE,) * 3)


TILE = SUBLANES * LANES


def _pack(arrays):
    parts = []
    for a in arrays:
        flat = a.reshape(-1).astype(F32)
        flat = jnp.pad(flat, (0, -flat.size % TILE))
        parts.append(flat.reshape(-1, LANES))
    return jnp.concatenate(parts, axis=0)


def _unpack(pack, like):
    out, row = [], 0
    for a in like:
        n = int(np.prod(a.shape))
        nrows = -(-n // TILE) * SUBLANES
        out.append(pack[row:row + nrows].reshape(-1)[:n].reshape(a.shape))
        row += nrows
    return out


def _dense_gate_blocks(gate_w):
    w = gate_w.reshape(4, -1, 2, HEAD_DIM, HEAD_DIM)
    zero = jnp.zeros_like(w[:, :, 0])
    top = jnp.concatenate([w[:, :, 0], zero], axis=-1)
    bottom = jnp.concatenate([zero, w[:, :, 1]], axis=-1)
    return jnp.concatenate([top, bottom], axis=-2)


def _diag_gate_blocks(dense, shape):
    even = dense[:, :, :HEAD_DIM, :HEAD_DIM]
    odd = dense[:, :, HEAD_DIM:, HEAD_DIM:]
    return jnp.stack([even, odd], axis=2).reshape(shape)


LARGE = ("ffn1_w_in", "ffn1_w_out", "w_in_mix", "w_out_mix", "ffn2_w_in", "ffn2_w_out")
COLUMN_SHARDED = ("ffn1_w_in", "w_in_mix", "ffn2_w_in")
SHARDED_SMALL = ("lru_conv_w", "lru_lambda")
REPLICATED = ("norm_ffn1", "norm_mix", "lru_conv_b", "lru_gate_w", "lru_gate_b", "attn_rpb", "lru_out_norm",
              "attn_out_norm", "norm_ffn2", "norm_final")
SMALL_ORDER = REPLICATED + SHARDED_SMALL
WEIGHTS = ("norm_ffn1", "ffn1_w_in", "ffn1_w_out", "norm_mix", "w_in_mix", "lru_conv_w", "lru_conv_b", "lru_gate_w",
           "lru_gate_b", "lru_lambda", "attn_rpb", "lru_out_norm", "attn_out_norm", "w_out_mix", "norm_ffn2",
           "ffn2_w_in", "ffn2_w_out", "norm_final")


PARTS = {("gather", "w_in_mix"): 4, ("gather", "ffn2_w_out"): 4, ("gather", "ffn2_w_in"): 8,
         ("to_chips", "ffn2_w_in"): 8, ("to_chips", "w_in_mix"): 4, ("to_chips", "ffn1_w_out"): 4}
CARRIES = {
    "gather_ffn1_in": [(("gather", "ffn1_w_in"), 1), (("gather", "small"), 1)],
    "ffn1_hidden": [(("gather", "ffn1_w_out"), 1), (("gather", "w_in_mix"), 1)],
    "ffn1_out": [(("gather", "w_in_mix"), 3)],
    "mix_in_proj": [(("gather", "w_out_mix"), 1), (("gather", "ffn2_w_in"), 2)],
    "lru_fwd": [(("gather", "ffn2_w_in"), 3)],
    "attn_fwd": [(("gather", "ffn2_w_in"), 3)],
    "mix_out_proj": [(("gather", "ffn2_w_out"), 1)],
    "ffn2_hidden": [(("gather", "ffn2_w_out"), 3)],
    "to_sibling_ffn2_out": [(("to_sibling", "ffn2_w_out"), 1)],
    "ffn2_bwd": [(("to_chips", "ffn2_w_out"), 1)],
    "norm_ffn2_bwd": [(("to_sibling", "ffn2_w_in"), 1)],
    "mix_out_grad": [(("to_chips", "ffn2_w_in"), 1)],
    "mix_out_bwd": [(("to_chips", "ffn2_w_in"), 1)],
    "attn_bwd": [(("to_chips", "ffn2_w_in"), 4)],
    "lru_bwd": [(("to_chips", "ffn2_w_in"), 2), (("to_sibling", "w_out_mix"), 1)],
    "mix_in_grad_attn": [(("to_chips", "w_out_mix"), 1)],
    "mix_in_bwd": [(("to_sibling", "w_in_mix"), 1)],
    "ffn1_out_grad": [(("to_chips", "w_in_mix"), 2)],
    "ffn1_bwd": [(("to_chips", "w_in_mix"), 2), (("to_sibling", "ffn1_w_out"), 1), (("gather", "small_grads"), 1)],
    "ffn1_in_grad_gate": [(("to_chips", "ffn1_w_out"), 2)],
    "ffn1_in_grad_up": [(("to_chips", "ffn1_w_out"), 2)],
    "to_sibling_ffn1_in": [(("to_sibling", "ffn1_w_in"), 1)],
    "to_chips_ffn1": [(("to_chips", "ffn1_w_in"), 1), (("gather", "late_grads"), 1)],
}


class _Transfer:
    def __init__(self, kind, src, dest, block_rows, parts):
        self.kind, self.src, self.dest = kind, src, dest
        self.ranges, self.taken = _split(block_rows, parts), 0

    def take(self, count):
        lo, hi = self.ranges[self.taken][0], self.ranges[self.taken + count - 1][1]
        self.taken += count
        return _Piece(self.kind, self.src, self.dest, lo, hi)


class _Traffic:
    def __init__(self):
        self.transfers = {}

    def open(self, kind, name, src):
        if kind == "gather":
            dest, rows = _gathered(src), src.shape[0]
        elif kind == "to_sibling":
            dest, rows = SDS((src.shape[0] // 2, src.shape[1]), src.dtype), src.shape[0] // N_DEV
        else:
            dest, rows = SDS(src.shape, src.dtype), src.shape[0] // N_CHIP
        self.transfers[kind, name] = _Transfer(kind, src, dest, rows, PARTS.get((kind, name), 1))

    def _job(self, host):
        moved = [self.transfers[key] for key, _ in CARRIES[host]]
        return moved, _Job([tr.take(count) for tr, (_, count) in zip(moved, CARRIES[host])])

    def carry(self, host, fn, *args, **kw):
        if host not in CARRIES:
            return fn(*args, name=host, **kw)
        moved, job = self._job(host)
        res, landed = fn(*args, name=host, job=job, **kw)
        for tr, arr in zip(moved, landed):
            tr.dest = arr
        return res

    def alone(self, host):
        moved, job = self._job(host)
        for tr, arr in zip(moved, _run_job(job, host)):
            tr.dest = arr

    def result(self, kind, name):
        tr = self.transfers.pop((kind, name))
        assert tr.taken == len(tr.ranges), (kind, name)
        return tr.dest


def _forward_backward(x, target, shards, sharded_small, s):
    c = s["lru_conv_b"].shape[1]
    width = s["attn_out_norm"].shape[1]
    t = x.shape[0]
    traffic = _Traffic()
    carry = traffic.carry
    weight = lambda n: traffic.result("gather", n)

    for n in LARGE:
        traffic.open("gather", n, shards[n])
    traffic.open("gather", "small", sharded_small)
    traffic.alone("gather_ffn1_in")
    full_small = weight("small").reshape(N_DEV, SUBLANES, c // N_DEV)
    conv_w = full_small[:, :CONV_WIDTH].transpose(1, 0, 2).reshape(CONV_WIDTH, c)
    lam = full_small[:, CONV_WIDTH:CONV_WIDTH + 2].transpose(1, 0, 2).reshape(2, c)
    w = {"ffn1_w_in": weight("ffn1_w_in")}
    ffn_out = dict(nt=False, out_dtype=F32, tm=512, tn=512, scale=0.5)
    u1 = _rmsnorm_fwd(x, s["norm_ffn1"], "norm_ffn1")
    g1, up1, hid1, hid1_t = carry("ffn1_hidden", _ffn_hidden, u1, w["ffn1_w_in"])
    w["ffn1_w_out"] = weight("ffn1_w_out")
    h1 = carry("ffn1_out", _mm, hid1, w["ffn1_w_out"], residual=x, **ffn_out)
    w["w_in_mix"] = weight("w_in_mix")
    u2 = _rmsnorm_fwd(h1, s["norm_mix"], "norm_mix")
    proj = carry("mix_in_proj", _mm, u2, w["w_in_mix"], nt=True, out_dtype=F32, tm=512, tn=512)
    w["w_out_mix"] = weight("w_out_mix")
    gw = _dense_gate_blocks(s["lru_gate_w"]).astype(BF16)
    gb = s["lru_gate_b"].reshape(4, c)
    tables, tables_vjp = jax.vjp(_bias_tables, s["attn_rpb"])
    ya, hf, hb = carry("lru_fwd", _lru_fwd, proj, conv_w, s["lru_conv_b"], gw, gb, lam)
    yb = carry("attn_fwd", _attn_fwd, proj, tables, width)
    y, yt = _mixnorm_fwd(ya, yb, s["lru_out_norm"], s["attn_out_norm"], "mix_norm")
    h2 = carry("mix_out_proj", _mm, y, w["w_out_mix"], nt=False, out_dtype=F32, tm=512, tn=512, residual=h1)
    u3 = _rmsnorm_fwd(h2, s["norm_ffn2"], "norm_ffn2")
    w["ffn2_w_in"] = weight("ffn2_w_in")
    g2, up2, hid2, hid2_t = carry("ffn2_hidden", _ffn_hidden, u3, w["ffn2_w_in"])
    w["ffn2_w_out"] = weight("ffn2_w_out")
    h3 = carry("ffn2_out", _mm, hid2, w["ffn2_w_out"], residual=h2, **ffn_out)
    dh3, df2, loss_part, d_norm_final = _final_loss(h3, s["norm_final"], target, "final_loss")

    grads = {}
    grad_of = dict(nt=False, out_dtype=BF16, tm=512, tn=1024)

    def reduce_in_chip(n):
        traffic.open("to_sibling", n, grads[n])

    def reduce_over_chips(n):
        traffic.open("to_chips", n, _pair_sum(grads[n], traffic.result("to_sibling", n), "pair_sum_" + n))

    f = hid2_t.shape[0]
    grads["ffn2_w_out"] = carry("ffn2_out_grad", _mm, hid2_t, df2, **grad_of)
    reduce_in_chip("ffn2_w_out")
    traffic.alone("to_sibling_ffn2_out")
    reduce_over_chips("ffn2_w_out")
    du3, da2_t = carry("ffn2_bwd", _ffn_bwd, df2, g2, up2, w["ffn2_w_in"], w["ffn2_w_out"])
    grads["ffn2_w_in"] = carry("ffn2_in_grad", _mm, da2_t.reshape(2 * f, t), u3, **grad_of)
    reduce_in_chip("ffn2_w_in")
    dh2, dh2b, d_norm_ffn2 = carry("norm_ffn2_bwd", _rmsnorm_bwd, du3, h2, s["norm_ffn2"], dh3, 1.0)
    reduce_over_chips("ffn2_w_in")
    grads["w_out_mix"] = carry("mix_out_grad", _mm, yt, dh2b, **grad_of)
    reduce_in_chip("w_out_mix")
    dy = carry("mix_out_bwd", _mm, dh2b, w["w_out_mix"], nt=True, out_dtype=F32, tm=512, tn=512)
    dya, dyb, d_lru_out_norm, d_attn_out_norm = _mixnorm_bwd(dy, ya, yb, s["lru_out_norm"], s["attn_out_norm"],
                                                             "mix_norm_bwd")
    dq, dk, dv, dqkv_t, d_tables = carry("attn_bwd", _attn_bwd, proj, tables, dyb)
    dx_lru, dg_lru, dxg_t, d_conv_w, d_conv_b, d_gw, d_gb, d_lam = carry(
        "lru_bwd", _lru_bwd, proj, conv_w, s["lru_conv_b"], gw, gb, lam, hf, hb, dya)
    reduce_over_chips("w_out_mix")
    rows_of = 2 * c + 3 * width
    lru_rows = carry("mix_in_grad_lru", _mm, dxg_t.reshape(2 * c, t), u2, out_rows=rows_of, **grad_of)
    grads["w_in_mix"] = carry("mix_in_grad_attn", _mm, dqkv_t.reshape(3 * width, t), u2, out_rows=rows_of,
                              row_offset=2 * c, into=lru_rows, **grad_of)
    reduce_in_chip("w_in_mix")
    du2 = carry("mix_in_bwd", _mm, [dx_lru, dg_lru, dq, dk, dv], w["w_in_mix"], nt=False, out_dtype=F32, tm=512,
                tn=512)
    reduce_over_chips("w_in_mix")
    dh1, df1, d_norm_mix = carry("norm_mix_bwd", _rmsnorm_bwd, du2, h1, s["norm_mix"], dh2, 0.5)

    by_device = lambda a: a.reshape(a.shape[0], N_DEV, -1).transpose(1, 0, 2)
    small = {
        "norm_mix": d_norm_mix, "lru_conv_b": d_conv_b, "lru_gate_w": _diag_gate_blocks(d_gw, s["lru_gate_w"].shape),
        "lru_gate_b": d_gb.reshape(s["lru_gate_b"].shape), "attn_rpb": tables_vjp(d_tables)[0],
        "lru_out_norm": d_lru_out_norm, "attn_out_norm": d_attn_out_norm, "norm_ffn2": d_norm_ffn2,
        "norm_final": d_norm_final, "lru_conv_w": by_device(d_conv_w), "lru_lambda": by_device(d_lam),
    }
    early = [small[n] for n in SMALL_ORDER[1:]]
    traffic.open("gather", "small_grads", _pack(early))

    grads["ffn1_w_out"] = carry("ffn1_out_grad", _mm, hid1_t, df1, **grad_of)
    reduce_in_chip("ffn1_w_out")
    du1, da1_t = carry("ffn1_bwd", _ffn_bwd, df1, g1, up1, w["ffn1_w_in"], w["ffn1_w_out"])
    reduce_over_chips("ffn1_w_out")
    gate_rows = carry("ffn1_in_grad_gate", _mm, da1_t, u1, lead=0, out_rows=2 * f, **grad_of)
    grads["ffn1_w_in"] = carry("ffn1_in_grad_up", _mm, da1_t, u1, lead=1, out_rows=2 * f, row_offset=f,
                               into=gate_rows, **grad_of)
    reduce_in_chip("ffn1_w_in")
    grad_x, _, d_norm_ffn1 = _rmsnorm_bwd(du1, x, s["norm_ffn1"], dh1, 1.0, "norm_ffn1_bwd")
    traffic.open("gather", "late_grads", _pack([d_norm_ffn1]))
    traffic.alone("to_sibling_ffn1_in")
    reduce_over_chips("ffn1_w_in")
    traffic.alone("to_chips_ffn1")
    partials = {n: traffic.result("to_chips", n) for n in LARGE}
    reduced = (_unpack(_sum_devices(traffic.result("gather", "late_grads"), "sum_late_grads"), [d_norm_ffn1])
               + _unpack(_sum_devices(traffic.result("gather", "small_grads"), "sum_small_grads"), early))
    assert not traffic.transfers, list(traffic.transfers)
    return loss_part[0, 0], grad_x, partials, dict(zip(SMALL_ORDER, reduced))


def _step(x, loss_target, p, m, v):
    me = 4 * lax.axis_index("x") + 2 * lax.axis_index("y") + lax.axis_index("c")

    shards = {n: (_cast_transposed if n in COLUMN_SHARDED else _cast_rows)(p[n], "cast_" + n) for n in LARGE}
    sharded_small = (jnp.pad(p["lru_conv_w"], ((0, SUBLANES - CONV_WIDTH), (0, 0)))
                     + jnp.pad(p["lru_lambda"], ((CONV_WIDTH, SUBLANES - CONV_WIDTH - 2), (0, 0))))
    s = {n: p[n] if n in ("lru_gate_w", "lru_gate_b", "attn_rpb") else p[n].reshape(1, -1) for n in REPLICATED}

    loss_part, grad_x, partials, small = _forward_backward(x, loss_target, shards, sharded_small, s)
    loss = lax.psum(loss_part, ("x", "y", "c"))

    out = {}
    for n in LARGE:
        update = _adamw_cols if n in COLUMN_SHARDED else _adamw_rows
        out[n] = update(p[n], partials[n], m[n], v[n], "adamw_" + n)

    g_small = {n: lax.dynamic_index_in_dim(g, me, axis=0, keepdims=False) if n in SHARDED_SMALL else g
               for n, g in small.items()}
    names = SMALL_ORDER
    like = [p[n] for n in names]
    pack_of = lambda d: _pack([d[n].reshape(p[n].shape) for n in names])
    upd = _adamw_small(pack_of(p), pack_of(g_small), pack_of(m), pack_of(v), "adamw_small")
    for n, d_, m_, v_ in zip(names, *[_unpack(u, like) for u in upd]):
        out[n] = (g_small[n].reshape(p[n].shape), d_, m_, v_)
    return loss, grad_x, out


def kernel(x, norm_ffn1, ffn1_w_in, ffn1_w_out, norm_mix, w_in_mix, lru_conv_w, lru_conv_b, lru_gate_w, lru_gate_b, lru_lambda, attn_rpb, lru_out_norm, attn_out_norm, w_out_mix, norm_ffn2, ffn2_w_in, ffn2_w_out, norm_final, loss_target, m_norm_ffn1, m_ffn1_w_in, m_ffn1_w_out, m_norm_mix, m_w_in_mix, m_lru_conv_w, m_lru_conv_b, m_lru_gate_w, m_lru_gate_b, m_lru_lambda, m_attn_rpb, m_lru_out_norm, m_attn_out_norm, m_w_out_mix, m_norm_ffn2, m_ffn2_w_in, m_ffn2_w_out, m_norm_final, v_norm_ffn1, v_ffn1_w_in, v_ffn1_w_out, v_norm_mix, v_w_in_mix, v_lru_conv_w, v_lru_conv_b, v_lru_gate_w, v_lru_gate_b, v_lru_lambda, v_attn_rpb, v_lru_out_norm, v_attn_out_norm, v_w_out_mix, v_norm_ffn2, v_ffn2_w_in, v_ffn2_w_out, v_norm_final):
    given = dict(locals())
    drop_layer = lambda n, a: a if n == "norm_final" else a[0]
    p = {n: drop_layer(n, given[n]) for n in WEIGHTS}
    m = {n: drop_layer(n, given["m_" + n]) for n in WEIGHTS}
    v = {n: drop_layer(n, given["v_" + n]) for n in WEIGHTS}
    loss, grad_x, out = _step(x[0], loss_target[0], p, m, v)
    shaped = lambda n, a: a.reshape(given[n].shape)
    return (loss, grad_x[None], *[shaped(n, out[n][k]) for k in range(4) for n in WEIGHTS])
```

```python
import math

import numpy as np
import jax
import jax.numpy as jnp
from jax import lax
from jax.experimental import pallas as pl
from jax.experimental.pallas import tpu as pltpu

F32 = jnp.float32
BF16 = jnp.bfloat16
SDS = jax.ShapeDtypeStruct

N_DEV = 8
N_CHIP = 4
NORM_EPS = 1e-6
RG_C = 8.0
CONV_WIDTH = 4
HEAD_DIM = 64
GRID_W = 64
WIN_ROWS = 8
WIN_COLS = 16
NEG = -1e30

ADAM_LR = 0.001
ADAM_B1 = 0.9
ADAM_B2 = 0.999
ADAM_EPS = 1e-08
ADAM_WD = 0.01
ADAM_STEP = 10

LANES = 128
SUBLANES = 8
VMEM_LIMIT = 56 * 1024 * 1024

NT = (((1,), (1,)), ((), ()))
TN = (((0,), (0,)), ((), ()))
ANY = pl.BlockSpec(memory_space=pl.ANY)
WHOLE = pl.BlockSpec(memory_space=pltpu.VMEM)
MESH = pl.DeviceIdType.MESH


def _sigmoid(x):
    return 1.0 / (1.0 + jnp.exp(-x))


def _gelu_parts(x):
    c = math.sqrt(2.0 / math.pi)
    t = jnp.tanh(c * (x + 0.044715 * (x * x * x)))
    gelu = 0.5 * x * (1.0 + t)
    dgelu = 0.5 * (1.0 + t) + 0.5 * x * (1.0 - t * t) * (c * (1.0 + 3.0 * 0.044715 * (x * x)))
    return gelu, dgelu


def _expm1(x):
    poly = x * (1.0 + x * (1.0 / 2) * (1.0 + x * (1.0 / 3) * (1.0 + x * (1.0 / 4) * (1.0 + x * (1.0 / 5) * (1.0 + x * (1.0 / 6))))))
    return jnp.where(jnp.abs(x) < 0.25, poly, jnp.exp(x) - 1.0)


def _softplus(x):
    return jnp.maximum(x, 0.0) + jnp.log1p(jnp.exp(-jnp.abs(x)))


class _Piece:
    N_REMOTE = {"gather": 7, "to_sibling": N_CHIP, "to_chips": 3}
    N_LOCAL = {"gather": 1, "to_sibling": 0, "to_chips": 1}

    def __init__(self, kind, src, dest, lo, hi):
        self.kind, self.src, self.dest, self.lo, self.hi = kind, src, dest, lo, hi


RELAY_AT = 60


class _Job:
    def __init__(self, pieces):
        self.pieces = list(pieces)
        self.ins = [p.src for p in self.pieces]
        self.out_shapes = [SDS(p.dest.shape, p.dest.dtype) for p in self.pieces]
        self.aliased = [i for i, p in enumerate(self.pieces) if not isinstance(p.dest, SDS)]
        self.n_remote = sum(_Piece.N_REMOTE[p.kind] for p in self.pieces)
        self.n_local = max(sum(_Piece.N_LOCAL[p.kind] for p in self.pieces), 1)

    def _each(self, step, ins, outs, send_sems, recv_sems, local_sems):
        remote = local = 0
        for p, src, dst in zip(self.pieces, ins, outs):
            _EXCHANGES[p.kind](step, p, src, dst, send_sems, recv_sems, local_sems, remote, local)
            remote += _Piece.N_REMOTE[p.kind]
            local += _Piece.N_LOCAL[p.kind]

    def start(self, *refs):
        self._each("start", *refs)

    def relay(self, *refs):
        self._each("relay", *refs)

    def finish(self, *refs):
        self._each("finish", *refs)


def _call(body, *, name, args, out_shape, in_specs, out_specs, grid=(), scratch_shapes=(), aliases=None, job=None):
    single = not isinstance(out_shape, (tuple, list))
    out_shape = (out_shape,) if single else tuple(out_shape)
    out_specs = (out_specs,) if single else tuple(out_specs)
    aliases = dict(aliases or {})
    params = pltpu.CompilerParams(dimension_semantics=("arbitrary",) * len(grid) if grid else None,
                                  vmem_limit_bytes=VMEM_LIMIT)
    if job is None:
        res = pl.pallas_call(body, out_shape=out_shape, grid=grid, in_specs=list(in_specs), out_specs=out_specs,
                             scratch_shapes=list(scratch_shapes), input_output_aliases=aliases, name=name,
                             compiler_params=params)(*args)
        return res[0] if single else res

    n_in, n_out, n_scr = len(args), len(out_shape), len(scratch_shapes)
    j_in, j_out, j_alias = len(job.ins), len(job.out_shapes), len(job.aliased)

    def hosted(*refs):
        ins, refs = refs[:n_in], refs[n_in:]
        j_ins, refs = refs[:j_in], refs[j_in + j_alias:]
        outs, refs = refs[:n_out], refs[n_out:]
        j_outs, refs = refs[:j_out], refs[j_out:]
        scr, sems = refs[:n_scr], refs[n_scr:]
        if grid:
            step = 0
            for axis, size in enumerate(grid):
                step = step * size + pl.program_id(axis)
            steps = math.prod(grid)
            pl.when(step == 0)(lambda: job.start(j_ins, j_outs, *sems))
            body(*ins, *outs, *scr)
            pl.when(step == min(RELAY_AT * steps // 100, steps - 1))(lambda: job.relay(j_ins, j_outs, *sems))
            pl.when(step == steps - 1)(lambda: job.finish(j_ins, j_outs, *sems))
        else:
            job.start(j_ins, j_outs, *sems)
            body(*ins, *outs, *scr)
            job.relay(j_ins, j_outs, *sems)
            job.finish(j_ins, j_outs, *sems)

    res = pl.pallas_call(
        hosted, out_shape=out_shape + tuple(job.out_shapes), grid=grid,
        in_specs=list(in_specs) + [ANY] * (j_in + j_alias), out_specs=out_specs + (ANY,) * j_out,
        scratch_shapes=list(scratch_shapes) + [pltpu.SemaphoreType.DMA((job.n_remote,)),
                                               pltpu.SemaphoreType.DMA((job.n_remote,)),
                                               pltpu.SemaphoreType.DMA((job.n_local,))],
        input_output_aliases={**aliases, **{n_in + j_in + k: n_out + i for k, i in enumerate(job.aliased)}},
        name=name, compiler_params=params)(*args, *job.ins, *[job.pieces[i].dest for i in job.aliased])
    own, carried = res[:n_out], res[n_out:]
    return (own[0] if single else own), carried


def _run_job(job, name):
    return _call(lambda: None, name=name, args=[], out_shape=(), in_specs=[], out_specs=(), job=job)[1]


def _position():
    return lax.axis_index("x"), lax.axis_index("y"), lax.axis_index("c")


def _flat(px, py, pc):
    return 4 * px + 2 * py + pc


def _gather_exchange(step, p, src, dst, send_sems, recv_sems, local_sems, r0, l0):
    x, y, c = _position()
    me, sibling = (x, y, c), (x, y, 1 - c)
    along_x, along_y, diagonal = (1 - x, y), (x, 1 - y), (1 - x, 1 - y)
    south = c == 0
    passed_on = (jnp.where(south, 1 - x, x), jnp.where(south, y, 1 - y))
    passed_to = (jnp.where(south, x, 1 - x), jnp.where(south, 1 - y, y))
    rb, n_rows = p.src.shape[0], p.hi - p.lo
    mine = src.at[pl.ds(p.lo, n_rows), :]

    def rows(block):
        return dst.at[pl.ds(_flat(*block) * rb + p.lo, n_rows), :]

    def copy(k, block, to, own=False):
        return pltpu.make_async_remote_copy(
            src_ref=mine if own else rows(block), dst_ref=rows(block),
            send_sem=send_sems.at[r0 + k], recv_sem=recv_sems.at[r0 + k], device_id=to, device_id_type=MESH)

    local = pltpu.make_async_copy(mine, rows(me), local_sems.at[l0])
    if step == "start":
        local.start()
        copy(0, me, sibling, own=True).start()
        copy(1, me, (*along_x, c), own=True).start()
        copy(2, me, (*along_y, c), own=True).start()
    elif step == "relay":
        copy(1, (*along_x, c), me).wait_recv()
        copy(2, (*along_y, c), me).wait_recv()
        copy(3, (*passed_on, c), (*passed_to, c)).start()
        copy(4, (*along_x, c), sibling).start()
        copy(5, (*along_y, c), sibling).start()
    else:
        copy(3, (*diagonal, c), me).wait_recv()
        copy(6, (*diagonal, c), sibling).start()
        copy(0, sibling, me).wait_recv()
        copy(4, (*along_x, 1 - c), me).wait_recv()
        copy(5, (*along_y, 1 - c), me).wait_recv()
        copy(6, (*diagonal, 1 - c), me).wait_recv()
        copy(0, me, sibling, own=True).wait_send()
        copy(1, me, (*along_x, c), own=True).wait_send()
        copy(2, me, (*along_y, c), own=True).wait_send()
        copy(3, (*passed_on, c), (*passed_to, c)).wait_send()
        copy(4, (*along_x, c), sibling).wait_send()
        copy(5, (*along_y, c), sibling).wait_send()
        copy(6, (*diagonal, c), sibling).wait_send()
        local.wait()


def _sibling_exchange(step, p, src, dst, send_sems, recv_sems, local_sems, r0, l0):
    x, y, c = _position()
    rb, n_rows = p.src.shape[0] // N_DEV, p.hi - p.lo
    for q in range(N_CHIP):
        copy = pltpu.make_async_remote_copy(
            src_ref=src.at[pl.ds((2 * q + 1 - c) * rb + p.lo, n_rows), :],
            dst_ref=dst.at[pl.ds(q * rb + p.lo, n_rows), :],
            send_sem=send_sems.at[r0 + q], recv_sem=recv_sems.at[r0 + q], device_id=(x, y, 1 - c), device_id_type=MESH)
        if step == "start":
            copy.start()
        elif step == "finish":
            copy.wait()


CHIP_FLIPS = [(1, 0), (0, 1), (1, 1)]


def _chips_exchange(step, p, src, dst, send_sems, recv_sems, local_sems, r0, l0):
    x, y, c = _position()
    rb, n_rows = p.src.shape[0] // N_CHIP, p.hi - p.lo

    def slot(ref, px, py):
        return ref.at[pl.ds((2 * px + py) * rb + p.lo, n_rows), :]

    def copy(k, landing=False):
        px = 1 - x if CHIP_FLIPS[k][0] else x
        py = 1 - y if CHIP_FLIPS[k][1] else y
        return pltpu.make_async_remote_copy(
            src_ref=slot(dst, px, py) if landing else slot(src, px, py),
            dst_ref=slot(dst, px, py) if landing else slot(dst, x, y),
            send_sem=send_sems.at[r0 + k], recv_sem=recv_sems.at[r0 + k], device_id=(px, py, c), device_id_type=MESH)

    local = pltpu.make_async_copy(slot(src, x, y), slot(dst, x, y), local_sems.at[l0])
    if step == "start":
        local.start()
        for k in range(3):
            copy(k).start()
    elif step == "finish":
        for k in range(3):
            copy(k, landing=True).wait_recv()
        for k in range(3):
            copy(k).wait_send()
        local.wait()


_EXCHANGES = {"gather": _gather_exchange, "to_sibling": _sibling_exchange, "to_chips": _chips_exchange}


def _gathered(shard):
    return SDS((N_DEV * shard.shape[0], shard.shape[1]), shard.dtype)


def _split(rows, parts):
    cuts = [rows * k // parts // 16 * 16 for k in range(parts)] + [rows]
    return list(zip(cuts[:-1], cuts[1:]))


def _pair_sum(g, from_sibling, name):
    rb, n = g.shape[0] // N_DEV, g.shape[1]
    tr = rb if rb * n * 2 <= 3 * 1024 * 1024 else rb // 2
    core = lax.axis_index("c").astype(jnp.int32).reshape(1)

    def body(c_ref, g_ref, r_ref, o_ref):
        o_ref[...] = (g_ref[...].astype(F32) + r_ref[...].astype(F32)).astype(BF16)

    grid_spec = pltpu.PrefetchScalarGridSpec(
        num_scalar_prefetch=1, grid=(N_CHIP, rb // tr),
        in_specs=[pl.BlockSpec((None, None, tr, n), lambda q, i, c_ref: (q, c_ref[0], i, 0)),
                  pl.BlockSpec((None, tr, n), lambda q, i, c_ref: (q, i, 0))],
        out_specs=pl.BlockSpec((None, tr, n), lambda q, i, c_ref: (q, i, 0)))
    out = pl.pallas_call(
        body, grid_spec=grid_spec, out_shape=SDS((N_CHIP, rb, n), BF16), name=name,
        compiler_params=pltpu.CompilerParams(dimension_semantics=("arbitrary",) * 2, vmem_limit_bytes=VMEM_LIMIT))(
            core, g.reshape(N_CHIP, 2, rb, n), from_sibling.reshape(N_CHIP, rb, n))
    return out.reshape(N_CHIP * rb, n)


def _sum_devices(gathered, name):
    r = gathered.shape[0] // N_DEV

    def body(g_ref, o_ref):
        acc = g_ref[0]
        for s in range(1, N_DEV):
            acc = acc + g_ref[s]
        o_ref[...] = acc

    return _call(body, name=name, args=[gathered.reshape(N_DEV, r, LANES)], out_shape=SDS((r, LANES), F32),
                 in_specs=[WHOLE], out_specs=WHOLE)


def _cast_rows(w, name):
    def body(w_ref, o_ref):
        o_ref[...] = w_ref[...].astype(BF16)

    return _call(body, name=name, args=[w], out_shape=SDS(w.shape, BF16), in_specs=[WHOLE], out_specs=WHOLE)


def _cast_transposed(w, name):
    d, n = w.shape
    td = 512

    def body(w_ref, o_ref):
        o_ref[...] = w_ref[...].T.astype(BF16)

    return _call(body, name=name, args=[w], out_shape=SDS((n, d), BF16), grid=(d // td,),
                 in_specs=[pl.BlockSpec((td, n), lambda i: (i, 0))], out_specs=pl.BlockSpec((n, td), lambda i: (0, i)))


ROW_TILE = 256


def _rmsnorm_fwd(h, gain, name):
    t, d = h.shape

    def body(h_ref, g_ref, u_ref):
        x = h_ref[...]
        u_ref[...] = (x * lax.rsqrt(jnp.mean(x * x, axis=-1, keepdims=True) + NORM_EPS) * g_ref[...]).astype(BF16)

    row = pl.BlockSpec((ROW_TILE, d), lambda i: (i, 0))
    return _call(body, name=name, args=[h, gain], out_shape=SDS((t, d), BF16), grid=(t // ROW_TILE,),
                 in_specs=[row, pl.BlockSpec((1, d), lambda i: (0, 0))], out_specs=row)


def _rms_bwd_math(x, gain, dy):
    rstd = lax.rsqrt(jnp.mean(x * x, axis=-1, keepdims=True) + NORM_EPS)
    xhat = x * rstd
    dxh = dy * gain
    dx = rstd * (dxh - xhat * jnp.mean(dxh * xhat, axis=-1, keepdims=True))
    return dx, jnp.sum(dy * xhat, axis=0, keepdims=True)


def _rmsnorm_bwd(du, h, gain, resid, bf_scale, name, job=None):
    t, d = h.shape

    def body(du_ref, h_ref, g_ref, r_ref, dh_ref, dhb_ref, dg_ref):
        @pl.when(pl.program_id(0) == 0)
        def _():
            dg_ref[...] = jnp.zeros_like(dg_ref)

        dx, dg = _rms_bwd_math(h_ref[...], g_ref[...], du_ref[...])
        dh = r_ref[...] + dx
        dh_ref[...] = dh
        dhb_ref[...] = (bf_scale * dh).astype(BF16)
        dg_ref[...] += dg

    row = pl.BlockSpec((ROW_TILE, d), lambda i: (i, 0))
    vec = pl.BlockSpec((1, d), lambda i: (0, 0))
    return _call(body, name=name, args=[du, h, gain, resid],
                 out_shape=(SDS((t, d), F32), SDS((t, d), BF16), SDS((1, d), F32)), grid=(t // ROW_TILE,),
                 in_specs=[row, row, vec, row], out_specs=(row, row, vec), job=job)


def _final_loss(h, gain, target, name):
    t, d = h.shape

    def body(h_ref, g_ref, t_ref, dh_ref, dhb_ref, loss_ref, dg_ref):
        @pl.when(pl.program_id(0) == 0)
        def _():
            dg_ref[...] = jnp.zeros_like(dg_ref)
            loss_ref[...] = jnp.zeros_like(loss_ref)

        x = h_ref[...]
        gain = g_ref[...]
        out = x * lax.rsqrt(jnp.mean(x * x, axis=-1, keepdims=True) + NORM_EPS) * gain
        err = out - t_ref[...]
        loss_ref[...] += 0.5 * jnp.sum(jnp.mean(err * err, axis=-1, keepdims=True), axis=0, keepdims=True)
        dx, dg = _rms_bwd_math(x, gain, err * (1.0 / d))
        dh_ref[...] = dx
        dhb_ref[...] = (0.5 * dx).astype(BF16)
        dg_ref[...] += dg

    row = pl.BlockSpec((ROW_TILE, d), lambda i: (i, 0))
    vec = pl.BlockSpec((1, d), lambda i: (0, 0))
    one = pl.BlockSpec((SUBLANES, LANES), lambda i: (0, 0))
    return _call(body, name=name, args=[h, gain, target],
                 out_shape=(SDS((t, d), F32), SDS((t, d), BF16), SDS((SUBLANES, LANES), F32), SDS((1, d), F32)),
                 grid=(t // ROW_TILE,), in_specs=[row, vec, row], out_specs=(row, row, one, vec))


def _mixnorm_fwd(ya, yb, ga, gb, name):
    t, c = ya.shape

    def body(ya_ref, yb_ref, ga_ref, gb_ref, y_ref, yt_ref):
        for k, (src, g_ref) in enumerate(((ya_ref, ga_ref), (yb_ref, gb_ref))):
            x = src[...]
            u = x * lax.rsqrt(jnp.mean(x * x, axis=-1, keepdims=True) + NORM_EPS) * g_ref[...]
            y_ref[:, k * c:(k + 1) * c] = u.astype(BF16)
            yt_ref[k * c:(k + 1) * c, :] = u.T.astype(BF16)

    row = pl.BlockSpec((ROW_TILE, c), lambda i: (i, 0))
    vec = pl.BlockSpec((1, c), lambda i: (0, 0))
    return _call(body, name=name, args=[ya, yb, ga, gb],
                 out_shape=(SDS((t, 2 * c), BF16), SDS((2 * c, t), BF16)), grid=(t // ROW_TILE,),
                 in_specs=[row, row, vec, vec],
                 out_specs=(pl.BlockSpec((ROW_TILE, 2 * c), lambda i: (i, 0)),
                            pl.BlockSpec((2 * c, ROW_TILE), lambda i: (0, i))))


def _mixnorm_bwd(dy, ya, yb, ga, gb, name):
    t, c = ya.shape

    def body(dy_ref, ya_ref, yb_ref, ga_ref, gb_ref, dya_ref, dyb_ref, dga_ref, dgb_ref):
        @pl.when(pl.program_id(0) == 0)
        def _():
            dga_ref[...] = jnp.zeros_like(dga_ref)
            dgb_ref[...] = jnp.zeros_like(dgb_ref)

        dxa, dga = _rms_bwd_math(ya_ref[...], ga_ref[...], dy_ref[:, :c])
        dxb, dgb = _rms_bwd_math(yb_ref[...], gb_ref[...], dy_ref[:, c:])
        dya_ref[...] = dxa
        dyb_ref[...] = dxb
        dga_ref[...] += dga
        dgb_ref[...] += dgb

    row = pl.BlockSpec((ROW_TILE, c), lambda i: (i, 0))
    vec = pl.BlockSpec((1, c), lambda i: (0, 0))
    return _call(body, name=name, args=[dy, ya, yb, ga, gb],
                 out_shape=(SDS((t, c), F32), SDS((t, c), F32), SDS((1, c), F32), SDS((1, c), F32)),
                 grid=(t // ROW_TILE,),
                 in_specs=[pl.BlockSpec((ROW_TILE, 2 * c), lambda i: (i, 0)), row, row, vec, vec],
                 out_specs=(row, row, vec, vec))


def _tile(n, want):
    return max(t for t in range(LANES, min(n, want) + 1, LANES) if n % t == 0)


def _mm(a, b, *, nt, out_dtype, tm, tn, name, residual=None, scale=None, lead=None, out_rows=None, row_offset=0,
        into=None, job=None):
    parts = list(a) if isinstance(a, (list, tuple)) else [a]
    m = parts[0].shape[-2]
    widths = [p.shape[-1] for p in parts]
    k = sum(widths)
    n = b.shape[0] if nt else b.shape[1]
    tm, tn = _tile(math.gcd(m, row_offset), tm), _tile(n, tn)
    out_rows = m if out_rows is None else out_rows

    def body(*refs):
        a_refs, b_ref, rest = refs[:len(parts)], refs[len(parts)], refs[len(parts) + 1:]
        o_ref = rest[-1]
        out, at = None, 0
        for a_ref, width in zip(a_refs, widths):
            av = a_ref[...].astype(BF16)
            if nt:
                term = lax.dot_general(av, b_ref[:, at:at + width].astype(BF16), NT, preferred_element_type=F32)
            else:
                term = jnp.dot(av, b_ref[at:at + width, :].astype(BF16), preferred_element_type=F32)
            out = term if out is None else out + term
            at += width
        if residual is not None:
            out = rest[0][...] + (out if scale is None else scale * out)
        o_ref[...] = out.astype(out_dtype)

    a_specs =([pl.BlockSpec((tm, width), lambda i, j: (i, 0)) for width in widths] if lead is None
               else [pl.BlockSpec((None, tm, k), lambda i, j: (lead, i, 0))])
    in_specs = a_specs + [pl.BlockSpec((tn, k), lambda i, j: (j, 0)) if nt else pl.BlockSpec((k, tn), lambda i, j: (0, j))]
    args, aliases = parts + [b], {}
    if residual is not None:
        in_specs.append(pl.BlockSpec((tm, tn), lambda i, j: (i, j)))
        args.append(residual)
    if into is not None:
        in_specs.append(ANY)
        aliases[len(args)] = 0
        args.append(into)
    return _call(body, name=name, args=args, out_shape=SDS((out_rows, n), out_dtype), grid=(m // tm, n // tn),
                 in_specs=in_specs, out_specs=pl.BlockSpec((tm, tn), lambda i, j: (row_offset // tm + i, j)),
                 aliases=aliases, job=job)


FFN_TM = 512
FFN_HB = 512


def _ffn_hidden(u, w_in_t, name, job=None):
    t, d = u.shape
    f = w_in_t.shape[0] // 2

    def body(u_ref, w_ref, g_ref, up_ref, hid_ref, hid_t_ref):
        uu = u_ref[...]
        g = lax.dot_general(uu, w_ref[0], NT, preferred_element_type=F32)
        up = lax.dot_general(uu, w_ref[1], NT, preferred_element_type=F32)
        g_ref[...] = g.astype(BF16)
        up_ref[...] = up.astype(BF16)
        hid = (g * _sigmoid(g)) * up
        hid_ref[...] = hid.astype(BF16)
        hid_t_ref[...] = hid.T.astype(BF16)

    pre = pl.BlockSpec((FFN_TM, FFN_HB), lambda i, k: (i, k))
    return _call(body, name=name, args=[u, w_in_t.reshape(2, f, d)],
                 out_shape=(SDS((t, f), BF16), SDS((t, f), BF16), SDS((t, f), BF16), SDS((f, t), BF16)),
                 grid=(t // FFN_TM, f // FFN_HB),
                 in_specs=[pl.BlockSpec((FFN_TM, d), lambda i, k: (i, 0)),
                           pl.BlockSpec((2, FFN_HB, d), lambda i, k: (0, k, 0))],
                 out_specs=(pre, pre, pre, pl.BlockSpec((FFN_HB, FFN_TM), lambda i, k: (k, i))), job=job)


def _ffn_bwd(dfb, gpre, upre, w_in_t, w_out, name, job=None):
    t, d = dfb.shape
    f = w_out.shape[0]
    nk = f // FFN_HB

    def body(df_ref, g_ref, up_ref, w_ref, wo_ref, du_ref, da_t_ref, acc):
        k = pl.program_id(1)

        @pl.when(k == 0)
        def _():
            acc[...] = jnp.zeros_like(acc)

        dhid = lax.dot_general(df_ref[...], wo_ref[...], NT, preferred_element_type=F32)
        g, up = g_ref[...].astype(F32), up_ref[...].astype(F32)
        sig = _sigmoid(g)
        silu = g * sig
        dup = dhid * silu
        dg = dhid * up * (sig * (1.0 + g * (1.0 - sig)))
        da_t_ref[0] = dg.T.astype(BF16)
        da_t_ref[1] = dup.T.astype(BF16)
        acc[...] += (jnp.dot(dg.astype(BF16), w_ref[0], preferred_element_type=F32)
                     + jnp.dot(dup.astype(BF16), w_ref[1], preferred_element_type=F32))

        @pl.when(k == nk - 1)
        def _():
            du_ref[...] = acc[...]

    tok = pl.BlockSpec((FFN_TM, d), lambda i, k: (i, 0))
    pre = pl.BlockSpec((FFN_TM, FFN_HB), lambda i, k: (i, k))
    return _call(body, name=name, args=[dfb, gpre, upre, w_in_t.reshape(2, f, d), w_out],
                 out_shape=(SDS((t, d), F32), SDS((2, f, t), BF16)), grid=(t // FFN_TM, nk),
                 in_specs=[tok, pre, pre, pl.BlockSpec((2, FFN_HB, d), lambda i, k: (0, k, 0)),
                           pl.BlockSpec((FFN_HB, d), lambda i, k: (k, 0))],
                 out_specs=(tok, pl.BlockSpec((2, FFN_HB, FFN_TM), lambda i, k: (0, k, i))),
                 scratch_shapes=[pltpu.VMEM((FFN_TM, d), F32)], job=job)


CH = LANES
PAD = SUBLANES


def _lru_gates(xc, gw_ref, gb_ref, lam_ref, z):
    xcb = xc.astype(BF16)
    r = _sigmoid(jnp.dot(xcb, gw_ref[2 * z], preferred_element_type=F32) + gb_ref[pl.ds(2 * z, 1), :])
    i = _sigmoid(jnp.dot(xcb, gw_ref[2 * z + 1], preferred_element_type=F32) + gb_ref[pl.ds(2 * z + 1, 1), :])
    sp = _softplus(-lam_ref[pl.ds(z, 1), :])
    log_a = (-RG_C * r) * sp
    a = jnp.exp(log_a)
    mult = jnp.sqrt(-_expm1(2.0 * log_a))
    return r, i, sp, a, mult


def _conv(xpad, cw_ref, cb_ref, t):
    xc = cb_ref[...] + cw_ref[pl.ds(0, 1), :] * xpad[pl.ds(PAD - 2, t), :]
    for j in range(1, CONV_WIDTH):
        xc = xc + cw_ref[pl.ds(j, 1), :] * xpad[pl.ds(PAD - 2 + j, t), :]
    return xc


def _fill_padded(pad_ref, value, t):
    pad_ref[pl.ds(0, PAD), :] = jnp.zeros((PAD, CH), F32)
    pad_ref[pl.ds(PAD + t, PAD), :] = jnp.zeros((PAD, CH), F32)
    pad_ref[pl.ds(PAD, t), :] = value


def _scan_pair(t, a_up, b_up, out_up, a_down, b_down, out_down):
    row = lax.broadcasted_iota(jnp.int32, (SUBLANES, CH), 0)

    def compose(a, b, rising):
        for dist in (1, 2, 4):
            shift = dist if rising else SUBLANES - dist
            keep = (row >= dist) if rising else (row < SUBLANES - dist)
            b = jnp.where(keep, b + a * pltpu.roll(b, shift, axis=0), b)
            a = jnp.where(keep, a * pltpu.roll(a, shift, axis=0), a)
        return a, b

    def step(tt, carry):
        hu, hd = carry
        lo = pl.ds(pl.multiple_of(tt * SUBLANES, SUBLANES), SUBLANES)
        hi = pl.ds(pl.multiple_of(t - SUBLANES - tt * SUBLANES, SUBLANES), SUBLANES)
        a, b = compose(a_up[lo, :], b_up[lo, :], True)
        up = b + a * hu
        out_up[lo, :] = up
        a, b = compose(a_down[hi, :], b_down[hi, :], False)
        down = b + a * hd
        out_down[hi, :] = down
        return up[SUBLANES - 1:, :], down[:1, :]

    zero = jnp.zeros((1, CH), F32)
    lax.fori_loop(0, t // SUBLANES, step, (zero, zero), unroll=2)


def _lru_fwd(proj, cw, cb, gw, gb, lam, name, job=None):
    t = proj.shape[0]
    c = cw.shape[1]
    ncb = c // CH

    def body(x_ref, g_ref, cw_ref, cb_ref, gw_ref, gb_ref, lam_ref, ya_ref, hf_ref, hb_ref, xpad, a0, b0, a1, b1):
        _fill_padded(xpad, x_ref[...], t)
        xc = _conv(xpad, cw_ref, cb_ref, t)
        for z, (a_s, b_s) in enumerate(((a0, b0), (a1, b1))):
            _, i, _, a, mult = _lru_gates(xc, gw_ref, gb_ref, lam_ref, z)
            a_s[...] = a
            b_s[...] = mult * (i * xc)
        _scan_pair(t, a0, b0, hf_ref, a1, b1, hb_ref)
        gelu, _ = _gelu_parts(g_ref[...])
        ya_ref[...] = gelu * (hf_ref[...] + hb_ref[...])

    col = lambda off: pl.BlockSpec((t, CH), lambda i: (0, off + i))
    small = lambda rows: pl.BlockSpec((rows, CH), lambda i: (0, i))
    return _call(body, name=name, args=[proj, proj, cw, cb, gw, gb, lam], out_shape=(SDS((t, c), F32),) * 3,
                 grid=(ncb,),
                 in_specs=[col(0), col(ncb), small(CONV_WIDTH), small(1),
                           pl.BlockSpec((4, None, CH, CH), lambda i: (0, i, 0, 0)), small(4), small(2)],
                 out_specs=(col(0),) * 3,
                 scratch_shapes=[pltpu.VMEM((t + 2 * PAD, CH), F32)] + [pltpu.VMEM((t, CH), F32)] * 4, job=job)


def _lru_bwd(proj, cw, cb, gw, gb, lam, hf, hb, dya, name, job=None):
    t = proj.shape[0]
    c = cw.shape[1]
    ncb = c // CH

    def body(x_ref, g_ref, cw_ref, cb_ref, gw_ref, gb_ref, lam_ref, hf_ref, hb_ref, dya_ref,
             dx_ref, dg_ref, dt_ref, dcw_ref, dcb_ref, dgw_ref, dgb_ref, dlam_ref,
             xpad, hpad, dxc, a0, a1, dhs, dh0, dh1):
        _fill_padded(xpad, x_ref[...], t)
        xc = _conv(xpad, cw_ref, cb_ref, t)
        xcb = xc.astype(BF16)
        gates = [_lru_gates(xc, gw_ref, gb_ref, lam_ref, z) for z in range(2)]

        gelu, dgelu = _gelu_parts(g_ref[...])
        dya = dya_ref[...]
        dgate = dya * (hf_ref[...] + hb_ref[...]) * dgelu
        dg_ref[...] = dgate.astype(BF16)
        dt_ref[1] = dgate.T.astype(BF16)
        dhs[...] = dya * gelu

        _fill_padded(hpad, gates[0][3], t)
        a0[...] = hpad[pl.ds(PAD + 1, t), :]
        _fill_padded(hpad, gates[1][3], t)
        a1[...] = hpad[pl.ds(PAD - 1, t), :]
        _scan_pair(t, a1, dhs, dh1, a0, dhs, dh0)

        acc_dxc = jnp.zeros((t, CH), F32)
        for z, (h_ref, dh_ref, shift) in enumerate(((hf_ref, dh0, -1), (hb_ref, dh1, 1))):
            r, i, sp, a, mult = gates[z]
            _fill_padded(hpad, h_ref[...], t)
            h_nb = hpad[pl.ds(PAD + shift, t), :]
            db = dh_ref[...]
            da = db * h_nb
            d_i = db * mult * xc
            acc_dxc = acc_dxc + db * mult * i
            d_mult = db * i * xc
            d_la = da * a - d_mult * (a * a) / mult
            d_r = d_la * (-RG_C * sp)
            dlam_ref[pl.ds(z, 1), :] = (jnp.sum(d_la * (-RG_C * r), axis=0, keepdims=True)
                                        * (-_sigmoid(-lam_ref[pl.ds(z, 1), :])))
            for gate, d_pre in ((0, d_r * r * (1.0 - r)), (1, d_i * i * (1.0 - i))):
                zg = 2 * z + gate
                dgb_ref[pl.ds(zg, 1), :] = jnp.sum(d_pre, axis=0, keepdims=True)
                d_pre_b = d_pre.astype(BF16)
                dgw_ref[zg] = lax.dot_general(xcb, d_pre_b, TN, preferred_element_type=F32)
                acc_dxc = acc_dxc + lax.dot_general(d_pre_b, gw_ref[zg], NT, preferred_element_type=F32)

        dcb_ref[...] = jnp.sum(acc_dxc, axis=0, keepdims=True)
        for j in range(CONV_WIDTH):
            dcw_ref[pl.ds(j, 1), :] = jnp.sum(acc_dxc * xpad[pl.ds(PAD - 2 + j, t), :], axis=0, keepdims=True)
        _fill_padded(dxc, acc_dxc, t)
        dx = cw_ref[pl.ds(0, 1), :] * dxc[pl.ds(PAD + 2, t), :]
        for j in range(1, CONV_WIDTH):
            dx = dx + cw_ref[pl.ds(j, 1), :] * dxc[pl.ds(PAD + 2 - j, t), :]
        dx_ref[...] = dx.astype(BF16)
        dt_ref[0] = dx.T.astype(BF16)

    col = lambda off: pl.BlockSpec((t, CH), lambda i: (0, off + i))
    small = lambda rows: pl.BlockSpec((rows, CH), lambda i: (0, i))
    dense = pl.BlockSpec((4, None, CH, CH), lambda i: (0, i, 0, 0))
    padded = pltpu.VMEM((t + 2 * PAD, CH), F32)
    return _call(
        body, name=name, args=[proj, proj, cw, cb, gw, gb, lam, hf, hb, dya],
        out_shape=(SDS((t, c), BF16), SDS((t, c), BF16), SDS((2, c, t), BF16), SDS((CONV_WIDTH, c), F32),
                   SDS((1, c), F32), SDS((4, ncb, CH, CH), F32), SDS((4, c), F32), SDS((2, c), F32)),
        grid=(ncb,),
        in_specs=[col(0), col(ncb), small(CONV_WIDTH), small(1), dense, small(4), small(2), col(0), col(0), col(0)],
        out_specs=(col(0), col(0), pl.BlockSpec((2, CH, t), lambda i: (0, i, 0)), small(CONV_WIDTH), small(1),
                   dense, small(4), small(2)),
        scratch_shapes=[padded, padded, padded] + [pltpu.VMEM((t, CH), F32)] * 5, job=job)


Q_ROWS = 4
BAND_ROWS = WIN_ROWS + Q_ROWS
BAND_PAIRS = BAND_ROWS // 2
Q_BLOCK = Q_ROWS * GRID_W
BAND = BAND_ROWS * GRID_W
PAIR_W = 2 * GRID_W
N_BOTH = 2 * WIN_ROWS - 2
ENTRY_LEFT_OUT, ENTRY_RIGHT_OUT, ENTRY_OUT = N_BOTH, N_BOTH + 1, N_BOTH + 2
N_ENTRIES = N_BOTH + 3


def _bias_tables(rpb):
    cols = np.arange(GRID_W)
    start = np.clip(cols - WIN_COLS // 2, 0, GRID_W - WIN_COLS)
    valid = (cols[None, :] >= start[:, None]) & (cols[None, :] < start[:, None] + WIN_COLS)
    col_off = np.clip(cols[None, :] - cols[:, None] + WIN_COLS - 1, 0, 2 * WIN_COLS - 2)
    pick_col = jnp.asarray(np.eye(2 * WIN_COLS - 1, dtype=np.float32)[col_off] * valid[..., None])
    by_row = jnp.einsum("hrc,qkc->hrqk", rpb, pick_col, precision=lax.Precision.HIGHEST)
    by_row = jnp.where(jnp.asarray(valid)[None, None], by_row, NEG)
    out = jnp.full_like(by_row[:, :1], NEG)
    first_in, last_in = WIN_ROWS - 1 - WIN_ROWS // 2, 2 * (WIN_ROWS - 1) - WIN_ROWS // 2
    both = jnp.concatenate([by_row[:, :-1], by_row[:, 1:]], axis=-1)
    left_out = jnp.concatenate([out, by_row[:, first_in:first_in + 1]], axis=-1)
    right_out = jnp.concatenate([by_row[:, last_in:last_in + 1], out], axis=-1)
    return jnp.concatenate([both, left_out, right_out, jnp.concatenate([out, out], axis=-1)], axis=1)


def _band_start(m, rows):
    return jnp.clip(Q_ROWS * m - WIN_ROWS // 2, 0, rows - BAND_ROWS)


def _entry(r, key_row, rows):
    w0 = jnp.clip(r - WIN_ROWS // 2, 0, rows - WIN_ROWS)
    left = (key_row >= w0) & (key_row < w0 + WIN_ROWS)
    right = (key_row + 1 >= w0) & (key_row + 1 < w0 + WIN_ROWS)
    return jnp.where(left & right, key_row - r + WIN_ROWS - 1,
                     jnp.where(right, ENTRY_LEFT_OUT, jnp.where(left, ENTRY_RIGHT_OUT, ENTRY_OUT)))


def _transposed_pairs(dst, src_ref):
    for g in range(dst.shape[0]):
        dst[g] = src_ref[pl.ds(g * PAIR_W, PAIR_W), :].T.astype(BF16)


def _band_of(pairs_ref, first_pair, hh):
    heads = pl.ds(hh * HEAD_DIM, HEAD_DIM)
    return jnp.concatenate([pairs_ref[first_pair + g, heads, :] for g in range(BAND_PAIRS)], axis=1)


def _attn_block(qs, kt, tz_ref, hh, m, rows):
    rs = _band_start(m, rows)
    lanes = pl.ds(hh * HEAD_DIM, HEAD_DIM)
    qrows = pl.ds(pl.multiple_of(m * Q_BLOCK, Q_BLOCK), Q_BLOCK)
    band = pl.ds(pl.multiple_of(rs * GRID_W, PAIR_W), BAND)
    entries = [[_entry(Q_ROWS * m + i, rs + 2 * g, rows) for g in range(BAND_PAIRS)] for i in range(Q_ROWS)]
    bias = jnp.concatenate([jnp.concatenate([tz_ref[hh, e] for e in row], axis=1) for row in entries], axis=0)
    q = qs[qrows, lanes]
    s = jnp.dot(q, _band_of(kt, rs // 2, hh), preferred_element_type=F32) * (HEAD_DIM ** -0.5) + bias
    p = jnp.exp(s - jnp.max(s, axis=-1, keepdims=True))
    p = p / jnp.sum(p, axis=-1, keepdims=True)
    return q, p, qrows, band, lanes, entries, rs // 2


def _attn_fwd(proj, tables, width, name, job=None):
    t = proj.shape[0]
    rows = t // GRID_W
    npair = width // LANES
    first = (proj.shape[1] - 3 * width) // LANES

    def body(q_ref, k_ref, v_ref, tz_ref, o_ref, qs, vs, kt):
        qs[...] = q_ref[...].astype(BF16)
        vs[...] = v_ref[...].astype(BF16)
        _transposed_pairs(kt, k_ref)

        def block(m, carry):
            for hh in range(2):
                _, p, qrows, band, lanes, _, _ = _attn_block(qs, kt, tz_ref, hh, m, rows)
                o_ref[qrows, lanes] = jnp.dot(p.astype(BF16), vs[band, lanes], preferred_element_type=F32)
            return carry

        lax.fori_loop(0, rows // Q_ROWS, block, 0)

    col = lambda off: pl.BlockSpec((t, LANES), lambda i: (0, off + i))
    return _call(body, name=name, args=[proj, proj, proj, tables], out_shape=SDS((t, width), F32), grid=(npair,),
                 in_specs=[col(first), col(first + npair), col(first + 2 * npair),
                           pl.BlockSpec((2, N_ENTRIES, GRID_W, PAIR_W), lambda i: (i, 0, 0, 0))],
                 out_specs=col(0),
                 scratch_shapes=[pltpu.VMEM((t, LANES), BF16)] * 2 + [pltpu.VMEM((t // PAIR_W, LANES, PAIR_W), BF16)],
                 job=job)


def _attn_bwd(proj, tables, dyb, name, job=None):
    t, width = dyb.shape
    rows = t // GRID_W
    npair = width // LANES
    first = (proj.shape[1] - 3 * width) // LANES

    def body(q_ref, k_ref, v_ref, tz_ref, do_ref, dq_ref, dk_ref, dv_ref, dt_ref, dtz_ref, dq_s, dk_s, dv_s,
             qs, ks, vs, dos, kt, vt):
        qs[...] = q_ref[...].astype(BF16)
        ks[...] = k_ref[...].astype(BF16)
        vs[...] = v_ref[...].astype(BF16)
        dos[...] = do_ref[...].astype(BF16)
        _transposed_pairs(kt, k_ref)
        _transposed_pairs(vt, v_ref)
        dk_s[...] = jnp.zeros_like(dk_s)
        dv_s[...] = jnp.zeros_like(dv_s)
        dtz_ref[...] = jnp.zeros_like(dtz_ref)

        def block(m, carry):
            for hh in range(2):
                q, p, qrows, band, lanes, entries, first_pair = _attn_block(qs, kt, tz_ref, hh, m, rows)
                do = dos[qrows, lanes]
                dp = jnp.dot(do, _band_of(vt, first_pair, hh), preferred_element_type=F32)
                ds = p * (dp - jnp.sum(dp * p, axis=-1, keepdims=True))
                for i, row in enumerate(entries):
                    for g, e in enumerate(row):
                        dtz_ref[hh, e] += ds[i * GRID_W:(i + 1) * GRID_W, g * PAIR_W:(g + 1) * PAIR_W]
                dsb = (ds * (HEAD_DIM ** -0.5)).astype(BF16)
                dq_s[qrows, lanes] = jnp.dot(dsb, ks[band, lanes], preferred_element_type=F32)
                dk_s[band, lanes] += lax.dot_general(dsb, q, TN, preferred_element_type=F32)
                dv_s[band, lanes] += lax.dot_general(p.astype(BF16), do, TN, preferred_element_type=F32)
            return carry

        lax.fori_loop(0, rows // Q_ROWS, block, 0)
        for n, (src, dst) in enumerate(((dq_s, dq_ref), (dk_s, dk_ref), (dv_s, dv_ref))):
            val = src[...]
            dst[...] = val.astype(BF16)
            dt_ref[n] = val.T.astype(BF16)

    col = lambda off: pl.BlockSpec((t, LANES), lambda i: (0, off + i))
    table = pl.BlockSpec((2, N_ENTRIES, GRID_W, PAIR_W), lambda i: (i, 0, 0, 0))
    pairs = pltpu.VMEM((t // PAIR_W, LANES, PAIR_W), BF16)
    return _call(body, name=name, args=[proj, proj, proj, tables, dyb],
                 out_shape=(SDS((t, width), BF16),) * 3 + (SDS((3, width, t), BF16), SDS(tables.shape, F32)),
                 grid=(npair,),
                 in_specs=[col(first), col(first + npair), col(first + 2 * npair), table, col(0)],
                 out_specs=(col(0), col(0), col(0), pl.BlockSpec((3, LANES, t), lambda i: (0, i, 0)), table),
                 scratch_shapes=[pltpu.VMEM((t, LANES), F32)] * 3 + [pltpu.VMEM((t, LANES), BF16)] * 4 + [pairs, pairs],
                 job=job)


def _adamw_math(w, g, m, v):
    m = ADAM_B1 * m + (1.0 - ADAM_B1) * g
    v = ADAM_B2 * v + (1.0 - ADAM_B2) * (g * g)
    m_hat = m / (1.0 - ADAM_B1 ** ADAM_STEP)
    v_hat = v / (1.0 - ADAM_B2 ** ADAM_STEP)
    delta = -ADAM_LR * (m_hat / (jnp.sqrt(v_hat) + ADAM_EPS) + ADAM_WD * w)
    return delta, m, v


def _sum_partials(p_ref):
    g = p_ref[0].astype(F32)
    for s in range(1, N_CHIP):
        g = g + p_ref[s].astype(F32)
    return g


def _adamw_rows(w, partials, m, v, name):
    rb, n = w.shape
    tr = 64

    def body(w_ref, p_ref, m_ref, v_ref, g_ref, d_ref, nm_ref, nv_ref):
        g = _sum_partials(p_ref)
        g_ref[...] = g
        d_ref[...], nm_ref[...], nv_ref[...] = _adamw_math(w_ref[...], g, m_ref[...], v_ref[...])

    blk = pl.BlockSpec((tr, n), lambda i: (i, 0))
    return _call(body, name=name, args=[w, partials.reshape(N_CHIP, rb, n), m, v], out_shape=(SDS((rb, n), F32),) * 4,
                 grid=(rb // tr,), in_specs=[blk, pl.BlockSpec((N_CHIP, tr, n), lambda i: (0, i, 0)), blk, blk],
                 out_specs=(blk,) * 4)


def _adamw_cols(w, partials, m, v, name):
    d, nb = w.shape
    td = 256

    def body(w_ref, p_ref, m_ref, v_ref, g_ref, d_ref, nm_ref, nv_ref):
        g = _sum_partials(p_ref).T
        g_ref[...] = g
        d_ref[...], nm_ref[...], nv_ref[...] = _adamw_math(w_ref[...], g, m_ref[...], v_ref[...])

    blk = pl.BlockSpec((td, nb), lambda i: (i, 0))
    return _call(body, name=name, args=[w, partials.reshape(N_CHIP, nb, d), m, v], out_shape=(SDS((d, nb), F32),) * 4,
                 grid=(d // td,), in_specs=[blk, pl.BlockSpec((N_CHIP, nb, td), lambda i: (0, 0, i)), blk, blk],
                 out_specs=(blk,) * 4)


def _adamw_small(w, g, m, v, name):
    def body(w_ref, g_ref, m_ref, v_ref, d_ref, nm_ref, nv_ref):
        d_ref[...], nm_ref[...], nv_ref[...] = _adamw_math(w_ref[...], g_ref[...], m_ref[...], v_ref[...])

    return _call(body, name=name, args=[w, g, m, v], out_shape=(SDS(w.shape, F32),) * 3, in_specs=[WHOLE] * 4,
                 out_specs=(WHOLE,) * 3)


TILE = SUBLANES * LANES


def _pack(arrays):
    parts = []
    for a in arrays:
        flat = a.reshape(-1).astype(F32)
        flat = jnp.pad(flat, (0, -flat.size % TILE))
        parts.append(flat.reshape(-1, LANES))
    return jnp.concatenate(parts, axis=0)


def _unpack(pack, like):
    out, row = [], 0
    for a in like:
        n = int(np.prod(a.shape))
        nrows = -(-n // TILE) * SUBLANES
        out.append(pack[row:row + nrows].reshape(-1)[:n].reshape(a.shape))
        row += nrows
    return out


def _dense_gate_blocks(gate_w):
    w = gate_w.reshape(4, -1, 2, HEAD_DIM, HEAD_DIM)
    zero = jnp.zeros_like(w[:, :, 0])
    top = jnp.concatenate([w[:, :, 0], zero], axis=-1)
    bottom = jnp.concatenate([zero, w[:, :, 1]], axis=-1)
    return jnp.concatenate([top, bottom], axis=-2)


def _diag_gate_blocks(dense, shape):
    even = dense[:, :, :HEAD_DIM, :HEAD_DIM]
    odd = dense[:, :, HEAD_DIM:, HEAD_DIM:]
    return jnp.stack([even, odd], axis=2).reshape(shape)


LARGE = ("ffn1_w_in", "ffn1_w_out", "w_in_mix", "w_out_mix", "ffn2_w_in", "ffn2_w_out")
COLUMN_SHARDED = ("ffn1_w_in", "w_in_mix", "ffn2_w_in")
SHARDED_SMALL = ("lru_conv_w", "lru_lambda")
REPLICATED = ("norm_ffn1", "norm_mix", "lru_conv_b", "lru_gate_w", "lru_gate_b", "attn_rpb", "lru_out_norm",
              "attn_out_norm", "norm_ffn2", "norm_final")
SMALL_ORDER = REPLICATED + SHARDED_SMALL
WEIGHTS = ("norm_ffn1", "ffn1_w_in", "ffn1_w_out", "norm_mix", "w_in_mix", "lru_conv_w", "lru_conv_b", "lru_gate_w",
           "lru_gate_b", "lru_lambda", "attn_rpb", "lru_out_norm", "attn_out_norm", "w_out_mix", "norm_ffn2",
           "ffn2_w_in", "ffn2_w_out", "norm_final")


PARTS = {("gather", "w_in_mix"): 4, ("gather", "ffn2_w_in"): 8,
         ("to_chips", "ffn2_w_in"): 8, ("to_chips", "w_in_mix"): 4, ("to_chips", "ffn1_w_out"): 4}
CARRIES = {
    "gather_ffn1_in": [(("gather", "ffn1_w_in"), 1), (("gather", "small"), 1)],
    "ffn1_hidden": [(("gather", "ffn1_w_out"), 1), (("gather", "w_in_mix"), 1)],
    "ffn1_out": [(("gather", "w_in_mix"), 3)],
    "mix_in_proj": [(("gather", "w_out_mix"), 1), (("gather", "ffn2_w_in"), 1)],
    "lru_fwd": [(("gather", "ffn2_w_in"), 3)],
    "attn_fwd": [(("gather", "ffn2_w_in"), 3)],
    "mix_out_proj": [(("gather", "ffn2_w_in"), 1)],
    "ffn2_hidden": [(("gather", "ffn2_w_out"), 1)],
    "to_sibling_ffn2_out": [(("to_sibling", "ffn2_w_out"), 1)],
    "ffn2_bwd": [(("to_chips", "ffn2_w_out"), 1)],
    "norm_ffn2_bwd": [(("to_sibling", "ffn2_w_in"), 1)],
    "mix_out_grad": [(("to_chips", "ffn2_w_in"), 1)],
    "mix_out_bwd": [(("to_chips", "ffn2_w_in"), 1)],
    "attn_bwd": [(("to_chips", "ffn2_w_in"), 4)],
    "lru_bwd": [(("to_chips", "ffn2_w_in"), 2), (("to_sibling", "w_out_mix"), 1)],
    "mix_in_bwd": [(("to_chips", "w_out_mix"), 1), (("to_sibling", "w_in_mix"), 1)],
    "ffn1_out_grad": [(("to_chips", "w_in_mix"), 2)],
    "ffn1_bwd": [(("to_chips", "w_in_mix"), 2), (("to_sibling", "ffn1_w_out"), 1), (("gather", "small_grads"), 1)],
    "ffn1_in_grad_gate": [(("to_chips", "ffn1_w_out"), 2)],
    "ffn1_in_grad_up": [(("to_chips", "ffn1_w_out"), 2)],
    "norm_ffn1_bwd": [(("to_sibling", "ffn1_w_in"), 1)],
    "to_chips_ffn1": [(("to_chips", "ffn1_w_in"), 1), (("gather", "late_grads"), 1)],
}


class _Transfer:
    def __init__(self, kind, src, dest, block_rows, parts):
        self.kind, self.src, self.dest = kind, src, dest
        self.ranges, self.taken = _split(block_rows, parts), 0

    def take(self, count):
        lo, hi = self.ranges[self.taken][0], self.ranges[self.taken + count - 1][1]
        self.taken += count
        return _Piece(self.kind, self.src, self.dest, lo, hi)


class _Traffic:
    def __init__(self):
        self.transfers = {}

    def open(self, kind, name, src):
        if kind == "gather":
            dest, rows = _gathered(src), src.shape[0]
        elif kind == "to_sibling":
            dest, rows = SDS((src.shape[0] // 2, src.shape[1]), src.dtype), src.shape[0] // N_DEV
        else:
            dest, rows = SDS(src.shape, src.dtype), src.shape[0] // N_CHIP
        self.transfers[kind, name] = _Transfer(kind, src, dest, rows, PARTS.get((kind, name), 1))

    def _job(self, host):
        moved = [self.transfers[key] for key, _ in CARRIES[host]]
        return moved, _Job([tr.take(count) for tr, (_, count) in zip(moved, CARRIES[host])])

    def carry(self, host, fn, *args, **kw):
        if host not in CARRIES:
            return fn(*args, name=host, **kw)
        moved, job = self._job(host)
        res, landed = fn(*args, name=host, job=job, **kw)
        for tr, arr in zip(moved, landed):
            tr.dest = arr
        return res

    def alone(self, host):
        moved, job = self._job(host)
        for tr, arr in zip(moved, _run_job(job, host)):
            tr.dest = arr

    def result(self, kind, name):
        tr = self.transfers.pop((kind, name))
        assert tr.taken == len(tr.ranges), (kind, name)
        return tr.dest


def _forward_backward(x, target, shards, sharded_small, s):
    c = s["lru_conv_b"].shape[1]
    width = s["attn_out_norm"].shape[1]
    t = x.shape[0]
    traffic = _Traffic()
    carry = traffic.carry
    weight = lambda n: traffic.result("gather", n)

    for n in LARGE:
        traffic.open("gather", n, shards[n])
    traffic.open("gather", "small", sharded_small)
    traffic.alone("gather_ffn1_in")
    full_small = weight("small").reshape(N_DEV, SUBLANES, c // N_DEV)
    conv_w = full_small[:, :CONV_WIDTH].transpose(1, 0, 2).reshape(CONV_WIDTH, c)
    lam = full_small[:, CONV_WIDTH:CONV_WIDTH + 2].transpose(1, 0, 2).reshape(2, c)
    w = {"ffn1_w_in": weight("ffn1_w_in")}
    ffn_out = dict(nt=False, out_dtype=F32, tm=512, tn=512, scale=0.5)
    u1 = _rmsnorm_fwd(x, s["norm_ffn1"], "norm_ffn1")
    g1, up1, hid1, hid1_t = carry("ffn1_hidden", _ffn_hidden, u1, w["ffn1_w_in"])
    w["ffn1_w_out"] = weight("ffn1_w_out")
    h1 = carry("ffn1_out", _mm, hid1, w["ffn1_w_out"], residual=x, **ffn_out)
    w["w_in_mix"] = weight("w_in_mix")
    u2 = _rmsnorm_fwd(h1, s["norm_mix"], "norm_mix")
    proj = carry("mix_in_proj", _mm, u2, w["w_in_mix"], nt=True, out_dtype=F32, tm=512, tn=512)
    w["w_out_mix"] = weight("w_out_mix")
    gw = _dense_gate_blocks(s["lru_gate_w"]).astype(BF16)
    gb = s["lru_gate_b"].reshape(4, c)
    tables, tables_vjp = jax.vjp(_bias_tables, s["attn_rpb"])
    ya, hf, hb = carry("lru_fwd", _lru_fwd, proj, conv_w, s["lru_conv_b"], gw, gb, lam)
    yb = carry("attn_fwd", _attn_fwd, proj, tables, width)
    y, yt = _mixnorm_fwd(ya, yb, s["lru_out_norm"], s["attn_out_norm"], "mix_norm")
    h2 = carry("mix_out_proj", _mm, y, w["w_out_mix"], nt=False, out_dtype=F32, tm=512, tn=512, residual=h1)
    u3 = _rmsnorm_fwd(h2, s["norm_ffn2"], "norm_ffn2")
    w["ffn2_w_in"] = weight("ffn2_w_in")
    g2, up2, hid2, hid2_t = carry("ffn2_hidden", _ffn_hidden, u3, w["ffn2_w_in"])
    w["ffn2_w_out"] = weight("ffn2_w_out")
    h3 = carry("ffn2_out", _mm, hid2, w["ffn2_w_out"], residual=h2, **ffn_out)
    dh3, df2, loss_part, d_norm_final = _final_loss(h3, s["norm_final"], target, "final_loss")

    grads = {}
    grad_of = dict(nt=False, out_dtype=BF16, tm=512, tn=1024)

    def reduce_in_chip(n):
        traffic.open("to_sibling", n, grads[n])

    def reduce_over_chips(n):
        traffic.open("to_chips", n, _pair_sum(grads[n], traffic.result("to_sibling", n), "pair_sum_" + n))

    f = hid2_t.shape[0]
    grads["ffn2_w_out"] = carry("ffn2_out_grad", _mm, hid2_t, df2, **grad_of)
    reduce_in_chip("ffn2_w_out")
    traffic.alone("to_sibling_ffn2_out")
    reduce_over_chips("ffn2_w_out")
    du3, da2_t = carry("ffn2_bwd", _ffn_bwd, df2, g2, up2, w["ffn2_w_in"], w["ffn2_w_out"])
    grads["ffn2_w_in"] = carry("ffn2_in_grad", _mm, da2_t.reshape(2 * f, t), u3, **grad_of)
    reduce_in_chip("ffn2_w_in")
    dh2, dh2b, d_norm_ffn2 = carry("norm_ffn2_bwd", _rmsnorm_bwd, du3, h2, s["norm_ffn2"], dh3, 1.0)
    reduce_over_chips("ffn2_w_in")
    grads["w_out_mix"] = carry("mix_out_grad", _mm, yt, dh2b, **grad_of)
    reduce_in_chip("w_out_mix")
    dy = carry("mix_out_bwd", _mm, dh2b, w["w_out_mix"], nt=True, out_dtype=F32, tm=512, tn=512)
    dya, dyb, d_lru_out_norm, d_attn_out_norm = _mixnorm_bwd(dy, ya, yb, s["lru_out_norm"], s["attn_out_norm"],
                                                             "mix_norm_bwd")
    dq, dk, dv, dqkv_t, d_tables = carry("attn_bwd", _attn_bwd, proj, tables, dyb)
    dx_lru, dg_lru, dxg_t, d_conv_w, d_conv_b, d_gw, d_gb, d_lam = carry(
        "lru_bwd", _lru_bwd, proj, conv_w, s["lru_conv_b"], gw, gb, lam, hf, hb, dya)
    reduce_over_chips("w_out_mix")
    rows_of = 2 * c + 3 * width
    lru_rows = carry("mix_in_grad_lru", _mm, dxg_t.reshape(2 * c, t), u2, out_rows=rows_of, **grad_of)
    grads["w_in_mix"] = carry("mix_in_grad_attn", _mm, dqkv_t.reshape(3 * width, t), u2, out_rows=rows_of,
                              row_offset=2 * c, into=lru_rows, **grad_of)
    reduce_in_chip("w_in_mix")
    du2 = carry("mix_in_bwd", _mm, [dx_lru, dg_lru, dq, dk, dv], w["w_in_mix"], nt=False, out_dtype=F32, tm=512,
                tn=512)
    reduce_over_chips("w_in_mix")
    dh1, df1, d_norm_mix = carry("norm_mix_bwd", _rmsnorm_bwd, du2, h1, s["norm_mix"], dh2, 0.5)

    by_device = lambda a: a.reshape(a.shape[0], N_DEV, -1).transpose(1, 0, 2)
    small = {
        "norm_mix": d_norm_mix, "lru_conv_b": d_conv_b, "lru_gate_w": _diag_gate_blocks(d_gw, s["lru_gate_w"].shape),
        "lru_gate_b": d_gb.reshape(s["lru_gate_b"].shape), "attn_rpb": tables_vjp(d_tables)[0],
        "lru_out_norm": d_lru_out_norm, "attn_out_norm": d_attn_out_norm, "norm_ffn2": d_norm_ffn2,
        "norm_final": d_norm_final, "lru_conv_w": by_device(d_conv_w), "lru_lambda": by_device(d_lam),
    }
    early = [small[n] for n in SMALL_ORDER[1:]]
    traffic.open("gather", "small_grads", _pack(early))

    grads["ffn1_w_out"] = carry("ffn1_out_grad", _mm, hid1_t, df1, **grad_of)
    reduce_in_chip("ffn1_w_out")
    du1, da1_t = carry("ffn1_bwd", _ffn_bwd, df1, g1, up1, w["ffn1_w_in"], w["ffn1_w_out"])
    reduce_over_chips("ffn1_w_out")
    gate_rows = carry("ffn1_in_grad_gate", _mm, da1_t, u1, lead=0, out_rows=2 * f, **grad_of)
    grads["ffn1_w_in"] = carry("ffn1_in_grad_up", _mm, da1_t, u1, lead=1, out_rows=2 * f, row_offset=f,
                               into=gate_rows, **grad_of)
    reduce_in_chip("ffn1_w_in")
    grad_x, _, d_norm_ffn1 = carry("norm_ffn1_bwd", _rmsnorm_bwd, du1, x, s["norm_ffn1"], dh1, 1.0)
    traffic.open("gather", "late_grads", _pack([d_norm_ffn1]))
    reduce_over_chips("ffn1_w_in")
    traffic.alone("to_chips_ffn1")
    partials = {n: traffic.result("to_chips", n) for n in LARGE}
    reduced = (_unpack(_sum_devices(traffic.result("gather", "late_grads"), "sum_late_grads"), [d_norm_ffn1])
               + _unpack(_sum_devices(traffic.result("gather", "small_grads"), "sum_small_grads"), early))
    assert not traffic.transfers, list(traffic.transfers)
    return loss_part[0, 0], grad_x, partials, dict(zip(SMALL_ORDER, reduced))


def _step(x, loss_target, p, m, v):
    me = 4 * lax.axis_index("x") + 2 * lax.axis_index("y") + lax.axis_index("c")

    shards = {n: (_cast_transposed if n in COLUMN_SHARDED else _cast_rows)(p[n], "cast_" + n) for n in LARGE}
    sharded_small = (jnp.pad(p["lru_conv_w"], ((0, SUBLANES - CONV_WIDTH), (0, 0)))
                     + jnp.pad(p["lru_lambda"], ((CONV_WIDTH, SUBLANES - CONV_WIDTH - 2), (0, 0))))
    s = {n: p[n] if n in ("lru_gate_w", "lru_gate_b", "attn_rpb") else p[n].reshape(1, -1) for n in REPLICATED}

    loss_part, grad_x, partials, small = _forward_backward(x, loss_target, shards, sharded_small, s)
    loss = lax.psum(loss_part, ("x", "y", "c"))

    out = {}
    for n in LARGE:
        update = _adamw_cols if n in COLUMN_SHARDED else _adamw_rows
        out[n] = update(p[n], partials[n], m[n], v[n], "adamw_" + n)

    g_small = {n: lax.dynamic_index_in_dim(g, me, axis=0, keepdims=False) if n in SHARDED_SMALL else g
               for n, g in small.items()}
    names = SMALL_ORDER
    like = [p[n] for n in names]
    pack_of = lambda d: _pack([d[n].reshape(p[n].shape) for n in names])
    upd = _adamw_small(pack_of(p), pack_of(g_small), pack_of(m), pack_of(v), "adamw_small")
    for n, d_, m_, v_ in zip(names, *[_unpack(u, like) for u in upd]):
        out[n] = (g_small[n].reshape(p[n].shape), d_, m_, v_)
    return loss, grad_x, out


def kernel(x, norm_ffn1, ffn1_w_in, ffn1_w_out, norm_mix, w_in_mix, lru_conv_w, lru_conv_b, lru_gate_w, lru_gate_b, lru_lambda, attn_rpb, lru_out_norm, attn_out_norm, w_out_mix, norm_ffn2, ffn2_w_in, ffn2_w_out, norm_final, loss_target, m_norm_ffn1, m_ffn1_w_in, m_ffn1_w_out, m_norm_mix, m_w_in_mix, m_lru_conv_w, m_lru_conv_b, m_lru_gate_w, m_lru_gate_b, m_lru_lambda, m_attn_rpb, m_lru_out_norm, m_attn_out_norm, m_w_out_mix, m_norm_ffn2, m_ffn2_w_in, m_ffn2_w_out, m_norm_final, v_norm_ffn1, v_ffn1_w_in, v_ffn1_w_out, v_norm_mix, v_w_in_mix, v_lru_conv_w, v_lru_conv_b, v_lru_gate_w, v_lru_gate_b, v_lru_lambda, v_attn_rpb, v_lru_out_norm, v_attn_out_norm, v_w_out_mix, v_norm_ffn2, v_ffn2_w_in, v_ffn2_w_out, v_norm_final):
    given = dict(locals())
    drop_layer = lambda n, a: a if n == "norm_final" else a[0]
    p = {n: drop_layer(n, given[n]) for n in WEIGHTS}
    m = {n: drop_layer(n, given["m_" + n]) for n in WEIGHTS}
    v = {n: drop_layer(n, given["v_" + n]) for n in WEIGHTS}
    loss, grad_x, out = _step(x[0], loss_target[0], p, m, v)
    shaped = lambda n, a: a.reshape(given[n].shape)
    return (loss, grad_x[None], *[shaped(n, out[n][k]) for k in range(4) for n in WEIGHTS])
```

```python
import math

import numpy as np
import jax
import jax.numpy as jnp
from jax import lax
from jax.experimental import pallas as pl
from jax.experimental.pallas import tpu as pltpu

F32 = jnp.float32
BF16 = jnp.bfloat16
SDS = jax.ShapeDtypeStruct

N_DEV = 8
N_CHIP = 4
NORM_EPS = 1e-6
RG_C = 8.0
CONV_WIDTH = 4
HEAD_DIM = 64
GRID_W = 64
WIN_ROWS = 8
WIN_COLS = 16
NEG = -1e30

ADAM_LR = 0.001
ADAM_B1 = 0.9
ADAM_B2 = 0.999
ADAM_EPS = 1e-08
ADAM_WD = 0.01
ADAM_STEP = 10

LANES = 128
SUBLANES = 8
VMEM_LIMIT = 56 * 1024 * 1024

NT = (((1,), (1,)), ((), ()))
TN = (((0,), (0,)), ((), ()))
ANY = pl.BlockSpec(memory_space=pl.ANY)
WHOLE = pl.BlockSpec(memory_space=pltpu.VMEM)
MESH = pl.DeviceIdType.MESH


def _sigmoid(x):
    return 1.0 / (1.0 + jnp.exp(-x))


def _gelu_parts(x):
    c = math.sqrt(2.0 / math.pi)
    t = jnp.tanh(c * (x + 0.044715 * (x * x * x)))
    gelu = 0.5 * x * (1.0 + t)
    dgelu = 0.5 * (1.0 + t) + 0.5 * x * (1.0 - t * t) * (c * (1.0 + 3.0 * 0.044715 * (x * x)))
    return gelu, dgelu


def _expm1(x):
    poly = x * (1.0 + x * (1.0 / 2) * (1.0 + x * (1.0 / 3) * (1.0 + x * (1.0 / 4) * (1.0 + x * (1.0 / 5) * (1.0 + x * (1.0 / 6))))))
    return jnp.where(jnp.abs(x) < 0.25, poly, jnp.exp(x) - 1.0)


def _softplus(x):
    return jnp.maximum(x, 0.0) + jnp.log1p(jnp.exp(-jnp.abs(x)))


class _Piece:
    N_REMOTE = {"gather": 7, "to_sibling": N_CHIP, "to_chips": 3}
    N_LOCAL = {"gather": 1, "to_sibling": 0, "to_chips": 1}

    def __init__(self, kind, src, dest, lo, hi):
        self.kind, self.src, self.dest, self.lo, self.hi = kind, src, dest, lo, hi


RELAY_AT = 60


class _Job:
    def __init__(self, pieces):
        self.pieces = list(pieces)
        self.ins = [p.src for p in self.pieces]
        self.out_shapes = [SDS(p.dest.shape, p.dest.dtype) for p in self.pieces]
        self.aliased = [i for i, p in enumerate(self.pieces) if not isinstance(p.dest, SDS)]
        self.n_remote = sum(_Piece.N_REMOTE[p.kind] for p in self.pieces)
        self.n_local = max(sum(_Piece.N_LOCAL[p.kind] for p in self.pieces), 1)

    def _each(self, step, ins, outs, send_sems, recv_sems, local_sems):
        remote = local = 0
        for p, src, dst in zip(self.pieces, ins, outs):
            _EXCHANGES[p.kind](step, p, src, dst, send_sems, recv_sems, local_sems, remote, local)
            remote += _Piece.N_REMOTE[p.kind]
            local += _Piece.N_LOCAL[p.kind]

    def start(self, *refs):
        self._each("start", *refs)

    def relay(self, *refs):
        self._each("relay", *refs)

    def finish(self, *refs):
        self._each("finish", *refs)


def _call(body, *, name, args, out_shape, in_specs, out_specs, grid=(), scratch_shapes=(), aliases=None, job=None):
    single = not isinstance(out_shape, (tuple, list))
    out_shape = (out_shape,) if single else tuple(out_shape)
    out_specs = (out_specs,) if single else tuple(out_specs)
    aliases = dict(aliases or {})
    params = pltpu.CompilerParams(dimension_semantics=("arbitrary",) * len(grid) if grid else None,
                                  vmem_limit_bytes=VMEM_LIMIT)
    if job is None:
        res = pl.pallas_call(body, out_shape=out_shape, grid=grid, in_specs=list(in_specs), out_specs=out_specs,
                             scratch_shapes=list(scratch_shapes), input_output_aliases=aliases, name=name,
                             compiler_params=params)(*args)
        return res[0] if single else res

    n_in, n_out, n_scr = len(args), len(out_shape), len(scratch_shapes)
    j_in, j_out, j_alias = len(job.ins), len(job.out_shapes), len(job.aliased)

    def hosted(*refs):
        ins, refs = refs[:n_in], refs[n_in:]
        j_ins, refs = refs[:j_in], refs[j_in + j_alias:]
        outs, refs = refs[:n_out], refs[n_out:]
        j_outs, refs = refs[:j_out], refs[j_out:]
        scr, sems = refs[:n_scr], refs[n_scr:]
        if grid:
            step = 0
            for axis, size in enumerate(grid):
                step = step * size + pl.program_id(axis)
            steps = math.prod(grid)
            pl.when(step == 0)(lambda: job.start(j_ins, j_outs, *sems))
            body(*ins, *outs, *scr)
            pl.when(step == min(RELAY_AT * steps // 100, steps - 1))(lambda: job.relay(j_ins, j_outs, *sems))
            pl.when(step == steps - 1)(lambda: job.finish(j_ins, j_outs, *sems))
        else:
            job.start(j_ins, j_outs, *sems)
            body(*ins, *outs, *scr)
            job.relay(j_ins, j_outs, *sems)
            job.finish(j_ins, j_outs, *sems)

    res = pl.pallas_call(
        hosted, out_shape=out_shape + tuple(job.out_shapes), grid=grid,
        in_specs=list(in_specs) + [ANY] * (j_in + j_alias), out_specs=out_specs + (ANY,) * j_out,
        scratch_shapes=list(scratch_shapes) + [pltpu.SemaphoreType.DMA((job.n_remote,)),
                                               pltpu.SemaphoreType.DMA((job.n_remote,)),
                                               pltpu.SemaphoreType.DMA((job.n_local,))],
        input_output_aliases={**aliases, **{n_in + j_in + k: n_out + i for k, i in enumerate(job.aliased)}},
        name=name, compiler_params=params)(*args, *job.ins, *[job.pieces[i].dest for i in job.aliased])
    own, carried = res[:n_out], res[n_out:]
    return (own[0] if single else own), carried


def _run_job(job, name):
    return _call(lambda: None, name=name, args=[], out_shape=(), in_specs=[], out_specs=(), job=job)[1]


def _position():
    return lax.axis_index("x"), lax.axis_index("y"), lax.axis_index("c")


def _flat(px, py, pc):
    return 4 * px + 2 * py + pc


def _gather_exchange(step, p, src, dst, send_sems, recv_sems, local_sems, r0, l0):
    x, y, c = _position()
    me, sibling = (x, y, c), (x, y, 1 - c)
    along_x, along_y, diagonal = (1 - x, y), (x, 1 - y), (1 - x, 1 - y)
    south = c == 0
    passed_on = (jnp.where(south, 1 - x, x), jnp.where(south, y, 1 - y))
    passed_to = (jnp.where(south, x, 1 - x), jnp.where(south, 1 - y, y))
    rb, n_rows = p.src.shape[0], p.hi - p.lo
    mine = src.at[pl.ds(p.lo, n_rows), :]

    def rows(block):
        return dst.at[pl.ds(_flat(*block) * rb + p.lo, n_rows), :]

    def copy(k, block, to, own=False):
        return pltpu.make_async_remote_copy(
            src_ref=mine if own else rows(block), dst_ref=rows(block),
            send_sem=send_sems.at[r0 + k], recv_sem=recv_sems.at[r0 + k], device_id=to, device_id_type=MESH)

    local = pltpu.make_async_copy(mine, rows(me), local_sems.at[l0])
    if step == "start":
        local.start()
        copy(0, me, sibling, own=True).start()
        copy(1, me, (*along_x, c), own=True).start()
        copy(2, me, (*along_y, c), own=True).start()
    elif step == "relay":
        copy(1, (*along_x, c), me).wait_recv()
        copy(2, (*along_y, c), me).wait_recv()
        copy(3, (*passed_on, c), (*passed_to, c)).start()
        copy(4, (*along_x, c), sibling).start()
        copy(5, (*along_y, c), sibling).start()
    else:
        copy(3, (*diagonal, c), me).wait_recv()
        copy(6, (*diagonal, c), sibling).start()
        copy(0, sibling, me).wait_recv()
        copy(4, (*along_x, 1 - c), me).wait_recv()
        copy(5, (*along_y, 1 - c), me).wait_recv()
        copy(6, (*diagonal, 1 - c), me).wait_recv()
        copy(0, me, sibling, own=True).wait_send()
        copy(1, me, (*along_x, c), own=True).wait_send()
        copy(2, me, (*along_y, c), own=True).wait_send()
        copy(3, (*passed_on, c), (*passed_to, c)).wait_send()
        copy(4, (*along_x, c), sibling).wait_send()
        copy(5, (*along_y, c), sibling).wait_send()
        copy(6, (*diagonal, c), sibling).wait_send()
        local.wait()


def _sibling_exchange(step, p, src, dst, send_sems, recv_sems, local_sems, r0, l0):
    x, y, c = _position()
    rb, n_rows = p.src.shape[0] // N_DEV, p.hi - p.lo
    for q in range(N_CHIP):
        copy = pltpu.make_async_remote_copy(
            src_ref=src.at[pl.ds((2 * q + 1 - c) * rb + p.lo, n_rows), :],
            dst_ref=dst.at[pl.ds(q * rb + p.lo, n_rows), :],
            send_sem=send_sems.at[r0 + q], recv_sem=recv_sems.at[r0 + q], device_id=(x, y, 1 - c), device_id_type=MESH)
        if step == "start":
            copy.start()
        elif step == "finish":
            copy.wait()


CHIP_FLIPS = [(1, 0), (0, 1), (1, 1)]


def _chips_exchange(step, p, src, dst, send_sems, recv_sems, local_sems, r0, l0):
    x, y, c = _position()
    rb, n_rows = p.src.shape[0] // N_CHIP, p.hi - p.lo

    def slot(ref, px, py):
        return ref.at[pl.ds((2 * px + py) * rb + p.lo, n_rows), :]

    def copy(k, landing=False):
        px = 1 - x if CHIP_FLIPS[k][0] else x
        py = 1 - y if CHIP_FLIPS[k][1] else y
        return pltpu.make_async_remote_copy(
            src_ref=slot(dst, px, py) if landing else slot(src, px, py),
            dst_ref=slot(dst, px, py) if landing else slot(dst, x, y),
            send_sem=send_sems.at[r0 + k], recv_sem=recv_sems.at[r0 + k], device_id=(px, py, c), device_id_type=MESH)

    local = pltpu.make_async_copy(slot(src, x, y), slot(dst, x, y), local_sems.at[l0])
    if step == "start":
        local.start()
        for k in range(3):
            copy(k).start()
    elif step == "finish":
        for k in range(3):
            copy(k, landing=True).wait_recv()
        for k in range(3):
            copy(k).wait_send()
        local.wait()


_EXCHANGES = {"gather": _gather_exchange, "to_sibling": _sibling_exchange, "to_chips": _chips_exchange}


def _gathered(shard):
    return SDS((N_DEV * shard.shape[0], shard.shape[1]), shard.dtype)


def _split(rows, parts):
    cuts = [rows * k // parts // 16 * 16 for k in range(parts)] + [rows]
    return list(zip(cuts[:-1], cuts[1:]))


def _pair_sum(g, from_sibling, name):
    rb, n = g.shape[0] // N_DEV, g.shape[1]
    tr = rb if rb * n * 2 <= 3 * 1024 * 1024 else rb // 2
    core = lax.axis_index("c").astype(jnp.int32).reshape(1)

    def body(c_ref, g_ref, r_ref, o_ref):
        o_ref[...] = (g_ref[...].astype(F32) + r_ref[...].astype(F32)).astype(BF16)

    grid_spec = pltpu.PrefetchScalarGridSpec(
        num_scalar_prefetch=1, grid=(N_CHIP, rb // tr),
        in_specs=[pl.BlockSpec((None, None, tr, n), lambda q, i, c_ref: (q, c_ref[0], i, 0)),
                  pl.BlockSpec((None, tr, n), lambda q, i, c_ref: (q, i, 0))],
        out_specs=pl.BlockSpec((None, tr, n), lambda q, i, c_ref: (q, i, 0)))
    out = pl.pallas_call(
        body, grid_spec=grid_spec, out_shape=SDS((N_CHIP, rb, n), BF16), name=name,
        compiler_params=pltpu.CompilerParams(dimension_semantics=("arbitrary",) * 2, vmem_limit_bytes=VMEM_LIMIT))(
            core, g.reshape(N_CHIP, 2, rb, n), from_sibling.reshape(N_CHIP, rb, n))
    return out.reshape(N_CHIP * rb, n)


def _sum_devices(gathered, name):
    r = gathered.shape[0] // N_DEV

    def body(g_ref, o_ref):
        acc = g_ref[0]
        for s in range(1, N_DEV):
            acc = acc + g_ref[s]
        o_ref[...] = acc

    return _call(body, name=name, args=[gathered.reshape(N_DEV, r, LANES)], out_shape=SDS((r, LANES), F32),
                 in_specs=[WHOLE], out_specs=WHOLE)


def _cast_rows(w, name):
    def body(w_ref, o_ref):
        o_ref[...] = w_ref[...].astype(BF16)

    return _call(body, name=name, args=[w], out_shape=SDS(w.shape, BF16), in_specs=[WHOLE], out_specs=WHOLE)


def _cast_transposed(w, name):
    d, n = w.shape
    td = 512

    def body(w_ref, o_ref):
        o_ref[...] = w_ref[...].T.astype(BF16)

    return _call(body, name=name, args=[w], out_shape=SDS((n, d), BF16), grid=(d // td,),
                 in_specs=[pl.BlockSpec((td, n), lambda i: (i, 0))], out_specs=pl.BlockSpec((n, td), lambda i: (0, i)))


ROW_TILE = 256


def _rmsnorm_fwd(h, gain, name):
    t, d = h.shape

    def body(h_ref, g_ref, u_ref):
        x = h_ref[...]
        u_ref[...] = (x * lax.rsqrt(jnp.mean(x * x, axis=-1, keepdims=True) + NORM_EPS) * g_ref[...]).astype(BF16)

    row = pl.BlockSpec((ROW_TILE, d), lambda i: (i, 0))
    return _call(body, name=name, args=[h, gain], out_shape=SDS((t, d), BF16), grid=(t // ROW_TILE,),
                 in_specs=[row, pl.BlockSpec((1, d), lambda i: (0, 0))], out_specs=row)


def _rms_bwd_math(x, gain, dy):
    rstd = lax.rsqrt(jnp.mean(x * x, axis=-1, keepdims=True) + NORM_EPS)
    xhat = x * rstd
    dxh = dy * gain
    dx = rstd * (dxh - xhat * jnp.mean(dxh * xhat, axis=-1, keepdims=True))
    return dx, jnp.sum(dy * xhat, axis=0, keepdims=True)


def _rmsnorm_bwd(du, h, gain, resid, bf_scale, name, job=None):
    t, d = h.shape

    def body(du_ref, h_ref, g_ref, r_ref, dh_ref, dhb_ref, dg_ref):
        @pl.when(pl.program_id(0) == 0)
        def _():
            dg_ref[...] = jnp.zeros_like(dg_ref)

        dx, dg = _rms_bwd_math(h_ref[...], g_ref[...], du_ref[...])
        dh = r_ref[...] + dx
        dh_ref[...] = dh
        dhb_ref[...] = (bf_scale * dh).astype(BF16)
        dg_ref[...] += dg

    row = pl.BlockSpec((ROW_TILE, d), lambda i: (i, 0))
    vec = pl.BlockSpec((1, d), lambda i: (0, 0))
    return _call(body, name=name, args=[du, h, gain, resid],
                 out_shape=(SDS((t, d), F32), SDS((t, d), BF16), SDS((1, d), F32)), grid=(t // ROW_TILE,),
                 in_specs=[row, row, vec, row], out_specs=(row, row, vec), job=job)


def _final_loss(h, gain, target, name):
    t, d = h.shape

    def body(h_ref, g_ref, t_ref, dh_ref, dhb_ref, loss_ref, dg_ref):
        @pl.when(pl.program_id(0) == 0)
        def _():
            dg_ref[...] = jnp.zeros_like(dg_ref)
            loss_ref[...] = jnp.zeros_like(loss_ref)

        x = h_ref[...]
        gain = g_ref[...]
        out = x * lax.rsqrt(jnp.mean(x * x, axis=-1, keepdims=True) + NORM_EPS) * gain
        err = out - t_ref[...]
        loss_ref[...] += 0.5 * jnp.sum(jnp.mean(err * err, axis=-1, keepdims=True), axis=0, keepdims=True)
        dx, dg = _rms_bwd_math(x, gain, err * (1.0 / d))
        dh_ref[...] = dx
        dhb_ref[...] = (0.5 * dx).astype(BF16)
        dg_ref[...] += dg

    row = pl.BlockSpec((ROW_TILE, d), lambda i: (i, 0))
    vec = pl.BlockSpec((1, d), lambda i: (0, 0))
    one = pl.BlockSpec((SUBLANES, LANES), lambda i: (0, 0))
    return _call(body, name=name, args=[h, gain, target],
                 out_shape=(SDS((t, d), F32), SDS((t, d), BF16), SDS((SUBLANES, LANES), F32), SDS((1, d), F32)),
                 grid=(t // ROW_TILE,), in_specs=[row, vec, row], out_specs=(row, row, one, vec))


def _mixnorm_fwd(ya, yb, ga, gb, name):
    t, c = ya.shape

    def body(ya_ref, yb_ref, ga_ref, gb_ref, y_ref, yt_ref):
        for k, (src, g_ref) in enumerate(((ya_ref, ga_ref), (yb_ref, gb_ref))):
            x = src[...]
            u = x * lax.rsqrt(jnp.mean(x * x, axis=-1, keepdims=True) + NORM_EPS) * g_ref[...]
            y_ref[:, k * c:(k + 1) * c] = u.astype(BF16)
            yt_ref[k * c:(k + 1) * c, :] = u.T.astype(BF16)

    row = pl.BlockSpec((ROW_TILE, c), lambda i: (i, 0))
    vec = pl.BlockSpec((1, c), lambda i: (0, 0))
    return _call(body, name=name, args=[ya, yb, ga, gb],
                 out_shape=(SDS((t, 2 * c), BF16), SDS((2 * c, t), BF16)), grid=(t // ROW_TILE,),
                 in_specs=[row, row, vec, vec],
                 out_specs=(pl.BlockSpec((ROW_TILE, 2 * c), lambda i: (i, 0)),
                            pl.BlockSpec((2 * c, ROW_TILE), lambda i: (0, i))))


def _mixnorm_bwd(dy, ya, yb, ga, gb, name):
    t, c = ya.shape

    def body(dy_ref, ya_ref, yb_ref, ga_ref, gb_ref, dya_ref, dyb_ref, dga_ref, dgb_ref):
        @pl.when(pl.program_id(0) == 0)
        def _():
            dga_ref[...] = jnp.zeros_like(dga_ref)
            dgb_ref[...] = jnp.zeros_like(dgb_ref)

        dxa, dga = _rms_bwd_math(ya_ref[...], ga_ref[...], dy_ref[:, :c])
        dxb, dgb = _rms_bwd_math(yb_ref[...], gb_ref[...], dy_ref[:, c:])
        dya_ref[...] = dxa
        dyb_ref[...] = dxb
        dga_ref[...] += dga
        dgb_ref[...] += dgb

    row = pl.BlockSpec((ROW_TILE, c), lambda i: (i, 0))
    vec = pl.BlockSpec((1, c), lambda i: (0, 0))
    return _call(body, name=name, args=[dy, ya, yb, ga, gb],
                 out_shape=(SDS((t, c), F32), SDS((t, c), F32), SDS((1, c), F32), SDS((1, c), F32)),
                 grid=(t // ROW_TILE,),
                 in_specs=[pl.BlockSpec((ROW_TILE, 2 * c), lambda i: (i, 0)), row, row, vec, vec],
                 out_specs=(row, row, vec, vec))


def _tile(n, want):
    return max(t for t in range(LANES, min(n, want) + 1, LANES) if n % t == 0)


def _mm(a, b, *, nt, out_dtype, tm, tn, name, residual=None, scale=None, lead=None, out_rows=None, row_offset=0,
        into=None, job=None):
    parts = list(a) if isinstance(a, (list, tuple)) else [a]
    m = parts[0].shape[-2]
    widths = [p.shape[-1] for p in parts]
    k = sum(widths)
    n = b.shape[0] if nt else b.shape[1]
    tm, tn = _tile(math.gcd(m, row_offset), tm), _tile(n, tn)
    out_rows = m if out_rows is None else out_rows

    def body(*refs):
        a_refs, b_ref, rest = refs[:len(parts)], refs[len(parts)], refs[len(parts) + 1:]
        o_ref = rest[-1]
        out, at = None, 0
        for a_ref, width in zip(a_refs, widths):
            av = a_ref[...].astype(BF16)
            if nt:
                term = lax.dot_general(av, b_ref[:, at:at + width].astype(BF16), NT, preferred_element_type=F32)
            else:
                term = jnp.dot(av, b_ref[at:at + width, :].astype(BF16), preferred_element_type=F32)
            out = term if out is None else out + term
            at += width
        if residual is not None:
            out = rest[0][...] + (out if scale is None else scale * out)
        o_ref[...] = out.astype(out_dtype)

    a_specs =([pl.BlockSpec((tm, width), lambda i, j: (i, 0)) for width in widths] if lead is None
               else [pl.BlockSpec((None, tm, k), lambda i, j: (lead, i, 0))])
    in_specs = a_specs + [pl.BlockSpec((tn, k), lambda i, j: (j, 0)) if nt else pl.BlockSpec((k, tn), lambda i, j: (0, j))]
    args, aliases = parts + [b], {}
    if residual is not None:
        in_specs.append(pl.BlockSpec((tm, tn), lambda i, j: (i, j)))
        args.append(residual)
    if into is not None:
        in_specs.append(ANY)
        aliases[len(args)] = 0
        args.append(into)
    return _call(body, name=name, args=args, out_shape=SDS((out_rows, n), out_dtype), grid=(m // tm, n // tn),
                 in_specs=in_specs, out_specs=pl.BlockSpec((tm, tn), lambda i, j: (row_offset // tm + i, j)),
                 aliases=aliases, job=job)


FFN_TM = 512
FFN_HB = 512


def _ffn_hidden(u, w_in_t, name, job=None):
    t, d = u.shape
    f = w_in_t.shape[0] // 2

    def body(u_ref, w_ref, g_ref, up_ref, hid_ref, hid_t_ref):
        uu = u_ref[...]
        g = lax.dot_general(uu, w_ref[0], NT, preferred_element_type=F32)
        up = lax.dot_general(uu, w_ref[1], NT, preferred_element_type=F32)
        g_ref[...] = g.astype(BF16)
        up_ref[...] = up.astype(BF16)
        hid = (g * _sigmoid(g)) * up
        hid_ref[...] = hid.astype(BF16)
        hid_t_ref[...] = hid.T.astype(BF16)

    pre = pl.BlockSpec((FFN_TM, FFN_HB), lambda i, k: (i, k))
    return _call(body, name=name, args=[u, w_in_t.reshape(2, f, d)],
                 out_shape=(SDS((t, f), BF16), SDS((t, f), BF16), SDS((t, f), BF16), SDS((f, t), BF16)),
                 grid=(t // FFN_TM, f // FFN_HB),
                 in_specs=[pl.BlockSpec((FFN_TM, d), lambda i, k: (i, 0)),
                           pl.BlockSpec((2, FFN_HB, d), lambda i, k: (0, k, 0))],
                 out_specs=(pre, pre, pre, pl.BlockSpec((FFN_HB, FFN_TM), lambda i, k: (k, i))), job=job)


def _ffn_bwd(dfb, gpre, upre, w_in_t, w_out, name, job=None):
    t, d = dfb.shape
    f = w_out.shape[0]
    nk = f // FFN_HB

    def body(df_ref, g_ref, up_ref, w_ref, wo_ref, du_ref, da_t_ref, acc):
        k = pl.program_id(1)

        @pl.when(k == 0)
        def _():
            acc[...] = jnp.zeros_like(acc)

        dhid = lax.dot_general(df_ref[...], wo_ref[...], NT, preferred_element_type=F32)
        g, up = g_ref[...].astype(F32), up_ref[...].astype(F32)
        sig = _sigmoid(g)
        silu = g * sig
        dup = dhid * silu
        dg = dhid * up * (sig * (1.0 + g * (1.0 - sig)))
        da_t_ref[0] = dg.T.astype(BF16)
        da_t_ref[1] = dup.T.astype(BF16)
        acc[...] += (jnp.dot(dg.astype(BF16), w_ref[0], preferred_element_type=F32)
                     + jnp.dot(dup.astype(BF16), w_ref[1], preferred_element_type=F32))

        @pl.when(k == nk - 1)
        def _():
            du_ref[...] = acc[...]

    tok = pl.BlockSpec((FFN_TM, d), lambda i, k: (i, 0))
    pre = pl.BlockSpec((FFN_TM, FFN_HB), lambda i, k: (i, k))
    return _call(body, name=name, args=[dfb, gpre, upre, w_in_t.reshape(2, f, d), w_out],
                 out_shape=(SDS((t, d), F32), SDS((2, f, t), BF16)), grid=(t // FFN_TM, nk),
                 in_specs=[tok, pre, pre, pl.BlockSpec((2, FFN_HB, d), lambda i, k: (0, k, 0)),
                           pl.BlockSpec((FFN_HB, d), lambda i, k: (k, 0))],
                 out_specs=(tok, pl.BlockSpec((2, FFN_HB, FFN_TM), lambda i, k: (0, k, i))),
                 scratch_shapes=[pltpu.VMEM((FFN_TM, d), F32)], job=job)


CH = LANES
PAD = SUBLANES


def _lru_gates(xc, gw_ref, gb_ref, lam_ref, z):
    xcb = xc.astype(BF16)
    r = _sigmoid(jnp.dot(xcb, gw_ref[2 * z], preferred_element_type=F32) + gb_ref[pl.ds(2 * z, 1), :])
    i = _sigmoid(jnp.dot(xcb, gw_ref[2 * z + 1], preferred_element_type=F32) + gb_ref[pl.ds(2 * z + 1, 1), :])
    sp = _softplus(-lam_ref[pl.ds(z, 1), :])
    log_a = (-RG_C * r) * sp
    a = jnp.exp(log_a)
    mult = jnp.sqrt(-_expm1(2.0 * log_a))
    return r, i, sp, a, mult


def _conv(xpad, cw_ref, cb_ref, t):
    xc = cb_ref[...] + cw_ref[pl.ds(0, 1), :] * xpad[pl.ds(PAD - 2, t), :]
    for j in range(1, CONV_WIDTH):
        xc = xc + cw_ref[pl.ds(j, 1), :] * xpad[pl.ds(PAD - 2 + j, t), :]
    return xc


def _fill_padded(pad_ref, value, t):
    pad_ref[pl.ds(0, PAD), :] = jnp.zeros((PAD, CH), F32)
    pad_ref[pl.ds(PAD + t, PAD), :] = jnp.zeros((PAD, CH), F32)
    pad_ref[pl.ds(PAD, t), :] = value


def _scan_pair(t, a_up, b_up, out_up, a_down, b_down, out_down):
    row = lax.broadcasted_iota(jnp.int32, (SUBLANES, CH), 0)

    def compose(a, b, rising):
        for dist in (1, 2, 4):
            shift = dist if rising else SUBLANES - dist
            keep = (row >= dist) if rising else (row < SUBLANES - dist)
            b = jnp.where(keep, b + a * pltpu.roll(b, shift, axis=0), b)
            a = jnp.where(keep, a * pltpu.roll(a, shift, axis=0), a)
        return a, b

    def step(tt, carry):
        hu, hd = carry
        lo = pl.ds(pl.multiple_of(tt * SUBLANES, SUBLANES), SUBLANES)
        hi = pl.ds(pl.multiple_of(t - SUBLANES - tt * SUBLANES, SUBLANES), SUBLANES)
        a, b = compose(a_up[lo, :], b_up[lo, :], True)
        up = b + a * hu
        out_up[lo, :] = up
        a, b = compose(a_down[hi, :], b_down[hi, :], False)
        down = b + a * hd
        out_down[hi, :] = down
        return up[SUBLANES - 1:, :], down[:1, :]

    zero = jnp.zeros((1, CH), F32)
    lax.fori_loop(0, t // SUBLANES, step, (zero, zero), unroll=2)


def _lru_fwd(proj, cw, cb, gw, gb, lam, name, job=None):
    t = proj.shape[0]
    c = cw.shape[1]
    ncb = c // CH

    def body(x_ref, g_ref, cw_ref, cb_ref, gw_ref, gb_ref, lam_ref, ya_ref, hf_ref, hb_ref, xpad, a0, b0, a1, b1):
        _fill_padded(xpad, x_ref[...], t)
        xc = _conv(xpad, cw_ref, cb_ref, t)
        for z, (a_s, b_s) in enumerate(((a0, b0), (a1, b1))):
            _, i, _, a, mult = _lru_gates(xc, gw_ref, gb_ref, lam_ref, z)
            a_s[...] = a
            b_s[...] = mult * (i * xc)
        _scan_pair(t, a0, b0, hf_ref, a1, b1, hb_ref)
        gelu, _ = _gelu_parts(g_ref[...])
        ya_ref[...] = gelu * (hf_ref[...] + hb_ref[...])

    col = lambda off: pl.BlockSpec((t, CH), lambda i: (0, off + i))
    small = lambda rows: pl.BlockSpec((rows, CH), lambda i: (0, i))
    return _call(body, name=name, args=[proj, proj, cw, cb, gw, gb, lam], out_shape=(SDS((t, c), F32),) * 3,
                 grid=(ncb,),
                 in_specs=[col(0), col(ncb), small(CONV_WIDTH), small(1),
                           pl.BlockSpec((4, None, CH, CH), lambda i: (0, i, 0, 0)), small(4), small(2)],
                 out_specs=(col(0),) * 3,
                 scratch_shapes=[pltpu.VMEM((t + 2 * PAD, CH), F32)] + [pltpu.VMEM((t, CH), F32)] * 4, job=job)


def _lru_bwd(proj, cw, cb, gw, gb, lam, hf, hb, dya, name, job=None):
    t = proj.shape[0]
    c = cw.shape[1]
    ncb = c // CH

    def body(x_ref, g_ref, cw_ref, cb_ref, gw_ref, gb_ref, lam_ref, hf_ref, hb_ref, dya_ref,
             dx_ref, dg_ref, dt_ref, dcw_ref, dcb_ref, dgw_ref, dgb_ref, dlam_ref,
             xpad, hpad, dxc, a0, a1, dhs, dh0, dh1):
        _fill_padded(xpad, x_ref[...], t)
        xc = _conv(xpad, cw_ref, cb_ref, t)
        xcb = xc.astype(BF16)
        gates = [_lru_gates(xc, gw_ref, gb_ref, lam_ref, z) for z in range(2)]

        gelu, dgelu = _gelu_parts(g_ref[...])
        dya = dya_ref[...]
        dgate = dya * (hf_ref[...] + hb_ref[...]) * dgelu
        dg_ref[...] = dgate.astype(BF16)
        dt_ref[1] = dgate.T.astype(BF16)
        dhs[...] = dya * gelu

        _fill_padded(hpad, gates[0][3], t)
        a0[...] = hpad[pl.ds(PAD + 1, t), :]
        _fill_padded(hpad, gates[1][3], t)
        a1[...] = hpad[pl.ds(PAD - 1, t), :]
        _scan_pair(t, a1, dhs, dh1, a0, dhs, dh0)

        acc_dxc = jnp.zeros((t, CH), F32)
        for z, (h_ref, dh_ref, shift) in enumerate(((hf_ref, dh0, -1), (hb_ref, dh1, 1))):
            r, i, sp, a, mult = gates[z]
            _fill_padded(hpad, h_ref[...], t)
            h_nb = hpad[pl.ds(PAD + shift, t), :]
            db = dh_ref[...]
            da = db * h_nb
            d_i = db * mult * xc
            acc_dxc = acc_dxc + db * mult * i
            d_mult = db * i * xc
            d_la = da * a - d_mult * (a * a) / mult
            d_r = d_la * (-RG_C * sp)
            dlam_ref[pl.ds(z, 1), :] = (jnp.sum(d_la * (-RG_C * r), axis=0, keepdims=True)
                                        * (-_sigmoid(-lam_ref[pl.ds(z, 1), :])))
            for gate, d_pre in ((0, d_r * r * (1.0 - r)), (1, d_i * i * (1.0 - i))):
                zg = 2 * z + gate
                dgb_ref[pl.ds(zg, 1), :] = jnp.sum(d_pre, axis=0, keepdims=True)
                d_pre_b = d_pre.astype(BF16)
                dgw_ref[zg] = lax.dot_general(xcb, d_pre_b, TN, preferred_element_type=F32)
                acc_dxc = acc_dxc + lax.dot_general(d_pre_b, gw_ref[zg], NT, preferred_element_type=F32)

        dcb_ref[...] = jnp.sum(acc_dxc, axis=0, keepdims=True)
        for j in range(CONV_WIDTH):
            dcw_ref[pl.ds(j, 1), :] = jnp.sum(acc_dxc * xpad[pl.ds(PAD - 2 + j, t), :], axis=0, keepdims=True)
        _fill_padded(dxc, acc_dxc, t)
        dx = cw_ref[pl.ds(0, 1), :] * dxc[pl.ds(PAD + 2, t), :]
        for j in range(1, CONV_WIDTH):
            dx = dx + cw_ref[pl.ds(j, 1), :] * dxc[pl.ds(PAD + 2 - j, t), :]
        dx_ref[...] = dx.astype(BF16)
        dt_ref[0] = dx.T.astype(BF16)

    col = lambda off: pl.BlockSpec((t, CH), lambda i: (0, off + i))
    small = lambda rows: pl.BlockSpec((rows, CH), lambda i: (0, i))
    dense = pl.BlockSpec((4, None, CH, CH), lambda i: (0, i, 0, 0))
    padded = pltpu.VMEM((t + 2 * PAD, CH), F32)
    return _call(
        body, name=name, args=[proj, proj, cw, cb, gw, gb, lam, hf, hb, dya],
        out_shape=(SDS((t, c), BF16), SDS((t, c), BF16), SDS((2, c, t), BF16), SDS((CONV_WIDTH, c), F32),
                   SDS((1, c), F32), SDS((4, ncb, CH, CH), F32), SDS((4, c), F32), SDS((2, c), F32)),
        grid=(ncb,),
        in_specs=[col(0), col(ncb), small(CONV_WIDTH), small(1), dense, small(4), small(2), col(0), col(0), col(0)],
        out_specs=(col(0), col(0), pl.BlockSpec((2, CH, t), lambda i: (0, i, 0)), small(CONV_WIDTH), small(1),
                   dense, small(4), small(2)),
        scratch_shapes=[padded, padded, padded] + [pltpu.VMEM((t, CH), F32)] * 5, job=job)


Q_ROWS = 4
BAND_ROWS = WIN_ROWS + Q_ROWS
BAND_PAIRS = BAND_ROWS // 2
Q_BLOCK = Q_ROWS * GRID_W
BAND = BAND_ROWS * GRID_W
PAIR_W = 2 * GRID_W
N_BOTH = 2 * WIN_ROWS - 2
ENTRY_LEFT_OUT, ENTRY_RIGHT_OUT, ENTRY_OUT = N_BOTH, N_BOTH + 1, N_BOTH + 2
N_ENTRIES = N_BOTH + 3


def _bias_tables(rpb):
    cols = np.arange(GRID_W)
    start = np.clip(cols - WIN_COLS // 2, 0, GRID_W - WIN_COLS)
    valid = (cols[None, :] >= start[:, None]) & (cols[None, :] < start[:, None] + WIN_COLS)
    col_off = np.clip(cols[None, :] - cols[:, None] + WIN_COLS - 1, 0, 2 * WIN_COLS - 2)
    pick_col = jnp.asarray(np.eye(2 * WIN_COLS - 1, dtype=np.float32)[col_off] * valid[..., None])
    by_row = jnp.einsum("hrc,qkc->hrqk", rpb, pick_col, precision=lax.Precision.HIGHEST)
    by_row = jnp.where(jnp.asarray(valid)[None, None], by_row, NEG)
    out = jnp.full_like(by_row[:, :1], NEG)
    first_in, last_in = WIN_ROWS - 1 - WIN_ROWS // 2, 2 * (WIN_ROWS - 1) - WIN_ROWS // 2
    both = jnp.concatenate([by_row[:, :-1], by_row[:, 1:]], axis=-1)
    left_out = jnp.concatenate([out, by_row[:, first_in:first_in + 1]], axis=-1)
    right_out = jnp.concatenate([by_row[:, last_in:last_in + 1], out], axis=-1)
    return jnp.concatenate([both, left_out, right_out, jnp.concatenate([out, out], axis=-1)], axis=1)


def _band_start(m, rows):
    return jnp.clip(Q_ROWS * m - WIN_ROWS // 2, 0, rows - BAND_ROWS)


def _entry(r, key_row, rows):
    w0 = jnp.clip(r - WIN_ROWS // 2, 0, rows - WIN_ROWS)
    left = (key_row >= w0) & (key_row < w0 + WIN_ROWS)
    right = (key_row + 1 >= w0) & (key_row + 1 < w0 + WIN_ROWS)
    return jnp.where(left & right, key_row - r + WIN_ROWS - 1,
                     jnp.where(right, ENTRY_LEFT_OUT, jnp.where(left, ENTRY_RIGHT_OUT, ENTRY_OUT)))


def _transposed_pairs(dst, src_ref):
    for g in range(dst.shape[0]):
        dst[g] = src_ref[pl.ds(g * PAIR_W, PAIR_W), :].T.astype(BF16)


def _band_of(pairs_ref, first_pair, hh):
    heads = pl.ds(hh * HEAD_DIM, HEAD_DIM)
    return jnp.concatenate([pairs_ref[first_pair + g, heads, :] for g in range(BAND_PAIRS)], axis=1)


def _attn_block(qs, kt, tz_ref, hh, m, rows):
    rs = _band_start(m, rows)
    lanes = pl.ds(hh * HEAD_DIM, HEAD_DIM)
    qrows = pl.ds(pl.multiple_of(m * Q_BLOCK, Q_BLOCK), Q_BLOCK)
    band = pl.ds(pl.multiple_of(rs * GRID_W, PAIR_W), BAND)
    entries = [[_entry(Q_ROWS * m + i, rs + 2 * g, rows) for g in range(BAND_PAIRS)] for i in range(Q_ROWS)]
    bias = jnp.concatenate([jnp.concatenate([tz_ref[hh, e] for e in row], axis=1) for row in entries], axis=0)
    q = qs[qrows, lanes]
    s = jnp.dot(q, _band_of(kt, rs // 2, hh), preferred_element_type=F32) * (HEAD_DIM ** -0.5) + bias
    p = jnp.exp(s - jnp.max(s, axis=-1, keepdims=True))
    p = p / jnp.sum(p, axis=-1, keepdims=True)
    return q, p, qrows, band, lanes, entries, rs // 2


def _attn_fwd(proj, tables, width, name, job=None):
    t = proj.shape[0]
    rows = t // GRID_W
    npair = width // LANES
    first = (proj.shape[1] - 3 * width) // LANES

    def body(q_ref, k_ref, v_ref, tz_ref, o_ref, qs, vs, kt):
        qs[...] = q_ref[...].astype(BF16)
        vs[...] = v_ref[...].astype(BF16)
        _transposed_pairs(kt, k_ref)

        def block(m, carry):
            for hh in range(2):
                _, p, qrows, band, lanes, _, _ = _attn_block(qs, kt, tz_ref, hh, m, rows)
                o_ref[qrows, lanes] = jnp.dot(p.astype(BF16), vs[band, lanes], preferred_element_type=F32)
            return carry

        lax.fori_loop(0, rows // Q_ROWS, block, 0, unroll=2)

    col = lambda off: pl.BlockSpec((t, LANES), lambda i: (0, off + i))
    return _call(body, name=name, args=[proj, proj, proj, tables], out_shape=SDS((t, width), F32), grid=(npair,),
                 in_specs=[col(first), col(first + npair), col(first + 2 * npair),
                           pl.BlockSpec((2, N_ENTRIES, GRID_W, PAIR_W), lambda i: (i, 0, 0, 0))],
                 out_specs=col(0),
                 scratch_shapes=[pltpu.VMEM((t, LANES), BF16)] * 2 + [pltpu.VMEM((t // PAIR_W, LANES, PAIR_W), BF16)],
                 job=job)


def _attn_bwd(proj, tables, dyb, name, job=None):
    t, width = dyb.shape
    rows = t // GRID_W
    npair = width // LANES
    first = (proj.shape[1] - 3 * width) // LANES

    def body(q_ref, k_ref, v_ref, tz_ref, do_ref, dq_ref, dk_ref, dv_ref, dt_ref, dtz_ref, dq_s, dk_s, dv_s,
             qs, ks, vs, dos, kt, vt):
        qs[...] = q_ref[...].astype(BF16)
        ks[...] = k_ref[...].astype(BF16)
        vs[...] = v_ref[...].astype(BF16)
        dos[...] = do_ref[...].astype(BF16)
        _transposed_pairs(kt, k_ref)
        _transposed_pairs(vt, v_ref)
        dk_s[...] = jnp.zeros_like(dk_s)
        dv_s[...] = jnp.zeros_like(dv_s)
        dtz_ref[...] = jnp.zeros_like(dtz_ref)

        def block(m, carry):
            for hh in range(2):
                q, p, qrows, band, lanes, entries, first_pair = _attn_block(qs, kt, tz_ref, hh, m, rows)
                do = dos[qrows, lanes]
                dp = jnp.dot(do, _band_of(vt, first_pair, hh), preferred_element_type=F32)
                ds = p * (dp - jnp.sum(dp * p, axis=-1, keepdims=True))
                for i, row in enumerate(entries):
                    for g, e in enumerate(row):
                        dtz_ref[hh, e] += ds[i * GRID_W:(i + 1) * GRID_W, g * PAIR_W:(g + 1) * PAIR_W]
                dsb = (ds * (HEAD_DIM ** -0.5)).astype(BF16)
                dq_s[qrows, lanes] = jnp.dot(dsb, ks[band, lanes], preferred_element_type=F32)
                dk_s[band, lanes] += lax.dot_general(dsb, q, TN, preferred_element_type=F32)
                dv_s[band, lanes] += lax.dot_general(p.astype(BF16), do, TN, preferred_element_type=F32)
            return carry

        lax.fori_loop(0, rows // Q_ROWS, block, 0)
        for n, (src, dst) in enumerate(((dq_s, dq_ref), (dk_s, dk_ref), (dv_s, dv_ref))):
            val = src[...]
            dst[...] = val.astype(BF16)
            dt_ref[n] = val.T.astype(BF16)

    col = lambda off: pl.BlockSpec((t, LANES), lambda i: (0, off + i))
    table = pl.BlockSpec((2, N_ENTRIES, GRID_W, PAIR_W), lambda i: (i, 0, 0, 0))
    pairs = pltpu.VMEM((t // PAIR_W, LANES, PAIR_W), BF16)
    return _call(body, name=name, args=[proj, proj, proj, tables, dyb],
                 out_shape=(SDS((t, width), BF16),) * 3 + (SDS((3, width, t), BF16), SDS(tables.shape, F32)),
                 grid=(npair,),
                 in_specs=[col(first), col(first + npair), col(first + 2 * npair), table, col(0)],
                 out_specs=(col(0), col(0), col(0), pl.BlockSpec((3, LANES, t), lambda i: (0, i, 0)), table),
                 scratch_shapes=[pltpu.VMEM((t, LANES), F32)] * 3 + [pltpu.VMEM((t, LANES), BF16)] * 4 + [pairs, pairs],
                 job=job)


def _adamw_math(w, g, m, v):
    m = ADAM_B1 * m + (1.0 - ADAM_B1) * g
    v = ADAM_B2 * v + (1.0 - ADAM_B2) * (g * g)
    m_hat = m / (1.0 - ADAM_B1 ** ADAM_STEP)
    v_hat = v / (1.0 - ADAM_B2 ** ADAM_STEP)
    delta = -ADAM_LR * (m_hat / (jnp.sqrt(v_hat) + ADAM_EPS) + ADAM_WD * w)
    return delta, m, v


def _sum_partials(p_ref):
    g = p_ref[0].astype(F32)
    for s in range(1, N_CHIP):
        g = g + p_ref[s].astype(F32)
    return g


def _adamw_rows(w, partials, m, v, name):
    rb, n = w.shape
    tr = 64

    def body(w_ref, p_ref, m_ref, v_ref, g_ref, d_ref, nm_ref, nv_ref):
        g = _sum_partials(p_ref)
        g_ref[...] = g
        d_ref[...], nm_ref[...], nv_ref[...] = _adamw_math(w_ref[...], g, m_ref[...], v_ref[...])

    blk = pl.BlockSpec((tr, n), lambda i: (i, 0))
    return _call(body, name=name, args=[w, partials.reshape(N_CHIP, rb, n), m, v], out_shape=(SDS((rb, n), F32),) * 4,
                 grid=(rb // tr,), in_specs=[blk, pl.BlockSpec((N_CHIP, tr, n), lambda i: (0, i, 0)), blk, blk],
                 out_specs=(blk,) * 4)


def _adamw_cols(w, partials, m, v, name):
    d, nb = w.shape
    td = 256

    def body(w_ref, p_ref, m_ref, v_ref, g_ref, d_ref, nm_ref, nv_ref):
        g = _sum_partials(p_ref).T
        g_ref[...] = g
        d_ref[...], nm_ref[...], nv_ref[...] = _adamw_math(w_ref[...], g, m_ref[...], v_ref[...])

    blk = pl.BlockSpec((td, nb), lambda i: (i, 0))
    return _call(body, name=name, args=[w, partials.reshape(N_CHIP, nb, d), m, v], out_shape=(SDS((d, nb), F32),) * 4,
                 grid=(d // td,), in_specs=[blk, pl.BlockSpec((N_CHIP, nb, td), lambda i: (0, 0, i)), blk, blk],
                 out_specs=(blk,) * 4)


def _adamw_small(w, g, m, v, name):
    def body(w_ref, g_ref, m_ref, v_ref, d_ref, nm_ref, nv_ref):
        d_ref[...], nm_ref[...], nv_ref[...] = _adamw_math(w_ref[...], g_ref[...], m_ref[...], v_ref[...])

    return _call(body, name=name, args=[w, g, m, v], out_shape=(SDS(w.shape, F32),) * 3, in_specs=[WHOLE] * 4,
                 out_specs=(WHOLE,) * 3)


TILE = SUBLANES * LANES


def _pack(arrays):
    parts = []
    for a in arrays:
        flat = a.reshape(-1).astype(F32)
        flat = jnp.pad(flat, (0, -flat.size % TILE))
        parts.append(flat.reshape(-1, LANES))
    return jnp.concatenate(parts, axis=0)


def _unpack(pack, like):
    out, row = [], 0
    for a in like:
        n = int(np.prod(a.shape))
        nrows = -(-n // TILE) * SUBLANES
        out.append(pack[row:row + nrows].reshape(-1)[:n].reshape(a.shape))
        row += nrows
    return out


def _dense_gate_blocks(gate_w):
    w = gate_w.reshape(4, -1, 2, HEAD_DIM, HEAD_DIM)
    zero = jnp.zeros_like(w[:, :, 0])
    top = jnp.concatenate([w[:, :, 0], zero], axis=-1)
    bottom = jnp.concatenate([zero, w[:, :, 1]], axis=-1)
    return jnp.concatenate([top, bottom], axis=-2)


def _diag_gate_blocks(dense, shape):
    even = dense[:, :, :HEAD_DIM, :HEAD_DIM]
    odd = dense[:, :, HEAD_DIM:, HEAD_DIM:]
    return jnp.stack([even, odd], axis=2).reshape(shape)


LARGE = ("ffn1_w_in", "ffn1_w_out", "w_in_mix", "w_out_mix", "ffn2_w_in", "ffn2_w_out")
COLUMN_SHARDED = ("ffn1_w_in", "w_in_mix", "ffn2_w_in")
SHARDED_SMALL = ("lru_conv_w", "lru_lambda")
REPLICATED = ("norm_ffn1", "norm_mix", "lru_conv_b", "lru_gate_w", "lru_gate_b", "attn_rpb", "lru_out_norm",
              "attn_out_norm", "norm_ffn2", "norm_final")
SMALL_ORDER = REPLICATED + SHARDED_SMALL
WEIGHTS = ("norm_ffn1", "ffn1_w_in", "ffn1_w_out", "norm_mix", "w_in_mix", "lru_conv_w", "lru_conv_b", "lru_gate_w",
           "lru_gate_b", "lru_lambda", "attn_rpb", "lru_out_norm", "attn_out_norm", "w_out_mix", "norm_ffn2",
           "ffn2_w_in", "ffn2_w_out", "norm_final")


PARTS = {("gather", "w_in_mix"): 4, ("gather", "ffn2_w_in"): 8,
         ("to_chips", "ffn2_w_in"): 8, ("to_chips", "w_in_mix"): 4, ("to_chips", "ffn1_w_out"): 4}
CARRIES = {
    "gather_ffn1_in": [(("gather", "ffn1_w_in"), 1), (("gather", "small"), 1)],
    "ffn1_hidden": [(("gather", "ffn1_w_out"), 1), (("gather", "w_in_mix"), 1)],
    "ffn1_out": [(("gather", "w_in_mix"), 3)],
    "mix_in_proj": [(("gather", "w_out_mix"), 1), (("gather", "ffn2_w_in"), 1)],
    "lru_fwd": [(("gather", "ffn2_w_in"), 3)],
    "attn_fwd": [(("gather", "ffn2_w_in"), 3)],
    "mix_out_proj": [(("gather", "ffn2_w_in"), 1)],
    "ffn2_hidden": [(("gather", "ffn2_w_out"), 1)],
    "ffn2_bwd": [(("to_sibling", "ffn2_w_out"), 1)],
    "ffn2_in_grad": [(("to_chips", "ffn2_w_out"), 1)],
    "norm_ffn2_bwd": [(("to_sibling", "ffn2_w_in"), 1)],
    "mix_out_grad": [(("to_chips", "ffn2_w_in"), 1)],
    "mix_out_bwd": [(("to_chips", "ffn2_w_in"), 1)],
    "attn_bwd": [(("to_chips", "ffn2_w_in"), 4)],
    "lru_bwd": [(("to_chips", "ffn2_w_in"), 2), (("to_sibling", "w_out_mix"), 1)],
    "mix_in_bwd": [(("to_chips", "w_out_mix"), 1), (("to_sibling", "w_in_mix"), 1)],
    "ffn1_out_grad": [(("to_chips", "w_in_mix"), 2)],
    "ffn1_bwd": [(("to_chips", "w_in_mix"), 2), (("to_sibling", "ffn1_w_out"), 1), (("gather", "small_grads"), 1)],
    "ffn1_in_grad_gate": [(("to_chips", "ffn1_w_out"), 2)],
    "ffn1_in_grad_up": [(("to_chips", "ffn1_w_out"), 2)],
    "norm_ffn1_bwd": [(("to_sibling", "ffn1_w_in"), 1)],
    "to_chips_ffn1": [(("to_chips", "ffn1_w_in"), 1), (("gather", "late_grads"), 1)],
}


class _Transfer:
    def __init__(self, kind, src, dest, block_rows, parts):
        self.kind, self.src, self.dest = kind, src, dest
        self.ranges, self.taken = _split(block_rows, parts), 0

    def take(self, count):
        lo, hi = self.ranges[self.taken][0], self.ranges[self.taken + count - 1][1]
        self.taken += count
        return _Piece(self.kind, self.src, self.dest, lo, hi)


class _Traffic:
    def __init__(self):
        self.transfers = {}

    def open(self, kind, name, src):
        if kind == "gather":
            dest, rows = _gathered(src), src.shape[0]
        elif kind == "to_sibling":
            dest, rows = SDS((src.shape[0] // 2, src.shape[1]), src.dtype), src.shape[0] // N_DEV
        else:
            dest, rows = SDS(src.shape, src.dtype), src.shape[0] // N_CHIP
        self.transfers[kind, name] = _Transfer(kind, src, dest, rows, PARTS.get((kind, name), 1))

    def _job(self, host):
        moved = [self.transfers[key] for key, _ in CARRIES[host]]
        return moved, _Job([tr.take(count) for tr, (_, count) in zip(moved, CARRIES[host])])

    def carry(self, host, fn, *args, **kw):
        if host not in CARRIES:
            return fn(*args, name=host, **kw)
        moved, job = self._job(host)
        res, landed = fn(*args, name=host, job=job, **kw)
        for tr, arr in zip(moved, landed):
            tr.dest = arr
        return res

    def alone(self, host):
        moved, job = self._job(host)
        for tr, arr in zip(moved, _run_job(job, host)):
            tr.dest = arr

    def result(self, kind, name):
        tr = self.transfers.pop((kind, name))
        assert tr.taken == len(tr.ranges), (kind, name)
        return tr.dest


def _forward_backward(x, target, shards, sharded_small, s):
    c = s["lru_conv_b"].shape[1]
    width = s["attn_out_norm"].shape[1]
    t = x.shape[0]
    traffic = _Traffic()
    carry = traffic.carry
    weight = lambda n: traffic.result("gather", n)

    for n in LARGE:
        traffic.open("gather", n, shards[n])
    traffic.open("gather", "small", sharded_small)
    traffic.alone("gather_ffn1_in")
    full_small = weight("small").reshape(N_DEV, SUBLANES, c // N_DEV)
    conv_w = full_small[:, :CONV_WIDTH].transpose(1, 0, 2).reshape(CONV_WIDTH, c)
    lam = full_small[:, CONV_WIDTH:CONV_WIDTH + 2].transpose(1, 0, 2).reshape(2, c)
    w = {"ffn1_w_in": weight("ffn1_w_in")}
    ffn_out = dict(nt=False, out_dtype=F32, tm=512, tn=512, scale=0.5)
    u1 = _rmsnorm_fwd(x, s["norm_ffn1"], "norm_ffn1")
    g1, up1, hid1, hid1_t = carry("ffn1_hidden", _ffn_hidden, u1, w["ffn1_w_in"])
    w["ffn1_w_out"] = weight("ffn1_w_out")
    h1 = carry("ffn1_out", _mm, hid1, w["ffn1_w_out"], residual=x, **ffn_out)
    w["w_in_mix"] = weight("w_in_mix")
    u2 = _rmsnorm_fwd(h1, s["norm_mix"], "norm_mix")
    proj = carry("mix_in_proj", _mm, u2, w["w_in_mix"], nt=True, out_dtype=F32, tm=512, tn=512)
    w["w_out_mix"] = weight("w_out_mix")
    gw = _dense_gate_blocks(s["lru_gate_w"]).astype(BF16)
    gb = s["lru_gate_b"].reshape(4, c)
    tables, tables_vjp = jax.vjp(_bias_tables, s["attn_rpb"])
    ya, hf, hb = carry("lru_fwd", _lru_fwd, proj, conv_w, s["lru_conv_b"], gw, gb, lam)
    yb = carry("attn_fwd", _attn_fwd, proj, tables, width)
    y, yt = _mixnorm_fwd(ya, yb, s["lru_out_norm"], s["attn_out_norm"], "mix_norm")
    h2 = carry("mix_out_proj", _mm, y, w["w_out_mix"], nt=False, out_dtype=F32, tm=512, tn=512, residual=h1)
    u3 = _rmsnorm_fwd(h2, s["norm_ffn2"], "norm_ffn2")
    w["ffn2_w_in"] = weight("ffn2_w_in")
    g2, up2, hid2, hid2_t = carry("ffn2_hidden", _ffn_hidden, u3, w["ffn2_w_in"])
    w["ffn2_w_out"] = weight("ffn2_w_out")
    h3 = carry("ffn2_out", _mm, hid2, w["ffn2_w_out"], residual=h2, **ffn_out)
    dh3, df2, loss_part, d_norm_final = _final_loss(h3, s["norm_final"], target, "final_loss")

    grads = {}
    grad_of = dict(nt=False, out_dtype=BF16, tm=512, tn=1024)

    def reduce_in_chip(n):
        traffic.open("to_sibling", n, grads[n])

    def reduce_over_chips(n):
        traffic.open("to_chips", n, _pair_sum(grads[n], traffic.result("to_sibling", n), "pair_sum_" + n))

    f = hid2_t.shape[0]
    grads["ffn2_w_out"] = carry("ffn2_out_grad", _mm, hid2_t, df2, **grad_of)
    reduce_in_chip("ffn2_w_out")
    du3, da2_t = carry("ffn2_bwd", _ffn_bwd, df2, g2, up2, w["ffn2_w_in"], w["ffn2_w_out"])
    reduce_over_chips("ffn2_w_out")
    grads["ffn2_w_in"] = carry("ffn2_in_grad", _mm, da2_t.reshape(2 * f, t), u3, **grad_of)
    reduce_in_chip("ffn2_w_in")
    dh2, dh2b, d_norm_ffn2 = carry("norm_ffn2_bwd", _rmsnorm_bwd, du3, h2, s["norm_ffn2"], dh3, 1.0)
    reduce_over_chips("ffn2_w_in")
    grads["w_out_mix"] = carry("mix_out_grad", _mm, yt, dh2b, **grad_of)
    reduce_in_chip("w_out_mix")
    dy = carry("mix_out_bwd", _mm, dh2b, w["w_out_mix"], nt=True, out_dtype=F32, tm=512, tn=512)
    dya, dyb, d_lru_out_norm, d_attn_out_norm = _mixnorm_bwd(dy, ya, yb, s["lru_out_norm"], s["attn_out_norm"],
                                                             "mix_norm_bwd")
    dq, dk, dv, dqkv_t, d_tables = carry("attn_bwd", _attn_bwd, proj, tables, dyb)
    dx_lru, dg_lru, dxg_t, d_conv_w, d_conv_b, d_gw, d_gb, d_lam = carry(
        "lru_bwd", _lru_bwd, proj, conv_w, s["lru_conv_b"], gw, gb, lam, hf, hb, dya)
    reduce_over_chips("w_out_mix")
    rows_of = 2 * c + 3 * width
    lru_rows = carry("mix_in_grad_lru", _mm, dxg_t.reshape(2 * c, t), u2, out_rows=rows_of, **grad_of)
    grads["w_in_mix"] = carry("mix_in_grad_attn", _mm, dqkv_t.reshape(3 * width, t), u2, out_rows=rows_of,
                              row_offset=2 * c, into=lru_rows, **grad_of)
    reduce_in_chip("w_in_mix")
    du2 = carry("mix_in_bwd", _mm, [dx_lru, dg_lru, dq, dk, dv], w["w_in_mix"], nt=False, out_dtype=F32, tm=512,
                tn=512)
    reduce_over_chips("w_in_mix")
    dh1, df1, d_norm_mix = carry("norm_mix_bwd", _rmsnorm_bwd, du2, h1, s["norm_mix"], dh2, 0.5)

    by_device = lambda a: a.reshape(a.shape[0], N_DEV, -1).transpose(1, 0, 2)
    small = {
        "norm_mix": d_norm_mix, "lru_conv_b": d_conv_b, "lru_gate_w": _diag_gate_blocks(d_gw, s["lru_gate_w"].shape),
        "lru_gate_b": d_gb.reshape(s["lru_gate_b"].shape), "attn_rpb": tables_vjp(d_tables)[0],
        "lru_out_norm": d_lru_out_norm, "attn_out_norm": d_attn_out_norm, "norm_ffn2": d_norm_ffn2,
        "norm_final": d_norm_final, "lru_conv_w": by_device(d_conv_w), "lru_lambda": by_device(d_lam),
    }
    early = [small[n] for n in SMALL_ORDER[1:]]
    traffic.open("gather", "small_grads", _pack(early))

    grads["ffn1_w_out"] = carry("ffn1_out_grad", _mm, hid1_t, df1, **grad_of)
    reduce_in_chip("ffn1_w_out")
    du1, da1_t = carry("ffn1_bwd", _ffn_bwd, df1, g1, up1, w["ffn1_w_in"], w["ffn1_w_out"])
    reduce_over_chips("ffn1_w_out")
    gate_rows = carry("ffn1_in_grad_gate", _mm, da1_t, u1, lead=0, out_rows=2 * f, **grad_of)
    grads["ffn1_w_in"] = carry("ffn1_in_grad_up", _mm, da1_t, u1, lead=1, out_rows=2 * f, row_offset=f,
                               into=gate_rows, **grad_of)
    reduce_in_chip("ffn1_w_in")
    grad_x, _, d_norm_ffn1 = carry("norm_ffn1_bwd", _rmsnorm_bwd, du1, x, s["norm_ffn1"], dh1, 1.0)
    traffic.open("gather", "late_grads", _pack([d_norm_ffn1]))
    reduce_over_chips("ffn1_w_in")
    traffic.alone("to_chips_ffn1")
    partials = {n: traffic.result("to_chips", n) for n in LARGE}
    reduced = (_unpack(_sum_devices(traffic.result("gather", "late_grads"), "sum_late_grads"), [d_norm_ffn1])
               + _unpack(_sum_devices(traffic.result("gather", "small_grads"), "sum_small_grads"), early))
    assert not traffic.transfers, list(traffic.transfers)
    return loss_part[0, 0], grad_x, partials, dict(zip(SMALL_ORDER, reduced))


def _step(x, loss_target, p, m, v):
    me = 4 * lax.axis_index("x") + 2 * lax.axis_index("y") + lax.axis_index("c")

    shards = {n: (_cast_transposed if n in COLUMN_SHARDED else _cast_rows)(p[n], "cast_" + n) for n in LARGE}
    sharded_small = (jnp.pad(p["lru_conv_w"], ((0, SUBLANES - CONV_WIDTH), (0, 0)))
                     + jnp.pad(p["lru_lambda"], ((CONV_WIDTH, SUBLANES - CONV_WIDTH - 2), (0, 0))))
    s = {n: p[n] if n in ("lru_gate_w", "lru_gate_b", "attn_rpb") else p[n].reshape(1, -1) for n in REPLICATED}

    loss_part, grad_x, partials, small = _forward_backward(x, loss_target, shards, sharded_small, s)
    loss = lax.psum(loss_part, ("x", "y", "c"))

    out = {}
    for n in LARGE:
        update = _adamw_cols if n in COLUMN_SHARDED else _adamw_rows
        out[n] = update(p[n], partials[n], m[n], v[n], "adamw_" + n)

    g_small = {n: lax.dynamic_index_in_dim(g, me, axis=0, keepdims=False) if n in SHARDED_SMALL else g
               for n, g in small.items()}
    names = SMALL_ORDER
    like = [p[n] for n in names]
    pack_of = lambda d: _pack([d[n].reshape(p[n].shape) for n in names])
    upd = _adamw_small(pack_of(p), pack_of(g_small), pack_of(m), pack_of(v), "adamw_small")
    for n, d_, m_, v_ in zip(names, *[_unpack(u, like) for u in upd]):
        out[n] = (g_small[n].reshape(p[n].shape), d_, m_, v_)
    return loss, grad_x, out


def kernel(x, norm_ffn1, ffn1_w_in, ffn1_w_out, norm_mix, w_in_mix, lru_conv_w, lru_conv_b, lru_gate_w, lru_gate_b, lru_lambda, attn_rpb, lru_out_norm, attn_out_norm, w_out_mix, norm_ffn2, ffn2_w_in, ffn2_w_out, norm_final, loss_target, m_norm_ffn1, m_ffn1_w_in, m_ffn1_w_out, m_norm_mix, m_w_in_mix, m_lru_conv_w, m_lru_conv_b, m_lru_gate_w, m_lru_gate_b, m_lru_lambda, m_attn_rpb, m_lru_out_norm, m_attn_out_norm, m_w_out_mix, m_norm_ffn2, m_ffn2_w_in, m_ffn2_w_out, m_norm_final, v_norm_ffn1, v_ffn1_w_in, v_ffn1_w_out, v_norm_mix, v_w_in_mix, v_lru_conv_w, v_lru_conv_b, v_lru_gate_w, v_lru_gate_b, v_lru_lambda, v_attn_rpb, v_lru_out_norm, v_attn_out_norm, v_w_out_mix, v_norm_ffn2, v_ffn2_w_in, v_ffn2_w_out, v_norm_final):
    given = dict(locals())
    drop_layer = lambda n, a: a if n == "norm_final" else a[0]
    p = {n: drop_layer(n, given[n]) for n in WEIGHTS}
    m = {n: drop_layer(n, given["m_" + n]) for n in WEIGHTS}
    v = {n: drop_layer(n, given["v_" + n]) for n in WEIGHTS}
    loss, grad_x, out = _step(x[0], loss_target[0], p, m, v)
    shaped = lambda n, a: a.reshape(given[n].shape)
    return (loss, grad_x[None], *[shaped(n, out[n][k]) for k in range(4) for n in WEIGHTS])
```

```python
import math

import numpy as np
import jax
import jax.numpy as jnp
from jax import lax
from jax.experimental import pallas as pl
from jax.experimental.pallas import tpu as pltpu

F32 = jnp.float32
BF16 = jnp.bfloat16
SDS = jax.ShapeDtypeStruct

N_DEV = 8
N_CHIP = 4
NORM_EPS = 1e-6
RG_C = 8.0
CONV_WIDTH = 4
HEAD_DIM = 64
GRID_W = 64
WIN_ROWS = 8
WIN_COLS = 16
NEG = -1e30

ADAM_LR = 0.001
ADAM_B1 = 0.9
ADAM_B2 = 0.999
ADAM_EPS = 1e-08
ADAM_WD = 0.01
ADAM_STEP = 10

LANES = 128
SUBLANES = 8
VMEM_LIMIT = 56 * 1024 * 1024

NT = (((1,), (1,)), ((), ()))
TN = (((0,), (0,)), ((), ()))
ANY = pl.BlockSpec(memory_space=pl.ANY)
WHOLE = pl.BlockSpec(memory_space=pltpu.VMEM)
MESH = pl.DeviceIdType.MESH


def _sigmoid(x):
    return 1.0 / (1.0 + jnp.exp(-x))


def _gelu_parts(x):
    c = math.sqrt(2.0 / math.pi)
    t = jnp.tanh(c * (x + 0.044715 * (x * x * x)))
    gelu = 0.5 * x * (1.0 + t)
    dgelu = 0.5 * (1.0 + t) + 0.5 * x * (1.0 - t * t) * (c * (1.0 + 3.0 * 0.044715 * (x * x)))
    return gelu, dgelu


def _expm1(x):
    poly = x * (1.0 + x * (1.0 / 2) * (1.0 + x * (1.0 / 3) * (1.0 + x * (1.0 / 4) * (1.0 + x * (1.0 / 5) * (1.0 + x * (1.0 / 6))))))
    return jnp.where(jnp.abs(x) < 0.25, poly, jnp.exp(x) - 1.0)


def _softplus(x):
    return jnp.maximum(x, 0.0) + jnp.log1p(jnp.exp(-jnp.abs(x)))


class _Piece:
    N_REMOTE = {"gather": 7, "to_sibling": N_CHIP, "to_chips": 3}
    N_LOCAL = {"gather": 1, "to_sibling": 0, "to_chips": 1}

    def __init__(self, kind, src, dest, lo, hi):
        self.kind, self.src, self.dest, self.lo, self.hi = kind, src, dest, lo, hi


RELAY_AT = 60


class _Job:
    def __init__(self, pieces):
        self.pieces = list(pieces)
        self.ins = [p.src for p in self.pieces]
        self.out_shapes = [SDS(p.dest.shape, p.dest.dtype) for p in self.pieces]
        self.aliased = [i for i, p in enumerate(self.pieces) if not isinstance(p.dest, SDS)]
        self.n_remote = sum(_Piece.N_REMOTE[p.kind] for p in self.pieces)
        self.n_local = max(sum(_Piece.N_LOCAL[p.kind] for p in self.pieces), 1)

    def _each(self, step, ins, outs, send_sems, recv_sems, local_sems):
        remote = local = 0
        for p, src, dst in zip(self.pieces, ins, outs):
            _EXCHANGES[p.kind](step, p, src, dst, send_sems, recv_sems, local_sems, remote, local)
            remote += _Piece.N_REMOTE[p.kind]
            local += _Piece.N_LOCAL[p.kind]

    def start(self, *refs):
        self._each("start", *refs)

    def relay(self, *refs):
        self._each("relay", *refs)

    def finish(self, *refs):
        self._each("finish", *refs)


def _call(body, *, name, args, out_shape, in_specs, out_specs, grid=(), scratch_shapes=(), aliases=None, job=None):
    single = not isinstance(out_shape, (tuple, list))
    out_shape = (out_shape,) if single else tuple(out_shape)
    out_specs = (out_specs,) if single else tuple(out_specs)
    aliases = dict(aliases or {})
    params = pltpu.CompilerParams(dimension_semantics=("arbitrary",) * len(grid) if grid else None,
                                  vmem_limit_bytes=VMEM_LIMIT)
    if job is None:
        res = pl.pallas_call(body, out_shape=out_shape, grid=grid, in_specs=list(in_specs), out_specs=out_specs,
                             scratch_shapes=list(scratch_shapes), input_output_aliases=aliases, name=name,
                             compiler_params=params)(*args)
        return res[0] if single else res

    n_in, n_out, n_scr = len(args), len(out_shape), len(scratch_shapes)
    j_in, j_out, j_alias = len(job.ins), len(job.out_shapes), len(job.aliased)

    def hosted(*refs):
        ins, refs = refs[:n_in], refs[n_in:]
        j_ins, refs = refs[:j_in], refs[j_in + j_alias:]
        outs, refs = refs[:n_out], refs[n_out:]
        j_outs, refs = refs[:j_out], refs[j_out:]
        scr, sems = refs[:n_scr], refs[n_scr:]
        if grid:
            step = 0
            for axis, size in enumerate(grid):
                step = step * size + pl.program_id(axis)
            steps = math.prod(grid)
            pl.when(step == 0)(lambda: job.start(j_ins, j_outs, *sems))
            body(*ins, *outs, *scr)
            pl.when(step == min(RELAY_AT * steps // 100, steps - 1))(lambda: job.relay(j_ins, j_outs, *sems))
            pl.when(step == steps - 1)(lambda: job.finish(j_ins, j_outs, *sems))
        else:
            job.start(j_ins, j_outs, *sems)
            body(*ins, *outs, *scr)
            job.relay(j_ins, j_outs, *sems)
            job.finish(j_ins, j_outs, *sems)

    res = pl.pallas_call(
        hosted, out_shape=out_shape + tuple(job.out_shapes), grid=grid,
        in_specs=list(in_specs) + [ANY] * (j_in + j_alias), out_specs=out_specs + (ANY,) * j_out,
        scratch_shapes=list(scratch_shapes) + [pltpu.SemaphoreType.DMA((job.n_remote,)),
                                               pltpu.SemaphoreType.DMA((job.n_remote,)),
                                               pltpu.SemaphoreType.DMA((job.n_local,))],
        input_output_aliases={**aliases, **{n_in + j_in + k: n_out + i for k, i in enumerate(job.aliased)}},
        name=name, compiler_params=params)(*args, *job.ins, *[job.pieces[i].dest for i in job.aliased])
    own, carried = res[:n_out], res[n_out:]
    return (own[0] if single else own), carried


def _run_job(job, name):
    return _call(lambda: None, name=name, args=[], out_shape=(), in_specs=[], out_specs=(), job=job)[1]


def _position():
    return lax.axis_index("x"), lax.axis_index("y"), lax.axis_index("c")


def _flat(px, py, pc):
    return 4 * px + 2 * py + pc


def _gather_exchange(step, p, src, dst, send_sems, recv_sems, local_sems, r0, l0):
    x, y, c = _position()
    me, sibling = (x, y, c), (x, y, 1 - c)
    along_x, along_y, diagonal = (1 - x, y), (x, 1 - y), (1 - x, 1 - y)
    south = c == 0
    passed_on = (jnp.where(south, 1 - x, x), jnp.where(south, y, 1 - y))
    passed_to = (jnp.where(south, x, 1 - x), jnp.where(south, 1 - y, y))
    rb, n_rows = p.src.shape[0], p.hi - p.lo
    mine = src.at[pl.ds(p.lo, n_rows), :]

    def rows(block):
        return dst.at[pl.ds(_flat(*block) * rb + p.lo, n_rows), :]

    def copy(k, block, to, own=False):
        return pltpu.make_async_remote_copy(
            src_ref=mine if own else rows(block), dst_ref=rows(block),
            send_sem=send_sems.at[r0 + k], recv_sem=recv_sems.at[r0 + k], device_id=to, device_id_type=MESH)

    local = pltpu.make_async_copy(mine, rows(me), local_sems.at[l0])
    if step == "start":
        local.start()
        copy(0, me, sibling, own=True).start()
        copy(1, me, (*along_x, c), own=True).start()
        copy(2, me, (*along_y, c), own=True).start()
    elif step == "relay":
        copy(1, (*along_x, c), me).wait_recv()
        copy(2, (*along_y, c), me).wait_recv()
        copy(3, (*passed_on, c), (*passed_to, c)).start()
        copy(4, (*along_x, c), sibling).start()
        copy(5, (*along_y, c), sibling).start()
    else:
        copy(3, (*diagonal, c), me).wait_recv()
        copy(6, (*diagonal, c), sibling).start()
        copy(0, sibling, me).wait_recv()
        copy(4, (*along_x, 1 - c), me).wait_recv()
        copy(5, (*along_y, 1 - c), me).wait_recv()
        copy(6, (*diagonal, 1 - c), me).wait_recv()
        copy(0, me, sibling, own=True).wait_send()
        copy(1, me, (*along_x, c), own=True).wait_send()
        copy(2, me, (*along_y, c), own=True).wait_send()
        copy(3, (*passed_on, c), (*passed_to, c)).wait_send()
        copy(4, (*along_x, c), sibling).wait_send()
        copy(5, (*along_y, c), sibling).wait_send()
        copy(6, (*diagonal, c), sibling).wait_send()
        local.wait()


def _sibling_exchange(step, p, src, dst, send_sems, recv_sems, local_sems, r0, l0):
    x, y, c = _position()
    rb, n_rows = p.src.shape[0] // N_DEV, p.hi - p.lo
    for q in range(N_CHIP):
        copy = pltpu.make_async_remote_copy(
            src_ref=src.at[pl.ds((2 * q + 1 - c) * rb + p.lo, n_rows), :],
            dst_ref=dst.at[pl.ds(q * rb + p.lo, n_rows), :],
            send_sem=send_sems.at[r0 + q], recv_sem=recv_sems.at[r0 + q], device_id=(x, y, 1 - c), device_id_type=MESH)
        if step == "start":
            copy.start()
        elif step == "finish":
            copy.wait()


CHIP_FLIPS = [(1, 0), (0, 1), (1, 1)]


def _chips_exchange(step, p, src, dst, send_sems, recv_sems, local_sems, r0, l0):
    x, y, c = _position()
    rb, n_rows = p.src.shape[0] // N_CHIP, p.hi - p.lo

    def slot(ref, px, py):
        return ref.at[pl.ds((2 * px + py) * rb + p.lo, n_rows), :]

    def copy(k, landing=False):
        px = 1 - x if CHIP_FLIPS[k][0] else x
        py = 1 - y if CHIP_FLIPS[k][1] else y
        return pltpu.make_async_remote_copy(
            src_ref=slot(dst, px, py) if landing else slot(src, px, py),
            dst_ref=slot(dst, px, py) if landing else slot(dst, x, y),
            send_sem=send_sems.at[r0 + k], recv_sem=recv_sems.at[r0 + k], device_id=(px, py, c), device_id_type=MESH)

    local = pltpu.make_async_copy(slot(src, x, y), slot(dst, x, y), local_sems.at[l0])
    if step == "start":
        local.start()
        for k in range(3):
            copy(k).start()
    elif step == "finish":
        for k in range(3):
            copy(k, landing=True).wait_recv()
        for k in range(3):
            copy(k).wait_send()
        local.wait()


_EXCHANGES = {"gather": _gather_exchange, "to_sibling": _sibling_exchange, "to_chips": _chips_exchange}


def _gathered(shard):
    return SDS((N_DEV * shard.shape[0], shard.shape[1]), shard.dtype)


def _split(rows, parts):
    cuts = [rows * k // parts // 16 * 16 for k in range(parts)] + [rows]
    return list(zip(cuts[:-1], cuts[1:]))


def _pair_sum(g, from_sibling, name):
    rb, n = g.shape[0] // N_DEV, g.shape[1]
    tr = rb if rb * n * 2 <= 3 * 1024 * 1024 else rb // 2
    core = lax.axis_index("c").astype(jnp.int32).reshape(1)

    def body(c_ref, g_ref, r_ref, o_ref):
        o_ref[...] = (g_ref[...].astype(F32) + r_ref[...].astype(F32)).astype(BF16)

    grid_spec = pltpu.PrefetchScalarGridSpec(
        num_scalar_prefetch=1, grid=(N_CHIP, rb // tr),
        in_specs=[pl.BlockSpec((None, None, tr, n), lambda q, i, c_ref: (q, c_ref[0], i, 0)),
                  pl.BlockSpec((None, tr, n), lambda q, i, c_ref: (q, i, 0))],
        out_specs=pl.BlockSpec((None, tr, n), lambda q, i, c_ref: (q, i, 0)))
    out = pl.pallas_call(
        body, grid_spec=grid_spec, out_shape=SDS((N_CHIP, rb, n), BF16), name=name,
        compiler_params=pltpu.CompilerParams(dimension_semantics=("arbitrary",) * 2, vmem_limit_bytes=VMEM_LIMIT))(
            core, g.reshape(N_CHIP, 2, rb, n), from_sibling.reshape(N_CHIP, rb, n))
    return out.reshape(N_CHIP * rb, n)


SEM = pl.BlockSpec(memory_space=pltpu.SEMAPHORE)
IN_HBM = pl.BlockSpec(memory_space=pltpu.HBM)
SIDE_EFFECT = pltpu.SideEffectType.DATAFLOW_SIDE_EFFECTING


def _own_slot(partials, name):
    rb = partials.shape[0] // N_CHIP

    def body(src_ref, dst_ref, sem):
        x, y, _ = _position()
        mine = pl.ds((2 * x + y) * rb, rb)
        copy = pltpu.make_async_copy(src_ref.at[mine, :], dst_ref.at[mine, :], sem)
        copy.start()
        copy.wait()

    return pl.pallas_call(body, out_shape=SDS(partials.shape, partials.dtype), in_specs=[ANY], out_specs=ANY,
                          scratch_shapes=[pltpu.SemaphoreType.DMA(())], name=name)(partials)


def _chip_copies(src_ref, land_ref, sems, rb):
    x, y, c = _position()
    copies = []
    for k, (fx, fy) in enumerate(CHIP_FLIPS):
        px, py = (1 - x if fx else x), (1 - y if fy else y)
        copies.append(pltpu.make_async_remote_copy(
            src_ref=src_ref.at[pl.ds((2 * px + py) * rb, rb), :], dst_ref=land_ref.at[pl.ds((2 * x + y) * rb, rb), :],
            send_sem=sems[2 * k], recv_sem=sems[2 * k + 1], device_id=(px, py, c), device_id_type=MESH))
    return copies


def _to_chips_start(partials, land, name):
    rb = partials.shape[0] // N_CHIP

    def body(src_ref, land_ref, *rest):
        sems, token = rest[:6], rest[-1]
        for copy in _chip_copies(src_ref, land_ref, sems, rb):
            copy.start()
        token[...] = jnp.zeros_like(token)

    hbm = pltpu.HBM(partials.shape, partials.dtype)
    res = pl.pallas_call(
        body, name=name, out_shape=(pltpu.SemaphoreType.DMA(()),) * 6 + (hbm, hbm, SDS((SUBLANES, LANES), F32)),
        in_specs=(IN_HBM, IN_HBM), out_specs=(SEM,) * 6 + (IN_HBM, IN_HBM, WHOLE), input_output_aliases={0: 6, 1: 7},
        compiler_params=pltpu.CompilerParams(has_side_effects=SIDE_EFFECT))(
            pltpu.with_memory_space_constraint(partials, pltpu.HBM), pltpu.with_memory_space_constraint(land, pltpu.HBM))
    return res[:6], res[6], res[7], res[8]


def _to_chips_wait(sems, partials, land, after, name):
    rb = partials.shape[0] // N_CHIP

    def body(src_ref, land_ref, *rest):
        for copy in _chip_copies(src_ref, land_ref, rest[:6], rb):
            copy.wait_send()
            copy.wait_recv()

    hbm = pltpu.HBM(partials.shape, partials.dtype)
    return pl.pallas_call(
        body, name=name, out_shape=(hbm, hbm), in_specs=(IN_HBM, IN_HBM) + (SEM,) * 6 + (ANY,) * len(after),
        out_specs=(IN_HBM, IN_HBM), input_output_aliases={0: 0, 1: 1},
        compiler_params=pltpu.CompilerParams(has_side_effects=SIDE_EFFECT))(partials, land, *sems, *after)[1]


def _sum_devices(gathered, name):
    r = gathered.shape[0] // N_DEV

    def body(g_ref, o_ref):
        acc = g_ref[0]
        for s in range(1, N_DEV):
            acc = acc + g_ref[s]
        o_ref[...] = acc

    return _call(body, name=name, args=[gathered.reshape(N_DEV, r, LANES)], out_shape=SDS((r, LANES), F32),
                 in_specs=[WHOLE], out_specs=WHOLE)


def _cast_rows(w, name):
    def body(w_ref, o_ref):
        o_ref[...] = w_ref[...].astype(BF16)

    return _call(body, name=name, args=[w], out_shape=SDS(w.shape, BF16), in_specs=[WHOLE], out_specs=WHOLE)


def _cast_transposed(w, name):
    d, n = w.shape
    td = 512

    def body(w_ref, o_ref):
        o_ref[...] = w_ref[...].T.astype(BF16)

    return _call(body, name=name, args=[w], out_shape=SDS((n, d), BF16), grid=(d // td,),
                 in_specs=[pl.BlockSpec((td, n), lambda i: (i, 0))], out_specs=pl.BlockSpec((n, td), lambda i: (0, i)))


ROW_TILE = 256


def _rmsnorm_fwd(h, gain, name):
    t, d = h.shape

    def body(h_ref, g_ref, u_ref):
        x = h_ref[...]
        u_ref[...] = (x * lax.rsqrt(jnp.mean(x * x, axis=-1, keepdims=True) + NORM_EPS) * g_ref[...]).astype(BF16)

    row = pl.BlockSpec((ROW_TILE, d), lambda i: (i, 0))
    return _call(body, name=name, args=[h, gain], out_shape=SDS((t, d), BF16), grid=(t // ROW_TILE,),
                 in_specs=[row, pl.BlockSpec((1, d), lambda i: (0, 0))], out_specs=row)


def _rms_bwd_math(x, gain, dy):
    rstd = lax.rsqrt(jnp.mean(x * x, axis=-1, keepdims=True) + NORM_EPS)
    xhat = x * rstd
    dxh = dy * gain
    dx = rstd * (dxh - xhat * jnp.mean(dxh * xhat, axis=-1, keepdims=True))
    return dx, jnp.sum(dy * xhat, axis=0, keepdims=True)


def _rmsnorm_bwd(du, h, gain, resid, bf_scale, name, job=None):
    t, d = h.shape

    def body(du_ref, h_ref, g_ref, r_ref, dh_ref, dhb_ref, dg_ref):
        @pl.when(pl.program_id(0) == 0)
        def _():
            dg_ref[...] = jnp.zeros_like(dg_ref)

        dx, dg = _rms_bwd_math(h_ref[...], g_ref[...], du_ref[...])
        dh = r_ref[...] + dx
        dh_ref[...] = dh
        dhb_ref[...] = (bf_scale * dh).astype(BF16)
        dg_ref[...] += dg

    row = pl.BlockSpec((ROW_TILE, d), lambda i: (i, 0))
    vec = pl.BlockSpec((1, d), lambda i: (0, 0))
    return _call(body, name=name, args=[du, h, gain, resid],
                 out_shape=(SDS((t, d), F32), SDS((t, d), BF16), SDS((1, d), F32)), grid=(t // ROW_TILE,),
                 in_specs=[row, row, vec, row], out_specs=(row, row, vec), job=job)


def _final_loss(h, gain, target, name):
    t, d = h.shape

    def body(h_ref, g_ref, t_ref, dh_ref, dhb_ref, loss_ref, dg_ref):
        @pl.when(pl.program_id(0) == 0)
        def _():
            dg_ref[...] = jnp.zeros_like(dg_ref)
            loss_ref[...] = jnp.zeros_like(loss_ref)

        x = h_ref[...]
        gain = g_ref[...]
        out = x * lax.rsqrt(jnp.mean(x * x, axis=-1, keepdims=True) + NORM_EPS) * gain
        err = out - t_ref[...]
        loss_ref[...] += 0.5 * jnp.sum(jnp.mean(err * err, axis=-1, keepdims=True), axis=0, keepdims=True)
        dx, dg = _rms_bwd_math(x, gain, err * (1.0 / d))
        dh_ref[...] = dx
        dhb_ref[...] = (0.5 * dx).astype(BF16)
        dg_ref[...] += dg

    row = pl.BlockSpec((ROW_TILE, d), lambda i: (i, 0))
    vec = pl.BlockSpec((1, d), lambda i: (0, 0))
    one = pl.BlockSpec((SUBLANES, LANES), lambda i: (0, 0))
    return _call(body, name=name, args=[h, gain, target],
                 out_shape=(SDS((t, d), F32), SDS((t, d), BF16), SDS((SUBLANES, LANES), F32), SDS((1, d), F32)),
                 grid=(t // ROW_TILE,), in_specs=[row, vec, row], out_specs=(row, row, one, vec))


def _mixnorm_fwd(ya, yb, ga, gb, name):
    t, c = ya.shape

    def body(ya_ref, yb_ref, ga_ref, gb_ref, y_ref, yt_ref):
        for k, (src, g_ref) in enumerate(((ya_ref, ga_ref), (yb_ref, gb_ref))):
            x = src[...]
            u = x * lax.rsqrt(jnp.mean(x * x, axis=-1, keepdims=True) + NORM_EPS) * g_ref[...]
            y_ref[:, k * c:(k + 1) * c] = u.astype(BF16)
            yt_ref[k * c:(k + 1) * c, :] = u.T.astype(BF16)

    row = pl.BlockSpec((ROW_TILE, c), lambda i: (i, 0))
    vec = pl.BlockSpec((1, c), lambda i: (0, 0))
    return _call(body, name=name, args=[ya, yb, ga, gb],
                 out_shape=(SDS((t, 2 * c), BF16), SDS((2 * c, t), BF16)), grid=(t // ROW_TILE,),
                 in_specs=[row, row, vec, vec],
                 out_specs=(pl.BlockSpec((ROW_TILE, 2 * c), lambda i: (i, 0)),
                            pl.BlockSpec((2 * c, ROW_TILE), lambda i: (0, i))))


def _mixnorm_bwd(dy, ya, yb, ga, gb, name):
    t, c = ya.shape

    def body(dy_ref, ya_ref, yb_ref, ga_ref, gb_ref, dya_ref, dyb_ref, dga_ref, dgb_ref):
        @pl.when(pl.program_id(0) == 0)
        def _():
            dga_ref[...] = jnp.zeros_like(dga_ref)
            dgb_ref[...] = jnp.zeros_like(dgb_ref)

        dxa, dga = _rms_bwd_math(ya_ref[...], ga_ref[...], dy_ref[:, :c])
        dxb, dgb = _rms_bwd_math(yb_ref[...], gb_ref[...], dy_ref[:, c:])
        dya_ref[...] = dxa
        dyb_ref[...] = dxb
        dga_ref[...] += dga
        dgb_ref[...] += dgb

    row = pl.BlockSpec((ROW_TILE, c), lambda i: (i, 0))
    vec = pl.BlockSpec((1, c), lambda i: (0, 0))
    return _call(body, name=name, args=[dy, ya, yb, ga, gb],
                 out_shape=(SDS((t, c), F32), SDS((t, c), F32), SDS((1, c), F32), SDS((1, c), F32)),
                 grid=(t // ROW_TILE,),
                 in_specs=[pl.BlockSpec((ROW_TILE, 2 * c), lambda i: (i, 0)), row, row, vec, vec],
                 out_specs=(row, row, vec, vec))


def _tile(n, want):
    return max(t for t in range(LANES, min(n, want) + 1, LANES) if n % t == 0)


def _mm(a, b, *, nt, out_dtype, tm, tn, name, residual=None, scale=None, lead=None, out_rows=None, row_offset=0,
        into=None, job=None):
    parts = list(a) if isinstance(a, (list, tuple)) else [a]
    m = parts[0].shape[-2]
    widths = [p.shape[-1] for p in parts]
    k = sum(widths)
    n = b.shape[0] if nt else b.shape[1]
    tm, tn = _tile(math.gcd(m, row_offset), tm), _tile(n, tn)
    out_rows = m if out_rows is None else out_rows

    def body(*refs):
        a_refs, b_ref, rest = refs[:len(parts)], refs[len(parts)], refs[len(parts) + 1:]
        o_ref = rest[-1]
        out, at = None, 0
        for a_ref, width in zip(a_refs, widths):
            av = a_ref[...].astype(BF16)
            if nt:
                term = lax.dot_general(av, b_ref[:, at:at + width].astype(BF16), NT, preferred_element_type=F32)
            else:
                term = jnp.dot(av, b_ref[at:at + width, :].astype(BF16), preferred_element_type=F32)
            out = term if out is None else out + term
            at += width
        if residual is not None:
            out = rest[0][...] + (out if scale is None else scale * out)
        o_ref[...] = out.astype(out_dtype)

    a_specs =([pl.BlockSpec((tm, width), lambda i, j: (i, 0)) for width in widths] if lead is None
               else [pl.BlockSpec((None, tm, k), lambda i, j: (lead, i, 0))])
    in_specs = a_specs + [pl.BlockSpec((tn, k), lambda i, j: (j, 0)) if nt else pl.BlockSpec((k, tn), lambda i, j: (0, j))]
    args, aliases = parts + [b], {}
    if residual is not None:
        in_specs.append(pl.BlockSpec((tm, tn), lambda i, j: (i, j)))
        args.append(residual)
    if into is not None:
        in_specs.append(ANY)
        aliases[len(args)] = 0
        args.append(into)
    return _call(body, name=name, args=args, out_shape=SDS((out_rows, n), out_dtype), grid=(m // tm, n // tn),
                 in_specs=in_specs, out_specs=pl.BlockSpec((tm, tn), lambda i, j: (row_offset // tm + i, j)),
                 aliases=aliases, job=job)


FFN_TM = 512
FFN_HB = 512


def _ffn_hidden(u, w_in_t, name, job=None):
    t, d = u.shape
    f = w_in_t.shape[0] // 2

    def body(u_ref, w_ref, g_ref, up_ref, hid_ref, hid_t_ref):
        uu = u_ref[...]
        g = lax.dot_general(uu, w_ref[0], NT, preferred_element_type=F32)
        up = lax.dot_general(uu, w_ref[1], NT, preferred_element_type=F32)
        g_ref[...] = g.astype(BF16)
        up_ref[...] = up.astype(BF16)
        hid = (g * _sigmoid(g)) * up
        hid_ref[...] = hid.astype(BF16)
        hid_t_ref[...] = hid.T.astype(BF16)

    pre = pl.BlockSpec((FFN_TM, FFN_HB), lambda i, k: (i, k))
    return _call(body, name=name, args=[u, w_in_t.reshape(2, f, d)],
                 out_shape=(SDS((t, f), BF16), SDS((t, f), BF16), SDS((t, f), BF16), SDS((f, t), BF16)),
                 grid=(t // FFN_TM, f // FFN_HB),
                 in_specs=[pl.BlockSpec((FFN_TM, d), lambda i, k: (i, 0)),
                           pl.BlockSpec((2, FFN_HB, d), lambda i, k: (0, k, 0))],
                 out_specs=(pre, pre, pre, pl.BlockSpec((FFN_HB, FFN_TM), lambda i, k: (k, i))), job=job)


def _ffn_bwd(dfb, gpre, upre, w_in_t, w_out, name, job=None):
    t, d = dfb.shape
    f = w_out.shape[0]
    nk = f // FFN_HB

    def body(df_ref, g_ref, up_ref, w_ref, wo_ref, du_ref, da_t_ref, acc):
        k = pl.program_id(1)

        @pl.when(k == 0)
        def _():
            acc[...] = jnp.zeros_like(acc)

        dhid = lax.dot_general(df_ref[...], wo_ref[...], NT, preferred_element_type=F32)
        g, up = g_ref[...].astype(F32), up_ref[...].astype(F32)
        sig = _sigmoid(g)
        silu = g * sig
        dup = dhid * silu
        dg = dhid * up * (sig * (1.0 + g * (1.0 - sig)))
        da_t_ref[0] = dg.T.astype(BF16)
        da_t_ref[1] = dup.T.astype(BF16)
        acc[...] += (jnp.dot(dg.astype(BF16), w_ref[0], preferred_element_type=F32)
                     + jnp.dot(dup.astype(BF16), w_ref[1], preferred_element_type=F32))

        @pl.when(k == nk - 1)
        def _():
            du_ref[...] = acc[...]

    tok = pl.BlockSpec((FFN_TM, d), lambda i, k: (i, 0))
    pre = pl.BlockSpec((FFN_TM, FFN_HB), lambda i, k: (i, k))
    return _call(body, name=name, args=[dfb, gpre, upre, w_in_t.reshape(2, f, d), w_out],
                 out_shape=(SDS((t, d), F32), SDS((2, f, t), BF16)), grid=(t // FFN_TM, nk),
                 in_specs=[tok, pre, pre, pl.BlockSpec((2, FFN_HB, d), lambda i, k: (0, k, 0)),
                           pl.BlockSpec((FFN_HB, d), lambda i, k: (k, 0))],
                 out_specs=(tok, pl.BlockSpec((2, FFN_HB, FFN_TM), lambda i, k: (0, k, i))),
                 scratch_shapes=[pltpu.VMEM((FFN_TM, d), F32)], job=job)


CH = LANES
PAD = SUBLANES


def _lru_gates(xc, gw_ref, gb_ref, lam_ref, z):
    xcb = xc.astype(BF16)
    r = _sigmoid(jnp.dot(xcb, gw_ref[2 * z], preferred_element_type=F32) + gb_ref[pl.ds(2 * z, 1), :])
    i = _sigmoid(jnp.dot(xcb, gw_ref[2 * z + 1], preferred_element_type=F32) + gb_ref[pl.ds(2 * z + 1, 1), :])
    sp = _softplus(-lam_ref[pl.ds(z, 1), :])
    log_a = (-RG_C * r) * sp
    a = jnp.exp(log_a)
    mult = jnp.sqrt(-_expm1(2.0 * log_a))
    return r, i, sp, a, mult


def _conv(xpad, cw_ref, cb_ref, t):
    xc = cb_ref[...] + cw_ref[pl.ds(0, 1), :] * xpad[pl.ds(PAD - 2, t), :]
    for j in range(1, CONV_WIDTH):
        xc = xc + cw_ref[pl.ds(j, 1), :] * xpad[pl.ds(PAD - 2 + j, t), :]
    return xc


def _fill_padded(pad_ref, value, t):
    pad_ref[pl.ds(0, PAD), :] = jnp.zeros((PAD, CH), F32)
    pad_ref[pl.ds(PAD + t, PAD), :] = jnp.zeros((PAD, CH), F32)
    pad_ref[pl.ds(PAD, t), :] = value


def _scan_pair(t, a_up, b_up, out_up, a_down, b_down, out_down):
    row = lax.broadcasted_iota(jnp.int32, (SUBLANES, CH), 0)

    def compose(a, b, rising):
        for dist in (1, 2, 4):
            shift = dist if rising else SUBLANES - dist
            keep = (row >= dist) if rising else (row < SUBLANES - dist)
            b = jnp.where(keep, b + a * pltpu.roll(b, shift, axis=0), b)
            a = jnp.where(keep, a * pltpu.roll(a, shift, axis=0), a)
        return a, b

    def step(tt, carry):
        hu, hd = carry
        lo = pl.ds(pl.multiple_of(tt * SUBLANES, SUBLANES), SUBLANES)
        hi = pl.ds(pl.multiple_of(t - SUBLANES - tt * SUBLANES, SUBLANES), SUBLANES)
        a, b = compose(a_up[lo, :], b_up[lo, :], True)
        up = b + a * hu
        out_up[lo, :] = up
        a, b = compose(a_down[hi, :], b_down[hi, :], False)
        down = b + a * hd
        out_down[hi, :] = down
        return up[SUBLANES - 1:, :], down[:1, :]

    zero = jnp.zeros((1, CH), F32)
    lax.fori_loop(0, t // SUBLANES, step, (zero, zero), unroll=2)


def _lru_fwd(proj, cw, cb, gw, gb, lam, name, job=None):
    t = proj.shape[0]
    c = cw.shape[1]
    ncb = c // CH

    def body(x_ref, g_ref, cw_ref, cb_ref, gw_ref, gb_ref, lam_ref, ya_ref, hf_ref, hb_ref, xpad, a0, b0, a1, b1):
        _fill_padded(xpad, x_ref[...], t)
        xc = _conv(xpad, cw_ref, cb_ref, t)
        for z, (a_s, b_s) in enumerate(((a0, b0), (a1, b1))):
            _, i, _, a, mult = _lru_gates(xc, gw_ref, gb_ref, lam_ref, z)
            a_s[...] = a
            b_s[...] = mult * (i * xc)
        _scan_pair(t, a0, b0, hf_ref, a1, b1, hb_ref)
        gelu, _ = _gelu_parts(g_ref[...])
        ya_ref[...] = gelu * (hf_ref[...] + hb_ref[...])

    col = lambda off: pl.BlockSpec((t, CH), lambda i: (0, off + i))
    small = lambda rows: pl.BlockSpec((rows, CH), lambda i: (0, i))
    return _call(body, name=name, args=[proj, proj, cw, cb, gw, gb, lam], out_shape=(SDS((t, c), F32),) * 3,
                 grid=(ncb,),
                 in_specs=[col(0), col(ncb), small(CONV_WIDTH), small(1),
                           pl.BlockSpec((4, None, CH, CH), lambda i: (0, i, 0, 0)), small(4), small(2)],
                 out_specs=(col(0),) * 3,
                 scratch_shapes=[pltpu.VMEM((t + 2 * PAD, CH), F32)] + [pltpu.VMEM((t, CH), F32)] * 4, job=job)


def _lru_bwd(proj, cw, cb, gw, gb, lam, hf, hb, dya, name, job=None):
    t = proj.shape[0]
    c = cw.shape[1]
    ncb = c // CH

    def body(x_ref, g_ref, cw_ref, cb_ref, gw_ref, gb_ref, lam_ref, hf_ref, hb_ref, dya_ref,
             dx_ref, dg_ref, dt_ref, dcw_ref, dcb_ref, dgw_ref, dgb_ref, dlam_ref,
             xpad, hpad, dxc, a0, a1, dhs, dh0, dh1):
        _fill_padded(xpad, x_ref[...], t)
        xc = _conv(xpad, cw_ref, cb_ref, t)
        xcb = xc.astype(BF16)
        gates = [_lru_gates(xc, gw_ref, gb_ref, lam_ref, z) for z in range(2)]

        gelu, dgelu = _gelu_parts(g_ref[...])
        dya = dya_ref[...]
        dgate = dya * (hf_ref[...] + hb_ref[...]) * dgelu
        dg_ref[...] = dgate.astype(BF16)
        dt_ref[1] = dgate.T.astype(BF16)
        dhs[...] = dya * gelu

        _fill_padded(hpad, gates[0][3], t)
        a0[...] = hpad[pl.ds(PAD + 1, t), :]
        _fill_padded(hpad, gates[1][3], t)
        a1[...] = hpad[pl.ds(PAD - 1, t), :]
        _scan_pair(t, a1, dhs, dh1, a0, dhs, dh0)

        acc_dxc = jnp.zeros((t, CH), F32)
        for z, (h_ref, dh_ref, shift) in enumerate(((hf_ref, dh0, -1), (hb_ref, dh1, 1))):
            r, i, sp, a, mult = gates[z]
            _fill_padded(hpad, h_ref[...], t)
            h_nb = hpad[pl.ds(PAD + shift, t), :]
            db = dh_ref[...]
            da = db * h_nb
            d_i = db * mult * xc
            acc_dxc = acc_dxc + db * mult * i
            d_mult = db * i * xc
            d_la = da * a - d_mult * (a * a) / mult
            d_r = d_la * (-RG_C * sp)
            dlam_ref[pl.ds(z, 1), :] = (jnp.sum(d_la * (-RG_C * r), axis=0, keepdims=True)
                                        * (-_sigmoid(-lam_ref[pl.ds(z, 1), :])))
            for gate, d_pre in ((0, d_r * r * (1.0 - r)), (1, d_i * i * (1.0 - i))):
                zg = 2 * z + gate
                dgb_ref[pl.ds(zg, 1), :] = jnp.sum(d_pre, axis=0, keepdims=True)
                d_pre_b = d_pre.astype(BF16)
                dgw_ref[zg] = lax.dot_general(xcb, d_pre_b, TN, preferred_element_type=F32)
                acc_dxc = acc_dxc + lax.dot_general(d_pre_b, gw_ref[zg], NT, preferred_element_type=F32)

        dcb_ref[...] = jnp.sum(acc_dxc, axis=0, keepdims=True)
        for j in range(CONV_WIDTH):
            dcw_ref[pl.ds(j, 1), :] = jnp.sum(acc_dxc * xpad[pl.ds(PAD - 2 + j, t), :], axis=0, keepdims=True)
        _fill_padded(dxc, acc_dxc, t)
        dx = cw_ref[pl.ds(0, 1), :] * dxc[pl.ds(PAD + 2, t), :]
        for j in range(1, CONV_WIDTH):
            dx = dx + cw_ref[pl.ds(j, 1), :] * dxc[pl.ds(PAD + 2 - j, t), :]
        dx_ref[...] = dx.astype(BF16)
        dt_ref[0] = dx.T.astype(BF16)

    col = lambda off: pl.BlockSpec((t, CH), lambda i: (0, off + i))
    small = lambda rows: pl.BlockSpec((rows, CH), lambda i: (0, i))
    dense = pl.BlockSpec((4, None, CH, CH), lambda i: (0, i, 0, 0))
    padded = pltpu.VMEM((t + 2 * PAD, CH), F32)
    return _call(
        body, name=name, args=[proj, proj, cw, cb, gw, gb, lam, hf, hb, dya],
        out_shape=(SDS((t, c), BF16), SDS((t, c), BF16), SDS((2, c, t), BF16), SDS((CONV_WIDTH, c), F32),
                   SDS((1, c), F32), SDS((4, ncb, CH, CH), F32), SDS((4, c), F32), SDS((2, c), F32)),
        grid=(ncb,),
        in_specs=[col(0), col(ncb), small(CONV_WIDTH), small(1), dense, small(4), small(2), col(0), col(0), col(0)],
        out_specs=(col(0), col(0), pl.BlockSpec((2, CH, t), lambda i: (0, i, 0)), small(CONV_WIDTH), small(1),
                   dense, small(4), small(2)),
        scratch_shapes=[padded, padded, padded] + [pltpu.VMEM((t, CH), F32)] * 5, job=job)


Q_ROWS = 4
BAND_ROWS = WIN_ROWS + Q_ROWS
BAND_PAIRS = BAND_ROWS // 2
Q_BLOCK = Q_ROWS * GRID_W
BAND = BAND_ROWS * GRID_W
PAIR_W = 2 * GRID_W
N_BOTH = 2 * WIN_ROWS - 2
ENTRY_LEFT_OUT, ENTRY_RIGHT_OUT, ENTRY_OUT = N_BOTH, N_BOTH + 1, N_BOTH + 2
N_ENTRIES = N_BOTH + 3


def _bias_tables(rpb):
    cols = np.arange(GRID_W)
    start = np.clip(cols - WIN_COLS // 2, 0, GRID_W - WIN_COLS)
    valid = (cols[None, :] >= start[:, None]) & (cols[None, :] < start[:, None] + WIN_COLS)
    col_off = np.clip(cols[None, :] - cols[:, None] + WIN_COLS - 1, 0, 2 * WIN_COLS - 2)
    pick_col = jnp.asarray(np.eye(2 * WIN_COLS - 1, dtype=np.float32)[col_off] * valid[..., None])
    by_row = jnp.einsum("hrc,qkc->hrqk", rpb, pick_col, precision=lax.Precision.HIGHEST)
    by_row = jnp.where(jnp.asarray(valid)[None, None], by_row, NEG)
    out = jnp.full_like(by_row[:, :1], NEG)
    first_in, last_in = WIN_ROWS - 1 - WIN_ROWS // 2, 2 * (WIN_ROWS - 1) - WIN_ROWS // 2
    both = jnp.concatenate([by_row[:, :-1], by_row[:, 1:]], axis=-1)
    left_out = jnp.concatenate([out, by_row[:, first_in:first_in + 1]], axis=-1)
    right_out = jnp.concatenate([by_row[:, last_in:last_in + 1], out], axis=-1)
    return jnp.concatenate([both, left_out, right_out, jnp.concatenate([out, out], axis=-1)], axis=1)


def _band_start(m, rows):
    return jnp.clip(Q_ROWS * m - WIN_ROWS // 2, 0, rows - BAND_ROWS)


def _entry(r, key_row, rows):
    w0 = jnp.clip(r - WIN_ROWS // 2, 0, rows - WIN_ROWS)
    left = (key_row >= w0) & (key_row < w0 + WIN_ROWS)
    right = (key_row + 1 >= w0) & (key_row + 1 < w0 + WIN_ROWS)
    return jnp.where(left & right, key_row - r + WIN_ROWS - 1,
                     jnp.where(right, ENTRY_LEFT_OUT, jnp.where(left, ENTRY_RIGHT_OUT, ENTRY_OUT)))


def _transposed_pairs(dst, src_ref):
    for g in range(dst.shape[0]):
        dst[g] = src_ref[pl.ds(g * PAIR_W, PAIR_W), :].T.astype(BF16)


def _band_of(pairs_ref, first_pair, hh):
    heads = pl.ds(hh * HEAD_DIM, HEAD_DIM)
    return jnp.concatenate([pairs_ref[first_pair + g, heads, :] for g in range(BAND_PAIRS)], axis=1)


def _attn_block(qs, kt, tz_ref, hh, m, rows):
    rs = _band_start(m, rows)
    lanes = pl.ds(hh * HEAD_DIM, HEAD_DIM)
    qrows = pl.ds(pl.multiple_of(m * Q_BLOCK, Q_BLOCK), Q_BLOCK)
    band = pl.ds(pl.multiple_of(rs * GRID_W, PAIR_W), BAND)
    entries = [[_entry(Q_ROWS * m + i, rs + 2 * g, rows) for g in range(BAND_PAIRS)] for i in range(Q_ROWS)]
    bias = jnp.concatenate([jnp.concatenate([tz_ref[hh, e] for e in row], axis=1) for row in entries], axis=0)
    q = qs[qrows, lanes]
    s = jnp.dot(q, _band_of(kt, rs // 2, hh), preferred_element_type=F32) * (HEAD_DIM ** -0.5) + bias
    p = jnp.exp(s - jnp.max(s, axis=-1, keepdims=True))
    p = p / jnp.sum(p, axis=-1, keepdims=True)
    return q, p, qrows, band, lanes, entries, rs // 2


def _attn_fwd(proj, tables, width, name, job=None):
    t = proj.shape[0]
    rows = t // GRID_W
    npair = width // LANES
    first = (proj.shape[1] - 3 * width) // LANES

    def body(q_ref, k_ref, v_ref, tz_ref, o_ref, qs, vs, kt):
        qs[...] = q_ref[...].astype(BF16)
        vs[...] = v_ref[...].astype(BF16)
        _transposed_pairs(kt, k_ref)

        def block(m, carry):
            for hh in range(2):
                _, p, qrows, band, lanes, _, _ = _attn_block(qs, kt, tz_ref, hh, m, rows)
                o_ref[qrows, lanes] = jnp.dot(p.astype(BF16), vs[band, lanes], preferred_element_type=F32)
            return carry

        lax.fori_loop(0, rows // Q_ROWS, block, 0, unroll=2)

    col = lambda off: pl.BlockSpec((t, LANES), lambda i: (0, off + i))
    return _call(body, name=name, args=[proj, proj, proj, tables], out_shape=SDS((t, width), F32), grid=(npair,),
                 in_specs=[col(first), col(first + npair), col(first + 2 * npair),
                           pl.BlockSpec((2, N_ENTRIES, GRID_W, PAIR_W), lambda i: (i, 0, 0, 0))],
                 out_specs=col(0),
                 scratch_shapes=[pltpu.VMEM((t, LANES), BF16)] * 2 + [pltpu.VMEM((t // PAIR_W, LANES, PAIR_W), BF16)],
                 job=job)


def _attn_bwd(proj, tables, dyb, name, job=None):
    t, width = dyb.shape
    rows = t // GRID_W
    npair = width // LANES
    first = (proj.shape[1] - 3 * width) // LANES

    def body(q_ref, k_ref, v_ref, tz_ref, do_ref, dq_ref, dk_ref, dv_ref, dt_ref, dtz_ref, dq_s, dk_s, dv_s,
             qs, ks, vs, dos, kt, vt):
        qs[...] = q_ref[...].astype(BF16)
        ks[...] = k_ref[...].astype(BF16)
        vs[...] = v_ref[...].astype(BF16)
        dos[...] = do_ref[...].astype(BF16)
        _transposed_pairs(kt, k_ref)
        _transposed_pairs(vt, v_ref)
        dk_s[...] = jnp.zeros_like(dk_s)
        dv_s[...] = jnp.zeros_like(dv_s)
        dtz_ref[...] = jnp.zeros_like(dtz_ref)

        def block(m, carry):
            for hh in range(2):
                q, p, qrows, band, lanes, entries, first_pair = _attn_block(qs, kt, tz_ref, hh, m, rows)
                do = dos[qrows, lanes]
                dp = jnp.dot(do, _band_of(vt, first_pair, hh), preferred_element_type=F32)
                ds = p * (dp - jnp.sum(dp * p, axis=-1, keepdims=True))
                for i, row in enumerate(entries):
                    for g, e in enumerate(row):
                        dtz_ref[hh, e] += ds[i * GRID_W:(i + 1) * GRID_W, g * PAIR_W:(g + 1) * PAIR_W]
                dsb = (ds * (HEAD_DIM ** -0.5)).astype(BF16)
                dq_s[qrows, lanes] = jnp.dot(dsb, ks[band, lanes], preferred_element_type=F32)
                dk_s[band, lanes] += lax.dot_general(dsb, q, TN, preferred_element_type=F32)
                dv_s[band, lanes] += lax.dot_general(p.astype(BF16), do, TN, preferred_element_type=F32)
            return carry

        lax.fori_loop(0, rows // Q_ROWS, block, 0)
        for n, (src, dst) in enumerate(((dq_s, dq_ref), (dk_s, dk_ref), (dv_s, dv_ref))):
            val = src[...]
            dst[...] = val.astype(BF16)
            dt_ref[n] = val.T.astype(BF16)

    col = lambda off: pl.BlockSpec((t, LANES), lambda i: (0, off + i))
    table = pl.BlockSpec((2, N_ENTRIES, GRID_W, PAIR_W), lambda i: (i, 0, 0, 0))
    pairs = pltpu.VMEM((t // PAIR_W, LANES, PAIR_W), BF16)
    return _call(body, name=name, args=[proj, proj, proj, tables, dyb],
                 out_shape=(SDS((t, width), BF16),) * 3 + (SDS((3, width, t), BF16), SDS(tables.shape, F32)),
                 grid=(npair,),
                 in_specs=[col(first), col(first + npair), col(first + 2 * npair), table, col(0)],
                 out_specs=(col(0), col(0), col(0), pl.BlockSpec((3, LANES, t), lambda i: (0, i, 0)), table),
                 scratch_shapes=[pltpu.VMEM((t, LANES), F32)] * 3 + [pltpu.VMEM((t, LANES), BF16)] * 4 + [pairs, pairs],
                 job=job)


def _adamw_math(w, g, m, v):
    m = ADAM_B1 * m + (1.0 - ADAM_B1) * g
    v = ADAM_B2 * v + (1.0 - ADAM_B2) * (g * g)
    m_hat = m / (1.0 - ADAM_B1 ** ADAM_STEP)
    v_hat = v / (1.0 - ADAM_B2 ** ADAM_STEP)
    delta = -ADAM_LR * (m_hat / (jnp.sqrt(v_hat) + ADAM_EPS) + ADAM_WD * w)
    return delta, m, v


def _sum_partials(p_ref):
    g = p_ref[0].astype(F32)
    for s in range(1, N_CHIP):
        g = g + p_ref[s].astype(F32)
    return g


def _adamw_rows(w, partials, m, v, name, after=()):
    rb, n = w.shape
    tr = 64

    def body(w_ref, p_ref, m_ref, v_ref, *rest):
        g_ref, d_ref, nm_ref, nv_ref = rest[len(after):]
        g = _sum_partials(p_ref)
        g_ref[...] = g
        d_ref[...], nm_ref[...], nv_ref[...] = _adamw_math(w_ref[...], g, m_ref[...], v_ref[...])

    blk = pl.BlockSpec((tr, n), lambda i: (i, 0))
    return _call(body, name=name, args=[w, partials.reshape(N_CHIP, rb, n), m, v, *after],
                 out_shape=(SDS((rb, n), F32),) * 4, grid=(rb // tr,),
                 in_specs=[blk, pl.BlockSpec((N_CHIP, tr, n), lambda i: (0, i, 0)), blk, blk] + [ANY] * len(after),
                 out_specs=(blk,) * 4)


def _adamw_cols(w, partials, m, v, name, after=()):
    d, nb = w.shape
    td = 256

    def body(w_ref, p_ref, m_ref, v_ref, *rest):
        g_ref, d_ref, nm_ref, nv_ref = rest[len(after):]
        g = _sum_partials(p_ref).T
        g_ref[...] = g
        d_ref[...], nm_ref[...], nv_ref[...] = _adamw_math(w_ref[...], g, m_ref[...], v_ref[...])

    blk = pl.BlockSpec((td, nb), lambda i: (i, 0))
    return _call(body, name=name, args=[w, partials.reshape(N_CHIP, nb, d), m, v, *after],
                 out_shape=(SDS((d, nb), F32),) * 4, grid=(d // td,),
                 in_specs=[blk, pl.BlockSpec((N_CHIP, nb, td), lambda i: (0, 0, i)), blk, blk] + [ANY] * len(after),
                 out_specs=(blk,) * 4)


def _adamw_small(w, g, m, v, name):
    def body(w_ref, g_ref, m_ref, v_ref, d_ref, nm_ref, nv_ref):
        d_ref[...], nm_ref[...], nv_ref[...] = _adamw_math(w_ref[...], g_ref[...], m_ref[...], v_ref[...])

    return _call(body, name=name, args=[w, g, m, v], out_shape=(SDS(w.shape, F32),) * 3, in_specs=[WHOLE] * 4,
                 out_specs=(WHOLE,) * 3)


TILE = SUBLANES * LANES


def _pack(arrays):
    parts = []
    for a in arrays:
        flat = a.reshape(-1).astype(F32)
        flat = jnp.pad(flat, (0, -flat.size % TILE))
        parts.append(flat.reshape(-1, LANES))
    return jnp.concatenate(parts, axis=0)


def _unpack(pack, like):
    out, row = [], 0
    for a in like:
        n = int(np.prod(a.shape))
        nrows = -(-n // TILE) * SUBLANES
        out.append(pack[row:row + nrows].reshape(-1)[:n].reshape(a.shape))
        row += nrows
    return out


def _dense_gate_blocks(gate_w):
    w = gate_w.reshape(4, -1, 2, HEAD_DIM, HEAD_DIM)
    zero = jnp.zeros_like(w[:, :, 0])
    top = jnp.concatenate([w[:, :, 0], zero], axis=-1)
    bottom = jnp.concatenate([zero, w[:, :, 1]], axis=-1)
    return jnp.concatenate([top, bottom], axis=-2)


def _diag_gate_blocks(dense, shape):
    even = dense[:, :, :HEAD_DIM, :HEAD_DIM]
    odd = dense[:, :, HEAD_DIM:, HEAD_DIM:]
    return jnp.stack([even, odd], axis=2).reshape(shape)


LARGE = ("ffn1_w_in", "ffn1_w_out", "w_in_mix", "w_out_mix", "ffn2_w_in", "ffn2_w_out")
COLUMN_SHARDED = ("ffn1_w_in", "w_in_mix", "ffn2_w_in")
SHARDED_SMALL = ("lru_conv_w", "lru_lambda")
REPLICATED = ("norm_ffn1", "norm_mix", "lru_conv_b", "lru_gate_w", "lru_gate_b", "attn_rpb", "lru_out_norm",
              "attn_out_norm", "norm_ffn2", "norm_final")
SMALL_ORDER = REPLICATED + SHARDED_SMALL
WEIGHTS = ("norm_ffn1", "ffn1_w_in", "ffn1_w_out", "norm_mix", "w_in_mix", "lru_conv_w", "lru_conv_b", "lru_gate_w",
           "lru_gate_b", "lru_lambda", "attn_rpb", "lru_out_norm", "attn_out_norm", "w_out_mix", "norm_ffn2",
           "ffn2_w_in", "ffn2_w_out", "norm_final")


PARTS = {("gather", "w_in_mix"): 4, ("gather", "ffn2_w_in"): 8,
         ("to_chips", "ffn2_w_in"): 8, ("to_chips", "w_in_mix"): 4, ("to_chips", "ffn1_w_out"): 4}
CARRIES = {
    "gather_ffn1_in": [(("gather", "ffn1_w_in"), 1), (("gather", "small"), 1)],
    "ffn1_hidden": [(("gather", "ffn1_w_out"), 1), (("gather", "w_in_mix"), 1)],
    "ffn1_out": [(("gather", "w_in_mix"), 3)],
    "mix_in_proj": [(("gather", "w_out_mix"), 1), (("gather", "ffn2_w_in"), 1)],
    "lru_fwd": [(("gather", "ffn2_w_in"), 3)],
    "attn_fwd": [(("gather", "ffn2_w_in"), 3)],
    "mix_out_proj": [(("gather", "ffn2_w_in"), 1)],
    "ffn2_hidden": [(("gather", "ffn2_w_out"), 1)],
    "ffn2_bwd": [(("to_sibling", "ffn2_w_out"), 1)],
    "ffn2_in_grad": [(("to_chips", "ffn2_w_out"), 1)],
    "norm_ffn2_bwd": [(("to_sibling", "ffn2_w_in"), 1)],
    "mix_out_grad": [(("to_chips", "ffn2_w_in"), 1)],
    "mix_out_bwd": [(("to_chips", "ffn2_w_in"), 1)],
    "attn_bwd": [(("to_chips", "ffn2_w_in"), 4)],
    "lru_bwd": [(("to_chips", "ffn2_w_in"), 2), (("to_sibling", "w_out_mix"), 1)],
    "mix_in_bwd": [(("to_chips", "w_out_mix"), 1), (("to_sibling", "w_in_mix"), 1)],
    "ffn1_out_grad": [(("to_chips", "w_in_mix"), 2)],
    "ffn1_bwd": [(("to_chips", "w_in_mix"), 2), (("to_sibling", "ffn1_w_out"), 1), (("gather", "small_grads"), 1)],
    "ffn1_in_grad_gate": [(("to_chips", "ffn1_w_out"), 2)],
    "ffn1_in_grad_up": [(("to_chips", "ffn1_w_out"), 2)],
    "norm_ffn1_bwd": [(("to_sibling", "ffn1_w_in"), 1)],
    "gather_late_grads": [(("gather", "late_grads"), 1)],
}


class _Transfer:
    def __init__(self, kind, src, dest, block_rows, parts):
        self.kind, self.src, self.dest = kind, src, dest
        self.ranges, self.taken = _split(block_rows, parts), 0

    def take(self, count):
        lo, hi = self.ranges[self.taken][0], self.ranges[self.taken + count - 1][1]
        self.taken += count
        return _Piece(self.kind, self.src, self.dest, lo, hi)


class _Traffic:
    def __init__(self):
        self.transfers = {}

    def open(self, kind, name, src):
        if kind == "gather":
            dest, rows = _gathered(src), src.shape[0]
        elif kind == "to_sibling":
            dest, rows = SDS((src.shape[0] // 2, src.shape[1]), src.dtype), src.shape[0] // N_DEV
        else:
            dest, rows = SDS(src.shape, src.dtype), src.shape[0] // N_CHIP
        self.transfers[kind, name] = _Transfer(kind, src, dest, rows, PARTS.get((kind, name), 1))

    def _job(self, host):
        moved = [self.transfers[key] for key, _ in CARRIES[host]]
        return moved, _Job([tr.take(count) for tr, (_, count) in zip(moved, CARRIES[host])])

    def carry(self, host, fn, *args, **kw):
        if host not in CARRIES:
            return fn(*args, name=host, **kw)
        moved, job = self._job(host)
        res, landed = fn(*args, name=host, job=job, **kw)
        for tr, arr in zip(moved, landed):
            tr.dest = arr
        return res

    def alone(self, host):
        moved, job = self._job(host)
        for tr, arr in zip(moved, _run_job(job, host)):
            tr.dest = arr

    def result(self, kind, name):
        tr = self.transfers.pop((kind, name))
        assert tr.taken == len(tr.ranges), (kind, name)
        return tr.dest


def _forward_backward(x, target, shards, sharded_small, s):
    c = s["lru_conv_b"].shape[1]
    width = s["attn_out_norm"].shape[1]
    t = x.shape[0]
    traffic = _Traffic()
    carry = traffic.carry
    weight = lambda n: traffic.result("gather", n)

    for n in LARGE:
        traffic.open("gather", n, shards[n])
    traffic.open("gather", "small", sharded_small)
    traffic.alone("gather_ffn1_in")
    full_small = weight("small").reshape(N_DEV, SUBLANES, c // N_DEV)
    conv_w = full_small[:, :CONV_WIDTH].transpose(1, 0, 2).reshape(CONV_WIDTH, c)
    lam = full_small[:, CONV_WIDTH:CONV_WIDTH + 2].transpose(1, 0, 2).reshape(2, c)
    w = {"ffn1_w_in": weight("ffn1_w_in")}
    ffn_out = dict(nt=False, out_dtype=F32, tm=512, tn=512, scale=0.5)
    u1 = _rmsnorm_fwd(x, s["norm_ffn1"], "norm_ffn1")
    g1, up1, hid1, hid1_t = carry("ffn1_hidden", _ffn_hidden, u1, w["ffn1_w_in"])
    w["ffn1_w_out"] = weight("ffn1_w_out")
    h1 = carry("ffn1_out", _mm, hid1, w["ffn1_w_out"], residual=x, **ffn_out)
    w["w_in_mix"] = weight("w_in_mix")
    u2 = _rmsnorm_fwd(h1, s["norm_mix"], "norm_mix")
    proj = carry("mix_in_proj", _mm, u2, w["w_in_mix"], nt=True, out_dtype=F32, tm=512, tn=512)
    w["w_out_mix"] = weight("w_out_mix")
    gw = _dense_gate_blocks(s["lru_gate_w"]).astype(BF16)
    gb = s["lru_gate_b"].reshape(4, c)
    tables, tables_vjp = jax.vjp(_bias_tables, s["attn_rpb"])
    ya, hf, hb = carry("lru_fwd", _lru_fwd, proj, conv_w, s["lru_conv_b"], gw, gb, lam)
    yb = carry("attn_fwd", _attn_fwd, proj, tables, width)
    y, yt = _mixnorm_fwd(ya, yb, s["lru_out_norm"], s["attn_out_norm"], "mix_norm")
    h2 = carry("mix_out_proj", _mm, y, w["w_out_mix"], nt=False, out_dtype=F32, tm=512, tn=512, residual=h1)
    u3 = _rmsnorm_fwd(h2, s["norm_ffn2"], "norm_ffn2")
    w["ffn2_w_in"] = weight("ffn2_w_in")
    g2, up2, hid2, hid2_t = carry("ffn2_hidden", _ffn_hidden, u3, w["ffn2_w_in"])
    w["ffn2_w_out"] = weight("ffn2_w_out")
    h3 = carry("ffn2_out", _mm, hid2, w["ffn2_w_out"], residual=h2, **ffn_out)
    dh3, df2, loss_part, d_norm_final = _final_loss(h3, s["norm_final"], target, "final_loss")

    grads = {}
    grad_of = dict(nt=False, out_dtype=BF16, tm=512, tn=1024)

    def reduce_in_chip(n):
        traffic.open("to_sibling", n, grads[n])

    def reduce_over_chips(n):
        traffic.open("to_chips", n, _pair_sum(grads[n], traffic.result("to_sibling", n), "pair_sum_" + n))

    f = hid2_t.shape[0]
    grads["ffn2_w_out"] = carry("ffn2_out_grad", _mm, hid2_t, df2, **grad_of)
    reduce_in_chip("ffn2_w_out")
    du3, da2_t = carry("ffn2_bwd", _ffn_bwd, df2, g2, up2, w["ffn2_w_in"], w["ffn2_w_out"])
    reduce_over_chips("ffn2_w_out")
    grads["ffn2_w_in"] = carry("ffn2_in_grad", _mm, da2_t.reshape(2 * f, t), u3, **grad_of)
    reduce_in_chip("ffn2_w_in")
    dh2, dh2b, d_norm_ffn2 = carry("norm_ffn2_bwd", _rmsnorm_bwd, du3, h2, s["norm_ffn2"], dh3, 1.0)
    reduce_over_chips("ffn2_w_in")
    grads["w_out_mix"] = carry("mix_out_grad", _mm, yt, dh2b, **grad_of)
    reduce_in_chip("w_out_mix")
    dy = carry("mix_out_bwd", _mm, dh2b, w["w_out_mix"], nt=True, out_dtype=F32, tm=512, tn=512)
    dya, dyb, d_lru_out_norm, d_attn_out_norm = _mixnorm_bwd(dy, ya, yb, s["lru_out_norm"], s["attn_out_norm"],
                                                             "mix_norm_bwd")
    dq, dk, dv, dqkv_t, d_tables = carry("attn_bwd", _attn_bwd, proj, tables, dyb)
    dx_lru, dg_lru, dxg_t, d_conv_w, d_conv_b, d_gw, d_gb, d_lam = carry(
        "lru_bwd", _lru_bwd, proj, conv_w, s["lru_conv_b"], gw, gb, lam, hf, hb, dya)
    reduce_over_chips("w_out_mix")
    rows_of = 2 * c + 3 * width
    lru_rows = carry("mix_in_grad_lru", _mm, dxg_t.reshape(2 * c, t), u2, out_rows=rows_of, **grad_of)
    grads["w_in_mix"] = carry("mix_in_grad_attn", _mm, dqkv_t.reshape(3 * width, t), u2, out_rows=rows_of,
                              row_offset=2 * c, into=lru_rows, **grad_of)
    reduce_in_chip("w_in_mix")
    du2 = carry("mix_in_bwd", _mm, [dx_lru, dg_lru, dq, dk, dv], w["w_in_mix"], nt=False, out_dtype=F32, tm=512,
                tn=512)
    reduce_over_chips("w_in_mix")
    dh1, df1, d_norm_mix = carry("norm_mix_bwd", _rmsnorm_bwd, du2, h1, s["norm_mix"], dh2, 0.5)

    by_device = lambda a: a.reshape(a.shape[0], N_DEV, -1).transpose(1, 0, 2)
    small = {
        "norm_mix": d_norm_mix, "lru_conv_b": d_conv_b, "lru_gate_w": _diag_gate_blocks(d_gw, s["lru_gate_w"].shape),
        "lru_gate_b": d_gb.reshape(s["lru_gate_b"].shape), "attn_rpb": tables_vjp(d_tables)[0],
        "lru_out_norm": d_lru_out_norm, "attn_out_norm": d_attn_out_norm, "norm_ffn2": d_norm_ffn2,
        "norm_final": d_norm_final, "lru_conv_w": by_device(d_conv_w), "lru_lambda": by_device(d_lam),
    }
    early = [small[n] for n in SMALL_ORDER[1:]]
    traffic.open("gather", "small_grads", _pack(early))

    grads["ffn1_w_out"] = carry("ffn1_out_grad", _mm, hid1_t, df1, **grad_of)
    reduce_in_chip("ffn1_w_out")
    du1, da1_t = carry("ffn1_bwd", _ffn_bwd, df1, g1, up1, w["ffn1_w_in"], w["ffn1_w_out"])
    reduce_over_chips("ffn1_w_out")
    gate_rows = carry("ffn1_in_grad_gate", _mm, da1_t, u1, lead=0, out_rows=2 * f, **grad_of)
    grads["ffn1_w_in"] = carry("ffn1_in_grad_up", _mm, da1_t, u1, lead=1, out_rows=2 * f, row_offset=f,
                               into=gate_rows, **grad_of)
    reduce_in_chip("ffn1_w_in")
    grad_x, _, d_norm_ffn1 = carry("norm_ffn1_bwd", _rmsnorm_bwd, du1, x, s["norm_ffn1"], dh1, 1.0)
    traffic.open("gather", "late_grads", _pack([d_norm_ffn1]))
    traffic.alone("gather_late_grads")
    summed = _pair_sum(grads["ffn1_w_in"], traffic.result("to_sibling", "ffn1_w_in"), "pair_sum_ffn1_w_in")
    pending = _to_chips_start(summed, _own_slot(summed, "own_slot_ffn1_w_in"), "to_chips_ffn1_start")
    partials = {n: traffic.result("to_chips", n) for n in LARGE if n != "ffn1_w_in"}
    reduced = (_unpack(_sum_devices(traffic.result("gather", "late_grads"), "sum_late_grads"), [d_norm_ffn1])
               + _unpack(_sum_devices(traffic.result("gather", "small_grads"), "sum_small_grads"), early))
    assert not traffic.transfers, list(traffic.transfers)
    return loss_part[0, 0], grad_x, partials, pending, dict(zip(SMALL_ORDER, reduced))


def _step(x, loss_target, p, m, v):
    me = 4 * lax.axis_index("x") + 2 * lax.axis_index("y") + lax.axis_index("c")

    shards = {n: (_cast_transposed if n in COLUMN_SHARDED else _cast_rows)(p[n], "cast_" + n) for n in LARGE}
    sharded_small = (jnp.pad(p["lru_conv_w"], ((0, SUBLANES - CONV_WIDTH), (0, 0)))
                     + jnp.pad(p["lru_lambda"], ((CONV_WIDTH, SUBLANES - CONV_WIDTH - 2), (0, 0))))
    s = {n: p[n] if n in ("lru_gate_w", "lru_gate_b", "attn_rpb") else p[n].reshape(1, -1) for n in REPLICATED}

    loss_part, grad_x, partials, pending, small = _forward_backward(x, loss_target, shards, sharded_small, s)
    loss = lax.psum(loss_part, ("x", "y", "c"))

    sems, summed, land, token = pending
    update = lambda n, **kw: (_adamw_cols if n in COLUMN_SHARDED else _adamw_rows)(
        p[n], partials[n], m[n], v[n], "adamw_" + n, **kw)
    out = {n: update(n, after=[token]) for n in LARGE if n != "ffn1_w_in"}
    partials["ffn1_w_in"] = _to_chips_wait(sems, summed, land, [o[3] for o in out.values()], "to_chips_ffn1_wait")
    out["ffn1_w_in"] = update("ffn1_w_in")

    g_small = {n: lax.dynamic_index_in_dim(g, me, axis=0, keepdims=False) if n in SHARDED_SMALL else g
               for n, g in small.items()}
    names = SMALL_ORDER
    like = [p[n] for n in names]
    pack_of = lambda d: _pack([d[n].reshape(p[n].shape) for n in names])
    upd = _adamw_small(pack_of(p), pack_of(g_small), pack_of(m), pack_of(v), "adamw_small")
    for n, d_, m_, v_ in zip(names, *[_unpack(u, like) for u in upd]):
        out[n] = (g_small[n].reshape(p[n].shape), d_, m_, v_)
    return loss, grad_x, out


def kernel(x, norm_ffn1, ffn1_w_in, ffn1_w_out, norm_mix, w_in_mix, lru_conv_w, lru_conv_b, lru_gate_w, lru_gate_b, lru_lambda, attn_rpb, lru_out_norm, attn_out_norm, w_out_mix, norm_ffn2, ffn2_w_in, ffn2_w_out, norm_final, loss_target, m_norm_ffn1, m_ffn1_w_in, m_ffn1_w_out, m_norm_mix, m_w_in_mix, m_lru_conv_w, m_lru_conv_b, m_lru_gate_w, m_lru_gate_b, m_lru_lambda, m_attn_rpb, m_lru_out_norm, m_attn_out_norm, m_w_out_mix, m_norm_ffn2, m_ffn2_w_in, m_ffn2_w_out, m_norm_final, v_norm_ffn1, v_ffn1_w_in, v_ffn1_w_out, v_norm_mix, v_w_in_mix, v_lru_conv_w, v_lru_conv_b, v_lru_gate_w, v_lru_gate_b, v_lru_lambda, v_attn_rpb, v_lru_out_norm, v_attn_out_norm, v_w_out_mix, v_norm_ffn2, v_ffn2_w_in, v_ffn2_w_out, v_norm_final):
    given = dict(locals())
    drop_layer = lambda n, a: a if n == "norm_final" else a[0]
    p = {n: drop_layer(n, given[n]) for n in WEIGHTS}
    m = {n: drop_layer(n, given["m_" + n]) for n in WEIGHTS}
    v = {n: drop_layer(n, given["v_" + n]) for n in WEIGHTS}
    loss, grad_x, out = _step(x[0], loss_target[0], p, m, v)
    shaped = lambda n, a: a.reshape(given[n].shape)
    return (loss, grad_x[None], *[shaped(n, out[n][k]) for k in range(4) for n in WEIGHTS])
```

```python
import math

import numpy as np
import jax
import jax.numpy as jnp
from jax import lax
from jax.experimental import pallas as pl
from jax.experimental.pallas import tpu as pltpu

F32 = jnp.float32
BF16 = jnp.bfloat16
SDS = jax.ShapeDtypeStruct

N_DEV = 8
N_CHIP = 4
NORM_EPS = 1e-6
RG_C = 8.0
CONV_WIDTH = 4
HEAD_DIM = 64
GRID_W = 64
WIN_ROWS = 8
WIN_COLS = 16
NEG = -1e30

ADAM_LR = 0.001
ADAM_B1 = 0.9
ADAM_B2 = 0.999
ADAM_EPS = 1e-08
ADAM_WD = 0.01
ADAM_STEP = 10

LANES = 128
SUBLANES = 8
VMEM_LIMIT = 56 * 1024 * 1024

NT = (((1,), (1,)), ((), ()))
TN = (((0,), (0,)), ((), ()))
ANY = pl.BlockSpec(memory_space=pl.ANY)
WHOLE = pl.BlockSpec(memory_space=pltpu.VMEM)
MESH = pl.DeviceIdType.MESH


def _sigmoid(x):
    return 1.0 / (1.0 + jnp.exp(-x))


def _gelu_parts(x):
    c = math.sqrt(2.0 / math.pi)
    t = jnp.tanh(c * (x + 0.044715 * (x * x * x)))
    gelu = 0.5 * x * (1.0 + t)
    dgelu = 0.5 * (1.0 + t) + 0.5 * x * (1.0 - t * t) * (c * (1.0 + 3.0 * 0.044715 * (x * x)))
    return gelu, dgelu


def _expm1(x):
    poly = x * (1.0 + x * (1.0 / 2) * (1.0 + x * (1.0 / 3) * (1.0 + x * (1.0 / 4) * (1.0 + x * (1.0 / 5) * (1.0 + x * (1.0 / 6))))))
    return jnp.where(jnp.abs(x) < 0.25, poly, jnp.exp(x) - 1.0)


def _softplus(x):
    return jnp.maximum(x, 0.0) + jnp.log1p(jnp.exp(-jnp.abs(x)))


class _Piece:
    N_REMOTE = {"gather": 7, "to_sibling": N_CHIP, "to_chips": 3}
    N_LOCAL = {"gather": 1, "to_sibling": 0, "to_chips": 1}

    def __init__(self, kind, src, dest, lo, hi):
        self.kind, self.src, self.dest, self.lo, self.hi = kind, src, dest, lo, hi


RELAY_AT = 60


class _Job:
    def __init__(self, pieces):
        self.pieces = list(pieces)
        self.ins = [p.src for p in self.pieces]
        self.out_shapes = [SDS(p.dest.shape, p.dest.dtype) for p in self.pieces]
        self.aliased = [i for i, p in enumerate(self.pieces) if not isinstance(p.dest, SDS)]
        self.n_remote = sum(_Piece.N_REMOTE[p.kind] for p in self.pieces)
        self.n_local = max(sum(_Piece.N_LOCAL[p.kind] for p in self.pieces), 1)

    def _each(self, step, ins, outs, send_sems, recv_sems, local_sems):
        remote = local = 0
        for p, src, dst in zip(self.pieces, ins, outs):
            _EXCHANGES[p.kind](step, p, src, dst, send_sems, recv_sems, local_sems, remote, local)
            remote += _Piece.N_REMOTE[p.kind]
            local += _Piece.N_LOCAL[p.kind]

    def start(self, *refs):
        self._each("start", *refs)

    def relay(self, *refs):
        self._each("relay", *refs)

    def finish(self, *refs):
        self._each("finish", *refs)


def _call(body, *, name, args, out_shape, in_specs, out_specs, grid=(), scratch_shapes=(), aliases=None, job=None):
    single = not isinstance(out_shape, (tuple, list))
    out_shape = (out_shape,) if single else tuple(out_shape)
    out_specs = (out_specs,) if single else tuple(out_specs)
    aliases = dict(aliases or {})
    params = pltpu.CompilerParams(dimension_semantics=("arbitrary",) * len(grid) if grid else None,
                                  vmem_limit_bytes=VMEM_LIMIT)
    if job is None:
        res = pl.pallas_call(body, out_shape=out_shape, grid=grid, in_specs=list(in_specs), out_specs=out_specs,
                             scratch_shapes=list(scratch_shapes), input_output_aliases=aliases, name=name,
                             compiler_params=params)(*args)
        return res[0] if single else res

    n_in, n_out, n_scr = len(args), len(out_shape), len(scratch_shapes)
    j_in, j_out, j_alias = len(job.ins), len(job.out_shapes), len(job.aliased)

    def hosted(*refs):
        ins, refs = refs[:n_in], refs[n_in:]
        j_ins, refs = refs[:j_in], refs[j_in + j_alias:]
        outs, refs = refs[:n_out], refs[n_out:]
        j_outs, refs = refs[:j_out], refs[j_out:]
        scr, sems = refs[:n_scr], refs[n_scr:]
        if grid:
            step = 0
            for axis, size in enumerate(grid):
                step = step * size + pl.program_id(axis)
            steps = math.prod(grid)
            pl.when(step == 0)(lambda: job.start(j_ins, j_outs, *sems))
            body(*ins, *outs, *scr)
            pl.when(step == min(RELAY_AT * steps // 100, steps - 1))(lambda: job.relay(j_ins, j_outs, *sems))
            pl.when(step == steps - 1)(lambda: job.finish(j_ins, j_outs, *sems))
        else:
            job.start(j_ins, j_outs, *sems)
            body(*ins, *outs, *scr)
            job.relay(j_ins, j_outs, *sems)
            job.finish(j_ins, j_outs, *sems)

    res = pl.pallas_call(
        hosted, out_shape=out_shape + tuple(job.out_shapes), grid=grid,
        in_specs=list(in_specs) + [ANY] * (j_in + j_alias), out_specs=out_specs + (ANY,) * j_out,
        scratch_shapes=list(scratch_shapes) + [pltpu.SemaphoreType.DMA((job.n_remote,)),
                                               pltpu.SemaphoreType.DMA((job.n_remote,)),
                                               pltpu.SemaphoreType.DMA((job.n_local,))],
        input_output_aliases={**aliases, **{n_in + j_in + k: n_out + i for k, i in enumerate(job.aliased)}},
        name=name, compiler_params=params)(*args, *job.ins, *[job.pieces[i].dest for i in job.aliased])
    own, carried = res[:n_out], res[n_out:]
    return (own[0] if single else own), carried


def _run_job(job, name):
    return _call(lambda: None, name=name, args=[], out_shape=(), in_specs=[], out_specs=(), job=job)[1]


def _position():
    return lax.axis_index("x"), lax.axis_index("y"), lax.axis_index("c")


def _flat(px, py, pc):
    return 4 * px + 2 * py + pc


def _gather_exchange(step, p, src, dst, send_sems, recv_sems, local_sems, r0, l0):
    x, y, c = _position()
    me, sibling = (x, y, c), (x, y, 1 - c)
    along_x, along_y, diagonal = (1 - x, y), (x, 1 - y), (1 - x, 1 - y)
    south = c == 0
    passed_on = (jnp.where(south, 1 - x, x), jnp.where(south, y, 1 - y))
    passed_to = (jnp.where(south, x, 1 - x), jnp.where(south, 1 - y, y))
    rb, n_rows = p.src.shape[0], p.hi - p.lo
    mine = src.at[pl.ds(p.lo, n_rows), :]

    def rows(block):
        return dst.at[pl.ds(_flat(*block) * rb + p.lo, n_rows), :]

    def copy(k, block, to, own=False):
        return pltpu.make_async_remote_copy(
            src_ref=mine if own else rows(block), dst_ref=rows(block),
            send_sem=send_sems.at[r0 + k], recv_sem=recv_sems.at[r0 + k], device_id=to, device_id_type=MESH)

    local = pltpu.make_async_copy(mine, rows(me), local_sems.at[l0])
    if step == "start":
        local.start()
        copy(0, me, sibling, own=True).start()
        copy(1, me, (*along_x, c), own=True).start()
        copy(2, me, (*along_y, c), own=True).start()
    elif step == "relay":
        copy(1, (*along_x, c), me).wait_recv()
        copy(2, (*along_y, c), me).wait_recv()
        copy(3, (*passed_on, c), (*passed_to, c)).start()
        copy(4, (*along_x, c), sibling).start()
        copy(5, (*along_y, c), sibling).start()
    else:
        copy(3, (*diagonal, c), me).wait_recv()
        copy(6, (*diagonal, c), sibling).start()
        copy(0, sibling, me).wait_recv()
        copy(4, (*along_x, 1 - c), me).wait_recv()
        copy(5, (*along_y, 1 - c), me).wait_recv()
        copy(6, (*diagonal, 1 - c), me).wait_recv()
        copy(0, me, sibling, own=True).wait_send()
        copy(1, me, (*along_x, c), own=True).wait_send()
        copy(2, me, (*along_y, c), own=True).wait_send()
        copy(3, (*passed_on, c), (*passed_to, c)).wait_send()
        copy(4, (*along_x, c), sibling).wait_send()
        copy(5, (*along_y, c), sibling).wait_send()
        copy(6, (*diagonal, c), sibling).wait_send()
        local.wait()


def _sibling_exchange(step, p, src, dst, send_sems, recv_sems, local_sems, r0, l0):
    x, y, c = _position()
    rb, n_rows = p.src.shape[0] // N_DEV, p.hi - p.lo
    for q in range(N_CHIP):
        copy = pltpu.make_async_remote_copy(
            src_ref=src.at[pl.ds((2 * q + 1 - c) * rb + p.lo, n_rows), :],
            dst_ref=dst.at[pl.ds(q * rb + p.lo, n_rows), :],
            send_sem=send_sems.at[r0 + q], recv_sem=recv_sems.at[r0 + q], device_id=(x, y, 1 - c), device_id_type=MESH)
        if step == "start":
            copy.start()
        elif step == "finish":
            copy.wait()


CHIP_FLIPS = [(1, 0), (0, 1), (1, 1)]


def _chips_exchange(step, p, src, dst, send_sems, recv_sems, local_sems, r0, l0):
    x, y, c = _position()
    rb, n_rows = p.src.shape[0] // N_CHIP, p.hi - p.lo

    def slot(ref, px, py):
        return ref.at[pl.ds((2 * px + py) * rb + p.lo, n_rows), :]

    def copy(k, landing=False):
        px = 1 - x if CHIP_FLIPS[k][0] else x
        py = 1 - y if CHIP_FLIPS[k][1] else y
        return pltpu.make_async_remote_copy(
            src_ref=slot(dst, px, py) if landing else slot(src, px, py),
            dst_ref=slot(dst, px, py) if landing else slot(dst, x, y),
            send_sem=send_sems.at[r0 + k], recv_sem=recv_sems.at[r0 + k], device_id=(px, py, c), device_id_type=MESH)

    local = pltpu.make_async_copy(slot(src, x, y), slot(dst, x, y), local_sems.at[l0])
    if step == "start":
        local.start()
        for k in range(3):
            copy(k).start()
    elif step == "finish":
        for k in range(3):
            copy(k, landing=True).wait_recv()
        for k in range(3):
            copy(k).wait_send()
        local.wait()


_EXCHANGES = {"gather": _gather_exchange, "to_sibling": _sibling_exchange, "to_chips": _chips_exchange}


def _gathered(shard):
    return SDS((N_DEV * shard.shape[0], shard.shape[1]), shard.dtype)


def _split(rows, parts):
    cuts = [rows * k // parts // 16 * 16 for k in range(parts)] + [rows]
    return list(zip(cuts[:-1], cuts[1:]))


def _pair_sum(g, from_sibling, name):
    rb, n = g.shape[0] // N_DEV, g.shape[1]
    tr = rb if rb * n * 2 <= 3 * 1024 * 1024 else rb // 2
    core = lax.axis_index("c").astype(jnp.int32).reshape(1)

    def body(c_ref, g_ref, r_ref, o_ref):
        o_ref[...] = (g_ref[...].astype(F32) + r_ref[...].astype(F32)).astype(BF16)

    grid_spec = pltpu.PrefetchScalarGridSpec(
        num_scalar_prefetch=1, grid=(N_CHIP, rb // tr),
        in_specs=[pl.BlockSpec((None, None, tr, n), lambda q, i, c_ref: (q, c_ref[0], i, 0)),
                  pl.BlockSpec((None, tr, n), lambda q, i, c_ref: (q, i, 0))],
        out_specs=pl.BlockSpec((None, tr, n), lambda q, i, c_ref: (q, i, 0)))
    out = pl.pallas_call(
        body, grid_spec=grid_spec, out_shape=SDS((N_CHIP, rb, n), BF16), name=name,
        compiler_params=pltpu.CompilerParams(dimension_semantics=("arbitrary",) * 2, vmem_limit_bytes=VMEM_LIMIT))(
            core, g.reshape(N_CHIP, 2, rb, n), from_sibling.reshape(N_CHIP, rb, n))
    return out.reshape(N_CHIP * rb, n)


SEM = pl.BlockSpec(memory_space=pltpu.SEMAPHORE)
IN_HBM = pl.BlockSpec(memory_space=pltpu.HBM)
SIDE_EFFECT = pltpu.SideEffectType.DATAFLOW_SIDE_EFFECTING


def _own_slot(partials, name):
    rb, n = partials.shape[0] // N_CHIP, partials.shape[1]
    tr = rb // 2
    chip = (2 * lax.axis_index("x") + lax.axis_index("y")).astype(jnp.int32).reshape(1)

    def body(chip_ref, src_ref, dst_ref):
        dst_ref[...] = src_ref[...]

    block = pl.BlockSpec((None, tr, n), lambda i, chip_ref: (chip_ref[0], i, 0))
    grid_spec = pltpu.PrefetchScalarGridSpec(num_scalar_prefetch=1, grid=(rb // tr,), in_specs=[block], out_specs=block)
    out = pl.pallas_call(
        body, grid_spec=grid_spec, out_shape=SDS((N_CHIP, rb, n), partials.dtype), name=name,
        compiler_params=pltpu.CompilerParams(dimension_semantics=("arbitrary",), vmem_limit_bytes=VMEM_LIMIT))(
            chip, partials.reshape(N_CHIP, rb, n))
    return out.reshape(partials.shape)


def _chip_copies(src_ref, land_ref, sems, rb):
    x, y, c = _position()
    copies = []
    for k, (fx, fy) in enumerate(CHIP_FLIPS):
        px, py = (1 - x if fx else x), (1 - y if fy else y)
        copies.append(pltpu.make_async_remote_copy(
            src_ref=src_ref.at[pl.ds((2 * px + py) * rb, rb), :], dst_ref=land_ref.at[pl.ds((2 * x + y) * rb, rb), :],
            send_sem=sems[2 * k], recv_sem=sems[2 * k + 1], device_id=(px, py, c), device_id_type=MESH))
    return copies


def _to_chips_start(partials, land, name):
    rb = partials.shape[0] // N_CHIP

    def body(src_ref, land_ref, *rest):
        sems, token = rest[:6], rest[-1]
        for copy in _chip_copies(src_ref, land_ref, sems, rb):
            copy.start()
        token[...] = jnp.zeros_like(token)

    hbm = pltpu.HBM(partials.shape, partials.dtype)
    res = pl.pallas_call(
        body, name=name, out_shape=(pltpu.SemaphoreType.DMA(()),) * 6 + (hbm, hbm, SDS((SUBLANES, LANES), F32)),
        in_specs=(IN_HBM, IN_HBM), out_specs=(SEM,) * 6 + (IN_HBM, IN_HBM, WHOLE), input_output_aliases={0: 6, 1: 7},
        compiler_params=pltpu.CompilerParams(has_side_effects=SIDE_EFFECT))(
            pltpu.with_memory_space_constraint(partials, pltpu.HBM), pltpu.with_memory_space_constraint(land, pltpu.HBM))
    return res[:6], res[6], res[7], res[8]


def _to_chips_wait(sems, partials, land, after, name):
    rb = partials.shape[0] // N_CHIP

    def body(src_ref, land_ref, *rest):
        for copy in _chip_copies(src_ref, land_ref, rest[:6], rb):
            copy.wait_send()
            copy.wait_recv()

    hbm = pltpu.HBM(partials.shape, partials.dtype)
    return pl.pallas_call(
        body, name=name, out_shape=(hbm, hbm), in_specs=(IN_HBM, IN_HBM) + (SEM,) * 6 + (ANY,) * len(after),
        out_specs=(IN_HBM, IN_HBM), input_output_aliases={0: 0, 1: 1},
        compiler_params=pltpu.CompilerParams(has_side_effects=SIDE_EFFECT))(partials, land, *sems, *after)[1]


def _sum_devices(gathered, name):
    r = gathered.shape[0] // N_DEV

    def body(g_ref, o_ref):
        acc = g_ref[0]
        for s in range(1, N_DEV):
            acc = acc + g_ref[s]
        o_ref[...] = acc

    return _call(body, name=name, args=[gathered.reshape(N_DEV, r, LANES)], out_shape=SDS((r, LANES), F32),
                 in_specs=[WHOLE], out_specs=WHOLE)


def _cast_rows(w, name):
    def body(w_ref, o_ref):
        o_ref[...] = w_ref[...].astype(BF16)

    return _call(body, name=name, args=[w], out_shape=SDS(w.shape, BF16), in_specs=[WHOLE], out_specs=WHOLE)


def _cast_transposed(w, name):
    d, n = w.shape
    td = 512

    def body(w_ref, o_ref):
        o_ref[...] = w_ref[...].T.astype(BF16)

    return _call(body, name=name, args=[w], out_shape=SDS((n, d), BF16), grid=(d // td,),
                 in_specs=[pl.BlockSpec((td, n), lambda i: (i, 0))], out_specs=pl.BlockSpec((n, td), lambda i: (0, i)))


ROW_TILE = 256


def _rmsnorm_fwd(h, gain, name):
    t, d = h.shape

    def body(h_ref, g_ref, u_ref):
        x = h_ref[...]
        u_ref[...] = (x * lax.rsqrt(jnp.mean(x * x, axis=-1, keepdims=True) + NORM_EPS) * g_ref[...]).astype(BF16)

    row = pl.BlockSpec((ROW_TILE, d), lambda i: (i, 0))
    return _call(body, name=name, args=[h, gain], out_shape=SDS((t, d), BF16), grid=(t // ROW_TILE,),
                 in_specs=[row, pl.BlockSpec((1, d), lambda i: (0, 0))], out_specs=row)


def _rms_bwd_math(x, gain, dy):
    rstd = lax.rsqrt(jnp.mean(x * x, axis=-1, keepdims=True) + NORM_EPS)
    xhat = x * rstd
    dxh = dy * gain
    dx = rstd * (dxh - xhat * jnp.mean(dxh * xhat, axis=-1, keepdims=True))
    return dx, jnp.sum(dy * xhat, axis=0, keepdims=True)


def _rmsnorm_bwd(du, h, gain, resid, bf_scale, name, job=None):
    t, d = h.shape

    def body(du_ref, h_ref, g_ref, r_ref, dh_ref, dhb_ref, dg_ref):
        @pl.when(pl.program_id(0) == 0)
        def _():
            dg_ref[...] = jnp.zeros_like(dg_ref)

        dx, dg = _rms_bwd_math(h_ref[...], g_ref[...], du_ref[...])
        dh = r_ref[...] + dx
        dh_ref[...] = dh
        dhb_ref[...] = (bf_scale * dh).astype(BF16)
        dg_ref[...] += dg

    row = pl.BlockSpec((ROW_TILE, d), lambda i: (i, 0))
    vec = pl.BlockSpec((1, d), lambda i: (0, 0))
    return _call(body, name=name, args=[du, h, gain, resid],
                 out_shape=(SDS((t, d), F32), SDS((t, d), BF16), SDS((1, d), F32)), grid=(t // ROW_TILE,),
                 in_specs=[row, row, vec, row], out_specs=(row, row, vec), job=job)


def _final_loss(h, gain, target, name):
    t, d = h.shape

    def body(h_ref, g_ref, t_ref, dh_ref, dhb_ref, loss_ref, dg_ref):
        @pl.when(pl.program_id(0) == 0)
        def _():
            dg_ref[...] = jnp.zeros_like(dg_ref)
            loss_ref[...] = jnp.zeros_like(loss_ref)

        x = h_ref[...]
        gain = g_ref[...]
        out = x * lax.rsqrt(jnp.mean(x * x, axis=-1, keepdims=True) + NORM_EPS) * gain
        err = out - t_ref[...]
        loss_ref[...] += 0.5 * jnp.sum(jnp.mean(err * err, axis=-1, keepdims=True), axis=0, keepdims=True)
        dx, dg = _rms_bwd_math(x, gain, err * (1.0 / d))
        dh_ref[...] = dx
        dhb_ref[...] = (0.5 * dx).astype(BF16)
        dg_ref[...] += dg

    row = pl.BlockSpec((ROW_TILE, d), lambda i: (i, 0))
    vec = pl.BlockSpec((1, d), lambda i: (0, 0))
    one = pl.BlockSpec((SUBLANES, LANES), lambda i: (0, 0))
    return _call(body, name=name, args=[h, gain, target],
                 out_shape=(SDS((t, d), F32), SDS((t, d), BF16), SDS((SUBLANES, LANES), F32), SDS((1, d), F32)),
                 grid=(t // ROW_TILE,), in_specs=[row, vec, row], out_specs=(row, row, one, vec))


def _mixnorm_fwd(ya, yb, ga, gb, name):
    t, c = ya.shape

    def body(ya_ref, yb_ref, ga_ref, gb_ref, y_ref, yt_ref):
        for k, (src, g_ref) in enumerate(((ya_ref, ga_ref), (yb_ref, gb_ref))):
            x = src[...]
            u = x * lax.rsqrt(jnp.mean(x * x, axis=-1, keepdims=True) + NORM_EPS) * g_ref[...]
            y_ref[:, k * c:(k + 1) * c] = u.astype(BF16)
            yt_ref[k * c:(k + 1) * c, :] = u.T.astype(BF16)

    row = pl.BlockSpec((ROW_TILE, c), lambda i: (i, 0))
    vec = pl.BlockSpec((1, c), lambda i: (0, 0))
    return _call(body, name=name, args=[ya, yb, ga, gb],
                 out_shape=(SDS((t, 2 * c), BF16), SDS((2 * c, t), BF16)), grid=(t // ROW_TILE,),
                 in_specs=[row, row, vec, vec],
                 out_specs=(pl.BlockSpec((ROW_TILE, 2 * c), lambda i: (i, 0)),
                            pl.BlockSpec((2 * c, ROW_TILE), lambda i: (0, i))))


def _mixnorm_bwd(dy, ya, yb, ga, gb, name):
    t, c = ya.shape

    def body(dy_ref, ya_ref, yb_ref, ga_ref, gb_ref, dya_ref, dyb_ref, dga_ref, dgb_ref):
        @pl.when(pl.program_id(0) == 0)
        def _():
            dga_ref[...] = jnp.zeros_like(dga_ref)
            dgb_ref[...] = jnp.zeros_like(dgb_ref)

        dxa, dga = _rms_bwd_math(ya_ref[...], ga_ref[...], dy_ref[:, :c])
        dxb, dgb = _rms_bwd_math(yb_ref[...], gb_ref[...], dy_ref[:, c:])
        dya_ref[...] = dxa
        dyb_ref[...] = dxb
        dga_ref[...] += dga
        dgb_ref[...] += dgb

    row = pl.BlockSpec((ROW_TILE, c), lambda i: (i, 0))
    vec = pl.BlockSpec((1, c), lambda i: (0, 0))
    return _call(body, name=name, args=[dy, ya, yb, ga, gb],
                 out_shape=(SDS((t, c), F32), SDS((t, c), F32), SDS((1, c), F32), SDS((1, c), F32)),
                 grid=(t // ROW_TILE,),
                 in_specs=[pl.BlockSpec((ROW_TILE, 2 * c), lambda i: (i, 0)), row, row, vec, vec],
                 out_specs=(row, row, vec, vec))


def _tile(n, want):
    return max(t for t in range(LANES, min(n, want) + 1, LANES) if n % t == 0)


def _mm(a, b, *, nt, out_dtype, tm, tn, name, residual=None, scale=None, lead=None, out_rows=None, row_offset=0,
        into=None, job=None):
    parts = list(a) if isinstance(a, (list, tuple)) else [a]
    m = parts[0].shape[-2]
    widths = [p.shape[-1] for p in parts]
    k = sum(widths)
    n = b.shape[0] if nt else b.shape[1]
    tm, tn = _tile(math.gcd(m, row_offset), tm), _tile(n, tn)
    out_rows = m if out_rows is None else out_rows

    def body(*refs):
        a_refs, b_ref, rest = refs[:len(parts)], refs[len(parts)], refs[len(parts) + 1:]
        o_ref = rest[-1]
        out, at = None, 0
        for a_ref, width in zip(a_refs, widths):
            av = a_ref[...].astype(BF16)
            if nt:
                term = lax.dot_general(av, b_ref[:, at:at + width].astype(BF16), NT, preferred_element_type=F32)
            else:
                term = jnp.dot(av, b_ref[at:at + width, :].astype(BF16), preferred_element_type=F32)
            out = term if out is None else out + term
            at += width
        if residual is not None:
            out = rest[0][...] + (out if scale is None else scale * out)
        o_ref[...] = out.astype(out_dtype)

    a_specs =([pl.BlockSpec((tm, width), lambda i, j: (i, 0)) for width in widths] if lead is None
               else [pl.BlockSpec((None, tm, k), lambda i, j: (lead, i, 0))])
    in_specs = a_specs + [pl.BlockSpec((tn, k), lambda i, j: (j, 0)) if nt else pl.BlockSpec((k, tn), lambda i, j: (0, j))]
    args, aliases = parts + [b], {}
    if residual is not None:
        in_specs.append(pl.BlockSpec((tm, tn), lambda i, j: (i, j)))
        args.append(residual)
    if into is not None:
        in_specs.append(ANY)
        aliases[len(args)] = 0
        args.append(into)
    return _call(body, name=name, args=args, out_shape=SDS((out_rows, n), out_dtype), grid=(m // tm, n // tn),
                 in_specs=in_specs, out_specs=pl.BlockSpec((tm, tn), lambda i, j: (row_offset // tm + i, j)),
                 aliases=aliases, job=job)


FFN_TM = 512
FFN_HB = 512


def _ffn_hidden(u, w_in_t, name, job=None):
    t, d = u.shape
    f = w_in_t.shape[0] // 2

    def body(u_ref, w_ref, g_ref, up_ref, hid_ref, hid_t_ref):
        uu = u_ref[...]
        g = lax.dot_general(uu, w_ref[0], NT, preferred_element_type=F32)
        up = lax.dot_general(uu, w_ref[1], NT, preferred_element_type=F32)
        g_ref[...] = g.astype(BF16)
        up_ref[...] = up.astype(BF16)
        hid = (g * _sigmoid(g)) * up
        hid_ref[...] = hid.astype(BF16)
        hid_t_ref[...] = hid.T.astype(BF16)

    pre = pl.BlockSpec((FFN_TM, FFN_HB), lambda i, k: (i, k))
    return _call(body, name=name, args=[u, w_in_t.reshape(2, f, d)],
                 out_shape=(SDS((t, f), BF16), SDS((t, f), BF16), SDS((t, f), BF16), SDS((f, t), BF16)),
                 grid=(t // FFN_TM, f // FFN_HB),
                 in_specs=[pl.BlockSpec((FFN_TM, d), lambda i, k: (i, 0)),
                           pl.BlockSpec((2, FFN_HB, d), lambda i, k: (0, k, 0))],
                 out_specs=(pre, pre, pre, pl.BlockSpec((FFN_HB, FFN_TM), lambda i, k: (k, i))), job=job)


def _ffn_bwd(dfb, gpre, upre, w_in_t, w_out, name, job=None):
    t, d = dfb.shape
    f = w_out.shape[0]
    nk = f // FFN_HB

    def body(df_ref, g_ref, up_ref, w_ref, wo_ref, du_ref, da_t_ref, acc):
        k = pl.program_id(1)

        @pl.when(k == 0)
        def _():
            acc[...] = jnp.zeros_like(acc)

        dhid = lax.dot_general(df_ref[...], wo_ref[...], NT, preferred_element_type=F32)
        g, up = g_ref[...].astype(F32), up_ref[...].astype(F32)
        sig = _sigmoid(g)
        silu = g * sig
        dup = dhid * silu
        dg = dhid * up * (sig * (1.0 + g * (1.0 - sig)))
        da_t_ref[0] = dg.T.astype(BF16)
        da_t_ref[1] = dup.T.astype(BF16)
        acc[...] += (jnp.dot(dg.astype(BF16), w_ref[0], preferred_element_type=F32)
                     + jnp.dot(dup.astype(BF16), w_ref[1], preferred_element_type=F32))

        @pl.when(k == nk - 1)
        def _():
            du_ref[...] = acc[...]

    tok = pl.BlockSpec((FFN_TM, d), lambda i, k: (i, 0))
    pre = pl.BlockSpec((FFN_TM, FFN_HB), lambda i, k: (i, k))
    return _call(body, name=name, args=[dfb, gpre, upre, w_in_t.reshape(2, f, d), w_out],
                 out_shape=(SDS((t, d), F32), SDS((2, f, t), BF16)), grid=(t // FFN_TM, nk),
                 in_specs=[tok, pre, pre, pl.BlockSpec((2, FFN_HB, d), lambda i, k: (0, k, 0)),
                           pl.BlockSpec((FFN_HB, d), lambda i, k: (k, 0))],
                 out_specs=(tok, pl.BlockSpec((2, FFN_HB, FFN_TM), lambda i, k: (0, k, i))),
                 scratch_shapes=[pltpu.VMEM((FFN_TM, d), F32)], job=job)


CH = LANES
PAD = SUBLANES


def _lru_gates(xc, gw_ref, gb_ref, lam_ref, z):
    xcb = xc.astype(BF16)
    r = _sigmoid(jnp.dot(xcb, gw_ref[2 * z], preferred_element_type=F32) + gb_ref[pl.ds(2 * z, 1), :])
    i = _sigmoid(jnp.dot(xcb, gw_ref[2 * z + 1], preferred_element_type=F32) + gb_ref[pl.ds(2 * z + 1, 1), :])
    sp = _softplus(-lam_ref[pl.ds(z, 1), :])
    log_a = (-RG_C * r) * sp
    a = jnp.exp(log_a)
    mult = jnp.sqrt(-_expm1(2.0 * log_a))
    return r, i, sp, a, mult


def _conv(xpad, cw_ref, cb_ref, t):
    xc = cb_ref[...] + cw_ref[pl.ds(0, 1), :] * xpad[pl.ds(PAD - 2, t), :]
    for j in range(1, CONV_WIDTH):
        xc = xc + cw_ref[pl.ds(j, 1), :] * xpad[pl.ds(PAD - 2 + j, t), :]
    return xc


def _fill_padded(pad_ref, value, t):
    pad_ref[pl.ds(0, PAD), :] = jnp.zeros((PAD, CH), F32)
    pad_ref[pl.ds(PAD + t, PAD), :] = jnp.zeros((PAD, CH), F32)
    pad_ref[pl.ds(PAD, t), :] = value


def _scan_pair(t, a_up, b_up, out_up, a_down, b_down, out_down):
    row = lax.broadcasted_iota(jnp.int32, (SUBLANES, CH), 0)

    def compose(a, b, rising):
        for dist in (1, 2, 4):
            shift = dist if rising else SUBLANES - dist
            keep = (row >= dist) if rising else (row < SUBLANES - dist)
            b = jnp.where(keep, b + a * pltpu.roll(b, shift, axis=0), b)
            a = jnp.where(keep, a * pltpu.roll(a, shift, axis=0), a)
        return a, b

    def step(tt, carry):
        hu, hd = carry
        lo = pl.ds(pl.multiple_of(tt * SUBLANES, SUBLANES), SUBLANES)
        hi = pl.ds(pl.multiple_of(t - SUBLANES - tt * SUBLANES, SUBLANES), SUBLANES)
        a, b = compose(a_up[lo, :], b_up[lo, :], True)
        up = b + a * hu
        out_up[lo, :] = up
        a, b = compose(a_down[hi, :], b_down[hi, :], False)
        down = b + a * hd
        out_down[hi, :] = down
        return up[SUBLANES - 1:, :], down[:1, :]

    zero = jnp.zeros((1, CH), F32)
    lax.fori_loop(0, t // SUBLANES, step, (zero, zero), unroll=2)


def _lru_fwd(proj, cw, cb, gw, gb, lam, name, job=None):
    t = proj.shape[0]
    c = cw.shape[1]
    ncb = c // CH

    def body(x_ref, g_ref, cw_ref, cb_ref, gw_ref, gb_ref, lam_ref, ya_ref, hf_ref, hb_ref, xpad, a0, b0, a1, b1):
        _fill_padded(xpad, x_ref[...], t)
        xc = _conv(xpad, cw_ref, cb_ref, t)
        for z, (a_s, b_s) in enumerate(((a0, b0), (a1, b1))):
            _, i, _, a, mult = _lru_gates(xc, gw_ref, gb_ref, lam_ref, z)
            a_s[...] = a
            b_s[...] = mult * (i * xc)
        _scan_pair(t, a0, b0, hf_ref, a1, b1, hb_ref)
        gelu, _ = _gelu_parts(g_ref[...])
        ya_ref[...] = gelu * (hf_ref[...] + hb_ref[...])

    col = lambda off: pl.BlockSpec((t, CH), lambda i: (0, off + i))
    small = lambda rows: pl.BlockSpec((rows, CH), lambda i: (0, i))
    return _call(body, name=name, args=[proj, proj, cw, cb, gw, gb, lam], out_shape=(SDS((t, c), F32),) * 3,
                 grid=(ncb,),
                 in_specs=[col(0), col(ncb), small(CONV_WIDTH), small(1),
                           pl.BlockSpec((4, None, CH, CH), lambda i: (0, i, 0, 0)), small(4), small(2)],
                 out_specs=(col(0),) * 3,
                 scratch_shapes=[pltpu.VMEM((t + 2 * PAD, CH), F32)] + [pltpu.VMEM((t, CH), F32)] * 4, job=job)


def _lru_bwd(proj, cw, cb, gw, gb, lam, hf, hb, dya, name, job=None):
    t = proj.shape[0]
    c = cw.shape[1]
    ncb = c // CH

    def body(x_ref, g_ref, cw_ref, cb_ref, gw_ref, gb_ref, lam_ref, hf_ref, hb_ref, dya_ref,
             dx_ref, dg_ref, dt_ref, dcw_ref, dcb_ref, dgw_ref, dgb_ref, dlam_ref,
             xpad, hpad, dxc, a0, a1, dhs, dh0, dh1):
        _fill_padded(xpad, x_ref[...], t)
        xc = _conv(xpad, cw_ref, cb_ref, t)
        xcb = xc.astype(BF16)
        gates = [_lru_gates(xc, gw_ref, gb_ref, lam_ref, z) for z in range(2)]

        gelu, dgelu = _gelu_parts(g_ref[...])
        dya = dya_ref[...]
        dgate = dya * (hf_ref[...] + hb_ref[...]) * dgelu
        dg_ref[...] = dgate.astype(BF16)
        dt_ref[1] = dgate.T.astype(BF16)
        dhs[...] = dya * gelu

        _fill_padded(hpad, gates[0][3], t)
        a0[...] = hpad[pl.ds(PAD + 1, t), :]
        _fill_padded(hpad, gates[1][3], t)
        a1[...] = hpad[pl.ds(PAD - 1, t), :]
        _scan_pair(t, a1, dhs, dh1, a0, dhs, dh0)

        acc_dxc = jnp.zeros((t, CH), F32)
        for z, (h_ref, dh_ref, shift) in enumerate(((hf_ref, dh0, -1), (hb_ref, dh1, 1))):
            r, i, sp, a, mult = gates[z]
            _fill_padded(hpad, h_ref[...], t)
            h_nb = hpad[pl.ds(PAD + shift, t), :]
            db = dh_ref[...]
            da = db * h_nb
            d_i = db * mult * xc
            acc_dxc = acc_dxc + db * mult * i
            d_mult = db * i * xc
            d_la = da * a - d_mult * (a * a) / mult
            d_r = d_la * (-RG_C * sp)
            dlam_ref[pl.ds(z, 1), :] = (jnp.sum(d_la * (-RG_C * r), axis=0, keepdims=True)
                                        * (-_sigmoid(-lam_ref[pl.ds(z, 1), :])))
            for gate, d_pre in ((0, d_r * r * (1.0 - r)), (1, d_i * i * (1.0 - i))):
                zg = 2 * z + gate
                dgb_ref[pl.ds(zg, 1), :] = jnp.sum(d_pre, axis=0, keepdims=True)
                d_pre_b = d_pre.astype(BF16)
                dgw_ref[zg] = lax.dot_general(xcb, d_pre_b, TN, preferred_element_type=F32)
                acc_dxc = acc_dxc + lax.dot_general(d_pre_b, gw_ref[zg], NT, preferred_element_type=F32)

        dcb_ref[...] = jnp.sum(acc_dxc, axis=0, keepdims=True)
        for j in range(CONV_WIDTH):
            dcw_ref[pl.ds(j, 1), :] = jnp.sum(acc_dxc * xpad[pl.ds(PAD - 2 + j, t), :], axis=0, keepdims=True)
        _fill_padded(dxc, acc_dxc, t)
        dx = cw_ref[pl.ds(0, 1), :] * dxc[pl.ds(PAD + 2, t), :]
        for j in range(1, CONV_WIDTH):
            dx = dx + cw_ref[pl.ds(j, 1), :] * dxc[pl.ds(PAD + 2 - j, t), :]
        dx_ref[...] = dx.astype(BF16)
        dt_ref[0] = dx.T.astype(BF16)

    col = lambda off: pl.BlockSpec((t, CH), lambda i: (0, off + i))
    small = lambda rows: pl.BlockSpec((rows, CH), lambda i: (0, i))
    dense = pl.BlockSpec((4, None, CH, CH), lambda i: (0, i, 0, 0))
    padded = pltpu.VMEM((t + 2 * PAD, CH), F32)
    return _call(
        body, name=name, args=[proj, proj, cw, cb, gw, gb, lam, hf, hb, dya],
        out_shape=(SDS((t, c), BF16), SDS((t, c), BF16), SDS((2, c, t), BF16), SDS((CONV_WIDTH, c), F32),
                   SDS((1, c), F32), SDS((4, ncb, CH, CH), F32), SDS((4, c), F32), SDS((2, c), F32)),
        grid=(ncb,),
        in_specs=[col(0), col(ncb), small(CONV_WIDTH), small(1), dense, small(4), small(2), col(0), col(0), col(0)],
        out_specs=(col(0), col(0), pl.BlockSpec((2, CH, t), lambda i: (0, i, 0)), small(CONV_WIDTH), small(1),
                   dense, small(4), small(2)),
        scratch_shapes=[padded, padded, padded] + [pltpu.VMEM((t, CH), F32)] * 5, job=job)


Q_ROWS = 4
BAND_ROWS = WIN_ROWS + Q_ROWS
BAND_PAIRS = BAND_ROWS // 2
Q_BLOCK = Q_ROWS * GRID_W
BAND = BAND_ROWS * GRID_W
PAIR_W = 2 * GRID_W
N_BOTH = 2 * WIN_ROWS - 2
ENTRY_LEFT_OUT, ENTRY_RIGHT_OUT, ENTRY_OUT = N_BOTH, N_BOTH + 1, N_BOTH + 2
N_ENTRIES = N_BOTH + 3


def _bias_tables(rpb):
    cols = np.arange(GRID_W)
    start = np.clip(cols - WIN_COLS // 2, 0, GRID_W - WIN_COLS)
    valid = (cols[None, :] >= start[:, None]) & (cols[None, :] < start[:, None] + WIN_COLS)
    col_off = np.clip(cols[None, :] - cols[:, None] + WIN_COLS - 1, 0, 2 * WIN_COLS - 2)
    pick_col = jnp.asarray(np.eye(2 * WIN_COLS - 1, dtype=np.float32)[col_off] * valid[..., None])
    by_row = jnp.einsum("hrc,qkc->hrqk", rpb, pick_col, precision=lax.Precision.HIGHEST)
    by_row = jnp.where(jnp.asarray(valid)[None, None], by_row, NEG)
    out = jnp.full_like(by_row[:, :1], NEG)
    first_in, last_in = WIN_ROWS - 1 - WIN_ROWS // 2, 2 * (WIN_ROWS - 1) - WIN_ROWS // 2
    both = jnp.concatenate([by_row[:, :-1], by_row[:, 1:]], axis=-1)
    left_out = jnp.concatenate([out, by_row[:, first_in:first_in + 1]], axis=-1)
    right_out = jnp.concatenate([by_row[:, last_in:last_in + 1], out], axis=-1)
    return jnp.concatenate([both, left_out, right_out, jnp.concatenate([out, out], axis=-1)], axis=1)


def _band_start(m, rows):
    return jnp.clip(Q_ROWS * m - WIN_ROWS // 2, 0, rows - BAND_ROWS)


def _entry(r, key_row, rows):
    w0 = jnp.clip(r - WIN_ROWS // 2, 0, rows - WIN_ROWS)
    left = (key_row >= w0) & (key_row < w0 + WIN_ROWS)
    right = (key_row + 1 >= w0) & (key_row + 1 < w0 + WIN_ROWS)
    return jnp.where(left & right, key_row - r + WIN_ROWS - 1,
                     jnp.where(right, ENTRY_LEFT_OUT, jnp.where(left, ENTRY_RIGHT_OUT, ENTRY_OUT)))


def _transposed_pairs(dst, src_ref):
    for g in range(dst.shape[0]):
        dst[g] = src_ref[pl.ds(g * PAIR_W, PAIR_W), :].T.astype(BF16)


def _band_of(pairs_ref, first_pair, hh):
    heads = pl.ds(hh * HEAD_DIM, HEAD_DIM)
    return jnp.concatenate([pairs_ref[first_pair + g, heads, :] for g in range(BAND_PAIRS)], axis=1)


def _attn_block(qs, kt, tz_ref, hh, m, rows):
    rs = _band_start(m, rows)
    lanes = pl.ds(hh * HEAD_DIM, HEAD_DIM)
    qrows = pl.ds(pl.multiple_of(m * Q_BLOCK, Q_BLOCK), Q_BLOCK)
    band = pl.ds(pl.multiple_of(rs * GRID_W, PAIR_W), BAND)
    entries = [[_entry(Q_ROWS * m + i, rs + 2 * g, rows) for g in range(BAND_PAIRS)] for i in range(Q_ROWS)]
    bias = jnp.concatenate([jnp.concatenate([tz_ref[hh, e] for e in row], axis=1) for row in entries], axis=0)
    q = qs[qrows, lanes]
    s = jnp.dot(q, _band_of(kt, rs // 2, hh), preferred_element_type=F32) * (HEAD_DIM ** -0.5) + bias
    p = jnp.exp(s - jnp.max(s, axis=-1, keepdims=True))
    p = p / jnp.sum(p, axis=-1, keepdims=True)
    return q, p, qrows, band, lanes, entries, rs // 2


def _attn_fwd(proj, tables, width, name, job=None):
    t = proj.shape[0]
    rows = t // GRID_W
    npair = width // LANES
    first = (proj.shape[1] - 3 * width) // LANES

    def body(q_ref, k_ref, v_ref, tz_ref, o_ref, qs, vs, kt):
        qs[...] = q_ref[...].astype(BF16)
        vs[...] = v_ref[...].astype(BF16)
        _transposed_pairs(kt, k_ref)

        def block(m, carry):
            for hh in range(2):
                _, p, qrows, band, lanes, _, _ = _attn_block(qs, kt, tz_ref, hh, m, rows)
                o_ref[qrows, lanes] = jnp.dot(p.astype(BF16), vs[band, lanes], preferred_element_type=F32)
            return carry

        lax.fori_loop(0, rows // Q_ROWS, block, 0, unroll=2)

    col = lambda off: pl.BlockSpec((t, LANES), lambda i: (0, off + i))
    return _call(body, name=name, args=[proj, proj, proj, tables], out_shape=SDS((t, width), F32), grid=(npair,),
                 in_specs=[col(first), col(first + npair), col(first + 2 * npair),
                           pl.BlockSpec((2, N_ENTRIES, GRID_W, PAIR_W), lambda i: (i, 0, 0, 0))],
                 out_specs=col(0),
                 scratch_shapes=[pltpu.VMEM((t, LANES), BF16)] * 2 + [pltpu.VMEM((t // PAIR_W, LANES, PAIR_W), BF16)],
                 job=job)


def _attn_bwd(proj, tables, dyb, name, job=None):
    t, width = dyb.shape
    rows = t // GRID_W
    npair = width // LANES
    first = (proj.shape[1] - 3 * width) // LANES

    def body(q_ref, k_ref, v_ref, tz_ref, do_ref, dq_ref, dk_ref, dv_ref, dt_ref, dtz_ref, dq_s, dk_s, dv_s,
             qs, ks, vs, dos, kt, vt):
        qs[...] = q_ref[...].astype(BF16)
        ks[...] = k_ref[...].astype(BF16)
        vs[...] = v_ref[...].astype(BF16)
        dos[...] = do_ref[...].astype(BF16)
        _transposed_pairs(kt, k_ref)
        _transposed_pairs(vt, v_ref)
        dk_s[...] = jnp.zeros_like(dk_s)
        dv_s[...] = jnp.zeros_like(dv_s)
        dtz_ref[...] = jnp.zeros_like(dtz_ref)

        def block(m, carry):
            for hh in range(2):
                q, p, qrows, band, lanes, entries, first_pair = _attn_block(qs, kt, tz_ref, hh, m, rows)
                do = dos[qrows, lanes]
                dp = jnp.dot(do, _band_of(vt, first_pair, hh), preferred_element_type=F32)
                ds = p * (dp - jnp.sum(dp * p, axis=-1, keepdims=True))
                for i, row in enumerate(entries):
                    for g, e in enumerate(row):
                        dtz_ref[hh, e] += ds[i * GRID_W:(i + 1) * GRID_W, g * PAIR_W:(g + 1) * PAIR_W]
                dsb = (ds * (HEAD_DIM ** -0.5)).astype(BF16)
                dq_s[qrows, lanes] = jnp.dot(dsb, ks[band, lanes], preferred_element_type=F32)
                dk_s[band, lanes] += lax.dot_general(dsb, q, TN, preferred_element_type=F32)
                dv_s[band, lanes] += lax.dot_general(p.astype(BF16), do, TN, preferred_element_type=F32)
            return carry

        lax.fori_loop(0, rows // Q_ROWS, block, 0)
        for n, (src, dst) in enumerate(((dq_s, dq_ref), (dk_s, dk_ref), (dv_s, dv_ref))):
            val = src[...]
            dst[...] = val.astype(BF16)
            dt_ref[n] = val.T.astype(BF16)

    col = lambda off: pl.BlockSpec((t, LANES), lambda i: (0, off + i))
    table = pl.BlockSpec((2, N_ENTRIES, GRID_W, PAIR_W), lambda i: (i, 0, 0, 0))
    pairs = pltpu.VMEM((t // PAIR_W, LANES, PAIR_W), BF16)
    return _call(body, name=name, args=[proj, proj, proj, tables, dyb],
                 out_shape=(SDS((t, width), BF16),) * 3 + (SDS((3, width, t), BF16), SDS(tables.shape, F32)),
                 grid=(npair,),
                 in_specs=[col(first), col(first + npair), col(first + 2 * npair), table, col(0)],
                 out_specs=(col(0), col(0), col(0), pl.BlockSpec((3, LANES, t), lambda i: (0, i, 0)), table),
                 scratch_shapes=[pltpu.VMEM((t, LANES), F32)] * 3 + [pltpu.VMEM((t, LANES), BF16)] * 4 + [pairs, pairs],
                 job=job)


def _adamw_math(w, g, m, v):
    m = ADAM_B1 * m + (1.0 - ADAM_B1) * g
    v = ADAM_B2 * v + (1.0 - ADAM_B2) * (g * g)
    m_hat = m / (1.0 - ADAM_B1 ** ADAM_STEP)
    v_hat = v / (1.0 - ADAM_B2 ** ADAM_STEP)
    delta = -ADAM_LR * (m_hat / (jnp.sqrt(v_hat) + ADAM_EPS) + ADAM_WD * w)
    return delta, m, v


def _sum_partials(p_ref):
    g = p_ref[0].astype(F32)
    for s in range(1, N_CHIP):
        g = g + p_ref[s].astype(F32)
    return g


def _adamw_rows(w, partials, m, v, name, after=()):
    rb, n = w.shape
    tr = 64

    def body(w_ref, p_ref, m_ref, v_ref, *rest):
        g_ref, d_ref, nm_ref, nv_ref = rest[len(after):]
        g = _sum_partials(p_ref)
        g_ref[...] = g
        d_ref[...], nm_ref[...], nv_ref[...] = _adamw_math(w_ref[...], g, m_ref[...], v_ref[...])

    blk = pl.BlockSpec((tr, n), lambda i: (i, 0))
    return _call(body, name=name, args=[w, partials.reshape(N_CHIP, rb, n), m, v, *after],
                 out_shape=(SDS((rb, n), F32),) * 4, grid=(rb // tr,),
                 in_specs=[blk, pl.BlockSpec((N_CHIP, tr, n), lambda i: (0, i, 0)), blk, blk] + [ANY] * len(after),
                 out_specs=(blk,) * 4)


def _adamw_cols(w, partials, m, v, name, after=()):
    d, nb = w.shape
    td = 256

    def body(w_ref, p_ref, m_ref, v_ref, *rest):
        g_ref, d_ref, nm_ref, nv_ref = rest[len(after):]
        g = _sum_partials(p_ref).T
        g_ref[...] = g
        d_ref[...], nm_ref[...], nv_ref[...] = _adamw_math(w_ref[...], g, m_ref[...], v_ref[...])

    blk = pl.BlockSpec((td, nb), lambda i: (i, 0))
    return _call(body, name=name, args=[w, partials.reshape(N_CHIP, nb, d), m, v, *after],
                 out_shape=(SDS((d, nb), F32),) * 4, grid=(d // td,),
                 in_specs=[blk, pl.BlockSpec((N_CHIP, nb, td), lambda i: (0, 0, i)), blk, blk] + [ANY] * len(after),
                 out_specs=(blk,) * 4)


def _adamw_small(w, g, m, v, name):
    def body(w_ref, g_ref, m_ref, v_ref, d_ref, nm_ref, nv_ref):
        d_ref[...], nm_ref[...], nv_ref[...] = _adamw_math(w_ref[...], g_ref[...], m_ref[...], v_ref[...])

    return _call(body, name=name, args=[w, g, m, v], out_shape=(SDS(w.shape, F32),) * 3, in_specs=[WHOLE] * 4,
                 out_specs=(WHOLE,) * 3)


TILE = SUBLANES * LANES


def _pack(arrays):
    parts = []
    for a in arrays:
        flat = a.reshape(-1).astype(F32)
        flat = jnp.pad(flat, (0, -flat.size % TILE))
        parts.append(flat.reshape(-1, LANES))
    return jnp.concatenate(parts, axis=0)


def _unpack(pack, like):
    out, row = [], 0
    for a in like:
        n = int(np.prod(a.shape))
        nrows = -(-n // TILE) * SUBLANES
        out.append(pack[row:row + nrows].reshape(-1)[:n].reshape(a.shape))
        row += nrows
    return out


def _dense_gate_blocks(gate_w):
    w = gate_w.reshape(4, -1, 2, HEAD_DIM, HEAD_DIM)
    zero = jnp.zeros_like(w[:, :, 0])
    top = jnp.concatenate([w[:, :, 0], zero], axis=-1)
    bottom = jnp.concatenate([zero, w[:, :, 1]], axis=-1)
    return jnp.concatenate([top, bottom], axis=-2)


def _diag_gate_blocks(dense, shape):
    even = dense[:, :, :HEAD_DIM, :HEAD_DIM]
    odd = dense[:, :, HEAD_DIM:, HEAD_DIM:]
    return jnp.stack([even, odd], axis=2).reshape(shape)


LARGE = ("ffn1_w_in", "ffn1_w_out", "w_in_mix", "w_out_mix", "ffn2_w_in", "ffn2_w_out")
COLUMN_SHARDED = ("ffn1_w_in", "w_in_mix", "ffn2_w_in")
SHARDED_SMALL = ("lru_conv_w", "lru_lambda")
REPLICATED = ("norm_ffn1", "norm_mix", "lru_conv_b", "lru_gate_w", "lru_gate_b", "attn_rpb", "lru_out_norm",
              "attn_out_norm", "norm_ffn2", "norm_final")
SMALL_ORDER = REPLICATED + SHARDED_SMALL
WEIGHTS = ("norm_ffn1", "ffn1_w_in", "ffn1_w_out", "norm_mix", "w_in_mix", "lru_conv_w", "lru_conv_b", "lru_gate_w",
           "lru_gate_b", "lru_lambda", "attn_rpb", "lru_out_norm", "attn_out_norm", "w_out_mix", "norm_ffn2",
           "ffn2_w_in", "ffn2_w_out", "norm_final")


PARTS = {("gather", "w_in_mix"): 4, ("gather", "ffn2_w_in"): 8,
         ("to_chips", "ffn2_w_in"): 8, ("to_chips", "w_in_mix"): 4, ("to_chips", "ffn1_w_out"): 4}
CARRIES = {
    "gather_ffn1_in": [(("gather", "ffn1_w_in"), 1), (("gather", "small"), 1)],
    "ffn1_hidden": [(("gather", "ffn1_w_out"), 1), (("gather", "w_in_mix"), 1)],
    "ffn1_out": [(("gather", "w_in_mix"), 3)],
    "mix_in_proj": [(("gather", "w_out_mix"), 1), (("gather", "ffn2_w_in"), 1)],
    "lru_fwd": [(("gather", "ffn2_w_in"), 3)],
    "attn_fwd": [(("gather", "ffn2_w_in"), 3)],
    "mix_out_proj": [(("gather", "ffn2_w_in"), 1)],
    "ffn2_hidden": [(("gather", "ffn2_w_out"), 1)],
    "ffn2_bwd": [(("to_sibling", "ffn2_w_out"), 1)],
    "ffn2_in_grad": [(("to_chips", "ffn2_w_out"), 1)],
    "norm_ffn2_bwd": [(("to_sibling", "ffn2_w_in"), 1)],
    "mix_out_grad": [(("to_chips", "ffn2_w_in"), 1)],
    "mix_out_bwd": [(("to_chips", "ffn2_w_in"), 1)],
    "attn_bwd": [(("to_chips", "ffn2_w_in"), 4)],
    "lru_bwd": [(("to_chips", "ffn2_w_in"), 2), (("to_sibling", "w_out_mix"), 1)],
    "mix_in_bwd": [(("to_chips", "w_out_mix"), 1), (("to_sibling", "w_in_mix"), 1)],
    "ffn1_out_grad": [(("to_chips", "w_in_mix"), 2)],
    "ffn1_bwd": [(("to_chips", "w_in_mix"), 2), (("to_sibling", "ffn1_w_out"), 1), (("gather", "small_grads"), 1)],
    "ffn1_in_grad_gate": [(("to_chips", "ffn1_w_out"), 2)],
    "ffn1_in_grad_up": [(("to_chips", "ffn1_w_out"), 2)],
    "norm_ffn1_bwd": [(("to_sibling", "ffn1_w_in"), 1)],
    "gather_late_grads": [(("gather", "late_grads"), 1)],
}


class _Transfer:
    def __init__(self, kind, src, dest, block_rows, parts):
        self.kind, self.src, self.dest = kind, src, dest
        self.ranges, self.taken = _split(block_rows, parts), 0

    def take(self, count):
        lo, hi = self.ranges[self.taken][0], self.ranges[self.taken + count - 1][1]
        self.taken += count
        return _Piece(self.kind, self.src, self.dest, lo, hi)


class _Traffic:
    def __init__(self):
        self.transfers = {}

    def open(self, kind, name, src):
        if kind == "gather":
            dest, rows = _gathered(src), src.shape[0]
        elif kind == "to_sibling":
            dest, rows = SDS((src.shape[0] // 2, src.shape[1]), src.dtype), src.shape[0] // N_DEV
        else:
            dest, rows = SDS(src.shape, src.dtype), src.shape[0] // N_CHIP
        self.transfers[kind, name] = _Transfer(kind, src, dest, rows, PARTS.get((kind, name), 1))

    def _job(self, host):
        moved = [self.transfers[key] for key, _ in CARRIES[host]]
        return moved, _Job([tr.take(count) for tr, (_, count) in zip(moved, CARRIES[host])])

    def carry(self, host, fn, *args, **kw):
        if host not in CARRIES:
            return fn(*args, name=host, **kw)
        moved, job = self._job(host)
        res, landed = fn(*args, name=host, job=job, **kw)
        for tr, arr in zip(moved, landed):
            tr.dest = arr
        return res

    def alone(self, host):
        moved, job = self._job(host)
        for tr, arr in zip(moved, _run_job(job, host)):
            tr.dest = arr

    def result(self, kind, name):
        tr = self.transfers.pop((kind, name))
        assert tr.taken == len(tr.ranges), (kind, name)
        return tr.dest


def _forward_backward(x, target, shards, sharded_small, s):
    c = s["lru_conv_b"].shape[1]
    width = s["attn_out_norm"].shape[1]
    t = x.shape[0]
    traffic = _Traffic()
    carry = traffic.carry
    weight = lambda n: traffic.result("gather", n)

    for n in LARGE:
        traffic.open("gather", n, shards[n])
    traffic.open("gather", "small", sharded_small)
    traffic.alone("gather_ffn1_in")
    full_small = weight("small").reshape(N_DEV, SUBLANES, c // N_DEV)
    conv_w = full_small[:, :CONV_WIDTH].transpose(1, 0, 2).reshape(CONV_WIDTH, c)
    lam = full_small[:, CONV_WIDTH:CONV_WIDTH + 2].transpose(1, 0, 2).reshape(2, c)
    w = {"ffn1_w_in": weight("ffn1_w_in")}
    ffn_out = dict(nt=False, out_dtype=F32, tm=512, tn=512, scale=0.5)
    u1 = _rmsnorm_fwd(x, s["norm_ffn1"], "norm_ffn1")
    g1, up1, hid1, hid1_t = carry("ffn1_hidden", _ffn_hidden, u1, w["ffn1_w_in"])
    w["ffn1_w_out"] = weight("ffn1_w_out")
    h1 = carry("ffn1_out", _mm, hid1, w["ffn1_w_out"], residual=x, **ffn_out)
    w["w_in_mix"] = weight("w_in_mix")
    u2 = _rmsnorm_fwd(h1, s["norm_mix"], "norm_mix")
    proj = carry("mix_in_proj", _mm, u2, w["w_in_mix"], nt=True, out_dtype=F32, tm=512, tn=512)
    w["w_out_mix"] = weight("w_out_mix")
    gw = _dense_gate_blocks(s["lru_gate_w"]).astype(BF16)
    gb = s["lru_gate_b"].reshape(4, c)
    tables, tables_vjp = jax.vjp(_bias_tables, s["attn_rpb"])
    ya, hf, hb = carry("lru_fwd", _lru_fwd, proj, conv_w, s["lru_conv_b"], gw, gb, lam)
    yb = carry("attn_fwd", _attn_fwd, proj, tables, width)
    y, yt = _mixnorm_fwd(ya, yb, s["lru_out_norm"], s["attn_out_norm"], "mix_norm")
    h2 = carry("mix_out_proj", _mm, y, w["w_out_mix"], nt=False, out_dtype=F32, tm=512, tn=512, residual=h1)
    u3 = _rmsnorm_fwd(h2, s["norm_ffn2"], "norm_ffn2")
    w["ffn2_w_in"] = weight("ffn2_w_in")
    g2, up2, hid2, hid2_t = carry("ffn2_hidden", _ffn_hidden, u3, w["ffn2_w_in"])
    w["ffn2_w_out"] = weight("ffn2_w_out")
    h3 = carry("ffn2_out", _mm, hid2, w["ffn2_w_out"], residual=h2, **ffn_out)
    dh3, df2, loss_part, d_norm_final = _final_loss(h3, s["norm_final"], target, "final_loss")

    grads = {}
    grad_of = dict(nt=False, out_dtype=BF16, tm=512, tn=1024)

    def reduce_in_chip(n):
        traffic.open("to_sibling", n, grads[n])

    def reduce_over_chips(n):
        traffic.open("to_chips", n, _pair_sum(grads[n], traffic.result("to_sibling", n), "pair_sum_" + n))

    f = hid2_t.shape[0]
    grads["ffn2_w_out"] = carry("ffn2_out_grad", _mm, hid2_t, df2, **grad_of)
    reduce_in_chip("ffn2_w_out")
    du3, da2_t = carry("ffn2_bwd", _ffn_bwd, df2, g2, up2, w["ffn2_w_in"], w["ffn2_w_out"])
    reduce_over_chips("ffn2_w_out")
    grads["ffn2_w_in"] = carry("ffn2_in_grad", _mm, da2_t.reshape(2 * f, t), u3, **grad_of)
    reduce_in_chip("ffn2_w_in")
    dh2, dh2b, d_norm_ffn2 = carry("norm_ffn2_bwd", _rmsnorm_bwd, du3, h2, s["norm_ffn2"], dh3, 1.0)
    reduce_over_chips("ffn2_w_in")
    grads["w_out_mix"] = carry("mix_out_grad", _mm, yt, dh2b, **grad_of)
    reduce_in_chip("w_out_mix")
    dy = carry("mix_out_bwd", _mm, dh2b, w["w_out_mix"], nt=True, out_dtype=F32, tm=512, tn=512)
    dya, dyb, d_lru_out_norm, d_attn_out_norm = _mixnorm_bwd(dy, ya, yb, s["lru_out_norm"], s["attn_out_norm"],
                                                             "mix_norm_bwd")
    dq, dk, dv, dqkv_t, d_tables = carry("attn_bwd", _attn_bwd, proj, tables, dyb)
    dx_lru, dg_lru, dxg_t, d_conv_w, d_conv_b, d_gw, d_gb, d_lam = carry(
        "lru_bwd", _lru_bwd, proj, conv_w, s["lru_conv_b"], gw, gb, lam, hf, hb, dya)
    reduce_over_chips("w_out_mix")
    rows_of = 2 * c + 3 * width
    lru_rows = carry("mix_in_grad_lru", _mm, dxg_t.reshape(2 * c, t), u2, out_rows=rows_of, **grad_of)
    grads["w_in_mix"] = carry("mix_in_grad_attn", _mm, dqkv_t.reshape(3 * width, t), u2, out_rows=rows_of,
                              row_offset=2 * c, into=lru_rows, **grad_of)
    reduce_in_chip("w_in_mix")
    du2 = carry("mix_in_bwd", _mm, [dx_lru, dg_lru, dq, dk, dv], w["w_in_mix"], nt=False, out_dtype=F32, tm=512,
                tn=512)
    reduce_over_chips("w_in_mix")
    dh1, df1, d_norm_mix = carry("norm_mix_bwd", _rmsnorm_bwd, du2, h1, s["norm_mix"], dh2, 0.5)

    by_device = lambda a: a.reshape(a.shape[0], N_DEV, -1).transpose(1, 0, 2)
    small = {
        "norm_mix": d_norm_mix, "lru_conv_b": d_conv_b, "lru_gate_w": _diag_gate_blocks(d_gw, s["lru_gate_w"].shape),
        "lru_gate_b": d_gb.reshape(s["lru_gate_b"].shape), "attn_rpb": tables_vjp(d_tables)[0],
        "lru_out_norm": d_lru_out_norm, "attn_out_norm": d_attn_out_norm, "norm_ffn2": d_norm_ffn2,
        "norm_final": d_norm_final, "lru_conv_w": by_device(d_conv_w), "lru_lambda": by_device(d_lam),
    }
    early = [small[n] for n in SMALL_ORDER[1:]]
    traffic.open("gather", "small_grads", _pack(early))

    grads["ffn1_w_out"] = carry("ffn1_out_grad", _mm, hid1_t, df1, **grad_of)
    reduce_in_chip("ffn1_w_out")
    du1, da1_t = carry("ffn1_bwd", _ffn_bwd, df1, g1, up1, w["ffn1_w_in"], w["ffn1_w_out"])
    reduce_over_chips("ffn1_w_out")
    gate_rows = carry("ffn1_in_grad_gate", _mm, da1_t, u1, lead=0, out_rows=2 * f, **grad_of)
    grads["ffn1_w_in"] = carry("ffn1_in_grad_up", _mm, da1_t, u1, lead=1, out_rows=2 * f, row_offset=f,
                               into=gate_rows, **grad_of)
    reduce_in_chip("ffn1_w_in")
    grad_x, _, d_norm_ffn1 = carry("norm_ffn1_bwd", _rmsnorm_bwd, du1, x, s["norm_ffn1"], dh1, 1.0)
    traffic.open("gather", "late_grads", _pack([d_norm_ffn1]))
    traffic.alone("gather_late_grads")
    summed = _pair_sum(grads["ffn1_w_in"], traffic.result("to_sibling", "ffn1_w_in"), "pair_sum_ffn1_w_in")
    pending = _to_chips_start(summed, _own_slot(summed, "own_slot_ffn1_w_in"), "to_chips_ffn1_start")
    partials = {n: traffic.result("to_chips", n) for n in LARGE if n != "ffn1_w_in"}
    reduced = (_unpack(_sum_devices(traffic.result("gather", "late_grads"), "sum_late_grads"), [d_norm_ffn1])
               + _unpack(_sum_devices(traffic.result("gather", "small_grads"), "sum_small_grads"), early))
    assert not traffic.transfers, list(traffic.transfers)
    return loss_part[0, 0], grad_x, partials, pending, dict(zip(SMALL_ORDER, reduced))


def _step(x, loss_target, p, m, v):
    me = 4 * lax.axis_index("x") + 2 * lax.axis_index("y") + lax.axis_index("c")

    shards = {n: (_cast_transposed if n in COLUMN_SHARDED else _cast_rows)(p[n], "cast_" + n) for n in LARGE}
    sharded_small = (jnp.pad(p["lru_conv_w"], ((0, SUBLANES - CONV_WIDTH), (0, 0)))
                     + jnp.pad(p["lru_lambda"], ((CONV_WIDTH, SUBLANES - CONV_WIDTH - 2), (0, 0))))
    s = {n: p[n] if n in ("lru_gate_w", "lru_gate_b", "attn_rpb") else p[n].reshape(1, -1) for n in REPLICATED}

    loss_part, grad_x, partials, pending, small = _forward_backward(x, loss_target, shards, sharded_small, s)
    loss = lax.psum(loss_part, ("x", "y", "c"))

    sems, summed, land, token = pending
    update = lambda n, **kw: (_adamw_cols if n in COLUMN_SHARDED else _adamw_rows)(
        p[n], partials[n], m[n], v[n], "adamw_" + n, **kw)
    out = {n: update(n, after=[token]) for n in LARGE if n != "ffn1_w_in"}
    partials["ffn1_w_in"] = _to_chips_wait(sems, summed, land, [o[3] for o in out.values()], "to_chips_ffn1_wait")
    out["ffn1_w_in"] = update("ffn1_w_in")

    g_small = {n: lax.dynamic_index_in_dim(g, me, axis=0, keepdims=False) if n in SHARDED_SMALL else g
               for n, g in small.items()}
    names = SMALL_ORDER
    like = [p[n] for n in names]
    pack_of = lambda d: _pack([d[n].reshape(p[n].shape) for n in names])
    upd = _adamw_small(pack_of(p), pack_of(g_small), pack_of(m), pack_of(v), "adamw_small")
    for n, d_, m_, v_ in zip(names, *[_unpack(u, like) for u in upd]):
        out[n] = (g_small[n].reshape(p[n].shape), d_, m_, v_)
    return loss, grad_x, out


def kernel(x, norm_ffn1, ffn1_w_in, ffn1_w_out, norm_mix, w_in_mix, lru_conv_w, lru_conv_b, lru_gate_w, lru_gate_b, lru_lambda, attn_rpb, lru_out_norm, attn_out_norm, w_out_mix, norm_ffn2, ffn2_w_in, ffn2_w_out, norm_final, loss_target, m_norm_ffn1, m_ffn1_w_in, m_ffn1_w_out, m_norm_mix, m_w_in_mix, m_lru_conv_w, m_lru_conv_b, m_lru_gate_w, m_lru_gate_b, m_lru_lambda, m_attn_rpb, m_lru_out_norm, m_attn_out_norm, m_w_out_mix, m_norm_ffn2, m_ffn2_w_in, m_ffn2_w_out, m_norm_final, v_norm_ffn1, v_ffn1_w_in, v_ffn1_w_out, v_norm_mix, v_w_in_mix, v_lru_conv_w, v_lru_conv_b, v_lru_gate_w, v_lru_gate_b, v_lru_lambda, v_attn_rpb, v_lru_out_norm, v_attn_out_norm, v_w_out_mix, v_norm_ffn2, v_ffn2_w_in, v_ffn2_w_out, v_norm_final):
    given = dict(locals())
    drop_layer = lambda n, a: a if n == "norm_final" else a[0]
    p = {n: drop_layer(n, given[n]) for n in WEIGHTS}
    m = {n: drop_layer(n, given["m_" + n]) for n in WEIGHTS}
    v = {n: drop_layer(n, given["v_" + n]) for n in WEIGHTS}
    loss, grad_x, out = _step(x[0], loss_target[0], p, m, v)
    shaped = lambda n, a: a.reshape(given[n].shape)
    return (loss, grad_x[None], *[shaped(n, out[n][k]) for k in range(4) for n in WEIGHTS])
```

```python
import math

import numpy as np
import jax
import jax.numpy as jnp
from jax import lax
from jax.experimental import pallas as pl
from jax.experimental.pallas import tpu as pltpu

F32 = jnp.float32
BF16 = jnp.bfloat16
SDS = jax.ShapeDtypeStruct

N_DEV = 8
N_CHIP = 4
NORM_EPS = 1e-6
RG_C = 8.0
CONV_WIDTH = 4
HEAD_DIM = 64
GRID_W = 64
WIN_ROWS = 8
WIN_COLS = 16
NEG = -1e30

ADAM_LR = 0.001
ADAM_B1 = 0.9
ADAM_B2 = 0.999
ADAM_EPS = 1e-08
ADAM_WD = 0.01
ADAM_STEP = 10

LANES = 128
SUBLANES = 8
VMEM_LIMIT = 56 * 1024 * 1024

NT = (((1,), (1,)), ((), ()))
TN = (((0,), (0,)), ((), ()))
ANY = pl.BlockSpec(memory_space=pl.ANY)
WHOLE = pl.BlockSpec(memory_space=pltpu.VMEM)
MESH = pl.DeviceIdType.MESH


def _sigmoid(x):
    return 1.0 / (1.0 + jnp.exp(-x))


def _gelu_parts(x):
    c = math.sqrt(2.0 / math.pi)
    t = jnp.tanh(c * (x + 0.044715 * (x * x * x)))
    gelu = 0.5 * x * (1.0 + t)
    dgelu = 0.5 * (1.0 + t) + 0.5 * x * (1.0 - t * t) * (c * (1.0 + 3.0 * 0.044715 * (x * x)))
    return gelu, dgelu


def _expm1(x):
    poly = x * (1.0 + x * (1.0 / 2) * (1.0 + x * (1.0 / 3) * (1.0 + x * (1.0 / 4) * (1.0 + x * (1.0 / 5) * (1.0 + x * (1.0 / 6))))))
    return jnp.where(jnp.abs(x) < 0.25, poly, jnp.exp(x) - 1.0)


def _softplus(x):
    return jnp.maximum(x, 0.0) + jnp.log1p(jnp.exp(-jnp.abs(x)))


class _Piece:
    N_REMOTE = {"gather": 7, "to_sibling": N_CHIP, "to_chips": 3}
    N_LOCAL = {"gather": 1, "to_sibling": 0, "to_chips": 1}

    def __init__(self, kind, src, dest, lo, hi):
        self.kind, self.src, self.dest, self.lo, self.hi = kind, src, dest, lo, hi


RELAY_AT = 60
RUN_AFTER = []


class _Job:
    def __init__(self, pieces):
        self.pieces = list(pieces)
        self.ins = [p.src for p in self.pieces]
        self.out_shapes = [SDS(p.dest.shape, p.dest.dtype) for p in self.pieces]
        self.aliased = [i for i, p in enumerate(self.pieces) if not isinstance(p.dest, SDS)]
        self.n_remote = sum(_Piece.N_REMOTE[p.kind] for p in self.pieces)
        self.n_local = max(sum(_Piece.N_LOCAL[p.kind] for p in self.pieces), 1)

    def _each(self, step, ins, outs, send_sems, recv_sems, local_sems):
        remote = local = 0
        for p, src, dst in zip(self.pieces, ins, outs):
            _EXCHANGES[p.kind](step, p, src, dst, send_sems, recv_sems, local_sems, remote, local)
            remote += _Piece.N_REMOTE[p.kind]
            local += _Piece.N_LOCAL[p.kind]

    def start(self, *refs):
        self._each("start", *refs)

    def relay(self, *refs):
        self._each("relay", *refs)

    def finish(self, *refs):
        self._each("finish", *refs)


def _call(body, *, name, args, out_shape, in_specs, out_specs, grid=(), scratch_shapes=(), aliases=None, job=None):
    single = not isinstance(out_shape, (tuple, list))
    out_shape = (out_shape,) if single else tuple(out_shape)
    out_specs = (out_specs,) if single else tuple(out_specs)
    aliases = dict(aliases or {})
    if RUN_AFTER:
        tokens, n_plain, plain_body = list(RUN_AFTER), len(args), body
        RUN_AFTER.clear()
        body = lambda *refs: plain_body(*refs[:n_plain], *refs[n_plain + len(tokens):])
        args, in_specs = list(args) + tokens, list(in_specs) + [ANY] * len(tokens)
    params = pltpu.CompilerParams(dimension_semantics=("arbitrary",) * len(grid) if grid else None,
                                  vmem_limit_bytes=VMEM_LIMIT)
    if job is None:
        res = pl.pallas_call(body, out_shape=out_shape, grid=grid, in_specs=list(in_specs), out_specs=out_specs,
                             scratch_shapes=list(scratch_shapes), input_output_aliases=aliases, name=name,
                             compiler_params=params)(*args)
        return res[0] if single else res

    n_in, n_out, n_scr = len(args), len(out_shape), len(scratch_shapes)
    j_in, j_out, j_alias = len(job.ins), len(job.out_shapes), len(job.aliased)

    def hosted(*refs):
        ins, refs = refs[:n_in], refs[n_in:]
        j_ins, refs = refs[:j_in], refs[j_in + j_alias:]
        outs, refs = refs[:n_out], refs[n_out:]
        j_outs, refs = refs[:j_out], refs[j_out:]
        scr, sems = refs[:n_scr], refs[n_scr:]
        if grid:
            step = 0
            for axis, size in enumerate(grid):
                step = step * size + pl.program_id(axis)
            steps = math.prod(grid)
            pl.when(step == 0)(lambda: job.start(j_ins, j_outs, *sems))
            body(*ins, *outs, *scr)
            pl.when(step == min(RELAY_AT * steps // 100, steps - 1))(lambda: job.relay(j_ins, j_outs, *sems))
            pl.when(step == steps - 1)(lambda: job.finish(j_ins, j_outs, *sems))
        else:
            job.start(j_ins, j_outs, *sems)
            body(*ins, *outs, *scr)
            job.relay(j_ins, j_outs, *sems)
            job.finish(j_ins, j_outs, *sems)

    res = pl.pallas_call(
        hosted, out_shape=out_shape + tuple(job.out_shapes), grid=grid,
        in_specs=list(in_specs) + [ANY] * (j_in + j_alias), out_specs=out_specs + (ANY,) * j_out,
        scratch_shapes=list(scratch_shapes) + [pltpu.SemaphoreType.DMA((job.n_remote,)),
                                               pltpu.SemaphoreType.DMA((job.n_remote,)),
                                               pltpu.SemaphoreType.DMA((job.n_local,))],
        input_output_aliases={**aliases, **{n_in + j_in + k: n_out + i for k, i in enumerate(job.aliased)}},
        name=name, compiler_params=params)(*args, *job.ins, *[job.pieces[i].dest for i in job.aliased])
    own, carried = res[:n_out], res[n_out:]
    return (own[0] if single else own), carried


def _run_job(job, name):
    return _call(lambda: None, name=name, args=[], out_shape=(), in_specs=[], out_specs=(), job=job)[1]


def _position():
    return lax.axis_index("x"), lax.axis_index("y"), lax.axis_index("c")


def _flat(px, py, pc):
    return 4 * px + 2 * py + pc


def _gather_exchange(step, p, src, dst, send_sems, recv_sems, local_sems, r0, l0):
    x, y, c = _position()
    me, sibling = (x, y, c), (x, y, 1 - c)
    along_x, along_y, diagonal = (1 - x, y), (x, 1 - y), (1 - x, 1 - y)
    south = c == 0
    passed_on = (jnp.where(south, 1 - x, x), jnp.where(south, y, 1 - y))
    passed_to = (jnp.where(south, x, 1 - x), jnp.where(south, 1 - y, y))
    rb, n_rows = p.src.shape[0], p.hi - p.lo
    mine = src.at[pl.ds(p.lo, n_rows), :]

    def rows(block):
        return dst.at[pl.ds(_flat(*block) * rb + p.lo, n_rows), :]

    def copy(k, block, to, own=False):
        return pltpu.make_async_remote_copy(
            src_ref=mine if own else rows(block), dst_ref=rows(block),
            send_sem=send_sems.at[r0 + k], recv_sem=recv_sems.at[r0 + k], device_id=to, device_id_type=MESH)

    local = pltpu.make_async_copy(mine, rows(me), local_sems.at[l0])
    if step == "start":
        local.start()
        copy(0, me, sibling, own=True).start()
        copy(1, me, (*along_x, c), own=True).start()
        copy(2, me, (*along_y, c), own=True).start()
    elif step == "relay":
        copy(1, (*along_x, c), me).wait_recv()
        copy(2, (*along_y, c), me).wait_recv()
        copy(3, (*passed_on, c), (*passed_to, c)).start()
        copy(4, (*along_x, c), sibling).start()
        copy(5, (*along_y, c), sibling).start()
    else:
        copy(3, (*diagonal, c), me).wait_recv()
        copy(6, (*diagonal, c), sibling).start()
        copy(0, sibling, me).wait_recv()
        copy(4, (*along_x, 1 - c), me).wait_recv()
        copy(5, (*along_y, 1 - c), me).wait_recv()
        copy(6, (*diagonal, 1 - c), me).wait_recv()
        copy(0, me, sibling, own=True).wait_send()
        copy(1, me, (*along_x, c), own=True).wait_send()
        copy(2, me, (*along_y, c), own=True).wait_send()
        copy(3, (*passed_on, c), (*passed_to, c)).wait_send()
        copy(4, (*along_x, c), sibling).wait_send()
        copy(5, (*along_y, c), sibling).wait_send()
        copy(6, (*diagonal, c), sibling).wait_send()
        local.wait()


def _sibling_exchange(step, p, src, dst, send_sems, recv_sems, local_sems, r0, l0):
    x, y, c = _position()
    rb, n_rows = p.src.shape[0] // N_DEV, p.hi - p.lo
    for q in range(N_CHIP):
        copy = pltpu.make_async_remote_copy(
            src_ref=src.at[pl.ds((2 * q + 1 - c) * rb + p.lo, n_rows), :],
            dst_ref=dst.at[pl.ds(q * rb + p.lo, n_rows), :],
            send_sem=send_sems.at[r0 + q], recv_sem=recv_sems.at[r0 + q], device_id=(x, y, 1 - c), device_id_type=MESH)
        if step == "start":
            copy.start()
        elif step == "finish":
            copy.wait()


CHIP_FLIPS = [(1, 0), (0, 1), (1, 1)]


def _chips_exchange(step, p, src, dst, send_sems, recv_sems, local_sems, r0, l0):
    x, y, c = _position()
    rb, n_rows = p.src.shape[0] // N_CHIP, p.hi - p.lo

    def slot(ref, px, py):
        return ref.at[pl.ds((2 * px + py) * rb + p.lo, n_rows), :]

    def copy(k, landing=False):
        px = 1 - x if CHIP_FLIPS[k][0] else x
        py = 1 - y if CHIP_FLIPS[k][1] else y
        return pltpu.make_async_remote_copy(
            src_ref=slot(dst, px, py) if landing else slot(src, px, py),
            dst_ref=slot(dst, px, py) if landing else slot(dst, x, y),
            send_sem=send_sems.at[r0 + k], recv_sem=recv_sems.at[r0 + k], device_id=(px, py, c), device_id_type=MESH)

    local = pltpu.make_async_copy(slot(src, x, y), slot(dst, x, y), local_sems.at[l0])
    if step == "start":
        local.start()
        for k in range(3):
            copy(k).start()
    elif step == "finish":
        for k in range(3):
            copy(k, landing=True).wait_recv()
        for k in range(3):
            copy(k).wait_send()
        local.wait()


_EXCHANGES = {"gather": _gather_exchange, "to_sibling": _sibling_exchange, "to_chips": _chips_exchange}


def _gathered(shard):
    return SDS((N_DEV * shard.shape[0], shard.shape[1]), shard.dtype)


def _split(rows, parts):
    cuts = [rows * k // parts // 16 * 16 for k in range(parts)] + [rows]
    return list(zip(cuts[:-1], cuts[1:]))


def _pair_sum(g, from_sibling, name):
    rb, n = g.shape[0] // N_DEV, g.shape[1]
    tr = rb if rb * n * 2 <= 3 * 1024 * 1024 else rb // 2
    core = lax.axis_index("c").astype(jnp.int32).reshape(1)

    def body(c_ref, g_ref, r_ref, o_ref):
        o_ref[...] = (g_ref[...].astype(F32) + r_ref[...].astype(F32)).astype(BF16)

    grid_spec = pltpu.PrefetchScalarGridSpec(
        num_scalar_prefetch=1, grid=(N_CHIP, rb // tr),
        in_specs=[pl.BlockSpec((None, None, tr, n), lambda q, i, c_ref: (q, c_ref[0], i, 0)),
                  pl.BlockSpec((None, tr, n), lambda q, i, c_ref: (q, i, 0))],
        out_specs=pl.BlockSpec((None, tr, n), lambda q, i, c_ref: (q, i, 0)))
    out = pl.pallas_call(
        body, grid_spec=grid_spec, out_shape=SDS((N_CHIP, rb, n), BF16), name=name,
        compiler_params=pltpu.CompilerParams(dimension_semantics=("arbitrary",) * 2, vmem_limit_bytes=VMEM_LIMIT))(
            core, g.reshape(N_CHIP, 2, rb, n), from_sibling.reshape(N_CHIP, rb, n))
    return out.reshape(N_CHIP * rb, n)


SEM = pl.BlockSpec(memory_space=pltpu.SEMAPHORE)
IN_HBM = pl.BlockSpec(memory_space=pltpu.HBM)
SIDE_EFFECT = pltpu.SideEffectType.DATAFLOW_SIDE_EFFECTING


def _own_slot(partials, name):
    rb, n = partials.shape[0] // N_CHIP, partials.shape[1]
    tr = rb // 2
    chip = (2 * lax.axis_index("x") + lax.axis_index("y")).astype(jnp.int32).reshape(1)

    def body(chip_ref, src_ref, dst_ref):
        dst_ref[...] = src_ref[...]

    block = pl.BlockSpec((None, tr, n), lambda i, chip_ref: (chip_ref[0], i, 0))
    grid_spec = pltpu.PrefetchScalarGridSpec(num_scalar_prefetch=1, grid=(rb // tr,), in_specs=[block], out_specs=block)
    out = pl.pallas_call(
        body, grid_spec=grid_spec, out_shape=SDS((N_CHIP, rb, n), partials.dtype), name=name,
        compiler_params=pltpu.CompilerParams(dimension_semantics=("arbitrary",), vmem_limit_bytes=VMEM_LIMIT))(
            chip, partials.reshape(N_CHIP, rb, n))
    return out.reshape(partials.shape)


def _blank_like(src, rows, name):
    return pl.pallas_call(lambda src_ref, out_ref: None, out_shape=SDS((rows, src.shape[1]), src.dtype),
                          in_specs=[ANY], out_specs=ANY, name=name)(src)


def _chip_copies(src_ref, land_ref, sems):
    x, y, c = _position()
    rb = src_ref.shape[0] // N_CHIP
    copies = []
    for k, (fx, fy) in enumerate(CHIP_FLIPS):
        px, py = (1 - x if fx else x), (1 - y if fy else y)
        copies.append(pltpu.make_async_remote_copy(
            src_ref=src_ref.at[pl.ds((2 * px + py) * rb, rb), :], dst_ref=land_ref.at[pl.ds((2 * x + y) * rb, rb), :],
            send_sem=sems[2 * k], recv_sem=sems[2 * k + 1], device_id=(px, py, c), device_id_type=MESH))
    return copies


def _sibling_copies(src_ref, land_ref, sems):
    x, y, c = _position()
    rb = src_ref.shape[0] // N_DEV
    return [pltpu.make_async_remote_copy(
        src_ref=src_ref.at[pl.ds((2 * q + 1 - c) * rb, rb), :], dst_ref=land_ref.at[pl.ds(q * rb, rb), :],
        send_sem=sems[2 * q], recv_sem=sems[2 * q + 1], device_id=(x, y, 1 - c), device_id_type=MESH)
        for q in range(N_CHIP)]


SPLIT_COPIES = {"to_chips": (_chip_copies, 3), "to_sibling": (_sibling_copies, N_CHIP)}


def _split_start(kind, src, land, name):
    copies_of, n_copies = SPLIT_COPIES[kind]

    def body(src_ref, land_ref, *rest):
        sems, token = rest[:2 * n_copies], rest[-1]
        for copy in copies_of(src_ref, land_ref, sems):
            copy.start()
        token[...] = jnp.zeros_like(token)

    res = pl.pallas_call(
        body, name=name,
        out_shape=(pltpu.SemaphoreType.DMA(()),) * (2 * n_copies)
        + (pltpu.HBM(src.shape, src.dtype), pltpu.HBM(land.shape, land.dtype), SDS((SUBLANES, LANES), F32)),
        in_specs=(IN_HBM, IN_HBM), out_specs=(SEM,) * (2 * n_copies) + (IN_HBM, IN_HBM, WHOLE),
        input_output_aliases={0: 2 * n_copies, 1: 2 * n_copies + 1},
        compiler_params=pltpu.CompilerParams(has_side_effects=SIDE_EFFECT))(
            pltpu.with_memory_space_constraint(src, pltpu.HBM), pltpu.with_memory_space_constraint(land, pltpu.HBM))
    return (kind, res[:2 * n_copies], res[-3], res[-2]), res[-1]


def _split_wait(pending, after, name):
    kind, sems, src, land = pending
    copies_of, n_copies = SPLIT_COPIES[kind]

    def body(src_ref, land_ref, *rest):
        for copy in copies_of(src_ref, land_ref, rest[:2 * n_copies]):
            copy.wait_send()
            copy.wait_recv()

    return pl.pallas_call(
        body, name=name, out_shape=(pltpu.HBM(src.shape, src.dtype), pltpu.HBM(land.shape, land.dtype)),
        in_specs=(IN_HBM, IN_HBM) + (SEM,) * (2 * n_copies) + (ANY,) * len(after), out_specs=(IN_HBM, IN_HBM),
        input_output_aliases={0: 0, 1: 1},
        compiler_params=pltpu.CompilerParams(has_side_effects=SIDE_EFFECT))(src, land, *sems, *after)


def _sum_devices(gathered, name):
    r = gathered.shape[0] // N_DEV

    def body(g_ref, o_ref):
        acc = g_ref[0]
        for s in range(1, N_DEV):
            acc = acc + g_ref[s]
        o_ref[...] = acc

    return _call(body, name=name, args=[gathered.reshape(N_DEV, r, LANES)], out_shape=SDS((r, LANES), F32),
                 in_specs=[WHOLE], out_specs=WHOLE)


def _cast_rows(w, name):
    def body(w_ref, o_ref):
        o_ref[...] = w_ref[...].astype(BF16)

    return _call(body, name=name, args=[w], out_shape=SDS(w.shape, BF16), in_specs=[WHOLE], out_specs=WHOLE)


def _cast_transposed(w, name):
    d, n = w.shape
    td = 512

    def body(w_ref, o_ref):
        o_ref[...] = w_ref[...].T.astype(BF16)

    return _call(body, name=name, args=[w], out_shape=SDS((n, d), BF16), grid=(d // td,),
                 in_specs=[pl.BlockSpec((td, n), lambda i: (i, 0))], out_specs=pl.BlockSpec((n, td), lambda i: (0, i)))


ROW_TILE = 256


def _rmsnorm_fwd(h, gain, name):
    t, d = h.shape

    def body(h_ref, g_ref, u_ref):
        x = h_ref[...]
        u_ref[...] = (x * lax.rsqrt(jnp.mean(x * x, axis=-1, keepdims=True) + NORM_EPS) * g_ref[...]).astype(BF16)

    row = pl.BlockSpec((ROW_TILE, d), lambda i: (i, 0))
    return _call(body, name=name, args=[h, gain], out_shape=SDS((t, d), BF16), grid=(t // ROW_TILE,),
                 in_specs=[row, pl.BlockSpec((1, d), lambda i: (0, 0))], out_specs=row)


def _rms_bwd_math(x, gain, dy):
    rstd = lax.rsqrt(jnp.mean(x * x, axis=-1, keepdims=True) + NORM_EPS)
    xhat = x * rstd
    dxh = dy * gain
    dx = rstd * (dxh - xhat * jnp.mean(dxh * xhat, axis=-1, keepdims=True))
    return dx, jnp.sum(dy * xhat, axis=0, keepdims=True)


def _rmsnorm_bwd(du, h, gain, resid, bf_scale, name, job=None):
    t, d = h.shape

    def body(du_ref, h_ref, g_ref, r_ref, dh_ref, dhb_ref, dg_ref):
        @pl.when(pl.program_id(0) == 0)
        def _():
            dg_ref[...] = jnp.zeros_like(dg_ref)

        dx, dg = _rms_bwd_math(h_ref[...], g_ref[...], du_ref[...])
        dh = r_ref[...] + dx
        dh_ref[...] = dh
        dhb_ref[...] = (bf_scale * dh).astype(BF16)
        dg_ref[...] += dg

    row = pl.BlockSpec((ROW_TILE, d), lambda i: (i, 0))
    vec = pl.BlockSpec((1, d), lambda i: (0, 0))
    return _call(body, name=name, args=[du, h, gain, resid],
                 out_shape=(SDS((t, d), F32), SDS((t, d), BF16), SDS((1, d), F32)), grid=(t // ROW_TILE,),
                 in_specs=[row, row, vec, row], out_specs=(row, row, vec), job=job)


def _final_loss(h, gain, target, name):
    t, d = h.shape

    def body(h_ref, g_ref, t_ref, dh_ref, dhb_ref, loss_ref, dg_ref):
        @pl.when(pl.program_id(0) == 0)
        def _():
            dg_ref[...] = jnp.zeros_like(dg_ref)
            loss_ref[...] = jnp.zeros_like(loss_ref)

        x = h_ref[...]
        gain = g_ref[...]
        out = x * lax.rsqrt(jnp.mean(x * x, axis=-1, keepdims=True) + NORM_EPS) * gain
        err = out - t_ref[...]
        loss_ref[...] += 0.5 * jnp.sum(jnp.mean(err * err, axis=-1, keepdims=True), axis=0, keepdims=True)
        dx, dg = _rms_bwd_math(x, gain, err * (1.0 / d))
        dh_ref[...] = dx
        dhb_ref[...] = (0.5 * dx).astype(BF16)
        dg_ref[...] += dg

    row = pl.BlockSpec((ROW_TILE, d), lambda i: (i, 0))
    vec = pl.BlockSpec((1, d), lambda i: (0, 0))
    one = pl.BlockSpec((SUBLANES, LANES), lambda i: (0, 0))
    return _call(body, name=name, args=[h, gain, target],
                 out_shape=(SDS((t, d), F32), SDS((t, d), BF16), SDS((SUBLANES, LANES), F32), SDS((1, d), F32)),
                 grid=(t // ROW_TILE,), in_specs=[row, vec, row], out_specs=(row, row, one, vec))


def _mixnorm_fwd(ya, yb, ga, gb, name):
    t, c = ya.shape

    def body(ya_ref, yb_ref, ga_ref, gb_ref, y_ref, yt_ref):
        for k, (src, g_ref) in enumerate(((ya_ref, ga_ref), (yb_ref, gb_ref))):
            x = src[...]
            u = x * lax.rsqrt(jnp.mean(x * x, axis=-1, keepdims=True) + NORM_EPS) * g_ref[...]
            y_ref[:, k * c:(k + 1) * c] = u.astype(BF16)
            yt_ref[k * c:(k + 1) * c, :] = u.T.astype(BF16)

    row = pl.BlockSpec((ROW_TILE, c), lambda i: (i, 0))
    vec = pl.BlockSpec((1, c), lambda i: (0, 0))
    return _call(body, name=name, args=[ya, yb, ga, gb],
                 out_shape=(SDS((t, 2 * c), BF16), SDS((2 * c, t), BF16)), grid=(t // ROW_TILE,),
                 in_specs=[row, row, vec, vec],
                 out_specs=(pl.BlockSpec((ROW_TILE, 2 * c), lambda i: (i, 0)),
                            pl.BlockSpec((2 * c, ROW_TILE), lambda i: (0, i))))


def _mixnorm_bwd(dy, ya, yb, ga, gb, name):
    t, c = ya.shape

    def body(dy_ref, ya_ref, yb_ref, ga_ref, gb_ref, dya_ref, dyb_ref, dga_ref, dgb_ref):
        @pl.when(pl.program_id(0) == 0)
        def _():
            dga_ref[...] = jnp.zeros_like(dga_ref)
            dgb_ref[...] = jnp.zeros_like(dgb_ref)

        dxa, dga = _rms_bwd_math(ya_ref[...], ga_ref[...], dy_ref[:, :c])
        dxb, dgb = _rms_bwd_math(yb_ref[...], gb_ref[...], dy_ref[:, c:])
        dya_ref[...] = dxa
        dyb_ref[...] = dxb
        dga_ref[...] += dga
        dgb_ref[...] += dgb

    row = pl.BlockSpec((ROW_TILE, c), lambda i: (i, 0))
    vec = pl.BlockSpec((1, c), lambda i: (0, 0))
    return _call(body, name=name, args=[dy, ya, yb, ga, gb],
                 out_shape=(SDS((t, c), F32), SDS((t, c), F32), SDS((1, c), F32), SDS((1, c), F32)),
                 grid=(t // ROW_TILE,),
                 in_specs=[pl.BlockSpec((ROW_TILE, 2 * c), lambda i: (i, 0)), row, row, vec, vec],
                 out_specs=(row, row, vec, vec))


def _tile(n, want):
    return max(t for t in range(LANES, min(n, want) + 1, LANES) if n % t == 0)


def _mm(a, b, *, nt, out_dtype, tm, tn, name, residual=None, scale=None, lead=None, out_rows=None, row_offset=0,
        into=None, job=None):
    parts = list(a) if isinstance(a, (list, tuple)) else [a]
    m = parts[0].shape[-2]
    widths = [p.shape[-1] for p in parts]
    k = sum(widths)
    n = b.shape[0] if nt else b.shape[1]
    tm, tn = _tile(math.gcd(m, row_offset), tm), _tile(n, tn)
    out_rows = m if out_rows is None else out_rows

    def body(*refs):
        a_refs, b_ref, rest = refs[:len(parts)], refs[len(parts)], refs[len(parts) + 1:]
        o_ref = rest[-1]
        out, at = None, 0
        for a_ref, width in zip(a_refs, widths):
            av = a_ref[...].astype(BF16)
            if nt:
                term = lax.dot_general(av, b_ref[:, at:at + width].astype(BF16), NT, preferred_element_type=F32)
            else:
                term = jnp.dot(av, b_ref[at:at + width, :].astype(BF16), preferred_element_type=F32)
            out = term if out is None else out + term
            at += width
        if residual is not None:
            out = rest[0][...] + (out if scale is None else scale * out)
        o_ref[...] = out.astype(out_dtype)

    a_specs =([pl.BlockSpec((tm, width), lambda i, j: (i, 0)) for width in widths] if lead is None
               else [pl.BlockSpec((None, tm, k), lambda i, j: (lead, i, 0))])
    in_specs = a_specs + [pl.BlockSpec((tn, k), lambda i, j: (j, 0)) if nt else pl.BlockSpec((k, tn), lambda i, j: (0, j))]
    args, aliases = parts + [b], {}
    if residual is not None:
        in_specs.append(pl.BlockSpec((tm, tn), lambda i, j: (i, j)))
        args.append(residual)
    if into is not None:
        in_specs.append(ANY)
        aliases[len(args)] = 0
        args.append(into)
    return _call(body, name=name, args=args, out_shape=SDS((out_rows, n), out_dtype), grid=(m // tm, n // tn),
                 in_specs=in_specs, out_specs=pl.BlockSpec((tm, tn), lambda i, j: (row_offset // tm + i, j)),
                 aliases=aliases, job=job)


FFN_TM = 512
FFN_HB = 512


def _ffn_hidden(u, w_in_t, name, job=None):
    t, d = u.shape
    f = w_in_t.shape[0] // 2

    def body(u_ref, w_ref, g_ref, up_ref, hid_ref, hid_t_ref):
        uu = u_ref[...]
        g = lax.dot_general(uu, w_ref[0], NT, preferred_element_type=F32)
        up = lax.dot_general(uu, w_ref[1], NT, preferred_element_type=F32)
        g_ref[...] = g.astype(BF16)
        up_ref[...] = up.astype(BF16)
        hid = (g * _sigmoid(g)) * up
        hid_ref[...] = hid.astype(BF16)
        hid_t_ref[...] = hid.T.astype(BF16)

    pre = pl.BlockSpec((FFN_TM, FFN_HB), lambda i, k: (i, k))
    return _call(body, name=name, args=[u, w_in_t.reshape(2, f, d)],
                 out_shape=(SDS((t, f), BF16), SDS((t, f), BF16), SDS((t, f), BF16), SDS((f, t), BF16)),
                 grid=(t // FFN_TM, f // FFN_HB),
                 in_specs=[pl.BlockSpec((FFN_TM, d), lambda i, k: (i, 0)),
                           pl.BlockSpec((2, FFN_HB, d), lambda i, k: (0, k, 0))],
                 out_specs=(pre, pre, pre, pl.BlockSpec((FFN_HB, FFN_TM), lambda i, k: (k, i))), job=job)


def _ffn_bwd(dfb, gpre, upre, w_in_t, w_out, name, job=None):
    t, d = dfb.shape
    f = w_out.shape[0]
    nk = f // FFN_HB

    def body(df_ref, g_ref, up_ref, w_ref, wo_ref, du_ref, da_t_ref, acc):
        k = pl.program_id(1)

        @pl.when(k == 0)
        def _():
            acc[...] = jnp.zeros_like(acc)

        dhid = lax.dot_general(df_ref[...], wo_ref[...], NT, preferred_element_type=F32)
        g, up = g_ref[...].astype(F32), up_ref[...].astype(F32)
        sig = _sigmoid(g)
        silu = g * sig
        dup = dhid * silu
        dg = dhid * up * (sig * (1.0 + g * (1.0 - sig)))
        da_t_ref[0] = dg.T.astype(BF16)
        da_t_ref[1] = dup.T.astype(BF16)
        acc[...] += (jnp.dot(dg.astype(BF16), w_ref[0], preferred_element_type=F32)
                     + jnp.dot(dup.astype(BF16), w_ref[1], preferred_element_type=F32))

        @pl.when(k == nk - 1)
        def _():
            du_ref[...] = acc[...]

    tok = pl.BlockSpec((FFN_TM, d), lambda i, k: (i, 0))
    pre = pl.BlockSpec((FFN_TM, FFN_HB), lambda i, k: (i, k))
    return _call(body, name=name, args=[dfb, gpre, upre, w_in_t.reshape(2, f, d), w_out],
                 out_shape=(SDS((t, d), F32), SDS((2, f, t), BF16)), grid=(t // FFN_TM, nk),
                 in_specs=[tok, pre, pre, pl.BlockSpec((2, FFN_HB, d), lambda i, k: (0, k, 0)),
                           pl.BlockSpec((FFN_HB, d), lambda i, k: (k, 0))],
                 out_specs=(tok, pl.BlockSpec((2, FFN_HB, FFN_TM), lambda i, k: (0, k, i))),
                 scratch_shapes=[pltpu.VMEM((FFN_TM, d), F32)], job=job)


CH = LANES
PAD = SUBLANES


def _lru_gates(xc, gw_ref, gb_ref, lam_ref, z):
    xcb = xc.astype(BF16)
    r = _sigmoid(jnp.dot(xcb, gw_ref[2 * z], preferred_element_type=F32) + gb_ref[pl.ds(2 * z, 1), :])
    i = _sigmoid(jnp.dot(xcb, gw_ref[2 * z + 1], preferred_element_type=F32) + gb_ref[pl.ds(2 * z + 1, 1), :])
    sp = _softplus(-lam_ref[pl.ds(z, 1), :])
    log_a = (-RG_C * r) * sp
    a = jnp.exp(log_a)
    mult = jnp.sqrt(-_expm1(2.0 * log_a))
    return r, i, sp, a, mult


def _conv(xpad, cw_ref, cb_ref, t):
    xc = cb_ref[...] + cw_ref[pl.ds(0, 1), :] * xpad[pl.ds(PAD - 2, t), :]
    for j in range(1, CONV_WIDTH):
        xc = xc + cw_ref[pl.ds(j, 1), :] * xpad[pl.ds(PAD - 2 + j, t), :]
    return xc


def _fill_padded(pad_ref, value, t):
    pad_ref[pl.ds(0, PAD), :] = jnp.zeros((PAD, CH), F32)
    pad_ref[pl.ds(PAD + t, PAD), :] = jnp.zeros((PAD, CH), F32)
    pad_ref[pl.ds(PAD, t), :] = value


def _scan_pair(t, a_up, b_up, out_up, a_down, b_down, out_down):
    row = lax.broadcasted_iota(jnp.int32, (SUBLANES, CH), 0)

    def compose(a, b, rising):
        for dist in (1, 2, 4):
            shift = dist if rising else SUBLANES - dist
            keep = (row >= dist) if rising else (row < SUBLANES - dist)
            b = jnp.where(keep, b + a * pltpu.roll(b, shift, axis=0), b)
            a = jnp.where(keep, a * pltpu.roll(a, shift, axis=0), a)
        return a, b

    def step(tt, carry):
        hu, hd = carry
        lo = pl.ds(pl.multiple_of(tt * SUBLANES, SUBLANES), SUBLANES)
        hi = pl.ds(pl.multiple_of(t - SUBLANES - tt * SUBLANES, SUBLANES), SUBLANES)
        a, b = compose(a_up[lo, :], b_up[lo, :], True)
        up = b + a * hu
        out_up[lo, :] = up
        a, b = compose(a_down[hi, :], b_down[hi, :], False)
        down = b + a * hd
        out_down[hi, :] = down
        return up[SUBLANES - 1:, :], down[:1, :]

    zero = jnp.zeros((1, CH), F32)
    lax.fori_loop(0, t // SUBLANES, step, (zero, zero), unroll=2)


def _lru_fwd(proj, cw, cb, gw, gb, lam, name, job=None):
    t = proj.shape[0]
    c = cw.shape[1]
    ncb = c // CH

    def body(x_ref, g_ref, cw_ref, cb_ref, gw_ref, gb_ref, lam_ref, ya_ref, hf_ref, hb_ref, xpad, a0, b0, a1, b1):
        _fill_padded(xpad, x_ref[...], t)
        xc = _conv(xpad, cw_ref, cb_ref, t)
        for z, (a_s, b_s) in enumerate(((a0, b0), (a1, b1))):
            _, i, _, a, mult = _lru_gates(xc, gw_ref, gb_ref, lam_ref, z)
            a_s[...] = a
            b_s[...] = mult * (i * xc)
        _scan_pair(t, a0, b0, hf_ref, a1, b1, hb_ref)
        gelu, _ = _gelu_parts(g_ref[...])
        ya_ref[...] = gelu * (hf_ref[...] + hb_ref[...])

    col = lambda off: pl.BlockSpec((t, CH), lambda i: (0, off + i))
    small = lambda rows: pl.BlockSpec((rows, CH), lambda i: (0, i))
    return _call(body, name=name, args=[proj, proj, cw, cb, gw, gb, lam], out_shape=(SDS((t, c), F32),) * 3,
                 grid=(ncb,),
                 in_specs=[col(0), col(ncb), small(CONV_WIDTH), small(1),
                           pl.BlockSpec((4, None, CH, CH), lambda i: (0, i, 0, 0)), small(4), small(2)],
                 out_specs=(col(0),) * 3,
                 scratch_shapes=[pltpu.VMEM((t + 2 * PAD, CH), F32)] + [pltpu.VMEM((t, CH), F32)] * 4, job=job)


def _lru_bwd(proj, cw, cb, gw, gb, lam, hf, hb, dya, name, job=None):
    t = proj.shape[0]
    c = cw.shape[1]
    ncb = c // CH

    def body(x_ref, g_ref, cw_ref, cb_ref, gw_ref, gb_ref, lam_ref, hf_ref, hb_ref, dya_ref,
             dx_ref, dg_ref, dt_ref, dcw_ref, dcb_ref, dgw_ref, dgb_ref, dlam_ref,
             xpad, hpad, dxc, a0, a1, dhs, dh0, dh1):
        _fill_padded(xpad, x_ref[...], t)
        xc = _conv(xpad, cw_ref, cb_ref, t)
        xcb = xc.astype(BF16)
        gates = [_lru_gates(xc, gw_ref, gb_ref, lam_ref, z) for z in range(2)]

        gelu, dgelu = _gelu_parts(g_ref[...])
        dya = dya_ref[...]
        dgate = dya * (hf_ref[...] + hb_ref[...]) * dgelu
        dg_ref[...] = dgate.astype(BF16)
        dt_ref[1] = dgate.T.astype(BF16)
        dhs[...] = dya * gelu

        _fill_padded(hpad, gates[0][3], t)
        a0[...] = hpad[pl.ds(PAD + 1, t), :]
        _fill_padded(hpad, gates[1][3], t)
        a1[...] = hpad[pl.ds(PAD - 1, t), :]
        _scan_pair(t, a1, dhs, dh1, a0, dhs, dh0)

        acc_dxc = jnp.zeros((t, CH), F32)
        for z, (h_ref, dh_ref, shift) in enumerate(((hf_ref, dh0, -1), (hb_ref, dh1, 1))):
            r, i, sp, a, mult = gates[z]
            _fill_padded(hpad, h_ref[...], t)
            h_nb = hpad[pl.ds(PAD + shift, t), :]
            db = dh_ref[...]
            da = db * h_nb
            d_i = db * mult * xc
            acc_dxc = acc_dxc + db * mult * i
            d_mult = db * i * xc
            d_la = da * a - d_mult * (a * a) / mult
            d_r = d_la * (-RG_C * sp)
            dlam_ref[pl.ds(z, 1), :] = (jnp.sum(d_la * (-RG_C * r), axis=0, keepdims=True)
                                        * (-_sigmoid(-lam_ref[pl.ds(z, 1), :])))
            for gate, d_pre in ((0, d_r * r * (1.0 - r)), (1, d_i * i * (1.0 - i))):
                zg = 2 * z + gate
                dgb_ref[pl.ds(zg, 1), :] = jnp.sum(d_pre, axis=0, keepdims=True)
                d_pre_b = d_pre.astype(BF16)
                dgw_ref[zg] = lax.dot_general(xcb, d_pre_b, TN, preferred_element_type=F32)
                acc_dxc = acc_dxc + lax.dot_general(d_pre_b, gw_ref[zg], NT, preferred_element_type=F32)

        dcb_ref[...] = jnp.sum(acc_dxc, axis=0, keepdims=True)
        for j in range(CONV_WIDTH):
            dcw_ref[pl.ds(j, 1), :] = jnp.sum(acc_dxc * xpad[pl.ds(PAD - 2 + j, t), :], axis=0, keepdims=True)
        _fill_padded(dxc, acc_dxc, t)
        dx = cw_ref[pl.ds(0, 1), :] * dxc[pl.ds(PAD + 2, t), :]
        for j in range(1, CONV_WIDTH):
            dx = dx + cw_ref[pl.ds(j, 1), :] * dxc[pl.ds(PAD + 2 - j, t), :]
        dx_ref[...] = dx.astype(BF16)
        dt_ref[0] = dx.T.astype(BF16)

    col = lambda off: pl.BlockSpec((t, CH), lambda i: (0, off + i))
    small = lambda rows: pl.BlockSpec((rows, CH), lambda i: (0, i))
    dense = pl.BlockSpec((4, None, CH, CH), lambda i: (0, i, 0, 0))
    padded = pltpu.VMEM((t + 2 * PAD, CH), F32)
    return _call(
        body, name=name, args=[proj, proj, cw, cb, gw, gb, lam, hf, hb, dya],
        out_shape=(SDS((t, c), BF16), SDS((t, c), BF16), SDS((2, c, t), BF16), SDS((CONV_WIDTH, c), F32),
                   SDS((1, c), F32), SDS((4, ncb, CH, CH), F32), SDS((4, c), F32), SDS((2, c), F32)),
        grid=(ncb,),
        in_specs=[col(0), col(ncb), small(CONV_WIDTH), small(1), dense, small(4), small(2), col(0), col(0), col(0)],
        out_specs=(col(0), col(0), pl.BlockSpec((2, CH, t), lambda i: (0, i, 0)), small(CONV_WIDTH), small(1),
                   dense, small(4), small(2)),
        scratch_shapes=[padded, padded, padded] + [pltpu.VMEM((t, CH), F32)] * 5, job=job)


Q_ROWS = 4
BAND_ROWS = WIN_ROWS + Q_ROWS
BAND_PAIRS = BAND_ROWS // 2
Q_BLOCK = Q_ROWS * GRID_W
BAND = BAND_ROWS * GRID_W
PAIR_W = 2 * GRID_W
N_BOTH = 2 * WIN_ROWS - 2
ENTRY_LEFT_OUT, ENTRY_RIGHT_OUT, ENTRY_OUT = N_BOTH, N_BOTH + 1, N_BOTH + 2
N_ENTRIES = N_BOTH + 3


def _bias_tables(rpb):
    cols = np.arange(GRID_W)
    start = np.clip(cols - WIN_COLS // 2, 0, GRID_W - WIN_COLS)
    valid = (cols[None, :] >= start[:, None]) & (cols[None, :] < start[:, None] + WIN_COLS)
    col_off = np.clip(cols[None, :] - cols[:, None] + WIN_COLS - 1, 0, 2 * WIN_COLS - 2)
    pick_col = jnp.asarray(np.eye(2 * WIN_COLS - 1, dtype=np.float32)[col_off] * valid[..., None])
    by_row = jnp.einsum("hrc,qkc->hrqk", rpb, pick_col, precision=lax.Precision.HIGHEST)
    by_row = jnp.where(jnp.asarray(valid)[None, None], by_row, NEG)
    out = jnp.full_like(by_row[:, :1], NEG)
    first_in, last_in = WIN_ROWS - 1 - WIN_ROWS // 2, 2 * (WIN_ROWS - 1) - WIN_ROWS // 2
    both = jnp.concatenate([by_row[:, :-1], by_row[:, 1:]], axis=-1)
    left_out = jnp.concatenate([out, by_row[:, first_in:first_in + 1]], axis=-1)
    right_out = jnp.concatenate([by_row[:, last_in:last_in + 1], out], axis=-1)
    return jnp.concatenate([both, left_out, right_out, jnp.concatenate([out, out], axis=-1)], axis=1)


def _band_start(m, rows):
    return jnp.clip(Q_ROWS * m - WIN_ROWS // 2, 0, rows - BAND_ROWS)


def _entry(r, key_row, rows):
    w0 = jnp.clip(r - WIN_ROWS // 2, 0, rows - WIN_ROWS)
    left = (key_row >= w0) & (key_row < w0 + WIN_ROWS)
    right = (key_row + 1 >= w0) & (key_row + 1 < w0 + WIN_ROWS)
    return jnp.where(left & right, key_row - r + WIN_ROWS - 1,
                     jnp.where(right, ENTRY_LEFT_OUT, jnp.where(left, ENTRY_RIGHT_OUT, ENTRY_OUT)))


def _transposed_pairs(dst, src_ref):
    for g in range(dst.shape[0]):
        dst[g] = src_ref[pl.ds(g * PAIR_W, PAIR_W), :].T.astype(BF16)


def _band_of(pairs_ref, first_pair, hh):
    heads = pl.ds(hh * HEAD_DIM, HEAD_DIM)
    return jnp.concatenate([pairs_ref[first_pair + g, heads, :] for g in range(BAND_PAIRS)], axis=1)


def _attn_block(qs, kt, tz_ref, hh, m, rows):
    rs = _band_start(m, rows)
    lanes = pl.ds(hh * HEAD_DIM, HEAD_DIM)
    qrows = pl.ds(pl.multiple_of(m * Q_BLOCK, Q_BLOCK), Q_BLOCK)
    band = pl.ds(pl.multiple_of(rs * GRID_W, PAIR_W), BAND)
    entries = [[_entry(Q_ROWS * m + i, rs + 2 * g, rows) for g in range(BAND_PAIRS)] for i in range(Q_ROWS)]
    bias = jnp.concatenate([jnp.concatenate([tz_ref[hh, e] for e in row], axis=1) for row in entries], axis=0)
    q = qs[qrows, lanes]
    s = jnp.dot(q, _band_of(kt, rs // 2, hh), preferred_element_type=F32) * (HEAD_DIM ** -0.5) + bias
    p = jnp.exp(s - jnp.max(s, axis=-1, keepdims=True))
    p = p / jnp.sum(p, axis=-1, keepdims=True)
    return q, p, qrows, band, lanes, entries, rs // 2


def _attn_fwd(proj, tables, width, name, job=None):
    t = proj.shape[0]
    rows = t // GRID_W
    npair = width // LANES
    first = (proj.shape[1] - 3 * width) // LANES

    def body(q_ref, k_ref, v_ref, tz_ref, o_ref, qs, vs, kt):
        qs[...] = q_ref[...].astype(BF16)
        vs[...] = v_ref[...].astype(BF16)
        _transposed_pairs(kt, k_ref)

        def block(m, carry):
            for hh in range(2):
                _, p, qrows, band, lanes, _, _ = _attn_block(qs, kt, tz_ref, hh, m, rows)
                o_ref[qrows, lanes] = jnp.dot(p.astype(BF16), vs[band, lanes], preferred_element_type=F32)
            return carry

        lax.fori_loop(0, rows // Q_ROWS, block, 0, unroll=2)

    col = lambda off: pl.BlockSpec((t, LANES), lambda i: (0, off + i))
    return _call(body, name=name, args=[proj, proj, proj, tables], out_shape=SDS((t, width), F32), grid=(npair,),
                 in_specs=[col(first), col(first + npair), col(first + 2 * npair),
                           pl.BlockSpec((2, N_ENTRIES, GRID_W, PAIR_W), lambda i: (i, 0, 0, 0))],
                 out_specs=col(0),
                 scratch_shapes=[pltpu.VMEM((t, LANES), BF16)] * 2 + [pltpu.VMEM((t // PAIR_W, LANES, PAIR_W), BF16)],
                 job=job)


def _attn_bwd(proj, tables, dyb, name, job=None):
    t, width = dyb.shape
    rows = t // GRID_W
    npair = width // LANES
    first = (proj.shape[1] - 3 * width) // LANES

    def body(q_ref, k_ref, v_ref, tz_ref, do_ref, dq_ref, dk_ref, dv_ref, dt_ref, dtz_ref, dq_s, dk_s, dv_s,
             qs, ks, vs, dos, kt, vt):
        qs[...] = q_ref[...].astype(BF16)
        ks[...] = k_ref[...].astype(BF16)
        vs[...] = v_ref[...].astype(BF16)
        dos[...] = do_ref[...].astype(BF16)
        _transposed_pairs(kt, k_ref)
        _transposed_pairs(vt, v_ref)
        dk_s[...] = jnp.zeros_like(dk_s)
        dv_s[...] = jnp.zeros_like(dv_s)
        dtz_ref[...] = jnp.zeros_like(dtz_ref)

        def block(m, carry):
            for hh in range(2):
                q, p, qrows, band, lanes, entries, first_pair = _attn_block(qs, kt, tz_ref, hh, m, rows)
                do = dos[qrows, lanes]
                dp = jnp.dot(do, _band_of(vt, first_pair, hh), preferred_element_type=F32)
                ds = p * (dp - jnp.sum(dp * p, axis=-1, keepdims=True))
                for i, row in enumerate(entries):
                    for g, e in enumerate(row):
                        dtz_ref[hh, e] += ds[i * GRID_W:(i + 1) * GRID_W, g * PAIR_W:(g + 1) * PAIR_W]
                dsb = (ds * (HEAD_DIM ** -0.5)).astype(BF16)
                dq_s[qrows, lanes] = jnp.dot(dsb, ks[band, lanes], preferred_element_type=F32)
                dk_s[band, lanes] += lax.dot_general(dsb, q, TN, preferred_element_type=F32)
                dv_s[band, lanes] += lax.dot_general(p.astype(BF16), do, TN, preferred_element_type=F32)
            return carry

        lax.fori_loop(0, rows // Q_ROWS, block, 0)
        for n, (src, dst) in enumerate(((dq_s, dq_ref), (dk_s, dk_ref), (dv_s, dv_ref))):
            val = src[...]
            dst[...] = val.astype(BF16)
            dt_ref[n] = val.T.astype(BF16)

    col = lambda off: pl.BlockSpec((t, LANES), lambda i: (0, off + i))
    table = pl.BlockSpec((2, N_ENTRIES, GRID_W, PAIR_W), lambda i: (i, 0, 0, 0))
    pairs = pltpu.VMEM((t // PAIR_W, LANES, PAIR_W), BF16)
    return _call(body, name=name, args=[proj, proj, proj, tables, dyb],
                 out_shape=(SDS((t, width), BF16),) * 3 + (SDS((3, width, t), BF16), SDS(tables.shape, F32)),
                 grid=(npair,),
                 in_specs=[col(first), col(first + npair), col(first + 2 * npair), table, col(0)],
                 out_specs=(col(0), col(0), col(0), pl.BlockSpec((3, LANES, t), lambda i: (0, i, 0)), table),
                 scratch_shapes=[pltpu.VMEM((t, LANES), F32)] * 3 + [pltpu.VMEM((t, LANES), BF16)] * 4 + [pairs, pairs],
                 job=job)


def _adamw_math(w, g, m, v):
    m = ADAM_B1 * m + (1.0 - ADAM_B1) * g
    v = ADAM_B2 * v + (1.0 - ADAM_B2) * (g * g)
    m_hat = m / (1.0 - ADAM_B1 ** ADAM_STEP)
    v_hat = v / (1.0 - ADAM_B2 ** ADAM_STEP)
    delta = -ADAM_LR * (m_hat / (jnp.sqrt(v_hat) + ADAM_EPS) + ADAM_WD * w)
    return delta, m, v


def _sum_partials(p_ref):
    g = p_ref[0].astype(F32)
    for s in range(1, N_CHIP):
        g = g + p_ref[s].astype(F32)
    return g


def _adamw_rows(w, partials, m, v, name, after=()):
    rb, n = w.shape
    tr = 64

    def body(w_ref, p_ref, m_ref, v_ref, *rest):
        g_ref, d_ref, nm_ref, nv_ref = rest[len(after):]
        g = _sum_partials(p_ref)
        g_ref[...] = g
        d_ref[...], nm_ref[...], nv_ref[...] = _adamw_math(w_ref[...], g, m_ref[...], v_ref[...])

    blk = pl.BlockSpec((tr, n), lambda i: (i, 0))
    return _call(body, name=name, args=[w, partials.reshape(N_CHIP, rb, n), m, v, *after],
                 out_shape=(SDS((rb, n), F32),) * 4, grid=(rb // tr,),
                 in_specs=[blk, pl.BlockSpec((N_CHIP, tr, n), lambda i: (0, i, 0)), blk, blk] + [ANY] * len(after),
                 out_specs=(blk,) * 4)


def _adamw_cols(w, partials, m, v, name, after=()):
    d, nb = w.shape
    td = 256

    def body(w_ref, p_ref, m_ref, v_ref, *rest):
        g_ref, d_ref, nm_ref, nv_ref = rest[len(after):]
        g = _sum_partials(p_ref).T
        g_ref[...] = g
        d_ref[...], nm_ref[...], nv_ref[...] = _adamw_math(w_ref[...], g, m_ref[...], v_ref[...])

    blk = pl.BlockSpec((td, nb), lambda i: (i, 0))
    return _call(body, name=name, args=[w, partials.reshape(N_CHIP, nb, d), m, v, *after],
                 out_shape=(SDS((d, nb), F32),) * 4, grid=(d // td,),
                 in_specs=[blk, pl.BlockSpec((N_CHIP, nb, td), lambda i: (0, 0, i)), blk, blk] + [ANY] * len(after),
                 out_specs=(blk,) * 4)


def _adamw_small(w, g, m, v, name):
    def body(w_ref, g_ref, m_ref, v_ref, d_ref, nm_ref, nv_ref):
        d_ref[...], nm_ref[...], nv_ref[...] = _adamw_math(w_ref[...], g_ref[...], m_ref[...], v_ref[...])

    return _call(body, name=name, args=[w, g, m, v], out_shape=(SDS(w.shape, F32),) * 3, in_specs=[WHOLE] * 4,
                 out_specs=(WHOLE,) * 3)


TILE = SUBLANES * LANES


def _pack(arrays):
    parts = []
    for a in arrays:
        flat = a.reshape(-1).astype(F32)
        flat = jnp.pad(flat, (0, -flat.size % TILE))
        parts.append(flat.reshape(-1, LANES))
    return jnp.concatenate(parts, axis=0)


def _unpack(pack, like):
    out, row = [], 0
    for a in like:
        n = int(np.prod(a.shape))
        nrows = -(-n // TILE) * SUBLANES
        out.append(pack[row:row + nrows].reshape(-1)[:n].reshape(a.shape))
        row += nrows
    return out


def _dense_gate_blocks(gate_w):
    w = gate_w.reshape(4, -1, 2, HEAD_DIM, HEAD_DIM)
    zero = jnp.zeros_like(w[:, :, 0])
    top = jnp.concatenate([w[:, :, 0], zero], axis=-1)
    bottom = jnp.concatenate([zero, w[:, :, 1]], axis=-1)
    return jnp.concatenate([top, bottom], axis=-2)


def _diag_gate_blocks(dense, shape):
    even = dense[:, :, :HEAD_DIM, :HEAD_DIM]
    odd = dense[:, :, HEAD_DIM:, HEAD_DIM:]
    return jnp.stack([even, odd], axis=2).reshape(shape)


LARGE = ("ffn1_w_in", "ffn1_w_out", "w_in_mix", "w_out_mix", "ffn2_w_in", "ffn2_w_out")
COLUMN_SHARDED = ("ffn1_w_in", "w_in_mix", "ffn2_w_in")
SHARDED_SMALL = ("lru_conv_w", "lru_lambda")
REPLICATED = ("norm_ffn1", "norm_mix", "lru_conv_b", "lru_gate_w", "lru_gate_b", "attn_rpb", "lru_out_norm",
              "attn_out_norm", "norm_ffn2", "norm_final")
SMALL_ORDER = REPLICATED + SHARDED_SMALL
WEIGHTS = ("norm_ffn1", "ffn1_w_in", "ffn1_w_out", "norm_mix", "w_in_mix", "lru_conv_w", "lru_conv_b", "lru_gate_w",
           "lru_gate_b", "lru_lambda", "attn_rpb", "lru_out_norm", "attn_out_norm", "w_out_mix", "norm_ffn2",
           "ffn2_w_in", "ffn2_w_out", "norm_final")


PARTS = {("gather", "w_in_mix"): 4, ("gather", "ffn2_w_in"): 8}
CARRIES = {
    "gather_ffn1_in": [(("gather", "ffn1_w_in"), 1), (("gather", "small"), 1)],
    "ffn1_hidden": [(("gather", "ffn1_w_out"), 1), (("gather", "w_in_mix"), 1)],
    "ffn1_out": [(("gather", "w_in_mix"), 3)],
    "mix_in_proj": [(("gather", "w_out_mix"), 1), (("gather", "ffn2_w_in"), 1)],
    "lru_fwd": [(("gather", "ffn2_w_in"), 3)],
    "attn_fwd": [(("gather", "ffn2_w_in"), 3)],
    "mix_out_proj": [(("gather", "ffn2_w_in"), 1)],
    "ffn2_hidden": [(("gather", "ffn2_w_out"), 1)],
    "ffn1_bwd": [(("gather", "small_grads"), 1)],
    "gather_late_grads": [(("gather", "late_grads"), 1)],
}


class _Transfer:
    def __init__(self, kind, src, dest, block_rows, parts):
        self.kind, self.src, self.dest = kind, src, dest
        self.ranges, self.taken = _split(block_rows, parts), 0

    def take(self, count):
        lo, hi = self.ranges[self.taken][0], self.ranges[self.taken + count - 1][1]
        self.taken += count
        return _Piece(self.kind, self.src, self.dest, lo, hi)


class _Traffic:
    def __init__(self):
        self.transfers = {}

    def open(self, kind, name, src):
        if kind == "gather":
            dest, rows = _gathered(src), src.shape[0]
        elif kind == "to_sibling":
            dest, rows = SDS((src.shape[0] // 2, src.shape[1]), src.dtype), src.shape[0] // N_DEV
        else:
            dest, rows = SDS(src.shape, src.dtype), src.shape[0] // N_CHIP
        self.transfers[kind, name] = _Transfer(kind, src, dest, rows, PARTS.get((kind, name), 1))

    def _job(self, host):
        moved = [self.transfers[key] for key, _ in CARRIES[host]]
        return moved, _Job([tr.take(count) for tr, (_, count) in zip(moved, CARRIES[host])])

    def carry(self, host, fn, *args, **kw):
        if host not in CARRIES:
            return fn(*args, name=host, **kw)
        moved, job = self._job(host)
        res, landed = fn(*args, name=host, job=job, **kw)
        for tr, arr in zip(moved, landed):
            tr.dest = arr
        return res

    def alone(self, host):
        moved, job = self._job(host)
        for tr, arr in zip(moved, _run_job(job, host)):
            tr.dest = arr

    def result(self, kind, name):
        tr = self.transfers.pop((kind, name))
        assert tr.taken == len(tr.ranges), (kind, name)
        return tr.dest


def _forward_backward(x, target, shards, sharded_small, s):
    c = s["lru_conv_b"].shape[1]
    width = s["attn_out_norm"].shape[1]
    t = x.shape[0]
    traffic = _Traffic()
    carry = traffic.carry
    weight = lambda n: traffic.result("gather", n)

    for n in LARGE:
        traffic.open("gather", n, shards[n])
    traffic.open("gather", "small", sharded_small)
    traffic.alone("gather_ffn1_in")
    full_small = weight("small").reshape(N_DEV, SUBLANES, c // N_DEV)
    conv_w = full_small[:, :CONV_WIDTH].transpose(1, 0, 2).reshape(CONV_WIDTH, c)
    lam = full_small[:, CONV_WIDTH:CONV_WIDTH + 2].transpose(1, 0, 2).reshape(2, c)
    w = {"ffn1_w_in": weight("ffn1_w_in")}
    ffn_out = dict(nt=False, out_dtype=F32, tm=512, tn=512, scale=0.5)
    u1 = _rmsnorm_fwd(x, s["norm_ffn1"], "norm_ffn1")
    g1, up1, hid1, hid1_t = carry("ffn1_hidden", _ffn_hidden, u1, w["ffn1_w_in"])
    w["ffn1_w_out"] = weight("ffn1_w_out")
    h1 = carry("ffn1_out", _mm, hid1, w["ffn1_w_out"], residual=x, **ffn_out)
    w["w_in_mix"] = weight("w_in_mix")
    u2 = _rmsnorm_fwd(h1, s["norm_mix"], "norm_mix")
    proj = carry("mix_in_proj", _mm, u2, w["w_in_mix"], nt=True, out_dtype=F32, tm=512, tn=512)
    w["w_out_mix"] = weight("w_out_mix")
    gw = _dense_gate_blocks(s["lru_gate_w"]).astype(BF16)
    gb = s["lru_gate_b"].reshape(4, c)
    tables, tables_vjp = jax.vjp(_bias_tables, s["attn_rpb"])
    ya, hf, hb = carry("lru_fwd", _lru_fwd, proj, conv_w, s["lru_conv_b"], gw, gb, lam)
    yb = carry("attn_fwd", _attn_fwd, proj, tables, width)
    y, yt = _mixnorm_fwd(ya, yb, s["lru_out_norm"], s["attn_out_norm"], "mix_norm")
    h2 = carry("mix_out_proj", _mm, y, w["w_out_mix"], nt=False, out_dtype=F32, tm=512, tn=512, residual=h1)
    u3 = _rmsnorm_fwd(h2, s["norm_ffn2"], "norm_ffn2")
    w["ffn2_w_in"] = weight("ffn2_w_in")
    g2, up2, hid2, hid2_t = carry("ffn2_hidden", _ffn_hidden, u3, w["ffn2_w_in"])
    w["ffn2_w_out"] = weight("ffn2_w_out")
    h3 = carry("ffn2_out", _mm, hid2, w["ffn2_w_out"], residual=h2, **ffn_out)
    dh3, df2, loss_part, d_norm_final = _final_loss(h3, s["norm_final"], target, "final_loss")

    grads = {}
    grad_of = dict(nt=False, out_dtype=BF16, tm=512, tn=1024)

    to_sibling, to_chips = {}, {}

    def reduce_in_chip(n):
        land = _blank_like(grads[n], grads[n].shape[0] // 2, "landing_" + n)
        to_sibling[n], token = _split_start("to_sibling", grads[n], land, "to_sibling_" + n)
        RUN_AFTER.append(token)

    def reduce_over_chips(n, after):
        own, got = _split_wait(to_sibling.pop(n), [after], "from_sibling_" + n)
        summed = _pair_sum(own, got, "pair_sum_" + n)
        to_chips[n], token = _split_start("to_chips", summed, _own_slot(summed, "own_slot_" + n), "to_chips_" + n)
        RUN_AFTER.append(token)
        return token

    f = hid2_t.shape[0]
    grads["ffn2_w_out"] = carry("ffn2_out_grad", _mm, hid2_t, df2, **grad_of)
    reduce_in_chip("ffn2_w_out")
    du3, da2_t = carry("ffn2_bwd", _ffn_bwd, df2, g2, up2, w["ffn2_w_in"], w["ffn2_w_out"])
    reduce_over_chips("ffn2_w_out", du3)
    grads["ffn2_w_in"] = carry("ffn2_in_grad", _mm, da2_t.reshape(2 * f, t), u3, **grad_of)
    reduce_in_chip("ffn2_w_in")
    dh2, dh2b, d_norm_ffn2 = carry("norm_ffn2_bwd", _rmsnorm_bwd, du3, h2, s["norm_ffn2"], dh3, 1.0)
    grads["w_out_mix"] = carry("mix_out_grad", _mm, yt, dh2b, **grad_of)
    reduce_over_chips("ffn2_w_in", grads["w_out_mix"])
    reduce_in_chip("w_out_mix")
    dy = carry("mix_out_bwd", _mm, dh2b, w["w_out_mix"], nt=True, out_dtype=F32, tm=512, tn=512)
    dya, dyb, d_lru_out_norm, d_attn_out_norm = _mixnorm_bwd(dy, ya, yb, s["lru_out_norm"], s["attn_out_norm"],
                                                             "mix_norm_bwd")
    dq, dk, dv, dqkv_t, d_tables = carry("attn_bwd", _attn_bwd, proj, tables, dyb)
    reduce_over_chips("w_out_mix", dq)
    dx_lru, dg_lru, dxg_t, d_conv_w, d_conv_b, d_gw, d_gb, d_lam = carry(
        "lru_bwd", _lru_bwd, proj, conv_w, s["lru_conv_b"], gw, gb, lam, hf, hb, dya)
    rows_of = 2 * c + 3 * width
    lru_rows = carry("mix_in_grad_lru", _mm, dxg_t.reshape(2 * c, t), u2, out_rows=rows_of, **grad_of)
    grads["w_in_mix"] = carry("mix_in_grad_attn", _mm, dqkv_t.reshape(3 * width, t), u2, out_rows=rows_of,
                              row_offset=2 * c, into=lru_rows, **grad_of)
    reduce_in_chip("w_in_mix")
    du2 = carry("mix_in_bwd", _mm, [dx_lru, dg_lru, dq, dk, dv], w["w_in_mix"], nt=False, out_dtype=F32, tm=512,
                tn=512)
    dh1, df1, d_norm_mix = carry("norm_mix_bwd", _rmsnorm_bwd, du2, h1, s["norm_mix"], dh2, 0.5)
    reduce_over_chips("w_in_mix", dh1)

    by_device = lambda a: a.reshape(a.shape[0], N_DEV, -1).transpose(1, 0, 2)
    small = {
        "norm_mix": d_norm_mix, "lru_conv_b": d_conv_b, "lru_gate_w": _diag_gate_blocks(d_gw, s["lru_gate_w"].shape),
        "lru_gate_b": d_gb.reshape(s["lru_gate_b"].shape), "attn_rpb": tables_vjp(d_tables)[0],
        "lru_out_norm": d_lru_out_norm, "attn_out_norm": d_attn_out_norm, "norm_ffn2": d_norm_ffn2,
        "norm_final": d_norm_final, "lru_conv_w": by_device(d_conv_w), "lru_lambda": by_device(d_lam),
    }
    early = [small[n] for n in SMALL_ORDER[1:]]
    traffic.open("gather", "small_grads", _pack(early))

    grads["ffn1_w_out"] = carry("ffn1_out_grad", _mm, hid1_t, df1, **grad_of)
    reduce_in_chip("ffn1_w_out")
    du1, da1_t = carry("ffn1_bwd", _ffn_bwd, df1, g1, up1, w["ffn1_w_in"], w["ffn1_w_out"])
    reduce_over_chips("ffn1_w_out", du1)
    grads["ffn1_w_in"] = carry("ffn1_in_grad", _mm, da1_t.reshape(2 * f, t), u1, **grad_of)
    reduce_in_chip("ffn1_w_in")
    grad_x, _, d_norm_ffn1 = carry("norm_ffn1_bwd", _rmsnorm_bwd, du1, x, s["norm_ffn1"], dh1, 1.0)
    traffic.open("gather", "late_grads", _pack([d_norm_ffn1]))
    traffic.alone("gather_late_grads")
    reduced = (_unpack(_sum_devices(traffic.result("gather", "late_grads"), "sum_late_grads"), [d_norm_ffn1])
               + _unpack(_sum_devices(traffic.result("gather", "small_grads"), "sum_small_grads"), early))
    last_token = reduce_over_chips("ffn1_w_in", reduced[0])
    RUN_AFTER.clear()
    assert not traffic.transfers and not to_sibling, (list(traffic.transfers), list(to_sibling))
    return loss_part[0, 0], grad_x, to_chips, last_token, dict(zip(SMALL_ORDER, reduced))


def _step(x, loss_target, p, m, v):
    me = 4 * lax.axis_index("x") + 2 * lax.axis_index("y") + lax.axis_index("c")

    shards = {n: (_cast_transposed if n in COLUMN_SHARDED else _cast_rows)(p[n], "cast_" + n) for n in LARGE}
    sharded_small = (jnp.pad(p["lru_conv_w"], ((0, SUBLANES - CONV_WIDTH), (0, 0)))
                     + jnp.pad(p["lru_lambda"], ((CONV_WIDTH, SUBLANES - CONV_WIDTH - 2), (0, 0))))
    s = {n: p[n] if n in ("lru_gate_w", "lru_gate_b", "attn_rpb") else p[n].reshape(1, -1) for n in REPLICATED}

    loss_part, grad_x, to_chips, last_token, small = _forward_backward(x, loss_target, shards, sharded_small, s)
    loss = lax.psum(loss_part, ("x", "y", "c"))

    def update(n, after):
        partials = _split_wait(to_chips[n], after, "from_chips_" + n)[1]
        return (_adamw_cols if n in COLUMN_SHARDED else _adamw_rows)(p[n], partials, m[n], v[n], "adamw_" + n)

    out = {n: update(n, [last_token]) for n in LARGE if n != "ffn1_w_in"}
    out["ffn1_w_in"] = update("ffn1_w_in", [o[3] for o in out.values()])

    g_small = {n: lax.dynamic_index_in_dim(g, me, axis=0, keepdims=False) if n in SHARDED_SMALL else g
               for n, g in small.items()}
    names = SMALL_ORDER
    like = [p[n] for n in names]
    pack_of = lambda d: _pack([d[n].reshape(p[n].shape) for n in names])
    upd = _adamw_small(pack_of(p), pack_of(g_small), pack_of(m), pack_of(v), "adamw_small")
    for n, d_, m_, v_ in zip(names, *[_unpack(u, like) for u in upd]):
        out[n] = (g_small[n].reshape(p[n].shape), d_, m_, v_)
    return loss, grad_x, out


def kernel(x, norm_ffn1, ffn1_w_in, ffn1_w_out, norm_mix, w_in_mix, lru_conv_w, lru_conv_b, lru_gate_w, lru_gate_b, lru_lambda, attn_rpb, lru_out_norm, attn_out_norm, w_out_mix, norm_ffn2, ffn2_w_in, ffn2_w_out, norm_final, loss_target, m_norm_ffn1, m_ffn1_w_in, m_ffn1_w_out, m_norm_mix, m_w_in_mix, m_lru_conv_w, m_lru_conv_b, m_lru_gate_w, m_lru_gate_b, m_lru_lambda, m_attn_rpb, m_lru_out_norm, m_attn_out_norm, m_w_out_mix, m_norm_ffn2, m_ffn2_w_in, m_ffn2_w_out, m_norm_final, v_norm_ffn1, v_ffn1_w_in, v_ffn1_w_out, v_norm_mix, v_w_in_mix, v_lru_conv_w, v_lru_conv_b, v_lru_gate_w, v_lru_gate_b, v_lru_lambda, v_attn_rpb, v_lru_out_norm, v_attn_out_norm, v_w_out_mix, v_norm_ffn2, v_ffn2_w_in, v_ffn2_w_out, v_norm_final):
    given = dict(locals())
    drop_layer = lambda n, a: a if n == "norm_final" else a[0]
    p = {n: drop_layer(n, given[n]) for n in WEIGHTS}
    m = {n: drop_layer(n, given["m_" + n]) for n in WEIGHTS}
    v = {n: drop_layer(n, given["v_" + n]) for n in WEIGHTS}
    loss, grad_x, out = _step(x[0], loss_target[0], p, m, v)
    shaped = lambda n, a: a.reshape(given[n].shape)
    return (loss, grad_x[None], *[shaped(n, out[n][k]) for k in range(4) for n in WEIGHTS])
```

```python
import math

import numpy as np
import jax
import jax.numpy as jnp
from jax import lax
from jax.experimental import pallas as pl
from jax.experimental.pallas import tpu as pltpu

F32 = jnp.float32
BF16 = jnp.bfloat16
SDS = jax.ShapeDtypeStruct

N_DEV = 8
N_CHIP = 4
NORM_EPS = 1e-6
RG_C = 8.0
CONV_WIDTH = 4
HEAD_DIM = 64
GRID_W = 64
WIN_ROWS = 8
WIN_COLS = 16
NEG = -1e30

ADAM_LR = 0.001
ADAM_B1 = 0.9
ADAM_B2 = 0.999
ADAM_EPS = 1e-08
ADAM_WD = 0.01
ADAM_STEP = 10

LANES = 128
SUBLANES = 8
VMEM_LIMIT = 56 * 1024 * 1024

NT = (((1,), (1,)), ((), ()))
TN = (((0,), (0,)), ((), ()))
ANY = pl.BlockSpec(memory_space=pl.ANY)
WHOLE = pl.BlockSpec(memory_space=pltpu.VMEM)
MESH = pl.DeviceIdType.MESH


def _sigmoid(x):
    return 1.0 / (1.0 + jnp.exp(-x))


def _gelu_parts(x):
    c = math.sqrt(2.0 / math.pi)
    t = jnp.tanh(c * (x + 0.044715 * (x * x * x)))
    gelu = 0.5 * x * (1.0 + t)
    dgelu = 0.5 * (1.0 + t) + 0.5 * x * (1.0 - t * t) * (c * (1.0 + 3.0 * 0.044715 * (x * x)))
    return gelu, dgelu


def _expm1(x):
    poly = x * (1.0 + x * (1.0 / 2) * (1.0 + x * (1.0 / 3) * (1.0 + x * (1.0 / 4) * (1.0 + x * (1.0 / 5) * (1.0 + x * (1.0 / 6))))))
    return jnp.where(jnp.abs(x) < 0.25, poly, jnp.exp(x) - 1.0)


def _softplus(x):
    return jnp.maximum(x, 0.0) + jnp.log1p(jnp.exp(-jnp.abs(x)))


class _Piece:
    N_REMOTE = {"gather": 7, "to_sibling": N_CHIP, "to_chips": 3}
    N_LOCAL = {"gather": 1, "to_sibling": 0, "to_chips": 1}

    def __init__(self, kind, src, dest, lo, hi):
        self.kind, self.src, self.dest, self.lo, self.hi = kind, src, dest, lo, hi


RELAY_AT = 60
RUN_AFTER = []


class _Job:
    def __init__(self, pieces):
        self.pieces = list(pieces)
        self.ins = [p.src for p in self.pieces]
        self.out_shapes = [SDS(p.dest.shape, p.dest.dtype) for p in self.pieces]
        self.aliased = [i for i, p in enumerate(self.pieces) if not isinstance(p.dest, SDS)]
        self.n_remote = sum(_Piece.N_REMOTE[p.kind] for p in self.pieces)
        self.n_local = max(sum(_Piece.N_LOCAL[p.kind] for p in self.pieces), 1)

    def _each(self, step, ins, outs, send_sems, recv_sems, local_sems):
        remote = local = 0
        for p, src, dst in zip(self.pieces, ins, outs):
            _EXCHANGES[p.kind](step, p, src, dst, send_sems, recv_sems, local_sems, remote, local)
            remote += _Piece.N_REMOTE[p.kind]
            local += _Piece.N_LOCAL[p.kind]

    def start(self, *refs):
        self._each("start", *refs)

    def relay(self, *refs):
        self._each("relay", *refs)

    def finish(self, *refs):
        self._each("finish", *refs)


def _call(body, *, name, args, out_shape, in_specs, out_specs, grid=(), scratch_shapes=(), aliases=None, job=None):
    single = not isinstance(out_shape, (tuple, list))
    out_shape = (out_shape,) if single else tuple(out_shape)
    out_specs = (out_specs,) if single else tuple(out_specs)
    aliases = dict(aliases or {})
    if RUN_AFTER:
        tokens, n_plain, plain_body = list(RUN_AFTER), len(args), body
        RUN_AFTER.clear()
        body = lambda *refs: plain_body(*refs[:n_plain], *refs[n_plain + len(tokens):])
        args, in_specs = list(args) + tokens, list(in_specs) + [ANY] * len(tokens)
    params = pltpu.CompilerParams(dimension_semantics=("arbitrary",) * len(grid) if grid else None,
                                  vmem_limit_bytes=VMEM_LIMIT)
    if job is None:
        res = pl.pallas_call(body, out_shape=out_shape, grid=grid, in_specs=list(in_specs), out_specs=out_specs,
                             scratch_shapes=list(scratch_shapes), input_output_aliases=aliases, name=name,
                             compiler_params=params)(*args)
        return res[0] if single else res

    n_in, n_out, n_scr = len(args), len(out_shape), len(scratch_shapes)
    j_in, j_out, j_alias = len(job.ins), len(job.out_shapes), len(job.aliased)

    def hosted(*refs):
        ins, refs = refs[:n_in], refs[n_in:]
        j_ins, refs = refs[:j_in], refs[j_in + j_alias:]
        outs, refs = refs[:n_out], refs[n_out:]
        j_outs, refs = refs[:j_out], refs[j_out:]
        scr, sems = refs[:n_scr], refs[n_scr:]
        if grid:
            step = 0
            for axis, size in enumerate(grid):
                step = step * size + pl.program_id(axis)
            steps = math.prod(grid)
            pl.when(step == 0)(lambda: job.start(j_ins, j_outs, *sems))
            body(*ins, *outs, *scr)
            pl.when(step == min(RELAY_AT * steps // 100, steps - 1))(lambda: job.relay(j_ins, j_outs, *sems))
            pl.when(step == steps - 1)(lambda: job.finish(j_ins, j_outs, *sems))
        else:
            job.start(j_ins, j_outs, *sems)
            body(*ins, *outs, *scr)
            job.relay(j_ins, j_outs, *sems)
            job.finish(j_ins, j_outs, *sems)

    res = pl.pallas_call(
        hosted, out_shape=out_shape + tuple(job.out_shapes), grid=grid,
        in_specs=list(in_specs) + [ANY] * (j_in + j_alias), out_specs=out_specs + (ANY,) * j_out,
        scratch_shapes=list(scratch_shapes) + [pltpu.SemaphoreType.DMA((job.n_remote,)),
                                               pltpu.SemaphoreType.DMA((job.n_remote,)),
                                               pltpu.SemaphoreType.DMA((job.n_local,))],
        input_output_aliases={**aliases, **{n_in + j_in + k: n_out + i for k, i in enumerate(job.aliased)}},
        name=name, compiler_params=params)(*args, *job.ins, *[job.pieces[i].dest for i in job.aliased])
    own, carried = res[:n_out], res[n_out:]
    return (own[0] if single else own), carried


def _run_job(job, name):
    return _call(lambda: None, name=name, args=[], out_shape=(), in_specs=[], out_specs=(), job=job)[1]


def _position():
    return lax.axis_index("x"), lax.axis_index("y"), lax.axis_index("c")


def _flat(px, py, pc):
    return 4 * px + 2 * py + pc


def _gather_exchange(step, p, src, dst, send_sems, recv_sems, local_sems, r0, l0):
    x, y, c = _position()
    me, sibling = (x, y, c), (x, y, 1 - c)
    along_x, along_y, diagonal = (1 - x, y), (x, 1 - y), (1 - x, 1 - y)
    south = c == 0
    passed_on = (jnp.where(south, 1 - x, x), jnp.where(south, y, 1 - y))
    passed_to = (jnp.where(south, x, 1 - x), jnp.where(south, 1 - y, y))
    rb, n_rows = p.src.shape[0], p.hi - p.lo
    mine = src.at[pl.ds(p.lo, n_rows), :]

    def rows(block):
        return dst.at[pl.ds(_flat(*block) * rb + p.lo, n_rows), :]

    def copy(k, block, to, own=False):
        return pltpu.make_async_remote_copy(
            src_ref=mine if own else rows(block), dst_ref=rows(block),
            send_sem=send_sems.at[r0 + k], recv_sem=recv_sems.at[r0 + k], device_id=to, device_id_type=MESH)

    local = pltpu.make_async_copy(mine, rows(me), local_sems.at[l0])
    if step == "start":
        local.start()
        copy(0, me, sibling, own=True).start()
        copy(1, me, (*along_x, c), own=True).start()
        copy(2, me, (*along_y, c), own=True).start()
    elif step == "relay":
        copy(1, (*along_x, c), me).wait_recv()
        copy(2, (*along_y, c), me).wait_recv()
        copy(3, (*passed_on, c), (*passed_to, c)).start()
        copy(4, (*along_x, c), sibling).start()
        copy(5, (*along_y, c), sibling).start()
    else:
        copy(3, (*diagonal, c), me).wait_recv()
        copy(6, (*diagonal, c), sibling).start()
        copy(0, sibling, me).wait_recv()
        copy(4, (*along_x, 1 - c), me).wait_recv()
        copy(5, (*along_y, 1 - c), me).wait_recv()
        copy(6, (*diagonal, 1 - c), me).wait_recv()
        copy(0, me, sibling, own=True).wait_send()
        copy(1, me, (*along_x, c), own=True).wait_send()
        copy(2, me, (*along_y, c), own=True).wait_send()
        copy(3, (*passed_on, c), (*passed_to, c)).wait_send()
        copy(4, (*along_x, c), sibling).wait_send()
        copy(5, (*along_y, c), sibling).wait_send()
        copy(6, (*diagonal, c), sibling).wait_send()
        local.wait()


def _sibling_exchange(step, p, src, dst, send_sems, recv_sems, local_sems, r0, l0):
    x, y, c = _position()
    rb, n_rows = p.src.shape[0] // N_DEV, p.hi - p.lo
    for q in range(N_CHIP):
        copy = pltpu.make_async_remote_copy(
            src_ref=src.at[pl.ds((2 * q + 1 - c) * rb + p.lo, n_rows), :],
            dst_ref=dst.at[pl.ds(q * rb + p.lo, n_rows), :],
            send_sem=send_sems.at[r0 + q], recv_sem=recv_sems.at[r0 + q], device_id=(x, y, 1 - c), device_id_type=MESH)
        if step == "start":
            copy.start()
        elif step == "finish":
            copy.wait()


CHIP_FLIPS = [(1, 0), (0, 1), (1, 1)]


def _chips_exchange(step, p, src, dst, send_sems, recv_sems, local_sems, r0, l0):
    x, y, c = _position()
    rb, n_rows = p.src.shape[0] // N_CHIP, p.hi - p.lo

    def slot(ref, px, py):
        return ref.at[pl.ds((2 * px + py) * rb + p.lo, n_rows), :]

    def copy(k, landing=False):
        px = 1 - x if CHIP_FLIPS[k][0] else x
        py = 1 - y if CHIP_FLIPS[k][1] else y
        return pltpu.make_async_remote_copy(
            src_ref=slot(dst, px, py) if landing else slot(src, px, py),
            dst_ref=slot(dst, px, py) if landing else slot(dst, x, y),
            send_sem=send_sems.at[r0 + k], recv_sem=recv_sems.at[r0 + k], device_id=(px, py, c), device_id_type=MESH)

    local = pltpu.make_async_copy(slot(src, x, y), slot(dst, x, y), local_sems.at[l0])
    if step == "start":
        local.start()
        for k in range(3):
            copy(k).start()
    elif step == "finish":
        for k in range(3):
            copy(k, landing=True).wait_recv()
        for k in range(3):
            copy(k).wait_send()
        local.wait()


_EXCHANGES = {"gather": _gather_exchange, "to_sibling": _sibling_exchange, "to_chips": _chips_exchange}


def _gathered(shard):
    return SDS((N_DEV * shard.shape[0], shard.shape[1]), shard.dtype)


def _split(rows, parts):
    cuts = [rows * k // parts // 16 * 16 for k in range(parts)] + [rows]
    return list(zip(cuts[:-1], cuts[1:]))


def _pair_sum(g, from_sibling, name):
    rb, n = g.shape[0] // N_DEV, g.shape[1]
    tr = rb if rb * n * 2 <= 3 * 1024 * 1024 else rb // 2
    core = lax.axis_index("c").astype(jnp.int32).reshape(1)

    def body(c_ref, g_ref, r_ref, o_ref):
        o_ref[...] = (g_ref[...].astype(F32) + r_ref[...].astype(F32)).astype(BF16)

    grid_spec = pltpu.PrefetchScalarGridSpec(
        num_scalar_prefetch=1, grid=(N_CHIP, rb // tr),
        in_specs=[pl.BlockSpec((None, None, tr, n), lambda q, i, c_ref: (q, c_ref[0], i, 0)),
                  pl.BlockSpec((None, tr, n), lambda q, i, c_ref: (q, i, 0))],
        out_specs=pl.BlockSpec((None, tr, n), lambda q, i, c_ref: (q, i, 0)))
    out = pl.pallas_call(
        body, grid_spec=grid_spec, out_shape=SDS((N_CHIP, rb, n), BF16), name=name,
        compiler_params=pltpu.CompilerParams(dimension_semantics=("arbitrary",) * 2, vmem_limit_bytes=VMEM_LIMIT))(
            core, g.reshape(N_CHIP, 2, rb, n), from_sibling.reshape(N_CHIP, rb, n))
    return out.reshape(N_CHIP * rb, n)


SEM = pl.BlockSpec(memory_space=pltpu.SEMAPHORE)
IN_HBM = pl.BlockSpec(memory_space=pltpu.HBM)
SIDE_EFFECT = pltpu.SideEffectType.DATAFLOW_SIDE_EFFECTING


def _own_slot(partials, name):
    rb, n = partials.shape[0] // N_CHIP, partials.shape[1]
    tr = rb // 2
    chip = (2 * lax.axis_index("x") + lax.axis_index("y")).astype(jnp.int32).reshape(1)

    def body(chip_ref, src_ref, dst_ref):
        dst_ref[...] = src_ref[...]

    block = pl.BlockSpec((None, tr, n), lambda i, chip_ref: (chip_ref[0], i, 0))
    grid_spec = pltpu.PrefetchScalarGridSpec(num_scalar_prefetch=1, grid=(rb // tr,), in_specs=[block], out_specs=block)
    out = pl.pallas_call(
        body, grid_spec=grid_spec, out_shape=SDS((N_CHIP, rb, n), partials.dtype), name=name,
        compiler_params=pltpu.CompilerParams(dimension_semantics=("arbitrary",), vmem_limit_bytes=VMEM_LIMIT))(
            chip, partials.reshape(N_CHIP, rb, n))
    return out.reshape(partials.shape)


def _blank_like(src, rows, name):
    return pl.pallas_call(lambda src_ref, out_ref: None, out_shape=SDS((rows, src.shape[1]), src.dtype),
                          in_specs=[ANY], out_specs=ANY, name=name)(src)


def _chip_copies(src_ref, land_ref, sems):
    x, y, c = _position()
    rb = src_ref.shape[0] // N_CHIP
    copies = []
    for k, (fx, fy) in enumerate(CHIP_FLIPS):
        px, py = (1 - x if fx else x), (1 - y if fy else y)
        copies.append(pltpu.make_async_remote_copy(
            src_ref=src_ref.at[pl.ds((2 * px + py) * rb, rb), :], dst_ref=land_ref.at[pl.ds((2 * x + y) * rb, rb), :],
            send_sem=sems[2 * k], recv_sem=sems[2 * k + 1], device_id=(px, py, c), device_id_type=MESH))
    return copies


def _sibling_copies(src_ref, land_ref, sems):
    x, y, c = _position()
    rb = src_ref.shape[0] // N_DEV
    return [pltpu.make_async_remote_copy(
        src_ref=src_ref.at[pl.ds((2 * q + 1 - c) * rb, rb), :], dst_ref=land_ref.at[pl.ds(q * rb, rb), :],
        send_sem=sems[2 * q], recv_sem=sems[2 * q + 1], device_id=(x, y, 1 - c), device_id_type=MESH)
        for q in range(N_CHIP)]


SPLIT_COPIES = {"to_chips": (_chip_copies, 3), "to_sibling": (_sibling_copies, N_CHIP)}


def _split_start(kind, src, land, name):
    copies_of, n_copies = SPLIT_COPIES[kind]

    def body(src_ref, land_ref, *rest):
        sems, token = rest[:2 * n_copies], rest[-1]
        for copy in copies_of(src_ref, land_ref, sems):
            copy.start()
        token[...] = jnp.zeros_like(token)

    res = pl.pallas_call(
        body, name=name,
        out_shape=(pltpu.SemaphoreType.DMA(()),) * (2 * n_copies)
        + (pltpu.HBM(src.shape, src.dtype), pltpu.HBM(land.shape, land.dtype), SDS((SUBLANES, LANES), F32)),
        in_specs=(IN_HBM, IN_HBM), out_specs=(SEM,) * (2 * n_copies) + (IN_HBM, IN_HBM, WHOLE),
        input_output_aliases={0: 2 * n_copies, 1: 2 * n_copies + 1},
        compiler_params=pltpu.CompilerParams(has_side_effects=SIDE_EFFECT))(
            pltpu.with_memory_space_constraint(src, pltpu.HBM), pltpu.with_memory_space_constraint(land, pltpu.HBM))
    return (kind, res[:2 * n_copies], res[-3], res[-2]), res[-1]


def _split_wait(pending, after, name):
    kind, sems, src, land = pending
    copies_of, n_copies = SPLIT_COPIES[kind]

    def body(src_ref, land_ref, *rest):
        for copy in copies_of(src_ref, land_ref, rest[:2 * n_copies]):
            copy.wait_send()
            copy.wait_recv()

    return pl.pallas_call(
        body, name=name, out_shape=(pltpu.HBM(src.shape, src.dtype), pltpu.HBM(land.shape, land.dtype)),
        in_specs=(IN_HBM, IN_HBM) + (SEM,) * (2 * n_copies) + (ANY,) * len(after), out_specs=(IN_HBM, IN_HBM),
        input_output_aliases={0: 0, 1: 1},
        compiler_params=pltpu.CompilerParams(has_side_effects=SIDE_EFFECT))(src, land, *sems, *after)


def _sum_devices(gathered, name):
    r = gathered.shape[0] // N_DEV

    def body(g_ref, o_ref):
        acc = g_ref[0]
        for s in range(1, N_DEV):
            acc = acc + g_ref[s]
        o_ref[...] = acc

    return _call(body, name=name, args=[gathered.reshape(N_DEV, r, LANES)], out_shape=SDS((r, LANES), F32),
                 in_specs=[WHOLE], out_specs=WHOLE)


def _cast_rows(w, name):
    def body(w_ref, o_ref):
        o_ref[...] = w_ref[...].astype(BF16)

    return _call(body, name=name, args=[w], out_shape=SDS(w.shape, BF16), in_specs=[WHOLE], out_specs=WHOLE)


def _cast_transposed(w, name):
    d, n = w.shape
    td = 512

    def body(w_ref, o_ref):
        o_ref[...] = w_ref[...].T.astype(BF16)

    return _call(body, name=name, args=[w], out_shape=SDS((n, d), BF16), grid=(d // td,),
                 in_specs=[pl.BlockSpec((td, n), lambda i: (i, 0))], out_specs=pl.BlockSpec((n, td), lambda i: (0, i)))


ROW_TILE = 256


def _rmsnorm_fwd(h, gain, name):
    t, d = h.shape

    def body(h_ref, g_ref, u_ref):
        x = h_ref[...]
        u_ref[...] = (x * lax.rsqrt(jnp.mean(x * x, axis=-1, keepdims=True) + NORM_EPS) * g_ref[...]).astype(BF16)

    row = pl.BlockSpec((ROW_TILE, d), lambda i: (i, 0))
    return _call(body, name=name, args=[h, gain], out_shape=SDS((t, d), BF16), grid=(t // ROW_TILE,),
                 in_specs=[row, pl.BlockSpec((1, d), lambda i: (0, 0))], out_specs=row)


def _rms_bwd_math(x, gain, dy):
    rstd = lax.rsqrt(jnp.mean(x * x, axis=-1, keepdims=True) + NORM_EPS)
    xhat = x * rstd
    dxh = dy * gain
    dx = rstd * (dxh - xhat * jnp.mean(dxh * xhat, axis=-1, keepdims=True))
    return dx, jnp.sum(dy * xhat, axis=0, keepdims=True)


def _rmsnorm_bwd(du, h, gain, resid, bf_scale, name, job=None):
    t, d = h.shape

    def body(du_ref, h_ref, g_ref, r_ref, dh_ref, dhb_ref, dg_ref):
        @pl.when(pl.program_id(0) == 0)
        def _():
            dg_ref[...] = jnp.zeros_like(dg_ref)

        dx, dg = _rms_bwd_math(h_ref[...], g_ref[...], du_ref[...])
        dh = r_ref[...] + dx
        dh_ref[...] = dh
        dhb_ref[...] = (bf_scale * dh).astype(BF16)
        dg_ref[...] += dg

    row = pl.BlockSpec((ROW_TILE, d), lambda i: (i, 0))
    vec = pl.BlockSpec((1, d), lambda i: (0, 0))
    return _call(body, name=name, args=[du, h, gain, resid],
                 out_shape=(SDS((t, d), F32), SDS((t, d), BF16), SDS((1, d), F32)), grid=(t // ROW_TILE,),
                 in_specs=[row, row, vec, row], out_specs=(row, row, vec), job=job)


def _final_loss(h, gain, target, name):
    t, d = h.shape

    def body(h_ref, g_ref, t_ref, dh_ref, dhb_ref, loss_ref, dg_ref):
        @pl.when(pl.program_id(0) == 0)
        def _():
            dg_ref[...] = jnp.zeros_like(dg_ref)
            loss_ref[...] = jnp.zeros_like(loss_ref)

        x = h_ref[...]
        gain = g_ref[...]
        out = x * lax.rsqrt(jnp.mean(x * x, axis=-1, keepdims=True) + NORM_EPS) * gain
        err = out - t_ref[...]
        loss_ref[...] += 0.5 * jnp.sum(jnp.mean(err * err, axis=-1, keepdims=True), axis=0, keepdims=True)
        dx, dg = _rms_bwd_math(x, gain, err * (1.0 / d))
        dh_ref[...] = dx
        dhb_ref[...] = (0.5 * dx).astype(BF16)
        dg_ref[...] += dg

    row = pl.BlockSpec((ROW_TILE, d), lambda i: (i, 0))
    vec = pl.BlockSpec((1, d), lambda i: (0, 0))
    one = pl.BlockSpec((SUBLANES, LANES), lambda i: (0, 0))
    return _call(body, name=name, args=[h, gain, target],
                 out_shape=(SDS((t, d), F32), SDS((t, d), BF16), SDS((SUBLANES, LANES), F32), SDS((1, d), F32)),
                 grid=(t // ROW_TILE,), in_specs=[row, vec, row], out_specs=(row, row, one, vec))


def _mixnorm_fwd(ya, yb, ga, gb, name):
    t, c = ya.shape

    def body(ya_ref, yb_ref, ga_ref, gb_ref, y_ref, yt_ref):
        for k, (src, g_ref) in enumerate(((ya_ref, ga_ref), (yb_ref, gb_ref))):
            x = src[...]
            u = x * lax.rsqrt(jnp.mean(x * x, axis=-1, keepdims=True) + NORM_EPS) * g_ref[...]
            y_ref[:, k * c:(k + 1) * c] = u.astype(BF16)
            yt_ref[k * c:(k + 1) * c, :] = u.T.astype(BF16)

    row = pl.BlockSpec((ROW_TILE, c), lambda i: (i, 0))
    vec = pl.BlockSpec((1, c), lambda i: (0, 0))
    return _call(body, name=name, args=[ya, yb, ga, gb],
                 out_shape=(SDS((t, 2 * c), BF16), SDS((2 * c, t), BF16)), grid=(t // ROW_TILE,),
                 in_specs=[row, row, vec, vec],
                 out_specs=(pl.BlockSpec((ROW_TILE, 2 * c), lambda i: (i, 0)),
                            pl.BlockSpec((2 * c, ROW_TILE), lambda i: (0, i))))


def _mixnorm_bwd(dy, ya, yb, ga, gb, name):
    t, c = ya.shape

    def body(dy_ref, ya_ref, yb_ref, ga_ref, gb_ref, dya_ref, dyb_ref, dga_ref, dgb_ref):
        @pl.when(pl.program_id(0) == 0)
        def _():
            dga_ref[...] = jnp.zeros_like(dga_ref)
            dgb_ref[...] = jnp.zeros_like(dgb_ref)

        dxa, dga = _rms_bwd_math(ya_ref[...], ga_ref[...], dy_ref[:, :c])
        dxb, dgb = _rms_bwd_math(yb_ref[...], gb_ref[...], dy_ref[:, c:])
        dya_ref[...] = dxa
        dyb_ref[...] = dxb
        dga_ref[...] += dga
        dgb_ref[...] += dgb

    row = pl.BlockSpec((ROW_TILE, c), lambda i: (i, 0))
    vec = pl.BlockSpec((1, c), lambda i: (0, 0))
    return _call(body, name=name, args=[dy, ya, yb, ga, gb],
                 out_shape=(SDS((t, c), F32), SDS((t, c), F32), SDS((1, c), F32), SDS((1, c), F32)),
                 grid=(t // ROW_TILE,),
                 in_specs=[pl.BlockSpec((ROW_TILE, 2 * c), lambda i: (i, 0)), row, row, vec, vec],
                 out_specs=(row, row, vec, vec))


def _tile(n, want):
    return max(t for t in range(LANES, min(n, want) + 1, LANES) if n % t == 0)


def _mm(a, b, *, nt, out_dtype, tm, tn, name, residual=None, scale=None, take=None, out_rows=None, row_offset=0,
        into=None, job=None):
    parts = list(a) if isinstance(a, (list, tuple)) else [a]
    widths = [p.shape[-1] for p in parts]
    k = sum(widths)
    n = b.shape[0] if nt else b.shape[1]
    if take is None:
        m, which = parts[0].shape[0], lambda i: i
        tm = _tile(math.gcd(m, row_offset), tm)
    else:
        tm, tiles, which = take
        m = tm * tiles
    tn = _tile(n, tn)
    out_rows = m if out_rows is None else out_rows

    def body(*refs):
        a_refs, b_ref, rest = refs[:len(parts)], refs[len(parts)], refs[len(parts) + 1:]
        o_ref = rest[-1]
        out, at = None, 0
        for a_ref, width in zip(a_refs, widths):
            av = a_ref[...].astype(BF16)
            if nt:
                term = lax.dot_general(av, b_ref[:, at:at + width].astype(BF16), NT, preferred_element_type=F32)
            else:
                term = jnp.dot(av, b_ref[at:at + width, :].astype(BF16), preferred_element_type=F32)
            out = term if out is None else out + term
            at += width
        if residual is not None:
            out = rest[0][...] + (out if scale is None else scale * out)
        o_ref[...] = out.astype(out_dtype)

    a_specs = [pl.BlockSpec((tm, width), lambda i, j: (which(i), 0)) for width in widths]
    in_specs = a_specs + [pl.BlockSpec((tn, k), lambda i, j: (j, 0)) if nt else pl.BlockSpec((k, tn), lambda i, j: (0, j))]
    args, aliases = parts + [b], {}
    if residual is not None:
        in_specs.append(pl.BlockSpec((tm, tn), lambda i, j: (i, j)))
        args.append(residual)
    if into is not None:
        in_specs.append(ANY)
        aliases[len(args)] = 0
        args.append(into)
    return _call(body, name=name, args=args, out_shape=SDS((out_rows, n), out_dtype), grid=(m // tm, n // tn),
                 in_specs=in_specs, out_specs=pl.BlockSpec((tm, tn), lambda i, j: (row_offset // tm + i, j)),
                 aliases=aliases, job=job)


FFN_TM = 512
FFN_HB = 512


def _ffn_hidden(u, w_in_t, name, job=None):
    t, d = u.shape
    f = w_in_t.shape[0] // 2

    def body(u_ref, w_ref, g_ref, up_ref, hid_ref, hid_t_ref):
        uu = u_ref[...]
        g = lax.dot_general(uu, w_ref[0], NT, preferred_element_type=F32)
        up = lax.dot_general(uu, w_ref[1], NT, preferred_element_type=F32)
        g_ref[...] = g.astype(BF16)
        up_ref[...] = up.astype(BF16)
        hid = (g * _sigmoid(g)) * up
        hid_ref[...] = hid.astype(BF16)
        hid_t_ref[...] = hid.T.astype(BF16)

    pre = pl.BlockSpec((FFN_TM, FFN_HB), lambda i, k: (i, k))
    return _call(body, name=name, args=[u, w_in_t.reshape(2, f, d)],
                 out_shape=(SDS((t, f), BF16), SDS((t, f), BF16), SDS((t, f), BF16), SDS((f, t), BF16)),
                 grid=(t // FFN_TM, f // FFN_HB),
                 in_specs=[pl.BlockSpec((FFN_TM, d), lambda i, k: (i, 0)),
                           pl.BlockSpec((2, FFN_HB, d), lambda i, k: (0, k, 0))],
                 out_specs=(pre, pre, pre, pl.BlockSpec((FFN_HB, FFN_TM), lambda i, k: (k, i))), job=job)


def _ffn_bwd(dfb, gpre, upre, w_in_t, w_out, name, job=None):
    t, d = dfb.shape
    f = w_out.shape[0]
    nk = f // FFN_HB

    def body(df_ref, g_ref, up_ref, w_ref, wo_ref, du_ref, da_t_ref, acc):
        k = pl.program_id(1)

        @pl.when(k == 0)
        def _():
            acc[...] = jnp.zeros_like(acc)

        dhid = lax.dot_general(df_ref[...], wo_ref[...], NT, preferred_element_type=F32)
        g, up = g_ref[...].astype(F32), up_ref[...].astype(F32)
        sig = _sigmoid(g)
        silu = g * sig
        dup = dhid * silu
        dg = dhid * up * (sig * (1.0 + g * (1.0 - sig)))
        da_t_ref[0] = dg.T.astype(BF16)
        da_t_ref[1] = dup.T.astype(BF16)
        acc[...] += (jnp.dot(dg.astype(BF16), w_ref[0], preferred_element_type=F32)
                     + jnp.dot(dup.astype(BF16), w_ref[1], preferred_element_type=F32))

        @pl.when(k == nk - 1)
        def _():
            du_ref[...] = acc[...]

    tok = pl.BlockSpec((FFN_TM, d), lambda i, k: (i, 0))
    pre = pl.BlockSpec((FFN_TM, FFN_HB), lambda i, k: (i, k))
    return _call(body, name=name, args=[dfb, gpre, upre, w_in_t.reshape(2, f, d), w_out],
                 out_shape=(SDS((t, d), F32), SDS((2, f, t), BF16)), grid=(t // FFN_TM, nk),
                 in_specs=[tok, pre, pre, pl.BlockSpec((2, FFN_HB, d), lambda i, k: (0, k, 0)),
                           pl.BlockSpec((FFN_HB, d), lambda i, k: (k, 0))],
                 out_specs=(tok, pl.BlockSpec((2, FFN_HB, FFN_TM), lambda i, k: (0, k, i))),
                 scratch_shapes=[pltpu.VMEM((FFN_TM, d), F32)], job=job)


CH = LANES
PAD = SUBLANES


def _lru_gates(xc, gw_ref, gb_ref, lam_ref, z):
    xcb = xc.astype(BF16)
    r = _sigmoid(jnp.dot(xcb, gw_ref[2 * z], preferred_element_type=F32) + gb_ref[pl.ds(2 * z, 1), :])
    i = _sigmoid(jnp.dot(xcb, gw_ref[2 * z + 1], preferred_element_type=F32) + gb_ref[pl.ds(2 * z + 1, 1), :])
    sp = _softplus(-lam_ref[pl.ds(z, 1), :])
    log_a = (-RG_C * r) * sp
    a = jnp.exp(log_a)
    mult = jnp.sqrt(-_expm1(2.0 * log_a))
    return r, i, sp, a, mult


def _conv(xpad, cw_ref, cb_ref, t):
    xc = cb_ref[...] + cw_ref[pl.ds(0, 1), :] * xpad[pl.ds(PAD - 2, t), :]
    for j in range(1, CONV_WIDTH):
        xc = xc + cw_ref[pl.ds(j, 1), :] * xpad[pl.ds(PAD - 2 + j, t), :]
    return xc


def _fill_padded(pad_ref, value, t):
    pad_ref[pl.ds(0, PAD), :] = jnp.zeros((PAD, CH), F32)
    pad_ref[pl.ds(PAD + t, PAD), :] = jnp.zeros((PAD, CH), F32)
    pad_ref[pl.ds(PAD, t), :] = value


def _scan_pair(t, a_up, b_up, out_up, a_down, b_down, out_down):
    row = lax.broadcasted_iota(jnp.int32, (SUBLANES, CH), 0)

    def compose(a, b, rising):
        for dist in (1, 2, 4):
            shift = dist if rising else SUBLANES - dist
            keep = (row >= dist) if rising else (row < SUBLANES - dist)
            b = jnp.where(keep, b + a * pltpu.roll(b, shift, axis=0), b)
            a = jnp.where(keep, a * pltpu.roll(a, shift, axis=0), a)
        return a, b

    def step(tt, carry):
        hu, hd = carry
        lo = pl.ds(pl.multiple_of(tt * SUBLANES, SUBLANES), SUBLANES)
        hi = pl.ds(pl.multiple_of(t - SUBLANES - tt * SUBLANES, SUBLANES), SUBLANES)
        a, b = compose(a_up[lo, :], b_up[lo, :], True)
        up = b + a * hu
        out_up[lo, :] = up
        a, b = compose(a_down[hi, :], b_down[hi, :], False)
        down = b + a * hd
        out_down[hi, :] = down
        return up[SUBLANES - 1:, :], down[:1, :]

    zero = jnp.zeros((1, CH), F32)
    lax.fori_loop(0, t // SUBLANES, step, (zero, zero), unroll=2)


def _lru_fwd(proj, cw, cb, gw, gb, lam, name, job=None):
    t = proj.shape[0]
    c = cw.shape[1]
    ncb = c // CH

    def body(x_ref, g_ref, cw_ref, cb_ref, gw_ref, gb_ref, lam_ref, ya_ref, hf_ref, hb_ref, xpad, a0, b0, a1, b1):
        _fill_padded(xpad, x_ref[...], t)
        xc = _conv(xpad, cw_ref, cb_ref, t)
        for z, (a_s, b_s) in enumerate(((a0, b0), (a1, b1))):
            _, i, _, a, mult = _lru_gates(xc, gw_ref, gb_ref, lam_ref, z)
            a_s[...] = a
            b_s[...] = mult * (i * xc)
        _scan_pair(t, a0, b0, hf_ref, a1, b1, hb_ref)
        gelu, _ = _gelu_parts(g_ref[...])
        ya_ref[...] = gelu * (hf_ref[...] + hb_ref[...])

    col = lambda off: pl.BlockSpec((t, CH), lambda i: (0, off + i))
    small = lambda rows: pl.BlockSpec((rows, CH), lambda i: (0, i))
    return _call(body, name=name, args=[proj, proj, cw, cb, gw, gb, lam], out_shape=(SDS((t, c), F32),) * 3,
                 grid=(ncb,),
                 in_specs=[col(0), col(ncb), small(CONV_WIDTH), small(1),
                           pl.BlockSpec((4, None, CH, CH), lambda i: (0, i, 0, 0)), small(4), small(2)],
                 out_specs=(col(0),) * 3,
                 scratch_shapes=[pltpu.VMEM((t + 2 * PAD, CH), F32)] + [pltpu.VMEM((t, CH), F32)] * 4, job=job)


def _lru_bwd(proj, cw, cb, gw, gb, lam, hf, hb, dya, name, job=None):
    t = proj.shape[0]
    c = cw.shape[1]
    ncb = c // CH

    def body(x_ref, g_ref, cw_ref, cb_ref, gw_ref, gb_ref, lam_ref, hf_ref, hb_ref, dya_ref,
             dx_ref, dg_ref, dt_ref, dcw_ref, dcb_ref, dgw_ref, dgb_ref, dlam_ref,
             xpad, hpad, dxc, a0, a1, dhs, dh0, dh1):
        _fill_padded(xpad, x_ref[...], t)
        xc = _conv(xpad, cw_ref, cb_ref, t)
        xcb = xc.astype(BF16)
        gates = [_lru_gates(xc, gw_ref, gb_ref, lam_ref, z) for z in range(2)]

        gelu, dgelu = _gelu_parts(g_ref[...])
        dya = dya_ref[...]
        dgate = dya * (hf_ref[...] + hb_ref[...]) * dgelu
        dg_ref[...] = dgate.astype(BF16)
        dt_ref[1] = dgate.T.astype(BF16)
        dhs[...] = dya * gelu

        _fill_padded(hpad, gates[0][3], t)
        a0[...] = hpad[pl.ds(PAD + 1, t), :]
        _fill_padded(hpad, gates[1][3], t)
        a1[...] = hpad[pl.ds(PAD - 1, t), :]
        _scan_pair(t, a1, dhs, dh1, a0, dhs, dh0)

        acc_dxc = jnp.zeros((t, CH), F32)
        for z, (h_ref, dh_ref, shift) in enumerate(((hf_ref, dh0, -1), (hb_ref, dh1, 1))):
            r, i, sp, a, mult = gates[z]
            _fill_padded(hpad, h_ref[...], t)
            h_nb = hpad[pl.ds(PAD + shift, t), :]
            db = dh_ref[...]
            da = db * h_nb
            d_i = db * mult * xc
            acc_dxc = acc_dxc + db * mult * i
            d_mult = db * i * xc
            d_la = da * a - d_mult * (a * a) / mult
            d_r = d_la * (-RG_C * sp)
            dlam_ref[pl.ds(z, 1), :] = (jnp.sum(d_la * (-RG_C * r), axis=0, keepdims=True)
                                        * (-_sigmoid(-lam_ref[pl.ds(z, 1), :])))
            for gate, d_pre in ((0, d_r * r * (1.0 - r)), (1, d_i * i * (1.0 - i))):
                zg = 2 * z + gate
                dgb_ref[pl.ds(zg, 1), :] = jnp.sum(d_pre, axis=0, keepdims=True)
                d_pre_b = d_pre.astype(BF16)
                dgw_ref[zg] = lax.dot_general(xcb, d_pre_b, TN, preferred_element_type=F32)
                acc_dxc = acc_dxc + lax.dot_general(d_pre_b, gw_ref[zg], NT, preferred_element_type=F32)

        dcb_ref[...] = jnp.sum(acc_dxc, axis=0, keepdims=True)
        for j in range(CONV_WIDTH):
            dcw_ref[pl.ds(j, 1), :] = jnp.sum(acc_dxc * xpad[pl.ds(PAD - 2 + j, t), :], axis=0, keepdims=True)
        _fill_padded(dxc, acc_dxc, t)
        dx = cw_ref[pl.ds(0, 1), :] * dxc[pl.ds(PAD + 2, t), :]
        for j in range(1, CONV_WIDTH):
            dx = dx + cw_ref[pl.ds(j, 1), :] * dxc[pl.ds(PAD + 2 - j, t), :]
        dx_ref[...] = dx.astype(BF16)
        dt_ref[0] = dx.T.astype(BF16)

    col = lambda off: pl.BlockSpec((t, CH), lambda i: (0, off + i))
    small = lambda rows: pl.BlockSpec((rows, CH), lambda i: (0, i))
    dense = pl.BlockSpec((4, None, CH, CH), lambda i: (0, i, 0, 0))
    padded = pltpu.VMEM((t + 2 * PAD, CH), F32)
    return _call(
        body, name=name, args=[proj, proj, cw, cb, gw, gb, lam, hf, hb, dya],
        out_shape=(SDS((t, c), BF16), SDS((t, c), BF16), SDS((2, c, t), BF16), SDS((CONV_WIDTH, c), F32),
                   SDS((1, c), F32), SDS((4, ncb, CH, CH), F32), SDS((4, c), F32), SDS((2, c), F32)),
        grid=(ncb,),
        in_specs=[col(0), col(ncb), small(CONV_WIDTH), small(1), dense, small(4), small(2), col(0), col(0), col(0)],
        out_specs=(col(0), col(0), pl.BlockSpec((2, CH, t), lambda i: (0, i, 0)), small(CONV_WIDTH), small(1),
                   dense, small(4), small(2)),
        scratch_shapes=[padded, padded, padded] + [pltpu.VMEM((t, CH), F32)] * 5, job=job)


Q_ROWS = 4
BAND_ROWS = WIN_ROWS + Q_ROWS
BAND_PAIRS = BAND_ROWS // 2
Q_BLOCK = Q_ROWS * GRID_W
BAND = BAND_ROWS * GRID_W
PAIR_W = 2 * GRID_W
N_BOTH = 2 * WIN_ROWS - 2
ENTRY_LEFT_OUT, ENTRY_RIGHT_OUT, ENTRY_OUT = N_BOTH, N_BOTH + 1, N_BOTH + 2
N_ENTRIES = N_BOTH + 3


def _bias_tables(rpb):
    cols = np.arange(GRID_W)
    start = np.clip(cols - WIN_COLS // 2, 0, GRID_W - WIN_COLS)
    valid = (cols[None, :] >= start[:, None]) & (cols[None, :] < start[:, None] + WIN_COLS)
    col_off = np.clip(cols[None, :] - cols[:, None] + WIN_COLS - 1, 0, 2 * WIN_COLS - 2)
    pick_col = jnp.asarray(np.eye(2 * WIN_COLS - 1, dtype=np.float32)[col_off] * valid[..., None])
    by_row = jnp.einsum("hrc,qkc->hrqk", rpb, pick_col, precision=lax.Precision.HIGHEST)
    by_row = jnp.where(jnp.asarray(valid)[None, None], by_row, NEG)
    out = jnp.full_like(by_row[:, :1], NEG)
    first_in, last_in = WIN_ROWS - 1 - WIN_ROWS // 2, 2 * (WIN_ROWS - 1) - WIN_ROWS // 2
    both = jnp.concatenate([by_row[:, :-1], by_row[:, 1:]], axis=-1)
    left_out = jnp.concatenate([out, by_row[:, first_in:first_in + 1]], axis=-1)
    right_out = jnp.concatenate([by_row[:, last_in:last_in + 1], out], axis=-1)
    return jnp.concatenate([both, left_out, right_out, jnp.concatenate([out, out], axis=-1)], axis=1)


def _band_start(m, rows):
    return jnp.clip(Q_ROWS * m - WIN_ROWS // 2, 0, rows - BAND_ROWS)


def _entry(r, key_row, rows):
    w0 = jnp.clip(r - WIN_ROWS // 2, 0, rows - WIN_ROWS)
    left = (key_row >= w0) & (key_row < w0 + WIN_ROWS)
    right = (key_row + 1 >= w0) & (key_row + 1 < w0 + WIN_ROWS)
    return jnp.where(left & right, key_row - r + WIN_ROWS - 1,
                     jnp.where(right, ENTRY_LEFT_OUT, jnp.where(left, ENTRY_RIGHT_OUT, ENTRY_OUT)))


def _transposed_pairs(dst, src_ref):
    for g in range(dst.shape[0]):
        dst[g] = src_ref[pl.ds(g * PAIR_W, PAIR_W), :].T.astype(BF16)


def _band_of(pairs_ref, first_pair, hh):
    heads = pl.ds(hh * HEAD_DIM, HEAD_DIM)
    return jnp.concatenate([pairs_ref[first_pair + g, heads, :] for g in range(BAND_PAIRS)], axis=1)


def _attn_block(qs, kt, tz_ref, hh, m, rows):
    rs = _band_start(m, rows)
    lanes = pl.ds(hh * HEAD_DIM, HEAD_DIM)
    qrows = pl.ds(pl.multiple_of(m * Q_BLOCK, Q_BLOCK), Q_BLOCK)
    band = pl.ds(pl.multiple_of(rs * GRID_W, PAIR_W), BAND)
    entries = [[_entry(Q_ROWS * m + i, rs + 2 * g, rows) for g in range(BAND_PAIRS)] for i in range(Q_ROWS)]
    bias = jnp.concatenate([jnp.concatenate([tz_ref[hh, e] for e in row], axis=1) for row in entries], axis=0)
    q = qs[qrows, lanes]
    s = jnp.dot(q, _band_of(kt, rs // 2, hh), preferred_element_type=F32) * (HEAD_DIM ** -0.5) + bias
    p = jnp.exp(s - jnp.max(s, axis=-1, keepdims=True))
    p = p / jnp.sum(p, axis=-1, keepdims=True)
    return q, p, qrows, band, lanes, entries, rs // 2


def _attn_fwd(proj, tables, width, name, job=None):
    t = proj.shape[0]
    rows = t // GRID_W
    npair = width // LANES
    first = (proj.shape[1] - 3 * width) // LANES

    def body(q_ref, k_ref, v_ref, tz_ref, o_ref, qs, vs, kt):
        qs[...] = q_ref[...].astype(BF16)
        vs[...] = v_ref[...].astype(BF16)
        _transposed_pairs(kt, k_ref)

        def block(m, carry):
            for hh in range(2):
                _, p, qrows, band, lanes, _, _ = _attn_block(qs, kt, tz_ref, hh, m, rows)
                o_ref[qrows, lanes] = jnp.dot(p.astype(BF16), vs[band, lanes], preferred_element_type=F32)
            return carry

        lax.fori_loop(0, rows // Q_ROWS, block, 0, unroll=2)

    col = lambda off: pl.BlockSpec((t, LANES), lambda i: (0, off + i))
    return _call(body, name=name, args=[proj, proj, proj, tables], out_shape=SDS((t, width), F32), grid=(npair,),
                 in_specs=[col(first), col(first + npair), col(first + 2 * npair),
                           pl.BlockSpec((2, N_ENTRIES, GRID_W, PAIR_W), lambda i: (i, 0, 0, 0))],
                 out_specs=col(0),
                 scratch_shapes=[pltpu.VMEM((t, LANES), BF16)] * 2 + [pltpu.VMEM((t // PAIR_W, LANES, PAIR_W), BF16)],
                 job=job)


def _attn_bwd(proj, tables, dyb, name, job=None):
    t, width = dyb.shape
    rows = t // GRID_W
    npair = width // LANES
    first = (proj.shape[1] - 3 * width) // LANES

    def body(q_ref, k_ref, v_ref, tz_ref, do_ref, dq_ref, dk_ref, dv_ref, dt_ref, dtz_ref, dq_s, dk_s, dv_s,
             qs, ks, vs, dos, kt, vt):
        qs[...] = q_ref[...].astype(BF16)
        ks[...] = k_ref[...].astype(BF16)
        vs[...] = v_ref[...].astype(BF16)
        dos[...] = do_ref[...].astype(BF16)
        _transposed_pairs(kt, k_ref)
        _transposed_pairs(vt, v_ref)
        dk_s[...] = jnp.zeros_like(dk_s)
        dv_s[...] = jnp.zeros_like(dv_s)
        dtz_ref[...] = jnp.zeros_like(dtz_ref)

        def block(m, carry):
            for hh in range(2):
                q, p, qrows, band, lanes, entries, first_pair = _attn_block(qs, kt, tz_ref, hh, m, rows)
                do = dos[qrows, lanes]
                dp = jnp.dot(do, _band_of(vt, first_pair, hh), preferred_element_type=F32)
                ds = p * (dp - jnp.sum(dp * p, axis=-1, keepdims=True))
                for i, row in enumerate(entries):
                    for g, e in enumerate(row):
                        dtz_ref[hh, e] += ds[i * GRID_W:(i + 1) * GRID_W, g * PAIR_W:(g + 1) * PAIR_W]
                dsb = (ds * (HEAD_DIM ** -0.5)).astype(BF16)
                dq_s[qrows, lanes] = jnp.dot(dsb, ks[band, lanes], preferred_element_type=F32)
                dk_s[band, lanes] += lax.dot_general(dsb, q, TN, preferred_element_type=F32)
                dv_s[band, lanes] += lax.dot_general(p.astype(BF16), do, TN, preferred_element_type=F32)
            return carry

        lax.fori_loop(0, rows // Q_ROWS, block, 0)
        for n, (src, dst) in enumerate(((dq_s, dq_ref), (dk_s, dk_ref), (dv_s, dv_ref))):
            val = src[...]
            dst[...] = val.astype(BF16)
            dt_ref[n] = val.T.astype(BF16)

    col = lambda off: pl.BlockSpec((t, LANES), lambda i: (0, off + i))
    table = pl.BlockSpec((2, N_ENTRIES, GRID_W, PAIR_W), lambda i: (i, 0, 0, 0))
    pairs = pltpu.VMEM((t // PAIR_W, LANES, PAIR_W), BF16)
    return _call(body, name=name, args=[proj, proj, proj, tables, dyb],
                 out_shape=(SDS((t, width), BF16),) * 3 + (SDS((3, width, t), BF16), SDS(tables.shape, F32)),
                 grid=(npair,),
                 in_specs=[col(first), col(first + npair), col(first + 2 * npair), table, col(0)],
                 out_specs=(col(0), col(0), col(0), pl.BlockSpec((3, LANES, t), lambda i: (0, i, 0)), table),
                 scratch_shapes=[pltpu.VMEM((t, LANES), F32)] * 3 + [pltpu.VMEM((t, LANES), BF16)] * 4 + [pairs, pairs],
                 job=job)


def _adamw_math(w, g, m, v):
    m = ADAM_B1 * m + (1.0 - ADAM_B1) * g
    v = ADAM_B2 * v + (1.0 - ADAM_B2) * (g * g)
    m_hat = m / (1.0 - ADAM_B1 ** ADAM_STEP)
    v_hat = v / (1.0 - ADAM_B2 ** ADAM_STEP)
    delta = -ADAM_LR * (m_hat / (jnp.sqrt(v_hat) + ADAM_EPS) + ADAM_WD * w)
    return delta, m, v


def _sum_partials(p_ref):
    g = p_ref[0].astype(F32)
    for s in range(1, N_CHIP):
        g = g + p_ref[s].astype(F32)
    return g


def _adamw_rows(w, partials, m, v, name, after=()):
    rb, n = w.shape
    tr = 64

    def body(w_ref, p_ref, m_ref, v_ref, *rest):
        g_ref, d_ref, nm_ref, nv_ref = rest[len(after):]
        g = _sum_partials(p_ref)
        g_ref[...] = g
        d_ref[...], nm_ref[...], nv_ref[...] = _adamw_math(w_ref[...], g, m_ref[...], v_ref[...])

    blk = pl.BlockSpec((tr, n), lambda i: (i, 0))
    return _call(body, name=name, args=[w, partials.reshape(N_CHIP, rb, n), m, v, *after],
                 out_shape=(SDS((rb, n), F32),) * 4, grid=(rb // tr,),
                 in_specs=[blk, pl.BlockSpec((N_CHIP, tr, n), lambda i: (0, i, 0)), blk, blk] + [ANY] * len(after),
                 out_specs=(blk,) * 4)


def _adamw_cols(w, partials, m, v, name, after=()):
    d, nb = w.shape
    td = 256
    parts = list(partials) if isinstance(partials, (list, tuple)) else [partials]
    heights = [p.shape[0] // N_CHIP for p in parts]

    def body(w_ref, m_ref, v_ref, *rest):
        p_refs, (g_ref, d_ref, nm_ref, nv_ref) = rest[:len(parts)], rest[len(parts) + len(after):]
        g = jnp.concatenate([_sum_partials(p_ref) for p_ref in p_refs], axis=0).T
        g_ref[...] = g
        d_ref[...], nm_ref[...], nv_ref[...] = _adamw_math(w_ref[...], g, m_ref[...], v_ref[...])

    blk = pl.BlockSpec((td, nb), lambda i: (i, 0))
    return _call(body, name=name, args=[w, m, v, *[p.reshape(N_CHIP, h, d) for p, h in zip(parts, heights)], *after],
                 out_shape=(SDS((d, nb), F32),) * 4, grid=(d // td,),
                 in_specs=[blk, blk, blk] + [pl.BlockSpec((N_CHIP, h, td), lambda i: (0, 0, i)) for h in heights]
                 + [ANY] * len(after), out_specs=(blk,) * 4)


def _adamw_small(w, g, m, v, name):
    def body(w_ref, g_ref, m_ref, v_ref, d_ref, nm_ref, nv_ref):
        d_ref[...], nm_ref[...], nv_ref[...] = _adamw_math(w_ref[...], g_ref[...], m_ref[...], v_ref[...])

    return _call(body, name=name, args=[w, g, m, v], out_shape=(SDS(w.shape, F32),) * 3, in_specs=[WHOLE] * 4,
                 out_specs=(WHOLE,) * 3)


TILE = SUBLANES * LANES


def _pack(arrays):
    parts = []
    for a in arrays:
        flat = a.reshape(-1).astype(F32)
        flat = jnp.pad(flat, (0, -flat.size % TILE))
        parts.append(flat.reshape(-1, LANES))
    return jnp.concatenate(parts, axis=0)


def _unpack(pack, like):
    out, row = [], 0
    for a in like:
        n = int(np.prod(a.shape))
        nrows = -(-n // TILE) * SUBLANES
        out.append(pack[row:row + nrows].reshape(-1)[:n].reshape(a.shape))
        row += nrows
    return out


def _dense_gate_blocks(gate_w):
    w = gate_w.reshape(4, -1, 2, HEAD_DIM, HEAD_DIM)
    zero = jnp.zeros_like(w[:, :, 0])
    top = jnp.concatenate([w[:, :, 0], zero], axis=-1)
    bottom = jnp.concatenate([zero, w[:, :, 1]], axis=-1)
    return jnp.concatenate([top, bottom], axis=-2)


def _diag_gate_blocks(dense, shape):
    even = dense[:, :, :HEAD_DIM, :HEAD_DIM]
    odd = dense[:, :, HEAD_DIM:, HEAD_DIM:]
    return jnp.stack([even, odd], axis=2).reshape(shape)


LARGE = ("ffn1_w_in", "ffn1_w_out", "w_in_mix", "w_out_mix", "ffn2_w_in", "ffn2_w_out")
COLUMN_SHARDED = ("ffn1_w_in", "w_in_mix", "ffn2_w_in")
SHARDED_SMALL = ("lru_conv_w", "lru_lambda")
REPLICATED = ("norm_ffn1", "norm_mix", "lru_conv_b", "lru_gate_w", "lru_gate_b", "attn_rpb", "lru_out_norm",
              "attn_out_norm", "norm_ffn2", "norm_final")
SMALL_ORDER = REPLICATED + SHARDED_SMALL
WEIGHTS = ("norm_ffn1", "ffn1_w_in", "ffn1_w_out", "norm_mix", "w_in_mix", "lru_conv_w", "lru_conv_b", "lru_gate_w",
           "lru_gate_b", "lru_lambda", "attn_rpb", "lru_out_norm", "attn_out_norm", "w_out_mix", "norm_ffn2",
           "ffn2_w_in", "ffn2_w_out", "norm_final")


PARTS = {("gather", "w_in_mix"): 4, ("gather", "ffn2_w_in"): 8}
CARRIES = {
    "gather_ffn1_in": [(("gather", "ffn1_w_in"), 1), (("gather", "small"), 1)],
    "ffn1_hidden": [(("gather", "ffn1_w_out"), 1), (("gather", "w_in_mix"), 1)],
    "ffn1_out": [(("gather", "w_in_mix"), 3)],
    "mix_in_proj": [(("gather", "w_out_mix"), 1), (("gather", "ffn2_w_in"), 1)],
    "lru_fwd": [(("gather", "ffn2_w_in"), 3)],
    "attn_fwd": [(("gather", "ffn2_w_in"), 3)],
    "mix_out_proj": [(("gather", "ffn2_w_in"), 1)],
    "ffn2_hidden": [(("gather", "ffn2_w_out"), 1)],
    "ffn1_bwd": [(("gather", "small_grads"), 1)],
    "gather_late_grads": [(("gather", "late_grads"), 1)],
}


class _Transfer:
    def __init__(self, kind, src, dest, block_rows, parts):
        self.kind, self.src, self.dest = kind, src, dest
        self.ranges, self.taken = _split(block_rows, parts), 0

    def take(self, count):
        lo, hi = self.ranges[self.taken][0], self.ranges[self.taken + count - 1][1]
        self.taken += count
        return _Piece(self.kind, self.src, self.dest, lo, hi)


class _Traffic:
    def __init__(self):
        self.transfers = {}

    def open(self, kind, name, src):
        if kind == "gather":
            dest, rows = _gathered(src), src.shape[0]
        elif kind == "to_sibling":
            dest, rows = SDS((src.shape[0] // 2, src.shape[1]), src.dtype), src.shape[0] // N_DEV
        else:
            dest, rows = SDS(src.shape, src.dtype), src.shape[0] // N_CHIP
        self.transfers[kind, name] = _Transfer(kind, src, dest, rows, PARTS.get((kind, name), 1))

    def _job(self, host):
        moved = [self.transfers[key] for key, _ in CARRIES[host]]
        return moved, _Job([tr.take(count) for tr, (_, count) in zip(moved, CARRIES[host])])

    def carry(self, host, fn, *args, **kw):
        if host not in CARRIES:
            return fn(*args, name=host, **kw)
        moved, job = self._job(host)
        res, landed = fn(*args, name=host, job=job, **kw)
        for tr, arr in zip(moved, landed):
            tr.dest = arr
        return res

    def alone(self, host):
        moved, job = self._job(host)
        for tr, arr in zip(moved, _run_job(job, host)):
            tr.dest = arr

    def result(self, kind, name):
        tr = self.transfers.pop((kind, name))
        assert tr.taken == len(tr.ranges), (kind, name)
        return tr.dest


def _forward_backward(x, target, shards, sharded_small, s):
    c = s["lru_conv_b"].shape[1]
    width = s["attn_out_norm"].shape[1]
    t = x.shape[0]
    traffic = _Traffic()
    carry = traffic.carry
    weight = lambda n: traffic.result("gather", n)

    for n in LARGE:
        traffic.open("gather", n, shards[n])
    traffic.open("gather", "small", sharded_small)
    traffic.alone("gather_ffn1_in")
    full_small = weight("small").reshape(N_DEV, SUBLANES, c // N_DEV)
    conv_w = full_small[:, :CONV_WIDTH].transpose(1, 0, 2).reshape(CONV_WIDTH, c)
    lam = full_small[:, CONV_WIDTH:CONV_WIDTH + 2].transpose(1, 0, 2).reshape(2, c)
    w = {"ffn1_w_in": weight("ffn1_w_in")}
    ffn_out = dict(nt=False, out_dtype=F32, tm=512, tn=512, scale=0.5)
    u1 = _rmsnorm_fwd(x, s["norm_ffn1"], "norm_ffn1")
    g1, up1, hid1, hid1_t = carry("ffn1_hidden", _ffn_hidden, u1, w["ffn1_w_in"])
    w["ffn1_w_out"] = weight("ffn1_w_out")
    h1 = carry("ffn1_out", _mm, hid1, w["ffn1_w_out"], residual=x, **ffn_out)
    w["w_in_mix"] = weight("w_in_mix")
    u2 = _rmsnorm_fwd(h1, s["norm_mix"], "norm_mix")
    proj = carry("mix_in_proj", _mm, u2, w["w_in_mix"], nt=True, out_dtype=F32, tm=512, tn=512)
    w["w_out_mix"] = weight("w_out_mix")
    gw = _dense_gate_blocks(s["lru_gate_w"]).astype(BF16)
    gb = s["lru_gate_b"].reshape(4, c)
    tables, tables_vjp = jax.vjp(_bias_tables, s["attn_rpb"])
    ya, hf, hb = carry("lru_fwd", _lru_fwd, proj, conv_w, s["lru_conv_b"], gw, gb, lam)
    yb = carry("attn_fwd", _attn_fwd, proj, tables, width)
    y, yt = _mixnorm_fwd(ya, yb, s["lru_out_norm"], s["attn_out_norm"], "mix_norm")
    h2 = carry("mix_out_proj", _mm, y, w["w_out_mix"], nt=False, out_dtype=F32, tm=512, tn=512, residual=h1)
    u3 = _rmsnorm_fwd(h2, s["norm_ffn2"], "norm_ffn2")
    w["ffn2_w_in"] = weight("ffn2_w_in")
    g2, up2, hid2, hid2_t = carry("ffn2_hidden", _ffn_hidden, u3, w["ffn2_w_in"])
    w["ffn2_w_out"] = weight("ffn2_w_out")
    h3 = carry("ffn2_out", _mm, hid2, w["ffn2_w_out"], residual=h2, **ffn_out)
    dh3, df2, loss_part, d_norm_final = _final_loss(h3, s["norm_final"], target, "final_loss")

    grads = {}
    grad_of = dict(nt=False, out_dtype=BF16, tm=512, tn=1024)

    to_sibling, to_chips = {}, {}

    def reduce_in_chip(n):
        land = _blank_like(grads[n], grads[n].shape[0] // 2, "landing_" + n)
        to_sibling[n], token = _split_start("to_sibling", grads[n], land, "to_sibling_" + n)
        RUN_AFTER.append(token)

    def reduce_over_chips(n, after):
        own, got = _split_wait(to_sibling.pop(n), [after], "from_sibling_" + n)
        summed = _pair_sum(own, got, "pair_sum_" + n)
        to_chips[n], token = _split_start("to_chips", summed, _own_slot(summed, "own_slot_" + n), "to_chips_" + n)
        RUN_AFTER.append(token)
        return token

    f = hid2_t.shape[0]
    grads["ffn2_w_out"] = carry("ffn2_out_grad", _mm, hid2_t, df2, **grad_of)
    reduce_in_chip("ffn2_w_out")
    du3, da2_t = carry("ffn2_bwd", _ffn_bwd, df2, g2, up2, w["ffn2_w_in"], w["ffn2_w_out"])
    reduce_over_chips("ffn2_w_out", du3)
    grads["ffn2_w_in"] = carry("ffn2_in_grad", _mm, da2_t.reshape(2 * f, t), u3, **grad_of)
    reduce_in_chip("ffn2_w_in")
    dh2, dh2b, d_norm_ffn2 = carry("norm_ffn2_bwd", _rmsnorm_bwd, du3, h2, s["norm_ffn2"], dh3, 1.0)
    grads["w_out_mix"] = carry("mix_out_grad", _mm, yt, dh2b, **grad_of)
    reduce_over_chips("ffn2_w_in", grads["w_out_mix"])
    reduce_in_chip("w_out_mix")
    dy = carry("mix_out_bwd", _mm, dh2b, w["w_out_mix"], nt=True, out_dtype=F32, tm=512, tn=512)
    dya, dyb, d_lru_out_norm, d_attn_out_norm = _mixnorm_bwd(dy, ya, yb, s["lru_out_norm"], s["attn_out_norm"],
                                                             "mix_norm_bwd")
    dq, dk, dv, dqkv_t, d_tables = carry("attn_bwd", _attn_bwd, proj, tables, dyb)
    reduce_over_chips("w_out_mix", dq)
    dx_lru, dg_lru, dxg_t, d_conv_w, d_conv_b, d_gw, d_gb, d_lam = carry(
        "lru_bwd", _lru_bwd, proj, conv_w, s["lru_conv_b"], gw, gb, lam, hf, hb, dya)
    rows_of = 2 * c + 3 * width
    lru_rows = carry("mix_in_grad_lru", _mm, dxg_t.reshape(2 * c, t), u2, out_rows=rows_of, **grad_of)
    grads["w_in_mix"] = carry("mix_in_grad_attn", _mm, dqkv_t.reshape(3 * width, t), u2, out_rows=rows_of,
                              row_offset=2 * c, into=lru_rows, **grad_of)
    reduce_in_chip("w_in_mix")
    du2 = carry("mix_in_bwd", _mm, [dx_lru, dg_lru, dq, dk, dv], w["w_in_mix"], nt=False, out_dtype=F32, tm=512,
                tn=512)
    dh1, df1, d_norm_mix = carry("norm_mix_bwd", _rmsnorm_bwd, du2, h1, s["norm_mix"], dh2, 0.5)
    reduce_over_chips("w_in_mix", dh1)

    by_device = lambda a: a.reshape(a.shape[0], N_DEV, -1).transpose(1, 0, 2)
    small = {
        "norm_mix": d_norm_mix, "lru_conv_b": d_conv_b, "lru_gate_w": _diag_gate_blocks(d_gw, s["lru_gate_w"].shape),
        "lru_gate_b": d_gb.reshape(s["lru_gate_b"].shape), "attn_rpb": tables_vjp(d_tables)[0],
        "lru_out_norm": d_lru_out_norm, "attn_out_norm": d_attn_out_norm, "norm_ffn2": d_norm_ffn2,
        "norm_final": d_norm_final, "lru_conv_w": by_device(d_conv_w), "lru_lambda": by_device(d_lam),
    }
    early = [small[n] for n in SMALL_ORDER[1:]]
    traffic.open("gather", "small_grads", _pack(early))

    grads["ffn1_w_out"] = carry("ffn1_out_grad", _mm, hid1_t, df1, **grad_of)
    reduce_in_chip("ffn1_w_out")
    du1, da1_t = carry("ffn1_bwd", _ffn_bwd, df1, g1, up1, w["ffn1_w_in"], w["ffn1_w_out"])
    reduce_over_chips("ffn1_w_out", du1)
    tile = 2 * f // N_DEV // 4
    half_rows = lambda h: (tile, 2 * N_DEV, lambda i: (i // 2) * 4 + 2 * h + i % 2)
    grads["ffn1_w_in_a"] = carry("ffn1_in_grad_a", _mm, da1_t.reshape(2 * f, t), u1, take=half_rows(0), **grad_of)
    reduce_in_chip("ffn1_w_in_a")
    grads["ffn1_w_in_b"] = carry("ffn1_in_grad_b", _mm, da1_t.reshape(2 * f, t), u1, take=half_rows(1), **grad_of)
    reduce_over_chips("ffn1_w_in_a", grads["ffn1_w_in_b"])
    reduce_in_chip("ffn1_w_in_b")
    grad_x, _, d_norm_ffn1 = carry("norm_ffn1_bwd", _rmsnorm_bwd, du1, x, s["norm_ffn1"], dh1, 1.0)
    traffic.open("gather", "late_grads", _pack([d_norm_ffn1]))
    traffic.alone("gather_late_grads")
    reduced = (_unpack(_sum_devices(traffic.result("gather", "late_grads"), "sum_late_grads"), [d_norm_ffn1])
               + _unpack(_sum_devices(traffic.result("gather", "small_grads"), "sum_small_grads"), early))
    last_token = reduce_over_chips("ffn1_w_in_b", reduced[0])
    RUN_AFTER.clear()
    assert not traffic.transfers and not to_sibling, (list(traffic.transfers), list(to_sibling))
    return loss_part[0, 0], grad_x, to_chips, last_token, dict(zip(SMALL_ORDER, reduced))


def _step(x, loss_target, p, m, v):
    me = 4 * lax.axis_index("x") + 2 * lax.axis_index("y") + lax.axis_index("c")

    shards = {n: (_cast_transposed if n in COLUMN_SHARDED else _cast_rows)(p[n], "cast_" + n) for n in LARGE}
    sharded_small = (jnp.pad(p["lru_conv_w"], ((0, SUBLANES - CONV_WIDTH), (0, 0)))
                     + jnp.pad(p["lru_lambda"], ((CONV_WIDTH, SUBLANES - CONV_WIDTH - 2), (0, 0))))
    s = {n: p[n] if n in ("lru_gate_w", "lru_gate_b", "attn_rpb") else p[n].reshape(1, -1) for n in REPLICATED}

    loss_part, grad_x, to_chips, last_token, small = _forward_backward(x, loss_target, shards, sharded_small, s)
    loss = lax.psum(loss_part, ("x", "y", "c"))

    def landed(n, after):
        return _split_wait(to_chips[n], after, "from_chips_" + n)[1]

    def update(n, partials):
        return (_adamw_cols if n in COLUMN_SHARDED else _adamw_rows)(p[n], partials, m[n], v[n], "adamw_" + n)

    out = {n: update(n, landed(n, [last_token])) for n in LARGE if n != "ffn1_w_in"}
    done = [o[3] for o in out.values()]
    out["ffn1_w_in"] = update("ffn1_w_in", [landed("ffn1_w_in_a", done), landed("ffn1_w_in_b", done)])

    g_small = {n: lax.dynamic_index_in_dim(g, me, axis=0, keepdims=False) if n in SHARDED_SMALL else g
               for n, g in small.items()}
    names = SMALL_ORDER
    like = [p[n] for n in names]
    pack_of = lambda d: _pack([d[n].reshape(p[n].shape) for n in names])
    upd = _adamw_small(pack_of(p), pack_of(g_small), pack_of(m), pack_of(v), "adamw_small")
    for n, d_, m_, v_ in zip(names, *[_unpack(u, like) for u in upd]):
        out[n] = (g_small[n].reshape(p[n].shape), d_, m_, v_)
    return loss, grad_x, out


def kernel(x, norm_ffn1, ffn1_w_in, ffn1_w_out, norm_mix, w_in_mix, lru_conv_w, lru_conv_b, lru_gate_w, lru_gate_b, lru_lambda, attn_rpb, lru_out_norm, attn_out_norm, w_out_mix, norm_ffn2, ffn2_w_in, ffn2_w_out, norm_final, loss_target, m_norm_ffn1, m_ffn1_w_in, m_ffn1_w_out, m_norm_mix, m_w_in_mix, m_lru_conv_w, m_lru_conv_b, m_lru_gate_w, m_lru_gate_b, m_lru_lambda, m_attn_rpb, m_lru_out_norm, m_attn_out_norm, m_w_out_mix, m_norm_ffn2, m_ffn2_w_in, m_ffn2_w_out, m_norm_final, v_norm_ffn1, v_ffn1_w_in, v_ffn1_w_out, v_norm_mix, v_w_in_mix, v_lru_conv_w, v_lru_conv_b, v_lru_gate_w, v_lru_gate_b, v_lru_lambda, v_attn_rpb, v_lru_out_norm, v_attn_out_norm, v_w_out_mix, v_norm_ffn2, v_ffn2_w_in, v_ffn2_w_out, v_norm_final):
    given = dict(locals())
    drop_layer = lambda n, a: a if n == "norm_final" else a[0]
    p = {n: drop_layer(n, given[n]) for n in WEIGHTS}
    m = {n: drop_layer(n, given["m_" + n]) for n in WEIGHTS}
    v = {n: drop_layer(n, given["v_" + n]) for n in WEIGHTS}
    loss, grad_x, out = _step(x[0], loss_target[0], p, m, v)
    shaped = lambda n, a: a.reshape(given[n].shape)
    return (loss, grad_x[None], *[shaped(n, out[n][k]) for k in range(4) for n in WEIGHTS])
```

```python
import math

import numpy as np
import jax
import jax.numpy as jnp
from jax import lax
from jax.experimental import pallas as pl
from jax.experimental.pallas import tpu as pltpu

F32 = jnp.float32
BF16 = jnp.bfloat16
SDS = jax.ShapeDtypeStruct

N_DEV = 8
N_CHIP = 4
NORM_EPS = 1e-6
RG_C = 8.0
CONV_WIDTH = 4
HEAD_DIM = 64
GRID_W = 64
WIN_ROWS = 8
WIN_COLS = 16
NEG = -1e30

ADAM_LR = 0.001
ADAM_B1 = 0.9
ADAM_B2 = 0.999
ADAM_EPS = 1e-08
ADAM_WD = 0.01
ADAM_STEP = 10

LANES = 128
SUBLANES = 8
VMEM_LIMIT = 56 * 1024 * 1024

NT = (((1,), (1,)), ((), ()))
TN = (((0,), (0,)), ((), ()))
ANY = pl.BlockSpec(memory_space=pl.ANY)
WHOLE = pl.BlockSpec(memory_space=pltpu.VMEM)
MESH = pl.DeviceIdType.MESH


def _sigmoid(x):
    return 1.0 / (1.0 + jnp.exp(-x))


def _gelu_parts(x):
    c = math.sqrt(2.0 / math.pi)
    t = jnp.tanh(c * (x + 0.044715 * (x * x * x)))
    gelu = 0.5 * x * (1.0 + t)
    dgelu = 0.5 * (1.0 + t) + 0.5 * x * (1.0 - t * t) * (c * (1.0 + 3.0 * 0.044715 * (x * x)))
    return gelu, dgelu


def _expm1(x):
    poly = x * (1.0 + x * (1.0 / 2) * (1.0 + x * (1.0 / 3) * (1.0 + x * (1.0 / 4) * (1.0 + x * (1.0 / 5) * (1.0 + x * (1.0 / 6))))))
    return jnp.where(jnp.abs(x) < 0.25, poly, jnp.exp(x) - 1.0)


def _softplus(x):
    return jnp.maximum(x, 0.0) + jnp.log1p(jnp.exp(-jnp.abs(x)))


class _Piece:
    N_REMOTE = {"gather": 7, "to_sibling": N_CHIP, "to_chips": 3}
    N_LOCAL = {"gather": 1, "to_sibling": 0, "to_chips": 1}

    def __init__(self, kind, src, dest, lo, hi):
        self.kind, self.src, self.dest, self.lo, self.hi = kind, src, dest, lo, hi


RELAY_AT = 60
RUN_AFTER = []


class _Job:
    def __init__(self, pieces):
        self.pieces = list(pieces)
        self.ins = [p.src for p in self.pieces]
        self.out_shapes = [SDS(p.dest.shape, p.dest.dtype) for p in self.pieces]
        self.aliased = [i for i, p in enumerate(self.pieces) if not isinstance(p.dest, SDS)]
        self.n_remote = sum(_Piece.N_REMOTE[p.kind] for p in self.pieces)
        self.n_local = max(sum(_Piece.N_LOCAL[p.kind] for p in self.pieces), 1)

    def _each(self, step, ins, outs, send_sems, recv_sems, local_sems):
        remote = local = 0
        for p, src, dst in zip(self.pieces, ins, outs):
            _EXCHANGES[p.kind](step, p, src, dst, send_sems, recv_sems, local_sems, remote, local)
            remote += _Piece.N_REMOTE[p.kind]
            local += _Piece.N_LOCAL[p.kind]

    def start(self, *refs):
        self._each("start", *refs)

    def relay(self, *refs):
        self._each("relay", *refs)

    def finish(self, *refs):
        self._each("finish", *refs)


def _call(body, *, name, args, out_shape, in_specs, out_specs, grid=(), scratch_shapes=(), aliases=None, job=None):
    single = not isinstance(out_shape, (tuple, list))
    out_shape = (out_shape,) if single else tuple(out_shape)
    out_specs = (out_specs,) if single else tuple(out_specs)
    aliases = dict(aliases or {})
    if RUN_AFTER:
        tokens, n_plain, plain_body = list(RUN_AFTER), len(args), body
        RUN_AFTER.clear()
        body = lambda *refs: plain_body(*refs[:n_plain], *refs[n_plain + len(tokens):])
        args, in_specs = list(args) + tokens, list(in_specs) + [ANY] * len(tokens)
    params = pltpu.CompilerParams(dimension_semantics=("arbitrary",) * len(grid) if grid else None,
                                  vmem_limit_bytes=VMEM_LIMIT)
    if job is None:
        res = pl.pallas_call(body, out_shape=out_shape, grid=grid, in_specs=list(in_specs), out_specs=out_specs,
                             scratch_shapes=list(scratch_shapes), input_output_aliases=aliases, name=name,
                             compiler_params=params)(*args)
        return res[0] if single else res

    n_in, n_out, n_scr = len(args), len(out_shape), len(scratch_shapes)
    j_in, j_out, j_alias = len(job.ins), len(job.out_shapes), len(job.aliased)

    def hosted(*refs):
        ins, refs = refs[:n_in], refs[n_in:]
        j_ins, refs = refs[:j_in], refs[j_in + j_alias:]
        outs, refs = refs[:n_out], refs[n_out:]
        j_outs, refs = refs[:j_out], refs[j_out:]
        scr, sems = refs[:n_scr], refs[n_scr:]
        if grid:
            step = 0
            for axis, size in enumerate(grid):
                step = step * size + pl.program_id(axis)
            steps = math.prod(grid)
            pl.when(step == 0)(lambda: job.start(j_ins, j_outs, *sems))
            body(*ins, *outs, *scr)
            pl.when(step == min(RELAY_AT * steps // 100, steps - 1))(lambda: job.relay(j_ins, j_outs, *sems))
            pl.when(step == steps - 1)(lambda: job.finish(j_ins, j_outs, *sems))
        else:
            job.start(j_ins, j_outs, *sems)
            body(*ins, *outs, *scr)
            job.relay(j_ins, j_outs, *sems)
            job.finish(j_ins, j_outs, *sems)

    res = pl.pallas_call(
        hosted, out_shape=out_shape + tuple(job.out_shapes), grid=grid,
        in_specs=list(in_specs) + [ANY] * (j_in + j_alias), out_specs=out_specs + (ANY,) * j_out,
        scratch_shapes=list(scratch_shapes) + [pltpu.SemaphoreType.DMA((job.n_remote,)),
                                               pltpu.SemaphoreType.DMA((job.n_remote,)),
                                               pltpu.SemaphoreType.DMA((job.n_local,))],
        input_output_aliases={**aliases, **{n_in + j_in + k: n_out + i for k, i in enumerate(job.aliased)}},
        name=name, compiler_params=params)(*args, *job.ins, *[job.pieces[i].dest for i in job.aliased])
    own, carried = res[:n_out], res[n_out:]
    return (own[0] if single else own), carried


def _run_job(job, name):
    return _call(lambda: None, name=name, args=[], out_shape=(), in_specs=[], out_specs=(), job=job)[1]


def _position():
    return lax.axis_index("x"), lax.axis_index("y"), lax.axis_index("c")


def _flat(px, py, pc):
    return 4 * px + 2 * py + pc


def _gather_exchange(step, p, src, dst, send_sems, recv_sems, local_sems, r0, l0):
    x, y, c = _position()
    me, sibling = (x, y, c), (x, y, 1 - c)
    along_x, along_y, diagonal = (1 - x, y), (x, 1 - y), (1 - x, 1 - y)
    south = c == 0
    passed_on = (jnp.where(south, 1 - x, x), jnp.where(south, y, 1 - y))
    passed_to = (jnp.where(south, x, 1 - x), jnp.where(south, 1 - y, y))
    rb, n_rows = p.src.shape[0], p.hi - p.lo
    mine = src.at[pl.ds(p.lo, n_rows), :]

    def rows(block):
        return dst.at[pl.ds(_flat(*block) * rb + p.lo, n_rows), :]

    def copy(k, block, to, own=False):
        return pltpu.make_async_remote_copy(
            src_ref=mine if own else rows(block), dst_ref=rows(block),
            send_sem=send_sems.at[r0 + k], recv_sem=recv_sems.at[r0 + k], device_id=to, device_id_type=MESH)

    local = pltpu.make_async_copy(mine, rows(me), local_sems.at[l0])
    if step == "start":
        local.start()
        copy(0, me, sibling, own=True).start()
        copy(1, me, (*along_x, c), own=True).start()
        copy(2, me, (*along_y, c), own=True).start()
    elif step == "relay":
        copy(1, (*along_x, c), me).wait_recv()
        copy(2, (*along_y, c), me).wait_recv()
        copy(3, (*passed_on, c), (*passed_to, c)).start()
        copy(4, (*along_x, c), sibling).start()
        copy(5, (*along_y, c), sibling).start()
    else:
        copy(3, (*diagonal, c), me).wait_recv()
        copy(6, (*diagonal, c), sibling).start()
        copy(0, sibling, me).wait_recv()
        copy(4, (*along_x, 1 - c), me).wait_recv()
        copy(5, (*along_y, 1 - c), me).wait_recv()
        copy(6, (*diagonal, 1 - c), me).wait_recv()
        copy(0, me, sibling, own=True).wait_send()
        copy(1, me, (*along_x, c), own=True).wait_send()
        copy(2, me, (*along_y, c), own=True).wait_send()
        copy(3, (*passed_on, c), (*passed_to, c)).wait_send()
        copy(4, (*along_x, c), sibling).wait_send()
        copy(5, (*along_y, c), sibling).wait_send()
        copy(6, (*diagonal, c), sibling).wait_send()
        local.wait()


def _sibling_exchange(step, p, src, dst, send_sems, recv_sems, local_sems, r0, l0):
    x, y, c = _position()
    rb, n_rows = p.src.shape[0] // N_DEV, p.hi - p.lo
    for q in range(N_CHIP):
        copy = pltpu.make_async_remote_copy(
            src_ref=src.at[pl.ds((2 * q + 1 - c) * rb + p.lo, n_rows), :],
            dst_ref=dst.at[pl.ds(q * rb + p.lo, n_rows), :],
            send_sem=send_sems.at[r0 + q], recv_sem=recv_sems.at[r0 + q], device_id=(x, y, 1 - c), device_id_type=MESH)
        if step == "start":
            copy.start()
        elif step == "finish":
            copy.wait()


CHIP_FLIPS = [(1, 0), (0, 1), (1, 1)]


def _chips_exchange(step, p, src, dst, send_sems, recv_sems, local_sems, r0, l0):
    x, y, c = _position()
    rb, n_rows = p.src.shape[0] // N_CHIP, p.hi - p.lo

    def slot(ref, px, py):
        return ref.at[pl.ds((2 * px + py) * rb + p.lo, n_rows), :]

    def copy(k, landing=False):
        px = 1 - x if CHIP_FLIPS[k][0] else x
        py = 1 - y if CHIP_FLIPS[k][1] else y
        return pltpu.make_async_remote_copy(
            src_ref=slot(dst, px, py) if landing else slot(src, px, py),
            dst_ref=slot(dst, px, py) if landing else slot(dst, x, y),
            send_sem=send_sems.at[r0 + k], recv_sem=recv_sems.at[r0 + k], device_id=(px, py, c), device_id_type=MESH)

    local = pltpu.make_async_copy(slot(src, x, y), slot(dst, x, y), local_sems.at[l0])
    if step == "start":
        local.start()
        for k in range(3):
            copy(k).start()
    elif step == "finish":
        for k in range(3):
            copy(k, landing=True).wait_recv()
        for k in range(3):
            copy(k).wait_send()
        local.wait()


_EXCHANGES = {"gather": _gather_exchange, "to_sibling": _sibling_exchange, "to_chips": _chips_exchange}


def _gathered(shard):
    return SDS((N_DEV * shard.shape[0], shard.shape[1]), shard.dtype)


def _split(rows, parts):
    cuts = [rows * k // parts // 16 * 16 for k in range(parts)] + [rows]
    return list(zip(cuts[:-1], cuts[1:]))


def _pair_sum(g, from_sibling, name):
    rb, n = g.shape[0] // N_DEV, g.shape[1]
    tr = rb if rb * n * 2 <= 3 * 1024 * 1024 else rb // 2
    core = lax.axis_index("c").astype(jnp.int32).reshape(1)

    def body(c_ref, g_ref, r_ref, o_ref):
        o_ref[...] = (g_ref[...].astype(F32) + r_ref[...].astype(F32)).astype(BF16)

    grid_spec = pltpu.PrefetchScalarGridSpec(
        num_scalar_prefetch=1, grid=(N_CHIP, rb // tr),
        in_specs=[pl.BlockSpec((None, None, tr, n), lambda q, i, c_ref: (q, c_ref[0], i, 0)),
                  pl.BlockSpec((None, tr, n), lambda q, i, c_ref: (q, i, 0))],
        out_specs=pl.BlockSpec((None, tr, n), lambda q, i, c_ref: (q, i, 0)))
    out = pl.pallas_call(
        body, grid_spec=grid_spec, out_shape=SDS((N_CHIP, rb, n), BF16), name=name,
        compiler_params=pltpu.CompilerParams(dimension_semantics=("arbitrary",) * 2, vmem_limit_bytes=VMEM_LIMIT))(
            core, g.reshape(N_CHIP, 2, rb, n), from_sibling.reshape(N_CHIP, rb, n))
    return out.reshape(N_CHIP * rb, n)


SEM = pl.BlockSpec(memory_space=pltpu.SEMAPHORE)
IN_HBM = pl.BlockSpec(memory_space=pltpu.HBM)
SIDE_EFFECT = pltpu.SideEffectType.DATAFLOW_SIDE_EFFECTING


def _own_slot(partials, name):
    rb, n = partials.shape[0] // N_CHIP, partials.shape[1]
    tr = rb // 2
    chip = (2 * lax.axis_index("x") + lax.axis_index("y")).astype(jnp.int32).reshape(1)

    def body(chip_ref, src_ref, dst_ref):
        dst_ref[...] = src_ref[...]

    block = pl.BlockSpec((None, tr, n), lambda i, chip_ref: (chip_ref[0], i, 0))
    grid_spec = pltpu.PrefetchScalarGridSpec(num_scalar_prefetch=1, grid=(rb // tr,), in_specs=[block], out_specs=block)
    out = pl.pallas_call(
        body, grid_spec=grid_spec, out_shape=SDS((N_CHIP, rb, n), partials.dtype), name=name,
        compiler_params=pltpu.CompilerParams(dimension_semantics=("arbitrary",), vmem_limit_bytes=VMEM_LIMIT))(
            chip, partials.reshape(N_CHIP, rb, n))
    return out.reshape(partials.shape)


def _blank_like(src, rows, name):
    return pl.pallas_call(lambda src_ref, out_ref: None, out_shape=SDS((rows, src.shape[1]), src.dtype),
                          in_specs=[ANY], out_specs=ANY, name=name)(src)


def _chip_copies(src_ref, land_ref, sems):
    x, y, c = _position()
    rb = src_ref.shape[0] // N_CHIP
    copies = []
    for k, (fx, fy) in enumerate(CHIP_FLIPS):
        px, py = (1 - x if fx else x), (1 - y if fy else y)
        copies.append(pltpu.make_async_remote_copy(
            src_ref=src_ref.at[pl.ds((2 * px + py) * rb, rb), :], dst_ref=land_ref.at[pl.ds((2 * x + y) * rb, rb), :],
            send_sem=sems[2 * k], recv_sem=sems[2 * k + 1], device_id=(px, py, c), device_id_type=MESH))
    return copies


def _sibling_copies(src_ref, land_ref, sems):
    x, y, c = _position()
    rb = src_ref.shape[0] // N_DEV
    return [pltpu.make_async_remote_copy(
        src_ref=src_ref.at[pl.ds((2 * q + 1 - c) * rb, rb), :], dst_ref=land_ref.at[pl.ds(q * rb, rb), :],
        send_sem=sems[2 * q], recv_sem=sems[2 * q + 1], device_id=(x, y, 1 - c), device_id_type=MESH)
        for q in range(N_CHIP)]


SPLIT_COPIES = {"to_chips": (_chip_copies, 3), "to_sibling": (_sibling_copies, N_CHIP)}


def _split_start(kind, src, land, name):
    copies_of, n_copies = SPLIT_COPIES[kind]

    def body(src_ref, land_ref, *rest):
        sems, token = rest[:2 * n_copies], rest[-1]
        for copy in copies_of(src_ref, land_ref, sems):
            copy.start()
        token[...] = jnp.zeros_like(token)

    res = pl.pallas_call(
        body, name=name,
        out_shape=(pltpu.SemaphoreType.DMA(()),) * (2 * n_copies)
        + (pltpu.HBM(src.shape, src.dtype), pltpu.HBM(land.shape, land.dtype), SDS((SUBLANES, LANES), F32)),
        in_specs=(IN_HBM, IN_HBM), out_specs=(SEM,) * (2 * n_copies) + (IN_HBM, IN_HBM, WHOLE),
        input_output_aliases={0: 2 * n_copies, 1: 2 * n_copies + 1},
        compiler_params=pltpu.CompilerParams(has_side_effects=SIDE_EFFECT))(
            pltpu.with_memory_space_constraint(src, pltpu.HBM), pltpu.with_memory_space_constraint(land, pltpu.HBM))
    return (kind, res[:2 * n_copies], res[-3], res[-2]), res[-1]


def _split_wait(pending, after, name):
    kind, sems, src, land = pending
    copies_of, n_copies = SPLIT_COPIES[kind]

    def body(src_ref, land_ref, *rest):
        for copy in copies_of(src_ref, land_ref, rest[:2 * n_copies]):
            copy.wait_send()
            copy.wait_recv()

    return pl.pallas_call(
        body, name=name, out_shape=(pltpu.HBM(src.shape, src.dtype), pltpu.HBM(land.shape, land.dtype)),
        in_specs=(IN_HBM, IN_HBM) + (SEM,) * (2 * n_copies) + (ANY,) * len(after), out_specs=(IN_HBM, IN_HBM),
        input_output_aliases={0: 0, 1: 1},
        compiler_params=pltpu.CompilerParams(has_side_effects=SIDE_EFFECT))(src, land, *sems, *after)


def _sum_devices(gathered, name):
    r = gathered.shape[0] // N_DEV

    def body(g_ref, o_ref):
        acc = g_ref[0]
        for s in range(1, N_DEV):
            acc = acc + g_ref[s]
        o_ref[...] = acc

    return _call(body, name=name, args=[gathered.reshape(N_DEV, r, LANES)], out_shape=SDS((r, LANES), F32),
                 in_specs=[WHOLE], out_specs=WHOLE)


def _cast_rows(w, name):
    def body(w_ref, o_ref):
        o_ref[...] = w_ref[...].astype(BF16)

    return _call(body, name=name, args=[w], out_shape=SDS(w.shape, BF16), in_specs=[WHOLE], out_specs=WHOLE)


def _cast_transposed(w, name):
    d, n = w.shape
    td = 512

    def body(w_ref, o_ref):
        o_ref[...] = w_ref[...].T.astype(BF16)

    return _call(body, name=name, args=[w], out_shape=SDS((n, d), BF16), grid=(d // td,),
                 in_specs=[pl.BlockSpec((td, n), lambda i: (i, 0))], out_specs=pl.BlockSpec((n, td), lambda i: (0, i)))


ROW_TILE = 256


def _rmsnorm_fwd(h, gain, name):
    t, d = h.shape

    def body(h_ref, g_ref, u_ref):
        x = h_ref[...]
        u_ref[...] = (x * lax.rsqrt(jnp.mean(x * x, axis=-1, keepdims=True) + NORM_EPS) * g_ref[...]).astype(BF16)

    row = pl.BlockSpec((ROW_TILE, d), lambda i: (i, 0))
    return _call(body, name=name, args=[h, gain], out_shape=SDS((t, d), BF16), grid=(t // ROW_TILE,),
                 in_specs=[row, pl.BlockSpec((1, d), lambda i: (0, 0))], out_specs=row)


def _rms_bwd_math(x, gain, dy):
    rstd = lax.rsqrt(jnp.mean(x * x, axis=-1, keepdims=True) + NORM_EPS)
    xhat = x * rstd
    dxh = dy * gain
    dx = rstd * (dxh - xhat * jnp.mean(dxh * xhat, axis=-1, keepdims=True))
    return dx, jnp.sum(dy * xhat, axis=0, keepdims=True)


def _rmsnorm_bwd(du, h, gain, resid, bf_scale, name, job=None):
    t, d = h.shape

    def body(du_ref, h_ref, g_ref, r_ref, dh_ref, dhb_ref, dg_ref):
        @pl.when(pl.program_id(0) == 0)
        def _():
            dg_ref[...] = jnp.zeros_like(dg_ref)

        dx, dg = _rms_bwd_math(h_ref[...], g_ref[...], du_ref[...])
        dh = r_ref[...] + dx
        dh_ref[...] = dh
        dhb_ref[...] = (bf_scale * dh).astype(BF16)
        dg_ref[...] += dg

    row = pl.BlockSpec((ROW_TILE, d), lambda i: (i, 0))
    vec = pl.BlockSpec((1, d), lambda i: (0, 0))
    return _call(body, name=name, args=[du, h, gain, resid],
                 out_shape=(SDS((t, d), F32), SDS((t, d), BF16), SDS((1, d), F32)), grid=(t // ROW_TILE,),
                 in_specs=[row, row, vec, row], out_specs=(row, row, vec), job=job)


def _final_loss(h, gain, target, name):
    t, d = h.shape

    def body(h_ref, g_ref, t_ref, dh_ref, dhb_ref, loss_ref, dg_ref):
        @pl.when(pl.program_id(0) == 0)
        def _():
            dg_ref[...] = jnp.zeros_like(dg_ref)
            loss_ref[...] = jnp.zeros_like(loss_ref)

        x = h_ref[...]
        gain = g_ref[...]
        out = x * lax.rsqrt(jnp.mean(x * x, axis=-1, keepdims=True) + NORM_EPS) * gain
        err = out - t_ref[...]
        loss_ref[...] += 0.5 * jnp.sum(jnp.mean(err * err, axis=-1, keepdims=True), axis=0, keepdims=True)
        dx, dg = _rms_bwd_math(x, gain, err * (1.0 / d))
        dh_ref[...] = dx
        dhb_ref[...] = (0.5 * dx).astype(BF16)
        dg_ref[...] += dg

    row = pl.BlockSpec((ROW_TILE, d), lambda i: (i, 0))
    vec = pl.BlockSpec((1, d), lambda i: (0, 0))
    one = pl.BlockSpec((SUBLANES, LANES), lambda i: (0, 0))
    return _call(body, name=name, args=[h, gain, target],
                 out_shape=(SDS((t, d), F32), SDS((t, d), BF16), SDS((SUBLANES, LANES), F32), SDS((1, d), F32)),
                 grid=(t // ROW_TILE,), in_specs=[row, vec, row], out_specs=(row, row, one, vec))


def _mixnorm_fwd(ya, yb, ga, gb, name):
    t, c = ya.shape

    def body(ya_ref, yb_ref, ga_ref, gb_ref, y_ref, yt_ref):
        for k, (src, g_ref) in enumerate(((ya_ref, ga_ref), (yb_ref, gb_ref))):
            x = src[...]
            u = x * lax.rsqrt(jnp.mean(x * x, axis=-1, keepdims=True) + NORM_EPS) * g_ref[...]
            y_ref[:, k * c:(k + 1) * c] = u.astype(BF16)
            yt_ref[k * c:(k + 1) * c, :] = u.T.astype(BF16)

    row = pl.BlockSpec((ROW_TILE, c), lambda i: (i, 0))
    vec = pl.BlockSpec((1, c), lambda i: (0, 0))
    return _call(body, name=name, args=[ya, yb, ga, gb],
                 out_shape=(SDS((t, 2 * c), BF16), SDS((2 * c, t), BF16)), grid=(t // ROW_TILE,),
                 in_specs=[row, row, vec, vec],
                 out_specs=(pl.BlockSpec((ROW_TILE, 2 * c), lambda i: (i, 0)),
                            pl.BlockSpec((2 * c, ROW_TILE), lambda i: (0, i))))


def _mixnorm_bwd(dy, ya, yb, ga, gb, name):
    t, c = ya.shape

    def body(dy_ref, ya_ref, yb_ref, ga_ref, gb_ref, dya_ref, dyb_ref, dga_ref, dgb_ref):
        @pl.when(pl.program_id(0) == 0)
        def _():
            dga_ref[...] = jnp.zeros_like(dga_ref)
            dgb_ref[...] = jnp.zeros_like(dgb_ref)

        dxa, dga = _rms_bwd_math(ya_ref[...], ga_ref[...], dy_ref[:, :c])
        dxb, dgb = _rms_bwd_math(yb_ref[...], gb_ref[...], dy_ref[:, c:])
        dya_ref[...] = dxa
        dyb_ref[...] = dxb
        dga_ref[...] += dga
        dgb_ref[...] += dgb

    row = pl.BlockSpec((ROW_TILE, c), lambda i: (i, 0))
    vec = pl.BlockSpec((1, c), lambda i: (0, 0))
    return _call(body, name=name, args=[dy, ya, yb, ga, gb],
                 out_shape=(SDS((t, c), F32), SDS((t, c), F32), SDS((1, c), F32), SDS((1, c), F32)),
                 grid=(t // ROW_TILE,),
                 in_specs=[pl.BlockSpec((ROW_TILE, 2 * c), lambda i: (i, 0)), row, row, vec, vec],
                 out_specs=(row, row, vec, vec))


def _tile(n, want):
    return max(t for t in range(LANES, min(n, want) + 1, LANES) if n % t == 0)


def _mm(a, b, *, nt, out_dtype, tm, tn, name, residual=None, scale=None, take=None, out_rows=None, row_offset=0,
        into=None, job=None):
    parts = list(a) if isinstance(a, (list, tuple)) else [a]
    widths = [p.shape[-1] for p in parts]
    k = sum(widths)
    n = b.shape[0] if nt else b.shape[1]
    if take is None:
        m, which = parts[0].shape[0], lambda i: i
        tm = _tile(math.gcd(m, row_offset), tm)
    else:
        tm, tiles, which = take
        m = tm * tiles
    tn = _tile(n, tn)
    out_rows = m if out_rows is None else out_rows

    def body(*refs):
        a_refs, b_ref, rest = refs[:len(parts)], refs[len(parts)], refs[len(parts) + 1:]
        o_ref = rest[-1]
        out, at = None, 0
        for a_ref, width in zip(a_refs, widths):
            av = a_ref[...].astype(BF16)
            if nt:
                term = lax.dot_general(av, b_ref[:, at:at + width].astype(BF16), NT, preferred_element_type=F32)
            else:
                term = jnp.dot(av, b_ref[at:at + width, :].astype(BF16), preferred_element_type=F32)
            out = term if out is None else out + term
            at += width
        if residual is not None:
            out = rest[0][...] + (out if scale is None else scale * out)
        o_ref[...] = out.astype(out_dtype)

    a_specs = [pl.BlockSpec((tm, width), lambda i, j: (which(i), 0)) for width in widths]
    in_specs = a_specs + [pl.BlockSpec((tn, k), lambda i, j: (j, 0)) if nt else pl.BlockSpec((k, tn), lambda i, j: (0, j))]
    args, aliases = parts + [b], {}
    if residual is not None:
        in_specs.append(pl.BlockSpec((tm, tn), lambda i, j: (i, j)))
        args.append(residual)
    if into is not None:
        in_specs.append(ANY)
        aliases[len(args)] = 0
        args.append(into)
    return _call(body, name=name, args=args, out_shape=SDS((out_rows, n), out_dtype), grid=(m // tm, n // tn),
                 in_specs=in_specs, out_specs=pl.BlockSpec((tm, tn), lambda i, j: (row_offset // tm + i, j)),
                 aliases=aliases, job=job)


FFN_TM = 512
FFN_HB = 512


def _ffn_hidden(u, w_in_t, name, job=None):
    t, d = u.shape
    f = w_in_t.shape[0] // 2

    def body(u_ref, w_ref, g_ref, up_ref, hid_ref, hid_t_ref):
        uu = u_ref[...]
        g = lax.dot_general(uu, w_ref[0], NT, preferred_element_type=F32)
        up = lax.dot_general(uu, w_ref[1], NT, preferred_element_type=F32)
        g_ref[...] = g.astype(BF16)
        up_ref[...] = up.astype(BF16)
        hid = (g * _sigmoid(g)) * up
        hid_ref[...] = hid.astype(BF16)
        hid_t_ref[...] = hid.T.astype(BF16)

    pre = pl.BlockSpec((FFN_TM, FFN_HB), lambda i, k: (i, k))
    return _call(body, name=name, args=[u, w_in_t.reshape(2, f, d)],
                 out_shape=(SDS((t, f), BF16), SDS((t, f), BF16), SDS((t, f), BF16), SDS((f, t), BF16)),
                 grid=(t // FFN_TM, f // FFN_HB),
                 in_specs=[pl.BlockSpec((FFN_TM, d), lambda i, k: (i, 0)),
                           pl.BlockSpec((2, FFN_HB, d), lambda i, k: (0, k, 0))],
                 out_specs=(pre, pre, pre, pl.BlockSpec((FFN_HB, FFN_TM), lambda i, k: (k, i))), job=job)


def _ffn_bwd(dfb, gpre, upre, w_in_t, w_out, name, job=None):
    t, d = dfb.shape
    f = w_out.shape[0]
    nk = f // FFN_HB

    def body(df_ref, g_ref, up_ref, w_ref, wo_ref, du_ref, da_t_ref, acc):
        k = pl.program_id(1)

        @pl.when(k == 0)
        def _():
            acc[...] = jnp.zeros_like(acc)

        dhid = lax.dot_general(df_ref[...], wo_ref[...], NT, preferred_element_type=F32)
        g, up = g_ref[...].astype(F32), up_ref[...].astype(F32)
        sig = _sigmoid(g)
        silu = g * sig
        dup = dhid * silu
        dg = dhid * up * (sig * (1.0 + g * (1.0 - sig)))
        da_t_ref[0] = dg.T.astype(BF16)
        da_t_ref[1] = dup.T.astype(BF16)
        acc[...] += (jnp.dot(dg.astype(BF16), w_ref[0], preferred_element_type=F32)
                     + jnp.dot(dup.astype(BF16), w_ref[1], preferred_element_type=F32))

        @pl.when(k == nk - 1)
        def _():
            du_ref[...] = acc[...]

    tok = pl.BlockSpec((FFN_TM, d), lambda i, k: (i, 0))
    pre = pl.BlockSpec((FFN_TM, FFN_HB), lambda i, k: (i, k))
    return _call(body, name=name, args=[dfb, gpre, upre, w_in_t.reshape(2, f, d), w_out],
                 out_shape=(SDS((t, d), F32), SDS((2, f, t), BF16)), grid=(t // FFN_TM, nk),
                 in_specs=[tok, pre, pre, pl.BlockSpec((2, FFN_HB, d), lambda i, k: (0, k, 0)),
                           pl.BlockSpec((FFN_HB, d), lambda i, k: (k, 0))],
                 out_specs=(tok, pl.BlockSpec((2, FFN_HB, FFN_TM), lambda i, k: (0, k, i))),
                 scratch_shapes=[pltpu.VMEM((FFN_TM, d), F32)], job=job)


CH = LANES
PAD = SUBLANES


def _lru_gates(xc, gw_ref, gb_ref, lam_ref, z):
    xcb = xc.astype(BF16)
    r = _sigmoid(jnp.dot(xcb, gw_ref[2 * z], preferred_element_type=F32) + gb_ref[pl.ds(2 * z, 1), :])
    i = _sigmoid(jnp.dot(xcb, gw_ref[2 * z + 1], preferred_element_type=F32) + gb_ref[pl.ds(2 * z + 1, 1), :])
    sp = _softplus(-lam_ref[pl.ds(z, 1), :])
    log_a = (-RG_C * r) * sp
    a = jnp.exp(log_a)
    mult = jnp.sqrt(-_expm1(2.0 * log_a))
    return r, i, sp, a, mult


def _conv(xpad, cw_ref, cb_ref, t):
    xc = cb_ref[...] + cw_ref[pl.ds(0, 1), :] * xpad[pl.ds(PAD - 2, t), :]
    for j in range(1, CONV_WIDTH):
        xc = xc + cw_ref[pl.ds(j, 1), :] * xpad[pl.ds(PAD - 2 + j, t), :]
    return xc


def _fill_padded(pad_ref, value, t):
    pad_ref[pl.ds(0, PAD), :] = jnp.zeros((PAD, CH), F32)
    pad_ref[pl.ds(PAD + t, PAD), :] = jnp.zeros((PAD, CH), F32)
    pad_ref[pl.ds(PAD, t), :] = value


def _scan_pair(t, a_up, b_up, out_up, a_down, b_down, out_down):
    row = lax.broadcasted_iota(jnp.int32, (SUBLANES, CH), 0)

    def compose(a, b, rising):
        for dist in (1, 2, 4):
            shift = dist if rising else SUBLANES - dist
            keep = (row >= dist) if rising else (row < SUBLANES - dist)
            b = jnp.where(keep, b + a * pltpu.roll(b, shift, axis=0), b)
            a = jnp.where(keep, a * pltpu.roll(a, shift, axis=0), a)
        return a, b

    def step(tt, carry):
        hu, hd = carry
        lo = pl.ds(pl.multiple_of(tt * SUBLANES, SUBLANES), SUBLANES)
        hi = pl.ds(pl.multiple_of(t - SUBLANES - tt * SUBLANES, SUBLANES), SUBLANES)
        a, b = compose(a_up[lo, :], b_up[lo, :], True)
        up = b + a * hu
        out_up[lo, :] = up
        a, b = compose(a_down[hi, :], b_down[hi, :], False)
        down = b + a * hd
        out_down[hi, :] = down
        return up[SUBLANES - 1:, :], down[:1, :]

    zero = jnp.zeros((1, CH), F32)
    lax.fori_loop(0, t // SUBLANES, step, (zero, zero), unroll=2)


def _lru_fwd(proj, cw, cb, gw, gb, lam, name, job=None):
    t = proj.shape[0]
    c = cw.shape[1]
    ncb = c // CH

    def body(x_ref, g_ref, cw_ref, cb_ref, gw_ref, gb_ref, lam_ref, ya_ref, hf_ref, hb_ref, xpad, a0, b0, a1, b1):
        _fill_padded(xpad, x_ref[...], t)
        xc = _conv(xpad, cw_ref, cb_ref, t)
        for z, (a_s, b_s) in enumerate(((a0, b0), (a1, b1))):
            _, i, _, a, mult = _lru_gates(xc, gw_ref, gb_ref, lam_ref, z)
            a_s[...] = a
            b_s[...] = mult * (i * xc)
        _scan_pair(t, a0, b0, hf_ref, a1, b1, hb_ref)
        gelu, _ = _gelu_parts(g_ref[...])
        ya_ref[...] = gelu * (hf_ref[...] + hb_ref[...])

    col = lambda off: pl.BlockSpec((t, CH), lambda i: (0, off + i))
    small = lambda rows: pl.BlockSpec((rows, CH), lambda i: (0, i))
    return _call(body, name=name, args=[proj, proj, cw, cb, gw, gb, lam], out_shape=(SDS((t, c), F32),) * 3,
                 grid=(ncb,),
                 in_specs=[col(0), col(ncb), small(CONV_WIDTH), small(1),
                           pl.BlockSpec((4, None, CH, CH), lambda i: (0, i, 0, 0)), small(4), small(2)],
                 out_specs=(col(0),) * 3,
                 scratch_shapes=[pltpu.VMEM((t + 2 * PAD, CH), F32)] + [pltpu.VMEM((t, CH), F32)] * 4, job=job)


def _lru_bwd(proj, cw, cb, gw, gb, lam, hf, hb, dya, name, job=None):
    t = proj.shape[0]
    c = cw.shape[1]
    ncb = c // CH

    def body(x_ref, g_ref, cw_ref, cb_ref, gw_ref, gb_ref, lam_ref, hf_ref, hb_ref, dya_ref,
             dx_ref, dg_ref, dt_ref, dcw_ref, dcb_ref, dgw_ref, dgb_ref, dlam_ref,
             xpad, hpad, dxc, a0, a1, dhs, dh0, dh1):
        _fill_padded(xpad, x_ref[...], t)
        xc = _conv(xpad, cw_ref, cb_ref, t)
        xcb = xc.astype(BF16)
        gates = [_lru_gates(xc, gw_ref, gb_ref, lam_ref, z) for z in range(2)]

        gelu, dgelu = _gelu_parts(g_ref[...])
        dya = dya_ref[...]
        dgate = dya * (hf_ref[...] + hb_ref[...]) * dgelu
        dg_ref[...] = dgate.astype(BF16)
        dt_ref[1] = dgate.T.astype(BF16)
        dhs[...] = dya * gelu

        _fill_padded(hpad, gates[0][3], t)
        a0[...] = hpad[pl.ds(PAD + 1, t), :]
        _fill_padded(hpad, gates[1][3], t)
        a1[...] = hpad[pl.ds(PAD - 1, t), :]
        _scan_pair(t, a1, dhs, dh1, a0, dhs, dh0)

        acc_dxc = jnp.zeros((t, CH), F32)
        for z, (h_ref, dh_ref, shift) in enumerate(((hf_ref, dh0, -1), (hb_ref, dh1, 1))):
            r, i, sp, a, mult = gates[z]
            _fill_padded(hpad, h_ref[...], t)
            h_nb = hpad[pl.ds(PAD + shift, t), :]
            db = dh_ref[...]
            da = db * h_nb
            d_i = db * mult * xc
            acc_dxc = acc_dxc + db * mult * i
            d_mult = db * i * xc
            d_la = da * a - d_mult * (a * a) / mult
            d_r = d_la * (-RG_C * sp)
            dlam_ref[pl.ds(z, 1), :] = (jnp.sum(d_la * (-RG_C * r), axis=0, keepdims=True)
                                        * (-_sigmoid(-lam_ref[pl.ds(z, 1), :])))
            for gate, d_pre in ((0, d_r * r * (1.0 - r)), (1, d_i * i * (1.0 - i))):
                zg = 2 * z + gate
                dgb_ref[pl.ds(zg, 1), :] = jnp.sum(d_pre, axis=0, keepdims=True)
                d_pre_b = d_pre.astype(BF16)
                dgw_ref[zg] = lax.dot_general(xcb, d_pre_b, TN, preferred_element_type=F32)
                acc_dxc = acc_dxc + lax.dot_general(d_pre_b, gw_ref[zg], NT, preferred_element_type=F32)

        dcb_ref[...] = jnp.sum(acc_dxc, axis=0, keepdims=True)
        for j in range(CONV_WIDTH):
            dcw_ref[pl.ds(j, 1), :] = jnp.sum(acc_dxc * xpad[pl.ds(PAD - 2 + j, t), :], axis=0, keepdims=True)
        _fill_padded(dxc, acc_dxc, t)
        dx = cw_ref[pl.ds(0, 1), :] * dxc[pl.ds(PAD + 2, t), :]
        for j in range(1, CONV_WIDTH):
            dx = dx + cw_ref[pl.ds(j, 1), :] * dxc[pl.ds(PAD + 2 - j, t), :]
        dx_ref[...] = dx.astype(BF16)
        dt_ref[0] = dx.T.astype(BF16)

    col = lambda off: pl.BlockSpec((t, CH), lambda i: (0, off + i))
    small = lambda rows: pl.BlockSpec((rows, CH), lambda i: (0, i))
    dense = pl.BlockSpec((4, None, CH, CH), lambda i: (0, i, 0, 0))
    padded = pltpu.VMEM((t + 2 * PAD, CH), F32)
    return _call(
        body, name=name, args=[proj, proj, cw, cb, gw, gb, lam, hf, hb, dya],
        out_shape=(SDS((t, c), BF16), SDS((t, c), BF16), SDS((2, c, t), BF16), SDS((CONV_WIDTH, c), F32),
                   SDS((1, c), F32), SDS((4, ncb, CH, CH), F32), SDS((4, c), F32), SDS((2, c), F32)),
        grid=(ncb,),
        in_specs=[col(0), col(ncb), small(CONV_WIDTH), small(1), dense, small(4), small(2), col(0), col(0), col(0)],
        out_specs=(col(0), col(0), pl.BlockSpec((2, CH, t), lambda i: (0, i, 0)), small(CONV_WIDTH), small(1),
                   dense, small(4), small(2)),
        scratch_shapes=[padded, padded, padded] + [pltpu.VMEM((t, CH), F32)] * 5, job=job)


Q_ROWS = 4
BAND_ROWS = WIN_ROWS + Q_ROWS
BAND_PAIRS = BAND_ROWS // 2
Q_BLOCK = Q_ROWS * GRID_W
BAND = BAND_ROWS * GRID_W
PAIR_W = 2 * GRID_W
N_BOTH = 2 * WIN_ROWS - 2
ENTRY_LEFT_OUT, ENTRY_RIGHT_OUT, ENTRY_OUT = N_BOTH, N_BOTH + 1, N_BOTH + 2
N_ENTRIES = N_BOTH + 3


def _bias_tables(rpb):
    cols = np.arange(GRID_W)
    start = np.clip(cols - WIN_COLS // 2, 0, GRID_W - WIN_COLS)
    valid = (cols[None, :] >= start[:, None]) & (cols[None, :] < start[:, None] + WIN_COLS)
    col_off = np.clip(cols[None, :] - cols[:, None] + WIN_COLS - 1, 0, 2 * WIN_COLS - 2)
    pick_col = jnp.asarray(np.eye(2 * WIN_COLS - 1, dtype=np.float32)[col_off] * valid[..., None])
    by_row = jnp.einsum("hrc,qkc->hrqk", rpb, pick_col, precision=lax.Precision.HIGHEST)
    by_row = jnp.where(jnp.asarray(valid)[None, None], by_row, NEG)
    out = jnp.full_like(by_row[:, :1], NEG)
    first_in, last_in = WIN_ROWS - 1 - WIN_ROWS // 2, 2 * (WIN_ROWS - 1) - WIN_ROWS // 2
    both = jnp.concatenate([by_row[:, :-1], by_row[:, 1:]], axis=-1)
    left_out = jnp.concatenate([out, by_row[:, first_in:first_in + 1]], axis=-1)
    right_out = jnp.concatenate([by_row[:, last_in:last_in + 1], out], axis=-1)
    return jnp.concatenate([both, left_out, right_out, jnp.concatenate([out, out], axis=-1)], axis=1)


def _band_start(m, rows):
    return jnp.clip(Q_ROWS * m - WIN_ROWS // 2, 0, rows - BAND_ROWS)


def _entry(r, key_row, rows):
    w0 = jnp.clip(r - WIN_ROWS // 2, 0, rows - WIN_ROWS)
    left = (key_row >= w0) & (key_row < w0 + WIN_ROWS)
    right = (key_row + 1 >= w0) & (key_row + 1 < w0 + WIN_ROWS)
    return jnp.where(left & right, key_row - r + WIN_ROWS - 1,
                     jnp.where(right, ENTRY_LEFT_OUT, jnp.where(left, ENTRY_RIGHT_OUT, ENTRY_OUT)))


def _transposed_pairs(dst, src_ref):
    for g in range(dst.shape[0]):
        dst[g] = src_ref[pl.ds(g * PAIR_W, PAIR_W), :].T.astype(BF16)


def _band_of(pairs_ref, first_pair, hh):
    heads = pl.ds(hh * HEAD_DIM, HEAD_DIM)
    return jnp.concatenate([pairs_ref[first_pair + g, heads, :] for g in range(BAND_PAIRS)], axis=1)


def _attn_block(qs, kt, tz_ref, hh, m, rows):
    rs = _band_start(m, rows)
    lanes = pl.ds(hh * HEAD_DIM, HEAD_DIM)
    qrows = pl.ds(pl.multiple_of(m * Q_BLOCK, Q_BLOCK), Q_BLOCK)
    band = pl.ds(pl.multiple_of(rs * GRID_W, PAIR_W), BAND)
    entries = [[_entry(Q_ROWS * m + i, rs + 2 * g, rows) for g in range(BAND_PAIRS)] for i in range(Q_ROWS)]
    bias = jnp.concatenate([jnp.concatenate([tz_ref[hh, e] for e in row], axis=1) for row in entries], axis=0)
    q = qs[qrows, lanes]
    s = jnp.dot(q, _band_of(kt, rs // 2, hh), preferred_element_type=F32) * (HEAD_DIM ** -0.5) + bias
    p = jnp.exp(s - jnp.max(s, axis=-1, keepdims=True))
    p = p / jnp.sum(p, axis=-1, keepdims=True)
    return q, p, qrows, band, lanes, entries, rs // 2


def _attn_fwd(proj, tables, width, name, job=None):
    t = proj.shape[0]
    rows = t // GRID_W
    npair = width // LANES
    first = (proj.shape[1] - 3 * width) // LANES

    def body(q_ref, k_ref, v_ref, tz_ref, o_ref, qs, vs, kt):
        qs[...] = q_ref[...].astype(BF16)
        vs[...] = v_ref[...].astype(BF16)
        _transposed_pairs(kt, k_ref)

        def block(m, carry):
            for hh in range(2):
                _, p, qrows, band, lanes, _, _ = _attn_block(qs, kt, tz_ref, hh, m, rows)
                o_ref[qrows, lanes] = jnp.dot(p.astype(BF16), vs[band, lanes], preferred_element_type=F32)
            return carry

        lax.fori_loop(0, rows // Q_ROWS, block, 0, unroll=2)

    col = lambda off: pl.BlockSpec((t, LANES), lambda i: (0, off + i))
    return _call(body, name=name, args=[proj, proj, proj, tables], out_shape=SDS((t, width), F32), grid=(npair,),
                 in_specs=[col(first), col(first + npair), col(first + 2 * npair),
                           pl.BlockSpec((2, N_ENTRIES, GRID_W, PAIR_W), lambda i: (i, 0, 0, 0))],
                 out_specs=col(0),
                 scratch_shapes=[pltpu.VMEM((t, LANES), BF16)] * 2 + [pltpu.VMEM((t // PAIR_W, LANES, PAIR_W), BF16)],
                 job=job)


def _attn_bwd(proj, tables, dyb, name, job=None):
    t, width = dyb.shape
    rows = t // GRID_W
    npair = width // LANES
    first = (proj.shape[1] - 3 * width) // LANES

    def body(q_ref, k_ref, v_ref, tz_ref, do_ref, dq_ref, dk_ref, dv_ref, dt_ref, dtz_ref, dq_s, dk_s, dv_s,
             qs, ks, vs, dos, kt, vt):
        qs[...] = q_ref[...].astype(BF16)
        ks[...] = k_ref[...].astype(BF16)
        vs[...] = v_ref[...].astype(BF16)
        dos[...] = do_ref[...].astype(BF16)
        _transposed_pairs(kt, k_ref)
        _transposed_pairs(vt, v_ref)
        dk_s[...] = jnp.zeros_like(dk_s)
        dv_s[...] = jnp.zeros_like(dv_s)
        dtz_ref[...] = jnp.zeros_like(dtz_ref)

        def block(m, carry):
            for hh in range(2):
                q, p, qrows, band, lanes, entries, first_pair = _attn_block(qs, kt, tz_ref, hh, m, rows)
                do = dos[qrows, lanes]
                dp = jnp.dot(do, _band_of(vt, first_pair, hh), preferred_element_type=F32)
                ds = p * (dp - jnp.sum(dp * p, axis=-1, keepdims=True))
                for i, row in enumerate(entries):
                    for g, e in enumerate(row):
                        dtz_ref[hh, e] += ds[i * GRID_W:(i + 1) * GRID_W, g * PAIR_W:(g + 1) * PAIR_W]
                dsb = (ds * (HEAD_DIM ** -0.5)).astype(BF16)
                dq_s[qrows, lanes] = jnp.dot(dsb, ks[band, lanes], preferred_element_type=F32)
                dk_s[band, lanes] += lax.dot_general(dsb, q, TN, preferred_element_type=F32)
                dv_s[band, lanes] += lax.dot_general(p.astype(BF16), do, TN, preferred_element_type=F32)
            return carry

        lax.fori_loop(0, rows // Q_ROWS, block, 0)
        for n, (src, dst) in enumerate(((dq_s, dq_ref), (dk_s, dk_ref), (dv_s, dv_ref))):
            val = src[...]
            dst[...] = val.astype(BF16)
            dt_ref[n] = val.T.astype(BF16)

    col = lambda off: pl.BlockSpec((t, LANES), lambda i: (0, off + i))
    table = pl.BlockSpec((2, N_ENTRIES, GRID_W, PAIR_W), lambda i: (i, 0, 0, 0))
    pairs = pltpu.VMEM((t // PAIR_W, LANES, PAIR_W), BF16)
    return _call(body, name=name, args=[proj, proj, proj, tables, dyb],
                 out_shape=(SDS((t, width), BF16),) * 3 + (SDS((3, width, t), BF16), SDS(tables.shape, F32)),
                 grid=(npair,),
                 in_specs=[col(first), col(first + npair), col(first + 2 * npair), table, col(0)],
                 out_specs=(col(0), col(0), col(0), pl.BlockSpec((3, LANES, t), lambda i: (0, i, 0)), table),
                 scratch_shapes=[pltpu.VMEM((t, LANES), F32)] * 3 + [pltpu.VMEM((t, LANES), BF16)] * 4 + [pairs, pairs],
                 job=job)


def _adamw_math(w, g, m, v):
    m = ADAM_B1 * m + (1.0 - ADAM_B1) * g
    v = ADAM_B2 * v + (1.0 - ADAM_B2) * (g * g)
    m_hat = m / (1.0 - ADAM_B1 ** ADAM_STEP)
    v_hat = v / (1.0 - ADAM_B2 ** ADAM_STEP)
    delta = -ADAM_LR * (m_hat / (jnp.sqrt(v_hat) + ADAM_EPS) + ADAM_WD * w)
    return delta, m, v


def _sum_partials(p_ref):
    g = p_ref[0].astype(F32)
    for s in range(1, N_CHIP):
        g = g + p_ref[s].astype(F32)
    return g


def _adamw_rows(w, partials, m, v, name, after=()):
    rb, n = w.shape
    tr = 64

    def body(w_ref, p_ref, m_ref, v_ref, *rest):
        g_ref, d_ref, nm_ref, nv_ref = rest[len(after):]
        g = _sum_partials(p_ref)
        g_ref[...] = g
        d_ref[...], nm_ref[...], nv_ref[...] = _adamw_math(w_ref[...], g, m_ref[...], v_ref[...])

    blk = pl.BlockSpec((tr, n), lambda i: (i, 0))
    return _call(body, name=name, args=[w, partials.reshape(N_CHIP, rb, n), m, v, *after],
                 out_shape=(SDS((rb, n), F32),) * 4, grid=(rb // tr,),
                 in_specs=[blk, pl.BlockSpec((N_CHIP, tr, n), lambda i: (0, i, 0)), blk, blk] + [ANY] * len(after),
                 out_specs=(blk,) * 4)


def _adamw_cols(w, partials, m, v, name, after=()):
    d, nb = w.shape
    td = 256
    parts = list(partials) if isinstance(partials, (list, tuple)) else [partials]
    heights = [p.shape[0] // N_CHIP for p in parts]

    def body(w_ref, m_ref, v_ref, *rest):
        p_refs, (g_ref, d_ref, nm_ref, nv_ref) = rest[:len(parts)], rest[len(parts) + len(after):]
        g = jnp.concatenate([_sum_partials(p_ref) for p_ref in p_refs], axis=0).T
        g_ref[...] = g
        d_ref[...], nm_ref[...], nv_ref[...] = _adamw_math(w_ref[...], g, m_ref[...], v_ref[...])

    blk = pl.BlockSpec((td, nb), lambda i: (i, 0))
    return _call(body, name=name, args=[w, m, v, *[p.reshape(N_CHIP, h, d) for p, h in zip(parts, heights)], *after],
                 out_shape=(SDS((d, nb), F32),) * 4, grid=(d // td,),
                 in_specs=[blk, blk, blk] + [pl.BlockSpec((N_CHIP, h, td), lambda i: (0, 0, i)) for h in heights]
                 + [ANY] * len(after), out_specs=(blk,) * 4)


def _adamw_small(w, g, m, v, name):
    def body(w_ref, g_ref, m_ref, v_ref, d_ref, nm_ref, nv_ref):
        d_ref[...], nm_ref[...], nv_ref[...] = _adamw_math(w_ref[...], g_ref[...], m_ref[...], v_ref[...])

    return _call(body, name=name, args=[w, g, m, v], out_shape=(SDS(w.shape, F32),) * 3, in_specs=[WHOLE] * 4,
                 out_specs=(WHOLE,) * 3)


TILE = SUBLANES * LANES


def _pack(arrays):
    parts = []
    for a in arrays:
        flat = a.reshape(-1).astype(F32)
        flat = jnp.pad(flat, (0, -flat.size % TILE))
        parts.append(flat.reshape(-1, LANES))
    return jnp.concatenate(parts, axis=0)


def _unpack(pack, like):
    out, row = [], 0
    for a in like:
        n = int(np.prod(a.shape))
        nrows = -(-n // TILE) * SUBLANES
        out.append(pack[row:row + nrows].reshape(-1)[:n].reshape(a.shape))
        row += nrows
    return out


def _dense_gate_blocks(gate_w):
    w = gate_w.reshape(4, -1, 2, HEAD_DIM, HEAD_DIM)
    zero = jnp.zeros_like(w[:, :, 0])
    top = jnp.concatenate([w[:, :, 0], zero], axis=-1)
    bottom = jnp.concatenate([zero, w[:, :, 1]], axis=-1)
    return jnp.concatenate([top, bottom], axis=-2)


def _diag_gate_blocks(dense, shape):
    even = dense[:, :, :HEAD_DIM, :HEAD_DIM]
    odd = dense[:, :, HEAD_DIM:, HEAD_DIM:]
    return jnp.stack([even, odd], axis=2).reshape(shape)


LARGE = ("ffn1_w_in", "ffn1_w_out", "w_in_mix", "w_out_mix", "ffn2_w_in", "ffn2_w_out")
COLUMN_SHARDED = ("ffn1_w_in", "w_in_mix", "ffn2_w_in")
SHARDED_SMALL = ("lru_conv_w", "lru_lambda")
REPLICATED = ("norm_ffn1", "norm_mix", "lru_conv_b", "lru_gate_w", "lru_gate_b", "attn_rpb", "lru_out_norm",
              "attn_out_norm", "norm_ffn2", "norm_final")
SMALL_ORDER = REPLICATED + SHARDED_SMALL
WEIGHTS = ("norm_ffn1", "ffn1_w_in", "ffn1_w_out", "norm_mix", "w_in_mix", "lru_conv_w", "lru_conv_b", "lru_gate_w",
           "lru_gate_b", "lru_lambda", "attn_rpb", "lru_out_norm", "attn_out_norm", "w_out_mix", "norm_ffn2",
           "ffn2_w_in", "ffn2_w_out", "norm_final")


PARTS = {("gather", "w_in_mix"): 4, ("gather", "ffn2_w_in"): 8}
CARRIES = {
    "gather_ffn1_in": [(("gather", "ffn1_w_in"), 1), (("gather", "small"), 1)],
    "ffn1_hidden": [(("gather", "ffn1_w_out"), 1), (("gather", "w_in_mix"), 1)],
    "ffn1_out": [(("gather", "w_in_mix"), 3)],
    "mix_in_proj": [(("gather", "w_out_mix"), 1), (("gather", "ffn2_w_in"), 1)],
    "lru_fwd": [(("gather", "ffn2_w_in"), 3)],
    "attn_fwd": [(("gather", "ffn2_w_in"), 3)],
    "mix_out_proj": [(("gather", "ffn2_w_in"), 1)],
    "ffn2_hidden": [(("gather", "ffn2_w_out"), 1)],
    "ffn1_bwd": [(("gather", "small_grads"), 1)],
    "gather_late_grads": [(("gather", "late_grads"), 1)],
}


class _Transfer:
    def __init__(self, kind, src, dest, block_rows, parts):
        self.kind, self.src, self.dest = kind, src, dest
        self.ranges, self.taken = _split(block_rows, parts), 0

    def take(self, count):
        lo, hi = self.ranges[self.taken][0], self.ranges[self.taken + count - 1][1]
        self.taken += count
        return _Piece(self.kind, self.src, self.dest, lo, hi)


class _Traffic:
    def __init__(self):
        self.transfers = {}

    def open(self, kind, name, src):
        if kind == "gather":
            dest, rows = _gathered(src), src.shape[0]
        elif kind == "to_sibling":
            dest, rows = SDS((src.shape[0] // 2, src.shape[1]), src.dtype), src.shape[0] // N_DEV
        else:
            dest, rows = SDS(src.shape, src.dtype), src.shape[0] // N_CHIP
        self.transfers[kind, name] = _Transfer(kind, src, dest, rows, PARTS.get((kind, name), 1))

    def _job(self, host):
        moved = [self.transfers[key] for key, _ in CARRIES[host]]
        return moved, _Job([tr.take(count) for tr, (_, count) in zip(moved, CARRIES[host])])

    def carry(self, host, fn, *args, **kw):
        if host not in CARRIES:
            return fn(*args, name=host, **kw)
        moved, job = self._job(host)
        res, landed = fn(*args, name=host, job=job, **kw)
        for tr, arr in zip(moved, landed):
            tr.dest = arr
        return res

    def alone(self, host):
        moved, job = self._job(host)
        for tr, arr in zip(moved, _run_job(job, host)):
            tr.dest = arr

    def result(self, kind, name):
        tr = self.transfers.pop((kind, name))
        assert tr.taken == len(tr.ranges), (kind, name)
        return tr.dest


def _forward_backward(x, target, shards, sharded_small, s):
    c = s["lru_conv_b"].shape[1]
    width = s["attn_out_norm"].shape[1]
    t = x.shape[0]
    traffic = _Traffic()
    carry = traffic.carry
    weight = lambda n: traffic.result("gather", n)

    for n in LARGE:
        traffic.open("gather", n, shards[n])
    traffic.open("gather", "small", sharded_small)
    traffic.alone("gather_ffn1_in")
    full_small = weight("small").reshape(N_DEV, SUBLANES, c // N_DEV)
    conv_w = full_small[:, :CONV_WIDTH].transpose(1, 0, 2).reshape(CONV_WIDTH, c)
    lam = full_small[:, CONV_WIDTH:CONV_WIDTH + 2].transpose(1, 0, 2).reshape(2, c)
    w = {"ffn1_w_in": weight("ffn1_w_in")}
    ffn_out = dict(nt=False, out_dtype=F32, tm=512, tn=512, scale=0.5)
    u1 = _rmsnorm_fwd(x, s["norm_ffn1"], "norm_ffn1")
    g1, up1, hid1, hid1_t = carry("ffn1_hidden", _ffn_hidden, u1, w["ffn1_w_in"])
    w["ffn1_w_out"] = weight("ffn1_w_out")
    h1 = carry("ffn1_out", _mm, hid1, w["ffn1_w_out"], residual=x, **ffn_out)
    w["w_in_mix"] = weight("w_in_mix")
    u2 = _rmsnorm_fwd(h1, s["norm_mix"], "norm_mix")
    proj = carry("mix_in_proj", _mm, u2, w["w_in_mix"], nt=True, out_dtype=F32, tm=512, tn=512)
    w["w_out_mix"] = weight("w_out_mix")
    gw = _dense_gate_blocks(s["lru_gate_w"]).astype(BF16)
    gb = s["lru_gate_b"].reshape(4, c)
    tables, tables_vjp = jax.vjp(_bias_tables, s["attn_rpb"])
    ya, hf, hb = carry("lru_fwd", _lru_fwd, proj, conv_w, s["lru_conv_b"], gw, gb, lam)
    yb = carry("attn_fwd", _attn_fwd, proj, tables, width)
    y, yt = _mixnorm_fwd(ya, yb, s["lru_out_norm"], s["attn_out_norm"], "mix_norm")
    h2 = carry("mix_out_proj", _mm, y, w["w_out_mix"], nt=False, out_dtype=F32, tm=512, tn=512, residual=h1)
    u3 = _rmsnorm_fwd(h2, s["norm_ffn2"], "norm_ffn2")
    w["ffn2_w_in"] = weight("ffn2_w_in")
    g2, up2, hid2, hid2_t = carry("ffn2_hidden", _ffn_hidden, u3, w["ffn2_w_in"])
    w["ffn2_w_out"] = weight("ffn2_w_out")
    h3 = carry("ffn2_out", _mm, hid2, w["ffn2_w_out"], residual=h2, **ffn_out)
    dh3, df2, loss_part, d_norm_final = _final_loss(h3, s["norm_final"], target, "final_loss")

    grads = {}
    grad_of = dict(nt=False, out_dtype=BF16, tm=512, tn=1024)

    to_sibling, to_chips = {}, {}

    def reduce_in_chip(n):
        land = _blank_like(grads[n], grads[n].shape[0] // 2, "landing_" + n)
        to_sibling[n], token = _split_start("to_sibling", grads[n], land, "to_sibling_" + n)
        RUN_AFTER.append(token)

    def reduce_over_chips(n, after):
        own, got = _split_wait(to_sibling.pop(n), [after], "from_sibling_" + n)
        summed = _pair_sum(own, got, "pair_sum_" + n)
        to_chips[n], token = _split_start("to_chips", summed, _own_slot(summed, "own_slot_" + n), "to_chips_" + n)
        RUN_AFTER.append(token)
        return token

    f = hid2_t.shape[0]
    grads["ffn2_w_out"] = carry("ffn2_out_grad", _mm, hid2_t, df2, **grad_of)
    reduce_in_chip("ffn2_w_out")
    du3, da2_t = carry("ffn2_bwd", _ffn_bwd, df2, g2, up2, w["ffn2_w_in"], w["ffn2_w_out"])
    reduce_over_chips("ffn2_w_out", du3)
    grads["ffn2_w_in"] = carry("ffn2_in_grad", _mm, da2_t.reshape(2 * f, t), u3, **grad_of)
    reduce_in_chip("ffn2_w_in")
    dh2, dh2b, d_norm_ffn2 = carry("norm_ffn2_bwd", _rmsnorm_bwd, du3, h2, s["norm_ffn2"], dh3, 1.0)
    grads["w_out_mix"] = carry("mix_out_grad", _mm, yt, dh2b, **grad_of)
    reduce_over_chips("ffn2_w_in", grads["w_out_mix"])
    reduce_in_chip("w_out_mix")
    dy = carry("mix_out_bwd", _mm, dh2b, w["w_out_mix"], nt=True, out_dtype=F32, tm=512, tn=512)
    dya, dyb, d_lru_out_norm, d_attn_out_norm = _mixnorm_bwd(dy, ya, yb, s["lru_out_norm"], s["attn_out_norm"],
                                                             "mix_norm_bwd")
    dq, dk, dv, dqkv_t, d_tables = carry("attn_bwd", _attn_bwd, proj, tables, dyb)
    reduce_over_chips("w_out_mix", dq)
    dx_lru, dg_lru, dxg_t, d_conv_w, d_conv_b, d_gw, d_gb, d_lam = carry(
        "lru_bwd", _lru_bwd, proj, conv_w, s["lru_conv_b"], gw, gb, lam, hf, hb, dya)
    rows_of = 2 * c + 3 * width
    lru_rows = carry("mix_in_grad_lru", _mm, dxg_t.reshape(2 * c, t), u2, out_rows=rows_of, **grad_of)
    grads["w_in_mix"] = carry("mix_in_grad_attn", _mm, dqkv_t.reshape(3 * width, t), u2, out_rows=rows_of,
                              row_offset=2 * c, into=lru_rows, **grad_of)
    reduce_in_chip("w_in_mix")
    du2 = carry("mix_in_bwd", _mm, [dx_lru, dg_lru, dq, dk, dv], w["w_in_mix"], nt=False, out_dtype=F32, tm=512,
                tn=512)
    dh1, df1, d_norm_mix = carry("norm_mix_bwd", _rmsnorm_bwd, du2, h1, s["norm_mix"], dh2, 0.5)
    reduce_over_chips("w_in_mix", dh1)

    by_device = lambda a: a.reshape(a.shape[0], N_DEV, -1).transpose(1, 0, 2)
    small = {
        "norm_mix": d_norm_mix, "lru_conv_b": d_conv_b, "lru_gate_w": _diag_gate_blocks(d_gw, s["lru_gate_w"].shape),
        "lru_gate_b": d_gb.reshape(s["lru_gate_b"].shape), "attn_rpb": tables_vjp(d_tables)[0],
        "lru_out_norm": d_lru_out_norm, "attn_out_norm": d_attn_out_norm, "norm_ffn2": d_norm_ffn2,
        "norm_final": d_norm_final, "lru_conv_w": by_device(d_conv_w), "lru_lambda": by_device(d_lam),
    }
    early = [small[n] for n in SMALL_ORDER[1:]]
    traffic.open("gather", "small_grads", _pack(early))

    grads["ffn1_w_out"] = carry("ffn1_out_grad", _mm, hid1_t, df1, **grad_of)
    reduce_in_chip("ffn1_w_out")
    du1, da1_t = carry("ffn1_bwd", _ffn_bwd, df1, g1, up1, w["ffn1_w_in"], w["ffn1_w_out"])
    grad_x, _, d_norm_ffn1 = carry("norm_ffn1_bwd", _rmsnorm_bwd, du1, x, s["norm_ffn1"], dh1, 1.0)
    traffic.open("gather", "late_grads", _pack([d_norm_ffn1]))
    traffic.alone("gather_late_grads")
    late = traffic.result("gather", "late_grads")
    reduce_over_chips("ffn1_w_out", late)
    half = 2 * f // N_DEV // 2
    half_rows = lambda h: (half, N_DEV, lambda i: 2 * i + h)
    grads["ffn1_w_in_a"] = carry("ffn1_in_grad_a", _mm, da1_t.reshape(2 * f, t), u1, take=half_rows(0), **grad_of)
    reduce_in_chip("ffn1_w_in_a")
    grads["ffn1_w_in_b"] = carry("ffn1_in_grad_b", _mm, da1_t.reshape(2 * f, t), u1, take=half_rows(1), **grad_of)
    reduce_over_chips("ffn1_w_in_a", grads["ffn1_w_in_b"])
    reduce_in_chip("ffn1_w_in_b")
    reduced = (_unpack(_sum_devices(late, "sum_late_grads"), [d_norm_ffn1])
               + _unpack(_sum_devices(traffic.result("gather", "small_grads"), "sum_small_grads"), early))
    last_token = reduce_over_chips("ffn1_w_in_b", reduced[1])
    RUN_AFTER.clear()
    assert not traffic.transfers and not to_sibling, (list(traffic.transfers), list(to_sibling))
    return loss_part[0, 0], grad_x, to_chips, last_token, dict(zip(SMALL_ORDER, reduced))


def _step(x, loss_target, p, m, v):
    me = 4 * lax.axis_index("x") + 2 * lax.axis_index("y") + lax.axis_index("c")

    shards = {n: (_cast_transposed if n in COLUMN_SHARDED else _cast_rows)(p[n], "cast_" + n) for n in LARGE}
    sharded_small = (jnp.pad(p["lru_conv_w"], ((0, SUBLANES - CONV_WIDTH), (0, 0)))
                     + jnp.pad(p["lru_lambda"], ((CONV_WIDTH, SUBLANES - CONV_WIDTH - 2), (0, 0))))
    s = {n: p[n] if n in ("lru_gate_w", "lru_gate_b", "attn_rpb") else p[n].reshape(1, -1) for n in REPLICATED}

    loss_part, grad_x, to_chips, last_token, small = _forward_backward(x, loss_target, shards, sharded_small, s)
    loss = lax.psum(loss_part, ("x", "y", "c"))

    def landed(n, after):
        return _split_wait(to_chips[n], after, "from_chips_" + n)[1]

    def update(n, partials):
        return (_adamw_cols if n in COLUMN_SHARDED else _adamw_rows)(p[n], partials, m[n], v[n], "adamw_" + n)

    out = {n: update(n, landed(n, [last_token])) for n in LARGE if n != "ffn1_w_in"}
    done = [o[3] for o in out.values()]
    out["ffn1_w_in"] = update("ffn1_w_in", [landed("ffn1_w_in_a", done), landed("ffn1_w_in_b", done)])

    g_small = {n: lax.dynamic_index_in_dim(g, me, axis=0, keepdims=False) if n in SHARDED_SMALL else g
               for n, g in small.items()}
    names = SMALL_ORDER
    like = [p[n] for n in names]
    pack_of = lambda d: _pack([d[n].reshape(p[n].shape) for n in names])
    upd = _adamw_small(pack_of(p), pack_of(g_small), pack_of(m), pack_of(v), "adamw_small")
    for n, d_, m_, v_ in zip(names, *[_unpack(u, like) for u in upd]):
        out[n] = (g_small[n].reshape(p[n].shape), d_, m_, v_)
    return loss, grad_x, out


def kernel(x, norm_ffn1, ffn1_w_in, ffn1_w_out, norm_mix, w_in_mix, lru_conv_w, lru_conv_b, lru_gate_w, lru_gate_b, lru_lambda, attn_rpb, lru_out_norm, attn_out_norm, w_out_mix, norm_ffn2, ffn2_w_in, ffn2_w_out, norm_final, loss_target, m_norm_ffn1, m_ffn1_w_in, m_ffn1_w_out, m_norm_mix, m_w_in_mix, m_lru_conv_w, m_lru_conv_b, m_lru_gate_w, m_lru_gate_b, m_lru_lambda, m_attn_rpb, m_lru_out_norm, m_attn_out_norm, m_w_out_mix, m_norm_ffn2, m_ffn2_w_in, m_ffn2_w_out, m_norm_final, v_norm_ffn1, v_ffn1_w_in, v_ffn1_w_out, v_norm_mix, v_w_in_mix, v_lru_conv_w, v_lru_conv_b, v_lru_gate_w, v_lru_gate_b, v_lru_lambda, v_attn_rpb, v_lru_out_norm, v_attn_out_norm, v_w_out_mix, v_norm_ffn2, v_ffn2_w_in, v_ffn2_w_out, v_norm_final):
    given = dict(locals())
    drop_layer = lambda n, a: a if n == "norm_final" else a[0]
    p = {n: drop_layer(n, given[n]) for n in WEIGHTS}
    m = {n: drop_layer(n, given["m_" + n]) for n in WEIGHTS}
    v = {n: drop_layer(n, given["v_" + n]) for n in WEIGHTS}
    loss, grad_x, out = _step(x[0], loss_target[0], p, m, v)
    shaped = lambda n, a: a.reshape(given[n].shape)
    return (loss, grad_x[None], *[shaped(n, out[n][k]) for k in range(4) for n in WEIGHTS])
```

```python
import math

import numpy as np
import jax
import jax.numpy as jnp
from jax import lax
from jax.experimental import pallas as pl
from jax.experimental.pallas import tpu as pltpu

F32 = jnp.float32
BF16 = jnp.bfloat16
SDS = jax.ShapeDtypeStruct

N_DEV = 8
N_CHIP = 4
NORM_EPS = 1e-6
RG_C = 8.0
CONV_WIDTH = 4
HEAD_DIM = 64
GRID_W = 64
WIN_ROWS = 8
WIN_COLS = 16
NEG = -1e30

ADAM_LR = 0.001
ADAM_B1 = 0.9
ADAM_B2 = 0.999
ADAM_EPS = 1e-08
ADAM_WD = 0.01
ADAM_STEP = 10

LANES = 128
SUBLANES = 8
VMEM_LIMIT = 56 * 1024 * 1024

NT = (((1,), (1,)), ((), ()))
TN = (((0,), (0,)), ((), ()))
ANY = pl.BlockSpec(memory_space=pl.ANY)
WHOLE = pl.BlockSpec(memory_space=pltpu.VMEM)
MESH = pl.DeviceIdType.MESH


def _sigmoid(x):
    return 1.0 / (1.0 + jnp.exp(-x))


def _gelu_parts(x):
    c = math.sqrt(2.0 / math.pi)
    t = jnp.tanh(c * (x + 0.044715 * (x * x * x)))
    gelu = 0.5 * x * (1.0 + t)
    dgelu = 0.5 * (1.0 + t) + 0.5 * x * (1.0 - t * t) * (c * (1.0 + 3.0 * 0.044715 * (x * x)))
    return gelu, dgelu


def _expm1(x):
    poly = x * (1.0 + x * (1.0 / 2) * (1.0 + x * (1.0 / 3) * (1.0 + x * (1.0 / 4) * (1.0 + x * (1.0 / 5) * (1.0 + x * (1.0 / 6))))))
    return jnp.where(jnp.abs(x) < 0.25, poly, jnp.exp(x) - 1.0)


def _softplus(x):
    return jnp.maximum(x, 0.0) + jnp.log1p(jnp.exp(-jnp.abs(x)))


class _Piece:
    N_REMOTE = {"gather": 7}
    N_LOCAL = {"gather": 1}

    def __init__(self, kind, src, dest, lo, hi):
        self.kind, self.src, self.dest, self.lo, self.hi = kind, src, dest, lo, hi


RELAY_AT = 60
RUN_AFTER = []


class _Job:
    def __init__(self, pieces):
        self.pieces = list(pieces)
        self.ins = [p.src for p in self.pieces if p.src is not None]
        self.out_shapes = [SDS(p.dest.shape, p.dest.dtype) for p in self.pieces]
        self.aliased = [i for i, p in enumerate(self.pieces) if not isinstance(p.dest, SDS)]
        self.n_remote = sum(_Piece.N_REMOTE[p.kind] for p in self.pieces)
        self.n_local = max(sum(_Piece.N_LOCAL[p.kind] for p in self.pieces), 1)

    def _each(self, step, ins, outs, send_sems, recv_sems, local_sems):
        remote = local = 0
        ins = iter(ins)
        for p, dst in zip(self.pieces, outs):
            src = None if p.src is None else next(ins)
            _EXCHANGES[p.kind](step, p, src, dst, send_sems, recv_sems, local_sems, remote, local)
            remote += _Piece.N_REMOTE[p.kind]
            local += _Piece.N_LOCAL[p.kind]

    def start(self, *refs):
        self._each("start", *refs)

    def relay(self, *refs):
        self._each("relay", *refs)

    def finish(self, *refs):
        self._each("finish", *refs)


def _call(body, *, name, args, out_shape, in_specs, out_specs, grid=(), scratch_shapes=(), aliases=None, job=None):
    single = not isinstance(out_shape, (tuple, list))
    out_shape = (out_shape,) if single else tuple(out_shape)
    out_specs = (out_specs,) if single else tuple(out_specs)
    aliases = dict(aliases or {})
    if RUN_AFTER:
        tokens, n_plain, plain_body = list(RUN_AFTER), len(args), body
        RUN_AFTER.clear()
        body = lambda *refs: plain_body(*refs[:n_plain], *refs[n_plain + len(tokens):])
        args, in_specs = list(args) + tokens, list(in_specs) + [ANY] * len(tokens)
    params = pltpu.CompilerParams(dimension_semantics=("arbitrary",) * len(grid) if grid else None,
                                  vmem_limit_bytes=VMEM_LIMIT)
    if job is None:
        res = pl.pallas_call(body, out_shape=out_shape, grid=grid, in_specs=list(in_specs), out_specs=out_specs,
                             scratch_shapes=list(scratch_shapes), input_output_aliases=aliases, name=name,
                             compiler_params=params)(*args)
        return res[0] if single else res

    n_in, n_out, n_scr = len(args), len(out_shape), len(scratch_shapes)
    j_in, j_out, j_alias = len(job.ins), len(job.out_shapes), len(job.aliased)

    def hosted(*refs):
        ins, refs = refs[:n_in], refs[n_in:]
        j_ins, refs = refs[:j_in], refs[j_in + j_alias:]
        outs, refs = refs[:n_out], refs[n_out:]
        j_outs, refs = refs[:j_out], refs[j_out:]
        scr, sems = refs[:n_scr], refs[n_scr:]
        if grid:
            step = 0
            for axis, size in enumerate(grid):
                step = step * size + pl.program_id(axis)
            steps = math.prod(grid)
            pl.when(step == 0)(lambda: job.start(j_ins, j_outs, *sems))
            body(*ins, *outs, *scr)
            pl.when(step == min(RELAY_AT * steps // 100, steps - 1))(lambda: job.relay(j_ins, j_outs, *sems))
            pl.when(step == steps - 1)(lambda: job.finish(j_ins, j_outs, *sems))
        else:
            job.start(j_ins, j_outs, *sems)
            body(*ins, *outs, *scr)
            job.relay(j_ins, j_outs, *sems)
            job.finish(j_ins, j_outs, *sems)

    res = pl.pallas_call(
        hosted, out_shape=out_shape + tuple(job.out_shapes), grid=grid,
        in_specs=list(in_specs) + [ANY] * (j_in + j_alias), out_specs=out_specs + (ANY,) * j_out,
        scratch_shapes=list(scratch_shapes) + [pltpu.SemaphoreType.DMA((job.n_remote,)),
                                               pltpu.SemaphoreType.DMA((job.n_remote,)),
                                               pltpu.SemaphoreType.DMA((job.n_local,))],
        input_output_aliases={**aliases, **{n_in + j_in + k: n_out + i for k, i in enumerate(job.aliased)}},
        name=name, compiler_params=params)(*args, *job.ins, *[job.pieces[i].dest for i in job.aliased])
    own, carried = res[:n_out], res[n_out:]
    return (own[0] if single else own), carried


def _run_job(job, name):
    return _call(lambda: None, name=name, args=[], out_shape=(), in_specs=[], out_specs=(), job=job)[1]


def _position():
    return lax.axis_index("x"), lax.axis_index("y"), lax.axis_index("c")


def _flat(px, py, pc):
    return 4 * px + 2 * py + pc


def _gather_exchange(step, p, src, dst, send_sems, recv_sems, local_sems, r0, l0):
    x, y, c = _position()
    me, sibling = (x, y, c), (x, y, 1 - c)
    along_x, along_y, diagonal = (1 - x, y), (x, 1 - y), (1 - x, 1 - y)
    south = c == 0
    passed_on = (jnp.where(south, 1 - x, x), jnp.where(south, y, 1 - y))
    passed_to = (jnp.where(south, x, 1 - x), jnp.where(south, 1 - y, y))
    placed = p.src is None
    rb, n_rows = p.dest.shape[0] // N_DEV, p.hi - p.lo

    def rows(block):
        return dst.at[pl.ds(_flat(*block) * rb + p.lo, n_rows), :]

    mine = rows(me) if placed else src.at[pl.ds(p.lo, n_rows), :]

    def copy(k, block, to, own=False):
        return pltpu.make_async_remote_copy(
            src_ref=mine if own else rows(block), dst_ref=rows(block),
            send_sem=send_sems.at[r0 + k], recv_sem=recv_sems.at[r0 + k], device_id=to, device_id_type=MESH)

    local = None if placed else pltpu.make_async_copy(mine, rows(me), local_sems.at[l0])
    if step == "start":
        if local is not None:
            local.start()
        copy(0, me, sibling, own=True).start()
        copy(1, me, (*along_x, c), own=True).start()
        copy(2, me, (*along_y, c), own=True).start()
    elif step == "relay":
        copy(1, (*along_x, c), me).wait_recv()
        copy(2, (*along_y, c), me).wait_recv()
        copy(3, (*passed_on, c), (*passed_to, c)).start()
        copy(4, (*along_x, c), sibling).start()
        copy(5, (*along_y, c), sibling).start()
    else:
        copy(3, (*diagonal, c), me).wait_recv()
        copy(6, (*diagonal, c), sibling).start()
        copy(0, sibling, me).wait_recv()
        copy(4, (*along_x, 1 - c), me).wait_recv()
        copy(5, (*along_y, 1 - c), me).wait_recv()
        copy(6, (*diagonal, 1 - c), me).wait_recv()
        copy(0, me, sibling, own=True).wait_send()
        copy(1, me, (*along_x, c), own=True).wait_send()
        copy(2, me, (*along_y, c), own=True).wait_send()
        copy(3, (*passed_on, c), (*passed_to, c)).wait_send()
        copy(4, (*along_x, c), sibling).wait_send()
        copy(5, (*along_y, c), sibling).wait_send()
        copy(6, (*diagonal, c), sibling).wait_send()
        if local is not None:
            local.wait()


CHIP_FLIPS = [(1, 0), (0, 1), (1, 1)]
_EXCHANGES = {"gather": _gather_exchange}


def _gathered(shard):
    return SDS((N_DEV * shard.shape[0], shard.shape[1]), shard.dtype)


def _split(rows, parts):
    cuts = [rows * k // parts // 16 * 16 for k in range(parts)] + [rows]
    return list(zip(cuts[:-1], cuts[1:]))


def _pair_sum(g, from_sibling, name):
    rb, n = g.shape[0] // N_DEV, g.shape[1]
    tr = rb if rb * n * 2 <= 3 * 1024 * 1024 else rb // 2
    core = lax.axis_index("c").astype(jnp.int32).reshape(1)

    def body(c_ref, g_ref, r_ref, o_ref):
        o_ref[...] = (g_ref[...].astype(F32) + r_ref[...].astype(F32)).astype(BF16)

    grid_spec = pltpu.PrefetchScalarGridSpec(
        num_scalar_prefetch=1, grid=(N_CHIP, rb // tr),
        in_specs=[pl.BlockSpec((None, None, tr, n), lambda q, i, c_ref: (q, c_ref[0], i, 0)),
                  pl.BlockSpec((None, tr, n), lambda q, i, c_ref: (q, i, 0))],
        out_specs=pl.BlockSpec((None, tr, n), lambda q, i, c_ref: (q, i, 0)))
    out = pl.pallas_call(
        body, grid_spec=grid_spec, out_shape=SDS((N_CHIP, rb, n), BF16), name=name,
        compiler_params=pltpu.CompilerParams(dimension_semantics=("arbitrary",) * 2, vmem_limit_bytes=VMEM_LIMIT))(
            core, g.reshape(N_CHIP, 2, rb, n), from_sibling.reshape(N_CHIP, rb, n))
    return out.reshape(N_CHIP * rb, n)


SEM = pl.BlockSpec(memory_space=pltpu.SEMAPHORE)
IN_HBM = pl.BlockSpec(memory_space=pltpu.HBM)
SIDE_EFFECT = pltpu.SideEffectType.DATAFLOW_SIDE_EFFECTING


def _own_slot(partials, name):
    rb, n = partials.shape[0] // N_CHIP, partials.shape[1]
    tr = rb // 2
    chip = (2 * lax.axis_index("x") + lax.axis_index("y")).astype(jnp.int32).reshape(1)

    def body(chip_ref, src_ref, dst_ref):
        dst_ref[...] = src_ref[...]

    block = pl.BlockSpec((None, tr, n), lambda i, chip_ref: (chip_ref[0], i, 0))
    grid_spec = pltpu.PrefetchScalarGridSpec(num_scalar_prefetch=1, grid=(rb // tr,), in_specs=[block], out_specs=block)
    out = pl.pallas_call(
        body, grid_spec=grid_spec, out_shape=SDS((N_CHIP, rb, n), partials.dtype), name=name,
        compiler_params=pltpu.CompilerParams(dimension_semantics=("arbitrary",), vmem_limit_bytes=VMEM_LIMIT))(
            chip, partials.reshape(N_CHIP, rb, n))
    return out.reshape(partials.shape)


def _blank_like(src, rows, name):
    return pl.pallas_call(lambda src_ref, out_ref: None, out_shape=SDS((rows, src.shape[1]), src.dtype),
                          in_specs=[ANY], out_specs=ANY, name=name)(src)


def _chip_copies(src_ref, land_ref, sems):
    x, y, c = _position()
    rb = src_ref.shape[0] // N_CHIP
    copies = []
    for k, (fx, fy) in enumerate(CHIP_FLIPS):
        px, py = (1 - x if fx else x), (1 - y if fy else y)
        copies.append(pltpu.make_async_remote_copy(
            src_ref=src_ref.at[pl.ds((2 * px + py) * rb, rb), :], dst_ref=land_ref.at[pl.ds((2 * x + y) * rb, rb), :],
            send_sem=sems[2 * k], recv_sem=sems[2 * k + 1], device_id=(px, py, c), device_id_type=MESH))
    return copies


def _sibling_copies(src_ref, land_ref, sems):
    x, y, c = _position()
    rb = src_ref.shape[0] // N_DEV
    return [pltpu.make_async_remote_copy(
        src_ref=src_ref.at[pl.ds((2 * q + 1 - c) * rb, rb), :], dst_ref=land_ref.at[pl.ds(q * rb, rb), :],
        send_sem=sems[2 * q], recv_sem=sems[2 * q + 1], device_id=(x, y, 1 - c), device_id_type=MESH)
        for q in range(N_CHIP)]


SPLIT_COPIES = {"to_chips": (_chip_copies, 3), "to_sibling": (_sibling_copies, N_CHIP)}


def _split_start(kind, src, land, name):
    copies_of, n_copies = SPLIT_COPIES[kind]

    def body(src_ref, land_ref, *rest):
        sems, token = rest[:2 * n_copies], rest[-1]
        for copy in copies_of(src_ref, land_ref, sems):
            copy.start()
        token[...] = jnp.zeros_like(token)

    res = pl.pallas_call(
        body, name=name,
        out_shape=(pltpu.SemaphoreType.DMA(()),) * (2 * n_copies)
        + (pltpu.HBM(src.shape, src.dtype), pltpu.HBM(land.shape, land.dtype), SDS((SUBLANES, LANES), F32)),
        in_specs=(IN_HBM, IN_HBM), out_specs=(SEM,) * (2 * n_copies) + (IN_HBM, IN_HBM, WHOLE),
        input_output_aliases={0: 2 * n_copies, 1: 2 * n_copies + 1},
        compiler_params=pltpu.CompilerParams(has_side_effects=SIDE_EFFECT))(
            pltpu.with_memory_space_constraint(src, pltpu.HBM), pltpu.with_memory_space_constraint(land, pltpu.HBM))
    return (kind, res[:2 * n_copies], res[-3], res[-2]), res[-1]


def _split_wait(pending, after, name):
    kind, sems, src, land = pending
    copies_of, n_copies = SPLIT_COPIES[kind]

    def body(src_ref, land_ref, *rest):
        for copy in copies_of(src_ref, land_ref, rest[:2 * n_copies]):
            copy.wait_send()
            copy.wait_recv()

    return pl.pallas_call(
        body, name=name, out_shape=(pltpu.HBM(src.shape, src.dtype), pltpu.HBM(land.shape, land.dtype)),
        in_specs=(IN_HBM, IN_HBM) + (SEM,) * (2 * n_copies) + (ANY,) * len(after), out_specs=(IN_HBM, IN_HBM),
        input_output_aliases={0: 0, 1: 1},
        compiler_params=pltpu.CompilerParams(has_side_effects=SIDE_EFFECT))(src, land, *sems, *after)


def _sum_devices(gathered, name):
    r = gathered.shape[0] // N_DEV

    def body(g_ref, o_ref):
        acc = g_ref[0]
        for s in range(1, N_DEV):
            acc = acc + g_ref[s]
        o_ref[...] = acc

    return _call(body, name=name, args=[gathered.reshape(N_DEV, r, LANES)], out_shape=SDS((r, LANES), F32),
                 in_specs=[WHOLE], out_specs=WHOLE)


def _cast_into_place(w, transposed, name):
    me = _flat(*_position()).astype(jnp.int32).reshape(1)
    if transposed:
        d, rb = w.shape
        td = 512
        grid = (d // td,)
        in_spec = pl.BlockSpec((td, rb), lambda i, me_ref: (i, 0))
        out_spec = pl.BlockSpec((rb, td), lambda i, me_ref: (me_ref[0], i))
    else:
        rb, d = w.shape
        grid = (1,)
        in_spec = pl.BlockSpec((rb, d), lambda i, me_ref: (0, 0))
        out_spec = pl.BlockSpec((rb, d), lambda i, me_ref: (me_ref[0], 0))

    def body(me_ref, w_ref, o_ref):
        value = w_ref[...]
        o_ref[...] = (value.T if transposed else value).astype(BF16)

    grid_spec = pltpu.PrefetchScalarGridSpec(num_scalar_prefetch=1, grid=grid, in_specs=[in_spec], out_specs=out_spec)
    return pl.pallas_call(
        body, grid_spec=grid_spec, out_shape=SDS((N_DEV * rb, d), BF16), name=name,
        compiler_params=pltpu.CompilerParams(dimension_semantics=("arbitrary",), vmem_limit_bytes=VMEM_LIMIT))(me, w)


ROW_TILE = 256


def _rmsnorm_fwd(h, gain, name):
    t, d = h.shape

    def body(h_ref, g_ref, u_ref):
        x = h_ref[...]
        u_ref[...] = (x * lax.rsqrt(jnp.mean(x * x, axis=-1, keepdims=True) + NORM_EPS) * g_ref[...]).astype(BF16)

    row = pl.BlockSpec((ROW_TILE, d), lambda i: (i, 0))
    return _call(body, name=name, args=[h, gain], out_shape=SDS((t, d), BF16), grid=(t // ROW_TILE,),
                 in_specs=[row, pl.BlockSpec((1, d), lambda i: (0, 0))], out_specs=row)


def _rms_bwd_math(x, gain, dy):
    rstd = lax.rsqrt(jnp.mean(x * x, axis=-1, keepdims=True) + NORM_EPS)
    xhat = x * rstd
    dxh = dy * gain
    dx = rstd * (dxh - xhat * jnp.mean(dxh * xhat, axis=-1, keepdims=True))
    return dx, jnp.sum(dy * xhat, axis=0, keepdims=True)


def _rmsnorm_bwd(du, h, gain, resid, bf_scale, name, job=None):
    t, d = h.shape

    def body(du_ref, h_ref, g_ref, r_ref, dh_ref, dhb_ref, dg_ref):
        @pl.when(pl.program_id(0) == 0)
        def _():
            dg_ref[...] = jnp.zeros_like(dg_ref)

        dx, dg = _rms_bwd_math(h_ref[...], g_ref[...], du_ref[...])
        dh = r_ref[...] + dx
        dh_ref[...] = dh
        dhb_ref[...] = (bf_scale * dh).astype(BF16)
        dg_ref[...] += dg

    row = pl.BlockSpec((ROW_TILE, d), lambda i: (i, 0))
    vec = pl.BlockSpec((1, d), lambda i: (0, 0))
    return _call(body, name=name, args=[du, h, gain, resid],
                 out_shape=(SDS((t, d), F32), SDS((t, d), BF16), SDS((1, d), F32)), grid=(t // ROW_TILE,),
                 in_specs=[row, row, vec, row], out_specs=(row, row, vec), job=job)


def _final_loss(h, gain, target, name):
    t, d = h.shape

    def body(h_ref, g_ref, t_ref, dh_ref, dhb_ref, loss_ref, dg_ref):
        @pl.when(pl.program_id(0) == 0)
        def _():
            dg_ref[...] = jnp.zeros_like(dg_ref)
            loss_ref[...] = jnp.zeros_like(loss_ref)

        x = h_ref[...]
        gain = g_ref[...]
        out = x * lax.rsqrt(jnp.mean(x * x, axis=-1, keepdims=True) + NORM_EPS) * gain
        err = out - t_ref[...]
        loss_ref[...] += 0.5 * jnp.sum(jnp.mean(err * err, axis=-1, keepdims=True), axis=0, keepdims=True)
        dx, dg = _rms_bwd_math(x, gain, err * (1.0 / d))
        dh_ref[...] = dx
        dhb_ref[...] = (0.5 * dx).astype(BF16)
        dg_ref[...] += dg

    row = pl.BlockSpec((ROW_TILE, d), lambda i: (i, 0))
    vec = pl.BlockSpec((1, d), lambda i: (0, 0))
    one = pl.BlockSpec((SUBLANES, LANES), lambda i: (0, 0))
    return _call(body, name=name, args=[h, gain, target],
                 out_shape=(SDS((t, d), F32), SDS((t, d), BF16), SDS((SUBLANES, LANES), F32), SDS((1, d), F32)),
                 grid=(t // ROW_TILE,), in_specs=[row, vec, row], out_specs=(row, row, one, vec))


def _mixnorm_fwd(ya, yb, ga, gb, name):
    t, c = ya.shape

    def body(ya_ref, yb_ref, ga_ref, gb_ref, y_ref, yt_ref):
        for k, (src, g_ref) in enumerate(((ya_ref, ga_ref), (yb_ref, gb_ref))):
            x = src[...]
            u = x * lax.rsqrt(jnp.mean(x * x, axis=-1, keepdims=True) + NORM_EPS) * g_ref[...]
            y_ref[:, k * c:(k + 1) * c] = u.astype(BF16)
            yt_ref[k * c:(k + 1) * c, :] = u.T.astype(BF16)

    row = pl.BlockSpec((ROW_TILE, c), lambda i: (i, 0))
    vec = pl.BlockSpec((1, c), lambda i: (0, 0))
    return _call(body, name=name, args=[ya, yb, ga, gb],
                 out_shape=(SDS((t, 2 * c), BF16), SDS((2 * c, t), BF16)), grid=(t // ROW_TILE,),
                 in_specs=[row, row, vec, vec],
                 out_specs=(pl.BlockSpec((ROW_TILE, 2 * c), lambda i: (i, 0)),
                            pl.BlockSpec((2 * c, ROW_TILE), lambda i: (0, i))))


def _mixnorm_bwd(dy, ya, yb, ga, gb, name):
    t, c = ya.shape

    def body(dy_ref, ya_ref, yb_ref, ga_ref, gb_ref, dya_ref, dyb_ref, dga_ref, dgb_ref):
        @pl.when(pl.program_id(0) == 0)
        def _():
            dga_ref[...] = jnp.zeros_like(dga_ref)
            dgb_ref[...] = jnp.zeros_like(dgb_ref)

        dxa, dga = _rms_bwd_math(ya_ref[...], ga_ref[...], dy_ref[:, :c])
        dxb, dgb = _rms_bwd_math(yb_ref[...], gb_ref[...], dy_ref[:, c:])
        dya_ref[...] = dxa
        dyb_ref[...] = dxb
        dga_ref[...] += dga
        dgb_ref[...] += dgb

    row = pl.BlockSpec((ROW_TILE, c), lambda i: (i, 0))
    vec = pl.BlockSpec((1, c), lambda i: (0, 0))
    return _call(body, name=name, args=[dy, ya, yb, ga, gb],
                 out_shape=(SDS((t, c), F32), SDS((t, c), F32), SDS((1, c), F32), SDS((1, c), F32)),
                 grid=(t // ROW_TILE,),
                 in_specs=[pl.BlockSpec((ROW_TILE, 2 * c), lambda i: (i, 0)), row, row, vec, vec],
                 out_specs=(row, row, vec, vec))


def _tile(n, want):
    return max(t for t in range(LANES, min(n, want) + 1, LANES) if n % t == 0)


def _mm(a, b, *, nt, out_dtype, tm, tn, name, residual=None, scale=None, take=None, out_rows=None, row_offset=0,
        into=None, job=None):
    parts = list(a) if isinstance(a, (list, tuple)) else [a]
    widths = [p.shape[-1] for p in parts]
    k = sum(widths)
    n = b.shape[0] if nt else b.shape[1]
    if take is None:
        m, which = parts[0].shape[0], lambda i: i
        tm = _tile(math.gcd(m, row_offset), tm)
    else:
        tm, tiles, which = take
        m = tm * tiles
    tn = _tile(n, tn)
    out_rows = m if out_rows is None else out_rows

    def body(*refs):
        a_refs, b_ref, rest = refs[:len(parts)], refs[len(parts)], refs[len(parts) + 1:]
        o_ref = rest[-1]
        out, at = None, 0
        for a_ref, width in zip(a_refs, widths):
            av = a_ref[...].astype(BF16)
            if nt:
                term = lax.dot_general(av, b_ref[:, at:at + width].astype(BF16), NT, preferred_element_type=F32)
            else:
                term = jnp.dot(av, b_ref[at:at + width, :].astype(BF16), preferred_element_type=F32)
            out = term if out is None else out + term
            at += width
        if residual is not None:
            out = rest[0][...] + (out if scale is None else scale * out)
        o_ref[...] = out.astype(out_dtype)

    a_specs = [pl.BlockSpec((tm, width), lambda i, j: (which(i), 0)) for width in widths]
    in_specs = a_specs + [pl.BlockSpec((tn, k), lambda i, j: (j, 0)) if nt else pl.BlockSpec((k, tn), lambda i, j: (0, j))]
    args, aliases = parts + [b], {}
    if residual is not None:
        in_specs.append(pl.BlockSpec((tm, tn), lambda i, j: (i, j)))
        args.append(residual)
    if into is not None:
        in_specs.append(ANY)
        aliases[len(args)] = 0
        args.append(into)
    return _call(body, name=name, args=args, out_shape=SDS((out_rows, n), out_dtype), grid=(m // tm, n // tn),
                 in_specs=in_specs, out_specs=pl.BlockSpec((tm, tn), lambda i, j: (row_offset // tm + i, j)),
                 aliases=aliases, job=job)


FFN_TM = 512
FFN_HB = 512


def _ffn_hidden(u, w_in_t, name, job=None):
    t, d = u.shape
    f = w_in_t.shape[0] // 2

    def body(u_ref, w_ref, g_ref, up_ref, hid_ref, hid_t_ref):
        uu = u_ref[...]
        g = lax.dot_general(uu, w_ref[0], NT, preferred_element_type=F32)
        up = lax.dot_general(uu, w_ref[1], NT, preferred_element_type=F32)
        g_ref[...] = g.astype(BF16)
        up_ref[...] = up.astype(BF16)
        hid = (g * _sigmoid(g)) * up
        hid_ref[...] = hid.astype(BF16)
        hid_t_ref[...] = hid.T.astype(BF16)

    pre = pl.BlockSpec((FFN_TM, FFN_HB), lambda i, k: (i, k))
    return _call(body, name=name, args=[u, w_in_t.reshape(2, f, d)],
                 out_shape=(SDS((t, f), BF16), SDS((t, f), BF16), SDS((t, f), BF16), SDS((f, t), BF16)),
                 grid=(t // FFN_TM, f // FFN_HB),
                 in_specs=[pl.BlockSpec((FFN_TM, d), lambda i, k: (i, 0)),
                           pl.BlockSpec((2, FFN_HB, d), lambda i, k: (0, k, 0))],
                 out_specs=(pre, pre, pre, pl.BlockSpec((FFN_HB, FFN_TM), lambda i, k: (k, i))), job=job)


def _ffn_bwd(dfb, gpre, upre, w_in_t, w_out, name, job=None):
    t, d = dfb.shape
    f = w_out.shape[0]
    nk = f // FFN_HB

    def body(df_ref, g_ref, up_ref, w_ref, wo_ref, du_ref, da_t_ref, acc):
        k = pl.program_id(1)

        @pl.when(k == 0)
        def _():
            acc[...] = jnp.zeros_like(acc)

        dhid = lax.dot_general(df_ref[...], wo_ref[...], NT, preferred_element_type=F32)
        g, up = g_ref[...].astype(F32), up_ref[...].astype(F32)
        sig = _sigmoid(g)
        silu = g * sig
        dup = dhid * silu
        dg = dhid * up * (sig * (1.0 + g * (1.0 - sig)))
        da_t_ref[0] = dg.T.astype(BF16)
        da_t_ref[1] = dup.T.astype(BF16)
        acc[...] += (jnp.dot(dg.astype(BF16), w_ref[0], preferred_element_type=F32)
                     + jnp.dot(dup.astype(BF16), w_ref[1], preferred_element_type=F32))

        @pl.when(k == nk - 1)
        def _():
            du_ref[...] = acc[...]

    tok = pl.BlockSpec((FFN_TM, d), lambda i, k: (i, 0))
    pre = pl.BlockSpec((FFN_TM, FFN_HB), lambda i, k: (i, k))
    return _call(body, name=name, args=[dfb, gpre, upre, w_in_t.reshape(2, f, d), w_out],
                 out_shape=(SDS((t, d), F32), SDS((2, f, t), BF16)), grid=(t // FFN_TM, nk),
                 in_specs=[tok, pre, pre, pl.BlockSpec((2, FFN_HB, d), lambda i, k: (0, k, 0)),
                           pl.BlockSpec((FFN_HB, d), lambda i, k: (k, 0))],
                 out_specs=(tok, pl.BlockSpec((2, FFN_HB, FFN_TM), lambda i, k: (0, k, i))),
                 scratch_shapes=[pltpu.VMEM((FFN_TM, d), F32)], job=job)


CH = LANES
PAD = SUBLANES


def _lru_gates(xc, gw_ref, gb_ref, lam_ref, z):
    xcb = xc.astype(BF16)
    r = _sigmoid(jnp.dot(xcb, gw_ref[2 * z], preferred_element_type=F32) + gb_ref[pl.ds(2 * z, 1), :])
    i = _sigmoid(jnp.dot(xcb, gw_ref[2 * z + 1], preferred_element_type=F32) + gb_ref[pl.ds(2 * z + 1, 1), :])
    sp = _softplus(-lam_ref[pl.ds(z, 1), :])
    log_a = (-RG_C * r) * sp
    a = jnp.exp(log_a)
    mult = jnp.sqrt(-_expm1(2.0 * log_a))
    return r, i, sp, a, mult


def _conv(xpad, cw_ref, cb_ref, t):
    xc = cb_ref[...] + cw_ref[pl.ds(0, 1), :] * xpad[pl.ds(PAD - 2, t), :]
    for j in range(1, CONV_WIDTH):
        xc = xc + cw_ref[pl.ds(j, 1), :] * xpad[pl.ds(PAD - 2 + j, t), :]
    return xc


def _fill_padded(pad_ref, value, t):
    pad_ref[pl.ds(0, PAD), :] = jnp.zeros((PAD, CH), F32)
    pad_ref[pl.ds(PAD + t, PAD), :] = jnp.zeros((PAD, CH), F32)
    pad_ref[pl.ds(PAD, t), :] = value


def _scan_pair(t, a_up, b_up, out_up, a_down, b_down, out_down):
    row = lax.broadcasted_iota(jnp.int32, (SUBLANES, CH), 0)

    def compose(a, b, rising):
        for dist in (1, 2, 4):
            shift = dist if rising else SUBLANES - dist
            keep = (row >= dist) if rising else (row < SUBLANES - dist)
            b = jnp.where(keep, b + a * pltpu.roll(b, shift, axis=0), b)
            a = jnp.where(keep, a * pltpu.roll(a, shift, axis=0), a)
        return a, b

    def step(tt, carry):
        hu, hd = carry
        lo = pl.ds(pl.multiple_of(tt * SUBLANES, SUBLANES), SUBLANES)
        hi = pl.ds(pl.multiple_of(t - SUBLANES - tt * SUBLANES, SUBLANES), SUBLANES)
        a, b = compose(a_up[lo, :], b_up[lo, :], True)
        up = b + a * hu
        out_up[lo, :] = up
        a, b = compose(a_down[hi, :], b_down[hi, :], False)
        down = b + a * hd
        out_down[hi, :] = down
        return up[SUBLANES - 1:, :], down[:1, :]

    zero = jnp.zeros((1, CH), F32)
    lax.fori_loop(0, t // SUBLANES, step, (zero, zero), unroll=2)


def _lru_fwd(proj, cw, cb, gw, gb, lam, name, job=None):
    t = proj.shape[0]
    c = cw.shape[1]
    ncb = c // CH

    def body(x_ref, g_ref, cw_ref, cb_ref, gw_ref, gb_ref, lam_ref, ya_ref, hf_ref, hb_ref, xpad, a0, b0, a1, b1):
        _fill_padded(xpad, x_ref[...], t)
        xc = _conv(xpad, cw_ref, cb_ref, t)
        for z, (a_s, b_s) in enumerate(((a0, b0), (a1, b1))):
            _, i, _, a, mult = _lru_gates(xc, gw_ref, gb_ref, lam_ref, z)
            a_s[...] = a
            b_s[...] = mult * (i * xc)
        _scan_pair(t, a0, b0, hf_ref, a1, b1, hb_ref)
        gelu, _ = _gelu_parts(g_ref[...])
        ya_ref[...] = gelu * (hf_ref[...] + hb_ref[...])

    col = lambda off: pl.BlockSpec((t, CH), lambda i: (0, off + i))
    small = lambda rows: pl.BlockSpec((rows, CH), lambda i: (0, i))
    return _call(body, name=name, args=[proj, proj, cw, cb, gw, gb, lam], out_shape=(SDS((t, c), F32),) * 3,
                 grid=(ncb,),
                 in_specs=[col(0), col(ncb), small(CONV_WIDTH), small(1),
                           pl.BlockSpec((4, None, CH, CH), lambda i: (0, i, 0, 0)), small(4), small(2)],
                 out_specs=(col(0),) * 3,
                 scratch_shapes=[pltpu.VMEM((t + 2 * PAD, CH), F32)] + [pltpu.VMEM((t, CH), F32)] * 4, job=job)


def _lru_bwd(proj, cw, cb, gw, gb, lam, hf, hb, dya, name, job=None):
    t = proj.shape[0]
    c = cw.shape[1]
    ncb = c // CH

    def body(x_ref, g_ref, cw_ref, cb_ref, gw_ref, gb_ref, lam_ref, hf_ref, hb_ref, dya_ref,
             dx_ref, dg_ref, dt_ref, dcw_ref, dcb_ref, dgw_ref, dgb_ref, dlam_ref,
             xpad, hpad, dxc, a0, a1, dhs, dh0, dh1):
        _fill_padded(xpad, x_ref[...], t)
        xc = _conv(xpad, cw_ref, cb_ref, t)
        xcb = xc.astype(BF16)
        gates = [_lru_gates(xc, gw_ref, gb_ref, lam_ref, z) for z in range(2)]

        gelu, dgelu = _gelu_parts(g_ref[...])
        dya = dya_ref[...]
        dgate = dya * (hf_ref[...] + hb_ref[...]) * dgelu
        dg_ref[...] = dgate.astype(BF16)
        dt_ref[1] = dgate.T.astype(BF16)
        dhs[...] = dya * gelu

        _fill_padded(hpad, gates[0][3], t)
        a0[...] = hpad[pl.ds(PAD + 1, t), :]
        _fill_padded(hpad, gates[1][3], t)
        a1[...] = hpad[pl.ds(PAD - 1, t), :]
        _scan_pair(t, a1, dhs, dh1, a0, dhs, dh0)

        acc_dxc = jnp.zeros((t, CH), F32)
        for z, (h_ref, dh_ref, shift) in enumerate(((hf_ref, dh0, -1), (hb_ref, dh1, 1))):
            r, i, sp, a, mult = gates[z]
            _fill_padded(hpad, h_ref[...], t)
            h_nb = hpad[pl.ds(PAD + shift, t), :]
            db = dh_ref[...]
            da = db * h_nb
            d_i = db * mult * xc
            acc_dxc = acc_dxc + db * mult * i
            d_mult = db * i * xc
            d_la = da * a - d_mult * (a * a) / mult
            d_r = d_la * (-RG_C * sp)
            dlam_ref[pl.ds(z, 1), :] = (jnp.sum(d_la * (-RG_C * r), axis=0, keepdims=True)
                                        * (-_sigmoid(-lam_ref[pl.ds(z, 1), :])))
            for gate, d_pre in ((0, d_r * r * (1.0 - r)), (1, d_i * i * (1.0 - i))):
                zg = 2 * z + gate
                dgb_ref[pl.ds(zg, 1), :] = jnp.sum(d_pre, axis=0, keepdims=True)
                d_pre_b = d_pre.astype(BF16)
                dgw_ref[zg] = lax.dot_general(xcb, d_pre_b, TN, preferred_element_type=F32)
                acc_dxc = acc_dxc + lax.dot_general(d_pre_b, gw_ref[zg], NT, preferred_element_type=F32)

        dcb_ref[...] = jnp.sum(acc_dxc, axis=0, keepdims=True)
        for j in range(CONV_WIDTH):
            dcw_ref[pl.ds(j, 1), :] = jnp.sum(acc_dxc * xpad[pl.ds(PAD - 2 + j, t), :], axis=0, keepdims=True)
        _fill_padded(dxc, acc_dxc, t)
        dx = cw_ref[pl.ds(0, 1), :] * dxc[pl.ds(PAD + 2, t), :]
        for j in range(1, CONV_WIDTH):
            dx = dx + cw_ref[pl.ds(j, 1), :] * dxc[pl.ds(PAD + 2 - j, t), :]
        dx_ref[...] = dx.astype(BF16)
        dt_ref[0] = dx.T.astype(BF16)

    col = lambda off: pl.BlockSpec((t, CH), lambda i: (0, off + i))
    small = lambda rows: pl.BlockSpec((rows, CH), lambda i: (0, i))
    dense = pl.BlockSpec((4, None, CH, CH), lambda i: (0, i, 0, 0))
    padded = pltpu.VMEM((t + 2 * PAD, CH), F32)
    return _call(
        body, name=name, args=[proj, proj, cw, cb, gw, gb, lam, hf, hb, dya],
        out_shape=(SDS((t, c), BF16), SDS((t, c), BF16), SDS((2, c, t), BF16), SDS((CONV_WIDTH, c), F32),
                   SDS((1, c), F32), SDS((4, ncb, CH, CH), F32), SDS((4, c), F32), SDS((2, c), F32)),
        grid=(ncb,),
        in_specs=[col(0), col(ncb), small(CONV_WIDTH), small(1), dense, small(4), small(2), col(0), col(0), col(0)],
        out_specs=(col(0), col(0), pl.BlockSpec((2, CH, t), lambda i: (0, i, 0)), small(CONV_WIDTH), small(1),
                   dense, small(4), small(2)),
        scratch_shapes=[padded, padded, padded] + [pltpu.VMEM((t, CH), F32)] * 5, job=job)


Q_ROWS = 4
BAND_ROWS = WIN_ROWS + Q_ROWS
BAND_PAIRS = BAND_ROWS // 2
Q_BLOCK = Q_ROWS * GRID_W
BAND = BAND_ROWS * GRID_W
PAIR_W = 2 * GRID_W
N_BOTH = 2 * WIN_ROWS - 2
ENTRY_LEFT_OUT, ENTRY_RIGHT_OUT, ENTRY_OUT = N_BOTH, N_BOTH + 1, N_BOTH + 2
N_ENTRIES = N_BOTH + 3


def _bias_tables(rpb):
    cols = np.arange(GRID_W)
    start = np.clip(cols - WIN_COLS // 2, 0, GRID_W - WIN_COLS)
    valid = (cols[None, :] >= start[:, None]) & (cols[None, :] < start[:, None] + WIN_COLS)
    col_off = np.clip(cols[None, :] - cols[:, None] + WIN_COLS - 1, 0, 2 * WIN_COLS - 2)
    pick_col = jnp.asarray(np.eye(2 * WIN_COLS - 1, dtype=np.float32)[col_off] * valid[..., None])
    by_row = jnp.einsum("hrc,qkc->hrqk", rpb, pick_col, precision=lax.Precision.HIGHEST)
    by_row = jnp.where(jnp.asarray(valid)[None, None], by_row, NEG)
    out = jnp.full_like(by_row[:, :1], NEG)
    first_in, last_in = WIN_ROWS - 1 - WIN_ROWS // 2, 2 * (WIN_ROWS - 1) - WIN_ROWS // 2
    both = jnp.concatenate([by_row[:, :-1], by_row[:, 1:]], axis=-1)
    left_out = jnp.concatenate([out, by_row[:, first_in:first_in + 1]], axis=-1)
    right_out = jnp.concatenate([by_row[:, last_in:last_in + 1], out], axis=-1)
    return jnp.concatenate([both, left_out, right_out, jnp.concatenate([out, out], axis=-1)], axis=1)


def _band_start(m, rows):
    return jnp.clip(Q_ROWS * m - WIN_ROWS // 2, 0, rows - BAND_ROWS)


def _entry(r, key_row, rows):
    w0 = jnp.clip(r - WIN_ROWS // 2, 0, rows - WIN_ROWS)
    left = (key_row >= w0) & (key_row < w0 + WIN_ROWS)
    right = (key_row + 1 >= w0) & (key_row + 1 < w0 + WIN_ROWS)
    return jnp.where(left & right, key_row - r + WIN_ROWS - 1,
                     jnp.where(right, ENTRY_LEFT_OUT, jnp.where(left, ENTRY_RIGHT_OUT, ENTRY_OUT)))


def _transposed_pairs(dst, src_ref):
    for g in range(dst.shape[0]):
        dst[g] = src_ref[pl.ds(g * PAIR_W, PAIR_W), :].T.astype(BF16)


def _band_of(pairs_ref, first_pair, hh):
    heads = pl.ds(hh * HEAD_DIM, HEAD_DIM)
    return jnp.concatenate([pairs_ref[first_pair + g, heads, :] for g in range(BAND_PAIRS)], axis=1)


def _attn_block(qs, kt, tz_ref, hh, m, rows):
    rs = _band_start(m, rows)
    lanes = pl.ds(hh * HEAD_DIM, HEAD_DIM)
    qrows = pl.ds(pl.multiple_of(m * Q_BLOCK, Q_BLOCK), Q_BLOCK)
    band = pl.ds(pl.multiple_of(rs * GRID_W, PAIR_W), BAND)
    entries = [[_entry(Q_ROWS * m + i, rs + 2 * g, rows) for g in range(BAND_PAIRS)] for i in range(Q_ROWS)]
    bias = jnp.concatenate([jnp.concatenate([tz_ref[hh, e] for e in row], axis=1) for row in entries], axis=0)
    q = qs[qrows, lanes]
    s = jnp.dot(q, _band_of(kt, rs // 2, hh), preferred_element_type=F32) * (HEAD_DIM ** -0.5) + bias
    p = jnp.exp(s - jnp.max(s, axis=-1, keepdims=True))
    p = p / jnp.sum(p, axis=-1, keepdims=True)
    return q, p, qrows, band, lanes, entries, rs // 2


def _attn_fwd(proj, tables, width, name, job=None):
    t = proj.shape[0]
    rows = t // GRID_W
    npair = width // LANES
    first = (proj.shape[1] - 3 * width) // LANES

    def body(q_ref, k_ref, v_ref, tz_ref, o_ref, qs, vs, kt):
        qs[...] = q_ref[...].astype(BF16)
        vs[...] = v_ref[...].astype(BF16)
        _transposed_pairs(kt, k_ref)

        def block(m, carry):
            for hh in range(2):
                _, p, qrows, band, lanes, _, _ = _attn_block(qs, kt, tz_ref, hh, m, rows)
                o_ref[qrows, lanes] = jnp.dot(p.astype(BF16), vs[band, lanes], preferred_element_type=F32)
            return carry

        lax.fori_loop(0, rows // Q_ROWS, block, 0, unroll=2)

    col = lambda off: pl.BlockSpec((t, LANES), lambda i: (0, off + i))
    return _call(body, name=name, args=[proj, proj, proj, tables], out_shape=SDS((t, width), F32), grid=(npair,),
                 in_specs=[col(first), col(first + npair), col(first + 2 * npair),
                           pl.BlockSpec((2, N_ENTRIES, GRID_W, PAIR_W), lambda i: (i, 0, 0, 0))],
                 out_specs=col(0),
                 scratch_shapes=[pltpu.VMEM((t, LANES), BF16)] * 2 + [pltpu.VMEM((t // PAIR_W, LANES, PAIR_W), BF16)],
                 job=job)


def _attn_bwd(proj, tables, dyb, name, job=None):
    t, width = dyb.shape
    rows = t // GRID_W
    npair = width // LANES
    first = (proj.shape[1] - 3 * width) // LANES

    def body(q_ref, k_ref, v_ref, tz_ref, do_ref, dq_ref, dk_ref, dv_ref, dt_ref, dtz_ref, dq_s, dk_s, dv_s,
             qs, ks, vs, dos, kt, vt):
        qs[...] = q_ref[...].astype(BF16)
        ks[...] = k_ref[...].astype(BF16)
        vs[...] = v_ref[...].astype(BF16)
        dos[...] = do_ref[...].astype(BF16)
        _transposed_pairs(kt, k_ref)
        _transposed_pairs(vt, v_ref)
        dk_s[...] = jnp.zeros_like(dk_s)
        dv_s[...] = jnp.zeros_like(dv_s)
        dtz_ref[...] = jnp.zeros_like(dtz_ref)

        def block(m, carry):
            for hh in range(2):
                q, p, qrows, band, lanes, entries, first_pair = _attn_block(qs, kt, tz_ref, hh, m, rows)
                do = dos[qrows, lanes]
                dp = jnp.dot(do, _band_of(vt, first_pair, hh), preferred_element_type=F32)
                ds = p * (dp - jnp.sum(dp * p, axis=-1, keepdims=True))
                for i, row in enumerate(entries):
                    for g, e in enumerate(row):
                        dtz_ref[hh, e] += ds[i * GRID_W:(i + 1) * GRID_W, g * PAIR_W:(g + 1) * PAIR_W]
                dsb = (ds * (HEAD_DIM ** -0.5)).astype(BF16)
                dq_s[qrows, lanes] = jnp.dot(dsb, ks[band, lanes], preferred_element_type=F32)
                dk_s[band, lanes] += lax.dot_general(dsb, q, TN, preferred_element_type=F32)
                dv_s[band, lanes] += lax.dot_general(p.astype(BF16), do, TN, preferred_element_type=F32)
            return carry

        lax.fori_loop(0, rows // Q_ROWS, block, 0)
        for n, (src, dst) in enumerate(((dq_s, dq_ref), (dk_s, dk_ref), (dv_s, dv_ref))):
            val = src[...]
            dst[...] = val.astype(BF16)
            dt_ref[n] = val.T.astype(BF16)

    col = lambda off: pl.BlockSpec((t, LANES), lambda i: (0, off + i))
    table = pl.BlockSpec((2, N_ENTRIES, GRID_W, PAIR_W), lambda i: (i, 0, 0, 0))
    pairs = pltpu.VMEM((t // PAIR_W, LANES, PAIR_W), BF16)
    return _call(body, name=name, args=[proj, proj, proj, tables, dyb],
                 out_shape=(SDS((t, width), BF16),) * 3 + (SDS((3, width, t), BF16), SDS(tables.shape, F32)),
                 grid=(npair,),
                 in_specs=[col(first), col(first + npair), col(first + 2 * npair), table, col(0)],
                 out_specs=(col(0), col(0), col(0), pl.BlockSpec((3, LANES, t), lambda i: (0, i, 0)), table),
                 scratch_shapes=[pltpu.VMEM((t, LANES), F32)] * 3 + [pltpu.VMEM((t, LANES), BF16)] * 4 + [pairs, pairs],
                 job=job)


def _adamw_math(w, g, m, v):
    m = ADAM_B1 * m + (1.0 - ADAM_B1) * g
    v = ADAM_B2 * v + (1.0 - ADAM_B2) * (g * g)
    m_hat = m / (1.0 - ADAM_B1 ** ADAM_STEP)
    v_hat = v / (1.0 - ADAM_B2 ** ADAM_STEP)
    delta = -ADAM_LR * (m_hat / (jnp.sqrt(v_hat) + ADAM_EPS) + ADAM_WD * w)
    return delta, m, v


def _sum_partials(p_ref):
    g = p_ref[0].astype(F32)
    for s in range(1, N_CHIP):
        g = g + p_ref[s].astype(F32)
    return g


def _adamw_rows(w, partials, m, v, name, after=()):
    rb, n = w.shape
    tr = 64

    def body(w_ref, p_ref, m_ref, v_ref, *rest):
        g_ref, d_ref, nm_ref, nv_ref = rest[len(after):]
        g = _sum_partials(p_ref)
        g_ref[...] = g
        d_ref[...], nm_ref[...], nv_ref[...] = _adamw_math(w_ref[...], g, m_ref[...], v_ref[...])

    blk = pl.BlockSpec((tr, n), lambda i: (i, 0))
    return _call(body, name=name, args=[w, partials.reshape(N_CHIP, rb, n), m, v, *after],
                 out_shape=(SDS((rb, n), F32),) * 4, grid=(rb // tr,),
                 in_specs=[blk, pl.BlockSpec((N_CHIP, tr, n), lambda i: (0, i, 0)), blk, blk] + [ANY] * len(after),
                 out_specs=(blk,) * 4)


def _adamw_cols(w, partials, m, v, name, after=()):
    d, nb = w.shape
    td = 256
    parts = list(partials) if isinstance(partials, (list, tuple)) else [partials]
    heights = [p.shape[0] // N_CHIP for p in parts]

    def body(w_ref, m_ref, v_ref, *rest):
        p_refs, (g_ref, d_ref, nm_ref, nv_ref) = rest[:len(parts)], rest[len(parts) + len(after):]
        g = jnp.concatenate([_sum_partials(p_ref) for p_ref in p_refs], axis=0).T
        g_ref[...] = g
        d_ref[...], nm_ref[...], nv_ref[...] = _adamw_math(w_ref[...], g, m_ref[...], v_ref[...])

    blk = pl.BlockSpec((td, nb), lambda i: (i, 0))
    return _call(body, name=name, args=[w, m, v, *[p.reshape(N_CHIP, h, d) for p, h in zip(parts, heights)], *after],
                 out_shape=(SDS((d, nb), F32),) * 4, grid=(d // td,),
                 in_specs=[blk, blk, blk] + [pl.BlockSpec((N_CHIP, h, td), lambda i: (0, 0, i)) for h in heights]
                 + [ANY] * len(after), out_specs=(blk,) * 4)


def _adamw_small(w, g, m, v, name):
    def body(w_ref, g_ref, m_ref, v_ref, d_ref, nm_ref, nv_ref):
        d_ref[...], nm_ref[...], nv_ref[...] = _adamw_math(w_ref[...], g_ref[...], m_ref[...], v_ref[...])

    return _call(body, name=name, args=[w, g, m, v], out_shape=(SDS(w.shape, F32),) * 3, in_specs=[WHOLE] * 4,
                 out_specs=(WHOLE,) * 3)


TILE = SUBLANES * LANES


def _pack(arrays):
    parts = []
    for a in arrays:
        flat = a.reshape(-1).astype(F32)
        flat = jnp.pad(flat, (0, -flat.size % TILE))
        parts.append(flat.reshape(-1, LANES))
    return jnp.concatenate(parts, axis=0)


def _unpack(pack, like):
    out, row = [], 0
    for a in like:
        n = int(np.prod(a.shape))
        nrows = -(-n // TILE) * SUBLANES
        out.append(pack[row:row + nrows].reshape(-1)[:n].reshape(a.shape))
        row += nrows
    return out


def _dense_gate_blocks(gate_w):
    w = gate_w.reshape(4, -1, 2, HEAD_DIM, HEAD_DIM)
    zero = jnp.zeros_like(w[:, :, 0])
    top = jnp.concatenate([w[:, :, 0], zero], axis=-1)
    bottom = jnp.concatenate([zero, w[:, :, 1]], axis=-1)
    return jnp.concatenate([top, bottom], axis=-2)


def _diag_gate_blocks(dense, shape):
    even = dense[:, :, :HEAD_DIM, :HEAD_DIM]
    odd = dense[:, :, HEAD_DIM:, HEAD_DIM:]
    return jnp.stack([even, odd], axis=2).reshape(shape)


LARGE = ("ffn1_w_in", "ffn1_w_out", "w_in_mix", "w_out_mix", "ffn2_w_in", "ffn2_w_out")
COLUMN_SHARDED = ("ffn1_w_in", "w_in_mix", "ffn2_w_in")
SHARDED_SMALL = ("lru_conv_w", "lru_lambda")
REPLICATED = ("norm_ffn1", "norm_mix", "lru_conv_b", "lru_gate_w", "lru_gate_b", "attn_rpb", "lru_out_norm",
              "attn_out_norm", "norm_ffn2", "norm_final")
SMALL_ORDER = REPLICATED + SHARDED_SMALL
WEIGHTS = ("norm_ffn1", "ffn1_w_in", "ffn1_w_out", "norm_mix", "w_in_mix", "lru_conv_w", "lru_conv_b", "lru_gate_w",
           "lru_gate_b", "lru_lambda", "attn_rpb", "lru_out_norm", "attn_out_norm", "w_out_mix", "norm_ffn2",
           "ffn2_w_in", "ffn2_w_out", "norm_final")


PARTS = {("gather", "w_in_mix"): 4, ("gather", "ffn2_w_in"): 8}
CARRIES = {
    "gather_ffn1_in": [(("gather", "ffn1_w_in"), 1), (("gather", "small"), 1)],
    "ffn1_hidden": [(("gather", "ffn1_w_out"), 1), (("gather", "w_in_mix"), 1)],
    "ffn1_out": [(("gather", "w_in_mix"), 3)],
    "mix_in_proj": [(("gather", "w_out_mix"), 1), (("gather", "ffn2_w_in"), 1)],
    "lru_fwd": [(("gather", "ffn2_w_in"), 3)],
    "attn_fwd": [(("gather", "ffn2_w_in"), 3)],
    "mix_out_proj": [(("gather", "ffn2_w_in"), 1)],
    "ffn2_hidden": [(("gather", "ffn2_w_out"), 1)],
    "ffn1_bwd": [(("gather", "small_grads"), 1)],
    "gather_late_grads": [(("gather", "late_grads"), 1)],
}


class _Transfer:
    def __init__(self, kind, src, dest, block_rows, parts):
        self.kind, self.src, self.dest = kind, src, dest
        self.ranges, self.taken = _split(block_rows, parts), 0

    def take(self, count):
        lo, hi = self.ranges[self.taken][0], self.ranges[self.taken + count - 1][1]
        self.taken += count
        return _Piece(self.kind, self.src, self.dest, lo, hi)


class _Traffic:
    def __init__(self):
        self.transfers = {}

    def open(self, kind, name, src, placed=None):
        dest = _gathered(src) if placed is None else placed
        self.transfers[kind, name] = _Transfer(kind, src, dest, dest.shape[0] // N_DEV, PARTS.get((kind, name), 1))

    def _job(self, host):
        moved = [self.transfers[key] for key, _ in CARRIES[host]]
        return moved, _Job([tr.take(count) for tr, (_, count) in zip(moved, CARRIES[host])])

    def carry(self, host, fn, *args, **kw):
        if host not in CARRIES:
            return fn(*args, name=host, **kw)
        moved, job = self._job(host)
        res, landed = fn(*args, name=host, job=job, **kw)
        for tr, arr in zip(moved, landed):
            tr.dest = arr
        return res

    def alone(self, host):
        moved, job = self._job(host)
        for tr, arr in zip(moved, _run_job(job, host)):
            tr.dest = arr

    def result(self, kind, name):
        tr = self.transfers.pop((kind, name))
        assert tr.taken == len(tr.ranges), (kind, name)
        return tr.dest


def _forward_backward(x, target, shards, sharded_small, s):
    c = s["lru_conv_b"].shape[1]
    width = s["attn_out_norm"].shape[1]
    t = x.shape[0]
    traffic = _Traffic()
    carry = traffic.carry
    weight = lambda n: traffic.result("gather", n)

    for n in LARGE:
        traffic.open("gather", n, None, placed=shards[n])
    traffic.open("gather", "small", sharded_small)
    traffic.alone("gather_ffn1_in")
    full_small = weight("small").reshape(N_DEV, SUBLANES, c // N_DEV)
    conv_w = full_small[:, :CONV_WIDTH].transpose(1, 0, 2).reshape(CONV_WIDTH, c)
    lam = full_small[:, CONV_WIDTH:CONV_WIDTH + 2].transpose(1, 0, 2).reshape(2, c)
    w = {"ffn1_w_in": weight("ffn1_w_in")}
    ffn_out = dict(nt=False, out_dtype=F32, tm=512, tn=512, scale=0.5)
    u1 = _rmsnorm_fwd(x, s["norm_ffn1"], "norm_ffn1")
    g1, up1, hid1, hid1_t = carry("ffn1_hidden", _ffn_hidden, u1, w["ffn1_w_in"])
    w["ffn1_w_out"] = weight("ffn1_w_out")
    h1 = carry("ffn1_out", _mm, hid1, w["ffn1_w_out"], residual=x, **ffn_out)
    w["w_in_mix"] = weight("w_in_mix")
    u2 = _rmsnorm_fwd(h1, s["norm_mix"], "norm_mix")
    proj = carry("mix_in_proj", _mm, u2, w["w_in_mix"], nt=True, out_dtype=F32, tm=512, tn=512)
    w["w_out_mix"] = weight("w_out_mix")
    gw = _dense_gate_blocks(s["lru_gate_w"]).astype(BF16)
    gb = s["lru_gate_b"].reshape(4, c)
    tables, tables_vjp = jax.vjp(_bias_tables, s["attn_rpb"])
    ya, hf, hb = carry("lru_fwd", _lru_fwd, proj, conv_w, s["lru_conv_b"], gw, gb, lam)
    yb = carry("attn_fwd", _attn_fwd, proj, tables, width)
    y, yt = _mixnorm_fwd(ya, yb, s["lru_out_norm"], s["attn_out_norm"], "mix_norm")
    h2 = carry("mix_out_proj", _mm, y, w["w_out_mix"], nt=False, out_dtype=F32, tm=512, tn=512, residual=h1)
    u3 = _rmsnorm_fwd(h2, s["norm_ffn2"], "norm_ffn2")
    w["ffn2_w_in"] = weight("ffn2_w_in")
    g2, up2, hid2, hid2_t = carry("ffn2_hidden", _ffn_hidden, u3, w["ffn2_w_in"])
    w["ffn2_w_out"] = weight("ffn2_w_out")
    h3 = carry("ffn2_out", _mm, hid2, w["ffn2_w_out"], residual=h2, **ffn_out)
    dh3, df2, loss_part, d_norm_final = _final_loss(h3, s["norm_final"], target, "final_loss")

    grads = {}
    grad_of = dict(nt=False, out_dtype=BF16, tm=512, tn=1024)

    to_sibling, to_chips = {}, {}

    def reduce_in_chip(n):
        land = _blank_like(grads[n], grads[n].shape[0] // 2, "landing_" + n)
        to_sibling[n], token = _split_start("to_sibling", grads[n], land, "to_sibling_" + n)
        RUN_AFTER.append(token)

    def reduce_over_chips(n, after):
        own, got = _split_wait(to_sibling.pop(n), [after], "from_sibling_" + n)
        summed = _pair_sum(own, got, "pair_sum_" + n)
        to_chips[n], token = _split_start("to_chips", summed, _own_slot(summed, "own_slot_" + n), "to_chips_" + n)
        RUN_AFTER.append(token)
        return token

    f = hid2_t.shape[0]
    grads["ffn2_w_out"] = carry("ffn2_out_grad", _mm, hid2_t, df2, **grad_of)
    reduce_in_chip("ffn2_w_out")
    du3, da2_t = carry("ffn2_bwd", _ffn_bwd, df2, g2, up2, w["ffn2_w_in"], w["ffn2_w_out"])
    reduce_over_chips("ffn2_w_out", du3)
    grads["ffn2_w_in"] = carry("ffn2_in_grad", _mm, da2_t.reshape(2 * f, t), u3, **grad_of)
    reduce_in_chip("ffn2_w_in")
    dh2, dh2b, d_norm_ffn2 = carry("norm_ffn2_bwd", _rmsnorm_bwd, du3, h2, s["norm_ffn2"], dh3, 1.0)
    grads["w_out_mix"] = carry("mix_out_grad", _mm, yt, dh2b, **grad_of)
    reduce_over_chips("ffn2_w_in", grads["w_out_mix"])
    reduce_in_chip("w_out_mix")
    dy = carry("mix_out_bwd", _mm, dh2b, w["w_out_mix"], nt=True, out_dtype=F32, tm=512, tn=512)
    dya, dyb, d_lru_out_norm, d_attn_out_norm = _mixnorm_bwd(dy, ya, yb, s["lru_out_norm"], s["attn_out_norm"],
                                                             "mix_norm_bwd")
    dq, dk, dv, dqkv_t, d_tables = carry("attn_bwd", _attn_bwd, proj, tables, dyb)
    reduce_over_chips("w_out_mix", dq)
    dx_lru, dg_lru, dxg_t, d_conv_w, d_conv_b, d_gw, d_gb, d_lam = carry(
        "lru_bwd", _lru_bwd, proj, conv_w, s["lru_conv_b"], gw, gb, lam, hf, hb, dya)
    rows_of = 2 * c + 3 * width
    lru_rows = carry("mix_in_grad_lru", _mm, dxg_t.reshape(2 * c, t), u2, out_rows=rows_of, **grad_of)
    grads["w_in_mix"] = carry("mix_in_grad_attn", _mm, dqkv_t.reshape(3 * width, t), u2, out_rows=rows_of,
                              row_offset=2 * c, into=lru_rows, **grad_of)
    reduce_in_chip("w_in_mix")
    du2 = carry("mix_in_bwd", _mm, [dx_lru, dg_lru, dq, dk, dv], w["w_in_mix"], nt=False, out_dtype=F32, tm=512,
                tn=512)
    dh1, df1, d_norm_mix = carry("norm_mix_bwd", _rmsnorm_bwd, du2, h1, s["norm_mix"], dh2, 0.5)
    reduce_over_chips("w_in_mix", dh1)

    by_device = lambda a: a.reshape(a.shape[0], N_DEV, -1).transpose(1, 0, 2)
    small = {
        "norm_mix": d_norm_mix, "lru_conv_b": d_conv_b, "lru_gate_w": _diag_gate_blocks(d_gw, s["lru_gate_w"].shape),
        "lru_gate_b": d_gb.reshape(s["lru_gate_b"].shape), "attn_rpb": tables_vjp(d_tables)[0],
        "lru_out_norm": d_lru_out_norm, "attn_out_norm": d_attn_out_norm, "norm_ffn2": d_norm_ffn2,
        "norm_final": d_norm_final, "lru_conv_w": by_device(d_conv_w), "lru_lambda": by_device(d_lam),
    }
    early = [small[n] for n in SMALL_ORDER[1:]]
    traffic.open("gather", "small_grads", _pack(early))

    grads["ffn1_w_out"] = carry("ffn1_out_grad", _mm, hid1_t, df1, **grad_of)
    reduce_in_chip("ffn1_w_out")
    du1, da1_t = carry("ffn1_bwd", _ffn_bwd, df1, g1, up1, w["ffn1_w_in"], w["ffn1_w_out"])
    grad_x, _, d_norm_ffn1 = carry("norm_ffn1_bwd", _rmsnorm_bwd, du1, x, s["norm_ffn1"], dh1, 1.0)
    traffic.open("gather", "late_grads", _pack([d_norm_ffn1]))
    traffic.alone("gather_late_grads")
    late = traffic.result("gather", "late_grads")
    reduce_over_chips("ffn1_w_out", late)
    half = 2 * f // N_DEV // 2
    half_rows = lambda h: (half, N_DEV, lambda i: 2 * i + h)
    grads["ffn1_w_in_a"] = carry("ffn1_in_grad_a", _mm, da1_t.reshape(2 * f, t), u1, take=half_rows(0), **grad_of)
    reduce_in_chip("ffn1_w_in_a")
    grads["ffn1_w_in_b"] = carry("ffn1_in_grad_b", _mm, da1_t.reshape(2 * f, t), u1, take=half_rows(1), **grad_of)
    reduce_over_chips("ffn1_w_in_a", grads["ffn1_w_in_b"])
    reduce_in_chip("ffn1_w_in_b")
    reduced = (_unpack(_sum_devices(late, "sum_late_grads"), [d_norm_ffn1])
               + _unpack(_sum_devices(traffic.result("gather", "small_grads"), "sum_small_grads"), early))
    last_token = reduce_over_chips("ffn1_w_in_b", reduced[1])
    RUN_AFTER.clear()
    assert not traffic.transfers and not to_sibling, (list(traffic.transfers), list(to_sibling))
    return loss_part[0, 0], grad_x, to_chips, last_token, dict(zip(SMALL_ORDER, reduced))


def _step(x, loss_target, p, m, v):
    me = 4 * lax.axis_index("x") + 2 * lax.axis_index("y") + lax.axis_index("c")

    shards = {n: _cast_into_place(p[n], n in COLUMN_SHARDED, "cast_" + n) for n in LARGE}
    sharded_small = (jnp.pad(p["lru_conv_w"], ((0, SUBLANES - CONV_WIDTH), (0, 0)))
                     + jnp.pad(p["lru_lambda"], ((CONV_WIDTH, SUBLANES - CONV_WIDTH - 2), (0, 0))))
    s = {n: p[n] if n in ("lru_gate_w", "lru_gate_b", "attn_rpb") else p[n].reshape(1, -1) for n in REPLICATED}

    loss_part, grad_x, to_chips, last_token, small = _forward_backward(x, loss_target, shards, sharded_small, s)
    loss = lax.psum(loss_part, ("x", "y", "c"))

    def landed(n, after):
        return _split_wait(to_chips[n], after, "from_chips_" + n)[1]

    def update(n, partials):
        return (_adamw_cols if n in COLUMN_SHARDED else _adamw_rows)(p[n], partials, m[n], v[n], "adamw_" + n)

    out = {n: update(n, landed(n, [last_token])) for n in LARGE if n != "ffn1_w_in"}
    done = [o[3] for o in out.values()]
    out["ffn1_w_in"] = update("ffn1_w_in", [landed("ffn1_w_in_a", done), landed("ffn1_w_in_b", done)])

    g_small = {n: lax.dynamic_index_in_dim(g, me, axis=0, keepdims=False) if n in SHARDED_SMALL else g
               for n, g in small.items()}
    names = SMALL_ORDER
    like = [p[n] for n in names]
    pack_of = lambda d: _pack([d[n].reshape(p[n].shape) for n in names])
    upd = _adamw_small(pack_of(p), pack_of(g_small), pack_of(m), pack_of(v), "adamw_small")
    for n, d_, m_, v_ in zip(names, *[_unpack(u, like) for u in upd]):
        out[n] = (g_small[n].reshape(p[n].shape), d_, m_, v_)
    return loss, grad_x, out


def kernel(x, norm_ffn1, ffn1_w_in, ffn1_w_out, norm_mix, w_in_mix, lru_conv_w, lru_conv_b, lru_gate_w, lru_gate_b, lru_lambda, attn_rpb, lru_out_norm, attn_out_norm, w_out_mix, norm_ffn2, ffn2_w_in, ffn2_w_out, norm_final, loss_target, m_norm_ffn1, m_ffn1_w_in, m_ffn1_w_out, m_norm_mix, m_w_in_mix, m_lru_conv_w, m_lru_conv_b, m_lru_gate_w, m_lru_gate_b, m_lru_lambda, m_attn_rpb, m_lru_out_norm, m_attn_out_norm, m_w_out_mix, m_norm_ffn2, m_ffn2_w_in, m_ffn2_w_out, m_norm_final, v_norm_ffn1, v_ffn1_w_in, v_ffn1_w_out, v_norm_mix, v_w_in_mix, v_lru_conv_w, v_lru_conv_b, v_lru_gate_w, v_lru_gate_b, v_lru_lambda, v_attn_rpb, v_lru_out_norm, v_attn_out_norm, v_w_out_mix, v_norm_ffn2, v_ffn2_w_in, v_ffn2_w_out, v_norm_final):
    given = dict(locals())
    drop_layer = lambda n, a: a if n == "norm_final" else a[0]
    p = {n: drop_layer(n, given[n]) for n in WEIGHTS}
    m = {n: drop_layer(n, given["m_" + n]) for n in WEIGHTS}
    v = {n: drop_layer(n, given["v_" + n]) for n in WEIGHTS}
    loss, grad_x, out = _step(x[0], loss_target[0], p, m, v)
    shaped = lambda n, a: a.reshape(given[n].shape)
    return (loss, grad_x[None], *[shaped(n, out[n][k]) for k in range(4) for n in WEIGHTS])
```

```python
import math

import numpy as np
import jax
import jax.numpy as jnp
from jax import lax
from jax.experimental import pallas as pl
from jax.experimental.pallas import tpu as pltpu

F32 = jnp.float32
BF16 = jnp.bfloat16
SDS = jax.ShapeDtypeStruct

N_DEV = 8
N_CHIP = 4
NORM_EPS = 1e-6
RG_C = 8.0
CONV_WIDTH = 4
HEAD_DIM = 64
GRID_W = 64
WIN_ROWS = 8
WIN_COLS = 16
NEG = -1e30

ADAM_LR = 0.001
ADAM_B1 = 0.9
ADAM_B2 = 0.999
ADAM_EPS = 1e-08
ADAM_WD = 0.01
ADAM_STEP = 10

LANES = 128
SUBLANES = 8
VMEM_LIMIT = 56 * 1024 * 1024

NT = (((1,), (1,)), ((), ()))
TN = (((0,), (0,)), ((), ()))
ANY = pl.BlockSpec(memory_space=pl.ANY)
WHOLE = pl.BlockSpec(memory_space=pltpu.VMEM)
MESH = pl.DeviceIdType.MESH


def _sigmoid(x):
    return 1.0 / (1.0 + jnp.exp(-x))


def _gelu_parts(x):
    c = math.sqrt(2.0 / math.pi)
    t = jnp.tanh(c * (x + 0.044715 * (x * x * x)))
    gelu = 0.5 * x * (1.0 + t)
    dgelu = 0.5 * (1.0 + t) + 0.5 * x * (1.0 - t * t) * (c * (1.0 + 3.0 * 0.044715 * (x * x)))
    return gelu, dgelu


def _expm1(x):
    poly = x * (1.0 + x * (1.0 / 2) * (1.0 + x * (1.0 / 3) * (1.0 + x * (1.0 / 4) * (1.0 + x * (1.0 / 5) * (1.0 + x * (1.0 / 6))))))
    return jnp.where(jnp.abs(x) < 0.25, poly, jnp.exp(x) - 1.0)


def _softplus(x):
    return jnp.maximum(x, 0.0) + jnp.log1p(jnp.exp(-jnp.abs(x)))


class _Piece:
    N_REMOTE = {"gather": 7}
    N_LOCAL = {"gather": 1}

    def __init__(self, kind, src, dest, lo, hi):
        self.kind, self.src, self.dest, self.lo, self.hi = kind, src, dest, lo, hi


RELAY_AT = 60
RUN_AFTER = []


class _Job:
    def __init__(self, pieces):
        self.pieces = list(pieces)
        self.ins = [p.src for p in self.pieces if p.src is not None]
        self.out_shapes = [SDS(p.dest.shape, p.dest.dtype) for p in self.pieces]
        self.aliased = [i for i, p in enumerate(self.pieces) if not isinstance(p.dest, SDS)]
        self.n_remote = sum(_Piece.N_REMOTE[p.kind] for p in self.pieces)
        self.n_local = max(sum(_Piece.N_LOCAL[p.kind] for p in self.pieces), 1)

    def _each(self, step, ins, outs, send_sems, recv_sems, local_sems):
        remote = local = 0
        ins = iter(ins)
        for p, dst in zip(self.pieces, outs):
            src = None if p.src is None else next(ins)
            _EXCHANGES[p.kind](step, p, src, dst, send_sems, recv_sems, local_sems, remote, local)
            remote += _Piece.N_REMOTE[p.kind]
            local += _Piece.N_LOCAL[p.kind]

    def start(self, *refs):
        self._each("start", *refs)

    def relay(self, *refs):
        self._each("relay", *refs)

    def finish(self, *refs):
        self._each("finish", *refs)


def _call(body, *, name, args, out_shape, in_specs, out_specs, grid=(), scratch_shapes=(), aliases=None, job=None):
    single = not isinstance(out_shape, (tuple, list))
    out_shape = (out_shape,) if single else tuple(out_shape)
    out_specs = (out_specs,) if single else tuple(out_specs)
    aliases = dict(aliases or {})
    if RUN_AFTER:
        tokens, n_plain, plain_body = list(RUN_AFTER), len(args), body
        RUN_AFTER.clear()
        body = lambda *refs: plain_body(*refs[:n_plain], *refs[n_plain + len(tokens):])
        args, in_specs = list(args) + tokens, list(in_specs) + [ANY] * len(tokens)
    params = pltpu.CompilerParams(dimension_semantics=("arbitrary",) * len(grid) if grid else None,
                                  vmem_limit_bytes=VMEM_LIMIT)
    if job is None:
        res = pl.pallas_call(body, out_shape=out_shape, grid=grid, in_specs=list(in_specs), out_specs=out_specs,
                             scratch_shapes=list(scratch_shapes), input_output_aliases=aliases, name=name,
                             compiler_params=params)(*args)
        return res[0] if single else res

    n_in, n_out, n_scr = len(args), len(out_shape), len(scratch_shapes)
    j_in, j_out, j_alias = len(job.ins), len(job.out_shapes), len(job.aliased)

    def hosted(*refs):
        ins, refs = refs[:n_in], refs[n_in:]
        j_ins, refs = refs[:j_in], refs[j_in + j_alias:]
        outs, refs = refs[:n_out], refs[n_out:]
        j_outs, refs = refs[:j_out], refs[j_out:]
        scr, sems = refs[:n_scr], refs[n_scr:]
        if grid:
            step = 0
            for axis, size in enumerate(grid):
                step = step * size + pl.program_id(axis)
            steps = math.prod(grid)
            pl.when(step == 0)(lambda: job.start(j_ins, j_outs, *sems))
            body(*ins, *outs, *scr)
            pl.when(step == min(RELAY_AT * steps // 100, steps - 1))(lambda: job.relay(j_ins, j_outs, *sems))
            pl.when(step == steps - 1)(lambda: job.finish(j_ins, j_outs, *sems))
        else:
            job.start(j_ins, j_outs, *sems)
            body(*ins, *outs, *scr)
            job.relay(j_ins, j_outs, *sems)
            job.finish(j_ins, j_outs, *sems)

    res = pl.pallas_call(
        hosted, out_shape=out_shape + tuple(job.out_shapes), grid=grid,
        in_specs=list(in_specs) + [ANY] * (j_in + j_alias), out_specs=out_specs + (ANY,) * j_out,
        scratch_shapes=list(scratch_shapes) + [pltpu.SemaphoreType.DMA((job.n_remote,)),
                                               pltpu.SemaphoreType.DMA((job.n_remote,)),
                                               pltpu.SemaphoreType.DMA((job.n_local,))],
        input_output_aliases={**aliases, **{n_in + j_in + k: n_out + i for k, i in enumerate(job.aliased)}},
        name=name, compiler_params=params)(*args, *job.ins, *[job.pieces[i].dest for i in job.aliased])
    own, carried = res[:n_out], res[n_out:]
    return (own[0] if single else own), carried


def _run_job(job, name):
    return _call(lambda: None, name=name, args=[], out_shape=(), in_specs=[], out_specs=(), job=job)[1]


def _position():
    return lax.axis_index("x"), lax.axis_index("y"), lax.axis_index("c")


def _flat(px, py, pc):
    return 4 * px + 2 * py + pc


def _gather_exchange(step, p, src, dst, send_sems, recv_sems, local_sems, r0, l0):
    x, y, c = _position()
    me, sibling = (x, y, c), (x, y, 1 - c)
    along_x, along_y, diagonal = (1 - x, y), (x, 1 - y), (1 - x, 1 - y)
    south = c == 0
    passed_on = (jnp.where(south, 1 - x, x), jnp.where(south, y, 1 - y))
    passed_to = (jnp.where(south, x, 1 - x), jnp.where(south, 1 - y, y))
    placed = p.src is None
    rb, n_rows = p.dest.shape[0] // N_DEV, p.hi - p.lo

    def rows(block):
        return dst.at[pl.ds(_flat(*block) * rb + p.lo, n_rows), :]

    mine = rows(me) if placed else src.at[pl.ds(p.lo, n_rows), :]

    def copy(k, block, to, own=False):
        return pltpu.make_async_remote_copy(
            src_ref=mine if own else rows(block), dst_ref=rows(block),
            send_sem=send_sems.at[r0 + k], recv_sem=recv_sems.at[r0 + k], device_id=to, device_id_type=MESH)

    local = None if placed else pltpu.make_async_copy(mine, rows(me), local_sems.at[l0])
    if step == "start":
        if local is not None:
            local.start()
        copy(0, me, sibling, own=True).start()
        copy(1, me, (*along_x, c), own=True).start()
        copy(2, me, (*along_y, c), own=True).start()
    elif step == "relay":
        copy(1, (*along_x, c), me).wait_recv()
        copy(2, (*along_y, c), me).wait_recv()
        copy(3, (*passed_on, c), (*passed_to, c)).start()
        copy(4, (*along_x, c), sibling).start()
        copy(5, (*along_y, c), sibling).start()
    else:
        copy(3, (*diagonal, c), me).wait_recv()
        copy(6, (*diagonal, c), sibling).start()
        copy(0, sibling, me).wait_recv()
        copy(4, (*along_x, 1 - c), me).wait_recv()
        copy(5, (*along_y, 1 - c), me).wait_recv()
        copy(6, (*diagonal, 1 - c), me).wait_recv()
        copy(0, me, sibling, own=True).wait_send()
        copy(1, me, (*along_x, c), own=True).wait_send()
        copy(2, me, (*along_y, c), own=True).wait_send()
        copy(3, (*passed_on, c), (*passed_to, c)).wait_send()
        copy(4, (*along_x, c), sibling).wait_send()
        copy(5, (*along_y, c), sibling).wait_send()
        copy(6, (*diagonal, c), sibling).wait_send()
        if local is not None:
            local.wait()


CHIP_FLIPS = [(1, 0), (0, 1), (1, 1)]
_EXCHANGES = {"gather": _gather_exchange}


def _gathered(shard):
    return SDS((N_DEV * shard.shape[0], shard.shape[1]), shard.dtype)


def _split(rows, parts):
    cuts = [rows * k // parts // 16 * 16 for k in range(parts)] + [rows]
    return list(zip(cuts[:-1], cuts[1:]))


def _pair_sum(g, from_sibling, name):
    rb, n = g.shape[0] // N_DEV, g.shape[1]
    tr = rb if rb * n * 2 <= 3 * 1024 * 1024 else rb // 2
    core = lax.axis_index("c").astype(jnp.int32).reshape(1)

    def body(c_ref, g_ref, r_ref, o_ref):
        o_ref[...] = (g_ref[...].astype(F32) + r_ref[...].astype(F32)).astype(BF16)

    grid_spec = pltpu.PrefetchScalarGridSpec(
        num_scalar_prefetch=1, grid=(N_CHIP, rb // tr),
        in_specs=[pl.BlockSpec((None, None, tr, n), lambda q, i, c_ref: (q, c_ref[0], i, 0)),
                  pl.BlockSpec((None, tr, n), lambda q, i, c_ref: (q, i, 0))],
        out_specs=pl.BlockSpec((None, tr, n), lambda q, i, c_ref: (q, i, 0)))
    out = pl.pallas_call(
        body, grid_spec=grid_spec, out_shape=SDS((N_CHIP, rb, n), BF16), name=name,
        compiler_params=pltpu.CompilerParams(dimension_semantics=("arbitrary",) * 2, vmem_limit_bytes=VMEM_LIMIT))(
            core, g.reshape(N_CHIP, 2, rb, n), from_sibling.reshape(N_CHIP, rb, n))
    return out.reshape(N_CHIP * rb, n)


SEM = pl.BlockSpec(memory_space=pltpu.SEMAPHORE)
IN_HBM = pl.BlockSpec(memory_space=pltpu.HBM)
SIDE_EFFECT = pltpu.SideEffectType.DATAFLOW_SIDE_EFFECTING


def _own_slot(partials, name):
    rb, n = partials.shape[0] // N_CHIP, partials.shape[1]
    tr = rb // 2
    chip = (2 * lax.axis_index("x") + lax.axis_index("y")).astype(jnp.int32).reshape(1)

    def body(chip_ref, src_ref, dst_ref):
        dst_ref[...] = src_ref[...]

    block = pl.BlockSpec((None, tr, n), lambda i, chip_ref: (chip_ref[0], i, 0))
    grid_spec = pltpu.PrefetchScalarGridSpec(num_scalar_prefetch=1, grid=(rb // tr,), in_specs=[block], out_specs=block)
    out = pl.pallas_call(
        body, grid_spec=grid_spec, out_shape=SDS((N_CHIP, rb, n), partials.dtype), name=name,
        compiler_params=pltpu.CompilerParams(dimension_semantics=("arbitrary",), vmem_limit_bytes=VMEM_LIMIT))(
            chip, partials.reshape(N_CHIP, rb, n))
    return out.reshape(partials.shape)


def _blank_like(src, rows, name):
    return pl.pallas_call(lambda src_ref, out_ref: None, out_shape=SDS((rows, src.shape[1]), src.dtype),
                          in_specs=[ANY], out_specs=ANY, name=name)(src)


def _chip_copies(src_ref, land_ref, sems):
    x, y, c = _position()
    rb = src_ref.shape[0] // N_CHIP
    copies = []
    for k, (fx, fy) in enumerate(CHIP_FLIPS):
        px, py = (1 - x if fx else x), (1 - y if fy else y)
        copies.append(pltpu.make_async_remote_copy(
            src_ref=src_ref.at[pl.ds((2 * px + py) * rb, rb), :], dst_ref=land_ref.at[pl.ds((2 * x + y) * rb, rb), :],
            send_sem=sems[2 * k], recv_sem=sems[2 * k + 1], device_id=(px, py, c), device_id_type=MESH))
    return copies


def _sibling_copies(src_ref, land_ref, sems):
    x, y, c = _position()
    rb = src_ref.shape[0] // N_DEV
    return [pltpu.make_async_remote_copy(
        src_ref=src_ref.at[pl.ds((2 * q + 1 - c) * rb, rb), :], dst_ref=land_ref.at[pl.ds(q * rb, rb), :],
        send_sem=sems[2 * q], recv_sem=sems[2 * q + 1], device_id=(x, y, 1 - c), device_id_type=MESH)
        for q in range(N_CHIP)]


SPLIT_COPIES = {"to_chips": (_chip_copies, 3), "to_sibling": (_sibling_copies, N_CHIP)}


def _split_start(kind, src, land, name):
    copies_of, n_copies = SPLIT_COPIES[kind]

    def body(src_ref, land_ref, *rest):
        sems, token = rest[:2 * n_copies], rest[-1]
        for copy in copies_of(src_ref, land_ref, sems):
            copy.start()
        token[...] = jnp.zeros_like(token)

    res = pl.pallas_call(
        body, name=name,
        out_shape=(pltpu.SemaphoreType.DMA(()),) * (2 * n_copies)
        + (pltpu.HBM(src.shape, src.dtype), pltpu.HBM(land.shape, land.dtype), SDS((SUBLANES, LANES), F32)),
        in_specs=(IN_HBM, IN_HBM), out_specs=(SEM,) * (2 * n_copies) + (IN_HBM, IN_HBM, WHOLE),
        input_output_aliases={0: 2 * n_copies, 1: 2 * n_copies + 1},
        compiler_params=pltpu.CompilerParams(has_side_effects=SIDE_EFFECT))(
            pltpu.with_memory_space_constraint(src, pltpu.HBM), pltpu.with_memory_space_constraint(land, pltpu.HBM))
    return (kind, res[:2 * n_copies], res[-3], res[-2]), res[-1]


def _split_wait(pending, after, name):
    kind, sems, src, land = pending
    copies_of, n_copies = SPLIT_COPIES[kind]

    def body(src_ref, land_ref, *rest):
        for copy in copies_of(src_ref, land_ref, rest[:2 * n_copies]):
            copy.wait_send()
            copy.wait_recv()

    return pl.pallas_call(
        body, name=name, out_shape=(pltpu.HBM(src.shape, src.dtype), pltpu.HBM(land.shape, land.dtype)),
        in_specs=(IN_HBM, IN_HBM) + (SEM,) * (2 * n_copies) + (ANY,) * len(after), out_specs=(IN_HBM, IN_HBM),
        input_output_aliases={0: 0, 1: 1},
        compiler_params=pltpu.CompilerParams(has_side_effects=SIDE_EFFECT))(src, land, *sems, *after)


def _sum_devices(gathered, name):
    r = gathered.shape[0] // N_DEV

    def body(g_ref, o_ref):
        acc = g_ref[0]
        for s in range(1, N_DEV):
            acc = acc + g_ref[s]
        o_ref[...] = acc

    return _call(body, name=name, args=[gathered.reshape(N_DEV, r, LANES)], out_shape=SDS((r, LANES), F32),
                 in_specs=[WHOLE], out_specs=WHOLE)


def _cast_into_place(w, transposed, name):
    me = _flat(*_position()).astype(jnp.int32).reshape(1)
    if transposed:
        d, rb = w.shape
        td = 512
        grid = (d // td,)
        in_spec = pl.BlockSpec((td, rb), lambda i, me_ref: (i, 0))
        out_spec = pl.BlockSpec((rb, td), lambda i, me_ref: (me_ref[0], i))
    else:
        rb, d = w.shape
        grid = (1,)
        in_spec = pl.BlockSpec((rb, d), lambda i, me_ref: (0, 0))
        out_spec = pl.BlockSpec((rb, d), lambda i, me_ref: (me_ref[0], 0))

    def body(me_ref, w_ref, o_ref):
        value = w_ref[...]
        o_ref[...] = (value.T if transposed else value).astype(BF16)

    grid_spec = pltpu.PrefetchScalarGridSpec(num_scalar_prefetch=1, grid=grid, in_specs=[in_spec], out_specs=out_spec)
    return pl.pallas_call(
        body, grid_spec=grid_spec, out_shape=SDS((N_DEV * rb, d), BF16), name=name,
        compiler_params=pltpu.CompilerParams(dimension_semantics=("arbitrary",), vmem_limit_bytes=VMEM_LIMIT))(me, w)


ROW_TILE = 256


def _rmsnorm_fwd(h, gain, name):
    t, d = h.shape

    def body(h_ref, g_ref, u_ref):
        x = h_ref[...]
        u_ref[...] = (x * lax.rsqrt(jnp.mean(x * x, axis=-1, keepdims=True) + NORM_EPS) * g_ref[...]).astype(BF16)

    row = pl.BlockSpec((ROW_TILE, d), lambda i: (i, 0))
    return _call(body, name=name, args=[h, gain], out_shape=SDS((t, d), BF16), grid=(t // ROW_TILE,),
                 in_specs=[row, pl.BlockSpec((1, d), lambda i: (0, 0))], out_specs=row)


def _rms_bwd_math(x, gain, dy):
    rstd = lax.rsqrt(jnp.mean(x * x, axis=-1, keepdims=True) + NORM_EPS)
    xhat = x * rstd
    dxh = dy * gain
    dx = rstd * (dxh - xhat * jnp.mean(dxh * xhat, axis=-1, keepdims=True))
    return dx, jnp.sum(dy * xhat, axis=0, keepdims=True)


def _rmsnorm_bwd(du, h, gain, resid, bf_scale, name, job=None):
    t, d = h.shape

    def body(du_ref, h_ref, g_ref, r_ref, dh_ref, dhb_ref, dg_ref):
        @pl.when(pl.program_id(0) == 0)
        def _():
            dg_ref[...] = jnp.zeros_like(dg_ref)

        dx, dg = _rms_bwd_math(h_ref[...], g_ref[...], du_ref[...])
        dh = r_ref[...] + dx
        dh_ref[...] = dh
        dhb_ref[...] = (bf_scale * dh).astype(BF16)
        dg_ref[...] += dg

    row = pl.BlockSpec((ROW_TILE, d), lambda i: (i, 0))
    vec = pl.BlockSpec((1, d), lambda i: (0, 0))
    return _call(body, name=name, args=[du, h, gain, resid],
                 out_shape=(SDS((t, d), F32), SDS((t, d), BF16), SDS((1, d), F32)), grid=(t // ROW_TILE,),
                 in_specs=[row, row, vec, row], out_specs=(row, row, vec), job=job)


def _final_loss(h, gain, target, name):
    t, d = h.shape

    def body(h_ref, g_ref, t_ref, dh_ref, dhb_ref, loss_ref, dg_ref):
        @pl.when(pl.program_id(0) == 0)
        def _():
            dg_ref[...] = jnp.zeros_like(dg_ref)
            loss_ref[...] = jnp.zeros_like(loss_ref)

        x = h_ref[...]
        gain = g_ref[...]
        out = x * lax.rsqrt(jnp.mean(x * x, axis=-1, keepdims=True) + NORM_EPS) * gain
        err = out - t_ref[...]
        loss_ref[...] += 0.5 * jnp.sum(jnp.mean(err * err, axis=-1, keepdims=True), axis=0, keepdims=True)
        dx, dg = _rms_bwd_math(x, gain, err * (1.0 / d))
        dh_ref[...] = dx
        dhb_ref[...] = (0.5 * dx).astype(BF16)
        dg_ref[...] += dg

    row = pl.BlockSpec((ROW_TILE, d), lambda i: (i, 0))
    vec = pl.BlockSpec((1, d), lambda i: (0, 0))
    one = pl.BlockSpec((SUBLANES, LANES), lambda i: (0, 0))
    return _call(body, name=name, args=[h, gain, target],
                 out_shape=(SDS((t, d), F32), SDS((t, d), BF16), SDS((SUBLANES, LANES), F32), SDS((1, d), F32)),
                 grid=(t // ROW_TILE,), in_specs=[row, vec, row], out_specs=(row, row, one, vec))


def _mixnorm_fwd(ya, yb, ga, gb, name):
    t, c = ya.shape

    def body(ya_ref, yb_ref, ga_ref, gb_ref, y_ref, yt_ref):
        for k, (src, g_ref) in enumerate(((ya_ref, ga_ref), (yb_ref, gb_ref))):
            x = src[...]
            u = x * lax.rsqrt(jnp.mean(x * x, axis=-1, keepdims=True) + NORM_EPS) * g_ref[...]
            y_ref[:, k * c:(k + 1) * c] = u.astype(BF16)
            yt_ref[k * c:(k + 1) * c, :] = u.T.astype(BF16)

    row = pl.BlockSpec((ROW_TILE, c), lambda i: (i, 0))
    vec = pl.BlockSpec((1, c), lambda i: (0, 0))
    return _call(body, name=name, args=[ya, yb, ga, gb],
                 out_shape=(SDS((t, 2 * c), BF16), SDS((2 * c, t), BF16)), grid=(t // ROW_TILE,),
                 in_specs=[row, row, vec, vec],
                 out_specs=(pl.BlockSpec((ROW_TILE, 2 * c), lambda i: (i, 0)),
                            pl.BlockSpec((2 * c, ROW_TILE), lambda i: (0, i))))


def _mixnorm_bwd(dy, ya, yb, ga, gb, name):
    t, c = ya.shape

    def body(dy_ref, ya_ref, yb_ref, ga_ref, gb_ref, dya_ref, dyb_ref, dga_ref, dgb_ref):
        @pl.when(pl.program_id(0) == 0)
        def _():
            dga_ref[...] = jnp.zeros_like(dga_ref)
            dgb_ref[...] = jnp.zeros_like(dgb_ref)

        dxa, dga = _rms_bwd_math(ya_ref[...], ga_ref[...], dy_ref[:, :c])
        dxb, dgb = _rms_bwd_math(yb_ref[...], gb_ref[...], dy_ref[:, c:])
        dya_ref[...] = dxa
        dyb_ref[...] = dxb
        dga_ref[...] += dga
        dgb_ref[...] += dgb

    row = pl.BlockSpec((ROW_TILE, c), lambda i: (i, 0))
    vec = pl.BlockSpec((1, c), lambda i: (0, 0))
    return _call(body, name=name, args=[dy, ya, yb, ga, gb],
                 out_shape=(SDS((t, c), F32), SDS((t, c), F32), SDS((1, c), F32), SDS((1, c), F32)),
                 grid=(t // ROW_TILE,),
                 in_specs=[pl.BlockSpec((ROW_TILE, 2 * c), lambda i: (i, 0)), row, row, vec, vec],
                 out_specs=(row, row, vec, vec))


def _tile(n, want):
    return max(t for t in range(LANES, min(n, want) + 1, LANES) if n % t == 0)


def _mm(a, b, *, nt, out_dtype, tm, tn, name, residual=None, scale=None, take=None, out_rows=None, row_offset=0,
        into=None, job=None):
    parts = list(a) if isinstance(a, (list, tuple)) else [a]
    widths = [p.shape[-1] for p in parts]
    k = sum(widths)
    n = b.shape[0] if nt else b.shape[1]
    if take is None:
        m, which = parts[0].shape[0], lambda i: i
        tm = _tile(math.gcd(m, row_offset), tm)
    else:
        tm, tiles, which = take
        m = tm * tiles
    tn = _tile(n, tn)
    out_rows = m if out_rows is None else out_rows

    def body(*refs):
        a_refs, b_ref, rest = refs[:len(parts)], refs[len(parts)], refs[len(parts) + 1:]
        o_ref = rest[-1]
        out, at = None, 0
        for a_ref, width in zip(a_refs, widths):
            av = a_ref[...].astype(BF16)
            if nt:
                term = lax.dot_general(av, b_ref[:, at:at + width].astype(BF16), NT, preferred_element_type=F32)
            else:
                term = jnp.dot(av, b_ref[at:at + width, :].astype(BF16), preferred_element_type=F32)
            out = term if out is None else out + term
            at += width
        if residual is not None:
            out = rest[0][...] + (out if scale is None else scale * out)
        o_ref[...] = out.astype(out_dtype)

    a_specs = [pl.BlockSpec((tm, width), lambda i, j: (which(i), 0)) for width in widths]
    in_specs = a_specs + [pl.BlockSpec((tn, k), lambda i, j: (j, 0)) if nt else pl.BlockSpec((k, tn), lambda i, j: (0, j))]
    args, aliases = parts + [b], {}
    if residual is not None:
        in_specs.append(pl.BlockSpec((tm, tn), lambda i, j: (i, j)))
        args.append(residual)
    if into is not None:
        in_specs.append(ANY)
        aliases[len(args)] = 0
        args.append(into)
    return _call(body, name=name, args=args, out_shape=SDS((out_rows, n), out_dtype), grid=(m // tm, n // tn),
                 in_specs=in_specs, out_specs=pl.BlockSpec((tm, tn), lambda i, j: (row_offset // tm + i, j)),
                 aliases=aliases, job=job)


FFN_TM = 512
FFN_HB = 512
HIDDEN_TM = 1024


def _ffn_hidden(u, w_in_t, name, job=None):
    t, d = u.shape
    f = w_in_t.shape[0] // 2

    def body(u_ref, w_ref, g_ref, up_ref, hid_ref, hid_t_ref):
        uu = u_ref[...]
        g = lax.dot_general(uu, w_ref[0], NT, preferred_element_type=F32)
        up = lax.dot_general(uu, w_ref[1], NT, preferred_element_type=F32)
        g_ref[...] = g.astype(BF16)
        up_ref[...] = up.astype(BF16)
        hid = (g * _sigmoid(g)) * up
        hid_ref[...] = hid.astype(BF16)
        hid_t_ref[...] = hid.T.astype(BF16)

    tm = min(HIDDEN_TM, t)
    pre = pl.BlockSpec((tm, FFN_HB), lambda i, k: (i, k))
    return _call(body, name=name, args=[u, w_in_t.reshape(2, f, d)],
                 out_shape=(SDS((t, f), BF16), SDS((t, f), BF16), SDS((t, f), BF16), SDS((f, t), BF16)),
                 grid=(t // tm, f // FFN_HB),
                 in_specs=[pl.BlockSpec((tm, d), lambda i, k: (i, 0)),
                           pl.BlockSpec((2, FFN_HB, d), lambda i, k: (0, k, 0))],
                 out_specs=(pre, pre, pre, pl.BlockSpec((FFN_HB, tm), lambda i, k: (k, i))), job=job)


def _ffn_bwd(dfb, gpre, upre, w_in_t, w_out, name, job=None):
    t, d = dfb.shape
    f = w_out.shape[0]
    nk = f // FFN_HB

    def body(df_ref, g_ref, up_ref, w_ref, wo_ref, du_ref, da_t_ref, acc):
        k = pl.program_id(1)

        @pl.when(k == 0)
        def _():
            acc[...] = jnp.zeros_like(acc)

        dhid = lax.dot_general(df_ref[...], wo_ref[...], NT, preferred_element_type=F32)
        g, up = g_ref[...].astype(F32), up_ref[...].astype(F32)
        sig = _sigmoid(g)
        silu = g * sig
        dup = dhid * silu
        dg = dhid * up * (sig * (1.0 + g * (1.0 - sig)))
        da_t_ref[0] = dg.T.astype(BF16)
        da_t_ref[1] = dup.T.astype(BF16)
        acc[...] += (jnp.dot(dg.astype(BF16), w_ref[0], preferred_element_type=F32)
                     + jnp.dot(dup.astype(BF16), w_ref[1], preferred_element_type=F32))

        @pl.when(k == nk - 1)
        def _():
            du_ref[...] = acc[...]

    tok = pl.BlockSpec((FFN_TM, d), lambda i, k: (i, 0))
    pre = pl.BlockSpec((FFN_TM, FFN_HB), lambda i, k: (i, k))
    return _call(body, name=name, args=[dfb, gpre, upre, w_in_t.reshape(2, f, d), w_out],
                 out_shape=(SDS((t, d), F32), SDS((2, f, t), BF16)), grid=(t // FFN_TM, nk),
                 in_specs=[tok, pre, pre, pl.BlockSpec((2, FFN_HB, d), lambda i, k: (0, k, 0)),
                           pl.BlockSpec((FFN_HB, d), lambda i, k: (k, 0))],
                 out_specs=(tok, pl.BlockSpec((2, FFN_HB, FFN_TM), lambda i, k: (0, k, i))),
                 scratch_shapes=[pltpu.VMEM((FFN_TM, d), F32)], job=job)


CH = LANES
PAD = SUBLANES


def _lru_gates(xc, gw_ref, gb_ref, lam_ref, z):
    xcb = xc.astype(BF16)
    r = _sigmoid(jnp.dot(xcb, gw_ref[2 * z], preferred_element_type=F32) + gb_ref[pl.ds(2 * z, 1), :])
    i = _sigmoid(jnp.dot(xcb, gw_ref[2 * z + 1], preferred_element_type=F32) + gb_ref[pl.ds(2 * z + 1, 1), :])
    sp = _softplus(-lam_ref[pl.ds(z, 1), :])
    log_a = (-RG_C * r) * sp
    a = jnp.exp(log_a)
    mult = jnp.sqrt(-_expm1(2.0 * log_a))
    return r, i, sp, a, mult


def _conv(xpad, cw_ref, cb_ref, t):
    xc = cb_ref[...] + cw_ref[pl.ds(0, 1), :] * xpad[pl.ds(PAD - 2, t), :]
    for j in range(1, CONV_WIDTH):
        xc = xc + cw_ref[pl.ds(j, 1), :] * xpad[pl.ds(PAD - 2 + j, t), :]
    return xc


def _fill_padded(pad_ref, value, t):
    pad_ref[pl.ds(0, PAD), :] = jnp.zeros((PAD, CH), F32)
    pad_ref[pl.ds(PAD + t, PAD), :] = jnp.zeros((PAD, CH), F32)
    pad_ref[pl.ds(PAD, t), :] = value


def _scan_pair(t, a_up, b_up, out_up, a_down, b_down, out_down):
    row = lax.broadcasted_iota(jnp.int32, (SUBLANES, CH), 0)

    def compose(a, b, rising):
        for dist in (1, 2, 4):
            shift = dist if rising else SUBLANES - dist
            keep = (row >= dist) if rising else (row < SUBLANES - dist)
            b = jnp.where(keep, b + a * pltpu.roll(b, shift, axis=0), b)
            a = jnp.where(keep, a * pltpu.roll(a, shift, axis=0), a)
        return a, b

    def step(tt, carry):
        hu, hd = carry
        lo = pl.ds(pl.multiple_of(tt * SUBLANES, SUBLANES), SUBLANES)
        hi = pl.ds(pl.multiple_of(t - SUBLANES - tt * SUBLANES, SUBLANES), SUBLANES)
        a, b = compose(a_up[lo, :], b_up[lo, :], True)
        up = b + a * hu
        out_up[lo, :] = up
        a, b = compose(a_down[hi, :], b_down[hi, :], False)
        down = b + a * hd
        out_down[hi, :] = down
        return up[SUBLANES - 1:, :], down[:1, :]

    zero = jnp.zeros((1, CH), F32)
    lax.fori_loop(0, t // SUBLANES, step, (zero, zero), unroll=2)


def _lru_fwd(proj, cw, cb, gw, gb, lam, name, job=None):
    t = proj.shape[0]
    c = cw.shape[1]
    ncb = c // CH

    def body(x_ref, g_ref, cw_ref, cb_ref, gw_ref, gb_ref, lam_ref, ya_ref, hf_ref, hb_ref, xpad, a0, b0, a1, b1):
        _fill_padded(xpad, x_ref[...], t)
        xc = _conv(xpad, cw_ref, cb_ref, t)
        for z, (a_s, b_s) in enumerate(((a0, b0), (a1, b1))):
            _, i, _, a, mult = _lru_gates(xc, gw_ref, gb_ref, lam_ref, z)
            a_s[...] = a
            b_s[...] = mult * (i * xc)
        _scan_pair(t, a0, b0, hf_ref, a1, b1, hb_ref)
        gelu, _ = _gelu_parts(g_ref[...])
        ya_ref[...] = gelu * (hf_ref[...] + hb_ref[...])

    col = lambda off: pl.BlockSpec((t, CH), lambda i: (0, off + i))
    small = lambda rows: pl.BlockSpec((rows, CH), lambda i: (0, i))
    return _call(body, name=name, args=[proj, proj, cw, cb, gw, gb, lam], out_shape=(SDS((t, c), F32),) * 3,
                 grid=(ncb,),
                 in_specs=[col(0), col(ncb), small(CONV_WIDTH), small(1),
                           pl.BlockSpec((4, None, CH, CH), lambda i: (0, i, 0, 0)), small(4), small(2)],
                 out_specs=(col(0),) * 3,
                 scratch_shapes=[pltpu.VMEM((t + 2 * PAD, CH), F32)] + [pltpu.VMEM((t, CH), F32)] * 4, job=job)


def _lru_bwd(proj, cw, cb, gw, gb, lam, hf, hb, dya, name, job=None):
    t = proj.shape[0]
    c = cw.shape[1]
    ncb = c // CH

    def body(x_ref, g_ref, cw_ref, cb_ref, gw_ref, gb_ref, lam_ref, hf_ref, hb_ref, dya_ref,
             dx_ref, dg_ref, dt_ref, dcw_ref, dcb_ref, dgw_ref, dgb_ref, dlam_ref,
             xpad, hpad, dxc, a0, a1, dhs, dh0, dh1):
        _fill_padded(xpad, x_ref[...], t)
        xc = _conv(xpad, cw_ref, cb_ref, t)
        xcb = xc.astype(BF16)
        gates = [_lru_gates(xc, gw_ref, gb_ref, lam_ref, z) for z in range(2)]

        gelu, dgelu = _gelu_parts(g_ref[...])
        dya = dya_ref[...]
        dgate = dya * (hf_ref[...] + hb_ref[...]) * dgelu
        dg_ref[...] = dgate.astype(BF16)
        dt_ref[1] = dgate.T.astype(BF16)
        dhs[...] = dya * gelu

        _fill_padded(hpad, gates[0][3], t)
        a0[...] = hpad[pl.ds(PAD + 1, t), :]
        _fill_padded(hpad, gates[1][3], t)
        a1[...] = hpad[pl.ds(PAD - 1, t), :]
        _scan_pair(t, a1, dhs, dh1, a0, dhs, dh0)

        acc_dxc = jnp.zeros((t, CH), F32)
        for z, (h_ref, dh_ref, shift) in enumerate(((hf_ref, dh0, -1), (hb_ref, dh1, 1))):
            r, i, sp, a, mult = gates[z]
            _fill_padded(hpad, h_ref[...], t)
            h_nb = hpad[pl.ds(PAD + shift, t), :]
            db = dh_ref[...]
            da = db * h_nb
            d_i = db * mult * xc
            acc_dxc = acc_dxc + db * mult * i
            d_mult = db * i * xc
            d_la = da * a - d_mult * (a * a) / mult
            d_r = d_la * (-RG_C * sp)
            dlam_ref[pl.ds(z, 1), :] = (jnp.sum(d_la * (-RG_C * r), axis=0, keepdims=True)
                                        * (-_sigmoid(-lam_ref[pl.ds(z, 1), :])))
            for gate, d_pre in ((0, d_r * r * (1.0 - r)), (1, d_i * i * (1.0 - i))):
                zg = 2 * z + gate
                dgb_ref[pl.ds(zg, 1), :] = jnp.sum(d_pre, axis=0, keepdims=True)
                d_pre_b = d_pre.astype(BF16)
                dgw_ref[zg] = lax.dot_general(xcb, d_pre_b, TN, preferred_element_type=F32)
                acc_dxc = acc_dxc + lax.dot_general(d_pre_b, gw_ref[zg], NT, preferred_element_type=F32)

        dcb_ref[...] = jnp.sum(acc_dxc, axis=0, keepdims=True)
        for j in range(CONV_WIDTH):
            dcw_ref[pl.ds(j, 1), :] = jnp.sum(acc_dxc * xpad[pl.ds(PAD - 2 + j, t), :], axis=0, keepdims=True)
        _fill_padded(dxc, acc_dxc, t)
        dx = cw_ref[pl.ds(0, 1), :] * dxc[pl.ds(PAD + 2, t), :]
        for j in range(1, CONV_WIDTH):
            dx = dx + cw_ref[pl.ds(j, 1), :] * dxc[pl.ds(PAD + 2 - j, t), :]
        dx_ref[...] = dx.astype(BF16)
        dt_ref[0] = dx.T.astype(BF16)

    col = lambda off: pl.BlockSpec((t, CH), lambda i: (0, off + i))
    small = lambda rows: pl.BlockSpec((rows, CH), lambda i: (0, i))
    dense = pl.BlockSpec((4, None, CH, CH), lambda i: (0, i, 0, 0))
    padded = pltpu.VMEM((t + 2 * PAD, CH), F32)
    return _call(
        body, name=name, args=[proj, proj, cw, cb, gw, gb, lam, hf, hb, dya],
        out_shape=(SDS((t, c), BF16), SDS((t, c), BF16), SDS((2, c, t), BF16), SDS((CONV_WIDTH, c), F32),
                   SDS((1, c), F32), SDS((4, ncb, CH, CH), F32), SDS((4, c), F32), SDS((2, c), F32)),
        grid=(ncb,),
        in_specs=[col(0), col(ncb), small(CONV_WIDTH), small(1), dense, small(4), small(2), col(0), col(0), col(0)],
        out_specs=(col(0), col(0), pl.BlockSpec((2, CH, t), lambda i: (0, i, 0)), small(CONV_WIDTH), small(1),
                   dense, small(4), small(2)),
        scratch_shapes=[padded, padded, padded] + [pltpu.VMEM((t, CH), F32)] * 5, job=job)


Q_ROWS = 4
BAND_ROWS = WIN_ROWS + Q_ROWS
BAND_PAIRS = BAND_ROWS // 2
Q_BLOCK = Q_ROWS * GRID_W
BAND = BAND_ROWS * GRID_W
PAIR_W = 2 * GRID_W
N_BOTH = 2 * WIN_ROWS - 2
ENTRY_LEFT_OUT, ENTRY_RIGHT_OUT, ENTRY_OUT = N_BOTH, N_BOTH + 1, N_BOTH + 2
N_ENTRIES = N_BOTH + 3


def _bias_tables(rpb):
    cols = np.arange(GRID_W)
    start = np.clip(cols - WIN_COLS // 2, 0, GRID_W - WIN_COLS)
    valid = (cols[None, :] >= start[:, None]) & (cols[None, :] < start[:, None] + WIN_COLS)
    col_off = np.clip(cols[None, :] - cols[:, None] + WIN_COLS - 1, 0, 2 * WIN_COLS - 2)
    pick_col = jnp.asarray(np.eye(2 * WIN_COLS - 1, dtype=np.float32)[col_off] * valid[..., None])
    by_row = jnp.einsum("hrc,qkc->hrqk", rpb, pick_col, precision=lax.Precision.HIGHEST)
    by_row = jnp.where(jnp.asarray(valid)[None, None], by_row, NEG)
    out = jnp.full_like(by_row[:, :1], NEG)
    first_in, last_in = WIN_ROWS - 1 - WIN_ROWS // 2, 2 * (WIN_ROWS - 1) - WIN_ROWS // 2
    both = jnp.concatenate([by_row[:, :-1], by_row[:, 1:]], axis=-1)
    left_out = jnp.concatenate([out, by_row[:, first_in:first_in + 1]], axis=-1)
    right_out = jnp.concatenate([by_row[:, last_in:last_in + 1], out], axis=-1)
    return jnp.concatenate([both, left_out, right_out, jnp.concatenate([out, out], axis=-1)], axis=1)


def _band_start(m, rows):
    return jnp.clip(Q_ROWS * m - WIN_ROWS // 2, 0, rows - BAND_ROWS)


def _entry(r, key_row, rows):
    w0 = jnp.clip(r - WIN_ROWS // 2, 0, rows - WIN_ROWS)
    left = (key_row >= w0) & (key_row < w0 + WIN_ROWS)
    right = (key_row + 1 >= w0) & (key_row + 1 < w0 + WIN_ROWS)
    return jnp.where(left & right, key_row - r + WIN_ROWS - 1,
                     jnp.where(right, ENTRY_LEFT_OUT, jnp.where(left, ENTRY_RIGHT_OUT, ENTRY_OUT)))


def _transposed_pairs(dst, src_ref):
    for g in range(dst.shape[0]):
        dst[g] = src_ref[pl.ds(g * PAIR_W, PAIR_W), :].T.astype(BF16)


def _band_of(pairs_ref, first_pair, hh):
    heads = pl.ds(hh * HEAD_DIM, HEAD_DIM)
    return jnp.concatenate([pairs_ref[first_pair + g, heads, :] for g in range(BAND_PAIRS)], axis=1)


def _attn_block(qs, kt, tz_ref, hh, m, rows):
    rs = _band_start(m, rows)
    lanes = pl.ds(hh * HEAD_DIM, HEAD_DIM)
    qrows = pl.ds(pl.multiple_of(m * Q_BLOCK, Q_BLOCK), Q_BLOCK)
    band = pl.ds(pl.multiple_of(rs * GRID_W, PAIR_W), BAND)
    entries = [[_entry(Q_ROWS * m + i, rs + 2 * g, rows) for g in range(BAND_PAIRS)] for i in range(Q_ROWS)]
    bias = jnp.concatenate([jnp.concatenate([tz_ref[hh, e] for e in row], axis=1) for row in entries], axis=0)
    q = qs[qrows, lanes]
    s = jnp.dot(q, _band_of(kt, rs // 2, hh), preferred_element_type=F32) * (HEAD_DIM ** -0.5) + bias
    p = jnp.exp(s - jnp.max(s, axis=-1, keepdims=True))
    p = p / jnp.sum(p, axis=-1, keepdims=True)
    return q, p, qrows, band, lanes, entries, rs // 2


def _attn_fwd(proj, tables, width, name, job=None):
    t = proj.shape[0]
    rows = t // GRID_W
    npair = width // LANES
    first = (proj.shape[1] - 3 * width) // LANES

    def body(q_ref, k_ref, v_ref, tz_ref, o_ref, qs, vs, kt):
        qs[...] = q_ref[...].astype(BF16)
        vs[...] = v_ref[...].astype(BF16)
        _transposed_pairs(kt, k_ref)

        def block(m, carry):
            for hh in range(2):
                _, p, qrows, band, lanes, _, _ = _attn_block(qs, kt, tz_ref, hh, m, rows)
                o_ref[qrows, lanes] = jnp.dot(p.astype(BF16), vs[band, lanes], preferred_element_type=F32)
            return carry

        lax.fori_loop(0, rows // Q_ROWS, block, 0, unroll=2)

    col = lambda off: pl.BlockSpec((t, LANES), lambda i: (0, off + i))
    return _call(body, name=name, args=[proj, proj, proj, tables], out_shape=SDS((t, width), F32), grid=(npair,),
                 in_specs=[col(first), col(first + npair), col(first + 2 * npair),
                           pl.BlockSpec((2, N_ENTRIES, GRID_W, PAIR_W), lambda i: (i, 0, 0, 0))],
                 out_specs=col(0),
                 scratch_shapes=[pltpu.VMEM((t, LANES), BF16)] * 2 + [pltpu.VMEM((t // PAIR_W, LANES, PAIR_W), BF16)],
                 job=job)


def _attn_bwd(proj, tables, dyb, name, job=None):
    t, width = dyb.shape
    rows = t // GRID_W
    npair = width // LANES
    first = (proj.shape[1] - 3 * width) // LANES

    def body(q_ref, k_ref, v_ref, tz_ref, do_ref, dq_ref, dk_ref, dv_ref, dt_ref, dtz_ref, dq_s, dk_s, dv_s,
             qs, ks, vs, dos, kt, vt):
        qs[...] = q_ref[...].astype(BF16)
        ks[...] = k_ref[...].astype(BF16)
        vs[...] = v_ref[...].astype(BF16)
        dos[...] = do_ref[...].astype(BF16)
        _transposed_pairs(kt, k_ref)
        _transposed_pairs(vt, v_ref)
        dk_s[...] = jnp.zeros_like(dk_s)
        dv_s[...] = jnp.zeros_like(dv_s)
        dtz_ref[...] = jnp.zeros_like(dtz_ref)

        def block(m, carry):
            for hh in range(2):
                q, p, qrows, band, lanes, entries, first_pair = _attn_block(qs, kt, tz_ref, hh, m, rows)
                do = dos[qrows, lanes]
                dp = jnp.dot(do, _band_of(vt, first_pair, hh), preferred_element_type=F32)
                ds = p * (dp - jnp.sum(dp * p, axis=-1, keepdims=True))
                for i, row in enumerate(entries):
                    for g, e in enumerate(row):
                        dtz_ref[hh, e] += ds[i * GRID_W:(i + 1) * GRID_W, g * PAIR_W:(g + 1) * PAIR_W]
                dsb = (ds * (HEAD_DIM ** -0.5)).astype(BF16)
                dq_s[qrows, lanes] = jnp.dot(dsb, ks[band, lanes], preferred_element_type=F32)
                dk_s[band, lanes] += lax.dot_general(dsb, q, TN, preferred_element_type=F32)
                dv_s[band, lanes] += lax.dot_general(p.astype(BF16), do, TN, preferred_element_type=F32)
            return carry

        lax.fori_loop(0, rows // Q_ROWS, block, 0)
        for n, (src, dst) in enumerate(((dq_s, dq_ref), (dk_s, dk_ref), (dv_s, dv_ref))):
            val = src[...]
            dst[...] = val.astype(BF16)
            dt_ref[n] = val.T.astype(BF16)

    col = lambda off: pl.BlockSpec((t, LANES), lambda i: (0, off + i))
    table = pl.BlockSpec((2, N_ENTRIES, GRID_W, PAIR_W), lambda i: (i, 0, 0, 0))
    pairs = pltpu.VMEM((t // PAIR_W, LANES, PAIR_W), BF16)
    return _call(body, name=name, args=[proj, proj, proj, tables, dyb],
                 out_shape=(SDS((t, width), BF16),) * 3 + (SDS((3, width, t), BF16), SDS(tables.shape, F32)),
                 grid=(npair,),
                 in_specs=[col(first), col(first + npair), col(first + 2 * npair), table, col(0)],
                 out_specs=(col(0), col(0), col(0), pl.BlockSpec((3, LANES, t), lambda i: (0, i, 0)), table),
                 scratch_shapes=[pltpu.VMEM((t, LANES), F32)] * 3 + [pltpu.VMEM((t, LANES), BF16)] * 4 + [pairs, pairs],
                 job=job)


def _adamw_math(w, g, m, v):
    m = ADAM_B1 * m + (1.0 - ADAM_B1) * g
    v = ADAM_B2 * v + (1.0 - ADAM_B2) * (g * g)
    m_hat = m / (1.0 - ADAM_B1 ** ADAM_STEP)
    v_hat = v / (1.0 - ADAM_B2 ** ADAM_STEP)
    delta = -ADAM_LR * (m_hat / (jnp.sqrt(v_hat) + ADAM_EPS) + ADAM_WD * w)
    return delta, m, v


def _sum_partials(p_ref):
    g = p_ref[0].astype(F32)
    for s in range(1, N_CHIP):
        g = g + p_ref[s].astype(F32)
    return g


def _adamw_rows(w, partials, m, v, name):
    rb, n = w.shape
    tr = 64

    def body(w_ref, p_ref, m_ref, v_ref, g_ref, d_ref, nm_ref, nv_ref):
        g = _sum_partials(p_ref)
        g_ref[...] = g
        d_ref[...], nm_ref[...], nv_ref[...] = _adamw_math(w_ref[...], g, m_ref[...], v_ref[...])

    blk = pl.BlockSpec((tr, n), lambda i: (i, 0))
    return _call(body, name=name, args=[w, partials.reshape(N_CHIP, rb, n), m, v],
                 out_shape=(SDS((rb, n), F32),) * 4, grid=(rb // tr,),
                 in_specs=[blk, pl.BlockSpec((N_CHIP, tr, n), lambda i: (0, i, 0)), blk, blk], out_specs=(blk,) * 4)


def _adamw_cols(w, partials, m, v, name):
    d, nb = w.shape
    td = 256
    parts = list(partials) if isinstance(partials, (list, tuple)) else [partials]
    heights = [p.shape[0] // N_CHIP for p in parts]

    def body(w_ref, m_ref, v_ref, *rest):
        p_refs, (g_ref, d_ref, nm_ref, nv_ref) = rest[:len(parts)], rest[len(parts):]
        g = jnp.concatenate([_sum_partials(p_ref) for p_ref in p_refs], axis=0).T
        g_ref[...] = g
        d_ref[...], nm_ref[...], nv_ref[...] = _adamw_math(w_ref[...], g, m_ref[...], v_ref[...])

    blk = pl.BlockSpec((td, nb), lambda i: (i, 0))
    return _call(body, name=name, args=[w, m, v, *[p.reshape(N_CHIP, h, d) for p, h in zip(parts, heights)]],
                 out_shape=(SDS((d, nb), F32),) * 4, grid=(d // td,),
                 in_specs=[blk, blk, blk] + [pl.BlockSpec((N_CHIP, h, td), lambda i: (0, 0, i)) for h in heights],
                 out_specs=(blk,) * 4)


def _adamw_small(w, g, m, v, name):
    def body(w_ref, g_ref, m_ref, v_ref, d_ref, nm_ref, nv_ref):
        d_ref[...], nm_ref[...], nv_ref[...] = _adamw_math(w_ref[...], g_ref[...], m_ref[...], v_ref[...])

    return _call(body, name=name, args=[w, g, m, v], out_shape=(SDS(w.shape, F32),) * 3, in_specs=[WHOLE] * 4,
                 out_specs=(WHOLE,) * 3)


TILE = SUBLANES * LANES


def _pack(arrays):
    parts = []
    for a in arrays:
        flat = a.reshape(-1).astype(F32)
        flat = jnp.pad(flat, (0, -flat.size % TILE))
        parts.append(flat.reshape(-1, LANES))
    return jnp.concatenate(parts, axis=0)


def _unpack(pack, like):
    out, row = [], 0
    for a in like:
        n = int(np.prod(a.shape))
        nrows = -(-n // TILE) * SUBLANES
        out.append(pack[row:row + nrows].reshape(-1)[:n].reshape(a.shape))
        row += nrows
    return out


def _dense_gate_blocks(gate_w):
    w = gate_w.reshape(4, -1, 2, HEAD_DIM, HEAD_DIM)
    zero = jnp.zeros_like(w[:, :, 0])
    top = jnp.concatenate([w[:, :, 0], zero], axis=-1)
    bottom = jnp.concatenate([zero, w[:, :, 1]], axis=-1)
    return jnp.concatenate([top, bottom], axis=-2)


def _diag_gate_blocks(dense, shape):
    even = dense[:, :, :HEAD_DIM, :HEAD_DIM]
    odd = dense[:, :, HEAD_DIM:, HEAD_DIM:]
    return jnp.stack([even, odd], axis=2).reshape(shape)


LARGE = ("ffn1_w_in", "ffn1_w_out", "w_in_mix", "w_out_mix", "ffn2_w_in", "ffn2_w_out")
COLUMN_SHARDED = ("ffn1_w_in", "w_in_mix", "ffn2_w_in")
SHARDED_SMALL = ("lru_conv_w", "lru_lambda")
REPLICATED = ("norm_ffn1", "norm_mix", "lru_conv_b", "lru_gate_w", "lru_gate_b", "attn_rpb", "lru_out_norm",
              "attn_out_norm", "norm_ffn2", "norm_final")
SMALL_ORDER = REPLICATED + SHARDED_SMALL
WEIGHTS = ("norm_ffn1", "ffn1_w_in", "ffn1_w_out", "norm_mix", "w_in_mix", "lru_conv_w", "lru_conv_b", "lru_gate_w",
           "lru_gate_b", "lru_lambda", "attn_rpb", "lru_out_norm", "attn_out_norm", "w_out_mix", "norm_ffn2",
           "ffn2_w_in", "ffn2_w_out", "norm_final")


PARTS = {("gather", "w_in_mix"): 4, ("gather", "ffn2_w_in"): 8}
CARRIES = {
    "gather_ffn1_in": [(("gather", "ffn1_w_in"), 1), (("gather", "small"), 1)],
    "ffn1_hidden": [(("gather", "ffn1_w_out"), 1), (("gather", "w_in_mix"), 1)],
    "ffn1_out": [(("gather", "w_in_mix"), 3)],
    "mix_in_proj": [(("gather", "w_out_mix"), 1), (("gather", "ffn2_w_in"), 1)],
    "lru_fwd": [(("gather", "ffn2_w_in"), 3)],
    "attn_fwd": [(("gather", "ffn2_w_in"), 3)],
    "mix_out_proj": [(("gather", "ffn2_w_in"), 1)],
    "ffn2_hidden": [(("gather", "ffn2_w_out"), 1)],
    "ffn1_bwd": [(("gather", "small_grads"), 1)],
    "gather_late_grads": [(("gather", "late_grads"), 1)],
}


class _Transfer:
    def __init__(self, kind, src, dest, block_rows, parts):
        self.kind, self.src, self.dest = kind, src, dest
        self.ranges, self.taken = _split(block_rows, parts), 0

    def take(self, count):
        lo, hi = self.ranges[self.taken][0], self.ranges[self.taken + count - 1][1]
        self.taken += count
        return _Piece(self.kind, self.src, self.dest, lo, hi)


class _Traffic:
    def __init__(self):
        self.transfers = {}

    def open(self, kind, name, src, placed=None):
        dest = _gathered(src) if placed is None else placed
        self.transfers[kind, name] = _Transfer(kind, src, dest, dest.shape[0] // N_DEV, PARTS.get((kind, name), 1))

    def _job(self, host):
        moved = [self.transfers[key] for key, _ in CARRIES[host]]
        return moved, _Job([tr.take(count) for tr, (_, count) in zip(moved, CARRIES[host])])

    def carry(self, host, fn, *args, **kw):
        if host not in CARRIES:
            return fn(*args, name=host, **kw)
        moved, job = self._job(host)
        res, landed = fn(*args, name=host, job=job, **kw)
        for tr, arr in zip(moved, landed):
            tr.dest = arr
        return res

    def alone(self, host):
        moved, job = self._job(host)
        for tr, arr in zip(moved, _run_job(job, host)):
            tr.dest = arr

    def result(self, kind, name):
        tr = self.transfers.pop((kind, name))
        assert tr.taken == len(tr.ranges), (kind, name)
        return tr.dest


def _forward_backward(x, target, shards, sharded_small, s):
    c = s["lru_conv_b"].shape[1]
    width = s["attn_out_norm"].shape[1]
    t = x.shape[0]
    traffic = _Traffic()
    carry = traffic.carry
    weight = lambda n: traffic.result("gather", n)

    for n in LARGE:
        traffic.open("gather", n, None, placed=shards[n])
    traffic.open("gather", "small", sharded_small)
    traffic.alone("gather_ffn1_in")
    full_small = weight("small").reshape(N_DEV, SUBLANES, c // N_DEV)
    conv_w = full_small[:, :CONV_WIDTH].transpose(1, 0, 2).reshape(CONV_WIDTH, c)
    lam = full_small[:, CONV_WIDTH:CONV_WIDTH + 2].transpose(1, 0, 2).reshape(2, c)
    w = {"ffn1_w_in": weight("ffn1_w_in")}
    ffn_out = dict(nt=False, out_dtype=F32, tm=1024, tn=512, scale=0.5)
    u1 = _rmsnorm_fwd(x, s["norm_ffn1"], "norm_ffn1")
    g1, up1, hid1, hid1_t = carry("ffn1_hidden", _ffn_hidden, u1, w["ffn1_w_in"])
    w["ffn1_w_out"] = weight("ffn1_w_out")
    h1 = carry("ffn1_out", _mm, hid1, w["ffn1_w_out"], residual=x, **ffn_out)
    w["w_in_mix"] = weight("w_in_mix")
    u2 = _rmsnorm_fwd(h1, s["norm_mix"], "norm_mix")
    proj = carry("mix_in_proj", _mm, u2, w["w_in_mix"], nt=True, out_dtype=F32, tm=512, tn=512)
    w["w_out_mix"] = weight("w_out_mix")
    gw = _dense_gate_blocks(s["lru_gate_w"]).astype(BF16)
    gb = s["lru_gate_b"].reshape(4, c)
    tables, tables_vjp = jax.vjp(_bias_tables, s["attn_rpb"])
    ya, hf, hb = carry("lru_fwd", _lru_fwd, proj, conv_w, s["lru_conv_b"], gw, gb, lam)
    yb = carry("attn_fwd", _attn_fwd, proj, tables, width)
    y, yt = _mixnorm_fwd(ya, yb, s["lru_out_norm"], s["attn_out_norm"], "mix_norm")
    h2 = carry("mix_out_proj", _mm, y, w["w_out_mix"], nt=False, out_dtype=F32, tm=512, tn=512, residual=h1)
    u3 = _rmsnorm_fwd(h2, s["norm_ffn2"], "norm_ffn2")
    w["ffn2_w_in"] = weight("ffn2_w_in")
    g2, up2, hid2, hid2_t = carry("ffn2_hidden", _ffn_hidden, u3, w["ffn2_w_in"])
    w["ffn2_w_out"] = weight("ffn2_w_out")
    h3 = carry("ffn2_out", _mm, hid2, w["ffn2_w_out"], residual=h2, **ffn_out)
    dh3, df2, loss_part, d_norm_final = _final_loss(h3, s["norm_final"], target, "final_loss")

    grads = {}
    grad_of = dict(nt=False, out_dtype=BF16, tm=512, tn=1024)

    to_sibling, to_chips = {}, {}

    def reduce_in_chip(n):
        land = _blank_like(grads[n], grads[n].shape[0] // 2, "landing_" + n)
        to_sibling[n], token = _split_start("to_sibling", grads[n], land, "to_sibling_" + n)
        RUN_AFTER.append(token)

    def reduce_over_chips(n, after):
        own, got = _split_wait(to_sibling.pop(n), [after], "from_sibling_" + n)
        summed = _pair_sum(own, got, "pair_sum_" + n)
        to_chips[n], token = _split_start("to_chips", summed, _own_slot(summed, "own_slot_" + n), "to_chips_" + n)
        RUN_AFTER.append(token)
        return token

    f = hid2_t.shape[0]
    grads["ffn2_w_out"] = carry("ffn2_out_grad", _mm, hid2_t, df2, **grad_of)
    reduce_in_chip("ffn2_w_out")
    du3, da2_t = carry("ffn2_bwd", _ffn_bwd, df2, g2, up2, w["ffn2_w_in"], w["ffn2_w_out"])
    reduce_over_chips("ffn2_w_out", du3)
    grads["ffn2_w_in"] = carry("ffn2_in_grad", _mm, da2_t.reshape(2 * f, t), u3, **grad_of)
    reduce_in_chip("ffn2_w_in")
    dh2, dh2b, d_norm_ffn2 = carry("norm_ffn2_bwd", _rmsnorm_bwd, du3, h2, s["norm_ffn2"], dh3, 1.0)
    grads["w_out_mix"] = carry("mix_out_grad", _mm, yt, dh2b, **grad_of)
    reduce_over_chips("ffn2_w_in", grads["w_out_mix"])
    reduce_in_chip("w_out_mix")
    dy = carry("mix_out_bwd", _mm, dh2b, w["w_out_mix"], nt=True, out_dtype=F32, tm=512, tn=512)
    dya, dyb, d_lru_out_norm, d_attn_out_norm = _mixnorm_bwd(dy, ya, yb, s["lru_out_norm"], s["attn_out_norm"],
                                                             "mix_norm_bwd")
    dq, dk, dv, dqkv_t, d_tables = carry("attn_bwd", _attn_bwd, proj, tables, dyb)
    reduce_over_chips("w_out_mix", dq)
    dx_lru, dg_lru, dxg_t, d_conv_w, d_conv_b, d_gw, d_gb, d_lam = carry(
        "lru_bwd", _lru_bwd, proj, conv_w, s["lru_conv_b"], gw, gb, lam, hf, hb, dya)
    rows_of = 2 * c + 3 * width
    lru_rows = carry("mix_in_grad_lru", _mm, dxg_t.reshape(2 * c, t), u2, out_rows=rows_of, **grad_of)
    grads["w_in_mix"] = carry("mix_in_grad_attn", _mm, dqkv_t.reshape(3 * width, t), u2, out_rows=rows_of,
                              row_offset=2 * c, into=lru_rows, **grad_of)
    reduce_in_chip("w_in_mix")
    du2 = carry("mix_in_bwd", _mm, [dx_lru, dg_lru, dq, dk, dv], w["w_in_mix"], nt=False, out_dtype=F32, tm=512,
                tn=512)
    dh1, df1, d_norm_mix = carry("norm_mix_bwd", _rmsnorm_bwd, du2, h1, s["norm_mix"], dh2, 0.5)
    reduce_over_chips("w_in_mix", dh1)

    by_device = lambda a: a.reshape(a.shape[0], N_DEV, -1).transpose(1, 0, 2)
    small = {
        "norm_mix": d_norm_mix, "lru_conv_b": d_conv_b, "lru_gate_w": _diag_gate_blocks(d_gw, s["lru_gate_w"].shape),
        "lru_gate_b": d_gb.reshape(s["lru_gate_b"].shape), "attn_rpb": tables_vjp(d_tables)[0],
        "lru_out_norm": d_lru_out_norm, "attn_out_norm": d_attn_out_norm, "norm_ffn2": d_norm_ffn2,
        "norm_final": d_norm_final, "lru_conv_w": by_device(d_conv_w), "lru_lambda": by_device(d_lam),
    }
    early = [small[n] for n in SMALL_ORDER[1:]]
    traffic.open("gather", "small_grads", _pack(early))

    grads["ffn1_w_out"] = carry("ffn1_out_grad", _mm, hid1_t, df1, **grad_of)
    reduce_in_chip("ffn1_w_out")
    du1, da1_t = carry("ffn1_bwd", _ffn_bwd, df1, g1, up1, w["ffn1_w_in"], w["ffn1_w_out"])
    grad_x, _, d_norm_ffn1 = carry("norm_ffn1_bwd", _rmsnorm_bwd, du1, x, s["norm_ffn1"], dh1, 1.0)
    traffic.open("gather", "late_grads", _pack([d_norm_ffn1]))
    traffic.alone("gather_late_grads")
    late = traffic.result("gather", "late_grads")
    reduce_over_chips("ffn1_w_out", late)
    half = 2 * f // N_DEV // 2
    half_rows = lambda h: (half, N_DEV, lambda i: 2 * i + h)
    grads["ffn1_w_in_a"] = carry("ffn1_in_grad_a", _mm, da1_t.reshape(2 * f, t), u1, take=half_rows(0), **grad_of)
    reduce_in_chip("ffn1_w_in_a")
    grads["ffn1_w_in_b"] = carry("ffn1_in_grad_b", _mm, da1_t.reshape(2 * f, t), u1, take=half_rows(1), **grad_of)
    reduce_over_chips("ffn1_w_in_a", grads["ffn1_w_in_b"])
    reduce_in_chip("ffn1_w_in_b")
    reduced = (_unpack(_sum_devices(late, "sum_late_grads"), [d_norm_ffn1])
               + _unpack(_sum_devices(traffic.result("gather", "small_grads"), "sum_small_grads"), early))
    last_token = reduce_over_chips("ffn1_w_in_b", reduced[1])
    RUN_AFTER.clear()
    assert not traffic.transfers and not to_sibling, (list(traffic.transfers), list(to_sibling))
    return loss_part[0, 0], grad_x, to_chips, last_token, dict(zip(SMALL_ORDER, reduced))


def _step(x, loss_target, p, m, v):
    me = 4 * lax.axis_index("x") + 2 * lax.axis_index("y") + lax.axis_index("c")

    shards = {n: _cast_into_place(p[n], n in COLUMN_SHARDED, "cast_" + n) for n in LARGE}
    sharded_small = (jnp.pad(p["lru_conv_w"], ((0, SUBLANES - CONV_WIDTH), (0, 0)))
                     + jnp.pad(p["lru_lambda"], ((CONV_WIDTH, SUBLANES - CONV_WIDTH - 2), (0, 0))))
    s = {n: p[n] if n in ("lru_gate_w", "lru_gate_b", "attn_rpb") else p[n].reshape(1, -1) for n in REPLICATED}

    loss_part, grad_x, to_chips, last_token, small = _forward_backward(x, loss_target, shards, sharded_small, s)
    loss = lax.psum(loss_part, ("x", "y", "c"))

    def landed(n, after):
        return _split_wait(to_chips[n], after, "from_chips_" + n)[1]

    def update(n, partials):
        return (_adamw_cols if n in COLUMN_SHARDED else _adamw_rows)(p[n], partials, m[n], v[n], "adamw_" + n)

    out = {n: update(n, landed(n, [last_token])) for n in LARGE if n != "ffn1_w_in"}
    done = [o[3] for o in out.values()]
    out["ffn1_w_in"] = update("ffn1_w_in", [landed("ffn1_w_in_a", done), landed("ffn1_w_in_b", done)])

    g_small = {n: lax.dynamic_index_in_dim(g, me, axis=0, keepdims=False) if n in SHARDED_SMALL else g
               for n, g in small.items()}
    names = SMALL_ORDER
    like = [p[n] for n in names]
    pack_of = lambda d: _pack([d[n].reshape(p[n].shape) for n in names])
    upd = _adamw_small(pack_of(p), pack_of(g_small), pack_of(m), pack_of(v), "adamw_small")
    for n, d_, m_, v_ in zip(names, *[_unpack(u, like) for u in upd]):
        out[n] = (g_small[n].reshape(p[n].shape), d_, m_, v_)
    return loss, grad_x, out


def kernel(x, norm_ffn1, ffn1_w_in, ffn1_w_out, norm_mix, w_in_mix, lru_conv_w, lru_conv_b, lru_gate_w, lru_gate_b, lru_lambda, attn_rpb, lru_out_norm, attn_out_norm, w_out_mix, norm_ffn2, ffn2_w_in, ffn2_w_out, norm_final, loss_target, m_norm_ffn1, m_ffn1_w_in, m_ffn1_w_out, m_norm_mix, m_w_in_mix, m_lru_conv_w, m_lru_conv_b, m_lru_gate_w, m_lru_gate_b, m_lru_lambda, m_attn_rpb, m_lru_out_norm, m_attn_out_norm, m_w_out_mix, m_norm_ffn2, m_ffn2_w_in, m_ffn2_w_out, m_norm_final, v_norm_ffn1, v_ffn1_w_in, v_ffn1_w_out, v_norm_mix, v_w_in_mix, v_lru_conv_w, v_lru_conv_b, v_lru_gate_w, v_lru_gate_b, v_lru_lambda, v_attn_rpb, v_lru_out_norm, v_attn_out_norm, v_w_out_mix, v_norm_ffn2, v_ffn2_w_in, v_ffn2_w_out, v_norm_final):
    given = dict(locals())
    drop_layer = lambda n, a: a if n == "norm_final" else a[0]
    p = {n: drop_layer(n, given[n]) for n in WEIGHTS}
    m = {n: drop_layer(n, given["m_" + n]) for n in WEIGHTS}
    v = {n: drop_layer(n, given["v_" + n]) for n in WEIGHTS}
    loss, grad_x, out = _step(x[0], loss_target[0], p, m, v)
    shaped = lambda n, a: a.reshape(given[n].shape)
    return (loss, grad_x[None], *[shaped(n, out[n][k]) for k in range(4) for n in WEIGHTS])
```

```python
import math

import numpy as np
import jax
import jax.numpy as jnp
from jax import lax
from jax.experimental import pallas as pl
from jax.experimental.pallas import tpu as pltpu

F32 = jnp.float32
BF16 = jnp.bfloat16
SDS = jax.ShapeDtypeStruct

N_DEV = 8
N_CHIP = 4
NORM_EPS = 1e-6
RG_C = 8.0
CONV_WIDTH = 4
HEAD_DIM = 64
GRID_W = 64
WIN_ROWS = 8
WIN_COLS = 16
NEG = -1e30

ADAM_LR = 0.001
ADAM_B1 = 0.9
ADAM_B2 = 0.999
ADAM_EPS = 1e-08
ADAM_WD = 0.01
ADAM_STEP = 10

LANES = 128
SUBLANES = 8
VMEM_LIMIT = 56 * 1024 * 1024

NT = (((1,), (1,)), ((), ()))
TN = (((0,), (0,)), ((), ()))
ANY = pl.BlockSpec(memory_space=pl.ANY)
WHOLE = pl.BlockSpec(memory_space=pltpu.VMEM)
MESH = pl.DeviceIdType.MESH


def _sigmoid(x):
    return 1.0 / (1.0 + jnp.exp(-x))


def _gelu_parts(x):
    c = math.sqrt(2.0 / math.pi)
    t = jnp.tanh(c * (x + 0.044715 * (x * x * x)))
    gelu = 0.5 * x * (1.0 + t)
    dgelu = 0.5 * (1.0 + t) + 0.5 * x * (1.0 - t * t) * (c * (1.0 + 3.0 * 0.044715 * (x * x)))
    return gelu, dgelu


def _expm1(x):
    poly = x * (1.0 + x * (1.0 / 2) * (1.0 + x * (1.0 / 3) * (1.0 + x * (1.0 / 4) * (1.0 + x * (1.0 / 5) * (1.0 + x * (1.0 / 6))))))
    return jnp.where(jnp.abs(x) < 0.25, poly, jnp.exp(x) - 1.0)


def _softplus(x):
    return jnp.maximum(x, 0.0) + jnp.log1p(jnp.exp(-jnp.abs(x)))


class _Piece:
    N_REMOTE = {"gather": 7}
    N_LOCAL = {"gather": 1}

    def __init__(self, kind, src, dest, lo, hi):
        self.kind, self.src, self.dest, self.lo, self.hi = kind, src, dest, lo, hi


RELAY_AT = 60
RUN_AFTER = []


class _Job:
    def __init__(self, pieces):
        self.pieces = list(pieces)
        self.ins = [p.src for p in self.pieces if p.src is not None]
        self.out_shapes = [SDS(p.dest.shape, p.dest.dtype) for p in self.pieces]
        self.aliased = [i for i, p in enumerate(self.pieces) if not isinstance(p.dest, SDS)]
        self.n_remote = sum(_Piece.N_REMOTE[p.kind] for p in self.pieces)
        self.n_local = max(sum(_Piece.N_LOCAL[p.kind] for p in self.pieces), 1)

    def _each(self, step, ins, outs, send_sems, recv_sems, local_sems):
        remote = local = 0
        ins = iter(ins)
        for p, dst in zip(self.pieces, outs):
            src = None if p.src is None else next(ins)
            _EXCHANGES[p.kind](step, p, src, dst, send_sems, recv_sems, local_sems, remote, local)
            remote += _Piece.N_REMOTE[p.kind]
            local += _Piece.N_LOCAL[p.kind]

    def start(self, *refs):
        self._each("start", *refs)

    def relay(self, *refs):
        self._each("relay", *refs)

    def finish(self, *refs):
        self._each("finish", *refs)


def _call(body, *, name, args, out_shape, in_specs, out_specs, grid=(), scratch_shapes=(), aliases=None, job=None):
    single = not isinstance(out_shape, (tuple, list))
    out_shape = (out_shape,) if single else tuple(out_shape)
    out_specs = (out_specs,) if single else tuple(out_specs)
    aliases = dict(aliases or {})
    if RUN_AFTER:
        tokens, n_plain, plain_body = list(RUN_AFTER), len(args), body
        RUN_AFTER.clear()
        body = lambda *refs: plain_body(*refs[:n_plain], *refs[n_plain + len(tokens):])
        args, in_specs = list(args) + tokens, list(in_specs) + [ANY] * len(tokens)
    params = pltpu.CompilerParams(dimension_semantics=("arbitrary",) * len(grid) if grid else None,
                                  vmem_limit_bytes=VMEM_LIMIT)
    if job is None:
        res = pl.pallas_call(body, out_shape=out_shape, grid=grid, in_specs=list(in_specs), out_specs=out_specs,
                             scratch_shapes=list(scratch_shapes), input_output_aliases=aliases, name=name,
                             compiler_params=params)(*args)
        return res[0] if single else res

    n_in, n_out, n_scr = len(args), len(out_shape), len(scratch_shapes)
    j_in, j_out, j_alias = len(job.ins), len(job.out_shapes), len(job.aliased)

    def hosted(*refs):
        ins, refs = refs[:n_in], refs[n_in:]
        j_ins, refs = refs[:j_in], refs[j_in + j_alias:]
        outs, refs = refs[:n_out], refs[n_out:]
        j_outs, refs = refs[:j_out], refs[j_out:]
        scr, sems = refs[:n_scr], refs[n_scr:]
        if grid:
            step = 0
            for axis, size in enumerate(grid):
                step = step * size + pl.program_id(axis)
            steps = math.prod(grid)
            pl.when(step == 0)(lambda: job.start(j_ins, j_outs, *sems))
            body(*ins, *outs, *scr)
            pl.when(step == min(RELAY_AT * steps // 100, steps - 1))(lambda: job.relay(j_ins, j_outs, *sems))
            pl.when(step == steps - 1)(lambda: job.finish(j_ins, j_outs, *sems))
        else:
            job.start(j_ins, j_outs, *sems)
            body(*ins, *outs, *scr)
            job.relay(j_ins, j_outs, *sems)
            job.finish(j_ins, j_outs, *sems)

    res = pl.pallas_call(
        hosted, out_shape=out_shape + tuple(job.out_shapes), grid=grid,
        in_specs=list(in_specs) + [ANY] * (j_in + j_alias), out_specs=out_specs + (ANY,) * j_out,
        scratch_shapes=list(scratch_shapes) + [pltpu.SemaphoreType.DMA((job.n_remote,)),
                                               pltpu.SemaphoreType.DMA((job.n_remote,)),
                                               pltpu.SemaphoreType.DMA((job.n_local,))],
        input_output_aliases={**aliases, **{n_in + j_in + k: n_out + i for k, i in enumerate(job.aliased)}},
        name=name, compiler_params=params)(*args, *job.ins, *[job.pieces[i].dest for i in job.aliased])
    own, carried = res[:n_out], res[n_out:]
    return (own[0] if single else own), carried


def _run_job(job, name):
    return _call(lambda: None, name=name, args=[], out_shape=(), in_specs=[], out_specs=(), job=job)[1]


def _position():
    return lax.axis_index("x"), lax.axis_index("y"), lax.axis_index("c")


def _flat(px, py, pc):
    return 4 * px + 2 * py + pc


def _gather_exchange(step, p, src, dst, send_sems, recv_sems, local_sems, r0, l0):
    x, y, c = _position()
    me, sibling = (x, y, c), (x, y, 1 - c)
    along_x, along_y, diagonal = (1 - x, y), (x, 1 - y), (1 - x, 1 - y)
    south = c == 0
    passed_on = (jnp.where(south, 1 - x, x), jnp.where(south, y, 1 - y))
    passed_to = (jnp.where(south, x, 1 - x), jnp.where(south, 1 - y, y))
    placed = p.src is None
    rb, n_rows = p.dest.shape[0] // N_DEV, p.hi - p.lo

    def rows(block):
        return dst.at[pl.ds(_flat(*block) * rb + p.lo, n_rows), :]

    mine = rows(me) if placed else src.at[pl.ds(p.lo, n_rows), :]

    def copy(k, block, to, own=False):
        return pltpu.make_async_remote_copy(
            src_ref=mine if own else rows(block), dst_ref=rows(block),
            send_sem=send_sems.at[r0 + k], recv_sem=recv_sems.at[r0 + k], device_id=to, device_id_type=MESH)

    local = None if placed else pltpu.make_async_copy(mine, rows(me), local_sems.at[l0])
    if step == "start":
        if local is not None:
            local.start()
        copy(0, me, sibling, own=True).start()
        copy(1, me, (*along_x, c), own=True).start()
        copy(2, me, (*along_y, c), own=True).start()
    elif step == "relay":
        copy(1, (*along_x, c), me).wait_recv()
        copy(2, (*along_y, c), me).wait_recv()
        copy(3, (*passed_on, c), (*passed_to, c)).start()
        copy(4, (*along_x, c), sibling).start()
        copy(5, (*along_y, c), sibling).start()
    else:
        copy(3, (*diagonal, c), me).wait_recv()
        copy(6, (*diagonal, c), sibling).start()
        copy(0, sibling, me).wait_recv()
        copy(4, (*along_x, 1 - c), me).wait_recv()
        copy(5, (*along_y, 1 - c), me).wait_recv()
        copy(6, (*diagonal, 1 - c), me).wait_recv()
        copy(0, me, sibling, own=True).wait_send()
        copy(1, me, (*along_x, c), own=True).wait_send()
        copy(2, me, (*along_y, c), own=True).wait_send()
        copy(3, (*passed_on, c), (*passed_to, c)).wait_send()
        copy(4, (*along_x, c), sibling).wait_send()
        copy(5, (*along_y, c), sibling).wait_send()
        copy(6, (*diagonal, c), sibling).wait_send()
        if local is not None:
            local.wait()


CHIP_FLIPS = [(1, 0), (0, 1), (1, 1)]
_EXCHANGES = {"gather": _gather_exchange}


def _gathered(shard):
    return SDS((N_DEV * shard.shape[0], shard.shape[1]), shard.dtype)


def _split(rows, parts):
    cuts = [rows * k // parts // 16 * 16 for k in range(parts)] + [rows]
    return list(zip(cuts[:-1], cuts[1:]))


def _pair_sum(g, from_sibling, name):
    rb, n = g.shape[0] // N_DEV, g.shape[1]
    tr = rb if rb * n * 2 <= 3 * 1024 * 1024 else rb // 2
    core = lax.axis_index("c").astype(jnp.int32).reshape(1)

    def body(c_ref, g_ref, r_ref, o_ref):
        o_ref[...] = (g_ref[...].astype(F32) + r_ref[...].astype(F32)).astype(BF16)

    grid_spec = pltpu.PrefetchScalarGridSpec(
        num_scalar_prefetch=1, grid=(N_CHIP, rb // tr),
        in_specs=[pl.BlockSpec((None, None, tr, n), lambda q, i, c_ref: (q, c_ref[0], i, 0)),
                  pl.BlockSpec((None, tr, n), lambda q, i, c_ref: (q, i, 0))],
        out_specs=pl.BlockSpec((None, tr, n), lambda q, i, c_ref: (q, i, 0)))
    out = pl.pallas_call(
        body, grid_spec=grid_spec, out_shape=SDS((N_CHIP, rb, n), BF16), name=name,
        compiler_params=pltpu.CompilerParams(dimension_semantics=("arbitrary",) * 2, vmem_limit_bytes=VMEM_LIMIT))(
            core, g.reshape(N_CHIP, 2, rb, n), from_sibling.reshape(N_CHIP, rb, n))
    return out.reshape(N_CHIP * rb, n)


SEM = pl.BlockSpec(memory_space=pltpu.SEMAPHORE)
IN_HBM = pl.BlockSpec(memory_space=pltpu.HBM)
SIDE_EFFECT = pltpu.SideEffectType.DATAFLOW_SIDE_EFFECTING


def _own_slot(partials, name):
    rb, n = partials.shape[0] // N_CHIP, partials.shape[1]
    tr = rb // 2
    chip = (2 * lax.axis_index("x") + lax.axis_index("y")).astype(jnp.int32).reshape(1)

    def body(chip_ref, src_ref, dst_ref):
        dst_ref[...] = src_ref[...]

    block = pl.BlockSpec((None, tr, n), lambda i, chip_ref: (chip_ref[0], i, 0))
    grid_spec = pltpu.PrefetchScalarGridSpec(num_scalar_prefetch=1, grid=(rb // tr,), in_specs=[block], out_specs=block)
    out = pl.pallas_call(
        body, grid_spec=grid_spec, out_shape=SDS((N_CHIP, rb, n), partials.dtype), name=name,
        compiler_params=pltpu.CompilerParams(dimension_semantics=("arbitrary",), vmem_limit_bytes=VMEM_LIMIT))(
            chip, partials.reshape(N_CHIP, rb, n))
    return out.reshape(partials.shape)


def _blank_like(src, rows, name):
    return pl.pallas_call(lambda src_ref, out_ref: None, out_shape=SDS((rows, src.shape[1]), src.dtype),
                          in_specs=[ANY], out_specs=ANY, name=name)(src)


def _chip_copies(src_ref, land_ref, sems):
    x, y, c = _position()
    rb = src_ref.shape[0] // N_CHIP
    copies = []
    for k, (fx, fy) in enumerate(CHIP_FLIPS):
        px, py = (1 - x if fx else x), (1 - y if fy else y)
        copies.append(pltpu.make_async_remote_copy(
            src_ref=src_ref.at[pl.ds((2 * px + py) * rb, rb), :], dst_ref=land_ref.at[pl.ds((2 * x + y) * rb, rb), :],
            send_sem=sems[2 * k], recv_sem=sems[2 * k + 1], device_id=(px, py, c), device_id_type=MESH))
    return copies


def _sibling_copies(src_ref, land_ref, sems):
    x, y, c = _position()
    rb = src_ref.shape[0] // N_DEV
    return [pltpu.make_async_remote_copy(
        src_ref=src_ref.at[pl.ds((2 * q + 1 - c) * rb, rb), :], dst_ref=land_ref.at[pl.ds(q * rb, rb), :],
        send_sem=sems[2 * q], recv_sem=sems[2 * q + 1], device_id=(x, y, 1 - c), device_id_type=MESH)
        for q in range(N_CHIP)]


SPLIT_COPIES = {"to_chips": (_chip_copies, 3), "to_sibling": (_sibling_copies, N_CHIP)}


def _split_start(kind, src, land, name):
    copies_of, n_copies = SPLIT_COPIES[kind]

    def body(src_ref, land_ref, *rest):
        sems, token = rest[:2 * n_copies], rest[-1]
        for copy in copies_of(src_ref, land_ref, sems):
            copy.start()
        token[...] = jnp.zeros_like(token)

    res = pl.pallas_call(
        body, name=name,
        out_shape=(pltpu.SemaphoreType.DMA(()),) * (2 * n_copies)
        + (pltpu.HBM(src.shape, src.dtype), pltpu.HBM(land.shape, land.dtype), SDS((SUBLANES, LANES), F32)),
        in_specs=(IN_HBM, IN_HBM), out_specs=(SEM,) * (2 * n_copies) + (IN_HBM, IN_HBM, WHOLE),
        input_output_aliases={0: 2 * n_copies, 1: 2 * n_copies + 1},
        compiler_params=pltpu.CompilerParams(has_side_effects=SIDE_EFFECT))(
            pltpu.with_memory_space_constraint(src, pltpu.HBM), pltpu.with_memory_space_constraint(land, pltpu.HBM))
    return (kind, res[:2 * n_copies], res[-3], res[-2]), res[-1]


def _split_wait(pending, after, name):
    kind, sems, src, land = pending
    copies_of, n_copies = SPLIT_COPIES[kind]

    def body(src_ref, land_ref, *rest):
        for copy in copies_of(src_ref, land_ref, rest[:2 * n_copies]):
            copy.wait_send()
            copy.wait_recv()

    return pl.pallas_call(
        body, name=name, out_shape=(pltpu.HBM(src.shape, src.dtype), pltpu.HBM(land.shape, land.dtype)),
        in_specs=(IN_HBM, IN_HBM) + (SEM,) * (2 * n_copies) + (ANY,) * len(after), out_specs=(IN_HBM, IN_HBM),
        input_output_aliases={0: 0, 1: 1},
        compiler_params=pltpu.CompilerParams(has_side_effects=SIDE_EFFECT))(src, land, *sems, *after)


def _sum_devices(gathered, name):
    r = gathered.shape[0] // N_DEV

    def body(g_ref, o_ref):
        acc = g_ref[0]
        for s in range(1, N_DEV):
            acc = acc + g_ref[s]
        o_ref[...] = acc

    return _call(body, name=name, args=[gathered.reshape(N_DEV, r, LANES)], out_shape=SDS((r, LANES), F32),
                 in_specs=[WHOLE], out_specs=WHOLE)


def _cast_into_place(w, transposed, name):
    me = _flat(*_position()).astype(jnp.int32).reshape(1)
    if transposed:
        d, rb = w.shape
        td = 512
        grid = (d // td,)
        in_spec = pl.BlockSpec((td, rb), lambda i, me_ref: (i, 0))
        out_spec = pl.BlockSpec((rb, td), lambda i, me_ref: (me_ref[0], i))
    else:
        rb, d = w.shape
        grid = (1,)
        in_spec = pl.BlockSpec((rb, d), lambda i, me_ref: (0, 0))
        out_spec = pl.BlockSpec((rb, d), lambda i, me_ref: (me_ref[0], 0))

    def body(me_ref, w_ref, o_ref):
        value = w_ref[...]
        o_ref[...] = (value.T if transposed else value).astype(BF16)

    grid_spec = pltpu.PrefetchScalarGridSpec(num_scalar_prefetch=1, grid=grid, in_specs=[in_spec], out_specs=out_spec)
    return pl.pallas_call(
        body, grid_spec=grid_spec, out_shape=SDS((N_DEV * rb, d), BF16), name=name,
        compiler_params=pltpu.CompilerParams(dimension_semantics=("arbitrary",), vmem_limit_bytes=VMEM_LIMIT))(me, w)


ROW_TILE = 256


def _rmsnorm_fwd(h, gain, name):
    t, d = h.shape

    def body(h_ref, g_ref, u_ref):
        x = h_ref[...]
        u_ref[...] = (x * lax.rsqrt(jnp.mean(x * x, axis=-1, keepdims=True) + NORM_EPS) * g_ref[...]).astype(BF16)

    row = pl.BlockSpec((ROW_TILE, d), lambda i: (i, 0))
    return _call(body, name=name, args=[h, gain], out_shape=SDS((t, d), BF16), grid=(t // ROW_TILE,),
                 in_specs=[row, pl.BlockSpec((1, d), lambda i: (0, 0))], out_specs=row)


def _rms_bwd_math(x, gain, dy):
    rstd = lax.rsqrt(jnp.mean(x * x, axis=-1, keepdims=True) + NORM_EPS)
    xhat = x * rstd
    dxh = dy * gain
    dx = rstd * (dxh - xhat * jnp.mean(dxh * xhat, axis=-1, keepdims=True))
    return dx, jnp.sum(dy * xhat, axis=0, keepdims=True)


def _rmsnorm_bwd(du, h, gain, resid, bf_scale, name, job=None):
    t, d = h.shape

    def body(du_ref, h_ref, g_ref, r_ref, dh_ref, dhb_ref, dg_ref):
        @pl.when(pl.program_id(0) == 0)
        def _():
            dg_ref[...] = jnp.zeros_like(dg_ref)

        dx, dg = _rms_bwd_math(h_ref[...], g_ref[...], du_ref[...])
        dh = r_ref[...] + dx
        dh_ref[...] = dh
        dhb_ref[...] = (bf_scale * dh).astype(BF16)
        dg_ref[...] += dg

    row = pl.BlockSpec((ROW_TILE, d), lambda i: (i, 0))
    vec = pl.BlockSpec((1, d), lambda i: (0, 0))
    return _call(body, name=name, args=[du, h, gain, resid],
                 out_shape=(SDS((t, d), F32), SDS((t, d), BF16), SDS((1, d), F32)), grid=(t // ROW_TILE,),
                 in_specs=[row, row, vec, row], out_specs=(row, row, vec), job=job)


def _final_loss(h, gain, target, name):
    t, d = h.shape

    def body(h_ref, g_ref, t_ref, dh_ref, dhb_ref, loss_ref, dg_ref):
        @pl.when(pl.program_id(0) == 0)
        def _():
            dg_ref[...] = jnp.zeros_like(dg_ref)
            loss_ref[...] = jnp.zeros_like(loss_ref)

        x = h_ref[...]
        gain = g_ref[...]
        out = x * lax.rsqrt(jnp.mean(x * x, axis=-1, keepdims=True) + NORM_EPS) * gain
        err = out - t_ref[...]
        loss_ref[...] += 0.5 * jnp.sum(jnp.mean(err * err, axis=-1, keepdims=True), axis=0, keepdims=True)
        dx, dg = _rms_bwd_math(x, gain, err * (1.0 / d))
        dh_ref[...] = dx
        dhb_ref[...] = (0.5 * dx).astype(BF16)
        dg_ref[...] += dg

    row = pl.BlockSpec((ROW_TILE, d), lambda i: (i, 0))
    vec = pl.BlockSpec((1, d), lambda i: (0, 0))
    one = pl.BlockSpec((SUBLANES, LANES), lambda i: (0, 0))
    return _call(body, name=name, args=[h, gain, target],
                 out_shape=(SDS((t, d), F32), SDS((t, d), BF16), SDS((SUBLANES, LANES), F32), SDS((1, d), F32)),
                 grid=(t // ROW_TILE,), in_specs=[row, vec, row], out_specs=(row, row, one, vec))


def _mixnorm_fwd(ya, yb, ga, gb, name):
    t, c = ya.shape

    def body(ya_ref, yb_ref, ga_ref, gb_ref, y_ref, yt_ref):
        for k, (src, g_ref) in enumerate(((ya_ref, ga_ref), (yb_ref, gb_ref))):
            x = src[...]
            u = x * lax.rsqrt(jnp.mean(x * x, axis=-1, keepdims=True) + NORM_EPS) * g_ref[...]
            y_ref[:, k * c:(k + 1) * c] = u.astype(BF16)
            yt_ref[k * c:(k + 1) * c, :] = u.T.astype(BF16)

    row = pl.BlockSpec((ROW_TILE, c), lambda i: (i, 0))
    vec = pl.BlockSpec((1, c), lambda i: (0, 0))
    return _call(body, name=name, args=[ya, yb, ga, gb],
                 out_shape=(SDS((t, 2 * c), BF16), SDS((2 * c, t), BF16)), grid=(t // ROW_TILE,),
                 in_specs=[row, row, vec, vec],
                 out_specs=(pl.BlockSpec((ROW_TILE, 2 * c), lambda i: (i, 0)),
                            pl.BlockSpec((2 * c, ROW_TILE), lambda i: (0, i))))


def _mixnorm_bwd(dy, ya, yb, ga, gb, name):
    t, c = ya.shape

    def body(dy_ref, ya_ref, yb_ref, ga_ref, gb_ref, dya_ref, dyb_ref, dga_ref, dgb_ref):
        @pl.when(pl.program_id(0) == 0)
        def _():
            dga_ref[...] = jnp.zeros_like(dga_ref)
            dgb_ref[...] = jnp.zeros_like(dgb_ref)

        dxa, dga = _rms_bwd_math(ya_ref[...], ga_ref[...], dy_ref[:, :c])
        dxb, dgb = _rms_bwd_math(yb_ref[...], gb_ref[...], dy_ref[:, c:])
        dya_ref[...] = dxa
        dyb_ref[...] = dxb
        dga_ref[...] += dga
        dgb_ref[...] += dgb

    row = pl.BlockSpec((ROW_TILE, c), lambda i: (i, 0))
    vec = pl.BlockSpec((1, c), lambda i: (0, 0))
    return _call(body, name=name, args=[dy, ya, yb, ga, gb],
                 out_shape=(SDS((t, c), F32), SDS((t, c), F32), SDS((1, c), F32), SDS((1, c), F32)),
                 grid=(t // ROW_TILE,),
                 in_specs=[pl.BlockSpec((ROW_TILE, 2 * c), lambda i: (i, 0)), row, row, vec, vec],
                 out_specs=(row, row, vec, vec))


def _tile(n, want):
    return max(t for t in range(LANES, min(n, want) + 1, LANES) if n % t == 0)


def _mm(a, b, *, nt, out_dtype, tm, tn, name, residual=None, scale=None, take=None, out_rows=None, row_offset=0,
        into=None, job=None):
    parts = list(a) if isinstance(a, (list, tuple)) else [a]
    widths = [p.shape[-1] for p in parts]
    k = sum(widths)
    n = b.shape[0] if nt else b.shape[1]
    if take is None:
        m, which = parts[0].shape[0], lambda i: i
        tm = _tile(math.gcd(m, row_offset), tm)
    else:
        tm, tiles, which = take
        m = tm * tiles
    tn = _tile(n, tn)
    out_rows = m if out_rows is None else out_rows

    def body(*refs):
        a_refs, b_ref, rest = refs[:len(parts)], refs[len(parts)], refs[len(parts) + 1:]
        o_ref = rest[-1]
        out, at = None, 0
        for a_ref, width in zip(a_refs, widths):
            av = a_ref[...].astype(BF16)
            if nt:
                term = lax.dot_general(av, b_ref[:, at:at + width].astype(BF16), NT, preferred_element_type=F32)
            else:
                term = jnp.dot(av, b_ref[at:at + width, :].astype(BF16), preferred_element_type=F32)
            out = term if out is None else out + term
            at += width
        if residual is not None:
            out = rest[0][...] + (out if scale is None else scale * out)
        o_ref[...] = out.astype(out_dtype)

    a_specs = [pl.BlockSpec((tm, width), lambda i, j: (which(i), 0)) for width in widths]
    in_specs = a_specs + [pl.BlockSpec((tn, k), lambda i, j: (j, 0)) if nt else pl.BlockSpec((k, tn), lambda i, j: (0, j))]
    args, aliases = parts + [b], {}
    if residual is not None:
        in_specs.append(pl.BlockSpec((tm, tn), lambda i, j: (i, j)))
        args.append(residual)
    if into is not None:
        in_specs.append(ANY)
        aliases[len(args)] = 0
        args.append(into)
    return _call(body, name=name, args=args, out_shape=SDS((out_rows, n), out_dtype), grid=(m // tm, n // tn),
                 in_specs=in_specs, out_specs=pl.BlockSpec((tm, tn), lambda i, j: (row_offset // tm + i, j)),
                 aliases=aliases, job=job)


FFN_HB = 512
HIDDEN_TM = 1024
BWD_TM, BWD_HB = 1024, 256


def _ffn_hidden(u, w_in_t, name, job=None):
    t, d = u.shape
    f = w_in_t.shape[0] // 2

    def body(u_ref, w_ref, g_ref, up_ref, hid_ref, hid_t_ref):
        uu = u_ref[...]
        g = lax.dot_general(uu, w_ref[0], NT, preferred_element_type=F32)
        up = lax.dot_general(uu, w_ref[1], NT, preferred_element_type=F32)
        g_ref[...] = g.astype(BF16)
        up_ref[...] = up.astype(BF16)
        hid = (g * _sigmoid(g)) * up
        hid_ref[...] = hid.astype(BF16)
        hid_t_ref[...] = hid.T.astype(BF16)

    tm = min(HIDDEN_TM, t)
    pre = pl.BlockSpec((tm, FFN_HB), lambda i, k: (i, k))
    return _call(body, name=name, args=[u, w_in_t.reshape(2, f, d)],
                 out_shape=(SDS((t, f), BF16), SDS((t, f), BF16), SDS((t, f), BF16), SDS((f, t), BF16)),
                 grid=(t // tm, f // FFN_HB),
                 in_specs=[pl.BlockSpec((tm, d), lambda i, k: (i, 0)),
                           pl.BlockSpec((2, FFN_HB, d), lambda i, k: (0, k, 0))],
                 out_specs=(pre, pre, pre, pl.BlockSpec((FFN_HB, tm), lambda i, k: (k, i))), job=job)


def _ffn_bwd(dfb, gpre, upre, w_in_t, w_out, name, job=None):
    t, d = dfb.shape
    f = w_out.shape[0]
    tm, hb = min(BWD_TM, t), BWD_HB

    def body(df_ref, g_ref, up_ref, w_ref, wo_ref, du_ref, da_t_ref):
        @pl.when(pl.program_id(1) == 0)
        def _():
            du_ref[...] = jnp.zeros_like(du_ref)

        dhid = lax.dot_general(df_ref[...], wo_ref[...], NT, preferred_element_type=F32)
        g, up = g_ref[...].astype(F32), up_ref[...].astype(F32)
        sig = _sigmoid(g)
        silu = g * sig
        dup = dhid * silu
        dg = dhid * up * (sig * (1.0 + g * (1.0 - sig)))
        da_t_ref[0] = dg.T.astype(BF16)
        da_t_ref[1] = dup.T.astype(BF16)
        du_ref[...] += (jnp.dot(dg.astype(BF16), w_ref[0], preferred_element_type=F32)
                        + jnp.dot(dup.astype(BF16), w_ref[1], preferred_element_type=F32))

    tok = pl.BlockSpec((tm, d), lambda i, k: (i, 0))
    pre = pl.BlockSpec((tm, hb), lambda i, k: (i, k))
    return _call(body, name=name, args=[dfb, gpre, upre, w_in_t.reshape(2, f, d), w_out],
                 out_shape=(SDS((t, d), F32), SDS((2, f, t), BF16)), grid=(t // tm, f // hb),
                 in_specs=[tok, pre, pre, pl.BlockSpec((2, hb, d), lambda i, k: (0, k, 0)),
                           pl.BlockSpec((hb, d), lambda i, k: (k, 0))],
                 out_specs=(tok, pl.BlockSpec((2, hb, tm), lambda i, k: (0, k, i))), job=job)


CH = LANES
PAD = SUBLANES


def _lru_gates(xc, gw_ref, gb_ref, lam_ref, z):
    xcb = xc.astype(BF16)
    r = _sigmoid(jnp.dot(xcb, gw_ref[2 * z], preferred_element_type=F32) + gb_ref[pl.ds(2 * z, 1), :])
    i = _sigmoid(jnp.dot(xcb, gw_ref[2 * z + 1], preferred_element_type=F32) + gb_ref[pl.ds(2 * z + 1, 1), :])
    sp = _softplus(-lam_ref[pl.ds(z, 1), :])
    log_a = (-RG_C * r) * sp
    a = jnp.exp(log_a)
    mult = jnp.sqrt(-_expm1(2.0 * log_a))
    return r, i, sp, a, mult


def _conv(xpad, cw_ref, cb_ref, t):
    xc = cb_ref[...] + cw_ref[pl.ds(0, 1), :] * xpad[pl.ds(PAD - 2, t), :]
    for j in range(1, CONV_WIDTH):
        xc = xc + cw_ref[pl.ds(j, 1), :] * xpad[pl.ds(PAD - 2 + j, t), :]
    return xc


def _fill_padded(pad_ref, value, t):
    pad_ref[pl.ds(0, PAD), :] = jnp.zeros((PAD, CH), F32)
    pad_ref[pl.ds(PAD + t, PAD), :] = jnp.zeros((PAD, CH), F32)
    pad_ref[pl.ds(PAD, t), :] = value


def _scan_pair(t, a_up, b_up, out_up, a_down, b_down, out_down):
    row = lax.broadcasted_iota(jnp.int32, (SUBLANES, CH), 0)

    def compose(a, b, rising):
        for dist in (1, 2, 4):
            shift = dist if rising else SUBLANES - dist
            keep = (row >= dist) if rising else (row < SUBLANES - dist)
            b = jnp.where(keep, b + a * pltpu.roll(b, shift, axis=0), b)
            a = jnp.where(keep, a * pltpu.roll(a, shift, axis=0), a)
        return a, b

    def step(tt, carry):
        hu, hd = carry
        lo = pl.ds(pl.multiple_of(tt * SUBLANES, SUBLANES), SUBLANES)
        hi = pl.ds(pl.multiple_of(t - SUBLANES - tt * SUBLANES, SUBLANES), SUBLANES)
        a, b = compose(a_up[lo, :], b_up[lo, :], True)
        up = b + a * hu
        out_up[lo, :] = up
        a, b = compose(a_down[hi, :], b_down[hi, :], False)
        down = b + a * hd
        out_down[hi, :] = down
        return up[SUBLANES - 1:, :], down[:1, :]

    zero = jnp.zeros((1, CH), F32)
    lax.fori_loop(0, t // SUBLANES, step, (zero, zero), unroll=2)


def _lru_fwd(proj, cw, cb, gw, gb, lam, name, job=None):
    t = proj.shape[0]
    c = cw.shape[1]
    ncb = c // CH

    def body(x_ref, g_ref, cw_ref, cb_ref, gw_ref, gb_ref, lam_ref, ya_ref, hf_ref, hb_ref, xpad, a0, b0, a1, b1):
        _fill_padded(xpad, x_ref[...], t)
        xc = _conv(xpad, cw_ref, cb_ref, t)
        for z, (a_s, b_s) in enumerate(((a0, b0), (a1, b1))):
            _, i, _, a, mult = _lru_gates(xc, gw_ref, gb_ref, lam_ref, z)
            a_s[...] = a
            b_s[...] = mult * (i * xc)
        _scan_pair(t, a0, b0, hf_ref, a1, b1, hb_ref)
        gelu, _ = _gelu_parts(g_ref[...])
        ya_ref[...] = gelu * (hf_ref[...] + hb_ref[...])

    col = lambda off: pl.BlockSpec((t, CH), lambda i: (0, off + i))
    small = lambda rows: pl.BlockSpec((rows, CH), lambda i: (0, i))
    return _call(body, name=name, args=[proj, proj, cw, cb, gw, gb, lam], out_shape=(SDS((t, c), F32),) * 3,
                 grid=(ncb,),
                 in_specs=[col(0), col(ncb), small(CONV_WIDTH), small(1),
                           pl.BlockSpec((4, None, CH, CH), lambda i: (0, i, 0, 0)), small(4), small(2)],
                 out_specs=(col(0),) * 3,
                 scratch_shapes=[pltpu.VMEM((t + 2 * PAD, CH), F32)] + [pltpu.VMEM((t, CH), F32)] * 4, job=job)


def _lru_bwd(proj, cw, cb, gw, gb, lam, hf, hb, dya, name, job=None):
    t = proj.shape[0]
    c = cw.shape[1]
    ncb = c // CH

    def body(x_ref, g_ref, cw_ref, cb_ref, gw_ref, gb_ref, lam_ref, hf_ref, hb_ref, dya_ref,
             dx_ref, dg_ref, dt_ref, dcw_ref, dcb_ref, dgw_ref, dgb_ref, dlam_ref,
             xpad, hpad, dxc, a0, a1, dhs, dh0, dh1):
        _fill_padded(xpad, x_ref[...], t)
        xc = _conv(xpad, cw_ref, cb_ref, t)
        xcb = xc.astype(BF16)
        gates = [_lru_gates(xc, gw_ref, gb_ref, lam_ref, z) for z in range(2)]

        gelu, dgelu = _gelu_parts(g_ref[...])
        dya = dya_ref[...]
        dgate = dya * (hf_ref[...] + hb_ref[...]) * dgelu
        dg_ref[...] = dgate.astype(BF16)
        dt_ref[1] = dgate.T.astype(BF16)
        dhs[...] = dya * gelu

        _fill_padded(hpad, gates[0][3], t)
        a0[...] = hpad[pl.ds(PAD + 1, t), :]
        _fill_padded(hpad, gates[1][3], t)
        a1[...] = hpad[pl.ds(PAD - 1, t), :]
        _scan_pair(t, a1, dhs, dh1, a0, dhs, dh0)

        acc_dxc = jnp.zeros((t, CH), F32)
        for z, (h_ref, dh_ref, shift) in enumerate(((hf_ref, dh0, -1), (hb_ref, dh1, 1))):
            r, i, sp, a, mult = gates[z]
            _fill_padded(hpad, h_ref[...], t)
            h_nb = hpad[pl.ds(PAD + shift, t), :]
            db = dh_ref[...]
            da = db * h_nb
            d_i = db * mult * xc
            acc_dxc = acc_dxc + db * mult * i
            d_mult = db * i * xc
            d_la = da * a - d_mult * (a * a) / mult
            d_r = d_la * (-RG_C * sp)
            dlam_ref[pl.ds(z, 1), :] = (jnp.sum(d_la * (-RG_C * r), axis=0, keepdims=True)
                                        * (-_sigmoid(-lam_ref[pl.ds(z, 1), :])))
            for gate, d_pre in ((0, d_r * r * (1.0 - r)), (1, d_i * i * (1.0 - i))):
                zg = 2 * z + gate
                dgb_ref[pl.ds(zg, 1), :] = jnp.sum(d_pre, axis=0, keepdims=True)
                d_pre_b = d_pre.astype(BF16)
                dgw_ref[zg] = lax.dot_general(xcb, d_pre_b, TN, preferred_element_type=F32)
                acc_dxc = acc_dxc + lax.dot_general(d_pre_b, gw_ref[zg], NT, preferred_element_type=F32)

        dcb_ref[...] = jnp.sum(acc_dxc, axis=0, keepdims=True)
        for j in range(CONV_WIDTH):
            dcw_ref[pl.ds(j, 1), :] = jnp.sum(acc_dxc * xpad[pl.ds(PAD - 2 + j, t), :], axis=0, keepdims=True)
        _fill_padded(dxc, acc_dxc, t)
        dx = cw_ref[pl.ds(0, 1), :] * dxc[pl.ds(PAD + 2, t), :]
        for j in range(1, CONV_WIDTH):
            dx = dx + cw_ref[pl.ds(j, 1), :] * dxc[pl.ds(PAD + 2 - j, t), :]
        dx_ref[...] = dx.astype(BF16)
        dt_ref[0] = dx.T.astype(BF16)

    col = lambda off: pl.BlockSpec((t, CH), lambda i: (0, off + i))
    small = lambda rows: pl.BlockSpec((rows, CH), lambda i: (0, i))
    dense = pl.BlockSpec((4, None, CH, CH), lambda i: (0, i, 0, 0))
    padded = pltpu.VMEM((t + 2 * PAD, CH), F32)
    return _call(
        body, name=name, args=[proj, proj, cw, cb, gw, gb, lam, hf, hb, dya],
        out_shape=(SDS((t, c), BF16), SDS((t, c), BF16), SDS((2, c, t), BF16), SDS((CONV_WIDTH, c), F32),
                   SDS((1, c), F32), SDS((4, ncb, CH, CH), F32), SDS((4, c), F32), SDS((2, c), F32)),
        grid=(ncb,),
        in_specs=[col(0), col(ncb), small(CONV_WIDTH), small(1), dense, small(4), small(2), col(0), col(0), col(0)],
        out_specs=(col(0), col(0), pl.BlockSpec((2, CH, t), lambda i: (0, i, 0)), small(CONV_WIDTH), small(1),
                   dense, small(4), small(2)),
        scratch_shapes=[padded, padded, padded] + [pltpu.VMEM((t, CH), F32)] * 5, job=job)


Q_ROWS = 4
BAND_ROWS = WIN_ROWS + Q_ROWS
BAND_PAIRS = BAND_ROWS // 2
Q_BLOCK = Q_ROWS * GRID_W
BAND = BAND_ROWS * GRID_W
PAIR_W = 2 * GRID_W
N_BOTH = 2 * WIN_ROWS - 2
ENTRY_LEFT_OUT, ENTRY_RIGHT_OUT, ENTRY_OUT = N_BOTH, N_BOTH + 1, N_BOTH + 2
N_ENTRIES = N_BOTH + 3


def _bias_tables(rpb):
    cols = np.arange(GRID_W)
    start = np.clip(cols - WIN_COLS // 2, 0, GRID_W - WIN_COLS)
    valid = (cols[None, :] >= start[:, None]) & (cols[None, :] < start[:, None] + WIN_COLS)
    col_off = np.clip(cols[None, :] - cols[:, None] + WIN_COLS - 1, 0, 2 * WIN_COLS - 2)
    pick_col = jnp.asarray(np.eye(2 * WIN_COLS - 1, dtype=np.float32)[col_off] * valid[..., None])
    by_row = jnp.einsum("hrc,qkc->hrqk", rpb, pick_col, precision=lax.Precision.HIGHEST)
    by_row = jnp.where(jnp.asarray(valid)[None, None], by_row, NEG)
    out = jnp.full_like(by_row[:, :1], NEG)
    first_in, last_in = WIN_ROWS - 1 - WIN_ROWS // 2, 2 * (WIN_ROWS - 1) - WIN_ROWS // 2
    both = jnp.concatenate([by_row[:, :-1], by_row[:, 1:]], axis=-1)
    left_out = jnp.concatenate([out, by_row[:, first_in:first_in + 1]], axis=-1)
    right_out = jnp.concatenate([by_row[:, last_in:last_in + 1], out], axis=-1)
    return jnp.concatenate([both, left_out, right_out, jnp.concatenate([out, out], axis=-1)], axis=1)


def _band_start(m, rows):
    return jnp.clip(Q_ROWS * m - WIN_ROWS // 2, 0, rows - BAND_ROWS)


def _entry(r, key_row, rows):
    w0 = jnp.clip(r - WIN_ROWS // 2, 0, rows - WIN_ROWS)
    left = (key_row >= w0) & (key_row < w0 + WIN_ROWS)
    right = (key_row + 1 >= w0) & (key_row + 1 < w0 + WIN_ROWS)
    return jnp.where(left & right, key_row - r + WIN_ROWS - 1,
                     jnp.where(right, ENTRY_LEFT_OUT, jnp.where(left, ENTRY_RIGHT_OUT, ENTRY_OUT)))


def _transposed_pairs(dst, src_ref):
    for g in range(dst.shape[0]):
        dst[g] = src_ref[pl.ds(g * PAIR_W, PAIR_W), :].T.astype(BF16)


def _band_of(pairs_ref, first_pair, hh):
    heads = pl.ds(hh * HEAD_DIM, HEAD_DIM)
    return jnp.concatenate([pairs_ref[first_pair + g, heads, :] for g in range(BAND_PAIRS)], axis=1)


def _attn_block(qs, kt, tz_ref, hh, m, rows):
    rs = _band_start(m, rows)
    lanes = pl.ds(hh * HEAD_DIM, HEAD_DIM)
    qrows = pl.ds(pl.multiple_of(m * Q_BLOCK, Q_BLOCK), Q_BLOCK)
    band = pl.ds(pl.multiple_of(rs * GRID_W, PAIR_W), BAND)
    entries = [[_entry(Q_ROWS * m + i, rs + 2 * g, rows) for g in range(BAND_PAIRS)] for i in range(Q_ROWS)]
    bias = jnp.concatenate([jnp.concatenate([tz_ref[hh, e] for e in row], axis=1) for row in entries], axis=0)
    q = qs[qrows, lanes]
    s = jnp.dot(q, _band_of(kt, rs // 2, hh), preferred_element_type=F32) * (HEAD_DIM ** -0.5) + bias
    p = jnp.exp(s - jnp.max(s, axis=-1, keepdims=True))
    p = p / jnp.sum(p, axis=-1, keepdims=True)
    return q, p, qrows, band, lanes, entries, rs // 2


def _attn_fwd(proj, tables, width, name, job=None):
    t = proj.shape[0]
    rows = t // GRID_W
    npair = width // LANES
    first = (proj.shape[1] - 3 * width) // LANES

    def body(q_ref, k_ref, v_ref, tz_ref, o_ref, qs, vs, kt):
        qs[...] = q_ref[...].astype(BF16)
        vs[...] = v_ref[...].astype(BF16)
        _transposed_pairs(kt, k_ref)

        def block(m, carry):
            for hh in range(2):
                _, p, qrows, band, lanes, _, _ = _attn_block(qs, kt, tz_ref, hh, m, rows)
                o_ref[qrows, lanes] = jnp.dot(p.astype(BF16), vs[band, lanes], preferred_element_type=F32)
            return carry

        lax.fori_loop(0, rows // Q_ROWS, block, 0, unroll=2)

    col = lambda off: pl.BlockSpec((t, LANES), lambda i: (0, off + i))
    return _call(body, name=name, args=[proj, proj, proj, tables], out_shape=SDS((t, width), F32), grid=(npair,),
                 in_specs=[col(first), col(first + npair), col(first + 2 * npair),
                           pl.BlockSpec((2, N_ENTRIES, GRID_W, PAIR_W), lambda i: (i, 0, 0, 0))],
                 out_specs=col(0),
                 scratch_shapes=[pltpu.VMEM((t, LANES), BF16)] * 2 + [pltpu.VMEM((t // PAIR_W, LANES, PAIR_W), BF16)],
                 job=job)


def _attn_bwd(proj, tables, dyb, name, job=None):
    t, width = dyb.shape
    rows = t // GRID_W
    npair = width // LANES
    first = (proj.shape[1] - 3 * width) // LANES

    def body(q_ref, k_ref, v_ref, tz_ref, do_ref, dq_ref, dk_ref, dv_ref, dt_ref, dtz_ref, dq_s, dk_s, dv_s,
             qs, ks, vs, dos, kt, vt):
        qs[...] = q_ref[...].astype(BF16)
        ks[...] = k_ref[...].astype(BF16)
        vs[...] = v_ref[...].astype(BF16)
        dos[...] = do_ref[...].astype(BF16)
        _transposed_pairs(kt, k_ref)
        _transposed_pairs(vt, v_ref)
        dk_s[...] = jnp.zeros_like(dk_s)
        dv_s[...] = jnp.zeros_like(dv_s)
        dtz_ref[...] = jnp.zeros_like(dtz_ref)

        def block(m, carry):
            for hh in range(2):
                q, p, qrows, band, lanes, entries, first_pair = _attn_block(qs, kt, tz_ref, hh, m, rows)
                do = dos[qrows, lanes]
                dp = jnp.dot(do, _band_of(vt, first_pair, hh), preferred_element_type=F32)
                ds = p * (dp - jnp.sum(dp * p, axis=-1, keepdims=True))
                for i, row in enumerate(entries):
                    for g, e in enumerate(row):
                        dtz_ref[hh, e] += ds[i * GRID_W:(i + 1) * GRID_W, g * PAIR_W:(g + 1) * PAIR_W]
                dsb = (ds * (HEAD_DIM ** -0.5)).astype(BF16)
                dq_s[qrows, lanes] = jnp.dot(dsb, ks[band, lanes], preferred_element_type=F32)
                dk_s[band, lanes] += lax.dot_general(dsb, q, TN, preferred_element_type=F32)
                dv_s[band, lanes] += lax.dot_general(p.astype(BF16), do, TN, preferred_element_type=F32)
            return carry

        lax.fori_loop(0, rows // Q_ROWS, block, 0)
        for n, (src, dst) in enumerate(((dq_s, dq_ref), (dk_s, dk_ref), (dv_s, dv_ref))):
            val = src[...]
            dst[...] = val.astype(BF16)
            dt_ref[n] = val.T.astype(BF16)

    col = lambda off: pl.BlockSpec((t, LANES), lambda i: (0, off + i))
    table = pl.BlockSpec((2, N_ENTRIES, GRID_W, PAIR_W), lambda i: (i, 0, 0, 0))
    pairs = pltpu.VMEM((t // PAIR_W, LANES, PAIR_W), BF16)
    return _call(body, name=name, args=[proj, proj, proj, tables, dyb],
                 out_shape=(SDS((t, width), BF16),) * 3 + (SDS((3, width, t), BF16), SDS(tables.shape, F32)),
                 grid=(npair,),
                 in_specs=[col(first), col(first + npair), col(first + 2 * npair), table, col(0)],
                 out_specs=(col(0), col(0), col(0), pl.BlockSpec((3, LANES, t), lambda i: (0, i, 0)), table),
                 scratch_shapes=[pltpu.VMEM((t, LANES), F32)] * 3 + [pltpu.VMEM((t, LANES), BF16)] * 4 + [pairs, pairs],
                 job=job)


def _adamw_math(w, g, m, v):
    m = ADAM_B1 * m + (1.0 - ADAM_B1) * g
    v = ADAM_B2 * v + (1.0 - ADAM_B2) * (g * g)
    m_hat = m / (1.0 - ADAM_B1 ** ADAM_STEP)
    v_hat = v / (1.0 - ADAM_B2 ** ADAM_STEP)
    delta = -ADAM_LR * (m_hat / (jnp.sqrt(v_hat) + ADAM_EPS) + ADAM_WD * w)
    return delta, m, v


def _sum_partials(p_ref):
    g = p_ref[0].astype(F32)
    for s in range(1, N_CHIP):
        g = g + p_ref[s].astype(F32)
    return g


def _adamw_rows(w, partials, m, v, name):
    rb, n = w.shape
    tr = 64

    def body(w_ref, p_ref, m_ref, v_ref, g_ref, d_ref, nm_ref, nv_ref):
        g = _sum_partials(p_ref)
        g_ref[...] = g
        d_ref[...], nm_ref[...], nv_ref[...] = _adamw_math(w_ref[...], g, m_ref[...], v_ref[...])

    blk = pl.BlockSpec((tr, n), lambda i: (i, 0))
    return _call(body, name=name, args=[w, partials.reshape(N_CHIP, rb, n), m, v],
                 out_shape=(SDS((rb, n), F32),) * 4, grid=(rb // tr,),
                 in_specs=[blk, pl.BlockSpec((N_CHIP, tr, n), lambda i: (0, i, 0)), blk, blk], out_specs=(blk,) * 4)


def _adamw_cols(w, partials, m, v, name):
    d, nb = w.shape
    td = 256
    parts = list(partials) if isinstance(partials, (list, tuple)) else [partials]
    heights = [p.shape[0] // N_CHIP for p in parts]

    def body(w_ref, m_ref, v_ref, *rest):
        p_refs, (g_ref, d_ref, nm_ref, nv_ref) = rest[:len(parts)], rest[len(parts):]
        g = jnp.concatenate([_sum_partials(p_ref) for p_ref in p_refs], axis=0).T
        g_ref[...] = g
        d_ref[...], nm_ref[...], nv_ref[...] = _adamw_math(w_ref[...], g, m_ref[...], v_ref[...])

    blk = pl.BlockSpec((td, nb), lambda i: (i, 0))
    return _call(body, name=name, args=[w, m, v, *[p.reshape(N_CHIP, h, d) for p, h in zip(parts, heights)]],
                 out_shape=(SDS((d, nb), F32),) * 4, grid=(d // td,),
                 in_specs=[blk, blk, blk] + [pl.BlockSpec((N_CHIP, h, td), lambda i: (0, 0, i)) for h in heights],
                 out_specs=(blk,) * 4)


def _adamw_small(w, g, m, v, name):
    def body(w_ref, g_ref, m_ref, v_ref, d_ref, nm_ref, nv_ref):
        d_ref[...], nm_ref[...], nv_ref[...] = _adamw_math(w_ref[...], g_ref[...], m_ref[...], v_ref[...])

    return _call(body, name=name, args=[w, g, m, v], out_shape=(SDS(w.shape, F32),) * 3, in_specs=[WHOLE] * 4,
                 out_specs=(WHOLE,) * 3)


TILE = SUBLANES * LANES


def _pack(arrays):
    parts = []
    for a in arrays:
        flat = a.reshape(-1).astype(F32)
        flat = jnp.pad(flat, (0, -flat.size % TILE))
        parts.append(flat.reshape(-1, LANES))
    return jnp.concatenate(parts, axis=0)


def _unpack(pack, like):
    out, row = [], 0
    for a in like:
        n = int(np.prod(a.shape))
        nrows = -(-n // TILE) * SUBLANES
        out.append(pack[row:row + nrows].reshape(-1)[:n].reshape(a.shape))
        row += nrows
    return out


def _dense_gate_blocks(gate_w):
    w = gate_w.reshape(4, -1, 2, HEAD_DIM, HEAD_DIM)
    zero = jnp.zeros_like(w[:, :, 0])
    top = jnp.concatenate([w[:, :, 0], zero], axis=-1)
    bottom = jnp.concatenate([zero, w[:, :, 1]], axis=-1)
    return jnp.concatenate([top, bottom], axis=-2)


def _diag_gate_blocks(dense, shape):
    even = dense[:, :, :HEAD_DIM, :HEAD_DIM]
    odd = dense[:, :, HEAD_DIM:, HEAD_DIM:]
    return jnp.stack([even, odd], axis=2).reshape(shape)


LARGE = ("ffn1_w_in", "ffn1_w_out", "w_in_mix", "w_out_mix", "ffn2_w_in", "ffn2_w_out")
COLUMN_SHARDED = ("ffn1_w_in", "w_in_mix", "ffn2_w_in")
SHARDED_SMALL = ("lru_conv_w", "lru_lambda")
REPLICATED = ("norm_ffn1", "norm_mix", "lru_conv_b", "lru_gate_w", "lru_gate_b", "attn_rpb", "lru_out_norm",
              "attn_out_norm", "norm_ffn2", "norm_final")
SMALL_ORDER = REPLICATED + SHARDED_SMALL
WEIGHTS = ("norm_ffn1", "ffn1_w_in", "ffn1_w_out", "norm_mix", "w_in_mix", "lru_conv_w", "lru_conv_b", "lru_gate_w",
           "lru_gate_b", "lru_lambda", "attn_rpb", "lru_out_norm", "attn_out_norm", "w_out_mix", "norm_ffn2",
           "ffn2_w_in", "ffn2_w_out", "norm_final")


PARTS = {("gather", "w_in_mix"): 4, ("gather", "ffn2_w_in"): 8}
CARRIES = {
    "gather_ffn1_in": [(("gather", "ffn1_w_in"), 1), (("gather", "small"), 1)],
    "ffn1_hidden": [(("gather", "ffn1_w_out"), 1), (("gather", "w_in_mix"), 1)],
    "ffn1_out": [(("gather", "w_in_mix"), 3)],
    "mix_in_proj": [(("gather", "w_out_mix"), 1), (("gather", "ffn2_w_in"), 1)],
    "lru_fwd": [(("gather", "ffn2_w_in"), 3)],
    "attn_fwd": [(("gather", "ffn2_w_in"), 3)],
    "mix_out_proj": [(("gather", "ffn2_w_in"), 1)],
    "ffn2_hidden": [(("gather", "ffn2_w_out"), 1)],
    "ffn1_bwd": [(("gather", "small_grads"), 1)],
    "gather_late_grads": [(("gather", "late_grads"), 1)],
}


class _Transfer:
    def __init__(self, kind, src, dest, block_rows, parts):
        self.kind, self.src, self.dest = kind, src, dest
        self.ranges, self.taken = _split(block_rows, parts), 0

    def take(self, count):
        lo, hi = self.ranges[self.taken][0], self.ranges[self.taken + count - 1][1]
        self.taken += count
        return _Piece(self.kind, self.src, self.dest, lo, hi)


class _Traffic:
    def __init__(self):
        self.transfers = {}

    def open(self, kind, name, src, placed=None):
        dest = _gathered(src) if placed is None else placed
        self.transfers[kind, name] = _Transfer(kind, src, dest, dest.shape[0] // N_DEV, PARTS.get((kind, name), 1))

    def _job(self, host):
        moved = [self.transfers[key] for key, _ in CARRIES[host]]
        return moved, _Job([tr.take(count) for tr, (_, count) in zip(moved, CARRIES[host])])

    def carry(self, host, fn, *args, **kw):
        if host not in CARRIES:
            return fn(*args, name=host, **kw)
        moved, job = self._job(host)
        res, landed = fn(*args, name=host, job=job, **kw)
        for tr, arr in zip(moved, landed):
            tr.dest = arr
        return res

    def alone(self, host):
        moved, job = self._job(host)
        for tr, arr in zip(moved, _run_job(job, host)):
            tr.dest = arr

    def result(self, kind, name):
        tr = self.transfers.pop((kind, name))
        assert tr.taken == len(tr.ranges), (kind, name)
        return tr.dest


def _forward_backward(x, target, shards, sharded_small, s):
    c = s["lru_conv_b"].shape[1]
    width = s["attn_out_norm"].shape[1]
    t = x.shape[0]
    traffic = _Traffic()
    carry = traffic.carry
    weight = lambda n: traffic.result("gather", n)

    for n in LARGE:
        traffic.open("gather", n, None, placed=shards[n])
    traffic.open("gather", "small", sharded_small)
    traffic.alone("gather_ffn1_in")
    full_small = weight("small").reshape(N_DEV, SUBLANES, c // N_DEV)
    conv_w = full_small[:, :CONV_WIDTH].transpose(1, 0, 2).reshape(CONV_WIDTH, c)
    lam = full_small[:, CONV_WIDTH:CONV_WIDTH + 2].transpose(1, 0, 2).reshape(2, c)
    w = {"ffn1_w_in": weight("ffn1_w_in")}
    ffn_out = dict(nt=False, out_dtype=F32, tm=1024, tn=512, scale=0.5)
    u1 = _rmsnorm_fwd(x, s["norm_ffn1"], "norm_ffn1")
    g1, up1, hid1, hid1_t = carry("ffn1_hidden", _ffn_hidden, u1, w["ffn1_w_in"])
    w["ffn1_w_out"] = weight("ffn1_w_out")
    h1 = carry("ffn1_out", _mm, hid1, w["ffn1_w_out"], residual=x, **ffn_out)
    w["w_in_mix"] = weight("w_in_mix")
    u2 = _rmsnorm_fwd(h1, s["norm_mix"], "norm_mix")
    proj = carry("mix_in_proj", _mm, u2, w["w_in_mix"], nt=True, out_dtype=F32, tm=1024, tn=512)
    w["w_out_mix"] = weight("w_out_mix")
    gw = _dense_gate_blocks(s["lru_gate_w"]).astype(BF16)
    gb = s["lru_gate_b"].reshape(4, c)
    tables, tables_vjp = jax.vjp(_bias_tables, s["attn_rpb"])
    ya, hf, hb = carry("lru_fwd", _lru_fwd, proj, conv_w, s["lru_conv_b"], gw, gb, lam)
    yb = carry("attn_fwd", _attn_fwd, proj, tables, width)
    y, yt = _mixnorm_fwd(ya, yb, s["lru_out_norm"], s["attn_out_norm"], "mix_norm")
    h2 = carry("mix_out_proj", _mm, y, w["w_out_mix"], nt=False, out_dtype=F32, tm=512, tn=512, residual=h1)
    u3 = _rmsnorm_fwd(h2, s["norm_ffn2"], "norm_ffn2")
    w["ffn2_w_in"] = weight("ffn2_w_in")
    g2, up2, hid2, hid2_t = carry("ffn2_hidden", _ffn_hidden, u3, w["ffn2_w_in"])
    w["ffn2_w_out"] = weight("ffn2_w_out")
    h3 = carry("ffn2_out", _mm, hid2, w["ffn2_w_out"], residual=h2, **ffn_out)
    dh3, df2, loss_part, d_norm_final = _final_loss(h3, s["norm_final"], target, "final_loss")

    grads = {}
    grad_of = dict(nt=False, out_dtype=BF16, tm=512, tn=2048)

    to_sibling, to_chips = {}, {}

    def reduce_in_chip(n):
        land = _blank_like(grads[n], grads[n].shape[0] // 2, "landing_" + n)
        to_sibling[n], token = _split_start("to_sibling", grads[n], land, "to_sibling_" + n)
        RUN_AFTER.append(token)

    def reduce_over_chips(n, after):
        own, got = _split_wait(to_sibling.pop(n), [after], "from_sibling_" + n)
        summed = _pair_sum(own, got, "pair_sum_" + n)
        to_chips[n], token = _split_start("to_chips", summed, _own_slot(summed, "own_slot_" + n), "to_chips_" + n)
        RUN_AFTER.append(token)
        return token

    f = hid2_t.shape[0]
    grads["ffn2_w_out"] = carry("ffn2_out_grad", _mm, hid2_t, df2, **grad_of)
    reduce_in_chip("ffn2_w_out")
    du3, da2_t = carry("ffn2_bwd", _ffn_bwd, df2, g2, up2, w["ffn2_w_in"], w["ffn2_w_out"])
    reduce_over_chips("ffn2_w_out", du3)
    grads["ffn2_w_in"] = carry("ffn2_in_grad", _mm, da2_t.reshape(2 * f, t), u3, **grad_of)
    reduce_in_chip("ffn2_w_in")
    dh2, dh2b, d_norm_ffn2 = carry("norm_ffn2_bwd", _rmsnorm_bwd, du3, h2, s["norm_ffn2"], dh3, 1.0)
    grads["w_out_mix"] = carry("mix_out_grad", _mm, yt, dh2b, **grad_of)
    reduce_over_chips("ffn2_w_in", grads["w_out_mix"])
    reduce_in_chip("w_out_mix")
    dy = carry("mix_out_bwd", _mm, dh2b, w["w_out_mix"], nt=True, out_dtype=F32, tm=512, tn=512)
    dya, dyb, d_lru_out_norm, d_attn_out_norm = _mixnorm_bwd(dy, ya, yb, s["lru_out_norm"], s["attn_out_norm"],
                                                             "mix_norm_bwd")
    dq, dk, dv, dqkv_t, d_tables = carry("attn_bwd", _attn_bwd, proj, tables, dyb)
    reduce_over_chips("w_out_mix", dq)
    dx_lru, dg_lru, dxg_t, d_conv_w, d_conv_b, d_gw, d_gb, d_lam = carry(
        "lru_bwd", _lru_bwd, proj, conv_w, s["lru_conv_b"], gw, gb, lam, hf, hb, dya)
    rows_of = 2 * c + 3 * width
    lru_rows = carry("mix_in_grad_lru", _mm, dxg_t.reshape(2 * c, t), u2, out_rows=rows_of, **grad_of)
    grads["w_in_mix"] = carry("mix_in_grad_attn", _mm, dqkv_t.reshape(3 * width, t), u2, out_rows=rows_of,
                              row_offset=2 * c, into=lru_rows, **grad_of)
    reduce_in_chip("w_in_mix")
    du2 = carry("mix_in_bwd", _mm, [dx_lru, dg_lru, dq, dk, dv], w["w_in_mix"], nt=False, out_dtype=F32, tm=1024,
                tn=512)
    dh1, df1, d_norm_mix = carry("norm_mix_bwd", _rmsnorm_bwd, du2, h1, s["norm_mix"], dh2, 0.5)
    reduce_over_chips("w_in_mix", dh1)

    by_device = lambda a: a.reshape(a.shape[0], N_DEV, -1).transpose(1, 0, 2)
    small = {
        "norm_mix": d_norm_mix, "lru_conv_b": d_conv_b, "lru_gate_w": _diag_gate_blocks(d_gw, s["lru_gate_w"].shape),
        "lru_gate_b": d_gb.reshape(s["lru_gate_b"].shape), "attn_rpb": tables_vjp(d_tables)[0],
        "lru_out_norm": d_lru_out_norm, "attn_out_norm": d_attn_out_norm, "norm_ffn2": d_norm_ffn2,
        "norm_final": d_norm_final, "lru_conv_w": by_device(d_conv_w), "lru_lambda": by_device(d_lam),
    }
    early = [small[n] for n in SMALL_ORDER[1:]]
    traffic.open("gather", "small_grads", _pack(early))

    grads["ffn1_w_out"] = carry("ffn1_out_grad", _mm, hid1_t, df1, **grad_of)
    reduce_in_chip("ffn1_w_out")
    du1, da1_t = carry("ffn1_bwd", _ffn_bwd, df1, g1, up1, w["ffn1_w_in"], w["ffn1_w_out"])
    grad_x, _, d_norm_ffn1 = carry("norm_ffn1_bwd", _rmsnorm_bwd, du1, x, s["norm_ffn1"], dh1, 1.0)
    traffic.open("gather", "late_grads", _pack([d_norm_ffn1]))
    traffic.alone("gather_late_grads")
    late = traffic.result("gather", "late_grads")
    reduce_over_chips("ffn1_w_out", late)
    half = 2 * f // N_DEV // 2
    half_rows = lambda h: (half, N_DEV, lambda i: 2 * i + h)
    grads["ffn1_w_in_a"] = carry("ffn1_in_grad_a", _mm, da1_t.reshape(2 * f, t), u1, take=half_rows(0), **grad_of)
    reduce_in_chip("ffn1_w_in_a")
    grads["ffn1_w_in_b"] = carry("ffn1_in_grad_b", _mm, da1_t.reshape(2 * f, t), u1, take=half_rows(1), **grad_of)
    reduce_over_chips("ffn1_w_in_a", grads["ffn1_w_in_b"])
    reduce_in_chip("ffn1_w_in_b")
    reduced = (_unpack(_sum_devices(late, "sum_late_grads"), [d_norm_ffn1])
               + _unpack(_sum_devices(traffic.result("gather", "small_grads"), "sum_small_grads"), early))
    last_token = reduce_over_chips("ffn1_w_in_b", reduced[1])
    RUN_AFTER.clear()
    assert not traffic.transfers and not to_sibling, (list(traffic.transfers), list(to_sibling))
    return loss_part[0, 0], grad_x, to_chips, last_token, dict(zip(SMALL_ORDER, reduced))


def _step(x, loss_target, p, m, v):
    me = 4 * lax.axis_index("x") + 2 * lax.axis_index("y") + lax.axis_index("c")

    shards = {n: _cast_into_place(p[n], n in COLUMN_SHARDED, "cast_" + n) for n in LARGE}
    sharded_small = (jnp.pad(p["lru_conv_w"], ((0, SUBLANES - CONV_WIDTH), (0, 0)))
                     + jnp.pad(p["lru_lambda"], ((CONV_WIDTH, SUBLANES - CONV_WIDTH - 2), (0, 0))))
    s = {n: p[n] if n in ("lru_gate_w", "lru_gate_b", "attn_rpb") else p[n].reshape(1, -1) for n in REPLICATED}

    loss_part, grad_x, to_chips, last_token, small = _forward_backward(x, loss_target, shards, sharded_small, s)
    loss = lax.psum(loss_part, ("x", "y", "c"))

    def landed(n, after):
        return _split_wait(to_chips[n], after, "from_chips_" + n)[1]

    def update(n, partials):
        return (_adamw_cols if n in COLUMN_SHARDED else _adamw_rows)(p[n], partials, m[n], v[n], "adamw_" + n)

    out = {n: update(n, landed(n, [last_token])) for n in LARGE if n != "ffn1_w_in"}
    done = [o[3] for o in out.values()]
    out["ffn1_w_in"] = update("ffn1_w_in", [landed("ffn1_w_in_a", done), landed("ffn1_w_in_b", done)])

    g_small = {n: lax.dynamic_index_in_dim(g, me, axis=0, keepdims=False) if n in SHARDED_SMALL else g
               for n, g in small.items()}
    names = SMALL_ORDER
    like = [p[n] for n in names]
    pack_of = lambda d: _pack([d[n].reshape(p[n].shape) for n in names])
    upd = _adamw_small(pack_of(p), pack_of(g_small), pack_of(m), pack_of(v), "adamw_small")
    for n, d_, m_, v_ in zip(names, *[_unpack(u, like) for u in upd]):
        out[n] = (g_small[n].reshape(p[n].shape), d_, m_, v_)
    return loss, grad_x, out


def kernel(x, norm_ffn1, ffn1_w_in, ffn1_w_out, norm_mix, w_in_mix, lru_conv_w, lru_conv_b, lru_gate_w, lru_gate_b, lru_lambda, attn_rpb, lru_out_norm, attn_out_norm, w_out_mix, norm_ffn2, ffn2_w_in, ffn2_w_out, norm_final, loss_target, m_norm_ffn1, m_ffn1_w_in, m_ffn1_w_out, m_norm_mix, m_w_in_mix, m_lru_conv_w, m_lru_conv_b, m_lru_gate_w, m_lru_gate_b, m_lru_lambda, m_attn_rpb, m_lru_out_norm, m_attn_out_norm, m_w_out_mix, m_norm_ffn2, m_ffn2_w_in, m_ffn2_w_out, m_norm_final, v_norm_ffn1, v_ffn1_w_in, v_ffn1_w_out, v_norm_mix, v_w_in_mix, v_lru_conv_w, v_lru_conv_b, v_lru_gate_w, v_lru_gate_b, v_lru_lambda, v_attn_rpb, v_lru_out_norm, v_attn_out_norm, v_w_out_mix, v_norm_ffn2, v_ffn2_w_in, v_ffn2_w_out, v_norm_final):
    given = dict(locals())
    drop_layer = lambda n, a: a if n == "norm_final" else a[0]
    p = {n: drop_layer(n, given[n]) for n in WEIGHTS}
    m = {n: drop_layer(n, given["m_" + n]) for n in WEIGHTS}
    v = {n: drop_layer(n, given["v_" + n]) for n in WEIGHTS}
    loss, grad_x, out = _step(x[0], loss_target[0], p, m, v)
    shaped = lambda n, a: a.reshape(given[n].shape)
    return (loss, grad_x[None], *[shaped(n, out[n][k]) for k in range(4) for n in WEIGHTS])
```

```python
import math

import numpy as np
import jax
import jax.numpy as jnp
from jax import lax
from jax.experimental import pallas as pl
from jax.experimental.pallas import tpu as pltpu

F32 = jnp.float32
BF16 = jnp.bfloat16
SDS = jax.ShapeDtypeStruct

N_DEV = 8
N_CHIP = 4
NORM_EPS = 1e-6
RG_C = 8.0
CONV_WIDTH = 4
HEAD_DIM = 64
GRID_W = 64
WIN_ROWS = 8
WIN_COLS = 16
NEG = -1e30

ADAM_LR = 0.001
ADAM_B1 = 0.9
ADAM_B2 = 0.999
ADAM_EPS = 1e-08
ADAM_WD = 0.01
ADAM_STEP = 10

LANES = 128
SUBLANES = 8
VMEM_LIMIT = 56 * 1024 * 1024

NT = (((1,), (1,)), ((), ()))
TN = (((0,), (0,)), ((), ()))
ANY = pl.BlockSpec(memory_space=pl.ANY)
WHOLE = pl.BlockSpec(memory_space=pltpu.VMEM)
MESH = pl.DeviceIdType.MESH


def _sigmoid(x):
    return 1.0 / (1.0 + jnp.exp(-x))


def _gelu_parts(x):
    c = math.sqrt(2.0 / math.pi)
    t = jnp.tanh(c * (x + 0.044715 * (x * x * x)))
    gelu = 0.5 * x * (1.0 + t)
    dgelu = 0.5 * (1.0 + t) + 0.5 * x * (1.0 - t * t) * (c * (1.0 + 3.0 * 0.044715 * (x * x)))
    return gelu, dgelu


def _expm1(x):
    poly = x * (1.0 + x * (1.0 / 2) * (1.0 + x * (1.0 / 3) * (1.0 + x * (1.0 / 4) * (1.0 + x * (1.0 / 5) * (1.0 + x * (1.0 / 6))))))
    return jnp.where(jnp.abs(x) < 0.25, poly, jnp.exp(x) - 1.0)


def _softplus(x):
    return jnp.maximum(x, 0.0) + jnp.log1p(jnp.exp(-jnp.abs(x)))


class _Piece:
    N_REMOTE = {"gather": 7}
    N_LOCAL = {"gather": 1}

    def __init__(self, kind, src, dest, lo, hi):
        self.kind, self.src, self.dest, self.lo, self.hi = kind, src, dest, lo, hi


RELAY_AT = 60
RUN_AFTER = []


class _Job:
    def __init__(self, pieces):
        self.pieces = list(pieces)
        self.ins = [p.src for p in self.pieces if p.src is not None]
        self.out_shapes = [SDS(p.dest.shape, p.dest.dtype) for p in self.pieces]
        self.aliased = [i for i, p in enumerate(self.pieces) if not isinstance(p.dest, SDS)]
        self.n_remote = sum(_Piece.N_REMOTE[p.kind] for p in self.pieces)
        self.n_local = max(sum(_Piece.N_LOCAL[p.kind] for p in self.pieces), 1)

    def _each(self, step, ins, outs, send_sems, recv_sems, local_sems):
        remote = local = 0
        ins = iter(ins)
        for p, dst in zip(self.pieces, outs):
            src = None if p.src is None else next(ins)
            _EXCHANGES[p.kind](step, p, src, dst, send_sems, recv_sems, local_sems, remote, local)
            remote += _Piece.N_REMOTE[p.kind]
            local += _Piece.N_LOCAL[p.kind]

    def start(self, *refs):
        self._each("start", *refs)

    def relay(self, *refs):
        self._each("relay", *refs)

    def finish(self, *refs):
        self._each("finish", *refs)


def _call(body, *, name, args, out_shape, in_specs, out_specs, grid=(), scratch_shapes=(), aliases=None, job=None):
    single = not isinstance(out_shape, (tuple, list))
    out_shape = (out_shape,) if single else tuple(out_shape)
    out_specs = (out_specs,) if single else tuple(out_specs)
    aliases = dict(aliases or {})
    if RUN_AFTER:
        tokens, n_plain, plain_body = list(RUN_AFTER), len(args), body
        RUN_AFTER.clear()
        body = lambda *refs: plain_body(*refs[:n_plain], *refs[n_plain + len(tokens):])
        args, in_specs = list(args) + tokens, list(in_specs) + [ANY] * len(tokens)
    params = pltpu.CompilerParams(dimension_semantics=("arbitrary",) * len(grid) if grid else None,
                                  vmem_limit_bytes=VMEM_LIMIT)
    if job is None:
        res = pl.pallas_call(body, out_shape=out_shape, grid=grid, in_specs=list(in_specs), out_specs=out_specs,
                             scratch_shapes=list(scratch_shapes), input_output_aliases=aliases, name=name,
                             compiler_params=params)(*args)
        return res[0] if single else res

    n_in, n_out, n_scr = len(args), len(out_shape), len(scratch_shapes)
    j_in, j_out, j_alias = len(job.ins), len(job.out_shapes), len(job.aliased)

    def hosted(*refs):
        ins, refs = refs[:n_in], refs[n_in:]
        j_ins, refs = refs[:j_in], refs[j_in + j_alias:]
        outs, refs = refs[:n_out], refs[n_out:]
        j_outs, refs = refs[:j_out], refs[j_out:]
        scr, sems = refs[:n_scr], refs[n_scr:]
        if grid:
            step = 0
            for axis, size in enumerate(grid):
                step = step * size + pl.program_id(axis)
            steps = math.prod(grid)
            pl.when(step == 0)(lambda: job.start(j_ins, j_outs, *sems))
            body(*ins, *outs, *scr)
            pl.when(step == min(RELAY_AT * steps // 100, steps - 1))(lambda: job.relay(j_ins, j_outs, *sems))
            pl.when(step == steps - 1)(lambda: job.finish(j_ins, j_outs, *sems))
        else:
            job.start(j_ins, j_outs, *sems)
            body(*ins, *outs, *scr)
            job.relay(j_ins, j_outs, *sems)
            job.finish(j_ins, j_outs, *sems)

    res = pl.pallas_call(
        hosted, out_shape=out_shape + tuple(job.out_shapes), grid=grid,
        in_specs=list(in_specs) + [ANY] * (j_in + j_alias), out_specs=out_specs + (ANY,) * j_out,
        scratch_shapes=list(scratch_shapes) + [pltpu.SemaphoreType.DMA((job.n_remote,)),
                                               pltpu.SemaphoreType.DMA((job.n_remote,)),
                                               pltpu.SemaphoreType.DMA((job.n_local,))],
        input_output_aliases={**aliases, **{n_in + j_in + k: n_out + i for k, i in enumerate(job.aliased)}},
        name=name, compiler_params=params)(*args, *job.ins, *[job.pieces[i].dest for i in job.aliased])
    own, carried = res[:n_out], res[n_out:]
    return (own[0] if single else own), carried


def _run_job(job, name):
    return _call(lambda: None, name=name, args=[], out_shape=(), in_specs=[], out_specs=(), job=job)[1]


def _position():
    return lax.axis_index("x"), lax.axis_index("y"), lax.axis_index("c")


def _flat(px, py, pc):
    return 4 * px + 2 * py + pc


def _gather_exchange(step, p, src, dst, send_sems, recv_sems, local_sems, r0, l0):
    x, y, c = _position()
    me, sibling = (x, y, c), (x, y, 1 - c)
    along_x, along_y, diagonal = (1 - x, y), (x, 1 - y), (1 - x, 1 - y)
    south = c == 0
    passed_on = (jnp.where(south, 1 - x, x), jnp.where(south, y, 1 - y))
    passed_to = (jnp.where(south, x, 1 - x), jnp.where(south, 1 - y, y))
    placed = p.src is None
    rb, n_rows = p.dest.shape[0] // N_DEV, p.hi - p.lo

    def rows(block):
        return dst.at[pl.ds(_flat(*block) * rb + p.lo, n_rows), :]

    mine = rows(me) if placed else src.at[pl.ds(p.lo, n_rows), :]

    def copy(k, block, to, own=False):
        return pltpu.make_async_remote_copy(
            src_ref=mine if own else rows(block), dst_ref=rows(block),
            send_sem=send_sems.at[r0 + k], recv_sem=recv_sems.at[r0 + k], device_id=to, device_id_type=MESH)

    local = None if placed else pltpu.make_async_copy(mine, rows(me), local_sems.at[l0])
    if step == "start":
        if local is not None:
            local.start()
        copy(0, me, sibling, own=True).start()
        copy(1, me, (*along_x, c), own=True).start()
        copy(2, me, (*along_y, c), own=True).start()
    elif step == "relay":
        copy(1, (*along_x, c), me).wait_recv()
        copy(2, (*along_y, c), me).wait_recv()
        copy(3, (*passed_on, c), (*passed_to, c)).start()
        copy(4, (*along_x, c), sibling).start()
        copy(5, (*along_y, c), sibling).start()
    else:
        copy(3, (*diagonal, c), me).wait_recv()
        copy(6, (*diagonal, c), sibling).start()
        copy(0, sibling, me).wait_recv()
        copy(4, (*along_x, 1 - c), me).wait_recv()
        copy(5, (*along_y, 1 - c), me).wait_recv()
        copy(6, (*diagonal, 1 - c), me).wait_recv()
        copy(0, me, sibling, own=True).wait_send()
        copy(1, me, (*along_x, c), own=True).wait_send()
        copy(2, me, (*along_y, c), own=True).wait_send()
        copy(3, (*passed_on, c), (*passed_to, c)).wait_send()
        copy(4, (*along_x, c), sibling).wait_send()
        copy(5, (*along_y, c), sibling).wait_send()
        copy(6, (*diagonal, c), sibling).wait_send()
        if local is not None:
            local.wait()


CHIP_FLIPS = [(1, 0), (0, 1), (1, 1)]
_EXCHANGES = {"gather": _gather_exchange}


def _gathered(shard):
    return SDS((N_DEV * shard.shape[0], shard.shape[1]), shard.dtype)


def _split(rows, parts):
    cuts = [rows * k // parts // 16 * 16 for k in range(parts)] + [rows]
    return list(zip(cuts[:-1], cuts[1:]))


def _pair_sum(g, from_sibling, name):
    rb, n = g.shape[0] // N_DEV, g.shape[1]
    tr = rb if rb * n * 2 <= 3 * 1024 * 1024 else rb // 2
    core = lax.axis_index("c").astype(jnp.int32).reshape(1)

    def body(c_ref, g_ref, r_ref, o_ref):
        o_ref[...] = (g_ref[...].astype(F32) + r_ref[...].astype(F32)).astype(BF16)

    grid_spec = pltpu.PrefetchScalarGridSpec(
        num_scalar_prefetch=1, grid=(N_CHIP, rb // tr),
        in_specs=[pl.BlockSpec((None, None, tr, n), lambda q, i, c_ref: (q, c_ref[0], i, 0)),
                  pl.BlockSpec((None, tr, n), lambda q, i, c_ref: (q, i, 0))],
        out_specs=pl.BlockSpec((None, tr, n), lambda q, i, c_ref: (q, i, 0)))
    out = pl.pallas_call(
        body, grid_spec=grid_spec, out_shape=SDS((N_CHIP, rb, n), BF16), name=name,
        compiler_params=pltpu.CompilerParams(dimension_semantics=("arbitrary",) * 2, vmem_limit_bytes=VMEM_LIMIT))(
            core, g.reshape(N_CHIP, 2, rb, n), from_sibling.reshape(N_CHIP, rb, n))
    return out.reshape(N_CHIP * rb, n)


SEM = pl.BlockSpec(memory_space=pltpu.SEMAPHORE)
IN_HBM = pl.BlockSpec(memory_space=pltpu.HBM)
SIDE_EFFECT = pltpu.SideEffectType.DATAFLOW_SIDE_EFFECTING


def _own_slot(partials, name):
    rb, n = partials.shape[0] // N_CHIP, partials.shape[1]
    tr = rb // 2
    chip = (2 * lax.axis_index("x") + lax.axis_index("y")).astype(jnp.int32).reshape(1)

    def body(chip_ref, src_ref, dst_ref):
        dst_ref[...] = src_ref[...]

    block = pl.BlockSpec((None, tr, n), lambda i, chip_ref: (chip_ref[0], i, 0))
    grid_spec = pltpu.PrefetchScalarGridSpec(num_scalar_prefetch=1, grid=(rb // tr,), in_specs=[block], out_specs=block)
    out = pl.pallas_call(
        body, grid_spec=grid_spec, out_shape=SDS((N_CHIP, rb, n), partials.dtype), name=name,
        compiler_params=pltpu.CompilerParams(dimension_semantics=("arbitrary",), vmem_limit_bytes=VMEM_LIMIT))(
            chip, partials.reshape(N_CHIP, rb, n))
    return out.reshape(partials.shape)


def _blank_like(src, rows, name):
    return pl.pallas_call(lambda src_ref, out_ref: None, out_shape=SDS((rows, src.shape[1]), src.dtype),
                          in_specs=[ANY], out_specs=ANY, name=name)(src)


def _chip_copies(src_ref, land_ref, sems):
    x, y, c = _position()
    rb = src_ref.shape[0] // N_CHIP
    copies = []
    for k, (fx, fy) in enumerate(CHIP_FLIPS):
        px, py = (1 - x if fx else x), (1 - y if fy else y)
        copies.append(pltpu.make_async_remote_copy(
            src_ref=src_ref.at[pl.ds((2 * px + py) * rb, rb), :], dst_ref=land_ref.at[pl.ds((2 * x + y) * rb, rb), :],
            send_sem=sems[2 * k], recv_sem=sems[2 * k + 1], device_id=(px, py, c), device_id_type=MESH))
    return copies


def _sibling_copies(src_ref, land_ref, sems):
    x, y, c = _position()
    rb = src_ref.shape[0] // N_DEV
    return [pltpu.make_async_remote_copy(
        src_ref=src_ref.at[pl.ds((2 * q + 1 - c) * rb, rb), :], dst_ref=land_ref.at[pl.ds(q * rb, rb), :],
        send_sem=sems[2 * q], recv_sem=sems[2 * q + 1], device_id=(x, y, 1 - c), device_id_type=MESH)
        for q in range(N_CHIP)]


SPLIT_COPIES = {"to_chips": (_chip_copies, 3), "to_sibling": (_sibling_copies, N_CHIP)}


def _split_start(kind, src, land, name):
    copies_of, n_copies = SPLIT_COPIES[kind]

    def body(src_ref, land_ref, *rest):
        sems, token = rest[:2 * n_copies], rest[-1]
        for copy in copies_of(src_ref, land_ref, sems):
            copy.start()
        token[...] = jnp.zeros_like(token)

    res = pl.pallas_call(
        body, name=name,
        out_shape=(pltpu.SemaphoreType.DMA(()),) * (2 * n_copies)
        + (pltpu.HBM(src.shape, src.dtype), pltpu.HBM(land.shape, land.dtype), SDS((SUBLANES, LANES), F32)),
        in_specs=(IN_HBM, IN_HBM), out_specs=(SEM,) * (2 * n_copies) + (IN_HBM, IN_HBM, WHOLE),
        input_output_aliases={0: 2 * n_copies, 1: 2 * n_copies + 1},
        compiler_params=pltpu.CompilerParams(has_side_effects=SIDE_EFFECT))(
            pltpu.with_memory_space_constraint(src, pltpu.HBM), pltpu.with_memory_space_constraint(land, pltpu.HBM))
    return (kind, res[:2 * n_copies], res[-3], res[-2]), res[-1]


def _split_wait(pending, after, name):
    kind, sems, src, land = pending
    copies_of, n_copies = SPLIT_COPIES[kind]

    def body(src_ref, land_ref, *rest):
        for copy in copies_of(src_ref, land_ref, rest[:2 * n_copies]):
            copy.wait_send()
            copy.wait_recv()

    return pl.pallas_call(
        body, name=name, out_shape=(pltpu.HBM(src.shape, src.dtype), pltpu.HBM(land.shape, land.dtype)),
        in_specs=(IN_HBM, IN_HBM) + (SEM,) * (2 * n_copies) + (ANY,) * len(after), out_specs=(IN_HBM, IN_HBM),
        input_output_aliases={0: 0, 1: 1},
        compiler_params=pltpu.CompilerParams(has_side_effects=SIDE_EFFECT))(src, land, *sems, *after)


def _sum_devices(gathered, name):
    r = gathered.shape[0] // N_DEV

    def body(g_ref, o_ref):
        acc = g_ref[0]
        for s in range(1, N_DEV):
            acc = acc + g_ref[s]
        o_ref[...] = acc

    return _call(body, name=name, args=[gathered.reshape(N_DEV, r, LANES)], out_shape=SDS((r, LANES), F32),
                 in_specs=[WHOLE], out_specs=WHOLE)


def _cast_into_place(w, transposed, name):
    me = _flat(*_position()).astype(jnp.int32).reshape(1)
    if transposed:
        d, rb = w.shape
        td = 512
        grid = (d // td,)
        in_spec = pl.BlockSpec((td, rb), lambda i, me_ref: (i, 0))
        out_spec = pl.BlockSpec((rb, td), lambda i, me_ref: (me_ref[0], i))
    else:
        rb, d = w.shape
        grid = (1,)
        in_spec = pl.BlockSpec((rb, d), lambda i, me_ref: (0, 0))
        out_spec = pl.BlockSpec((rb, d), lambda i, me_ref: (me_ref[0], 0))

    def body(me_ref, w_ref, o_ref):
        value = w_ref[...]
        o_ref[...] = (value.T if transposed else value).astype(BF16)

    grid_spec = pltpu.PrefetchScalarGridSpec(num_scalar_prefetch=1, grid=grid, in_specs=[in_spec], out_specs=out_spec)
    return pl.pallas_call(
        body, grid_spec=grid_spec, out_shape=SDS((N_DEV * rb, d), BF16), name=name,
        compiler_params=pltpu.CompilerParams(dimension_semantics=("arbitrary",), vmem_limit_bytes=VMEM_LIMIT))(me, w)


ROW_TILE = 256


def _rmsnorm_fwd(h, gain, name):
    t, d = h.shape

    def body(h_ref, g_ref, u_ref):
        x = h_ref[...]
        u_ref[...] = (x * lax.rsqrt(jnp.mean(x * x, axis=-1, keepdims=True) + NORM_EPS) * g_ref[...]).astype(BF16)

    row = pl.BlockSpec((ROW_TILE, d), lambda i: (i, 0))
    return _call(body, name=name, args=[h, gain], out_shape=SDS((t, d), BF16), grid=(t // ROW_TILE,),
                 in_specs=[row, pl.BlockSpec((1, d), lambda i: (0, 0))], out_specs=row)


def _rms_bwd_math(x, gain, dy):
    rstd = lax.rsqrt(jnp.mean(x * x, axis=-1, keepdims=True) + NORM_EPS)
    xhat = x * rstd
    dxh = dy * gain
    dx = rstd * (dxh - xhat * jnp.mean(dxh * xhat, axis=-1, keepdims=True))
    return dx, jnp.sum(dy * xhat, axis=0, keepdims=True)


def _rmsnorm_bwd(du, h, gain, resid, bf_scale, name, job=None):
    t, d = h.shape

    def body(du_ref, h_ref, g_ref, r_ref, dh_ref, dhb_ref, dg_ref):
        @pl.when(pl.program_id(0) == 0)
        def _():
            dg_ref[...] = jnp.zeros_like(dg_ref)

        dx, dg = _rms_bwd_math(h_ref[...], g_ref[...], du_ref[...])
        dh = r_ref[...] + dx
        dh_ref[...] = dh
        dhb_ref[...] = (bf_scale * dh).astype(BF16)
        dg_ref[...] += dg

    row = pl.BlockSpec((ROW_TILE, d), lambda i: (i, 0))
    vec = pl.BlockSpec((1, d), lambda i: (0, 0))
    return _call(body, name=name, args=[du, h, gain, resid],
                 out_shape=(SDS((t, d), F32), SDS((t, d), BF16), SDS((1, d), F32)), grid=(t // ROW_TILE,),
                 in_specs=[row, row, vec, row], out_specs=(row, row, vec), job=job)


def _final_loss(h, gain, target, name):
    t, d = h.shape

    def body(h_ref, g_ref, t_ref, dh_ref, dhb_ref, loss_ref, dg_ref):
        @pl.when(pl.program_id(0) == 0)
        def _():
            dg_ref[...] = jnp.zeros_like(dg_ref)
            loss_ref[...] = jnp.zeros_like(loss_ref)

        x = h_ref[...]
        gain = g_ref[...]
        out = x * lax.rsqrt(jnp.mean(x * x, axis=-1, keepdims=True) + NORM_EPS) * gain
        err = out - t_ref[...]
        loss_ref[...] += 0.5 * jnp.sum(jnp.mean(err * err, axis=-1, keepdims=True), axis=0, keepdims=True)
        dx, dg = _rms_bwd_math(x, gain, err * (1.0 / d))
        dh_ref[...] = dx
        dhb_ref[...] = (0.5 * dx).astype(BF16)
        dg_ref[...] += dg

    row = pl.BlockSpec((ROW_TILE, d), lambda i: (i, 0))
    vec = pl.BlockSpec((1, d), lambda i: (0, 0))
    one = pl.BlockSpec((SUBLANES, LANES), lambda i: (0, 0))
    return _call(body, name=name, args=[h, gain, target],
                 out_shape=(SDS((t, d), F32), SDS((t, d), BF16), SDS((SUBLANES, LANES), F32), SDS((1, d), F32)),
                 grid=(t // ROW_TILE,), in_specs=[row, vec, row], out_specs=(row, row, one, vec))


def _mixnorm_fwd(ya, yb, ga, gb, name):
    t, c = ya.shape

    def body(ya_ref, yb_ref, ga_ref, gb_ref, y_ref, yt_ref):
        for k, (src, g_ref) in enumerate(((ya_ref, ga_ref), (yb_ref, gb_ref))):
            x = src[...]
            u = x * lax.rsqrt(jnp.mean(x * x, axis=-1, keepdims=True) + NORM_EPS) * g_ref[...]
            y_ref[:, k * c:(k + 1) * c] = u.astype(BF16)
            yt_ref[k * c:(k + 1) * c, :] = u.T.astype(BF16)

    row = pl.BlockSpec((ROW_TILE, c), lambda i: (i, 0))
    vec = pl.BlockSpec((1, c), lambda i: (0, 0))
    return _call(body, name=name, args=[ya, yb, ga, gb],
                 out_shape=(SDS((t, 2 * c), BF16), SDS((2 * c, t), BF16)), grid=(t // ROW_TILE,),
                 in_specs=[row, row, vec, vec],
                 out_specs=(pl.BlockSpec((ROW_TILE, 2 * c), lambda i: (i, 0)),
                            pl.BlockSpec((2 * c, ROW_TILE), lambda i: (0, i))))


def _mixnorm_bwd(dy, ya, yb, ga, gb, name):
    t, c = ya.shape

    def body(dy_ref, ya_ref, yb_ref, ga_ref, gb_ref, dya_ref, dyb_ref, dga_ref, dgb_ref):
        @pl.when(pl.program_id(0) == 0)
        def _():
            dga_ref[...] = jnp.zeros_like(dga_ref)
            dgb_ref[...] = jnp.zeros_like(dgb_ref)

        dxa, dga = _rms_bwd_math(ya_ref[...], ga_ref[...], dy_ref[:, :c])
        dxb, dgb = _rms_bwd_math(yb_ref[...], gb_ref[...], dy_ref[:, c:])
        dya_ref[...] = dxa
        dyb_ref[...] = dxb
        dga_ref[...] += dga
        dgb_ref[...] += dgb

    row = pl.BlockSpec((ROW_TILE, c), lambda i: (i, 0))
    vec = pl.BlockSpec((1, c), lambda i: (0, 0))
    return _call(body, name=name, args=[dy, ya, yb, ga, gb],
                 out_shape=(SDS((t, c), F32), SDS((t, c), F32), SDS((1, c), F32), SDS((1, c), F32)),
                 grid=(t // ROW_TILE,),
                 in_specs=[pl.BlockSpec((ROW_TILE, 2 * c), lambda i: (i, 0)), row, row, vec, vec],
                 out_specs=(row, row, vec, vec))


def _tile(n, want):
    return max(t for t in range(LANES, min(n, want) + 1, LANES) if n % t == 0)


def _mm(a, b, *, nt, out_dtype, tm, tn, name, residual=None, scale=None, take=None, out_rows=None, row_offset=0,
        into=None, job=None):
    parts = list(a) if isinstance(a, (list, tuple)) else [a]
    widths = [p.shape[-1] for p in parts]
    k = sum(widths)
    n = b.shape[0] if nt else b.shape[1]
    if take is None:
        m, which = parts[0].shape[0], lambda i: i
        tm = _tile(math.gcd(m, row_offset), tm)
    else:
        tm, tiles, which = take
        m = tm * tiles
    tn = _tile(n, tn)
    out_rows = m if out_rows is None else out_rows

    def body(*refs):
        a_refs, b_ref, rest = refs[:len(parts)], refs[len(parts)], refs[len(parts) + 1:]
        o_ref = rest[-1]
        out, at = None, 0
        for a_ref, width in zip(a_refs, widths):
            av = a_ref[...].astype(BF16)
            if nt:
                term = lax.dot_general(av, b_ref[:, at:at + width].astype(BF16), NT, preferred_element_type=F32)
            else:
                term = jnp.dot(av, b_ref[at:at + width, :].astype(BF16), preferred_element_type=F32)
            out = term if out is None else out + term
            at += width
        if residual is not None:
            out = rest[0][...] + (out if scale is None else scale * out)
        o_ref[...] = out.astype(out_dtype)

    a_specs = [pl.BlockSpec((tm, width), lambda i, j: (which(i), 0)) for width in widths]
    in_specs = a_specs + [pl.BlockSpec((tn, k), lambda i, j: (j, 0)) if nt else pl.BlockSpec((k, tn), lambda i, j: (0, j))]
    args, aliases = parts + [b], {}
    if residual is not None:
        in_specs.append(pl.BlockSpec((tm, tn), lambda i, j: (i, j)))
        args.append(residual)
    if into is not None:
        in_specs.append(ANY)
        aliases[len(args)] = 0
        args.append(into)
    return _call(body, name=name, args=args, out_shape=SDS((out_rows, n), out_dtype), grid=(m // tm, n // tn),
                 in_specs=in_specs, out_specs=pl.BlockSpec((tm, tn), lambda i, j: (row_offset // tm + i, j)),
                 aliases=aliases, job=job)


FFN_HB = 512
HIDDEN_TM = 1024
BWD_TM = 512


def _ffn_hidden(u, w_in_t, name, job=None):
    t, d = u.shape
    f = w_in_t.shape[0] // 2

    def body(u_ref, w_ref, g_ref, up_ref, hid_ref, hid_t_ref):
        uu = u_ref[...]
        g = lax.dot_general(uu, w_ref[0], NT, preferred_element_type=F32)
        up = lax.dot_general(uu, w_ref[1], NT, preferred_element_type=F32)
        g_ref[...] = g.astype(BF16)
        up_ref[...] = up.astype(BF16)
        hid = (g * _sigmoid(g)) * up
        hid_ref[...] = hid.astype(BF16)
        hid_t_ref[...] = hid.T.astype(BF16)

    tm = min(HIDDEN_TM, t)
    pre = pl.BlockSpec((tm, FFN_HB), lambda i, k: (i, k))
    return _call(body, name=name, args=[u, w_in_t.reshape(2, f, d)],
                 out_shape=(SDS((t, f), BF16), SDS((t, f), BF16), SDS((t, f), BF16), SDS((f, t), BF16)),
                 grid=(t // tm, f // FFN_HB),
                 in_specs=[pl.BlockSpec((tm, d), lambda i, k: (i, 0)),
                           pl.BlockSpec((2, FFN_HB, d), lambda i, k: (0, k, 0))],
                 out_specs=(pre, pre, pre, pl.BlockSpec((FFN_HB, tm), lambda i, k: (k, i))), job=job)


def _ffn_bwd(dfb, gpre, upre, w_in_t, w_out, name, job=None):
    t, d = dfb.shape
    f = w_out.shape[0]
    tm, hb = min(BWD_TM, t), FFN_HB
    nk = f // hb

    def body(df_ref, g_ref, up_ref, w_ref, wo_ref, du_ref, da_t_ref, acc):
        k = pl.program_id(1)

        @pl.when(k == 0)
        def _():
            acc[...] = jnp.zeros_like(acc)

        dhid = lax.dot_general(df_ref[...], wo_ref[...], NT, preferred_element_type=F32)
        g, up = g_ref[...].astype(F32), up_ref[...].astype(F32)
        sig = _sigmoid(g)
        silu = g * sig
        dup = dhid * silu
        dg = dhid * up * (sig * (1.0 + g * (1.0 - sig)))
        da_t_ref[0] = dg.T.astype(BF16)
        da_t_ref[1] = dup.T.astype(BF16)
        acc[...] += (jnp.dot(dg.astype(BF16), w_ref[0], preferred_element_type=F32)
                     + jnp.dot(dup.astype(BF16), w_ref[1], preferred_element_type=F32))

        @pl.when(k == nk - 1)
        def _():
            du_ref[...] = acc[...]

    tok = pl.BlockSpec((tm, d), lambda i, k: (i, 0))
    pre = pl.BlockSpec((tm, hb), lambda i, k: (i, k))
    return _call(body, name=name, args=[dfb, gpre, upre, w_in_t.reshape(2, f, d), w_out],
                 out_shape=(SDS((t, d), F32), SDS((2, f, t), BF16)), grid=(t // tm, nk),
                 in_specs=[tok, pre, pre, pl.BlockSpec((2, hb, d), lambda i, k: (0, k, 0)),
                           pl.BlockSpec((hb, d), lambda i, k: (k, 0))],
                 out_specs=(tok, pl.BlockSpec((2, hb, tm), lambda i, k: (0, k, i))),
                 scratch_shapes=[pltpu.VMEM((tm, d), F32)], job=job)


CH = LANES
PAD = SUBLANES


def _lru_gates(xc, gw_ref, gb_ref, lam_ref, z):
    xcb = xc.astype(BF16)
    r = _sigmoid(jnp.dot(xcb, gw_ref[2 * z], preferred_element_type=F32) + gb_ref[pl.ds(2 * z, 1), :])
    i = _sigmoid(jnp.dot(xcb, gw_ref[2 * z + 1], preferred_element_type=F32) + gb_ref[pl.ds(2 * z + 1, 1), :])
    sp = _softplus(-lam_ref[pl.ds(z, 1), :])
    log_a = (-RG_C * r) * sp
    a = jnp.exp(log_a)
    mult = jnp.sqrt(-_expm1(2.0 * log_a))
    return r, i, sp, a, mult


def _conv(xpad, cw_ref, cb_ref, t):
    xc = cb_ref[...] + cw_ref[pl.ds(0, 1), :] * xpad[pl.ds(PAD - 2, t), :]
    for j in range(1, CONV_WIDTH):
        xc = xc + cw_ref[pl.ds(j, 1), :] * xpad[pl.ds(PAD - 2 + j, t), :]
    return xc


def _fill_padded(pad_ref, value, t):
    pad_ref[pl.ds(0, PAD), :] = jnp.zeros((PAD, CH), F32)
    pad_ref[pl.ds(PAD + t, PAD), :] = jnp.zeros((PAD, CH), F32)
    pad_ref[pl.ds(PAD, t), :] = value


def _scan_pair(t, a_up, b_up, out_up, a_down, b_down, out_down):
    row = lax.broadcasted_iota(jnp.int32, (SUBLANES, CH), 0)

    def compose(a, b, rising):
        for dist in (1, 2, 4):
            shift = dist if rising else SUBLANES - dist
            keep = (row >= dist) if rising else (row < SUBLANES - dist)
            b = jnp.where(keep, b + a * pltpu.roll(b, shift, axis=0), b)
            a = jnp.where(keep, a * pltpu.roll(a, shift, axis=0), a)
        return a, b

    def step(tt, carry):
        hu, hd = carry
        lo = pl.ds(pl.multiple_of(tt * SUBLANES, SUBLANES), SUBLANES)
        hi = pl.ds(pl.multiple_of(t - SUBLANES - tt * SUBLANES, SUBLANES), SUBLANES)
        a, b = compose(a_up[lo, :], b_up[lo, :], True)
        up = b + a * hu
        out_up[lo, :] = up
        a, b = compose(a_down[hi, :], b_down[hi, :], False)
        down = b + a * hd
        out_down[hi, :] = down
        return up[SUBLANES - 1:, :], down[:1, :]

    zero = jnp.zeros((1, CH), F32)
    lax.fori_loop(0, t // SUBLANES, step, (zero, zero), unroll=2)


def _lru_fwd(proj, cw, cb, gw, gb, lam, name, job=None):
    t = proj.shape[0]
    c = cw.shape[1]
    ncb = c // CH

    def body(x_ref, g_ref, cw_ref, cb_ref, gw_ref, gb_ref, lam_ref, ya_ref, hf_ref, hb_ref, xpad, a0, b0, a1, b1):
        _fill_padded(xpad, x_ref[...], t)
        xc = _conv(xpad, cw_ref, cb_ref, t)
        for z, (a_s, b_s) in enumerate(((a0, b0), (a1, b1))):
            _, i, _, a, mult = _lru_gates(xc, gw_ref, gb_ref, lam_ref, z)
            a_s[...] = a
            b_s[...] = mult * (i * xc)
        _scan_pair(t, a0, b0, hf_ref, a1, b1, hb_ref)
        gelu, _ = _gelu_parts(g_ref[...])
        ya_ref[...] = gelu * (hf_ref[...] + hb_ref[...])

    col = lambda off: pl.BlockSpec((t, CH), lambda i: (0, off + i))
    small = lambda rows: pl.BlockSpec((rows, CH), lambda i: (0, i))
    return _call(body, name=name, args=[proj, proj, cw, cb, gw, gb, lam], out_shape=(SDS((t, c), F32),) * 3,
                 grid=(ncb,),
                 in_specs=[col(0), col(ncb), small(CONV_WIDTH), small(1),
                           pl.BlockSpec((4, None, CH, CH), lambda i: (0, i, 0, 0)), small(4), small(2)],
                 out_specs=(col(0),) * 3,
                 scratch_shapes=[pltpu.VMEM((t + 2 * PAD, CH), F32)] + [pltpu.VMEM((t, CH), F32)] * 4, job=job)


def _lru_bwd(proj, cw, cb, gw, gb, lam, hf, hb, dya, name, job=None):
    t = proj.shape[0]
    c = cw.shape[1]
    ncb = c // CH

    def body(x_ref, g_ref, cw_ref, cb_ref, gw_ref, gb_ref, lam_ref, hf_ref, hb_ref, dya_ref,
             dx_ref, dg_ref, dt_ref, dcw_ref, dcb_ref, dgw_ref, dgb_ref, dlam_ref,
             xpad, hpad, dxc, a0, a1, dhs, dh0, dh1):
        _fill_padded(xpad, x_ref[...], t)
        xc = _conv(xpad, cw_ref, cb_ref, t)
        xcb = xc.astype(BF16)
        gates = [_lru_gates(xc, gw_ref, gb_ref, lam_ref, z) for z in range(2)]

        gelu, dgelu = _gelu_parts(g_ref[...])
        dya = dya_ref[...]
        dgate = dya * (hf_ref[...] + hb_ref[...]) * dgelu
        dg_ref[...] = dgate.astype(BF16)
        dt_ref[1] = dgate.T.astype(BF16)
        dhs[...] = dya * gelu

        _fill_padded(hpad, gates[0][3], t)
        a0[...] = hpad[pl.ds(PAD + 1, t), :]
        _fill_padded(hpad, gates[1][3], t)
        a1[...] = hpad[pl.ds(PAD - 1, t), :]
        _scan_pair(t, a1, dhs, dh1, a0, dhs, dh0)

        acc_dxc = jnp.zeros((t, CH), F32)
        for z, (h_ref, dh_ref, shift) in enumerate(((hf_ref, dh0, -1), (hb_ref, dh1, 1))):
            r, i, sp, a, mult = gates[z]
            _fill_padded(hpad, h_ref[...], t)
            h_nb = hpad[pl.ds(PAD + shift, t), :]
            db = dh_ref[...]
            da = db * h_nb
            d_i = db * mult * xc
            acc_dxc = acc_dxc + db * mult * i
            d_mult = db * i * xc
            d_la = da * a - d_mult * (a * a) / mult
            d_r = d_la * (-RG_C * sp)
            dlam_ref[pl.ds(z, 1), :] = (jnp.sum(d_la * (-RG_C * r), axis=0, keepdims=True)
                                        * (-_sigmoid(-lam_ref[pl.ds(z, 1), :])))
            for gate, d_pre in ((0, d_r * r * (1.0 - r)), (1, d_i * i * (1.0 - i))):
                zg = 2 * z + gate
                dgb_ref[pl.ds(zg, 1), :] = jnp.sum(d_pre, axis=0, keepdims=True)
                d_pre_b = d_pre.astype(BF16)
                dgw_ref[zg] = lax.dot_general(xcb, d_pre_b, TN, preferred_element_type=F32)
                acc_dxc = acc_dxc + lax.dot_general(d_pre_b, gw_ref[zg], NT, preferred_element_type=F32)

        dcb_ref[...] = jnp.sum(acc_dxc, axis=0, keepdims=True)
        for j in range(CONV_WIDTH):
            dcw_ref[pl.ds(j, 1), :] = jnp.sum(acc_dxc * xpad[pl.ds(PAD - 2 + j, t), :], axis=0, keepdims=True)
        _fill_padded(dxc, acc_dxc, t)
        dx = cw_ref[pl.ds(0, 1), :] * dxc[pl.ds(PAD + 2, t), :]
        for j in range(1, CONV_WIDTH):
            dx = dx + cw_ref[pl.ds(j, 1), :] * dxc[pl.ds(PAD + 2 - j, t), :]
        dx_ref[...] = dx.astype(BF16)
        dt_ref[0] = dx.T.astype(BF16)

    col = lambda off: pl.BlockSpec((t, CH), lambda i: (0, off + i))
    small = lambda rows: pl.BlockSpec((rows, CH), lambda i: (0, i))
    dense = pl.BlockSpec((4, None, CH, CH), lambda i: (0, i, 0, 0))
    padded = pltpu.VMEM((t + 2 * PAD, CH), F32)
    return _call(
        body, name=name, args=[proj, proj, cw, cb, gw, gb, lam, hf, hb, dya],
        out_shape=(SDS((t, c), BF16), SDS((t, c), BF16), SDS((2, c, t), BF16), SDS((CONV_WIDTH, c), F32),
                   SDS((1, c), F32), SDS((4, ncb, CH, CH), F32), SDS((4, c), F32), SDS((2, c), F32)),
        grid=(ncb,),
        in_specs=[col(0), col(ncb), small(CONV_WIDTH), small(1), dense, small(4), small(2), col(0), col(0), col(0)],
        out_specs=(col(0), col(0), pl.BlockSpec((2, CH, t), lambda i: (0, i, 0)), small(CONV_WIDTH), small(1),
                   dense, small(4), small(2)),
        scratch_shapes=[padded, padded, padded] + [pltpu.VMEM((t, CH), F32)] * 5, job=job)


Q_ROWS = 4
BAND_ROWS = WIN_ROWS + Q_ROWS
BAND_PAIRS = BAND_ROWS // 2
Q_BLOCK = Q_ROWS * GRID_W
BAND = BAND_ROWS * GRID_W
PAIR_W = 2 * GRID_W
N_BOTH = 2 * WIN_ROWS - 2
ENTRY_LEFT_OUT, ENTRY_RIGHT_OUT, ENTRY_OUT = N_BOTH, N_BOTH + 1, N_BOTH + 2
N_ENTRIES = N_BOTH + 3


def _bias_tables(rpb):
    cols = np.arange(GRID_W)
    start = np.clip(cols - WIN_COLS // 2, 0, GRID_W - WIN_COLS)
    valid = (cols[None, :] >= start[:, None]) & (cols[None, :] < start[:, None] + WIN_COLS)
    col_off = np.clip(cols[None, :] - cols[:, None] + WIN_COLS - 1, 0, 2 * WIN_COLS - 2)
    pick_col = jnp.asarray(np.eye(2 * WIN_COLS - 1, dtype=np.float32)[col_off] * valid[..., None])
    by_row = jnp.einsum("hrc,qkc->hrqk", rpb, pick_col, precision=lax.Precision.HIGHEST)
    by_row = jnp.where(jnp.asarray(valid)[None, None], by_row, NEG)
    out = jnp.full_like(by_row[:, :1], NEG)
    first_in, last_in = WIN_ROWS - 1 - WIN_ROWS // 2, 2 * (WIN_ROWS - 1) - WIN_ROWS // 2
    both = jnp.concatenate([by_row[:, :-1], by_row[:, 1:]], axis=-1)
    left_out = jnp.concatenate([out, by_row[:, first_in:first_in + 1]], axis=-1)
    right_out = jnp.concatenate([by_row[:, last_in:last_in + 1], out], axis=-1)
    return jnp.concatenate([both, left_out, right_out, jnp.concatenate([out, out], axis=-1)], axis=1)


def _band_start(m, rows):
    return jnp.clip(Q_ROWS * m - WIN_ROWS // 2, 0, rows - BAND_ROWS)


def _entry(r, key_row, rows):
    w0 = jnp.clip(r - WIN_ROWS // 2, 0, rows - WIN_ROWS)
    left = (key_row >= w0) & (key_row < w0 + WIN_ROWS)
    right = (key_row + 1 >= w0) & (key_row + 1 < w0 + WIN_ROWS)
    return jnp.where(left & right, key_row - r + WIN_ROWS - 1,
                     jnp.where(right, ENTRY_LEFT_OUT, jnp.where(left, ENTRY_RIGHT_OUT, ENTRY_OUT)))


def _transposed_pairs(dst, src_ref):
    for g in range(dst.shape[0]):
        dst[g] = src_ref[pl.ds(g * PAIR_W, PAIR_W), :].T.astype(BF16)


def _band_of(pairs_ref, first_pair, hh):
    heads = pl.ds(hh * HEAD_DIM, HEAD_DIM)
    return jnp.concatenate([pairs_ref[first_pair + g, heads, :] for g in range(BAND_PAIRS)], axis=1)


def _attn_block(qs, kt, tz_ref, hh, m, rows):
    rs = _band_start(m, rows)
    lanes = pl.ds(hh * HEAD_DIM, HEAD_DIM)
    qrows = pl.ds(pl.multiple_of(m * Q_BLOCK, Q_BLOCK), Q_BLOCK)
    band = pl.ds(pl.multiple_of(rs * GRID_W, PAIR_W), BAND)
    entries = [[_entry(Q_ROWS * m + i, rs + 2 * g, rows) for g in range(BAND_PAIRS)] for i in range(Q_ROWS)]
    bias = jnp.concatenate([jnp.concatenate([tz_ref[hh, e] for e in row], axis=1) for row in entries], axis=0)
    q = qs[qrows, lanes]
    s = jnp.dot(q, _band_of(kt, rs // 2, hh), preferred_element_type=F32) * (HEAD_DIM ** -0.5) + bias
    p = jnp.exp(s - jnp.max(s, axis=-1, keepdims=True))
    p = p / jnp.sum(p, axis=-1, keepdims=True)
    return q, p, qrows, band, lanes, entries, rs // 2


def _attn_fwd(proj, tables, width, name, job=None):
    t = proj.shape[0]
    rows = t // GRID_W
    npair = width // LANES
    first = (proj.shape[1] - 3 * width) // LANES

    def body(q_ref, k_ref, v_ref, tz_ref, o_ref, qs, vs, kt):
        qs[...] = q_ref[...].astype(BF16)
        vs[...] = v_ref[...].astype(BF16)
        _transposed_pairs(kt, k_ref)

        def block(m, carry):
            for hh in range(2):
                _, p, qrows, band, lanes, _, _ = _attn_block(qs, kt, tz_ref, hh, m, rows)
                o_ref[qrows, lanes] = jnp.dot(p.astype(BF16), vs[band, lanes], preferred_element_type=F32)
            return carry

        lax.fori_loop(0, rows // Q_ROWS, block, 0, unroll=2)

    col = lambda off: pl.BlockSpec((t, LANES), lambda i: (0, off + i))
    return _call(body, name=name, args=[proj, proj, proj, tables], out_shape=SDS((t, width), F32), grid=(npair,),
                 in_specs=[col(first), col(first + npair), col(first + 2 * npair),
                           pl.BlockSpec((2, N_ENTRIES, GRID_W, PAIR_W), lambda i: (i, 0, 0, 0))],
                 out_specs=col(0),
                 scratch_shapes=[pltpu.VMEM((t, LANES), BF16)] * 2 + [pltpu.VMEM((t // PAIR_W, LANES, PAIR_W), BF16)],
                 job=job)


def _attn_bwd(proj, tables, dyb, name, job=None):
    t, width = dyb.shape
    rows = t // GRID_W
    npair = width // LANES
    first = (proj.shape[1] - 3 * width) // LANES

    def body(q_ref, k_ref, v_ref, tz_ref, do_ref, dq_ref, dk_ref, dv_ref, dt_ref, dtz_ref, dq_s, dk_s, dv_s,
             qs, ks, vs, dos, kt, vt):
        qs[...] = q_ref[...].astype(BF16)
        ks[...] = k_ref[...].astype(BF16)
        vs[...] = v_ref[...].astype(BF16)
        dos[...] = do_ref[...].astype(BF16)
        _transposed_pairs(kt, k_ref)
        _transposed_pairs(vt, v_ref)
        dk_s[...] = jnp.zeros_like(dk_s)
        dv_s[...] = jnp.zeros_like(dv_s)
        dtz_ref[...] = jnp.zeros_like(dtz_ref)

        def block(m, carry):
            for hh in range(2):
                q, p, qrows, band, lanes, entries, first_pair = _attn_block(qs, kt, tz_ref, hh, m, rows)
                do = dos[qrows, lanes]
                dp = jnp.dot(do, _band_of(vt, first_pair, hh), preferred_element_type=F32)
                ds = p * (dp - jnp.sum(dp * p, axis=-1, keepdims=True))
                for i, row in enumerate(entries):
                    for g, e in enumerate(row):
                        dtz_ref[hh, e] += ds[i * GRID_W:(i + 1) * GRID_W, g * PAIR_W:(g + 1) * PAIR_W]
                dsb = (ds * (HEAD_DIM ** -0.5)).astype(BF16)
                dq_s[qrows, lanes] = jnp.dot(dsb, ks[band, lanes], preferred_element_type=F32)
                dk_s[band, lanes] += lax.dot_general(dsb, q, TN, preferred_element_type=F32)
                dv_s[band, lanes] += lax.dot_general(p.astype(BF16), do, TN, preferred_element_type=F32)
            return carry

        lax.fori_loop(0, rows // Q_ROWS, block, 0)
        for n, (src, dst) in enumerate(((dq_s, dq_ref), (dk_s, dk_ref), (dv_s, dv_ref))):
            val = src[...]
            dst[...] = val.astype(BF16)
            dt_ref[n] = val.T.astype(BF16)

    col = lambda off: pl.BlockSpec((t, LANES), lambda i: (0, off + i))
    table = pl.BlockSpec((2, N_ENTRIES, GRID_W, PAIR_W), lambda i: (i, 0, 0, 0))
    pairs = pltpu.VMEM((t // PAIR_W, LANES, PAIR_W), BF16)
    return _call(body, name=name, args=[proj, proj, proj, tables, dyb],
                 out_shape=(SDS((t, width), BF16),) * 3 + (SDS((3, width, t), BF16), SDS(tables.shape, F32)),
                 grid=(npair,),
                 in_specs=[col(first), col(first + npair), col(first + 2 * npair), table, col(0)],
                 out_specs=(col(0), col(0), col(0), pl.BlockSpec((3, LANES, t), lambda i: (0, i, 0)), table),
                 scratch_shapes=[pltpu.VMEM((t, LANES), F32)] * 3 + [pltpu.VMEM((t, LANES), BF16)] * 4 + [pairs, pairs],
                 job=job)


def _adamw_math(w, g, m, v):
    m = ADAM_B1 * m + (1.0 - ADAM_B1) * g
    v = ADAM_B2 * v + (1.0 - ADAM_B2) * (g * g)
    m_hat = m / (1.0 - ADAM_B1 ** ADAM_STEP)
    v_hat = v / (1.0 - ADAM_B2 ** ADAM_STEP)
    delta = -ADAM_LR * (m_hat / (jnp.sqrt(v_hat) + ADAM_EPS) + ADAM_WD * w)
    return delta, m, v


def _sum_partials(p_ref):
    g = p_ref[0].astype(F32)
    for s in range(1, N_CHIP):
        g = g + p_ref[s].astype(F32)
    return g


def _adamw_rows(w, partials, m, v, name):
    rb, n = w.shape
    tr = 64

    def body(w_ref, p_ref, m_ref, v_ref, g_ref, d_ref, nm_ref, nv_ref):
        g = _sum_partials(p_ref)
        g_ref[...] = g
        d_ref[...], nm_ref[...], nv_ref[...] = _adamw_math(w_ref[...], g, m_ref[...], v_ref[...])

    blk = pl.BlockSpec((tr, n), lambda i: (i, 0))
    return _call(body, name=name, args=[w, partials.reshape(N_CHIP, rb, n), m, v],
                 out_shape=(SDS((rb, n), F32),) * 4, grid=(rb // tr,),
                 in_specs=[blk, pl.BlockSpec((N_CHIP, tr, n), lambda i: (0, i, 0)), blk, blk], out_specs=(blk,) * 4)


def _adamw_cols(w, partials, m, v, name):
    d, nb = w.shape
    td = 256
    parts = list(partials) if isinstance(partials, (list, tuple)) else [partials]
    heights = [p.shape[0] // N_CHIP for p in parts]

    def body(w_ref, m_ref, v_ref, *rest):
        p_refs, (g_ref, d_ref, nm_ref, nv_ref) = rest[:len(parts)], rest[len(parts):]
        g = jnp.concatenate([_sum_partials(p_ref) for p_ref in p_refs], axis=0).T
        g_ref[...] = g
        d_ref[...], nm_ref[...], nv_ref[...] = _adamw_math(w_ref[...], g, m_ref[...], v_ref[...])

    blk = pl.BlockSpec((td, nb), lambda i: (i, 0))
    return _call(body, name=name, args=[w, m, v, *[p.reshape(N_CHIP, h, d) for p, h in zip(parts, heights)]],
                 out_shape=(SDS((d, nb), F32),) * 4, grid=(d // td,),
                 in_specs=[blk, blk, blk] + [pl.BlockSpec((N_CHIP, h, td), lambda i: (0, 0, i)) for h in heights],
                 out_specs=(blk,) * 4)


def _adamw_small(w, g, m, v, name):
    def body(w_ref, g_ref, m_ref, v_ref, d_ref, nm_ref, nv_ref):
        d_ref[...], nm_ref[...], nv_ref[...] = _adamw_math(w_ref[...], g_ref[...], m_ref[...], v_ref[...])

    return _call(body, name=name, args=[w, g, m, v], out_shape=(SDS(w.shape, F32),) * 3, in_specs=[WHOLE] * 4,
                 out_specs=(WHOLE,) * 3)


TILE = SUBLANES * LANES


def _pack(arrays):
    parts = []
    for a in arrays:
        flat = a.reshape(-1).astype(F32)
        flat = jnp.pad(flat, (0, -flat.size % TILE))
        parts.append(flat.reshape(-1, LANES))
    return jnp.concatenate(parts, axis=0)


def _unpack(pack, like):
    out, row = [], 0
    for a in like:
        n = int(np.prod(a.shape))
        nrows = -(-n // TILE) * SUBLANES
        out.append(pack[row:row + nrows].reshape(-1)[:n].reshape(a.shape))
        row += nrows
    return out


def _dense_gate_blocks(gate_w):
    w = gate_w.reshape(4, -1, 2, HEAD_DIM, HEAD_DIM)
    zero = jnp.zeros_like(w[:, :, 0])
    top = jnp.concatenate([w[:, :, 0], zero], axis=-1)
    bottom = jnp.concatenate([zero, w[:, :, 1]], axis=-1)
    return jnp.concatenate([top, bottom], axis=-2)


def _diag_gate_blocks(dense, shape):
    even = dense[:, :, :HEAD_DIM, :HEAD_DIM]
    odd = dense[:, :, HEAD_DIM:, HEAD_DIM:]
    return jnp.stack([even, odd], axis=2).reshape(shape)


LARGE = ("ffn1_w_in", "ffn1_w_out", "w_in_mix", "w_out_mix", "ffn2_w_in", "ffn2_w_out")
COLUMN_SHARDED = ("ffn1_w_in", "w_in_mix", "ffn2_w_in")
SHARDED_SMALL = ("lru_conv_w", "lru_lambda")
REPLICATED = ("norm_ffn1", "norm_mix", "lru_conv_b", "lru_gate_w", "lru_gate_b", "attn_rpb", "lru_out_norm",
              "attn_out_norm", "norm_ffn2", "norm_final")
SMALL_ORDER = REPLICATED + SHARDED_SMALL
WEIGHTS = ("norm_ffn1", "ffn1_w_in", "ffn1_w_out", "norm_mix", "w_in_mix", "lru_conv_w", "lru_conv_b", "lru_gate_w",
           "lru_gate_b", "lru_lambda", "attn_rpb", "lru_out_norm", "attn_out_norm", "w_out_mix", "norm_ffn2",
           "ffn2_w_in", "ffn2_w_out", "norm_final")


PARTS = {("gather", "w_in_mix"): 4, ("gather", "ffn2_w_in"): 8}
CARRIES = {
    "gather_ffn1_in": [(("gather", "ffn1_w_in"), 1), (("gather", "small"), 1)],
    "ffn1_hidden": [(("gather", "ffn1_w_out"), 1), (("gather", "w_in_mix"), 1)],
    "ffn1_out": [(("gather", "w_in_mix"), 3)],
    "mix_in_proj": [(("gather", "w_out_mix"), 1), (("gather", "ffn2_w_in"), 1)],
    "lru_fwd": [(("gather", "ffn2_w_in"), 3)],
    "attn_fwd": [(("gather", "ffn2_w_in"), 3)],
    "mix_out_proj": [(("gather", "ffn2_w_in"), 1)],
    "ffn2_hidden": [(("gather", "ffn2_w_out"), 1)],
    "ffn1_bwd": [(("gather", "small_grads"), 1)],
    "gather_late_grads": [(("gather", "late_grads"), 1)],
}


class _Transfer:
    def __init__(self, kind, src, dest, block_rows, parts):
        self.kind, self.src, self.dest = kind, src, dest
        self.ranges, self.taken = _split(block_rows, parts), 0

    def take(self, count):
        lo, hi = self.ranges[self.taken][0], self.ranges[self.taken + count - 1][1]
        self.taken += count
        return _Piece(self.kind, self.src, self.dest, lo, hi)


class _Traffic:
    def __init__(self):
        self.transfers = {}

    def open(self, kind, name, src, placed=None):
        dest = _gathered(src) if placed is None else placed
        self.transfers[kind, name] = _Transfer(kind, src, dest, dest.shape[0] // N_DEV, PARTS.get((kind, name), 1))

    def _job(self, host):
        moved = [self.transfers[key] for key, _ in CARRIES[host]]
        return moved, _Job([tr.take(count) for tr, (_, count) in zip(moved, CARRIES[host])])

    def carry(self, host, fn, *args, **kw):
        if host not in CARRIES:
            return fn(*args, name=host, **kw)
        moved, job = self._job(host)
        res, landed = fn(*args, name=host, job=job, **kw)
        for tr, arr in zip(moved, landed):
            tr.dest = arr
        return res

    def alone(self, host):
        moved, job = self._job(host)
        for tr, arr in zip(moved, _run_job(job, host)):
            tr.dest = arr

    def result(self, kind, name):
        tr = self.transfers.pop((kind, name))
        assert tr.taken == len(tr.ranges), (kind, name)
        return tr.dest


def _forward_backward(x, target, shards, sharded_small, s):
    c = s["lru_conv_b"].shape[1]
    width = s["attn_out_norm"].shape[1]
    t = x.shape[0]
    traffic = _Traffic()
    carry = traffic.carry
    weight = lambda n: traffic.result("gather", n)

    for n in LARGE:
        traffic.open("gather", n, None, placed=shards[n])
    traffic.open("gather", "small", sharded_small)
    traffic.alone("gather_ffn1_in")
    full_small = weight("small").reshape(N_DEV, SUBLANES, c // N_DEV)
    conv_w = full_small[:, :CONV_WIDTH].transpose(1, 0, 2).reshape(CONV_WIDTH, c)
    lam = full_small[:, CONV_WIDTH:CONV_WIDTH + 2].transpose(1, 0, 2).reshape(2, c)
    w = {"ffn1_w_in": weight("ffn1_w_in")}
    ffn_out = dict(nt=False, out_dtype=F32, tm=1024, tn=512, scale=0.5)
    u1 = _rmsnorm_fwd(x, s["norm_ffn1"], "norm_ffn1")
    g1, up1, hid1, hid1_t = carry("ffn1_hidden", _ffn_hidden, u1, w["ffn1_w_in"])
    w["ffn1_w_out"] = weight("ffn1_w_out")
    h1 = carry("ffn1_out", _mm, hid1, w["ffn1_w_out"], residual=x, **ffn_out)
    w["w_in_mix"] = weight("w_in_mix")
    u2 = _rmsnorm_fwd(h1, s["norm_mix"], "norm_mix")
    proj = carry("mix_in_proj", _mm, u2, w["w_in_mix"], nt=True, out_dtype=F32, tm=1024, tn=512)
    w["w_out_mix"] = weight("w_out_mix")
    gw = _dense_gate_blocks(s["lru_gate_w"]).astype(BF16)
    gb = s["lru_gate_b"].reshape(4, c)
    tables, tables_vjp = jax.vjp(_bias_tables, s["attn_rpb"])
    ya, hf, hb = carry("lru_fwd", _lru_fwd, proj, conv_w, s["lru_conv_b"], gw, gb, lam)
    yb = carry("attn_fwd", _attn_fwd, proj, tables, width)
    y, yt = _mixnorm_fwd(ya, yb, s["lru_out_norm"], s["attn_out_norm"], "mix_norm")
    h2 = carry("mix_out_proj", _mm, y, w["w_out_mix"], nt=False, out_dtype=F32, tm=512, tn=512, residual=h1)
    u3 = _rmsnorm_fwd(h2, s["norm_ffn2"], "norm_ffn2")
    w["ffn2_w_in"] = weight("ffn2_w_in")
    g2, up2, hid2, hid2_t = carry("ffn2_hidden", _ffn_hidden, u3, w["ffn2_w_in"])
    w["ffn2_w_out"] = weight("ffn2_w_out")
    h3 = carry("ffn2_out", _mm, hid2, w["ffn2_w_out"], residual=h2, **ffn_out)
    dh3, df2, loss_part, d_norm_final = _final_loss(h3, s["norm_final"], target, "final_loss")

    grads = {}
    grad_of = dict(nt=False, out_dtype=BF16, tm=512, tn=2048)

    to_sibling, to_chips = {}, {}

    def reduce_in_chip(n):
        land = _blank_like(grads[n], grads[n].shape[0] // 2, "landing_" + n)
        to_sibling[n], token = _split_start("to_sibling", grads[n], land, "to_sibling_" + n)
        RUN_AFTER.append(token)

    def reduce_over_chips(n, after):
        own, got = _split_wait(to_sibling.pop(n), [after], "from_sibling_" + n)
        summed = _pair_sum(own, got, "pair_sum_" + n)
        to_chips[n], token = _split_start("to_chips", summed, _own_slot(summed, "own_slot_" + n), "to_chips_" + n)
        RUN_AFTER.append(token)
        return token

    f = hid2_t.shape[0]
    grads["ffn2_w_out"] = carry("ffn2_out_grad", _mm, hid2_t, df2, **grad_of)
    reduce_in_chip("ffn2_w_out")
    du3, da2_t = carry("ffn2_bwd", _ffn_bwd, df2, g2, up2, w["ffn2_w_in"], w["ffn2_w_out"])
    reduce_over_chips("ffn2_w_out", du3)
    grads["ffn2_w_in"] = carry("ffn2_in_grad", _mm, da2_t.reshape(2 * f, t), u3, **grad_of)
    reduce_in_chip("ffn2_w_in")
    dh2, dh2b, d_norm_ffn2 = carry("norm_ffn2_bwd", _rmsnorm_bwd, du3, h2, s["norm_ffn2"], dh3, 1.0)
    grads["w_out_mix"] = carry("mix_out_grad", _mm, yt, dh2b, **grad_of)
    reduce_over_chips("ffn2_w_in", grads["w_out_mix"])
    reduce_in_chip("w_out_mix")
    dy = carry("mix_out_bwd", _mm, dh2b, w["w_out_mix"], nt=True, out_dtype=F32, tm=512, tn=512)
    dya, dyb, d_lru_out_norm, d_attn_out_norm = _mixnorm_bwd(dy, ya, yb, s["lru_out_norm"], s["attn_out_norm"],
                                                             "mix_norm_bwd")
    dq, dk, dv, dqkv_t, d_tables = carry("attn_bwd", _attn_bwd, proj, tables, dyb)
    reduce_over_chips("w_out_mix", dq)
    dx_lru, dg_lru, dxg_t, d_conv_w, d_conv_b, d_gw, d_gb, d_lam = carry(
        "lru_bwd", _lru_bwd, proj, conv_w, s["lru_conv_b"], gw, gb, lam, hf, hb, dya)
    rows_of = 2 * c + 3 * width
    lru_rows = carry("mix_in_grad_lru", _mm, dxg_t.reshape(2 * c, t), u2, out_rows=rows_of, **grad_of)
    grads["w_in_mix"] = carry("mix_in_grad_attn", _mm, dqkv_t.reshape(3 * width, t), u2, out_rows=rows_of,
                              row_offset=2 * c, into=lru_rows, **grad_of)
    reduce_in_chip("w_in_mix")
    du2 = carry("mix_in_bwd", _mm, [dx_lru, dg_lru, dq, dk, dv], w["w_in_mix"], nt=False, out_dtype=F32, tm=1024,
                tn=512)
    dh1, df1, d_norm_mix = carry("norm_mix_bwd", _rmsnorm_bwd, du2, h1, s["norm_mix"], dh2, 0.5)
    reduce_over_chips("w_in_mix", dh1)

    by_device = lambda a: a.reshape(a.shape[0], N_DEV, -1).transpose(1, 0, 2)
    small = {
        "norm_mix": d_norm_mix, "lru_conv_b": d_conv_b, "lru_gate_w": _diag_gate_blocks(d_gw, s["lru_gate_w"].shape),
        "lru_gate_b": d_gb.reshape(s["lru_gate_b"].shape), "attn_rpb": tables_vjp(d_tables)[0],
        "lru_out_norm": d_lru_out_norm, "attn_out_norm": d_attn_out_norm, "norm_ffn2": d_norm_ffn2,
        "norm_final": d_norm_final, "lru_conv_w": by_device(d_conv_w), "lru_lambda": by_device(d_lam),
    }
    early = [small[n] for n in SMALL_ORDER[1:]]
    traffic.open("gather", "small_grads", _pack(early))

    grads["ffn1_w_out"] = carry("ffn1_out_grad", _mm, hid1_t, df1, **grad_of)
    reduce_in_chip("ffn1_w_out")
    du1, da1_t = carry("ffn1_bwd", _ffn_bwd, df1, g1, up1, w["ffn1_w_in"], w["ffn1_w_out"])
    grad_x, _, d_norm_ffn1 = carry("norm_ffn1_bwd", _rmsnorm_bwd, du1, x, s["norm_ffn1"], dh1, 1.0)
    traffic.open("gather", "late_grads", _pack([d_norm_ffn1]))
    traffic.alone("gather_late_grads")
    late = traffic.result("gather", "late_grads")
    reduce_over_chips("ffn1_w_out", late)
    half = 2 * f // N_DEV // 2
    half_rows = lambda h: (half, N_DEV, lambda i: 2 * i + h)
    grads["ffn1_w_in_a"] = carry("ffn1_in_grad_a", _mm, da1_t.reshape(2 * f, t), u1, take=half_rows(0), **grad_of)
    reduce_in_chip("ffn1_w_in_a")
    grads["ffn1_w_in_b"] = carry("ffn1_in_grad_b", _mm, da1_t.reshape(2 * f, t), u1, take=half_rows(1), **grad_of)
    reduce_over_chips("ffn1_w_in_a", grads["ffn1_w_in_b"])
    reduce_in_chip("ffn1_w_in_b")
    reduced = (_unpack(_sum_devices(late, "sum_late_grads"), [d_norm_ffn1])
               + _unpack(_sum_devices(traffic.result("gather", "small_grads"), "sum_small_grads"), early))
    last_token = reduce_over_chips("ffn1_w_in_b", reduced[1])
    RUN_AFTER.clear()
    assert not traffic.transfers and not to_sibling, (list(traffic.transfers), list(to_sibling))
    return loss_part[0, 0], grad_x, to_chips, last_token, dict(zip(SMALL_ORDER, reduced))


def _step(x, loss_target, p, m, v):
    me = 4 * lax.axis_index("x") + 2 * lax.axis_index("y") + lax.axis_index("c")

    shards = {n: _cast_into_place(p[n], n in COLUMN_SHARDED, "cast_" + n) for n in LARGE}
    sharded_small = (jnp.pad(p["lru_conv_w"], ((0, SUBLANES - CONV_WIDTH), (0, 0)))
                     + jnp.pad(p["lru_lambda"], ((CONV_WIDTH, SUBLANES - CONV_WIDTH - 2), (0, 0))))
    s = {n: p[n] if n in ("lru_gate_w", "lru_gate_b", "attn_rpb") else p[n].reshape(1, -1) for n in REPLICATED}

    loss_part, grad_x, to_chips, last_token, small = _forward_backward(x, loss_target, shards, sharded_small, s)
    loss = lax.psum(loss_part, ("x", "y", "c"))

    def landed(n, after):
        return _split_wait(to_chips[n], after, "from_chips_" + n)[1]

    def update(n, partials):
        return (_adamw_cols if n in COLUMN_SHARDED else _adamw_rows)(p[n], partials, m[n], v[n], "adamw_" + n)

    out = {n: update(n, landed(n, [last_token])) for n in LARGE if n != "ffn1_w_in"}
    done = [o[3] for o in out.values()]
    out["ffn1_w_in"] = update("ffn1_w_in", [landed("ffn1_w_in_a", done), landed("ffn1_w_in_b", done)])

    g_small = {n: lax.dynamic_index_in_dim(g, me, axis=0, keepdims=False) if n in SHARDED_SMALL else g
               for n, g in small.items()}
    names = SMALL_ORDER
    like = [p[n] for n in names]
    pack_of = lambda d: _pack([d[n].reshape(p[n].shape) for n in names])
    upd = _adamw_small(pack_of(p), pack_of(g_small), pack_of(m), pack_of(v), "adamw_small")
    for n, d_, m_, v_ in zip(names, *[_unpack(u, like) for u in upd]):
        out[n] = (g_small[n].reshape(p[n].shape), d_, m_, v_)
    return loss, grad_x, out


def kernel(x, norm_ffn1, ffn1_w_in, ffn1_w_out, norm_mix, w_in_mix, lru_conv_w, lru_conv_b, lru_gate_w, lru_gate_b, lru_lambda, attn_rpb, lru_out_norm, attn_out_norm, w_out_mix, norm_ffn2, ffn2_w_in, ffn2_w_out, norm_final, loss_target, m_norm_ffn1, m_ffn1_w_in, m_ffn1_w_out, m_norm_mix, m_w_in_mix, m_lru_conv_w, m_lru_conv_b, m_lru_gate_w, m_lru_gate_b, m_lru_lambda, m_attn_rpb, m_lru_out_norm, m_attn_out_norm, m_w_out_mix, m_norm_ffn2, m_ffn2_w_in, m_ffn2_w_out, m_norm_final, v_norm_ffn1, v_ffn1_w_in, v_ffn1_w_out, v_norm_mix, v_w_in_mix, v_lru_conv_w, v_lru_conv_b, v_lru_gate_w, v_lru_gate_b, v_lru_lambda, v_attn_rpb, v_lru_out_norm, v_attn_out_norm, v_w_out_mix, v_norm_ffn2, v_ffn2_w_in, v_ffn2_w_out, v_norm_final):
    given = dict(locals())
    drop_layer = lambda n, a: a if n == "norm_final" else a[0]
    p = {n: drop_layer(n, given[n]) for n in WEIGHTS}
    m = {n: drop_layer(n, given["m_" + n]) for n in WEIGHTS}
    v = {n: drop_layer(n, given["v_" + n]) for n in WEIGHTS}
    loss, grad_x, out = _step(x[0], loss_target[0], p, m, v)
    shaped = lambda n, a: a.reshape(given[n].shape)
    return (loss, grad_x[None], *[shaped(n, out[n][k]) for k in range(4) for n in WEIGHTS])
```

```python
import math

import numpy as np
import jax
import jax.numpy as jnp
from jax import lax
from jax.experimental import pallas as pl
from jax.experimental.pallas import tpu as pltpu

F32 = jnp.float32
BF16 = jnp.bfloat16
SDS = jax.ShapeDtypeStruct

N_DEV = 8
N_CHIP = 4
NORM_EPS = 1e-6
RG_C = 8.0
CONV_WIDTH = 4
HEAD_DIM = 64
GRID_W = 64
WIN_ROWS = 8
WIN_COLS = 16
NEG = -1e30

ADAM_LR = 0.001
ADAM_B1 = 0.9
ADAM_B2 = 0.999
ADAM_EPS = 1e-08
ADAM_WD = 0.01
ADAM_STEP = 10

LANES = 128
SUBLANES = 8
VMEM_LIMIT = 56 * 1024 * 1024

NT = (((1,), (1,)), ((), ()))
TN = (((0,), (0,)), ((), ()))
ANY = pl.BlockSpec(memory_space=pl.ANY)
WHOLE = pl.BlockSpec(memory_space=pltpu.VMEM)
MESH = pl.DeviceIdType.MESH


def _sigmoid(x):
    return 1.0 / (1.0 + jnp.exp(-x))


def _gelu_parts(x):
    c = math.sqrt(2.0 / math.pi)
    t = jnp.tanh(c * (x + 0.044715 * (x * x * x)))
    gelu = 0.5 * x * (1.0 + t)
    dgelu = 0.5 * (1.0 + t) + 0.5 * x * (1.0 - t * t) * (c * (1.0 + 3.0 * 0.044715 * (x * x)))
    return gelu, dgelu


def _expm1(x):
    poly = x * (1.0 + x * (1.0 / 2) * (1.0 + x * (1.0 / 3) * (1.0 + x * (1.0 / 4) * (1.0 + x * (1.0 / 5) * (1.0 + x * (1.0 / 6))))))
    return jnp.where(jnp.abs(x) < 0.25, poly, jnp.exp(x) - 1.0)


def _softplus(x):
    return jnp.maximum(x, 0.0) + jnp.log1p(jnp.exp(-jnp.abs(x)))


class _Piece:
    N_REMOTE = {"gather": 7}
    N_LOCAL = {"gather": 1}

    def __init__(self, kind, src, dest, lo, hi):
        self.kind, self.src, self.dest, self.lo, self.hi = kind, src, dest, lo, hi


RELAY_AT = 60
RUN_AFTER = []


class _Job:
    def __init__(self, pieces):
        self.pieces = list(pieces)
        self.ins = [p.src for p in self.pieces if p.src is not None]
        self.out_shapes = [SDS(p.dest.shape, p.dest.dtype) for p in self.pieces]
        self.aliased = [i for i, p in enumerate(self.pieces) if not isinstance(p.dest, SDS)]
        self.n_remote = sum(_Piece.N_REMOTE[p.kind] for p in self.pieces)
        self.n_local = max(sum(_Piece.N_LOCAL[p.kind] for p in self.pieces), 1)

    def _each(self, step, ins, outs, send_sems, recv_sems, local_sems):
        remote = local = 0
        ins = iter(ins)
        for p, dst in zip(self.pieces, outs):
            src = None if p.src is None else next(ins)
            _EXCHANGES[p.kind](step, p, src, dst, send_sems, recv_sems, local_sems, remote, local)
            remote += _Piece.N_REMOTE[p.kind]
            local += _Piece.N_LOCAL[p.kind]

    def start(self, *refs):
        self._each("start", *refs)

    def relay(self, *refs):
        self._each("relay", *refs)

    def finish(self, *refs):
        self._each("finish", *refs)


def _call(body, *, name, args, out_shape, in_specs, out_specs, grid=(), scratch_shapes=(), aliases=None, job=None):
    single = not isinstance(out_shape, (tuple, list))
    out_shape = (out_shape,) if single else tuple(out_shape)
    out_specs = (out_specs,) if single else tuple(out_specs)
    aliases = dict(aliases or {})
    if RUN_AFTER:
        tokens, n_plain, plain_body = list(RUN_AFTER), len(args), body
        RUN_AFTER.clear()
        body = lambda *refs: plain_body(*refs[:n_plain], *refs[n_plain + len(tokens):])
        args, in_specs = list(args) + tokens, list(in_specs) + [ANY] * len(tokens)
    params = pltpu.CompilerParams(dimension_semantics=("arbitrary",) * len(grid) if grid else None,
                                  vmem_limit_bytes=VMEM_LIMIT)
    if job is None:
        res = pl.pallas_call(body, out_shape=out_shape, grid=grid, in_specs=list(in_specs), out_specs=out_specs,
                             scratch_shapes=list(scratch_shapes), input_output_aliases=aliases, name=name,
                             compiler_params=params)(*args)
        return res[0] if single else res

    n_in, n_out, n_scr = len(args), len(out_shape), len(scratch_shapes)
    j_in, j_out, j_alias = len(job.ins), len(job.out_shapes), len(job.aliased)

    def hosted(*refs):
        ins, refs = refs[:n_in], refs[n_in:]
        j_ins, refs = refs[:j_in], refs[j_in + j_alias:]
        outs, refs = refs[:n_out], refs[n_out:]
        j_outs, refs = refs[:j_out], refs[j_out:]
        scr, sems = refs[:n_scr], refs[n_scr:]
        if grid:
            step = 0
            for axis, size in enumerate(grid):
                step = step * size + pl.program_id(axis)
            steps = math.prod(grid)
            pl.when(step == 0)(lambda: job.start(j_ins, j_outs, *sems))
            body(*ins, *outs, *scr)
            pl.when(step == min(RELAY_AT * steps // 100, steps - 1))(lambda: job.relay(j_ins, j_outs, *sems))
            pl.when(step == steps - 1)(lambda: job.finish(j_ins, j_outs, *sems))
        else:
            job.start(j_ins, j_outs, *sems)
            body(*ins, *outs, *scr)
            job.relay(j_ins, j_outs, *sems)
            job.finish(j_ins, j_outs, *sems)

    res = pl.pallas_call(
        hosted, out_shape=out_shape + tuple(job.out_shapes), grid=grid,
        in_specs=list(in_specs) + [ANY] * (j_in + j_alias), out_specs=out_specs + (ANY,) * j_out,
        scratch_shapes=list(scratch_shapes) + [pltpu.SemaphoreType.DMA((job.n_remote,)),
                                               pltpu.SemaphoreType.DMA((job.n_remote,)),
                                               pltpu.SemaphoreType.DMA((job.n_local,))],
        input_output_aliases={**aliases, **{n_in + j_in + k: n_out + i for k, i in enumerate(job.aliased)}},
        name=name, compiler_params=params)(*args, *job.ins, *[job.pieces[i].dest for i in job.aliased])
    own, carried = res[:n_out], res[n_out:]
    return (own[0] if single else own), carried


def _run_job(job, name):
    return _call(lambda: None, name=name, args=[], out_shape=(), in_specs=[], out_specs=(), job=job)[1]


def _position():
    return lax.axis_index("x"), lax.axis_index("y"), lax.axis_index("c")


def _flat(px, py, pc):
    return 4 * px + 2 * py + pc


def _gather_exchange(step, p, src, dst, send_sems, recv_sems, local_sems, r0, l0):
    x, y, c = _position()
    me, sibling = (x, y, c), (x, y, 1 - c)
    along_x, along_y, diagonal = (1 - x, y), (x, 1 - y), (1 - x, 1 - y)
    south = c == 0
    passed_on = (jnp.where(south, 1 - x, x), jnp.where(south, y, 1 - y))
    passed_to = (jnp.where(south, x, 1 - x), jnp.where(south, 1 - y, y))
    placed = p.src is None
    rb, n_rows = p.dest.shape[0] // N_DEV, p.hi - p.lo

    def rows(block):
        return dst.at[pl.ds(_flat(*block) * rb + p.lo, n_rows), :]

    mine = rows(me) if placed else src.at[pl.ds(p.lo, n_rows), :]

    def copy(k, block, to, own=False):
        return pltpu.make_async_remote_copy(
            src_ref=mine if own else rows(block), dst_ref=rows(block),
            send_sem=send_sems.at[r0 + k], recv_sem=recv_sems.at[r0 + k], device_id=to, device_id_type=MESH)

    local = None if placed else pltpu.make_async_copy(mine, rows(me), local_sems.at[l0])
    if step == "start":
        if local is not None:
            local.start()
        copy(0, me, sibling, own=True).start()
        copy(1, me, (*along_x, c), own=True).start()
        copy(2, me, (*along_y, c), own=True).start()
    elif step == "relay":
        copy(1, (*along_x, c), me).wait_recv()
        copy(2, (*along_y, c), me).wait_recv()
        copy(3, (*passed_on, c), (*passed_to, c)).start()
        copy(4, (*along_x, c), sibling).start()
        copy(5, (*along_y, c), sibling).start()
    else:
        copy(3, (*diagonal, c), me).wait_recv()
        copy(6, (*diagonal, c), sibling).start()
        copy(0, sibling, me).wait_recv()
        copy(4, (*along_x, 1 - c), me).wait_recv()
        copy(5, (*along_y, 1 - c), me).wait_recv()
        copy(6, (*diagonal, 1 - c), me).wait_recv()
        copy(0, me, sibling, own=True).wait_send()
        copy(1, me, (*along_x, c), own=True).wait_send()
        copy(2, me, (*along_y, c), own=True).wait_send()
        copy(3, (*passed_on, c), (*passed_to, c)).wait_send()
        copy(4, (*along_x, c), sibling).wait_send()
        copy(5, (*along_y, c), sibling).wait_send()
        copy(6, (*diagonal, c), sibling).wait_send()
        if local is not None:
            local.wait()


CHIP_FLIPS = [(1, 0), (0, 1), (1, 1)]
_EXCHANGES = {"gather": _gather_exchange}


def _gathered(shard):
    return SDS((N_DEV * shard.shape[0], shard.shape[1]), shard.dtype)


def _split(rows, parts):
    cuts = [rows * k // parts // 16 * 16 for k in range(parts)] + [rows]
    return list(zip(cuts[:-1], cuts[1:]))


def _pair_sum(g, from_sibling, name):
    rb, n = g.shape[0] // N_DEV, g.shape[1]
    tr = rb if rb * n * 2 <= 3 * 1024 * 1024 else rb // 2
    core = lax.axis_index("c").astype(jnp.int32).reshape(1)

    def body(c_ref, g_ref, r_ref, o_ref):
        o_ref[...] = (g_ref[...].astype(F32) + r_ref[...].astype(F32)).astype(BF16)

    grid_spec = pltpu.PrefetchScalarGridSpec(
        num_scalar_prefetch=1, grid=(N_CHIP, rb // tr),
        in_specs=[pl.BlockSpec((None, None, tr, n), lambda q, i, c_ref: (q, c_ref[0], i, 0)),
                  pl.BlockSpec((None, tr, n), lambda q, i, c_ref: (q, i, 0))],
        out_specs=pl.BlockSpec((None, tr, n), lambda q, i, c_ref: (q, i, 0)))
    out = pl.pallas_call(
        body, grid_spec=grid_spec, out_shape=SDS((N_CHIP, rb, n), BF16), name=name,
        compiler_params=pltpu.CompilerParams(dimension_semantics=("arbitrary",) * 2, vmem_limit_bytes=VMEM_LIMIT))(
            core, g.reshape(N_CHIP, 2, rb, n), from_sibling.reshape(N_CHIP, rb, n))
    return out.reshape(N_CHIP * rb, n)


SEM = pl.BlockSpec(memory_space=pltpu.SEMAPHORE)
IN_HBM = pl.BlockSpec(memory_space=pltpu.HBM)
SIDE_EFFECT = pltpu.SideEffectType.DATAFLOW_SIDE_EFFECTING


def _own_slot(partials, name):
    rb, n = partials.shape[0] // N_CHIP, partials.shape[1]
    tr = rb // 2
    chip = (2 * lax.axis_index("x") + lax.axis_index("y")).astype(jnp.int32).reshape(1)

    def body(chip_ref, src_ref, dst_ref):
        dst_ref[...] = src_ref[...]

    block = pl.BlockSpec((None, tr, n), lambda i, chip_ref: (chip_ref[0], i, 0))
    grid_spec = pltpu.PrefetchScalarGridSpec(num_scalar_prefetch=1, grid=(rb // tr,), in_specs=[block], out_specs=block)
    out = pl.pallas_call(
        body, grid_spec=grid_spec, out_shape=SDS((N_CHIP, rb, n), partials.dtype), name=name,
        compiler_params=pltpu.CompilerParams(dimension_semantics=("arbitrary",), vmem_limit_bytes=VMEM_LIMIT))(
            chip, partials.reshape(N_CHIP, rb, n))
    return out.reshape(partials.shape)


def _blank_like(src, rows, name):
    return pl.pallas_call(lambda src_ref, out_ref: None, out_shape=SDS((rows, src.shape[1]), src.dtype),
                          in_specs=[ANY], out_specs=ANY, name=name)(src)


def _chip_copies(src_ref, land_ref, sems):
    x, y, c = _position()
    rb = src_ref.shape[0] // N_CHIP
    copies = []
    for k, (fx, fy) in enumerate(CHIP_FLIPS):
        px, py = (1 - x if fx else x), (1 - y if fy else y)
        copies.append(pltpu.make_async_remote_copy(
            src_ref=src_ref.at[pl.ds((2 * px + py) * rb, rb), :], dst_ref=land_ref.at[pl.ds((2 * x + y) * rb, rb), :],
            send_sem=sems[2 * k], recv_sem=sems[2 * k + 1], device_id=(px, py, c), device_id_type=MESH))
    return copies


def _sibling_copies(src_ref, land_ref, sems):
    x, y, c = _position()
    rb = src_ref.shape[0] // N_DEV
    return [pltpu.make_async_remote_copy(
        src_ref=src_ref.at[pl.ds((2 * q + 1 - c) * rb, rb), :], dst_ref=land_ref.at[pl.ds(q * rb, rb), :],
        send_sem=sems[2 * q], recv_sem=sems[2 * q + 1], device_id=(x, y, 1 - c), device_id_type=MESH)
        for q in range(N_CHIP)]


SPLIT_COPIES = {"to_chips": (_chip_copies, 3), "to_sibling": (_sibling_copies, N_CHIP)}


def _split_start(kind, src, land, name):
    copies_of, n_copies = SPLIT_COPIES[kind]

    def body(src_ref, land_ref, *rest):
        sems, token = rest[:2 * n_copies], rest[-1]
        for copy in copies_of(src_ref, land_ref, sems):
            copy.start()
        token[...] = jnp.zeros_like(token)

    res = pl.pallas_call(
        body, name=name,
        out_shape=(pltpu.SemaphoreType.DMA(()),) * (2 * n_copies)
        + (pltpu.HBM(src.shape, src.dtype), pltpu.HBM(land.shape, land.dtype), SDS((SUBLANES, LANES), F32)),
        in_specs=(IN_HBM, IN_HBM), out_specs=(SEM,) * (2 * n_copies) + (IN_HBM, IN_HBM, WHOLE),
        input_output_aliases={0: 2 * n_copies, 1: 2 * n_copies + 1},
        compiler_params=pltpu.CompilerParams(has_side_effects=SIDE_EFFECT))(
            pltpu.with_memory_space_constraint(src, pltpu.HBM), pltpu.with_memory_space_constraint(land, pltpu.HBM))
    return (kind, res[:2 * n_copies], res[-3], res[-2]), res[-1]


def _split_wait(pending, after, name):
    kind, sems, src, land = pending
    copies_of, n_copies = SPLIT_COPIES[kind]

    def body(src_ref, land_ref, *rest):
        for copy in copies_of(src_ref, land_ref, rest[:2 * n_copies]):
            copy.wait_send()
            copy.wait_recv()

    return pl.pallas_call(
        body, name=name, out_shape=(pltpu.HBM(src.shape, src.dtype), pltpu.HBM(land.shape, land.dtype)),
        in_specs=(IN_HBM, IN_HBM) + (SEM,) * (2 * n_copies) + (ANY,) * len(after), out_specs=(IN_HBM, IN_HBM),
        input_output_aliases={0: 0, 1: 1},
        compiler_params=pltpu.CompilerParams(has_side_effects=SIDE_EFFECT))(src, land, *sems, *after)


def _sum_devices(gathered, name):
    r = gathered.shape[0] // N_DEV

    def body(g_ref, o_ref):
        acc = g_ref[0]
        for s in range(1, N_DEV):
            acc = acc + g_ref[s]
        o_ref[...] = acc

    return _call(body, name=name, args=[gathered.reshape(N_DEV, r, LANES)], out_shape=SDS((r, LANES), F32),
                 in_specs=[WHOLE], out_specs=WHOLE)


def _cast_into_place(w, transposed, name):
    me = _flat(*_position()).astype(jnp.int32).reshape(1)
    if transposed:
        d, rb = w.shape
        td = 512
        grid = (d // td,)
        in_spec = pl.BlockSpec((td, rb), lambda i, me_ref: (i, 0))
        out_spec = pl.BlockSpec((rb, td), lambda i, me_ref: (me_ref[0], i))
    else:
        rb, d = w.shape
        grid = (1,)
        in_spec = pl.BlockSpec((rb, d), lambda i, me_ref: (0, 0))
        out_spec = pl.BlockSpec((rb, d), lambda i, me_ref: (me_ref[0], 0))

    def body(me_ref, w_ref, o_ref):
        value = w_ref[...]
        o_ref[...] = (value.T if transposed else value).astype(BF16)

    grid_spec = pltpu.PrefetchScalarGridSpec(num_scalar_prefetch=1, grid=grid, in_specs=[in_spec], out_specs=out_spec)
    return pl.pallas_call(
        body, grid_spec=grid_spec, out_shape=SDS((N_DEV * rb, d), BF16), name=name,
        compiler_params=pltpu.CompilerParams(dimension_semantics=("arbitrary",), vmem_limit_bytes=VMEM_LIMIT))(me, w)


ROW_TILE = 256


def _rmsnorm_fwd(h, gain, name):
    t, d = h.shape

    def body(h_ref, g_ref, u_ref):
        x = h_ref[...]
        u_ref[...] = (x * lax.rsqrt(jnp.mean(x * x, axis=-1, keepdims=True) + NORM_EPS) * g_ref[...]).astype(BF16)

    row = pl.BlockSpec((ROW_TILE, d), lambda i: (i, 0))
    return _call(body, name=name, args=[h, gain], out_shape=SDS((t, d), BF16), grid=(t // ROW_TILE,),
                 in_specs=[row, pl.BlockSpec((1, d), lambda i: (0, 0))], out_specs=row)


def _rms_bwd_math(x, gain, dy):
    rstd = lax.rsqrt(jnp.mean(x * x, axis=-1, keepdims=True) + NORM_EPS)
    xhat = x * rstd
    dxh = dy * gain
    dx = rstd * (dxh - xhat * jnp.mean(dxh * xhat, axis=-1, keepdims=True))
    return dx, jnp.sum(dy * xhat, axis=0, keepdims=True)


def _rmsnorm_bwd(du, h, gain, resid, bf_scale, name, job=None):
    t, d = h.shape

    def body(du_ref, h_ref, g_ref, r_ref, dh_ref, dhb_ref, dg_ref):
        @pl.when(pl.program_id(0) == 0)
        def _():
            dg_ref[...] = jnp.zeros_like(dg_ref)

        dx, dg = _rms_bwd_math(h_ref[...], g_ref[...], du_ref[...])
        dh = r_ref[...] + dx
        dh_ref[...] = dh
        dhb_ref[...] = (bf_scale * dh).astype(BF16)
        dg_ref[...] += dg

    row = pl.BlockSpec((ROW_TILE, d), lambda i: (i, 0))
    vec = pl.BlockSpec((1, d), lambda i: (0, 0))
    return _call(body, name=name, args=[du, h, gain, resid],
                 out_shape=(SDS((t, d), F32), SDS((t, d), BF16), SDS((1, d), F32)), grid=(t // ROW_TILE,),
                 in_specs=[row, row, vec, row], out_specs=(row, row, vec), job=job)


def _final_loss(h, gain, target, name):
    t, d = h.shape

    def body(h_ref, g_ref, t_ref, dh_ref, dhb_ref, loss_ref, dg_ref):
        @pl.when(pl.program_id(0) == 0)
        def _():
            dg_ref[...] = jnp.zeros_like(dg_ref)
            loss_ref[...] = jnp.zeros_like(loss_ref)

        x = h_ref[...]
        gain = g_ref[...]
        out = x * lax.rsqrt(jnp.mean(x * x, axis=-1, keepdims=True) + NORM_EPS) * gain
        err = out - t_ref[...]
        loss_ref[...] += 0.5 * jnp.sum(jnp.mean(err * err, axis=-1, keepdims=True), axis=0, keepdims=True)
        dx, dg = _rms_bwd_math(x, gain, err * (1.0 / d))
        dh_ref[...] = dx
        dhb_ref[...] = (0.5 * dx).astype(BF16)
        dg_ref[...] += dg

    row = pl.BlockSpec((ROW_TILE, d), lambda i: (i, 0))
    vec = pl.BlockSpec((1, d), lambda i: (0, 0))
    one = pl.BlockSpec((SUBLANES, LANES), lambda i: (0, 0))
    return _call(body, name=name, args=[h, gain, target],
                 out_shape=(SDS((t, d), F32), SDS((t, d), BF16), SDS((SUBLANES, LANES), F32), SDS((1, d), F32)),
                 grid=(t // ROW_TILE,), in_specs=[row, vec, row], out_specs=(row, row, one, vec))


def _mixnorm_fwd(ya, yb, ga, gb, name):
    t, c = ya.shape

    def body(ya_ref, yb_ref, ga_ref, gb_ref, y_ref, yt_ref):
        for k, (src, g_ref) in enumerate(((ya_ref, ga_ref), (yb_ref, gb_ref))):
            x = src[...]
            u = x * lax.rsqrt(jnp.mean(x * x, axis=-1, keepdims=True) + NORM_EPS) * g_ref[...]
            y_ref[:, k * c:(k + 1) * c] = u.astype(BF16)
            yt_ref[k * c:(k + 1) * c, :] = u.T.astype(BF16)

    row = pl.BlockSpec((ROW_TILE, c), lambda i: (i, 0))
    vec = pl.BlockSpec((1, c), lambda i: (0, 0))
    return _call(body, name=name, args=[ya, yb, ga, gb],
                 out_shape=(SDS((t, 2 * c), BF16), SDS((2 * c, t), BF16)), grid=(t // ROW_TILE,),
                 in_specs=[row, row, vec, vec],
                 out_specs=(pl.BlockSpec((ROW_TILE, 2 * c), lambda i: (i, 0)),
                            pl.BlockSpec((2 * c, ROW_TILE), lambda i: (0, i))))


def _mixnorm_bwd(dy, ya, yb, ga, gb, name):
    t, c = ya.shape

    def body(dy_ref, ya_ref, yb_ref, ga_ref, gb_ref, dya_ref, dyb_ref, dga_ref, dgb_ref):
        @pl.when(pl.program_id(0) == 0)
        def _():
            dga_ref[...] = jnp.zeros_like(dga_ref)
            dgb_ref[...] = jnp.zeros_like(dgb_ref)

        dxa, dga = _rms_bwd_math(ya_ref[...], ga_ref[...], dy_ref[:, :c])
        dxb, dgb = _rms_bwd_math(yb_ref[...], gb_ref[...], dy_ref[:, c:])
        dya_ref[...] = dxa
        dyb_ref[...] = dxb
        dga_ref[...] += dga
        dgb_ref[...] += dgb

    row = pl.BlockSpec((ROW_TILE, c), lambda i: (i, 0))
    vec = pl.BlockSpec((1, c), lambda i: (0, 0))
    return _call(body, name=name, args=[dy, ya, yb, ga, gb],
                 out_shape=(SDS((t, c), F32), SDS((t, c), F32), SDS((1, c), F32), SDS((1, c), F32)),
                 grid=(t // ROW_TILE,),
                 in_specs=[pl.BlockSpec((ROW_TILE, 2 * c), lambda i: (i, 0)), row, row, vec, vec],
                 out_specs=(row, row, vec, vec))


def _tile(n, want):
    return max(t for t in range(LANES, min(n, want) + 1, LANES) if n % t == 0)


def _mm(a, b, *, nt, out_dtype, tm, tn, name, residual=None, scale=None, take=None, out_rows=None, row_offset=0,
        into=None, job=None):
    parts = list(a) if isinstance(a, (list, tuple)) else [a]
    widths = [p.shape[-1] for p in parts]
    k = sum(widths)
    n = b.shape[0] if nt else b.shape[1]
    if take is None:
        m, which = parts[0].shape[0], lambda i: i
        tm = _tile(math.gcd(m, row_offset), tm)
    else:
        tm, tiles, which = take
        m = tm * tiles
    tn = _tile(n, tn)
    out_rows = m if out_rows is None else out_rows

    def body(*refs):
        a_refs, b_ref, rest = refs[:len(parts)], refs[len(parts)], refs[len(parts) + 1:]
        o_ref = rest[-1]
        out, at = None, 0
        for a_ref, width in zip(a_refs, widths):
            av = a_ref[...].astype(BF16)
            if nt:
                term = lax.dot_general(av, b_ref[:, at:at + width].astype(BF16), NT, preferred_element_type=F32)
            else:
                term = jnp.dot(av, b_ref[at:at + width, :].astype(BF16), preferred_element_type=F32)
            out = term if out is None else out + term
            at += width
        if residual is not None:
            out = rest[0][...] + (out if scale is None else scale * out)
        o_ref[...] = out.astype(out_dtype)

    a_specs = [pl.BlockSpec((tm, width), lambda i, j: (which(i), 0)) for width in widths]
    in_specs = a_specs + [pl.BlockSpec((tn, k), lambda i, j: (j, 0)) if nt else pl.BlockSpec((k, tn), lambda i, j: (0, j))]
    args, aliases = parts + [b], {}
    if residual is not None:
        in_specs.append(pl.BlockSpec((tm, tn), lambda i, j: (i, j)))
        args.append(residual)
    if into is not None:
        in_specs.append(ANY)
        aliases[len(args)] = 0
        args.append(into)
    return _call(body, name=name, args=args, out_shape=SDS((out_rows, n), out_dtype), grid=(m // tm, n // tn),
                 in_specs=in_specs, out_specs=pl.BlockSpec((tm, tn), lambda i, j: (row_offset // tm + i, j)),
                 aliases=aliases, job=job)


FFN_HB = 512
HIDDEN_TM = 1024
BWD_TM = 512


def _ffn_hidden(u, w_in_t, name, job=None):
    t, d = u.shape
    f = w_in_t.shape[0] // 2

    def body(u_ref, w_ref, g_ref, up_ref, hid_ref, hid_t_ref):
        uu = u_ref[...]
        g = lax.dot_general(uu, w_ref[0], NT, preferred_element_type=F32)
        up = lax.dot_general(uu, w_ref[1], NT, preferred_element_type=F32)
        g_ref[...] = g.astype(BF16)
        up_ref[...] = up.astype(BF16)
        hid = (g * _sigmoid(g)) * up
        hid_ref[...] = hid.astype(BF16)
        hid_t_ref[...] = hid.T.astype(BF16)

    tm = min(HIDDEN_TM, t)
    pre = pl.BlockSpec((tm, FFN_HB), lambda i, k: (i, k))
    return _call(body, name=name, args=[u, w_in_t.reshape(2, f, d)],
                 out_shape=(SDS((t, f), BF16), SDS((t, f), BF16), SDS((t, f), BF16), SDS((f, t), BF16)),
                 grid=(t // tm, f // FFN_HB),
                 in_specs=[pl.BlockSpec((tm, d), lambda i, k: (i, 0)),
                           pl.BlockSpec((2, FFN_HB, d), lambda i, k: (0, k, 0))],
                 out_specs=(pre, pre, pre, pl.BlockSpec((FFN_HB, tm), lambda i, k: (k, i))), job=job)


def _ffn_bwd(dfb, gpre, upre, w_in_t, w_out, name, job=None):
    t, d = dfb.shape
    f = w_out.shape[0]
    tm, hb = min(BWD_TM, t), FFN_HB
    nk = f // hb

    def body(df_ref, g_ref, up_ref, w_ref, wo_ref, du_ref, da_t_ref, acc):
        k = pl.program_id(1)

        @pl.when(k == 0)
        def _():
            acc[...] = jnp.zeros_like(acc)

        dhid = lax.dot_general(df_ref[...], wo_ref[...], NT, preferred_element_type=F32)
        g, up = g_ref[...].astype(F32), up_ref[...].astype(F32)
        sig = _sigmoid(g)
        silu = g * sig
        dup = dhid * silu
        dg = dhid * up * (sig * (1.0 + g * (1.0 - sig)))
        da_t_ref[0] = dg.T.astype(BF16)
        da_t_ref[1] = dup.T.astype(BF16)
        acc[...] += (jnp.dot(dg.astype(BF16), w_ref[0], preferred_element_type=F32)
                     + jnp.dot(dup.astype(BF16), w_ref[1], preferred_element_type=F32))

        @pl.when(k == nk - 1)
        def _():
            du_ref[...] = acc[...]

    tok = pl.BlockSpec((tm, d), lambda i, k: (i, 0))
    pre = pl.BlockSpec((tm, hb), lambda i, k: (i, k))
    return _call(body, name=name, args=[dfb, gpre, upre, w_in_t.reshape(2, f, d), w_out],
                 out_shape=(SDS((t, d), F32), SDS((2, f, t), BF16)), grid=(t // tm, nk),
                 in_specs=[tok, pre, pre, pl.BlockSpec((2, hb, d), lambda i, k: (0, k, 0)),
                           pl.BlockSpec((hb, d), lambda i, k: (k, 0))],
                 out_specs=(tok, pl.BlockSpec((2, hb, tm), lambda i, k: (0, k, i))),
                 scratch_shapes=[pltpu.VMEM((tm, d), F32)], job=job)


CH = LANES
PAD = SUBLANES


def _lru_gates(xc, gw_ref, gb_ref, lam_ref, z):
    xcb = xc.astype(BF16)
    r = _sigmoid(jnp.dot(xcb, gw_ref[2 * z], preferred_element_type=F32) + gb_ref[pl.ds(2 * z, 1), :])
    i = _sigmoid(jnp.dot(xcb, gw_ref[2 * z + 1], preferred_element_type=F32) + gb_ref[pl.ds(2 * z + 1, 1), :])
    sp = _softplus(-lam_ref[pl.ds(z, 1), :])
    log_a = (-RG_C * r) * sp
    a = jnp.exp(log_a)
    mult = jnp.sqrt(-_expm1(2.0 * log_a))
    return r, i, sp, a, mult


def _conv(xpad, cw_ref, cb_ref, t):
    xc = cb_ref[...] + cw_ref[pl.ds(0, 1), :] * xpad[pl.ds(PAD - 2, t), :]
    for j in range(1, CONV_WIDTH):
        xc = xc + cw_ref[pl.ds(j, 1), :] * xpad[pl.ds(PAD - 2 + j, t), :]
    return xc


def _fill_padded(pad_ref, value, t):
    pad_ref[pl.ds(0, PAD), :] = jnp.zeros((PAD, CH), F32)
    pad_ref[pl.ds(PAD + t, PAD), :] = jnp.zeros((PAD, CH), F32)
    pad_ref[pl.ds(PAD, t), :] = value


def _scan_pair(t, a_up, b_up, out_up, a_down, b_down, out_down):
    row = lax.broadcasted_iota(jnp.int32, (SUBLANES, CH), 0)

    def compose(a, b, rising):
        for dist in (1, 2, 4):
            shift = dist if rising else SUBLANES - dist
            keep = (row >= dist) if rising else (row < SUBLANES - dist)
            b = jnp.where(keep, b + a * pltpu.roll(b, shift, axis=0), b)
            a = jnp.where(keep, a * pltpu.roll(a, shift, axis=0), a)
        return a, b

    def step(tt, carry):
        hu, hd = carry
        lo = pl.ds(pl.multiple_of(tt * SUBLANES, SUBLANES), SUBLANES)
        hi = pl.ds(pl.multiple_of(t - SUBLANES - tt * SUBLANES, SUBLANES), SUBLANES)
        a, b = compose(a_up[lo, :], b_up[lo, :], True)
        up = b + a * hu
        out_up[lo, :] = up
        a, b = compose(a_down[hi, :], b_down[hi, :], False)
        down = b + a * hd
        out_down[hi, :] = down
        return up[SUBLANES - 1:, :], down[:1, :]

    zero = jnp.zeros((1, CH), F32)
    lax.fori_loop(0, t // SUBLANES, step, (zero, zero), unroll=2)


def _lru_fwd(proj, cw, cb, gw, gb, lam, name, job=None):
    t = proj.shape[0]
    c = cw.shape[1]
    ncb = c // CH

    def body(x_ref, g_ref, cw_ref, cb_ref, gw_ref, gb_ref, lam_ref, ya_ref, hf_ref, hb_ref, xpad, a0, b0, a1, b1):
        _fill_padded(xpad, x_ref[...], t)
        xc = _conv(xpad, cw_ref, cb_ref, t)
        for z, (a_s, b_s) in enumerate(((a0, b0), (a1, b1))):
            _, i, _, a, mult = _lru_gates(xc, gw_ref, gb_ref, lam_ref, z)
            a_s[...] = a
            b_s[...] = mult * (i * xc)
        _scan_pair(t, a0, b0, hf_ref, a1, b1, hb_ref)
        gelu, _ = _gelu_parts(g_ref[...])
        ya_ref[...] = gelu * (hf_ref[...] + hb_ref[...])

    col = lambda off: pl.BlockSpec((t, CH), lambda i: (0, off + i))
    small = lambda rows: pl.BlockSpec((rows, CH), lambda i: (0, i))
    return _call(body, name=name, args=[proj, proj, cw, cb, gw, gb, lam], out_shape=(SDS((t, c), F32),) * 3,
                 grid=(ncb,),
                 in_specs=[col(0), col(ncb), small(CONV_WIDTH), small(1),
                           pl.BlockSpec((4, None, CH, CH), lambda i: (0, i, 0, 0)), small(4), small(2)],
                 out_specs=(col(0),) * 3,
                 scratch_shapes=[pltpu.VMEM((t + 2 * PAD, CH), F32)] + [pltpu.VMEM((t, CH), F32)] * 4, job=job)


def _lru_bwd(proj, cw, cb, gw, gb, lam, hf, hb, dya, name, job=None):
    t = proj.shape[0]
    c = cw.shape[1]
    ncb = c // CH

    def body(x_ref, g_ref, cw_ref, cb_ref, gw_ref, gb_ref, lam_ref, hf_ref, hb_ref, dya_ref,
             dx_ref, dg_ref, dt_ref, dcw_ref, dcb_ref, dgw_ref, dgb_ref, dlam_ref,
             xpad, hpad, dxc, a0, a1, dhs, dh0, dh1):
        _fill_padded(xpad, x_ref[...], t)
        xc = _conv(xpad, cw_ref, cb_ref, t)
        xcb = xc.astype(BF16)
        gates = [_lru_gates(xc, gw_ref, gb_ref, lam_ref, z) for z in range(2)]

        gelu, dgelu = _gelu_parts(g_ref[...])
        dya = dya_ref[...]
        dgate = dya * (hf_ref[...] + hb_ref[...]) * dgelu
        dg_ref[...] = dgate.astype(BF16)
        dt_ref[1] = dgate.T.astype(BF16)
        dhs[...] = dya * gelu

        _fill_padded(hpad, gates[0][3], t)
        a0[...] = hpad[pl.ds(PAD + 1, t), :]
        _fill_padded(hpad, gates[1][3], t)
        a1[...] = hpad[pl.ds(PAD - 1, t), :]
        _scan_pair(t, a1, dhs, dh1, a0, dhs, dh0)

        acc_dxc = jnp.zeros((t, CH), F32)
        for z, (h_ref, dh_ref, shift) in enumerate(((hf_ref, dh0, -1), (hb_ref, dh1, 1))):
            r, i, sp, a, mult = gates[z]
            _fill_padded(hpad, h_ref[...], t)
            h_nb = hpad[pl.ds(PAD + shift, t), :]
            db = dh_ref[...]
            da = db * h_nb
            d_i = db * mult * xc
            acc_dxc = acc_dxc + db * mult * i
            d_mult = db * i * xc
            d_la = da * a - d_mult * (a * a) / mult
            d_r = d_la * (-RG_C * sp)
            dlam_ref[pl.ds(z, 1), :] = (jnp.sum(d_la * (-RG_C * r), axis=0, keepdims=True)
                                        * (-_sigmoid(-lam_ref[pl.ds(z, 1), :])))
            for gate, d_pre in ((0, d_r * r * (1.0 - r)), (1, d_i * i * (1.0 - i))):
                zg = 2 * z + gate
                dgb_ref[pl.ds(zg, 1), :] = jnp.sum(d_pre, axis=0, keepdims=True)
                d_pre_b = d_pre.astype(BF16)
                dgw_ref[zg] = lax.dot_general(xcb, d_pre_b, TN, preferred_element_type=F32)
                acc_dxc = acc_dxc + lax.dot_general(d_pre_b, gw_ref[zg], NT, preferred_element_type=F32)

        dcb_ref[...] = jnp.sum(acc_dxc, axis=0, keepdims=True)
        for j in range(CONV_WIDTH):
            dcw_ref[pl.ds(j, 1), :] = jnp.sum(acc_dxc * xpad[pl.ds(PAD - 2 + j, t), :], axis=0, keepdims=True)
        _fill_padded(dxc, acc_dxc, t)
        dx = cw_ref[pl.ds(0, 1), :] * dxc[pl.ds(PAD + 2, t), :]
        for j in range(1, CONV_WIDTH):
            dx = dx + cw_ref[pl.ds(j, 1), :] * dxc[pl.ds(PAD + 2 - j, t), :]
        dx_ref[...] = dx.astype(BF16)
        dt_ref[0] = dx.T.astype(BF16)

    col = lambda off: pl.BlockSpec((t, CH), lambda i: (0, off + i))
    small = lambda rows: pl.BlockSpec((rows, CH), lambda i: (0, i))
    dense = pl.BlockSpec((4, None, CH, CH), lambda i: (0, i, 0, 0))
    padded = pltpu.VMEM((t + 2 * PAD, CH), F32)
    return _call(
        body, name=name, args=[proj, proj, cw, cb, gw, gb, lam, hf, hb, dya],
        out_shape=(SDS((t, c), BF16), SDS((t, c), BF16), SDS((2, c, t), BF16), SDS((CONV_WIDTH, c), F32),
                   SDS((1, c), F32), SDS((4, ncb, CH, CH), F32), SDS((4, c), F32), SDS((2, c), F32)),
        grid=(ncb,),
        in_specs=[col(0), col(ncb), small(CONV_WIDTH), small(1), dense, small(4), small(2), col(0), col(0), col(0)],
        out_specs=(col(0), col(0), pl.BlockSpec((2, CH, t), lambda i: (0, i, 0)), small(CONV_WIDTH), small(1),
                   dense, small(4), small(2)),
        scratch_shapes=[padded, padded, padded] + [pltpu.VMEM((t, CH), F32)] * 5, job=job)


Q_ROWS = 4
BAND_ROWS = WIN_ROWS + Q_ROWS
BAND_PAIRS = BAND_ROWS // 2
Q_BLOCK = Q_ROWS * GRID_W
BAND = BAND_ROWS * GRID_W
PAIR_W = 2 * GRID_W
N_BOTH = 2 * WIN_ROWS - 2
ENTRY_LEFT_OUT, ENTRY_RIGHT_OUT, ENTRY_OUT = N_BOTH, N_BOTH + 1, N_BOTH + 2
N_ENTRIES = N_BOTH + 3


def _bias_tables(rpb):
    cols = np.arange(GRID_W)
    start = np.clip(cols - WIN_COLS // 2, 0, GRID_W - WIN_COLS)
    valid = (cols[None, :] >= start[:, None]) & (cols[None, :] < start[:, None] + WIN_COLS)
    col_off = np.clip(cols[None, :] - cols[:, None] + WIN_COLS - 1, 0, 2 * WIN_COLS - 2)
    pick_col = jnp.asarray(np.eye(2 * WIN_COLS - 1, dtype=np.float32)[col_off] * valid[..., None])
    by_row = jnp.einsum("hrc,qkc->hrqk", rpb, pick_col, precision=lax.Precision.HIGHEST)
    by_row = jnp.where(jnp.asarray(valid)[None, None], by_row, NEG)
    out = jnp.full_like(by_row[:, :1], NEG)
    first_in, last_in = WIN_ROWS - 1 - WIN_ROWS // 2, 2 * (WIN_ROWS - 1) - WIN_ROWS // 2
    both = jnp.concatenate([by_row[:, :-1], by_row[:, 1:]], axis=-1)
    left_out = jnp.concatenate([out, by_row[:, first_in:first_in + 1]], axis=-1)
    right_out = jnp.concatenate([by_row[:, last_in:last_in + 1], out], axis=-1)
    return jnp.concatenate([both, left_out, right_out, jnp.concatenate([out, out], axis=-1)], axis=1)


def _band_start(m, rows):
    return jnp.clip(Q_ROWS * m - WIN_ROWS // 2, 0, rows - BAND_ROWS)


def _entry(r, key_row, rows):
    w0 = jnp.clip(r - WIN_ROWS // 2, 0, rows - WIN_ROWS)
    left = (key_row >= w0) & (key_row < w0 + WIN_ROWS)
    right = (key_row + 1 >= w0) & (key_row + 1 < w0 + WIN_ROWS)
    return jnp.where(left & right, key_row - r + WIN_ROWS - 1,
                     jnp.where(right, ENTRY_LEFT_OUT, jnp.where(left, ENTRY_RIGHT_OUT, ENTRY_OUT)))


def _transposed_pairs(dst, src_ref):
    for g in range(dst.shape[0]):
        dst[g] = src_ref[pl.ds(g * PAIR_W, PAIR_W), :].T.astype(BF16)


def _band_of(pairs_ref, first_pair, hh):
    heads = pl.ds(hh * HEAD_DIM, HEAD_DIM)
    return jnp.concatenate([pairs_ref[first_pair + g, heads, :] for g in range(BAND_PAIRS)], axis=1)


def _attn_block(qs, kt, tz_ref, hh, m, rows):
    rs = _band_start(m, rows)
    lanes = pl.ds(hh * HEAD_DIM, HEAD_DIM)
    qrows = pl.ds(pl.multiple_of(m * Q_BLOCK, Q_BLOCK), Q_BLOCK)
    band = pl.ds(pl.multiple_of(rs * GRID_W, PAIR_W), BAND)
    entries = [[_entry(Q_ROWS * m + i, rs + 2 * g, rows) for g in range(BAND_PAIRS)] for i in range(Q_ROWS)]
    bias = jnp.concatenate([jnp.concatenate([tz_ref[hh, e] for e in row], axis=1) for row in entries], axis=0)
    q = qs[qrows, lanes]
    s = jnp.dot(q, _band_of(kt, rs // 2, hh), preferred_element_type=F32) * (HEAD_DIM ** -0.5) + bias
    p = jnp.exp(s - jnp.max(s, axis=-1, keepdims=True))
    p = p / jnp.sum(p, axis=-1, keepdims=True)
    return q, p, qrows, band, lanes, entries, rs // 2


def _attn_fwd(proj, tables, width, name, job=None):
    t = proj.shape[0]
    rows = t // GRID_W
    npair = width // LANES
    first = (proj.shape[1] - 3 * width) // LANES

    def body(q_ref, k_ref, v_ref, tz_ref, o_ref, qs, vs, kt):
        qs[...] = q_ref[...].astype(BF16)
        vs[...] = v_ref[...].astype(BF16)
        _transposed_pairs(kt, k_ref)

        def block(m, carry):
            for hh in range(2):
                _, p, qrows, band, lanes, _, _ = _attn_block(qs, kt, tz_ref, hh, m, rows)
                o_ref[qrows, lanes] = jnp.dot(p.astype(BF16), vs[band, lanes], preferred_element_type=F32)
            return carry

        lax.fori_loop(0, rows // Q_ROWS, block, 0, unroll=2)

    col = lambda off: pl.BlockSpec((t, LANES), lambda i: (0, off + i))
    return _call(body, name=name, args=[proj, proj, proj, tables], out_shape=SDS((t, width), F32), grid=(npair,),
                 in_specs=[col(first), col(first + npair), col(first + 2 * npair),
                           pl.BlockSpec((2, N_ENTRIES, GRID_W, PAIR_W), lambda i: (i, 0, 0, 0))],
                 out_specs=col(0),
                 scratch_shapes=[pltpu.VMEM((t, LANES), BF16)] * 2 + [pltpu.VMEM((t // PAIR_W, LANES, PAIR_W), BF16)],
                 job=job)


def _attn_bwd(proj, tables, dyb, name, job=None):
    t, width = dyb.shape
    rows = t // GRID_W
    npair = width // LANES
    first = (proj.shape[1] - 3 * width) // LANES

    def body(q_ref, k_ref, v_ref, tz_ref, do_ref, dq_ref, dk_ref, dv_ref, dt_ref, dtz_ref, dq_s, dk_s, dv_s,
             qs, ks, vs, dos, kt, vt):
        qs[...] = q_ref[...].astype(BF16)
        ks[...] = k_ref[...].astype(BF16)
        vs[...] = v_ref[...].astype(BF16)
        dos[...] = do_ref[...].astype(BF16)
        _transposed_pairs(kt, k_ref)
        _transposed_pairs(vt, v_ref)
        dk_s[...] = jnp.zeros_like(dk_s)
        dv_s[...] = jnp.zeros_like(dv_s)
        dtz_ref[...] = jnp.zeros_like(dtz_ref)

        def block(m, carry):
            for hh in range(2):
                q, p, qrows, band, lanes, entries, first_pair = _attn_block(qs, kt, tz_ref, hh, m, rows)
                do = dos[qrows, lanes]
                dp = jnp.dot(do, _band_of(vt, first_pair, hh), preferred_element_type=F32)
                ds = p * (dp - jnp.sum(dp * p, axis=-1, keepdims=True))
                for i, row in enumerate(entries):
                    for g, e in enumerate(row):
                        dtz_ref[hh, e] += ds[i * GRID_W:(i + 1) * GRID_W, g * PAIR_W:(g + 1) * PAIR_W]
                dsb = (ds * (HEAD_DIM ** -0.5)).astype(BF16)
                dq_s[qrows, lanes] = jnp.dot(dsb, ks[band, lanes], preferred_element_type=F32)
                dk_s[band, lanes] += lax.dot_general(dsb, q, TN, preferred_element_type=F32)
                dv_s[band, lanes] += lax.dot_general(p.astype(BF16), do, TN, preferred_element_type=F32)
            return carry

        lax.fori_loop(0, rows // Q_ROWS, block, 0)
        for n, (src, dst) in enumerate(((dq_s, dq_ref), (dk_s, dk_ref), (dv_s, dv_ref))):
            val = src[...]
            dst[...] = val.astype(BF16)
            dt_ref[n] = val.T.astype(BF16)

    col = lambda off: pl.BlockSpec((t, LANES), lambda i: (0, off + i))
    table = pl.BlockSpec((2, N_ENTRIES, GRID_W, PAIR_W), lambda i: (i, 0, 0, 0))
    pairs = pltpu.VMEM((t // PAIR_W, LANES, PAIR_W), BF16)
    return _call(body, name=name, args=[proj, proj, proj, tables, dyb],
                 out_shape=(SDS((t, width), BF16),) * 3 + (SDS((3, width, t), BF16), SDS(tables.shape, F32)),
                 grid=(npair,),
                 in_specs=[col(first), col(first + npair), col(first + 2 * npair), table, col(0)],
                 out_specs=(col(0), col(0), col(0), pl.BlockSpec((3, LANES, t), lambda i: (0, i, 0)), table),
                 scratch_shapes=[pltpu.VMEM((t, LANES), F32)] * 3 + [pltpu.VMEM((t, LANES), BF16)] * 4 + [pairs, pairs],
                 job=job)


def _adamw_math(w, g, m, v):
    m = ADAM_B1 * m + (1.0 - ADAM_B1) * g
    v = ADAM_B2 * v + (1.0 - ADAM_B2) * (g * g)
    m_hat = m / (1.0 - ADAM_B1 ** ADAM_STEP)
    v_hat = v / (1.0 - ADAM_B2 ** ADAM_STEP)
    delta = -ADAM_LR * (m_hat / (jnp.sqrt(v_hat) + ADAM_EPS) + ADAM_WD * w)
    return delta, m, v


def _sum_partials(p_ref):
    g = p_ref[0].astype(F32)
    for s in range(1, N_CHIP):
        g = g + p_ref[s].astype(F32)
    return g


def _adamw_rows(w, partials, m, v, name):
    rb, n = w.shape
    tr = 64

    def body(w_ref, p_ref, m_ref, v_ref, g_ref, d_ref, nm_ref, nv_ref):
        g = _sum_partials(p_ref)
        g_ref[...] = g
        d_ref[...], nm_ref[...], nv_ref[...] = _adamw_math(w_ref[...], g, m_ref[...], v_ref[...])

    blk = pl.BlockSpec((tr, n), lambda i: (i, 0))
    return _call(body, name=name, args=[w, partials.reshape(N_CHIP, rb, n), m, v],
                 out_shape=(SDS((rb, n), F32),) * 4, grid=(rb // tr,),
                 in_specs=[blk, pl.BlockSpec((N_CHIP, tr, n), lambda i: (0, i, 0)), blk, blk], out_specs=(blk,) * 4)


def _adamw_cols(w, partials, m, v, name):
    d, nb = w.shape
    td = 256
    parts = list(partials) if isinstance(partials, (list, tuple)) else [partials]
    heights = [p.shape[0] // N_CHIP for p in parts]

    def body(w_ref, m_ref, v_ref, *rest):
        p_refs, (g_ref, d_ref, nm_ref, nv_ref) = rest[:len(parts)], rest[len(parts):]
        g = jnp.concatenate([_sum_partials(p_ref) for p_ref in p_refs], axis=0).T
        g_ref[...] = g
        d_ref[...], nm_ref[...], nv_ref[...] = _adamw_math(w_ref[...], g, m_ref[...], v_ref[...])

    blk = pl.BlockSpec((td, nb), lambda i: (i, 0))
    return _call(body, name=name, args=[w, m, v, *[p.reshape(N_CHIP, h, d) for p, h in zip(parts, heights)]],
                 out_shape=(SDS((d, nb), F32),) * 4, grid=(d // td,),
                 in_specs=[blk, blk, blk] + [pl.BlockSpec((N_CHIP, h, td), lambda i: (0, 0, i)) for h in heights],
                 out_specs=(blk,) * 4)


def _adamw_small(w, g, m, v, name):
    def body(w_ref, g_ref, m_ref, v_ref, d_ref, nm_ref, nv_ref):
        d_ref[...], nm_ref[...], nv_ref[...] = _adamw_math(w_ref[...], g_ref[...], m_ref[...], v_ref[...])

    return _call(body, name=name, args=[w, g, m, v], out_shape=(SDS(w.shape, F32),) * 3, in_specs=[WHOLE] * 4,
                 out_specs=(WHOLE,) * 3)


TILE = SUBLANES * LANES


def _pack(arrays):
    parts = []
    for a in arrays:
        flat = a.reshape(-1).astype(F32)
        flat = jnp.pad(flat, (0, -flat.size % TILE))
        parts.append(flat.reshape(-1, LANES))
    return jnp.concatenate(parts, axis=0)


def _unpack(pack, like):
    out, row = [], 0
    for a in like:
        n = int(np.prod(a.shape))
        nrows = -(-n // TILE) * SUBLANES
        out.append(pack[row:row + nrows].reshape(-1)[:n].reshape(a.shape))
        row += nrows
    return out


def _dense_gate_blocks(gate_w):
    w = gate_w.reshape(4, -1, 2, HEAD_DIM, HEAD_DIM)
    zero = jnp.zeros_like(w[:, :, 0])
    top = jnp.concatenate([w[:, :, 0], zero], axis=-1)
    bottom = jnp.concatenate([zero, w[:, :, 1]], axis=-1)
    return jnp.concatenate([top, bottom], axis=-2)


def _diag_gate_blocks(dense, shape):
    even = dense[:, :, :HEAD_DIM, :HEAD_DIM]
    odd = dense[:, :, HEAD_DIM:, HEAD_DIM:]
    return jnp.stack([even, odd], axis=2).reshape(shape)


LARGE = ("ffn1_w_in", "ffn1_w_out", "w_in_mix", "w_out_mix", "ffn2_w_in", "ffn2_w_out")
COLUMN_SHARDED = ("ffn1_w_in", "w_in_mix", "ffn2_w_in")
SHARDED_SMALL = ("lru_conv_w", "lru_lambda")
REPLICATED = ("norm_ffn1", "norm_mix", "lru_conv_b", "lru_gate_w", "lru_gate_b", "attn_rpb", "lru_out_norm",
              "attn_out_norm", "norm_ffn2", "norm_final")
SMALL_ORDER = REPLICATED + SHARDED_SMALL
WEIGHTS = ("norm_ffn1", "ffn1_w_in", "ffn1_w_out", "norm_mix", "w_in_mix", "lru_conv_w", "lru_conv_b", "lru_gate_w",
           "lru_gate_b", "lru_lambda", "attn_rpb", "lru_out_norm", "attn_out_norm", "w_out_mix", "norm_ffn2",
           "ffn2_w_in", "ffn2_w_out", "norm_final")


PARTS = {("gather", "w_in_mix"): 4, ("gather", "ffn2_w_in"): 8}
CARRIES = {
    "gather_ffn1_in": [(("gather", "ffn1_w_in"), 1), (("gather", "small"), 1)],
    "ffn1_hidden": [(("gather", "ffn1_w_out"), 1), (("gather", "w_in_mix"), 1)],
    "ffn1_out": [(("gather", "w_in_mix"), 3)],
    "mix_in_proj": [(("gather", "w_out_mix"), 1), (("gather", "ffn2_w_in"), 1)],
    "lru_fwd": [(("gather", "ffn2_w_in"), 3)],
    "attn_fwd": [(("gather", "ffn2_w_in"), 3)],
    "mix_out_proj": [(("gather", "ffn2_w_in"), 1)],
    "ffn2_hidden": [(("gather", "ffn2_w_out"), 1)],
    "ffn1_bwd": [(("gather", "small_grads"), 1)],
    "gather_late_grads": [(("gather", "late_grads"), 1)],
}


class _Transfer:
    def __init__(self, kind, src, dest, block_rows, parts):
        self.kind, self.src, self.dest = kind, src, dest
        self.ranges, self.taken = _split(block_rows, parts), 0

    def take(self, count):
        lo, hi = self.ranges[self.taken][0], self.ranges[self.taken + count - 1][1]
        self.taken += count
        return _Piece(self.kind, self.src, self.dest, lo, hi)


class _Traffic:
    def __init__(self):
        self.transfers = {}

    def open(self, kind, name, src, placed=None):
        dest = _gathered(src) if placed is None else placed
        self.transfers[kind, name] = _Transfer(kind, src, dest, dest.shape[0] // N_DEV, PARTS.get((kind, name), 1))

    def _job(self, host):
        moved = [self.transfers[key] for key, _ in CARRIES[host]]
        return moved, _Job([tr.take(count) for tr, (_, count) in zip(moved, CARRIES[host])])

    def carry(self, host, fn, *args, **kw):
        if host not in CARRIES:
            return fn(*args, name=host, **kw)
        moved, job = self._job(host)
        res, landed = fn(*args, name=host, job=job, **kw)
        for tr, arr in zip(moved, landed):
            tr.dest = arr
        return res

    def alone(self, host):
        moved, job = self._job(host)
        for tr, arr in zip(moved, _run_job(job, host)):
            tr.dest = arr

    def result(self, kind, name):
        tr = self.transfers.pop((kind, name))
        assert tr.taken == len(tr.ranges), (kind, name)
        return tr.dest


def _forward_backward(x, target, shards, sharded_small, s):
    c = s["lru_conv_b"].shape[1]
    width = s["attn_out_norm"].shape[1]
    t = x.shape[0]
    traffic = _Traffic()
    carry = traffic.carry
    weight = lambda n: traffic.result("gather", n)

    for n in LARGE:
        traffic.open("gather", n, None, placed=shards[n])
    traffic.open("gather", "small", sharded_small)
    traffic.alone("gather_ffn1_in")
    full_small = weight("small").reshape(N_DEV, SUBLANES, c // N_DEV)
    conv_w = full_small[:, :CONV_WIDTH].transpose(1, 0, 2).reshape(CONV_WIDTH, c)
    lam = full_small[:, CONV_WIDTH:CONV_WIDTH + 2].transpose(1, 0, 2).reshape(2, c)
    w = {"ffn1_w_in": weight("ffn1_w_in")}
    ffn_out = dict(nt=False, out_dtype=F32, tm=1024, tn=512, scale=0.5)
    u1 = _rmsnorm_fwd(x, s["norm_ffn1"], "norm_ffn1")
    g1, up1, hid1, hid1_t = carry("ffn1_hidden", _ffn_hidden, u1, w["ffn1_w_in"])
    w["ffn1_w_out"] = weight("ffn1_w_out")
    h1 = carry("ffn1_out", _mm, hid1, w["ffn1_w_out"], residual=x, **ffn_out)
    w["w_in_mix"] = weight("w_in_mix")
    u2 = _rmsnorm_fwd(h1, s["norm_mix"], "norm_mix")
    proj = carry("mix_in_proj", _mm, u2, w["w_in_mix"], nt=True, out_dtype=F32, tm=1024, tn=512)
    w["w_out_mix"] = weight("w_out_mix")
    gw = _dense_gate_blocks(s["lru_gate_w"]).astype(BF16)
    gb = s["lru_gate_b"].reshape(4, c)
    tables, tables_vjp = jax.vjp(_bias_tables, s["attn_rpb"])
    ya, hf, hb = carry("lru_fwd", _lru_fwd, proj, conv_w, s["lru_conv_b"], gw, gb, lam)
    yb = carry("attn_fwd", _attn_fwd, proj, tables, width)
    y, yt = _mixnorm_fwd(ya, yb, s["lru_out_norm"], s["attn_out_norm"], "mix_norm")
    h2 = carry("mix_out_proj", _mm, y, w["w_out_mix"], nt=False, out_dtype=F32, tm=1024, tn=512, residual=h1)
    u3 = _rmsnorm_fwd(h2, s["norm_ffn2"], "norm_ffn2")
    w["ffn2_w_in"] = weight("ffn2_w_in")
    g2, up2, hid2, hid2_t = carry("ffn2_hidden", _ffn_hidden, u3, w["ffn2_w_in"])
    w["ffn2_w_out"] = weight("ffn2_w_out")
    h3 = carry("ffn2_out", _mm, hid2, w["ffn2_w_out"], residual=h2, **ffn_out)
    dh3, df2, loss_part, d_norm_final = _final_loss(h3, s["norm_final"], target, "final_loss")

    grads = {}
    grad_of = dict(nt=False, out_dtype=BF16, tm=512, tn=2048)

    to_sibling, to_chips = {}, {}

    def reduce_in_chip(n):
        land = _blank_like(grads[n], grads[n].shape[0] // 2, "landing_" + n)
        to_sibling[n], token = _split_start("to_sibling", grads[n], land, "to_sibling_" + n)
        RUN_AFTER.append(token)

    def reduce_over_chips(n, after):
        own, got = _split_wait(to_sibling.pop(n), [after], "from_sibling_" + n)
        summed = _pair_sum(own, got, "pair_sum_" + n)
        to_chips[n], token = _split_start("to_chips", summed, _own_slot(summed, "own_slot_" + n), "to_chips_" + n)
        RUN_AFTER.append(token)
        return token

    f = hid2_t.shape[0]
    grads["ffn2_w_out"] = carry("ffn2_out_grad", _mm, hid2_t, df2, **grad_of)
    reduce_in_chip("ffn2_w_out")
    du3, da2_t = carry("ffn2_bwd", _ffn_bwd, df2, g2, up2, w["ffn2_w_in"], w["ffn2_w_out"])
    reduce_over_chips("ffn2_w_out", du3)
    grads["ffn2_w_in"] = carry("ffn2_in_grad", _mm, da2_t.reshape(2 * f, t), u3, **grad_of)
    reduce_in_chip("ffn2_w_in")
    dh2, dh2b, d_norm_ffn2 = carry("norm_ffn2_bwd", _rmsnorm_bwd, du3, h2, s["norm_ffn2"], dh3, 1.0)
    grads["w_out_mix"] = carry("mix_out_grad", _mm, yt, dh2b, **grad_of)
    reduce_over_chips("ffn2_w_in", grads["w_out_mix"])
    reduce_in_chip("w_out_mix")
    dy = carry("mix_out_bwd", _mm, dh2b, w["w_out_mix"], nt=True, out_dtype=F32, tm=1024, tn=512)
    dya, dyb, d_lru_out_norm, d_attn_out_norm = _mixnorm_bwd(dy, ya, yb, s["lru_out_norm"], s["attn_out_norm"],
                                                             "mix_norm_bwd")
    dq, dk, dv, dqkv_t, d_tables = carry("attn_bwd", _attn_bwd, proj, tables, dyb)
    reduce_over_chips("w_out_mix", dq)
    dx_lru, dg_lru, dxg_t, d_conv_w, d_conv_b, d_gw, d_gb, d_lam = carry(
        "lru_bwd", _lru_bwd, proj, conv_w, s["lru_conv_b"], gw, gb, lam, hf, hb, dya)
    rows_of = 2 * c + 3 * width
    lru_rows = carry("mix_in_grad_lru", _mm, dxg_t.reshape(2 * c, t), u2, out_rows=rows_of, **grad_of)
    grads["w_in_mix"] = carry("mix_in_grad_attn", _mm, dqkv_t.reshape(3 * width, t), u2, out_rows=rows_of,
                              row_offset=2 * c, into=lru_rows, **grad_of)
    reduce_in_chip("w_in_mix")
    du2 = carry("mix_in_bwd", _mm, [dx_lru, dg_lru, dq, dk, dv], w["w_in_mix"], nt=False, out_dtype=F32, tm=1024,
                tn=512)
    dh1, df1, d_norm_mix = carry("norm_mix_bwd", _rmsnorm_bwd, du2, h1, s["norm_mix"], dh2, 0.5)
    reduce_over_chips("w_in_mix", dh1)

    by_device = lambda a: a.reshape(a.shape[0], N_DEV, -1).transpose(1, 0, 2)
    small = {
        "norm_mix": d_norm_mix, "lru_conv_b": d_conv_b, "lru_gate_w": _diag_gate_blocks(d_gw, s["lru_gate_w"].shape),
        "lru_gate_b": d_gb.reshape(s["lru_gate_b"].shape), "attn_rpb": tables_vjp(d_tables)[0],
        "lru_out_norm": d_lru_out_norm, "attn_out_norm": d_attn_out_norm, "norm_ffn2": d_norm_ffn2,
        "norm_final": d_norm_final, "lru_conv_w": by_device(d_conv_w), "lru_lambda": by_device(d_lam),
    }
    early = [small[n] for n in SMALL_ORDER[1:]]
    traffic.open("gather", "small_grads", _pack(early))

    grads["ffn1_w_out"] = carry("ffn1_out_grad", _mm, hid1_t, df1, **grad_of)
    reduce_in_chip("ffn1_w_out")
    du1, da1_t = carry("ffn1_bwd", _ffn_bwd, df1, g1, up1, w["ffn1_w_in"], w["ffn1_w_out"])
    grad_x, _, d_norm_ffn1 = carry("norm_ffn1_bwd", _rmsnorm_bwd, du1, x, s["norm_ffn1"], dh1, 1.0)
    traffic.open("gather", "late_grads", _pack([d_norm_ffn1]))
    traffic.alone("gather_late_grads")
    late = traffic.result("gather", "late_grads")
    reduce_over_chips("ffn1_w_out", late)
    half = 2 * f // N_DEV // 2
    half_rows = lambda h: (half, N_DEV, lambda i: 2 * i + h)
    grads["ffn1_w_in_a"] = carry("ffn1_in_grad_a", _mm, da1_t.reshape(2 * f, t), u1, take=half_rows(0), **grad_of)
    reduce_in_chip("ffn1_w_in_a")
    grads["ffn1_w_in_b"] = carry("ffn1_in_grad_b", _mm, da1_t.reshape(2 * f, t), u1, take=half_rows(1), **grad_of)
    reduce_over_chips("ffn1_w_in_a", grads["ffn1_w_in_b"])
    reduce_in_chip("ffn1_w_in_b")
    reduced = (_unpack(_sum_devices(late, "sum_late_grads"), [d_norm_ffn1])
               + _unpack(_sum_devices(traffic.result("gather", "small_grads"), "sum_small_grads"), early))
    last_token = reduce_over_chips("ffn1_w_in_b", reduced[1])
    RUN_AFTER.clear()
    assert not traffic.transfers and not to_sibling, (list(traffic.transfers), list(to_sibling))
    return loss_part[0, 0], grad_x, to_chips, last_token, dict(zip(SMALL_ORDER, reduced))


def _step(x, loss_target, p, m, v):
    me = 4 * lax.axis_index("x") + 2 * lax.axis_index("y") + lax.axis_index("c")

    shards = {n: _cast_into_place(p[n], n in COLUMN_SHARDED, "cast_" + n) for n in LARGE}
    sharded_small = (jnp.pad(p["lru_conv_w"], ((0, SUBLANES - CONV_WIDTH), (0, 0)))
                     + jnp.pad(p["lru_lambda"], ((CONV_WIDTH, SUBLANES - CONV_WIDTH - 2), (0, 0))))
    s = {n: p[n] if n in ("lru_gate_w", "lru_gate_b", "attn_rpb") else p[n].reshape(1, -1) for n in REPLICATED}

    loss_part, grad_x, to_chips, last_token, small = _forward_backward(x, loss_target, shards, sharded_small, s)
    loss = lax.psum(loss_part, ("x", "y", "c"))

    def landed(n, after):
        return _split_wait(to_chips[n], after, "from_chips_" + n)[1]

    def update(n, partials):
        return (_adamw_cols if n in COLUMN_SHARDED else _adamw_rows)(p[n], partials, m[n], v[n], "adamw_" + n)

    out = {n: update(n, landed(n, [last_token])) for n in LARGE if n != "ffn1_w_in"}
    done = [o[3] for o in out.values()]
    out["ffn1_w_in"] = update("ffn1_w_in", [landed("ffn1_w_in_a", done), landed("ffn1_w_in_b", done)])

    g_small = {n: lax.dynamic_index_in_dim(g, me, axis=0, keepdims=False) if n in SHARDED_SMALL else g
               for n, g in small.items()}
    names = SMALL_ORDER
    like = [p[n] for n in names]
    pack_of = lambda d: _pack([d[n].reshape(p[n].shape) for n in names])
    upd = _adamw_small(pack_of(p), pack_of(g_small), pack_of(m), pack_of(v), "adamw_small")
    for n, d_, m_, v_ in zip(names, *[_unpack(u, like) for u in upd]):
        out[n] = (g_small[n].reshape(p[n].shape), d_, m_, v_)
    return loss, grad_x, out


def kernel(x, norm_ffn1, ffn1_w_in, ffn1_w_out, norm_mix, w_in_mix, lru_conv_w, lru_conv_b, lru_gate_w, lru_gate_b, lru_lambda, attn_rpb, lru_out_norm, attn_out_norm, w_out_mix, norm_ffn2, ffn2_w_in, ffn2_w_out, norm_final, loss_target, m_norm_ffn1, m_ffn1_w_in, m_ffn1_w_out, m_norm_mix, m_w_in_mix, m_lru_conv_w, m_lru_conv_b, m_lru_gate_w, m_lru_gate_b, m_lru_lambda, m_attn_rpb, m_lru_out_norm, m_attn_out_norm, m_w_out_mix, m_norm_ffn2, m_ffn2_w_in, m_ffn2_w_out, m_norm_final, v_norm_ffn1, v_ffn1_w_in, v_ffn1_w_out, v_norm_mix, v_w_in_mix, v_lru_conv_w, v_lru_conv_b, v_lru_gate_w, v_lru_gate_b, v_lru_lambda, v_attn_rpb, v_lru_out_norm, v_attn_out_norm, v_w_out_mix, v_norm_ffn2, v_ffn2_w_in, v_ffn2_w_out, v_norm_final):
    given = dict(locals())
    drop_layer = lambda n, a: a if n == "norm_final" else a[0]
    p = {n: drop_layer(n, given[n]) for n in WEIGHTS}
    m = {n: drop_layer(n, given["m_" + n]) for n in WEIGHTS}
    v = {n: drop_layer(n, given["v_" + n]) for n in WEIGHTS}
    loss, grad_x, out = _step(x[0], loss_target[0], p, m, v)
    shaped = lambda n, a: a.reshape(given[n].shape)
    return (loss, grad_x[None], *[shaped(n, out[n][k]) for k in range(4) for n in WEIGHTS])
```

```python
import math

import numpy as np
import jax
import jax.numpy as jnp
from jax import lax
from jax.experimental import pallas as pl
from jax.experimental.pallas import tpu as pltpu

F32 = jnp.float32
BF16 = jnp.bfloat16
SDS = jax.ShapeDtypeStruct

N_DEV = 8
N_CHIP = 4
NORM_EPS = 1e-6
RG_C = 8.0
CONV_WIDTH = 4
HEAD_DIM = 64
GRID_W = 64
WIN_ROWS = 8
WIN_COLS = 16
NEG = -1e30

ADAM_LR = 0.001
ADAM_B1 = 0.9
ADAM_B2 = 0.999
ADAM_EPS = 1e-08
ADAM_WD = 0.01
ADAM_STEP = 10

LANES = 128
SUBLANES = 8
VMEM_LIMIT = 56 * 1024 * 1024

NT = (((1,), (1,)), ((), ()))
TN = (((0,), (0,)), ((), ()))
ANY = pl.BlockSpec(memory_space=pl.ANY)
WHOLE = pl.BlockSpec(memory_space=pltpu.VMEM)
MESH = pl.DeviceIdType.MESH


def _sigmoid(x):
    return 1.0 / (1.0 + jnp.exp(-x))


def _gelu_parts(x):
    c = math.sqrt(2.0 / math.pi)
    t = jnp.tanh(c * (x + 0.044715 * (x * x * x)))
    gelu = 0.5 * x * (1.0 + t)
    dgelu = 0.5 * (1.0 + t) + 0.5 * x * (1.0 - t * t) * (c * (1.0 + 3.0 * 0.044715 * (x * x)))
    return gelu, dgelu


def _expm1(x):
    poly = x * (1.0 + x * (1.0 / 2) * (1.0 + x * (1.0 / 3) * (1.0 + x * (1.0 / 4) * (1.0 + x * (1.0 / 5) * (1.0 + x * (1.0 / 6))))))
    return jnp.where(jnp.abs(x) < 0.25, poly, jnp.exp(x) - 1.0)


def _softplus(x):
    return jnp.maximum(x, 0.0) + jnp.log1p(jnp.exp(-jnp.abs(x)))


class _Piece:
    N_REMOTE = {"gather": 7}
    N_LOCAL = {"gather": 1}

    def __init__(self, kind, src, dest, lo, hi):
        self.kind, self.src, self.dest, self.lo, self.hi = kind, src, dest, lo, hi


RELAY_AT = 60
RUN_AFTER = []


class _Job:
    def __init__(self, pieces):
        self.pieces = list(pieces)
        self.ins = [p.src for p in self.pieces if p.src is not None]
        self.out_shapes = [SDS(p.dest.shape, p.dest.dtype) for p in self.pieces]
        self.aliased = [i for i, p in enumerate(self.pieces) if not isinstance(p.dest, SDS)]
        self.n_remote = sum(_Piece.N_REMOTE[p.kind] for p in self.pieces)
        self.n_local = max(sum(_Piece.N_LOCAL[p.kind] for p in self.pieces), 1)

    def _each(self, step, ins, outs, send_sems, recv_sems, local_sems):
        remote = local = 0
        ins = iter(ins)
        for p, dst in zip(self.pieces, outs):
            src = None if p.src is None else next(ins)
            _EXCHANGES[p.kind](step, p, src, dst, send_sems, recv_sems, local_sems, remote, local)
            remote += _Piece.N_REMOTE[p.kind]
            local += _Piece.N_LOCAL[p.kind]

    def start(self, *refs):
        self._each("start", *refs)

    def relay(self, *refs):
        self._each("relay", *refs)

    def finish(self, *refs):
        self._each("finish", *refs)


def _call(body, *, name, args, out_shape, in_specs, out_specs, grid=(), scratch_shapes=(), aliases=None, job=None):
    single = not isinstance(out_shape, (tuple, list))
    out_shape = (out_shape,) if single else tuple(out_shape)
    out_specs = (out_specs,) if single else tuple(out_specs)
    aliases = dict(aliases or {})
    if RUN_AFTER:
        tokens, n_plain, plain_body = list(RUN_AFTER), len(args), body
        RUN_AFTER.clear()
        body = lambda *refs: plain_body(*refs[:n_plain], *refs[n_plain + len(tokens):])
        args, in_specs = list(args) + tokens, list(in_specs) + [ANY] * len(tokens)
    params = pltpu.CompilerParams(dimension_semantics=("arbitrary",) * len(grid) if grid else None,
                                  vmem_limit_bytes=VMEM_LIMIT)
    if job is None:
        res = pl.pallas_call(body, out_shape=out_shape, grid=grid, in_specs=list(in_specs), out_specs=out_specs,
                             scratch_shapes=list(scratch_shapes), input_output_aliases=aliases, name=name,
                             compiler_params=params)(*args)
        return res[0] if single else res

    n_in, n_out, n_scr = len(args), len(out_shape), len(scratch_shapes)
    j_in, j_out, j_alias = len(job.ins), len(job.out_shapes), len(job.aliased)

    def hosted(*refs):
        ins, refs = refs[:n_in], refs[n_in:]
        j_ins, refs = refs[:j_in], refs[j_in + j_alias:]
        outs, refs = refs[:n_out], refs[n_out:]
        j_outs, refs = refs[:j_out], refs[j_out:]
        scr, sems = refs[:n_scr], refs[n_scr:]
        if grid:
            step = 0
            for axis, size in enumerate(grid):
                step = step * size + pl.program_id(axis)
            steps = math.prod(grid)
            pl.when(step == 0)(lambda: job.start(j_ins, j_outs, *sems))
            body(*ins, *outs, *scr)
            pl.when(step == min(RELAY_AT * steps // 100, steps - 1))(lambda: job.relay(j_ins, j_outs, *sems))
            pl.when(step == steps - 1)(lambda: job.finish(j_ins, j_outs, *sems))
        else:
            job.start(j_ins, j_outs, *sems)
            body(*ins, *outs, *scr)
            job.relay(j_ins, j_outs, *sems)
            job.finish(j_ins, j_outs, *sems)

    res = pl.pallas_call(
        hosted, out_shape=out_shape + tuple(job.out_shapes), grid=grid,
        in_specs=list(in_specs) + [ANY] * (j_in + j_alias), out_specs=out_specs + (ANY,) * j_out,
        scratch_shapes=list(scratch_shapes) + [pltpu.SemaphoreType.DMA((job.n_remote,)),
                                               pltpu.SemaphoreType.DMA((job.n_remote,)),
                                               pltpu.SemaphoreType.DMA((job.n_local,))],
        input_output_aliases={**aliases, **{n_in + j_in + k: n_out + i for k, i in enumerate(job.aliased)}},
        name=name, compiler_params=params)(*args, *job.ins, *[job.pieces[i].dest for i in job.aliased])
    own, carried = res[:n_out], res[n_out:]
    return (own[0] if single else own), carried


def _run_job(job, name):
    return _call(lambda: None, name=name, args=[], out_shape=(), in_specs=[], out_specs=(), job=job)[1]


def _position():
    return lax.axis_index("x"), lax.axis_index("y"), lax.axis_index("c")


def _flat(px, py, pc):
    return 4 * px + 2 * py + pc


def _gather_exchange(step, p, src, dst, send_sems, recv_sems, local_sems, r0, l0):
    x, y, c = _position()
    me, sibling = (x, y, c), (x, y, 1 - c)
    along_x, along_y, diagonal = (1 - x, y), (x, 1 - y), (1 - x, 1 - y)
    south = c == 0
    passed_on = (jnp.where(south, 1 - x, x), jnp.where(south, y, 1 - y))
    passed_to = (jnp.where(south, x, 1 - x), jnp.where(south, 1 - y, y))
    placed = p.src is None
    rb, n_rows = p.dest.shape[0] // N_DEV, p.hi - p.lo

    def rows(block):
        return dst.at[pl.ds(_flat(*block) * rb + p.lo, n_rows), :]

    mine = rows(me) if placed else src.at[pl.ds(p.lo, n_rows), :]

    def copy(k, block, to, own=False):
        return pltpu.make_async_remote_copy(
            src_ref=mine if own else rows(block), dst_ref=rows(block),
            send_sem=send_sems.at[r0 + k], recv_sem=recv_sems.at[r0 + k], device_id=to, device_id_type=MESH)

    local = None if placed else pltpu.make_async_copy(mine, rows(me), local_sems.at[l0])
    if step == "start":
        if local is not None:
            local.start()
        copy(0, me, sibling, own=True).start()
        copy(1, me, (*along_x, c), own=True).start()
        copy(2, me, (*along_y, c), own=True).start()
    elif step == "relay":
        copy(1, (*along_x, c), me).wait_recv()
        copy(2, (*along_y, c), me).wait_recv()
        copy(3, (*passed_on, c), (*passed_to, c)).start()
        copy(4, (*along_x, c), sibling).start()
        copy(5, (*along_y, c), sibling).start()
    else:
        copy(3, (*diagonal, c), me).wait_recv()
        copy(6, (*diagonal, c), sibling).start()
        copy(0, sibling, me).wait_recv()
        copy(4, (*along_x, 1 - c), me).wait_recv()
        copy(5, (*along_y, 1 - c), me).wait_recv()
        copy(6, (*diagonal, 1 - c), me).wait_recv()
        copy(0, me, sibling, own=True).wait_send()
        copy(1, me, (*along_x, c), own=True).wait_send()
        copy(2, me, (*along_y, c), own=True).wait_send()
        copy(3, (*passed_on, c), (*passed_to, c)).wait_send()
        copy(4, (*along_x, c), sibling).wait_send()
        copy(5, (*along_y, c), sibling).wait_send()
        copy(6, (*diagonal, c), sibling).wait_send()
        if local is not None:
            local.wait()


CHIP_FLIPS = [(1, 0), (0, 1), (1, 1)]
_EXCHANGES = {"gather": _gather_exchange}


def _gathered(shard):
    return SDS((N_DEV * shard.shape[0], shard.shape[1]), shard.dtype)


def _split(rows, parts):
    cuts = [rows * k // parts // 16 * 16 for k in range(parts)] + [rows]
    return list(zip(cuts[:-1], cuts[1:]))


def _pair_sum(g, from_sibling, name):
    rb, n = g.shape[0] // N_DEV, g.shape[1]
    tr = rb if rb * n * 2 <= 3 * 1024 * 1024 else rb // 2
    core = lax.axis_index("c").astype(jnp.int32).reshape(1)

    def body(c_ref, g_ref, r_ref, o_ref):
        o_ref[...] = (g_ref[...].astype(F32) + r_ref[...].astype(F32)).astype(BF16)

    grid_spec = pltpu.PrefetchScalarGridSpec(
        num_scalar_prefetch=1, grid=(N_CHIP, rb // tr),
        in_specs=[pl.BlockSpec((None, None, tr, n), lambda q, i, c_ref: (q, c_ref[0], i, 0)),
                  pl.BlockSpec((None, tr, n), lambda q, i, c_ref: (q, i, 0))],
        out_specs=pl.BlockSpec((None, tr, n), lambda q, i, c_ref: (q, i, 0)))
    out = pl.pallas_call(
        body, grid_spec=grid_spec, out_shape=SDS((N_CHIP, rb, n), BF16), name=name,
        compiler_params=pltpu.CompilerParams(dimension_semantics=("arbitrary",) * 2, vmem_limit_bytes=VMEM_LIMIT))(
            core, g.reshape(N_CHIP, 2, rb, n), from_sibling.reshape(N_CHIP, rb, n))
    return out.reshape(N_CHIP * rb, n)


SEM = pl.BlockSpec(memory_space=pltpu.SEMAPHORE)
IN_HBM = pl.BlockSpec(memory_space=pltpu.HBM)
SIDE_EFFECT = pltpu.SideEffectType.DATAFLOW_SIDE_EFFECTING


def _own_slot(partials, name):
    rb, n = partials.shape[0] // N_CHIP, partials.shape[1]
    tr = rb // 2
    chip = (2 * lax.axis_index("x") + lax.axis_index("y")).astype(jnp.int32).reshape(1)

    def body(chip_ref, src_ref, dst_ref):
        dst_ref[...] = src_ref[...]

    block = pl.BlockSpec((None, tr, n), lambda i, chip_ref: (chip_ref[0], i, 0))
    grid_spec = pltpu.PrefetchScalarGridSpec(num_scalar_prefetch=1, grid=(rb // tr,), in_specs=[block], out_specs=block)
    out = pl.pallas_call(
        body, grid_spec=grid_spec, out_shape=SDS((N_CHIP, rb, n), partials.dtype), name=name,
        compiler_params=pltpu.CompilerParams(dimension_semantics=("arbitrary",), vmem_limit_bytes=VMEM_LIMIT))(
            chip, partials.reshape(N_CHIP, rb, n))
    return out.reshape(partials.shape)


def _blank_like(src, rows, name):
    return pl.pallas_call(lambda src_ref, out_ref: None, out_shape=SDS((rows, src.shape[1]), src.dtype),
                          in_specs=[ANY], out_specs=ANY, name=name)(src)


def _chip_copies(src_ref, land_ref, sems):
    x, y, c = _position()
    rb = src_ref.shape[0] // N_CHIP
    copies = []
    for k, (fx, fy) in enumerate(CHIP_FLIPS):
        px, py = (1 - x if fx else x), (1 - y if fy else y)
        copies.append(pltpu.make_async_remote_copy(
            src_ref=src_ref.at[pl.ds((2 * px + py) * rb, rb), :], dst_ref=land_ref.at[pl.ds((2 * x + y) * rb, rb), :],
            send_sem=sems[2 * k], recv_sem=sems[2 * k + 1], device_id=(px, py, c), device_id_type=MESH))
    return copies


def _sibling_copies(src_ref, land_ref, sems):
    x, y, c = _position()
    rb = src_ref.shape[0] // N_DEV
    return [pltpu.make_async_remote_copy(
        src_ref=src_ref.at[pl.ds((2 * q + 1 - c) * rb, rb), :], dst_ref=land_ref.at[pl.ds(q * rb, rb), :],
        send_sem=sems[2 * q], recv_sem=sems[2 * q + 1], device_id=(x, y, 1 - c), device_id_type=MESH)
        for q in range(N_CHIP)]


SPLIT_COPIES = {"to_chips": (_chip_copies, 3), "to_sibling": (_sibling_copies, N_CHIP)}


def _split_start(kind, src, land, name):
    copies_of, n_copies = SPLIT_COPIES[kind]

    def body(src_ref, land_ref, *rest):
        sems, token = rest[:2 * n_copies], rest[-1]
        for copy in copies_of(src_ref, land_ref, sems):
            copy.start()
        token[...] = jnp.zeros_like(token)

    res = pl.pallas_call(
        body, name=name,
        out_shape=(pltpu.SemaphoreType.DMA(()),) * (2 * n_copies)
        + (pltpu.HBM(src.shape, src.dtype), pltpu.HBM(land.shape, land.dtype), SDS((SUBLANES, LANES), F32)),
        in_specs=(IN_HBM, IN_HBM), out_specs=(SEM,) * (2 * n_copies) + (IN_HBM, IN_HBM, WHOLE),
        input_output_aliases={0: 2 * n_copies, 1: 2 * n_copies + 1},
        compiler_params=pltpu.CompilerParams(has_side_effects=SIDE_EFFECT))(
            pltpu.with_memory_space_constraint(src, pltpu.HBM), pltpu.with_memory_space_constraint(land, pltpu.HBM))
    return (kind, res[:2 * n_copies], res[-3], res[-2]), res[-1]


def _split_wait(pending, after, name):
    kind, sems, src, land = pending
    copies_of, n_copies = SPLIT_COPIES[kind]

    def body(src_ref, land_ref, *rest):
        for copy in copies_of(src_ref, land_ref, rest[:2 * n_copies]):
            copy.wait_send()
            copy.wait_recv()

    return pl.pallas_call(
        body, name=name, out_shape=(pltpu.HBM(src.shape, src.dtype), pltpu.HBM(land.shape, land.dtype)),
        in_specs=(IN_HBM, IN_HBM) + (SEM,) * (2 * n_copies) + (ANY,) * len(after), out_specs=(IN_HBM, IN_HBM),
        input_output_aliases={0: 0, 1: 1},
        compiler_params=pltpu.CompilerParams(has_side_effects=SIDE_EFFECT))(src, land, *sems, *after)


def _sum_devices(gathered, name):
    r = gathered.shape[0] // N_DEV

    def body(g_ref, o_ref):
        acc = g_ref[0]
        for s in range(1, N_DEV):
            acc = acc + g_ref[s]
        o_ref[...] = acc

    return _call(body, name=name, args=[gathered.reshape(N_DEV, r, LANES)], out_shape=SDS((r, LANES), F32),
                 in_specs=[WHOLE], out_specs=WHOLE)


def _cast_into_place(w, transposed, name):
    me = _flat(*_position()).astype(jnp.int32).reshape(1)
    if transposed:
        d, rb = w.shape
        td = 512
        grid = (d // td,)
        in_spec = pl.BlockSpec((td, rb), lambda i, me_ref: (i, 0))
        out_spec = pl.BlockSpec((rb, td), lambda i, me_ref: (me_ref[0], i))
    else:
        rb, d = w.shape
        grid = (1,)
        in_spec = pl.BlockSpec((rb, d), lambda i, me_ref: (0, 0))
        out_spec = pl.BlockSpec((rb, d), lambda i, me_ref: (me_ref[0], 0))

    def body(me_ref, w_ref, o_ref):
        value = w_ref[...]
        o_ref[...] = (value.T if transposed else value).astype(BF16)

    grid_spec = pltpu.PrefetchScalarGridSpec(num_scalar_prefetch=1, grid=grid, in_specs=[in_spec], out_specs=out_spec)
    return pl.pallas_call(
        body, grid_spec=grid_spec, out_shape=SDS((N_DEV * rb, d), BF16), name=name,
        compiler_params=pltpu.CompilerParams(dimension_semantics=("arbitrary",), vmem_limit_bytes=VMEM_LIMIT))(me, w)


ROW_TILE = 256


def _rmsnorm_fwd(h, gain, name):
    t, d = h.shape

    def body(h_ref, g_ref, u_ref):
        x = h_ref[...]
        u_ref[...] = (x * lax.rsqrt(jnp.mean(x * x, axis=-1, keepdims=True) + NORM_EPS) * g_ref[...]).astype(BF16)

    row = pl.BlockSpec((ROW_TILE, d), lambda i: (i, 0))
    return _call(body, name=name, args=[h, gain], out_shape=SDS((t, d), BF16), grid=(t // ROW_TILE,),
                 in_specs=[row, pl.BlockSpec((1, d), lambda i: (0, 0))], out_specs=row)


def _rms_bwd_math(x, gain, dy):
    rstd = lax.rsqrt(jnp.mean(x * x, axis=-1, keepdims=True) + NORM_EPS)
    xhat = x * rstd
    dxh = dy * gain
    dx = rstd * (dxh - xhat * jnp.mean(dxh * xhat, axis=-1, keepdims=True))
    return dx, jnp.sum(dy * xhat, axis=0, keepdims=True)


def _rmsnorm_bwd(du, h, gain, resid, bf_scale, name, job=None):
    t, d = h.shape

    def body(du_ref, h_ref, g_ref, r_ref, dh_ref, dhb_ref, dg_ref):
        @pl.when(pl.program_id(0) == 0)
        def _():
            dg_ref[...] = jnp.zeros_like(dg_ref)

        dx, dg = _rms_bwd_math(h_ref[...], g_ref[...], du_ref[...])
        dh = r_ref[...] + dx
        dh_ref[...] = dh
        dhb_ref[...] = (bf_scale * dh).astype(BF16)
        dg_ref[...] += dg

    row = pl.BlockSpec((ROW_TILE, d), lambda i: (i, 0))
    vec = pl.BlockSpec((1, d), lambda i: (0, 0))
    return _call(body, name=name, args=[du, h, gain, resid],
                 out_shape=(SDS((t, d), F32), SDS((t, d), BF16), SDS((1, d), F32)), grid=(t // ROW_TILE,),
                 in_specs=[row, row, vec, row], out_specs=(row, row, vec), job=job)


def _final_loss(h, gain, target, name):
    t, d = h.shape

    def body(h_ref, g_ref, t_ref, dh_ref, dhb_ref, loss_ref, dg_ref):
        @pl.when(pl.program_id(0) == 0)
        def _():
            dg_ref[...] = jnp.zeros_like(dg_ref)
            loss_ref[...] = jnp.zeros_like(loss_ref)

        x = h_ref[...]
        gain = g_ref[...]
        out = x * lax.rsqrt(jnp.mean(x * x, axis=-1, keepdims=True) + NORM_EPS) * gain
        err = out - t_ref[...]
        loss_ref[...] += 0.5 * jnp.sum(jnp.mean(err * err, axis=-1, keepdims=True), axis=0, keepdims=True)
        dx, dg = _rms_bwd_math(x, gain, err * (1.0 / d))
        dh_ref[...] = dx
        dhb_ref[...] = (0.5 * dx).astype(BF16)
        dg_ref[...] += dg

    row = pl.BlockSpec((ROW_TILE, d), lambda i: (i, 0))
    vec = pl.BlockSpec((1, d), lambda i: (0, 0))
    one = pl.BlockSpec((SUBLANES, LANES), lambda i: (0, 0))
    return _call(body, name=name, args=[h, gain, target],
                 out_shape=(SDS((t, d), F32), SDS((t, d), BF16), SDS((SUBLANES, LANES), F32), SDS((1, d), F32)),
                 grid=(t // ROW_TILE,), in_specs=[row, vec, row], out_specs=(row, row, one, vec))


def _mixnorm_fwd(ya, yb, ga, gb, name):
    t, c = ya.shape

    def body(ya_ref, yb_ref, ga_ref, gb_ref, y_ref, yt_ref):
        for k, (src, g_ref) in enumerate(((ya_ref, ga_ref), (yb_ref, gb_ref))):
            x = src[...]
            u = x * lax.rsqrt(jnp.mean(x * x, axis=-1, keepdims=True) + NORM_EPS) * g_ref[...]
            y_ref[:, k * c:(k + 1) * c] = u.astype(BF16)
            yt_ref[k * c:(k + 1) * c, :] = u.T.astype(BF16)

    row = pl.BlockSpec((ROW_TILE, c), lambda i: (i, 0))
    vec = pl.BlockSpec((1, c), lambda i: (0, 0))
    return _call(body, name=name, args=[ya, yb, ga, gb],
                 out_shape=(SDS((t, 2 * c), BF16), SDS((2 * c, t), BF16)), grid=(t // ROW_TILE,),
                 in_specs=[row, row, vec, vec],
                 out_specs=(pl.BlockSpec((ROW_TILE, 2 * c), lambda i: (i, 0)),
                            pl.BlockSpec((2 * c, ROW_TILE), lambda i: (0, i))))


def _mixnorm_bwd(dy, ya, yb, ga, gb, name):
    t, c = ya.shape

    def body(dy_ref, ya_ref, yb_ref, ga_ref, gb_ref, dya_ref, dyb_ref, dga_ref, dgb_ref):
        @pl.when(pl.program_id(0) == 0)
        def _():
            dga_ref[...] = jnp.zeros_like(dga_ref)
            dgb_ref[...] = jnp.zeros_like(dgb_ref)

        dxa, dga = _rms_bwd_math(ya_ref[...], ga_ref[...], dy_ref[:, :c])
        dxb, dgb = _rms_bwd_math(yb_ref[...], gb_ref[...], dy_ref[:, c:])
        dya_ref[...] = dxa
        dyb_ref[...] = dxb
        dga_ref[...] += dga
        dgb_ref[...] += dgb

    row = pl.BlockSpec((ROW_TILE, c), lambda i: (i, 0))
    vec = pl.BlockSpec((1, c), lambda i: (0, 0))
    return _call(body, name=name, args=[dy, ya, yb, ga, gb],
                 out_shape=(SDS((t, c), F32), SDS((t, c), F32), SDS((1, c), F32), SDS((1, c), F32)),
                 grid=(t // ROW_TILE,),
                 in_specs=[pl.BlockSpec((ROW_TILE, 2 * c), lambda i: (i, 0)), row, row, vec, vec],
                 out_specs=(row, row, vec, vec))


def _tile(n, want):
    return max(t for t in range(LANES, min(n, want) + 1, LANES) if n % t == 0)


def _mm(a, b, *, nt, out_dtype, tm, tn, name, residual=None, scale=None, take=None, out_rows=None, row_offset=0,
        into=None, job=None):
    parts = list(a) if isinstance(a, (list, tuple)) else [a]
    widths = [p.shape[-1] for p in parts]
    k = sum(widths)
    n = b.shape[0] if nt else b.shape[1]
    if take is None:
        m, which = parts[0].shape[0], lambda i: i
        tm = _tile(math.gcd(m, row_offset), tm)
    else:
        tm, tiles, which = take
        m = tm * tiles
    tn = _tile(n, tn)
    out_rows = m if out_rows is None else out_rows

    def body(*refs):
        a_refs, b_ref, rest = refs[:len(parts)], refs[len(parts)], refs[len(parts) + 1:]
        o_ref = rest[-1]
        out, at = None, 0
        for a_ref, width in zip(a_refs, widths):
            av = a_ref[...].astype(BF16)
            if nt:
                term = lax.dot_general(av, b_ref[:, at:at + width].astype(BF16), NT, preferred_element_type=F32)
            else:
                term = jnp.dot(av, b_ref[at:at + width, :].astype(BF16), preferred_element_type=F32)
            out = term if out is None else out + term
            at += width
        if residual is not None:
            out = rest[0][...] + (out if scale is None else scale * out)
        o_ref[...] = out.astype(out_dtype)

    a_specs = [pl.BlockSpec((tm, width), lambda i, j: (which(i), 0)) for width in widths]
    in_specs = a_specs + [pl.BlockSpec((tn, k), lambda i, j: (j, 0)) if nt else pl.BlockSpec((k, tn), lambda i, j: (0, j))]
    args, aliases = parts + [b], {}
    if residual is not None:
        in_specs.append(pl.BlockSpec((tm, tn), lambda i, j: (i, j)))
        args.append(residual)
    if into is not None:
        in_specs.append(ANY)
        aliases[len(args)] = 0
        args.append(into)
    return _call(body, name=name, args=args, out_shape=SDS((out_rows, n), out_dtype), grid=(m // tm, n // tn),
                 in_specs=in_specs, out_specs=pl.BlockSpec((tm, tn), lambda i, j: (row_offset // tm + i, j)),
                 aliases=aliases, job=job)


FFN_HB = 512
HIDDEN_TM = 1024
BWD_TM = 1024


def _ffn_hidden(u, w_in_t, name, job=None):
    t, d = u.shape
    f = w_in_t.shape[0] // 2

    def body(u_ref, w_ref, g_ref, up_ref, hid_ref, hid_t_ref):
        uu = u_ref[...]
        g = lax.dot_general(uu, w_ref[0], NT, preferred_element_type=F32)
        up = lax.dot_general(uu, w_ref[1], NT, preferred_element_type=F32)
        g_ref[...] = g.astype(BF16)
        up_ref[...] = up.astype(BF16)
        hid = (g * _sigmoid(g)) * up
        hid_ref[...] = hid.astype(BF16)
        hid_t_ref[...] = hid.T.astype(BF16)

    tm = min(HIDDEN_TM, t)
    pre = pl.BlockSpec((tm, FFN_HB), lambda i, k: (i, k))
    return _call(body, name=name, args=[u, w_in_t.reshape(2, f, d)],
                 out_shape=(SDS((t, f), BF16), SDS((t, f), BF16), SDS((t, f), BF16), SDS((f, t), BF16)),
                 grid=(t // tm, f // FFN_HB),
                 in_specs=[pl.BlockSpec((tm, d), lambda i, k: (i, 0)),
                           pl.BlockSpec((2, FFN_HB, d), lambda i, k: (0, k, 0))],
                 out_specs=(pre, pre, pre, pl.BlockSpec((FFN_HB, tm), lambda i, k: (k, i))), job=job)


def _ffn_bwd(dfb, gpre, upre, w_in_t, w_out, name, job=None):
    t, d = dfb.shape
    f = w_out.shape[0]
    tm, hb = min(BWD_TM, t), FFN_HB
    nk = f // hb

    def body(df_ref, g_ref, up_ref, w_ref, wo_ref, du_ref, da_t_ref):
        k = pl.program_id(1)
        acc = du_ref

        @pl.when(k == 0)
        def _():
            acc[...] = jnp.zeros_like(acc)

        dhid = lax.dot_general(df_ref[...], wo_ref[...], NT, preferred_element_type=F32)
        g, up = g_ref[...].astype(F32), up_ref[...].astype(F32)
        sig = _sigmoid(g)
        silu = g * sig
        dup = dhid * silu
        dg = dhid * up * (sig * (1.0 + g * (1.0 - sig)))
        da_t_ref[0] = dg.T.astype(BF16)
        da_t_ref[1] = dup.T.astype(BF16)
        acc[...] += (jnp.dot(dg.astype(BF16), w_ref[0], preferred_element_type=F32)
                     + jnp.dot(dup.astype(BF16), w_ref[1], preferred_element_type=F32))

    tok = pl.BlockSpec((tm, d), lambda i, k: (i, 0))
    pre = pl.BlockSpec((tm, hb), lambda i, k: (i, k))
    return _call(body, name=name, args=[dfb, gpre, upre, w_in_t.reshape(2, f, d), w_out],
                 out_shape=(SDS((t, d), F32), SDS((2, f, t), BF16)), grid=(t // tm, nk),
                 in_specs=[tok, pre, pre, pl.BlockSpec((2, hb, d), lambda i, k: (0, k, 0)),
                           pl.BlockSpec((hb, d), lambda i, k: (k, 0))],
                 out_specs=(tok, pl.BlockSpec((2, hb, tm), lambda i, k: (0, k, i))), job=job)


CH = LANES
PAD = SUBLANES


def _lru_gates(xc, gw_ref, gb_ref, lam_ref, z):
    xcb = xc.astype(BF16)
    r = _sigmoid(jnp.dot(xcb, gw_ref[2 * z], preferred_element_type=F32) + gb_ref[pl.ds(2 * z, 1), :])
    i = _sigmoid(jnp.dot(xcb, gw_ref[2 * z + 1], preferred_element_type=F32) + gb_ref[pl.ds(2 * z + 1, 1), :])
    sp = _softplus(-lam_ref[pl.ds(z, 1), :])
    log_a = (-RG_C * r) * sp
    a = jnp.exp(log_a)
    mult = jnp.sqrt(-_expm1(2.0 * log_a))
    return r, i, sp, a, mult


def _conv(xpad, cw_ref, cb_ref, t):
    xc = cb_ref[...] + cw_ref[pl.ds(0, 1), :] * xpad[pl.ds(PAD - 2, t), :]
    for j in range(1, CONV_WIDTH):
        xc = xc + cw_ref[pl.ds(j, 1), :] * xpad[pl.ds(PAD - 2 + j, t), :]
    return xc


def _fill_padded(pad_ref, value, t):
    pad_ref[pl.ds(0, PAD), :] = jnp.zeros((PAD, CH), F32)
    pad_ref[pl.ds(PAD + t, PAD), :] = jnp.zeros((PAD, CH), F32)
    pad_ref[pl.ds(PAD, t), :] = value


def _scan_pair(t, a_up, b_up, out_up, a_down, b_down, out_down):
    row = lax.broadcasted_iota(jnp.int32, (SUBLANES, CH), 0)

    def compose(a, b, rising):
        for dist in (1, 2, 4):
            shift = dist if rising else SUBLANES - dist
            keep = (row >= dist) if rising else (row < SUBLANES - dist)
            b = jnp.where(keep, b + a * pltpu.roll(b, shift, axis=0), b)
            a = jnp.where(keep, a * pltpu.roll(a, shift, axis=0), a)
        return a, b

    def step(tt, carry):
        hu, hd = carry
        lo = pl.ds(pl.multiple_of(tt * SUBLANES, SUBLANES), SUBLANES)
        hi = pl.ds(pl.multiple_of(t - SUBLANES - tt * SUBLANES, SUBLANES), SUBLANES)
        a, b = compose(a_up[lo, :], b_up[lo, :], True)
        up = b + a * hu
        out_up[lo, :] = up
        a, b = compose(a_down[hi, :], b_down[hi, :], False)
        down = b + a * hd
        out_down[hi, :] = down
        return up[SUBLANES - 1:, :], down[:1, :]

    zero = jnp.zeros((1, CH), F32)
    lax.fori_loop(0, t // SUBLANES, step, (zero, zero), unroll=2)


def _lru_fwd(proj, cw, cb, gw, gb, lam, name, job=None):
    t = proj.shape[0]
    c = cw.shape[1]
    ncb = c // CH

    def body(x_ref, g_ref, cw_ref, cb_ref, gw_ref, gb_ref, lam_ref, ya_ref, hf_ref, hb_ref, xpad, a0, b0, a1, b1):
        _fill_padded(xpad, x_ref[...], t)
        xc = _conv(xpad, cw_ref, cb_ref, t)
        for z, (a_s, b_s) in enumerate(((a0, b0), (a1, b1))):
            _, i, _, a, mult = _lru_gates(xc, gw_ref, gb_ref, lam_ref, z)
            a_s[...] = a
            b_s[...] = mult * (i * xc)
        _scan_pair(t, a0, b0, hf_ref, a1, b1, hb_ref)
        gelu, _ = _gelu_parts(g_ref[...])
        ya_ref[...] = gelu * (hf_ref[...] + hb_ref[...])

    col = lambda off: pl.BlockSpec((t, CH), lambda i: (0, off + i))
    small = lambda rows: pl.BlockSpec((rows, CH), lambda i: (0, i))
    return _call(body, name=name, args=[proj, proj, cw, cb, gw, gb, lam], out_shape=(SDS((t, c), F32),) * 3,
                 grid=(ncb,),
                 in_specs=[col(0), col(ncb), small(CONV_WIDTH), small(1),
                           pl.BlockSpec((4, None, CH, CH), lambda i: (0, i, 0, 0)), small(4), small(2)],
                 out_specs=(col(0),) * 3,
                 scratch_shapes=[pltpu.VMEM((t + 2 * PAD, CH), F32)] + [pltpu.VMEM((t, CH), F32)] * 4, job=job)


def _lru_bwd(proj, cw, cb, gw, gb, lam, hf, hb, dya, name, job=None):
    t = proj.shape[0]
    c = cw.shape[1]
    ncb = c // CH

    def body(x_ref, g_ref, cw_ref, cb_ref, gw_ref, gb_ref, lam_ref, hf_ref, hb_ref, dya_ref,
             dx_ref, dg_ref, dt_ref, dcw_ref, dcb_ref, dgw_ref, dgb_ref, dlam_ref,
             xpad, hpad, dxc, a0, a1, dhs, dh0, dh1):
        _fill_padded(xpad, x_ref[...], t)
        xc = _conv(xpad, cw_ref, cb_ref, t)
        xcb = xc.astype(BF16)
        gates = [_lru_gates(xc, gw_ref, gb_ref, lam_ref, z) for z in range(2)]

        gelu, dgelu = _gelu_parts(g_ref[...])
        dya = dya_ref[...]
        dgate = dya * (hf_ref[...] + hb_ref[...]) * dgelu
        dg_ref[...] = dgate.astype(BF16)
        dt_ref[1] = dgate.T.astype(BF16)
        dhs[...] = dya * gelu

        _fill_padded(hpad, gates[0][3], t)
        a0[...] = hpad[pl.ds(PAD + 1, t), :]
        _fill_padded(hpad, gates[1][3], t)
        a1[...] = hpad[pl.ds(PAD - 1, t), :]
        _scan_pair(t, a1, dhs, dh1, a0, dhs, dh0)

        acc_dxc = jnp.zeros((t, CH), F32)
        for z, (h_ref, dh_ref, shift) in enumerate(((hf_ref, dh0, -1), (hb_ref, dh1, 1))):
            r, i, sp, a, mult = gates[z]
            _fill_padded(hpad, h_ref[...], t)
            h_nb = hpad[pl.ds(PAD + shift, t), :]
            db = dh_ref[...]
            da = db * h_nb
            d_i = db * mult * xc
            acc_dxc = acc_dxc + db * mult * i
            d_mult = db * i * xc
            d_la = da * a - d_mult * (a * a) / mult
            d_r = d_la * (-RG_C * sp)
            dlam_ref[pl.ds(z, 1), :] = (jnp.sum(d_la * (-RG_C * r), axis=0, keepdims=True)
                                        * (-_sigmoid(-lam_ref[pl.ds(z, 1), :])))
            for gate, d_pre in ((0, d_r * r * (1.0 - r)), (1, d_i * i * (1.0 - i))):
                zg = 2 * z + gate
                dgb_ref[pl.ds(zg, 1), :] = jnp.sum(d_pre, axis=0, keepdims=True)
                d_pre_b = d_pre.astype(BF16)
                dgw_ref[zg] = lax.dot_general(xcb, d_pre_b, TN, preferred_element_type=F32)
                acc_dxc = acc_dxc + lax.dot_general(d_pre_b, gw_ref[zg], NT, preferred_element_type=F32)

        dcb_ref[...] = jnp.sum(acc_dxc, axis=0, keepdims=True)
        for j in range(CONV_WIDTH):
            dcw_ref[pl.ds(j, 1), :] = jnp.sum(acc_dxc * xpad[pl.ds(PAD - 2 + j, t), :], axis=0, keepdims=True)
        _fill_padded(dxc, acc_dxc, t)
        dx = cw_ref[pl.ds(0, 1), :] * dxc[pl.ds(PAD + 2, t), :]
        for j in range(1, CONV_WIDTH):
            dx = dx + cw_ref[pl.ds(j, 1), :] * dxc[pl.ds(PAD + 2 - j, t), :]
        dx_ref[...] = dx.astype(BF16)
        dt_ref[0] = dx.T.astype(BF16)

    col = lambda off: pl.BlockSpec((t, CH), lambda i: (0, off + i))
    small = lambda rows: pl.BlockSpec((rows, CH), lambda i: (0, i))
    dense = pl.BlockSpec((4, None, CH, CH), lambda i: (0, i, 0, 0))
    padded = pltpu.VMEM((t + 2 * PAD, CH), F32)
    return _call(
        body, name=name, args=[proj, proj, cw, cb, gw, gb, lam, hf, hb, dya],
        out_shape=(SDS((t, c), BF16), SDS((t, c), BF16), SDS((2, c, t), BF16), SDS((CONV_WIDTH, c), F32),
                   SDS((1, c), F32), SDS((4, ncb, CH, CH), F32), SDS((4, c), F32), SDS((2, c), F32)),
        grid=(ncb,),
        in_specs=[col(0), col(ncb), small(CONV_WIDTH), small(1), dense, small(4), small(2), col(0), col(0), col(0)],
        out_specs=(col(0), col(0), pl.BlockSpec((2, CH, t), lambda i: (0, i, 0)), small(CONV_WIDTH), small(1),
                   dense, small(4), small(2)),
        scratch_shapes=[padded, padded, padded] + [pltpu.VMEM((t, CH), F32)] * 5, job=job)


Q_ROWS = 4
BAND_ROWS = WIN_ROWS + Q_ROWS
BAND_PAIRS = BAND_ROWS // 2
Q_BLOCK = Q_ROWS * GRID_W
BAND = BAND_ROWS * GRID_W
PAIR_W = 2 * GRID_W
N_BOTH = 2 * WIN_ROWS - 2
ENTRY_LEFT_OUT, ENTRY_RIGHT_OUT, ENTRY_OUT = N_BOTH, N_BOTH + 1, N_BOTH + 2
N_ENTRIES = N_BOTH + 3


def _bias_tables(rpb):
    cols = np.arange(GRID_W)
    start = np.clip(cols - WIN_COLS // 2, 0, GRID_W - WIN_COLS)
    valid = (cols[None, :] >= start[:, None]) & (cols[None, :] < start[:, None] + WIN_COLS)
    col_off = np.clip(cols[None, :] - cols[:, None] + WIN_COLS - 1, 0, 2 * WIN_COLS - 2)
    pick_col = jnp.asarray(np.eye(2 * WIN_COLS - 1, dtype=np.float32)[col_off] * valid[..., None])
    by_row = jnp.einsum("hrc,qkc->hrqk", rpb, pick_col, precision=lax.Precision.HIGHEST)
    by_row = jnp.where(jnp.asarray(valid)[None, None], by_row, NEG)
    out = jnp.full_like(by_row[:, :1], NEG)
    first_in, last_in = WIN_ROWS - 1 - WIN_ROWS // 2, 2 * (WIN_ROWS - 1) - WIN_ROWS // 2
    both = jnp.concatenate([by_row[:, :-1], by_row[:, 1:]], axis=-1)
    left_out = jnp.concatenate([out, by_row[:, first_in:first_in + 1]], axis=-1)
    right_out = jnp.concatenate([by_row[:, last_in:last_in + 1], out], axis=-1)
    return jnp.concatenate([both, left_out, right_out, jnp.concatenate([out, out], axis=-1)], axis=1)


def _band_start(m, rows):
    return jnp.clip(Q_ROWS * m - WIN_ROWS // 2, 0, rows - BAND_ROWS)


def _entry(r, key_row, rows):
    w0 = jnp.clip(r - WIN_ROWS // 2, 0, rows - WIN_ROWS)
    left = (key_row >= w0) & (key_row < w0 + WIN_ROWS)
    right = (key_row + 1 >= w0) & (key_row + 1 < w0 + WIN_ROWS)
    return jnp.where(left & right, key_row - r + WIN_ROWS - 1,
                     jnp.where(right, ENTRY_LEFT_OUT, jnp.where(left, ENTRY_RIGHT_OUT, ENTRY_OUT)))


def _transposed_pairs(dst, src_ref):
    for g in range(dst.shape[0]):
        dst[g] = src_ref[pl.ds(g * PAIR_W, PAIR_W), :].T.astype(BF16)


def _band_of(pairs_ref, first_pair, hh):
    heads = pl.ds(hh * HEAD_DIM, HEAD_DIM)
    return jnp.concatenate([pairs_ref[first_pair + g, heads, :] for g in range(BAND_PAIRS)], axis=1)


def _attn_block(qs, kt, tz_ref, hh, m, rows):
    rs = _band_start(m, rows)
    lanes = pl.ds(hh * HEAD_DIM, HEAD_DIM)
    qrows = pl.ds(pl.multiple_of(m * Q_BLOCK, Q_BLOCK), Q_BLOCK)
    band = pl.ds(pl.multiple_of(rs * GRID_W, PAIR_W), BAND)
    entries = [[_entry(Q_ROWS * m + i, rs + 2 * g, rows) for g in range(BAND_PAIRS)] for i in range(Q_ROWS)]
    bias = jnp.concatenate([jnp.concatenate([tz_ref[hh, e] for e in row], axis=1) for row in entries], axis=0)
    q = qs[qrows, lanes]
    s = jnp.dot(q, _band_of(kt, rs // 2, hh), preferred_element_type=F32) * (HEAD_DIM ** -0.5) + bias
    p = jnp.exp(s - jnp.max(s, axis=-1, keepdims=True))
    p = p / jnp.sum(p, axis=-1, keepdims=True)
    return q, p, qrows, band, lanes, entries, rs // 2


def _attn_fwd(proj, tables, width, name, job=None):
    t = proj.shape[0]
    rows = t // GRID_W
    npair = width // LANES
    first = (proj.shape[1] - 3 * width) // LANES

    def body(q_ref, k_ref, v_ref, tz_ref, o_ref, qs, vs, kt):
        qs[...] = q_ref[...].astype(BF16)
        vs[...] = v_ref[...].astype(BF16)
        _transposed_pairs(kt, k_ref)

        def block(m, carry):
            for hh in range(2):
                _, p, qrows, band, lanes, _, _ = _attn_block(qs, kt, tz_ref, hh, m, rows)
                o_ref[qrows, lanes] = jnp.dot(p.astype(BF16), vs[band, lanes], preferred_element_type=F32)
            return carry

        lax.fori_loop(0, rows // Q_ROWS, block, 0, unroll=2)

    col = lambda off: pl.BlockSpec((t, LANES), lambda i: (0, off + i))
    return _call(body, name=name, args=[proj, proj, proj, tables], out_shape=SDS((t, width), F32), grid=(npair,),
                 in_specs=[col(first), col(first + npair), col(first + 2 * npair),
                           pl.BlockSpec((2, N_ENTRIES, GRID_W, PAIR_W), lambda i: (i, 0, 0, 0))],
                 out_specs=col(0),
                 scratch_shapes=[pltpu.VMEM((t, LANES), BF16)] * 2 + [pltpu.VMEM((t // PAIR_W, LANES, PAIR_W), BF16)],
                 job=job)


def _attn_bwd(proj, tables, dyb, name, job=None):
    t, width = dyb.shape
    rows = t // GRID_W
    npair = width // LANES
    first = (proj.shape[1] - 3 * width) // LANES

    def body(q_ref, k_ref, v_ref, tz_ref, do_ref, dq_ref, dk_ref, dv_ref, dt_ref, dtz_ref, dq_s, dk_s, dv_s,
             qs, ks, vs, dos, kt, vt):
        qs[...] = q_ref[...].astype(BF16)
        ks[...] = k_ref[...].astype(BF16)
        vs[...] = v_ref[...].astype(BF16)
        dos[...] = do_ref[...].astype(BF16)
        _transposed_pairs(kt, k_ref)
        _transposed_pairs(vt, v_ref)
        dk_s[...] = jnp.zeros_like(dk_s)
        dv_s[...] = jnp.zeros_like(dv_s)
        dtz_ref[...] = jnp.zeros_like(dtz_ref)

        def block(m, carry):
            for hh in range(2):
                q, p, qrows, band, lanes, entries, first_pair = _attn_block(qs, kt, tz_ref, hh, m, rows)
                do = dos[qrows, lanes]
                dp = jnp.dot(do, _band_of(vt, first_pair, hh), preferred_element_type=F32)
                ds = p * (dp - jnp.sum(dp * p, axis=-1, keepdims=True))
                for i, row in enumerate(entries):
                    for g, e in enumerate(row):
                        dtz_ref[hh, e] += ds[i * GRID_W:(i + 1) * GRID_W, g * PAIR_W:(g + 1) * PAIR_W]
                dsb = (ds * (HEAD_DIM ** -0.5)).astype(BF16)
                dq_s[qrows, lanes] = jnp.dot(dsb, ks[band, lanes], preferred_element_type=F32)
                dk_s[band, lanes] += lax.dot_general(dsb, q, TN, preferred_element_type=F32)
                dv_s[band, lanes] += lax.dot_general(p.astype(BF16), do, TN, preferred_element_type=F32)
            return carry

        lax.fori_loop(0, rows // Q_ROWS, block, 0)
        for n, (src, dst) in enumerate(((dq_s, dq_ref), (dk_s, dk_ref), (dv_s, dv_ref))):
            val = src[...]
            dst[...] = val.astype(BF16)
            dt_ref[n] = val.T.astype(BF16)

    col = lambda off: pl.BlockSpec((t, LANES), lambda i: (0, off + i))
    table = pl.BlockSpec((2, N_ENTRIES, GRID_W, PAIR_W), lambda i: (i, 0, 0, 0))
    pairs = pltpu.VMEM((t // PAIR_W, LANES, PAIR_W), BF16)
    return _call(body, name=name, args=[proj, proj, proj, tables, dyb],
                 out_shape=(SDS((t, width), BF16),) * 3 + (SDS((3, width, t), BF16), SDS(tables.shape, F32)),
                 grid=(npair,),
                 in_specs=[col(first), col(first + npair), col(first + 2 * npair), table, col(0)],
                 out_specs=(col(0), col(0), col(0), pl.BlockSpec((3, LANES, t), lambda i: (0, i, 0)), table),
                 scratch_shapes=[pltpu.VMEM((t, LANES), F32)] * 3 + [pltpu.VMEM((t, LANES), BF16)] * 4 + [pairs, pairs],
                 job=job)


def _adamw_math(w, g, m, v):
    m = ADAM_B1 * m + (1.0 - ADAM_B1) * g
    v = ADAM_B2 * v + (1.0 - ADAM_B2) * (g * g)
    m_hat = m / (1.0 - ADAM_B1 ** ADAM_STEP)
    v_hat = v / (1.0 - ADAM_B2 ** ADAM_STEP)
    delta = -ADAM_LR * (m_hat / (jnp.sqrt(v_hat) + ADAM_EPS) + ADAM_WD * w)
    return delta, m, v


def _sum_partials(p_ref):
    g = p_ref[0].astype(F32)
    for s in range(1, N_CHIP):
        g = g + p_ref[s].astype(F32)
    return g


def _adamw_rows(w, partials, m, v, name):
    rb, n = w.shape
    tr = 64

    def body(w_ref, p_ref, m_ref, v_ref, g_ref, d_ref, nm_ref, nv_ref):
        g = _sum_partials(p_ref)
        g_ref[...] = g
        d_ref[...], nm_ref[...], nv_ref[...] = _adamw_math(w_ref[...], g, m_ref[...], v_ref[...])

    blk = pl.BlockSpec((tr, n), lambda i: (i, 0))
    return _call(body, name=name, args=[w, partials.reshape(N_CHIP, rb, n), m, v],
                 out_shape=(SDS((rb, n), F32),) * 4, grid=(rb // tr,),
                 in_specs=[blk, pl.BlockSpec((N_CHIP, tr, n), lambda i: (0, i, 0)), blk, blk], out_specs=(blk,) * 4)


def _adamw_cols(w, partials, m, v, name):
    d, nb = w.shape
    td = 256
    parts = list(partials) if isinstance(partials, (list, tuple)) else [partials]
    heights = [p.shape[0] // N_CHIP for p in parts]

    def body(w_ref, m_ref, v_ref, *rest):
        p_refs, (g_ref, d_ref, nm_ref, nv_ref) = rest[:len(parts)], rest[len(parts):]
        g = jnp.concatenate([_sum_partials(p_ref) for p_ref in p_refs], axis=0).T
        g_ref[...] = g
        d_ref[...], nm_ref[...], nv_ref[...] = _adamw_math(w_ref[...], g, m_ref[...], v_ref[...])

    blk = pl.BlockSpec((td, nb), lambda i: (i, 0))
    return _call(body, name=name, args=[w, m, v, *[p.reshape(N_CHIP, h, d) for p, h in zip(parts, heights)]],
                 out_shape=(SDS((d, nb), F32),) * 4, grid=(d // td,),
                 in_specs=[blk, blk, blk] + [pl.BlockSpec((N_CHIP, h, td), lambda i: (0, 0, i)) for h in heights],
                 out_specs=(blk,) * 4)


def _adamw_small(w, g, m, v, name):
    def body(w_ref, g_ref, m_ref, v_ref, d_ref, nm_ref, nv_ref):
        d_ref[...], nm_ref[...], nv_ref[...] = _adamw_math(w_ref[...], g_ref[...], m_ref[...], v_ref[...])

    return _call(body, name=name, args=[w, g, m, v], out_shape=(SDS(w.shape, F32),) * 3, in_specs=[WHOLE] * 4,
                 out_specs=(WHOLE,) * 3)


TILE = SUBLANES * LANES


def _pack(arrays):
    parts = []
    for a in arrays:
        flat = a.reshape(-1).astype(F32)
        flat = jnp.pad(flat, (0, -flat.size % TILE))
        parts.append(flat.reshape(-1, LANES))
    return jnp.concatenate(parts, axis=0)


def _unpack(pack, like):
    out, row = [], 0
    for a in like:
        n = int(np.prod(a.shape))
        nrows = -(-n // TILE) * SUBLANES
        out.append(pack[row:row + nrows].reshape(-1)[:n].reshape(a.shape))
        row += nrows
    return out


def _dense_gate_blocks(gate_w):
    w = gate_w.reshape(4, -1, 2, HEAD_DIM, HEAD_DIM)
    zero = jnp.zeros_like(w[:, :, 0])
    top = jnp.concatenate([w[:, :, 0], zero], axis=-1)
    bottom = jnp.concatenate([zero, w[:, :, 1]], axis=-1)
    return jnp.concatenate([top, bottom], axis=-2)


def _diag_gate_blocks(dense, shape):
    even = dense[:, :, :HEAD_DIM, :HEAD_DIM]
    odd = dense[:, :, HEAD_DIM:, HEAD_DIM:]
    return jnp.stack([even, odd], axis=2).reshape(shape)


LARGE = ("ffn1_w_in", "ffn1_w_out", "w_in_mix", "w_out_mix", "ffn2_w_in", "ffn2_w_out")
COLUMN_SHARDED = ("ffn1_w_in", "w_in_mix", "ffn2_w_in")
SHARDED_SMALL = ("lru_conv_w", "lru_lambda")
REPLICATED = ("norm_ffn1", "norm_mix", "lru_conv_b", "lru_gate_w", "lru_gate_b", "attn_rpb", "lru_out_norm",
              "attn_out_norm", "norm_ffn2", "norm_final")
SMALL_ORDER = REPLICATED + SHARDED_SMALL
WEIGHTS = ("norm_ffn1", "ffn1_w_in", "ffn1_w_out", "norm_mix", "w_in_mix", "lru_conv_w", "lru_conv_b", "lru_gate_w",
           "lru_gate_b", "lru_lambda", "attn_rpb", "lru_out_norm", "attn_out_norm", "w_out_mix", "norm_ffn2",
           "ffn2_w_in", "ffn2_w_out", "norm_final")


PARTS = {("gather", "w_in_mix"): 4, ("gather", "ffn2_w_in"): 8}
CARRIES = {
    "gather_ffn1_in": [(("gather", "ffn1_w_in"), 1), (("gather", "small"), 1)],
    "ffn1_hidden": [(("gather", "ffn1_w_out"), 1), (("gather", "w_in_mix"), 1)],
    "ffn1_out": [(("gather", "w_in_mix"), 3)],
    "mix_in_proj": [(("gather", "w_out_mix"), 1), (("gather", "ffn2_w_in"), 1)],
    "lru_fwd": [(("gather", "ffn2_w_in"), 3)],
    "attn_fwd": [(("gather", "ffn2_w_in"), 3)],
    "mix_out_proj": [(("gather", "ffn2_w_in"), 1)],
    "ffn2_hidden": [(("gather", "ffn2_w_out"), 1)],
    "ffn1_bwd": [(("gather", "small_grads"), 1)],
    "gather_late_grads": [(("gather", "late_grads"), 1)],
}


class _Transfer:
    def __init__(self, kind, src, dest, block_rows, parts):
        self.kind, self.src, self.dest = kind, src, dest
        self.ranges, self.taken = _split(block_rows, parts), 0

    def take(self, count):
        lo, hi = self.ranges[self.taken][0], self.ranges[self.taken + count - 1][1]
        self.taken += count
        return _Piece(self.kind, self.src, self.dest, lo, hi)


class _Traffic:
    def __init__(self):
        self.transfers = {}

    def open(self, kind, name, src, placed=None):
        dest = _gathered(src) if placed is None else placed
        self.transfers[kind, name] = _Transfer(kind, src, dest, dest.shape[0] // N_DEV, PARTS.get((kind, name), 1))

    def _job(self, host):
        moved = [self.transfers[key] for key, _ in CARRIES[host]]
        return moved, _Job([tr.take(count) for tr, (_, count) in zip(moved, CARRIES[host])])

    def carry(self, host, fn, *args, **kw):
        if host not in CARRIES:
            return fn(*args, name=host, **kw)
        moved, job = self._job(host)
        res, landed = fn(*args, name=host, job=job, **kw)
        for tr, arr in zip(moved, landed):
            tr.dest = arr
        return res

    def alone(self, host):
        moved, job = self._job(host)
        for tr, arr in zip(moved, _run_job(job, host)):
            tr.dest = arr

    def result(self, kind, name):
        tr = self.transfers.pop((kind, name))
        assert tr.taken == len(tr.ranges), (kind, name)
        return tr.dest


def _forward_backward(x, target, shards, sharded_small, s):
    c = s["lru_conv_b"].shape[1]
    width = s["attn_out_norm"].shape[1]
    t = x.shape[0]
    traffic = _Traffic()
    carry = traffic.carry
    weight = lambda n: traffic.result("gather", n)

    for n in LARGE:
        traffic.open("gather", n, None, placed=shards[n])
    traffic.open("gather", "small", sharded_small)
    traffic.alone("gather_ffn1_in")
    full_small = weight("small").reshape(N_DEV, SUBLANES, c // N_DEV)
    conv_w = full_small[:, :CONV_WIDTH].transpose(1, 0, 2).reshape(CONV_WIDTH, c)
    lam = full_small[:, CONV_WIDTH:CONV_WIDTH + 2].transpose(1, 0, 2).reshape(2, c)
    w = {"ffn1_w_in": weight("ffn1_w_in")}
    ffn_out = dict(nt=False, out_dtype=F32, tm=1024, tn=512, scale=0.5)
    u1 = _rmsnorm_fwd(x, s["norm_ffn1"], "norm_ffn1")
    g1, up1, hid1, hid1_t = carry("ffn1_hidden", _ffn_hidden, u1, w["ffn1_w_in"])
    w["ffn1_w_out"] = weight("ffn1_w_out")
    h1 = carry("ffn1_out", _mm, hid1, w["ffn1_w_out"], residual=x, **ffn_out)
    w["w_in_mix"] = weight("w_in_mix")
    u2 = _rmsnorm_fwd(h1, s["norm_mix"], "norm_mix")
    proj = carry("mix_in_proj", _mm, u2, w["w_in_mix"], nt=True, out_dtype=F32, tm=1024, tn=512)
    w["w_out_mix"] = weight("w_out_mix")
    gw = _dense_gate_blocks(s["lru_gate_w"]).astype(BF16)
    gb = s["lru_gate_b"].reshape(4, c)
    tables, tables_vjp = jax.vjp(_bias_tables, s["attn_rpb"])
    ya, hf, hb = carry("lru_fwd", _lru_fwd, proj, conv_w, s["lru_conv_b"], gw, gb, lam)
    yb = carry("attn_fwd", _attn_fwd, proj, tables, width)
    y, yt = _mixnorm_fwd(ya, yb, s["lru_out_norm"], s["attn_out_norm"], "mix_norm")
    h2 = carry("mix_out_proj", _mm, y, w["w_out_mix"], nt=False, out_dtype=F32, tm=1024, tn=512, residual=h1)
    u3 = _rmsnorm_fwd(h2, s["norm_ffn2"], "norm_ffn2")
    w["ffn2_w_in"] = weight("ffn2_w_in")
    g2, up2, hid2, hid2_t = carry("ffn2_hidden", _ffn_hidden, u3, w["ffn2_w_in"])
    w["ffn2_w_out"] = weight("ffn2_w_out")
    h3 = carry("ffn2_out", _mm, hid2, w["ffn2_w_out"], residual=h2, **ffn_out)
    dh3, df2, loss_part, d_norm_final = _final_loss(h3, s["norm_final"], target, "final_loss")

    grads = {}
    grad_of = dict(nt=False, out_dtype=BF16, tm=512, tn=2048)

    to_sibling, to_chips = {}, {}

    def reduce_in_chip(n):
        land = _blank_like(grads[n], grads[n].shape[0] // 2, "landing_" + n)
        to_sibling[n], token = _split_start("to_sibling", grads[n], land, "to_sibling_" + n)
        RUN_AFTER.append(token)

    def reduce_over_chips(n, after):
        own, got = _split_wait(to_sibling.pop(n), [after], "from_sibling_" + n)
        summed = _pair_sum(own, got, "pair_sum_" + n)
        to_chips[n], token = _split_start("to_chips", summed, _own_slot(summed, "own_slot_" + n), "to_chips_" + n)
        RUN_AFTER.append(token)
        return token

    f = hid2_t.shape[0]
    grads["ffn2_w_out"] = carry("ffn2_out_grad", _mm, hid2_t, df2, **grad_of)
    reduce_in_chip("ffn2_w_out")
    du3, da2_t = carry("ffn2_bwd", _ffn_bwd, df2, g2, up2, w["ffn2_w_in"], w["ffn2_w_out"])
    reduce_over_chips("ffn2_w_out", du3)
    grads["ffn2_w_in"] = carry("ffn2_in_grad", _mm, da2_t.reshape(2 * f, t), u3, **grad_of)
    reduce_in_chip("ffn2_w_in")
    dh2, dh2b, d_norm_ffn2 = carry("norm_ffn2_bwd", _rmsnorm_bwd, du3, h2, s["norm_ffn2"], dh3, 1.0)
    grads["w_out_mix"] = carry("mix_out_grad", _mm, yt, dh2b, **grad_of)
    reduce_over_chips("ffn2_w_in", grads["w_out_mix"])
    reduce_in_chip("w_out_mix")
    dy = carry("mix_out_bwd", _mm, dh2b, w["w_out_mix"], nt=True, out_dtype=F32, tm=1024, tn=512)
    dya, dyb, d_lru_out_norm, d_attn_out_norm = _mixnorm_bwd(dy, ya, yb, s["lru_out_norm"], s["attn_out_norm"],
                                                             "mix_norm_bwd")
    dq, dk, dv, dqkv_t, d_tables = carry("attn_bwd", _attn_bwd, proj, tables, dyb)
    reduce_over_chips("w_out_mix", dq)
    dx_lru, dg_lru, dxg_t, d_conv_w, d_conv_b, d_gw, d_gb, d_lam = carry(
        "lru_bwd", _lru_bwd, proj, conv_w, s["lru_conv_b"], gw, gb, lam, hf, hb, dya)
    rows_of = 2 * c + 3 * width
    lru_rows = carry("mix_in_grad_lru", _mm, dxg_t.reshape(2 * c, t), u2, out_rows=rows_of, **grad_of)
    grads["w_in_mix"] = carry("mix_in_grad_attn", _mm, dqkv_t.reshape(3 * width, t), u2, out_rows=rows_of,
                              row_offset=2 * c, into=lru_rows, **grad_of)
    reduce_in_chip("w_in_mix")
    du2 = carry("mix_in_bwd", _mm, [dx_lru, dg_lru, dq, dk, dv], w["w_in_mix"], nt=False, out_dtype=F32, tm=1024,
                tn=512)
    dh1, df1, d_norm_mix = carry("norm_mix_bwd", _rmsnorm_bwd, du2, h1, s["norm_mix"], dh2, 0.5)
    reduce_over_chips("w_in_mix", dh1)

    by_device = lambda a: a.reshape(a.shape[0], N_DEV, -1).transpose(1, 0, 2)
    small = {
        "norm_mix": d_norm_mix, "lru_conv_b": d_conv_b, "lru_gate_w": _diag_gate_blocks(d_gw, s["lru_gate_w"].shape),
        "lru_gate_b": d_gb.reshape(s["lru_gate_b"].shape), "attn_rpb": tables_vjp(d_tables)[0],
        "lru_out_norm": d_lru_out_norm, "attn_out_norm": d_attn_out_norm, "norm_ffn2": d_norm_ffn2,
        "norm_final": d_norm_final, "lru_conv_w": by_device(d_conv_w), "lru_lambda": by_device(d_lam),
    }
    early = [small[n] for n in SMALL_ORDER[1:]]
    traffic.open("gather", "small_grads", _pack(early))

    grads["ffn1_w_out"] = carry("ffn1_out_grad", _mm, hid1_t, df1, **grad_of)
    reduce_in_chip("ffn1_w_out")
    du1, da1_t = carry("ffn1_bwd", _ffn_bwd, df1, g1, up1, w["ffn1_w_in"], w["ffn1_w_out"])
    grad_x, _, d_norm_ffn1 = carry("norm_ffn1_bwd", _rmsnorm_bwd, du1, x, s["norm_ffn1"], dh1, 1.0)
    traffic.open("gather", "late_grads", _pack([d_norm_ffn1]))
    traffic.alone("gather_late_grads")
    late = traffic.result("gather", "late_grads")
    reduce_over_chips("ffn1_w_out", late)
    half = 2 * f // N_DEV // 2
    half_rows = lambda h: (half, N_DEV, lambda i: 2 * i + h)
    grads["ffn1_w_in_a"] = carry("ffn1_in_grad_a", _mm, da1_t.reshape(2 * f, t), u1, take=half_rows(0), **grad_of)
    reduce_in_chip("ffn1_w_in_a")
    grads["ffn1_w_in_b"] = carry("ffn1_in_grad_b", _mm, da1_t.reshape(2 * f, t), u1, take=half_rows(1), **grad_of)
    reduce_over_chips("ffn1_w_in_a", grads["ffn1_w_in_b"])
    reduce_in_chip("ffn1_w_in_b")
    reduced = (_unpack(_sum_devices(late, "sum_late_grads"), [d_norm_ffn1])
               + _unpack(_sum_devices(traffic.result("gather", "small_grads"), "sum_small_grads"), early))
    last_token = reduce_over_chips("ffn1_w_in_b", reduced[1])
    RUN_AFTER.clear()
    assert not traffic.transfers and not to_sibling, (list(traffic.transfers), list(to_sibling))
    return loss_part[0, 0], grad_x, to_chips, last_token, dict(zip(SMALL_ORDER, reduced))


def _step(x, loss_target, p, m, v):
    me = 4 * lax.axis_index("x") + 2 * lax.axis_index("y") + lax.axis_index("c")

    shards = {n: _cast_into_place(p[n], n in COLUMN_SHARDED, "cast_" + n) for n in LARGE}
    sharded_small = (jnp.pad(p["lru_conv_w"], ((0, SUBLANES - CONV_WIDTH), (0, 0)))
                     + jnp.pad(p["lru_lambda"], ((CONV_WIDTH, SUBLANES - CONV_WIDTH - 2), (0, 0))))
    s = {n: p[n] if n in ("lru_gate_w", "lru_gate_b", "attn_rpb") else p[n].reshape(1, -1) for n in REPLICATED}

    loss_part, grad_x, to_chips, last_token, small = _forward_backward(x, loss_target, shards, sharded_small, s)
    loss = lax.psum(loss_part, ("x", "y", "c"))

    def landed(n, after):
        return _split_wait(to_chips[n], after, "from_chips_" + n)[1]

    def update(n, partials):
        return (_adamw_cols if n in COLUMN_SHARDED else _adamw_rows)(p[n], partials, m[n], v[n], "adamw_" + n)

    out = {n: update(n, landed(n, [last_token])) for n in LARGE if n != "ffn1_w_in"}
    done = [o[3] for o in out.values()]
    out["ffn1_w_in"] = update("ffn1_w_in", [landed("ffn1_w_in_a", done), landed("ffn1_w_in_b", done)])

    g_small = {n: lax.dynamic_index_in_dim(g, me, axis=0, keepdims=False) if n in SHARDED_SMALL else g
               for n, g in small.items()}
    names = SMALL_ORDER
    like = [p[n] for n in names]
    pack_of = lambda d: _pack([d[n].reshape(p[n].shape) for n in names])
    upd = _adamw_small(pack_of(p), pack_of(g_small), pack_of(m), pack_of(v), "adamw_small")
    for n, d_, m_, v_ in zip(names, *[_unpack(u, like) for u in upd]):
        out[n] = (g_small[n].reshape(p[n].shape), d_, m_, v_)
    return loss, grad_x, out


def kernel(x, norm_ffn1, ffn1_w_in, ffn1_w_out, norm_mix, w_in_mix, lru_conv_w, lru_conv_b, lru_gate_w, lru_gate_b, lru_lambda, attn_rpb, lru_out_norm, attn_out_norm, w_out_mix, norm_ffn2, ffn2_w_in, ffn2_w_out, norm_final, loss_target, m_norm_ffn1, m_ffn1_w_in, m_ffn1_w_out, m_norm_mix, m_w_in_mix, m_lru_conv_w, m_lru_conv_b, m_lru_gate_w, m_lru_gate_b, m_lru_lambda, m_attn_rpb, m_lru_out_norm, m_attn_out_norm, m_w_out_mix, m_norm_ffn2, m_ffn2_w_in, m_ffn2_w_out, m_norm_final, v_norm_ffn1, v_ffn1_w_in, v_ffn1_w_out, v_norm_mix, v_w_in_mix, v_lru_conv_w, v_lru_conv_b, v_lru_gate_w, v_lru_gate_b, v_lru_lambda, v_attn_rpb, v_lru_out_norm, v_attn_out_norm, v_w_out_mix, v_norm_ffn2, v_ffn2_w_in, v_ffn2_w_out, v_norm_final):
    given = dict(locals())
    drop_layer = lambda n, a: a if n == "norm_final" else a[0]
    p = {n: drop_layer(n, given[n]) for n in WEIGHTS}
    m = {n: drop_layer(n, given["m_" + n]) for n in WEIGHTS}
    v = {n: drop_layer(n, given["v_" + n]) for n in WEIGHTS}
    loss, grad_x, out = _step(x[0], loss_target[0], p, m, v)
    shaped = lambda n, a: a.reshape(given[n].shape)
    return (loss, grad_x[None], *[shaped(n, out[n][k]) for k in range(4) for n in WEIGHTS])
```

```python
import math

import numpy as np
import jax
import jax.numpy as jnp
from jax import lax
from jax.experimental import pallas as pl
from jax.experimental.pallas import tpu as pltpu

F32 = jnp.float32
BF16 = jnp.bfloat16
SDS = jax.ShapeDtypeStruct

N_DEV = 8
N_CHIP = 4
NORM_EPS = 1e-6
RG_C = 8.0
CONV_WIDTH = 4
HEAD_DIM = 64
GRID_W = 64
WIN_ROWS = 8
WIN_COLS = 16
NEG = -1e30

ADAM_LR = 0.001
ADAM_B1 = 0.9
ADAM_B2 = 0.999
ADAM_EPS = 1e-08
ADAM_WD = 0.01
ADAM_STEP = 10

LANES = 128
SUBLANES = 8
VMEM_LIMIT = 56 * 1024 * 1024

NT = (((1,), (1,)), ((), ()))
TN = (((0,), (0,)), ((), ()))
ANY = pl.BlockSpec(memory_space=pl.ANY)
WHOLE = pl.BlockSpec(memory_space=pltpu.VMEM)
MESH = pl.DeviceIdType.MESH


def _sigmoid(x):
    return 1.0 / (1.0 + jnp.exp(-x))


def _gelu_parts(x):
    c = math.sqrt(2.0 / math.pi)
    t = jnp.tanh(c * (x + 0.044715 * (x * x * x)))
    gelu = 0.5 * x * (1.0 + t)
    dgelu = 0.5 * (1.0 + t) + 0.5 * x * (1.0 - t * t) * (c * (1.0 + 3.0 * 0.044715 * (x * x)))
    return gelu, dgelu


def _expm1(x):
    poly = x * (1.0 + x * (1.0 / 2) * (1.0 + x * (1.0 / 3) * (1.0 + x * (1.0 / 4) * (1.0 + x * (1.0 / 5) * (1.0 + x * (1.0 / 6))))))
    return jnp.where(jnp.abs(x) < 0.25, poly, jnp.exp(x) - 1.0)


def _softplus(x):
    return jnp.maximum(x, 0.0) + jnp.log1p(jnp.exp(-jnp.abs(x)))


class _Piece:
    N_REMOTE = {"gather": 7}
    N_LOCAL = {"gather": 1}

    def __init__(self, kind, src, dest, lo, hi):
        self.kind, self.src, self.dest, self.lo, self.hi = kind, src, dest, lo, hi


RELAY_AT = 60
RUN_AFTER = []


class _Job:
    def __init__(self, pieces):
        self.pieces = list(pieces)
        self.ins = [p.src for p in self.pieces if p.src is not None]
        self.out_shapes = [SDS(p.dest.shape, p.dest.dtype) for p in self.pieces]
        self.aliased = [i for i, p in enumerate(self.pieces) if not isinstance(p.dest, SDS)]
        self.n_remote = sum(_Piece.N_REMOTE[p.kind] for p in self.pieces)
        self.n_local = max(sum(_Piece.N_LOCAL[p.kind] for p in self.pieces), 1)

    def _each(self, step, ins, outs, send_sems, recv_sems, local_sems):
        remote = local = 0
        ins = iter(ins)
        for p, dst in zip(self.pieces, outs):
            src = None if p.src is None else next(ins)
            _EXCHANGES[p.kind](step, p, src, dst, send_sems, recv_sems, local_sems, remote, local)
            remote += _Piece.N_REMOTE[p.kind]
            local += _Piece.N_LOCAL[p.kind]

    def start(self, *refs):
        self._each("start", *refs)

    def relay(self, *refs):
        self._each("relay", *refs)

    def finish(self, *refs):
        self._each("finish", *refs)


def _call(body, *, name, args, out_shape, in_specs, out_specs, grid=(), scratch_shapes=(), aliases=None, job=None):
    single = not isinstance(out_shape, (tuple, list))
    out_shape = (out_shape,) if single else tuple(out_shape)
    out_specs = (out_specs,) if single else tuple(out_specs)
    aliases = dict(aliases or {})
    if RUN_AFTER:
        tokens, n_plain, plain_body = list(RUN_AFTER), len(args), body
        RUN_AFTER.clear()
        body = lambda *refs: plain_body(*refs[:n_plain], *refs[n_plain + len(tokens):])
        args, in_specs = list(args) + tokens, list(in_specs) + [ANY] * len(tokens)
    params = pltpu.CompilerParams(dimension_semantics=("arbitrary",) * len(grid) if grid else None,
                                  vmem_limit_bytes=VMEM_LIMIT)
    if job is None:
        res = pl.pallas_call(body, out_shape=out_shape, grid=grid, in_specs=list(in_specs), out_specs=out_specs,
                             scratch_shapes=list(scratch_shapes), input_output_aliases=aliases, name=name,
                             compiler_params=params)(*args)
        return res[0] if single else res

    n_in, n_out, n_scr = len(args), len(out_shape), len(scratch_shapes)
    j_in, j_out, j_alias = len(job.ins), len(job.out_shapes), len(job.aliased)

    def hosted(*refs):
        ins, refs = refs[:n_in], refs[n_in:]
        j_ins, refs = refs[:j_in], refs[j_in + j_alias:]
        outs, refs = refs[:n_out], refs[n_out:]
        j_outs, refs = refs[:j_out], refs[j_out:]
        scr, sems = refs[:n_scr], refs[n_scr:]
        if grid:
            step = 0
            for axis, size in enumerate(grid):
                step = step * size + pl.program_id(axis)
            steps = math.prod(grid)
            pl.when(step == 0)(lambda: job.start(j_ins, j_outs, *sems))
            body(*ins, *outs, *scr)
            pl.when(step == min(RELAY_AT * steps // 100, steps - 1))(lambda: job.relay(j_ins, j_outs, *sems))
            pl.when(step == steps - 1)(lambda: job.finish(j_ins, j_outs, *sems))
        else:
            job.start(j_ins, j_outs, *sems)
            body(*ins, *outs, *scr)
            job.relay(j_ins, j_outs, *sems)
            job.finish(j_ins, j_outs, *sems)

    res = pl.pallas_call(
        hosted, out_shape=out_shape + tuple(job.out_shapes), grid=grid,
        in_specs=list(in_specs) + [ANY] * (j_in + j_alias), out_specs=out_specs + (ANY,) * j_out,
        scratch_shapes=list(scratch_shapes) + [pltpu.SemaphoreType.DMA((job.n_remote,)),
                                               pltpu.SemaphoreType.DMA((job.n_remote,)),
                                               pltpu.SemaphoreType.DMA((job.n_local,))],
        input_output_aliases={**aliases, **{n_in + j_in + k: n_out + i for k, i in enumerate(job.aliased)}},
        name=name, compiler_params=params)(*args, *job.ins, *[job.pieces[i].dest for i in job.aliased])
    own, carried = res[:n_out], res[n_out:]
    return (own[0] if single else own), carried


def _run_job(job, name):
    return _call(lambda: None, name=name, args=[], out_shape=(), in_specs=[], out_specs=(), job=job)[1]


def _position():
    return lax.axis_index("x"), lax.axis_index("y"), lax.axis_index("c")


def _flat(px, py, pc):
    return 4 * px + 2 * py + pc


def _gather_exchange(step, p, src, dst, send_sems, recv_sems, local_sems, r0, l0):
    x, y, c = _position()
    me, sibling = (x, y, c), (x, y, 1 - c)
    along_x, along_y, diagonal = (1 - x, y), (x, 1 - y), (1 - x, 1 - y)
    south = c == 0
    passed_on = (jnp.where(south, 1 - x, x), jnp.where(south, y, 1 - y))
    passed_to = (jnp.where(south, x, 1 - x), jnp.where(south, 1 - y, y))
    placed = p.src is None
    rb, n_rows = p.dest.shape[0] // N_DEV, p.hi - p.lo

    def rows(block):
        return dst.at[pl.ds(_flat(*block) * rb + p.lo, n_rows), :]

    mine = rows(me) if placed else src.at[pl.ds(p.lo, n_rows), :]

    def copy(k, block, to, own=False):
        return pltpu.make_async_remote_copy(
            src_ref=mine if own else rows(block), dst_ref=rows(block),
            send_sem=send_sems.at[r0 + k], recv_sem=recv_sems.at[r0 + k], device_id=to, device_id_type=MESH)

    local = None if placed else pltpu.make_async_copy(mine, rows(me), local_sems.at[l0])
    if step == "start":
        if local is not None:
            local.start()
        copy(0, me, sibling, own=True).start()
        copy(1, me, (*along_x, c), own=True).start()
        copy(2, me, (*along_y, c), own=True).start()
    elif step == "relay":
        copy(1, (*along_x, c), me).wait_recv()
        copy(2, (*along_y, c), me).wait_recv()
        copy(3, (*passed_on, c), (*passed_to, c)).start()
        copy(4, (*along_x, c), sibling).start()
        copy(5, (*along_y, c), sibling).start()
    else:
        copy(3, (*diagonal, c), me).wait_recv()
        copy(6, (*diagonal, c), sibling).start()
        copy(0, sibling, me).wait_recv()
        copy(4, (*along_x, 1 - c), me).wait_recv()
        copy(5, (*along_y, 1 - c), me).wait_recv()
        copy(6, (*diagonal, 1 - c), me).wait_recv()
        copy(0, me, sibling, own=True).wait_send()
        copy(1, me, (*along_x, c), own=True).wait_send()
        copy(2, me, (*along_y, c), own=True).wait_send()
        copy(3, (*passed_on, c), (*passed_to, c)).wait_send()
        copy(4, (*along_x, c), sibling).wait_send()
        copy(5, (*along_y, c), sibling).wait_send()
        copy(6, (*diagonal, c), sibling).wait_send()
        if local is not None:
            local.wait()


CHIP_FLIPS = [(1, 0), (0, 1), (1, 1)]
_EXCHANGES = {"gather": _gather_exchange}


def _gathered(shard):
    return SDS((N_DEV * shard.shape[0], shard.shape[1]), shard.dtype)


def _split(rows, parts):
    cuts = [rows * k // parts // 16 * 16 for k in range(parts)] + [rows]
    return list(zip(cuts[:-1], cuts[1:]))


def _pair_sum(g, from_sibling, name):
    rb, n = g.shape[0] // N_DEV, g.shape[1]
    tr = rb if rb * n * 2 <= 3 * 1024 * 1024 else rb // 2
    core = lax.axis_index("c").astype(jnp.int32).reshape(1)

    def body(c_ref, g_ref, r_ref, o_ref):
        o_ref[...] = (g_ref[...].astype(F32) + r_ref[...].astype(F32)).astype(BF16)

    grid_spec = pltpu.PrefetchScalarGridSpec(
        num_scalar_prefetch=1, grid=(N_CHIP, rb // tr),
        in_specs=[pl.BlockSpec((None, None, tr, n), lambda q, i, c_ref: (q, c_ref[0], i, 0)),
                  pl.BlockSpec((None, tr, n), lambda q, i, c_ref: (q, i, 0))],
        out_specs=pl.BlockSpec((None, tr, n), lambda q, i, c_ref: (q, i, 0)))
    out = pl.pallas_call(
        body, grid_spec=grid_spec, out_shape=SDS((N_CHIP, rb, n), BF16), name=name,
        compiler_params=pltpu.CompilerParams(dimension_semantics=("arbitrary",) * 2, vmem_limit_bytes=VMEM_LIMIT))(
            core, g.reshape(N_CHIP, 2, rb, n), from_sibling.reshape(N_CHIP, rb, n))
    return out.reshape(N_CHIP * rb, n)


SEM = pl.BlockSpec(memory_space=pltpu.SEMAPHORE)
IN_HBM = pl.BlockSpec(memory_space=pltpu.HBM)
SIDE_EFFECT = pltpu.SideEffectType.DATAFLOW_SIDE_EFFECTING


def _own_slot(partials, name):
    rb, n = partials.shape[0] // N_CHIP, partials.shape[1]
    tr = rb // 2
    chip = (2 * lax.axis_index("x") + lax.axis_index("y")).astype(jnp.int32).reshape(1)

    def body(chip_ref, src_ref, dst_ref):
        dst_ref[...] = src_ref[...]

    block = pl.BlockSpec((None, tr, n), lambda i, chip_ref: (chip_ref[0], i, 0))
    grid_spec = pltpu.PrefetchScalarGridSpec(num_scalar_prefetch=1, grid=(rb // tr,), in_specs=[block], out_specs=block)
    out = pl.pallas_call(
        body, grid_spec=grid_spec, out_shape=SDS((N_CHIP, rb, n), partials.dtype), name=name,
        compiler_params=pltpu.CompilerParams(dimension_semantics=("arbitrary",), vmem_limit_bytes=VMEM_LIMIT))(
            chip, partials.reshape(N_CHIP, rb, n))
    return out.reshape(partials.shape)


def _blank_like(src, rows, name):
    return pl.pallas_call(lambda src_ref, out_ref: None, out_shape=SDS((rows, src.shape[1]), src.dtype),
                          in_specs=[ANY], out_specs=ANY, name=name)(src)


def _chip_copies(src_ref, land_ref, sems):
    x, y, c = _position()
    rb = src_ref.shape[0] // N_CHIP
    copies = []
    for k, (fx, fy) in enumerate(CHIP_FLIPS):
        px, py = (1 - x if fx else x), (1 - y if fy else y)
        copies.append(pltpu.make_async_remote_copy(
            src_ref=src_ref.at[pl.ds((2 * px + py) * rb, rb), :], dst_ref=land_ref.at[pl.ds((2 * x + y) * rb, rb), :],
            send_sem=sems[2 * k], recv_sem=sems[2 * k + 1], device_id=(px, py, c), device_id_type=MESH))
    return copies


def _sibling_copies(src_ref, land_ref, sems):
    x, y, c = _position()
    rb = src_ref.shape[0] // N_DEV
    return [pltpu.make_async_remote_copy(
        src_ref=src_ref.at[pl.ds((2 * q + 1 - c) * rb, rb), :], dst_ref=land_ref.at[pl.ds(q * rb, rb), :],
        send_sem=sems[2 * q], recv_sem=sems[2 * q + 1], device_id=(x, y, 1 - c), device_id_type=MESH)
        for q in range(N_CHIP)]


SPLIT_COPIES = {"to_chips": (_chip_copies, 3), "to_sibling": (_sibling_copies, N_CHIP)}


def _split_start(kind, src, land, name):
    copies_of, n_copies = SPLIT_COPIES[kind]

    def body(src_ref, land_ref, *rest):
        sems, token = rest[:2 * n_copies], rest[-1]
        for copy in copies_of(src_ref, land_ref, sems):
            copy.start()
        token[...] = jnp.zeros_like(token)

    res = pl.pallas_call(
        body, name=name,
        out_shape=(pltpu.SemaphoreType.DMA(()),) * (2 * n_copies)
        + (pltpu.HBM(src.shape, src.dtype), pltpu.HBM(land.shape, land.dtype), SDS((SUBLANES, LANES), F32)),
        in_specs=(IN_HBM, IN_HBM), out_specs=(SEM,) * (2 * n_copies) + (IN_HBM, IN_HBM, WHOLE),
        input_output_aliases={0: 2 * n_copies, 1: 2 * n_copies + 1},
        compiler_params=pltpu.CompilerParams(has_side_effects=SIDE_EFFECT))(
            pltpu.with_memory_space_constraint(src, pltpu.HBM), pltpu.with_memory_space_constraint(land, pltpu.HBM))
    return (kind, res[:2 * n_copies], res[-3], res[-2]), res[-1]


def _split_wait(pending, after, name):
    kind, sems, src, land = pending
    copies_of, n_copies = SPLIT_COPIES[kind]

    def body(src_ref, land_ref, *rest):
        for copy in copies_of(src_ref, land_ref, rest[:2 * n_copies]):
            copy.wait_send()
            copy.wait_recv()

    return pl.pallas_call(
        body, name=name, out_shape=(pltpu.HBM(src.shape, src.dtype), pltpu.HBM(land.shape, land.dtype)),
        in_specs=(IN_HBM, IN_HBM) + (SEM,) * (2 * n_copies) + (ANY,) * len(after), out_specs=(IN_HBM, IN_HBM),
        input_output_aliases={0: 0, 1: 1},
        compiler_params=pltpu.CompilerParams(has_side_effects=SIDE_EFFECT))(src, land, *sems, *after)


def _sum_devices(gathered, name):
    r = gathered.shape[0] // N_DEV

    def body(g_ref, o_ref):
        acc = g_ref[0]
        for s in range(1, N_DEV):
            acc = acc + g_ref[s]
        o_ref[...] = acc

    return _call(body, name=name, args=[gathered.reshape(N_DEV, r, LANES)], out_shape=SDS((r, LANES), F32),
                 in_specs=[WHOLE], out_specs=WHOLE)


def _cast_into_place(w, transposed, name):
    me = _flat(*_position()).astype(jnp.int32).reshape(1)
    if transposed:
        d, rb = w.shape
        td = 512
        grid = (d // td,)
        in_spec = pl.BlockSpec((td, rb), lambda i, me_ref: (i, 0))
        out_spec = pl.BlockSpec((rb, td), lambda i, me_ref: (me_ref[0], i))
    else:
        rb, d = w.shape
        grid = (1,)
        in_spec = pl.BlockSpec((rb, d), lambda i, me_ref: (0, 0))
        out_spec = pl.BlockSpec((rb, d), lambda i, me_ref: (me_ref[0], 0))

    def body(me_ref, w_ref, o_ref):
        value = w_ref[...]
        o_ref[...] = (value.T if transposed else value).astype(BF16)

    grid_spec = pltpu.PrefetchScalarGridSpec(num_scalar_prefetch=1, grid=grid, in_specs=[in_spec], out_specs=out_spec)
    return pl.pallas_call(
        body, grid_spec=grid_spec, out_shape=SDS((N_DEV * rb, d), BF16), name=name,
        compiler_params=pltpu.CompilerParams(dimension_semantics=("arbitrary",), vmem_limit_bytes=VMEM_LIMIT))(me, w)


ROW_TILE = 256


def _rmsnorm_fwd(h, gain, name):
    t, d = h.shape

    def body(h_ref, g_ref, u_ref):
        x = h_ref[...]
        u_ref[...] = (x * lax.rsqrt(jnp.mean(x * x, axis=-1, keepdims=True) + NORM_EPS) * g_ref[...]).astype(BF16)

    row = pl.BlockSpec((ROW_TILE, d), lambda i: (i, 0))
    return _call(body, name=name, args=[h, gain], out_shape=SDS((t, d), BF16), grid=(t // ROW_TILE,),
                 in_specs=[row, pl.BlockSpec((1, d), lambda i: (0, 0))], out_specs=row)


def _rms_bwd_math(x, gain, dy):
    rstd = lax.rsqrt(jnp.mean(x * x, axis=-1, keepdims=True) + NORM_EPS)
    xhat = x * rstd
    dxh = dy * gain
    dx = rstd * (dxh - xhat * jnp.mean(dxh * xhat, axis=-1, keepdims=True))
    return dx, jnp.sum(dy * xhat, axis=0, keepdims=True)


def _rmsnorm_bwd(du, h, gain, resid, bf_scale, name, job=None):
    t, d = h.shape

    def body(du_ref, h_ref, g_ref, r_ref, dh_ref, dhb_ref, dg_ref):
        @pl.when(pl.program_id(0) == 0)
        def _():
            dg_ref[...] = jnp.zeros_like(dg_ref)

        dx, dg = _rms_bwd_math(h_ref[...], g_ref[...], du_ref[...])
        dh = r_ref[...] + dx
        dh_ref[...] = dh
        dhb_ref[...] = (bf_scale * dh).astype(BF16)
        dg_ref[...] += dg

    row = pl.BlockSpec((ROW_TILE, d), lambda i: (i, 0))
    vec = pl.BlockSpec((1, d), lambda i: (0, 0))
    return _call(body, name=name, args=[du, h, gain, resid],
                 out_shape=(SDS((t, d), F32), SDS((t, d), BF16), SDS((1, d), F32)), grid=(t // ROW_TILE,),
                 in_specs=[row, row, vec, row], out_specs=(row, row, vec), job=job)


def _final_loss(h, gain, target, name):
    t, d = h.shape

    def body(h_ref, g_ref, t_ref, dh_ref, dhb_ref, loss_ref, dg_ref):
        @pl.when(pl.program_id(0) == 0)
        def _():
            dg_ref[...] = jnp.zeros_like(dg_ref)
            loss_ref[...] = jnp.zeros_like(loss_ref)

        x = h_ref[...]
        gain = g_ref[...]
        out = x * lax.rsqrt(jnp.mean(x * x, axis=-1, keepdims=True) + NORM_EPS) * gain
        err = out - t_ref[...]
        loss_ref[...] += 0.5 * jnp.sum(jnp.mean(err * err, axis=-1, keepdims=True), axis=0, keepdims=True)
        dx, dg = _rms_bwd_math(x, gain, err * (1.0 / d))
        dh_ref[...] = dx
        dhb_ref[...] = (0.5 * dx).astype(BF16)
        dg_ref[...] += dg

    row = pl.BlockSpec((ROW_TILE, d), lambda i: (i, 0))
    vec = pl.BlockSpec((1, d), lambda i: (0, 0))
    one = pl.BlockSpec((SUBLANES, LANES), lambda i: (0, 0))
    return _call(body, name=name, args=[h, gain, target],
                 out_shape=(SDS((t, d), F32), SDS((t, d), BF16), SDS((SUBLANES, LANES), F32), SDS((1, d), F32)),
                 grid=(t // ROW_TILE,), in_specs=[row, vec, row], out_specs=(row, row, one, vec))


def _mixnorm_fwd(ya, yb, ga, gb, name):
    t, c = ya.shape

    def body(ya_ref, yb_ref, ga_ref, gb_ref, y_ref, yt_ref):
        for k, (src, g_ref) in enumerate(((ya_ref, ga_ref), (yb_ref, gb_ref))):
            x = src[...]
            u = x * lax.rsqrt(jnp.mean(x * x, axis=-1, keepdims=True) + NORM_EPS) * g_ref[...]
            y_ref[:, k * c:(k + 1) * c] = u.astype(BF16)
            yt_ref[k * c:(k + 1) * c, :] = u.T.astype(BF16)

    row = pl.BlockSpec((ROW_TILE, c), lambda i: (i, 0))
    vec = pl.BlockSpec((1, c), lambda i: (0, 0))
    return _call(body, name=name, args=[ya, yb, ga, gb],
                 out_shape=(SDS((t, 2 * c), BF16), SDS((2 * c, t), BF16)), grid=(t // ROW_TILE,),
                 in_specs=[row, row, vec, vec],
                 out_specs=(pl.BlockSpec((ROW_TILE, 2 * c), lambda i: (i, 0)),
                            pl.BlockSpec((2 * c, ROW_TILE), lambda i: (0, i))))


def _mixnorm_bwd(dy, ya, yb, ga, gb, name):
    t, c = ya.shape

    def body(dy_ref, ya_ref, yb_ref, ga_ref, gb_ref, dya_ref, dyb_ref, dga_ref, dgb_ref):
        @pl.when(pl.program_id(0) == 0)
        def _():
            dga_ref[...] = jnp.zeros_like(dga_ref)
            dgb_ref[...] = jnp.zeros_like(dgb_ref)

        dxa, dga = _rms_bwd_math(ya_ref[...], ga_ref[...], dy_ref[:, :c])
        dxb, dgb = _rms_bwd_math(yb_ref[...], gb_ref[...], dy_ref[:, c:])
        dya_ref[...] = dxa
        dyb_ref[...] = dxb
        dga_ref[...] += dga
        dgb_ref[...] += dgb

    row = pl.BlockSpec((ROW_TILE, c), lambda i: (i, 0))
    vec = pl.BlockSpec((1, c), lambda i: (0, 0))
    return _call(body, name=name, args=[dy, ya, yb, ga, gb],
                 out_shape=(SDS((t, c), F32), SDS((t, c), F32), SDS((1, c), F32), SDS((1, c), F32)),
                 grid=(t // ROW_TILE,),
                 in_specs=[pl.BlockSpec((ROW_TILE, 2 * c), lambda i: (i, 0)), row, row, vec, vec],
                 out_specs=(row, row, vec, vec))


def _tile(n, want):
    return max(t for t in range(LANES, min(n, want) + 1, LANES) if n % t == 0)


def _mm(a, b, *, nt, out_dtype, tm, tn, name, residual=None, scale=None, take=None, out_rows=None, row_offset=0,
        into=None, job=None):
    parts = list(a) if isinstance(a, (list, tuple)) else [a]
    widths = [p.shape[-1] for p in parts]
    k = sum(widths)
    n = b.shape[0] if nt else b.shape[1]
    if take is None:
        m, which = parts[0].shape[0], lambda i: i
        tm = _tile(math.gcd(m, row_offset), tm)
    else:
        tm, tiles, which = take
        m = tm * tiles
    tn = _tile(n, tn)
    out_rows = m if out_rows is None else out_rows

    def body(*refs):
        a_refs, b_ref, rest = refs[:len(parts)], refs[len(parts)], refs[len(parts) + 1:]
        o_ref = rest[-1]
        out, at = None, 0
        for a_ref, width in zip(a_refs, widths):
            av = a_ref[...].astype(BF16)
            if nt:
                term = lax.dot_general(av, b_ref[:, at:at + width].astype(BF16), NT, preferred_element_type=F32)
            else:
                term = jnp.dot(av, b_ref[at:at + width, :].astype(BF16), preferred_element_type=F32)
            out = term if out is None else out + term
            at += width
        if residual is not None:
            out = rest[0][...] + (out if scale is None else scale * out)
        o_ref[...] = out.astype(out_dtype)

    a_specs = [pl.BlockSpec((tm, width), lambda i, j: (which(i), 0)) for width in widths]
    in_specs = a_specs + [pl.BlockSpec((tn, k), lambda i, j: (j, 0)) if nt else pl.BlockSpec((k, tn), lambda i, j: (0, j))]
    args, aliases = parts + [b], {}
    if residual is not None:
        in_specs.append(pl.BlockSpec((tm, tn), lambda i, j: (i, j)))
        args.append(residual)
    if into is not None:
        in_specs.append(ANY)
        aliases[len(args)] = 0
        args.append(into)
    return _call(body, name=name, args=args, out_shape=SDS((out_rows, n), out_dtype), grid=(m // tm, n // tn),
                 in_specs=in_specs, out_specs=pl.BlockSpec((tm, tn), lambda i, j: (row_offset // tm + i, j)),
                 aliases=aliases, job=job)


FFN_HB = 512
HIDDEN_TM = 2048
BWD_TM = 1024


def _ffn_hidden(u, w_in_t, name, job=None):
    t, d = u.shape
    f = w_in_t.shape[0] // 2

    def body(u_ref, w_ref, g_ref, up_ref, hid_ref, hid_t_ref):
        uu = u_ref[...]
        g = lax.dot_general(uu, w_ref[0], NT, preferred_element_type=F32)
        up = lax.dot_general(uu, w_ref[1], NT, preferred_element_type=F32)
        g_ref[...] = g.astype(BF16)
        up_ref[...] = up.astype(BF16)
        hid = (g * _sigmoid(g)) * up
        hid_ref[...] = hid.astype(BF16)
        hid_t_ref[...] = hid.T.astype(BF16)

    tm = min(HIDDEN_TM, t)
    pre = pl.BlockSpec((tm, FFN_HB), lambda i, k: (i, k))
    return _call(body, name=name, args=[u, w_in_t.reshape(2, f, d)],
                 out_shape=(SDS((t, f), BF16), SDS((t, f), BF16), SDS((t, f), BF16), SDS((f, t), BF16)),
                 grid=(t // tm, f // FFN_HB),
                 in_specs=[pl.BlockSpec((tm, d), lambda i, k: (i, 0)),
                           pl.BlockSpec((2, FFN_HB, d), lambda i, k: (0, k, 0))],
                 out_specs=(pre, pre, pre, pl.BlockSpec((FFN_HB, tm), lambda i, k: (k, i))), job=job)


def _ffn_bwd(dfb, gpre, upre, w_in_t, w_out, name, job=None):
    t, d = dfb.shape
    f = w_out.shape[0]
    tm, hb = min(BWD_TM, t), FFN_HB
    nk = f // hb

    def body(df_ref, g_ref, up_ref, w_ref, wo_ref, du_ref, da_t_ref):
        k = pl.program_id(1)
        acc = du_ref

        @pl.when(k == 0)
        def _():
            acc[...] = jnp.zeros_like(acc)

        dhid = lax.dot_general(df_ref[...], wo_ref[...], NT, preferred_element_type=F32)
        g, up = g_ref[...].astype(F32), up_ref[...].astype(F32)
        sig = _sigmoid(g)
        silu = g * sig
        dup = dhid * silu
        dg = dhid * up * (sig * (1.0 + g * (1.0 - sig)))
        da_t_ref[0] = dg.T.astype(BF16)
        da_t_ref[1] = dup.T.astype(BF16)
        acc[...] += (jnp.dot(dg.astype(BF16), w_ref[0], preferred_element_type=F32)
                     + jnp.dot(dup.astype(BF16), w_ref[1], preferred_element_type=F32))

    tok = pl.BlockSpec((tm, d), lambda i, k: (i, 0))
    pre = pl.BlockSpec((tm, hb), lambda i, k: (i, k))
    return _call(body, name=name, args=[dfb, gpre, upre, w_in_t.reshape(2, f, d), w_out],
                 out_shape=(SDS((t, d), F32), SDS((2, f, t), BF16)), grid=(t // tm, nk),
                 in_specs=[tok, pre, pre, pl.BlockSpec((2, hb, d), lambda i, k: (0, k, 0)),
                           pl.BlockSpec((hb, d), lambda i, k: (k, 0))],
                 out_specs=(tok, pl.BlockSpec((2, hb, tm), lambda i, k: (0, k, i))), job=job)


CH = LANES
PAD = SUBLANES


def _lru_gates(xc, gw_ref, gb_ref, lam_ref, z):
    xcb = xc.astype(BF16)
    r = _sigmoid(jnp.dot(xcb, gw_ref[2 * z], preferred_element_type=F32) + gb_ref[pl.ds(2 * z, 1), :])
    i = _sigmoid(jnp.dot(xcb, gw_ref[2 * z + 1], preferred_element_type=F32) + gb_ref[pl.ds(2 * z + 1, 1), :])
    sp = _softplus(-lam_ref[pl.ds(z, 1), :])
    log_a = (-RG_C * r) * sp
    a = jnp.exp(log_a)
    mult = jnp.sqrt(-_expm1(2.0 * log_a))
    return r, i, sp, a, mult


def _conv(xpad, cw_ref, cb_ref, t):
    xc = cb_ref[...] + cw_ref[pl.ds(0, 1), :] * xpad[pl.ds(PAD - 2, t), :]
    for j in range(1, CONV_WIDTH):
        xc = xc + cw_ref[pl.ds(j, 1), :] * xpad[pl.ds(PAD - 2 + j, t), :]
    return xc


def _fill_padded(pad_ref, value, t):
    pad_ref[pl.ds(0, PAD), :] = jnp.zeros((PAD, CH), F32)
    pad_ref[pl.ds(PAD + t, PAD), :] = jnp.zeros((PAD, CH), F32)
    pad_ref[pl.ds(PAD, t), :] = value


def _scan_pair(t, a_up, b_up, out_up, a_down, b_down, out_down):
    row = lax.broadcasted_iota(jnp.int32, (SUBLANES, CH), 0)

    def compose(a, b, rising):
        for dist in (1, 2, 4):
            shift = dist if rising else SUBLANES - dist
            keep = (row >= dist) if rising else (row < SUBLANES - dist)
            b = jnp.where(keep, b + a * pltpu.roll(b, shift, axis=0), b)
            a = jnp.where(keep, a * pltpu.roll(a, shift, axis=0), a)
        return a, b

    def step(tt, carry):
        hu, hd = carry
        lo = pl.ds(pl.multiple_of(tt * SUBLANES, SUBLANES), SUBLANES)
        hi = pl.ds(pl.multiple_of(t - SUBLANES - tt * SUBLANES, SUBLANES), SUBLANES)
        a, b = compose(a_up[lo, :], b_up[lo, :], True)
        up = b + a * hu
        out_up[lo, :] = up
        a, b = compose(a_down[hi, :], b_down[hi, :], False)
        down = b + a * hd
        out_down[hi, :] = down
        return up[SUBLANES - 1:, :], down[:1, :]

    zero = jnp.zeros((1, CH), F32)
    lax.fori_loop(0, t // SUBLANES, step, (zero, zero), unroll=2)


def _lru_fwd(proj, cw, cb, gw, gb, lam, name, job=None):
    t = proj.shape[0]
    c = cw.shape[1]
    ncb = c // CH

    def body(x_ref, g_ref, cw_ref, cb_ref, gw_ref, gb_ref, lam_ref, ya_ref, hf_ref, hb_ref, xpad, a0, b0, a1, b1):
        _fill_padded(xpad, x_ref[...], t)
        xc = _conv(xpad, cw_ref, cb_ref, t)
        for z, (a_s, b_s) in enumerate(((a0, b0), (a1, b1))):
            _, i, _, a, mult = _lru_gates(xc, gw_ref, gb_ref, lam_ref, z)
            a_s[...] = a
            b_s[...] = mult * (i * xc)
        _scan_pair(t, a0, b0, hf_ref, a1, b1, hb_ref)
        gelu, _ = _gelu_parts(g_ref[...])
        ya_ref[...] = gelu * (hf_ref[...] + hb_ref[...])

    col = lambda off: pl.BlockSpec((t, CH), lambda i: (0, off + i))
    small = lambda rows: pl.BlockSpec((rows, CH), lambda i: (0, i))
    return _call(body, name=name, args=[proj, proj, cw, cb, gw, gb, lam], out_shape=(SDS((t, c), F32),) * 3,
                 grid=(ncb,),
                 in_specs=[col(0), col(ncb), small(CONV_WIDTH), small(1),
                           pl.BlockSpec((4, None, CH, CH), lambda i: (0, i, 0, 0)), small(4), small(2)],
                 out_specs=(col(0),) * 3,
                 scratch_shapes=[pltpu.VMEM((t + 2 * PAD, CH), F32)] + [pltpu.VMEM((t, CH), F32)] * 4, job=job)


def _lru_bwd(proj, cw, cb, gw, gb, lam, hf, hb, dya, name, job=None):
    t = proj.shape[0]
    c = cw.shape[1]
    ncb = c // CH

    def body(x_ref, g_ref, cw_ref, cb_ref, gw_ref, gb_ref, lam_ref, hf_ref, hb_ref, dya_ref,
             dx_ref, dg_ref, dt_ref, dcw_ref, dcb_ref, dgw_ref, dgb_ref, dlam_ref,
             xpad, hpad, dxc, a0, a1, dhs, dh0, dh1):
        _fill_padded(xpad, x_ref[...], t)
        xc = _conv(xpad, cw_ref, cb_ref, t)
        xcb = xc.astype(BF16)
        gates = [_lru_gates(xc, gw_ref, gb_ref, lam_ref, z) for z in range(2)]

        gelu, dgelu = _gelu_parts(g_ref[...])
        dya = dya_ref[...]
        dgate = dya * (hf_ref[...] + hb_ref[...]) * dgelu
        dg_ref[...] = dgate.astype(BF16)
        dt_ref[1] = dgate.T.astype(BF16)
        dhs[...] = dya * gelu

        _fill_padded(hpad, gates[0][3], t)
        a0[...] = hpad[pl.ds(PAD + 1, t), :]
        _fill_padded(hpad, gates[1][3], t)
        a1[...] = hpad[pl.ds(PAD - 1, t), :]
        _scan_pair(t, a1, dhs, dh1, a0, dhs, dh0)

        acc_dxc = jnp.zeros((t, CH), F32)
        for z, (h_ref, dh_ref, shift) in enumerate(((hf_ref, dh0, -1), (hb_ref, dh1, 1))):
            r, i, sp, a, mult = gates[z]
            _fill_padded(hpad, h_ref[...], t)
            h_nb = hpad[pl.ds(PAD + shift, t), :]
            db = dh_ref[...]
            da = db * h_nb
            d_i = db * mult * xc
            acc_dxc = acc_dxc + db * mult * i
            d_mult = db * i * xc
            d_la = da * a - d_mult * (a * a) / mult
            d_r = d_la * (-RG_C * sp)
            dlam_ref[pl.ds(z, 1), :] = (jnp.sum(d_la * (-RG_C * r), axis=0, keepdims=True)
                                        * (-_sigmoid(-lam_ref[pl.ds(z, 1), :])))
            for gate, d_pre in ((0, d_r * r * (1.0 - r)), (1, d_i * i * (1.0 - i))):
                zg = 2 * z + gate
                dgb_ref[pl.ds(zg, 1), :] = jnp.sum(d_pre, axis=0, keepdims=True)
                d_pre_b = d_pre.astype(BF16)
                dgw_ref[zg] = lax.dot_general(xcb, d_pre_b, TN, preferred_element_type=F32)
                acc_dxc = acc_dxc + lax.dot_general(d_pre_b, gw_ref[zg], NT, preferred_element_type=F32)

        dcb_ref[...] = jnp.sum(acc_dxc, axis=0, keepdims=True)
        for j in range(CONV_WIDTH):
            dcw_ref[pl.ds(j, 1), :] = jnp.sum(acc_dxc * xpad[pl.ds(PAD - 2 + j, t), :], axis=0, keepdims=True)
        _fill_padded(dxc, acc_dxc, t)
        dx = cw_ref[pl.ds(0, 1), :] * dxc[pl.ds(PAD + 2, t), :]
        for j in range(1, CONV_WIDTH):
            dx = dx + cw_ref[pl.ds(j, 1), :] * dxc[pl.ds(PAD + 2 - j, t), :]
        dx_ref[...] = dx.astype(BF16)
        dt_ref[0] = dx.T.astype(BF16)

    col = lambda off: pl.BlockSpec((t, CH), lambda i: (0, off + i))
    small = lambda rows: pl.BlockSpec((rows, CH), lambda i: (0, i))
    dense = pl.BlockSpec((4, None, CH, CH), lambda i: (0, i, 0, 0))
    padded = pltpu.VMEM((t + 2 * PAD, CH), F32)
    return _call(
        body, name=name, args=[proj, proj, cw, cb, gw, gb, lam, hf, hb, dya],
        out_shape=(SDS((t, c), BF16), SDS((t, c), BF16), SDS((2, c, t), BF16), SDS((CONV_WIDTH, c), F32),
                   SDS((1, c), F32), SDS((4, ncb, CH, CH), F32), SDS((4, c), F32), SDS((2, c), F32)),
        grid=(ncb,),
        in_specs=[col(0), col(ncb), small(CONV_WIDTH), small(1), dense, small(4), small(2), col(0), col(0), col(0)],
        out_specs=(col(0), col(0), pl.BlockSpec((2, CH, t), lambda i: (0, i, 0)), small(CONV_WIDTH), small(1),
                   dense, small(4), small(2)),
        scratch_shapes=[padded, padded, padded] + [pltpu.VMEM((t, CH), F32)] * 5, job=job)


Q_ROWS = 4
BAND_ROWS = WIN_ROWS + Q_ROWS
BAND_PAIRS = BAND_ROWS // 2
Q_BLOCK = Q_ROWS * GRID_W
BAND = BAND_ROWS * GRID_W
PAIR_W = 2 * GRID_W
N_BOTH = 2 * WIN_ROWS - 2
ENTRY_LEFT_OUT, ENTRY_RIGHT_OUT, ENTRY_OUT = N_BOTH, N_BOTH + 1, N_BOTH + 2
N_ENTRIES = N_BOTH + 3


def _bias_tables(rpb):
    cols = np.arange(GRID_W)
    start = np.clip(cols - WIN_COLS // 2, 0, GRID_W - WIN_COLS)
    valid = (cols[None, :] >= start[:, None]) & (cols[None, :] < start[:, None] + WIN_COLS)
    col_off = np.clip(cols[None, :] - cols[:, None] + WIN_COLS - 1, 0, 2 * WIN_COLS - 2)
    pick_col = jnp.asarray(np.eye(2 * WIN_COLS - 1, dtype=np.float32)[col_off] * valid[..., None])
    by_row = jnp.einsum("hrc,qkc->hrqk", rpb, pick_col, precision=lax.Precision.HIGHEST)
    by_row = jnp.where(jnp.asarray(valid)[None, None], by_row, NEG)
    out = jnp.full_like(by_row[:, :1], NEG)
    first_in, last_in = WIN_ROWS - 1 - WIN_ROWS // 2, 2 * (WIN_ROWS - 1) - WIN_ROWS // 2
    both = jnp.concatenate([by_row[:, :-1], by_row[:, 1:]], axis=-1)
    left_out = jnp.concatenate([out, by_row[:, first_in:first_in + 1]], axis=-1)
    right_out = jnp.concatenate([by_row[:, last_in:last_in + 1], out], axis=-1)
    return jnp.concatenate([both, left_out, right_out, jnp.concatenate([out, out], axis=-1)], axis=1)


def _band_start(m, rows):
    return jnp.clip(Q_ROWS * m - WIN_ROWS // 2, 0, rows - BAND_ROWS)


def _entry(r, key_row, rows):
    w0 = jnp.clip(r - WIN_ROWS // 2, 0, rows - WIN_ROWS)
    left = (key_row >= w0) & (key_row < w0 + WIN_ROWS)
    right = (key_row + 1 >= w0) & (key_row + 1 < w0 + WIN_ROWS)
    return jnp.where(left & right, key_row - r + WIN_ROWS - 1,
                     jnp.where(right, ENTRY_LEFT_OUT, jnp.where(left, ENTRY_RIGHT_OUT, ENTRY_OUT)))


def _transposed_pairs(dst, src_ref):
    for g in range(dst.shape[0]):
        dst[g] = src_ref[pl.ds(g * PAIR_W, PAIR_W), :].T.astype(BF16)


def _band_of(pairs_ref, first_pair, hh):
    heads = pl.ds(hh * HEAD_DIM, HEAD_DIM)
    return jnp.concatenate([pairs_ref[first_pair + g, heads, :] for g in range(BAND_PAIRS)], axis=1)


def _attn_block(qs, kt, tz_ref, hh, m, rows):
    rs = _band_start(m, rows)
    lanes = pl.ds(hh * HEAD_DIM, HEAD_DIM)
    qrows = pl.ds(pl.multiple_of(m * Q_BLOCK, Q_BLOCK), Q_BLOCK)
    band = pl.ds(pl.multiple_of(rs * GRID_W, PAIR_W), BAND)
    entries = [[_entry(Q_ROWS * m + i, rs + 2 * g, rows) for g in range(BAND_PAIRS)] for i in range(Q_ROWS)]
    bias = jnp.concatenate([jnp.concatenate([tz_ref[hh, e] for e in row], axis=1) for row in entries], axis=0)
    q = qs[qrows, lanes]
    s = jnp.dot(q, _band_of(kt, rs // 2, hh), preferred_element_type=F32) * (HEAD_DIM ** -0.5) + bias
    p = jnp.exp(s - jnp.max(s, axis=-1, keepdims=True))
    p = p / jnp.sum(p, axis=-1, keepdims=True)
    return q, p, qrows, band, lanes, entries, rs // 2


def _attn_fwd(proj, tables, width, name, job=None):
    t = proj.shape[0]
    rows = t // GRID_W
    npair = width // LANES
    first = (proj.shape[1] - 3 * width) // LANES

    def body(q_ref, k_ref, v_ref, tz_ref, o_ref, qs, vs, kt):
        qs[...] = q_ref[...].astype(BF16)
        vs[...] = v_ref[...].astype(BF16)
        _transposed_pairs(kt, k_ref)

        def block(m, carry):
            for hh in range(2):
                _, p, qrows, band, lanes, _, _ = _attn_block(qs, kt, tz_ref, hh, m, rows)
                o_ref[qrows, lanes] = jnp.dot(p.astype(BF16), vs[band, lanes], preferred_element_type=F32)
            return carry

        lax.fori_loop(0, rows // Q_ROWS, block, 0, unroll=2)

    col = lambda off: pl.BlockSpec((t, LANES), lambda i: (0, off + i))
    return _call(body, name=name, args=[proj, proj, proj, tables], out_shape=SDS((t, width), F32), grid=(npair,),
                 in_specs=[col(first), col(first + npair), col(first + 2 * npair),
                           pl.BlockSpec((2, N_ENTRIES, GRID_W, PAIR_W), lambda i: (i, 0, 0, 0))],
                 out_specs=col(0),
                 scratch_shapes=[pltpu.VMEM((t, LANES), BF16)] * 2 + [pltpu.VMEM((t // PAIR_W, LANES, PAIR_W), BF16)],
                 job=job)


def _attn_bwd(proj, tables, dyb, name, job=None):
    t, width = dyb.shape
    rows = t // GRID_W
    npair = width // LANES
    first = (proj.shape[1] - 3 * width) // LANES

    def body(q_ref, k_ref, v_ref, tz_ref, do_ref, dq_ref, dk_ref, dv_ref, dt_ref, dtz_ref, dq_s, dk_s, dv_s,
             qs, ks, vs, dos, kt, vt):
        qs[...] = q_ref[...].astype(BF16)
        ks[...] = k_ref[...].astype(BF16)
        vs[...] = v_ref[...].astype(BF16)
        dos[...] = do_ref[...].astype(BF16)
        _transposed_pairs(kt, k_ref)
        _transposed_pairs(vt, v_ref)
        dk_s[...] = jnp.zeros_like(dk_s)
        dv_s[...] = jnp.zeros_like(dv_s)
        dtz_ref[...] = jnp.zeros_like(dtz_ref)

        def block(m, carry):
            for hh in range(2):
                q, p, qrows, band, lanes, entries, first_pair = _attn_block(qs, kt, tz_ref, hh, m, rows)
                do = dos[qrows, lanes]
                dp = jnp.dot(do, _band_of(vt, first_pair, hh), preferred_element_type=F32)
                ds = p * (dp - jnp.sum(dp * p, axis=-1, keepdims=True))
                for i, row in enumerate(entries):
                    for g, e in enumerate(row):
                        dtz_ref[hh, e] += ds[i * GRID_W:(i + 1) * GRID_W, g * PAIR_W:(g + 1) * PAIR_W]
                dsb = (ds * (HEAD_DIM ** -0.5)).astype(BF16)
                dq_s[qrows, lanes] = jnp.dot(dsb, ks[band, lanes], preferred_element_type=F32)
                dk_s[band, lanes] += lax.dot_general(dsb, q, TN, preferred_element_type=F32)
                dv_s[band, lanes] += lax.dot_general(p.astype(BF16), do, TN, preferred_element_type=F32)
            return carry

        lax.fori_loop(0, rows // Q_ROWS, block, 0)
        for n, (src, dst) in enumerate(((dq_s, dq_ref), (dk_s, dk_ref), (dv_s, dv_ref))):
            val = src[...]
            dst[...] = val.astype(BF16)
            dt_ref[n] = val.T.astype(BF16)

    col = lambda off: pl.BlockSpec((t, LANES), lambda i: (0, off + i))
    table = pl.BlockSpec((2, N_ENTRIES, GRID_W, PAIR_W), lambda i: (i, 0, 0, 0))
    pairs = pltpu.VMEM((t // PAIR_W, LANES, PAIR_W), BF16)
    return _call(body, name=name, args=[proj, proj, proj, tables, dyb],
                 out_shape=(SDS((t, width), BF16),) * 3 + (SDS((3, width, t), BF16), SDS(tables.shape, F32)),
                 grid=(npair,),
                 in_specs=[col(first), col(first + npair), col(first + 2 * npair), table, col(0)],
                 out_specs=(col(0), col(0), col(0), pl.BlockSpec((3, LANES, t), lambda i: (0, i, 0)), table),
                 scratch_shapes=[pltpu.VMEM((t, LANES), F32)] * 3 + [pltpu.VMEM((t, LANES), BF16)] * 4 + [pairs, pairs],
                 job=job)


def _adamw_math(w, g, m, v):
    m = ADAM_B1 * m + (1.0 - ADAM_B1) * g
    v = ADAM_B2 * v + (1.0 - ADAM_B2) * (g * g)
    m_hat = m / (1.0 - ADAM_B1 ** ADAM_STEP)
    v_hat = v / (1.0 - ADAM_B2 ** ADAM_STEP)
    delta = -ADAM_LR * (m_hat / (jnp.sqrt(v_hat) + ADAM_EPS) + ADAM_WD * w)
    return delta, m, v


def _sum_partials(p_ref):
    g = p_ref[0].astype(F32)
    for s in range(1, N_CHIP):
        g = g + p_ref[s].astype(F32)
    return g


def _adamw_rows(w, partials, m, v, name):
    rb, n = w.shape
    tr = 64

    def body(w_ref, p_ref, m_ref, v_ref, g_ref, d_ref, nm_ref, nv_ref):
        g = _sum_partials(p_ref)
        g_ref[...] = g
        d_ref[...], nm_ref[...], nv_ref[...] = _adamw_math(w_ref[...], g, m_ref[...], v_ref[...])

    blk = pl.BlockSpec((tr, n), lambda i: (i, 0))
    return _call(body, name=name, args=[w, partials.reshape(N_CHIP, rb, n), m, v],
                 out_shape=(SDS((rb, n), F32),) * 4, grid=(rb // tr,),
                 in_specs=[blk, pl.BlockSpec((N_CHIP, tr, n), lambda i: (0, i, 0)), blk, blk], out_specs=(blk,) * 4)


def _adamw_cols(w, partials, m, v, name):
    d, nb = w.shape
    td = 256
    parts = list(partials) if isinstance(partials, (list, tuple)) else [partials]
    heights = [p.shape[0] // N_CHIP for p in parts]

    def body(w_ref, m_ref, v_ref, *rest):
        p_refs, (g_ref, d_ref, nm_ref, nv_ref) = rest[:len(parts)], rest[len(parts):]
        g = jnp.concatenate([_sum_partials(p_ref) for p_ref in p_refs], axis=0).T
        g_ref[...] = g
        d_ref[...], nm_ref[...], nv_ref[...] = _adamw_math(w_ref[...], g, m_ref[...], v_ref[...])

    blk = pl.BlockSpec((td, nb), lambda i: (i, 0))
    return _call(body, name=name, args=[w, m, v, *[p.reshape(N_CHIP, h, d) for p, h in zip(parts, heights)]],
                 out_shape=(SDS((d, nb), F32),) * 4, grid=(d // td,),
                 in_specs=[blk, blk, blk] + [pl.BlockSpec((N_CHIP, h, td), lambda i: (0, 0, i)) for h in heights],
                 out_specs=(blk,) * 4)


def _adamw_small(w, g, m, v, name):
    def body(w_ref, g_ref, m_ref, v_ref, d_ref, nm_ref, nv_ref):
        d_ref[...], nm_ref[...], nv_ref[...] = _adamw_math(w_ref[...], g_ref[...], m_ref[...], v_ref[...])

    return _call(body, name=name, args=[w, g, m, v], out_shape=(SDS(w.shape, F32),) * 3, in_specs=[WHOLE] * 4,
                 out_specs=(WHOLE,) * 3)


TILE = SUBLANES * LANES


def _pack(arrays):
    parts = []
    for a in arrays:
        flat = a.reshape(-1).astype(F32)
        flat = jnp.pad(flat, (0, -flat.size % TILE))
        parts.append(flat.reshape(-1, LANES))
    return jnp.concatenate(parts, axis=0)


def _unpack(pack, like):
    out, row = [], 0
    for a in like:
        n = int(np.prod(a.shape))
        nrows = -(-n // TILE) * SUBLANES
        out.append(pack[row:row + nrows].reshape(-1)[:n].reshape(a.shape))
        row += nrows
    return out


def _dense_gate_blocks(gate_w):
    w = gate_w.reshape(4, -1, 2, HEAD_DIM, HEAD_DIM)
    zero = jnp.zeros_like(w[:, :, 0])
    top = jnp.concatenate([w[:, :, 0], zero], axis=-1)
    bottom = jnp.concatenate([zero, w[:, :, 1]], axis=-1)
    return jnp.concatenate([top, bottom], axis=-2)


def _diag_gate_blocks(dense, shape):
    even = dense[:, :, :HEAD_DIM, :HEAD_DIM]
    odd = dense[:, :, HEAD_DIM:, HEAD_DIM:]
    return jnp.stack([even, odd], axis=2).reshape(shape)


LARGE = ("ffn1_w_in", "ffn1_w_out", "w_in_mix", "w_out_mix", "ffn2_w_in", "ffn2_w_out")
COLUMN_SHARDED = ("ffn1_w_in", "w_in_mix", "ffn2_w_in")
SHARDED_SMALL = ("lru_conv_w", "lru_lambda")
REPLICATED = ("norm_ffn1", "norm_mix", "lru_conv_b", "lru_gate_w", "lru_gate_b", "attn_rpb", "lru_out_norm",
              "attn_out_norm", "norm_ffn2", "norm_final")
SMALL_ORDER = REPLICATED + SHARDED_SMALL
WEIGHTS = ("norm_ffn1", "ffn1_w_in", "ffn1_w_out", "norm_mix", "w_in_mix", "lru_conv_w", "lru_conv_b", "lru_gate_w",
           "lru_gate_b", "lru_lambda", "attn_rpb", "lru_out_norm", "attn_out_norm", "w_out_mix", "norm_ffn2",
           "ffn2_w_in", "ffn2_w_out", "norm_final")


PARTS = {("gather", "w_in_mix"): 4, ("gather", "ffn2_w_in"): 8}
CARRIES = {
    "gather_ffn1_in": [(("gather", "ffn1_w_in"), 1), (("gather", "small"), 1)],
    "ffn1_hidden": [(("gather", "ffn1_w_out"), 1), (("gather", "w_in_mix"), 1)],
    "ffn1_out": [(("gather", "w_in_mix"), 3)],
    "mix_in_proj": [(("gather", "w_out_mix"), 1), (("gather", "ffn2_w_in"), 1)],
    "lru_fwd": [(("gather", "ffn2_w_in"), 3)],
    "attn_fwd": [(("gather", "ffn2_w_in"), 3)],
    "mix_out_proj": [(("gather", "ffn2_w_in"), 1)],
    "ffn2_hidden": [(("gather", "ffn2_w_out"), 1)],
    "ffn1_bwd": [(("gather", "small_grads"), 1)],
    "gather_late_grads": [(("gather", "late_grads"), 1)],
}


class _Transfer:
    def __init__(self, kind, src, dest, block_rows, parts):
        self.kind, self.src, self.dest = kind, src, dest
        self.ranges, self.taken = _split(block_rows, parts), 0

    def take(self, count):
        lo, hi = self.ranges[self.taken][0], self.ranges[self.taken + count - 1][1]
        self.taken += count
        return _Piece(self.kind, self.src, self.dest, lo, hi)


class _Traffic:
    def __init__(self):
        self.transfers = {}

    def open(self, kind, name, src, placed=None):
        dest = _gathered(src) if placed is None else placed
        self.transfers[kind, name] = _Transfer(kind, src, dest, dest.shape[0] // N_DEV, PARTS.get((kind, name), 1))

    def _job(self, host):
        moved = [self.transfers[key] for key, _ in CARRIES[host]]
        return moved, _Job([tr.take(count) for tr, (_, count) in zip(moved, CARRIES[host])])

    def carry(self, host, fn, *args, **kw):
        if host not in CARRIES:
            return fn(*args, name=host, **kw)
        moved, job = self._job(host)
        res, landed = fn(*args, name=host, job=job, **kw)
        for tr, arr in zip(moved, landed):
            tr.dest = arr
        return res

    def alone(self, host):
        moved, job = self._job(host)
        for tr, arr in zip(moved, _run_job(job, host)):
            tr.dest = arr

    def result(self, kind, name):
        tr = self.transfers.pop((kind, name))
        assert tr.taken == len(tr.ranges), (kind, name)
        return tr.dest


def _forward_backward(x, target, shards, sharded_small, s):
    c = s["lru_conv_b"].shape[1]
    width = s["attn_out_norm"].shape[1]
    t = x.shape[0]
    traffic = _Traffic()
    carry = traffic.carry
    weight = lambda n: traffic.result("gather", n)

    for n in LARGE:
        traffic.open("gather", n, None, placed=shards[n])
    traffic.open("gather", "small", sharded_small)
    traffic.alone("gather_ffn1_in")
    full_small = weight("small").reshape(N_DEV, SUBLANES, c // N_DEV)
    conv_w = full_small[:, :CONV_WIDTH].transpose(1, 0, 2).reshape(CONV_WIDTH, c)
    lam = full_small[:, CONV_WIDTH:CONV_WIDTH + 2].transpose(1, 0, 2).reshape(2, c)
    w = {"ffn1_w_in": weight("ffn1_w_in")}
    ffn_out = dict(nt=False, out_dtype=F32, tm=1024, tn=512, scale=0.5)
    u1 = _rmsnorm_fwd(x, s["norm_ffn1"], "norm_ffn1")
    g1, up1, hid1, hid1_t = carry("ffn1_hidden", _ffn_hidden, u1, w["ffn1_w_in"])
    w["ffn1_w_out"] = weight("ffn1_w_out")
    h1 = carry("ffn1_out", _mm, hid1, w["ffn1_w_out"], residual=x, **ffn_out)
    w["w_in_mix"] = weight("w_in_mix")
    u2 = _rmsnorm_fwd(h1, s["norm_mix"], "norm_mix")
    proj = carry("mix_in_proj", _mm, u2, w["w_in_mix"], nt=True, out_dtype=F32, tm=1024, tn=512)
    w["w_out_mix"] = weight("w_out_mix")
    gw = _dense_gate_blocks(s["lru_gate_w"]).astype(BF16)
    gb = s["lru_gate_b"].reshape(4, c)
    tables, tables_vjp = jax.vjp(_bias_tables, s["attn_rpb"])
    ya, hf, hb = carry("lru_fwd", _lru_fwd, proj, conv_w, s["lru_conv_b"], gw, gb, lam)
    yb = carry("attn_fwd", _attn_fwd, proj, tables, width)
    y, yt = _mixnorm_fwd(ya, yb, s["lru_out_norm"], s["attn_out_norm"], "mix_norm")
    h2 = carry("mix_out_proj", _mm, y, w["w_out_mix"], nt=False, out_dtype=F32, tm=1024, tn=512, residual=h1)
    u3 = _rmsnorm_fwd(h2, s["norm_ffn2"], "norm_ffn2")
    w["ffn2_w_in"] = weight("ffn2_w_in")
    g2, up2, hid2, hid2_t = carry("ffn2_hidden", _ffn_hidden, u3, w["ffn2_w_in"])
    w["ffn2_w_out"] = weight("ffn2_w_out")
    h3 = carry("ffn2_out", _mm, hid2, w["ffn2_w_out"], residual=h2, **ffn_out)
    dh3, df2, loss_part, d_norm_final = _final_loss(h3, s["norm_final"], target, "final_loss")

    grads = {}
    grad_of = dict(nt=False, out_dtype=BF16, tm=512, tn=2048)

    to_sibling, to_chips = {}, {}

    def reduce_in_chip(n):
        land = _blank_like(grads[n], grads[n].shape[0] // 2, "landing_" + n)
        to_sibling[n], token = _split_start("to_sibling", grads[n], land, "to_sibling_" + n)
        RUN_AFTER.append(token)

    def reduce_over_chips(n, after):
        own, got = _split_wait(to_sibling.pop(n), [after], "from_sibling_" + n)
        summed = _pair_sum(own, got, "pair_sum_" + n)
        to_chips[n], token = _split_start("to_chips", summed, _own_slot(summed, "own_slot_" + n), "to_chips_" + n)
        RUN_AFTER.append(token)
        return token

    f = hid2_t.shape[0]
    grads["ffn2_w_out"] = carry("ffn2_out_grad", _mm, hid2_t, df2, **grad_of)
    reduce_in_chip("ffn2_w_out")
    du3, da2_t = carry("ffn2_bwd", _ffn_bwd, df2, g2, up2, w["ffn2_w_in"], w["ffn2_w_out"])
    reduce_over_chips("ffn2_w_out", du3)
    grads["ffn2_w_in"] = carry("ffn2_in_grad", _mm, da2_t.reshape(2 * f, t), u3, **grad_of)
    reduce_in_chip("ffn2_w_in")
    dh2, dh2b, d_norm_ffn2 = carry("norm_ffn2_bwd", _rmsnorm_bwd, du3, h2, s["norm_ffn2"], dh3, 1.0)
    grads["w_out_mix"] = carry("mix_out_grad", _mm, yt, dh2b, **grad_of)
    reduce_over_chips("ffn2_w_in", grads["w_out_mix"])
    reduce_in_chip("w_out_mix")
    dy = carry("mix_out_bwd", _mm, dh2b, w["w_out_mix"], nt=True, out_dtype=F32, tm=1024, tn=512)
    dya, dyb, d_lru_out_norm, d_attn_out_norm = _mixnorm_bwd(dy, ya, yb, s["lru_out_norm"], s["attn_out_norm"],
                                                             "mix_norm_bwd")
    dq, dk, dv, dqkv_t, d_tables = carry("attn_bwd", _attn_bwd, proj, tables, dyb)
    reduce_over_chips("w_out_mix", dq)
    dx_lru, dg_lru, dxg_t, d_conv_w, d_conv_b, d_gw, d_gb, d_lam = carry(
        "lru_bwd", _lru_bwd, proj, conv_w, s["lru_conv_b"], gw, gb, lam, hf, hb, dya)
    rows_of = 2 * c + 3 * width
    lru_rows = carry("mix_in_grad_lru", _mm, dxg_t.reshape(2 * c, t), u2, out_rows=rows_of, **grad_of)
    grads["w_in_mix"] = carry("mix_in_grad_attn", _mm, dqkv_t.reshape(3 * width, t), u2, out_rows=rows_of,
                              row_offset=2 * c, into=lru_rows, **grad_of)
    reduce_in_chip("w_in_mix")
    du2 = carry("mix_in_bwd", _mm, [dx_lru, dg_lru, dq, dk, dv], w["w_in_mix"], nt=False, out_dtype=F32, tm=1024,
                tn=512)
    dh1, df1, d_norm_mix = carry("norm_mix_bwd", _rmsnorm_bwd, du2, h1, s["norm_mix"], dh2, 0.5)
    reduce_over_chips("w_in_mix", dh1)

    by_device = lambda a: a.reshape(a.shape[0], N_DEV, -1).transpose(1, 0, 2)
    small = {
        "norm_mix": d_norm_mix, "lru_conv_b": d_conv_b, "lru_gate_w": _diag_gate_blocks(d_gw, s["lru_gate_w"].shape),
        "lru_gate_b": d_gb.reshape(s["lru_gate_b"].shape), "attn_rpb": tables_vjp(d_tables)[0],
        "lru_out_norm": d_lru_out_norm, "attn_out_norm": d_attn_out_norm, "norm_ffn2": d_norm_ffn2,
        "norm_final": d_norm_final, "lru_conv_w": by_device(d_conv_w), "lru_lambda": by_device(d_lam),
    }
    early = [small[n] for n in SMALL_ORDER[1:]]
    traffic.open("gather", "small_grads", _pack(early))

    grads["ffn1_w_out"] = carry("ffn1_out_grad", _mm, hid1_t, df1, **grad_of)
    reduce_in_chip("ffn1_w_out")
    du1, da1_t = carry("ffn1_bwd", _ffn_bwd, df1, g1, up1, w["ffn1_w_in"], w["ffn1_w_out"])
    grad_x, _, d_norm_ffn1 = carry("norm_ffn1_bwd", _rmsnorm_bwd, du1, x, s["norm_ffn1"], dh1, 1.0)
    traffic.open("gather", "late_grads", _pack([d_norm_ffn1]))
    traffic.alone("gather_late_grads")
    late = traffic.result("gather", "late_grads")
    reduce_over_chips("ffn1_w_out", late)
    half = 2 * f // N_DEV // 2
    half_rows = lambda h: (half, N_DEV, lambda i: 2 * i + h)
    grads["ffn1_w_in_a"] = carry("ffn1_in_grad_a", _mm, da1_t.reshape(2 * f, t), u1, take=half_rows(0), **grad_of)
    reduce_in_chip("ffn1_w_in_a")
    grads["ffn1_w_in_b"] = carry("ffn1_in_grad_b", _mm, da1_t.reshape(2 * f, t), u1, take=half_rows(1), **grad_of)
    reduce_over_chips("ffn1_w_in_a", grads["ffn1_w_in_b"])
    reduce_in_chip("ffn1_w_in_b")
    reduced = (_unpack(_sum_devices(late, "sum_late_grads"), [d_norm_ffn1])
               + _unpack(_sum_devices(traffic.result("gather", "small_grads"), "sum_small_grads"), early))
    last_token = reduce_over_chips("ffn1_w_in_b", reduced[1])
    RUN_AFTER.clear()
    assert not traffic.transfers and not to_sibling, (list(traffic.transfers), list(to_sibling))
    return loss_part[0, 0], grad_x, to_chips, last_token, dict(zip(SMALL_ORDER, reduced))


def _step(x, loss_target, p, m, v):
    me = 4 * lax.axis_index("x") + 2 * lax.axis_index("y") + lax.axis_index("c")

    shards = {n: _cast_into_place(p[n], n in COLUMN_SHARDED, "cast_" + n) for n in LARGE}
    sharded_small = (jnp.pad(p["lru_conv_w"], ((0, SUBLANES - CONV_WIDTH), (0, 0)))
                     + jnp.pad(p["lru_lambda"], ((CONV_WIDTH, SUBLANES - CONV_WIDTH - 2), (0, 0))))
    s = {n: p[n] if n in ("lru_gate_w", "lru_gate_b", "attn_rpb") else p[n].reshape(1, -1) for n in REPLICATED}

    loss_part, grad_x, to_chips, last_token, small = _forward_backward(x, loss_target, shards, sharded_small, s)
    loss = lax.psum(loss_part, ("x", "y", "c"))

    def landed(n, after):
        return _split_wait(to_chips[n], after, "from_chips_" + n)[1]

    def update(n, partials):
        return (_adamw_cols if n in COLUMN_SHARDED else _adamw_rows)(p[n], partials, m[n], v[n], "adamw_" + n)

    out = {n: update(n, landed(n, [last_token])) for n in LARGE if n != "ffn1_w_in"}
    done = [o[3] for o in out.values()]
    out["ffn1_w_in"] = update("ffn1_w_in", [landed("ffn1_w_in_a", done), landed("ffn1_w_in_b", done)])

    g_small = {n: lax.dynamic_index_in_dim(g, me, axis=0, keepdims=False) if n in SHARDED_SMALL else g
               for n, g in small.items()}
    names = SMALL_ORDER
    like = [p[n] for n in names]
    pack_of = lambda d: _pack([d[n].reshape(p[n].shape) for n in names])
    upd = _adamw_small(pack_of(p), pack_of(g_small), pack_of(m), pack_of(v), "adamw_small")
    for n, d_, m_, v_ in zip(names, *[_unpack(u, like) for u in upd]):
        out[n] = (g_small[n].reshape(p[n].shape), d_, m_, v_)
    return loss, grad_x, out


def kernel(x, norm_ffn1, ffn1_w_in, ffn1_w_out, norm_mix, w_in_mix, lru_conv_w, lru_conv_b, lru_gate_w, lru_gate_b, lru_lambda, attn_rpb, lru_out_norm, attn_out_norm, w_out_mix, norm_ffn2, ffn2_w_in, ffn2_w_out, norm_final, loss_target, m_norm_ffn1, m_ffn1_w_in, m_ffn1_w_out, m_norm_mix, m_w_in_mix, m_lru_conv_w, m_lru_conv_b, m_lru_gate_w, m_lru_gate_b, m_lru_lambda, m_attn_rpb, m_lru_out_norm, m_attn_out_norm, m_w_out_mix, m_norm_ffn2, m_ffn2_w_in, m_ffn2_w_out, m_norm_final, v_norm_ffn1, v_ffn1_w_in, v_ffn1_w_out, v_norm_mix, v_w_in_mix, v_lru_conv_w, v_lru_conv_b, v_lru_gate_w, v_lru_gate_b, v_lru_lambda, v_attn_rpb, v_lru_out_norm, v_attn_out_norm, v_w_out_mix, v_norm_ffn2, v_ffn2_w_in, v_ffn2_w_out, v_norm_final):
    given = dict(locals())
    drop_layer = lambda n, a: a if n == "norm_final" else a[0]
    p = {n: drop_layer(n, given[n]) for n in WEIGHTS}
    m = {n: drop_layer(n, given["m_" + n]) for n in WEIGHTS}
    v = {n: drop_layer(n, given["v_" + n]) for n in WEIGHTS}
    loss, grad_x, out = _step(x[0], loss_target[0], p, m, v)
    shaped = lambda n, a: a.reshape(given[n].shape)
    return (loss, grad_x[None], *[shaped(n, out[n][k]) for k in range(4) for n in WEIGHTS])
```
